```python
import jax, jax.numpy as jnp
from jax import lax
import numpy as np

D_MODEL = 1024
BATCH = 8
SEQ = 8192
DEPTH = 1

N_META = 16
EPS = 1e-6
DN_HEADS = 8
DN_DK = 128
DN_DV = 128
DN_CONV = 4
CHUNK = 64
QK_W = DN_HEADS * DN_DK
V_W = DN_HEADS * DN_DV
CF_CH = D_MODEL
CF_KERNEL = 31
IN_SIZES = (QK_W, QK_W, V_W, V_W, DN_HEADS, DN_HEADS, 2 * CF_CH, CF_CH, D_MODEL, D_MODEL)
IN_W = sum(IN_SIZES)

kernel_name = "hybrid_gdn_conformer_gated_merge"


def rmsnorm(x, w):
    xf = x.astype(jnp.float32)
    y = xf * lax.rsqrt(jnp.mean(xf * xf, axis=-1, keepdims=True) + EPS)
    return (y * w.astype(jnp.float32)).astype(x.dtype)


def layernorm(x, w, b):
    xf = x.astype(jnp.float32)
    mu = jnp.mean(xf, axis=-1, keepdims=True)
    var = jnp.mean(jnp.square(xf - mu), axis=-1, keepdims=True)
    y = (xf - mu) * lax.rsqrt(var + EPS)
    return (y * w.astype(jnp.float32) + b.astype(jnp.float32)).astype(x.dtype)


def l2norm(x):
    xf = x.astype(jnp.float32)
    return xf * lax.rsqrt(jnp.sum(xf * xf, axis=-1, keepdims=True) + EPS)


def causal_dwconv(x, w):
    k_w, ch = w.shape
    return lax.conv_general_dilated(
        x, w[:, None, :].astype(x.dtype), window_strides=(1,), padding=[(k_w - 1, 0)],
        dimension_numbers=("NWC", "WIO", "NWC"), feature_group_count=ch)


def chunk_gated_delta(q, k, v, beta, g):
    b, h, l, dk = q.shape
    dv = v.shape[-1]
    n = l // CHUNK
    q = q.reshape(b, h, n, CHUNK, dk)
    k = k.reshape(b, h, n, CHUNK, dk)
    v = v.reshape(b, h, n, CHUNK, dv)
    beta = beta.reshape(b, h, n, CHUNK)
    gc = jnp.cumsum(g.reshape(b, h, n, CHUNK), axis=-1)
    tri_incl = jnp.tril(jnp.ones((CHUNK, CHUNK), dtype=bool))
    tri_strict = jnp.tril(jnp.ones((CHUNK, CHUNK), dtype=bool), -1)
    decay = jnp.exp(jnp.where(tri_incl, gc[..., :, None] - gc[..., None, :], -jnp.inf))
    kk = jnp.einsum("bhncd,bhnsd->bhncs", k, k)
    lower = jnp.where(tri_strict, beta[..., :, None] * kk * decay, 0.0)
    a_mat = lower + jnp.eye(CHUNK, dtype=jnp.float32)
    rhs = jnp.concatenate([v * beta[..., None], k * (beta * jnp.exp(gc))[..., None]], axis=-1)
    sol = lax.linalg.triangular_solve(a_mat, rhs, left_side=True, lower=True, unit_diagonal=True)
    u, w = sol[..., :dv], sol[..., dv:]
    attn = jnp.einsum("bhncd,bhnsd->bhncs", q, k) * decay
    q_g = q * jnp.exp(gc)[..., None]
    g_last = gc[..., -1]
    k_end = k * jnp.exp(g_last[..., None] - gc)[..., None]

    def step(s, inp):
        qg_c, kend_c, u_c, w_c, at_c, gl_c = inp
        wv = u_c - jnp.einsum("bhck,bhkv->bhcv", w_c, s)
        o_c = jnp.einsum("bhck,bhkv->bhcv", qg_c, s) + jnp.einsum("bhcs,bhsv->bhcv", at_c, wv)
        s = s * jnp.exp(gl_c)[..., None, None] + jnp.einsum("bhck,bhcv->bhkv", kend_c, wv)
        return s, o_c

    xs = tuple(jnp.moveaxis(t, 2, 0) for t in (q_g, k_end, u, w, attn, g_last))
    s0 = jnp.zeros((b, h, dk, dv), jnp.float32)
    _, o = lax.scan(step, s0, xs)
    return jnp.moveaxis(o, 0, 2).reshape(b, h, l, dv)


def _fwd_setup_inputs(seed: int = 0) -> dict:
    key = jax.random.key(seed)
    ks = jax.random.split(key, 24)
    f32 = jnp.float32
    nrm = lambda kk, shape, s: jax.random.normal(kk, shape, f32) * s
    x = nrm(ks[0], (BATCH, SEQ, D_MODEL), 1.0)
    meta = nrm(ks[1], (N_META, D_MODEL), 1.0)
    norm_w = 1.0 + nrm(ks[2], (DEPTH, D_MODEL), 0.02)
    w_in = nrm(ks[3], (DEPTH, D_MODEL, IN_W), D_MODEL ** -0.5)
    conv_qkv_w = nrm(ks[4], (DEPTH, DN_CONV, 2 * QK_W + V_W), DN_CONV ** -0.5)
    a_log = jnp.log(jax.random.uniform(ks[5], (DEPTH, DN_HEADS), f32, 1.0, 16.0))
    dt = jnp.exp(jax.random.uniform(ks[6], (DEPTH, DN_HEADS), f32, np.log(1e-3), np.log(1e-1)))
    dt_bias = dt + jnp.log(-jnp.expm1(-dt))
    dn_norm_w = 1.0 + nrm(ks[7], (DEPTH, DN_DV), 0.02)
    w_dn_out = nrm(ks[8], (DEPTH, V_W, D_MODEL), V_W ** -0.5)
    dw_w = nrm(ks[9], (DEPTH, CF_KERNEL, CF_CH), CF_KERNEL ** -0.5)
    dw_b = nrm(ks[10], (DEPTH, CF_CH), 0.01)
    ln_w = 1.0 + nrm(ks[11], (DEPTH, CF_CH), 0.02)
    ln_b = nrm(ks[12], (DEPTH, CF_CH), 0.01)
    w_cf_out = nrm(ks[13], (DEPTH, CF_CH, D_MODEL), CF_CH ** -0.5)
    b_cf_out = nrm(ks[14], (DEPTH, D_MODEL), 0.01)
    w_o = nrm(ks[15], (DEPTH, D_MODEL, D_MODEL), D_MODEL ** -0.5)
    final_norm_w = 1.0 + nrm(ks[16], (D_MODEL,), 0.02)
    return {"x": x, "meta": meta, "norm_w": norm_w, "w_in": w_in, "conv_qkv_w": conv_qkv_w,
            "a_log": a_log, "dt_bias": dt_bias, "dn_norm_w": dn_norm_w, "w_dn_out": w_dn_out,
            "dw_w": dw_w, "dw_b": dw_b, "ln_w": ln_w, "ln_b": ln_b, "w_cf_out": w_cf_out,
            "b_cf_out": b_cf_out, "w_o": w_o, "final_norm_w": final_norm_w}


def _fwd_reference(x, meta, norm_w, w_in, conv_qkv_w, a_log, dt_bias, dn_norm_w, w_dn_out,
              dw_w, dw_b, ln_w, ln_b, w_cf_out, b_cf_out, w_o, final_norm_w):
    b = x.shape[0]
    x = jnp.concatenate([jnp.broadcast_to(meta[None].astype(x.dtype), (b, N_META, D_MODEL)), x], axis=1)
    l = x.shape[1]
    pad = (-l) % CHUNK
    split_idx = [int(v) for v in np.cumsum(IN_SIZES)[:-1]]
    for layer in range(DEPTH):
        h = rmsnorm(x, norm_w[layer])
        proj = h @ w_in[layer]
        q, k, v, za, b_lin, a_lin, glu, zb, ga, gb = jnp.split(proj, split_idx, axis=-1)

        qkv = jax.nn.silu(causal_dwconv(jnp.concatenate([q, k, v], axis=-1), conv_qkv_w[layer]))
        q, k, v = jnp.split(qkv, [QK_W, 2 * QK_W], axis=-1)
        q = l2norm(q.reshape(b, l, DN_HEADS, DN_DK)) * (DN_DK ** -0.5)
        k = l2norm(k.reshape(b, l, DN_HEADS, DN_DK))
        v = v.reshape(b, l, DN_HEADS, DN_DV).astype(jnp.float32)
        beta = jax.nn.sigmoid(b_lin.astype(jnp.float32))
        g = -jnp.exp(a_log[layer].astype(jnp.float32)) * jax.nn.softplus(
            a_lin.astype(jnp.float32) + dt_bias[layer].astype(jnp.float32))
        to_bh = lambda t: jnp.pad(jnp.moveaxis(t, 2, 1), [(0, 0), (0, 0), (pad, 0)] + [(0, 0)] * (t.ndim - 3))
        o = chunk_gated_delta(to_bh(q), to_bh(k), to_bh(v), to_bh(beta), to_bh(g))[:, :, pad:]
        o = jnp.moveaxis(o, 1, 2)
        o = rmsnorm(o, dn_norm_w[layer]) * jax.nn.silu(za.reshape(b, l, DN_HEADS, DN_DV).astype(jnp.float32))
        y_a = o.reshape(b, l, V_W).astype(x.dtype) @ w_dn_out[layer]

        glu_a, glu_b = jnp.split(glu, 2, axis=-1)
        c = glu_a * jax.nn.sigmoid(glu_b)
        c = causal_dwconv(c, dw_w[layer]) + dw_b[layer]
        c = jax.nn.silu(layernorm(c, ln_w[layer], ln_b[layer])) * jax.nn.silu(zb)
        y_b = c @ w_cf_out[layer] + b_cf_out[layer]

        merged = jax.nn.sigmoid(ga) * y_a + jax.nn.sigmoid(gb) * y_b
        x = x + merged @ w_o[layer]
    return rmsnorm(x, final_norm_w)[:, N_META:]


import jax as _jax
import jax.numpy as _jnp

TWIN_FORMAT = 'train_step'
FWD_PARAMS = ['x', 'meta', 'norm_w', 'w_in', 'conv_qkv_w', 'a_log', 'dt_bias', 'dn_norm_w', 'w_dn_out', 'dw_w', 'dw_b', 'ln_w', 'ln_b', 'w_cf_out', 'b_cf_out', 'w_o', 'final_norm_w']
TWIN_WEIGHTS = ['meta', 'norm_w', 'w_in', 'conv_qkv_w', 'a_log', 'dt_bias', 'dn_norm_w', 'w_dn_out', 'dw_w', 'dw_b', 'ln_w', 'ln_b', 'w_cf_out', 'b_cf_out', 'w_o', 'final_norm_w']
TWIN_DIFF_INPUT = 'x'
TWIN_INPUTS = ['x', 'meta', 'norm_w', 'w_in', 'conv_qkv_w', 'a_log', 'dt_bias', 'dn_norm_w', 'w_dn_out', 'dw_w', 'dw_b', 'ln_w', 'ln_b', 'w_cf_out', 'b_cf_out', 'w_o', 'final_norm_w', 'loss_target', 'm_meta', 'm_norm_w', 'm_w_in', 'm_conv_qkv_w', 'm_a_log', 'm_dt_bias', 'm_dn_norm_w', 'm_w_dn_out', 'm_dw_w', 'm_dw_b', 'm_ln_w', 'm_ln_b', 'm_w_cf_out', 'm_b_cf_out', 'm_w_o', 'm_final_norm_w', 'v_meta', 'v_norm_w', 'v_w_in', 'v_conv_qkv_w', 'v_a_log', 'v_dt_bias', 'v_dn_norm_w', 'v_w_dn_out', 'v_dw_w', 'v_dw_b', 'v_ln_w', 'v_ln_b', 'v_w_cf_out', 'v_b_cf_out', 'v_w_o', 'v_final_norm_w']
TWIN_OUTPUTS = ['loss', 'grad_x', 'grad_meta', 'grad_norm_w', 'grad_w_in', 'grad_conv_qkv_w', 'grad_a_log', 'grad_dt_bias', 'grad_dn_norm_w', 'grad_w_dn_out', 'grad_dw_w', 'grad_dw_b', 'grad_ln_w', 'grad_ln_b', 'grad_w_cf_out', 'grad_b_cf_out', 'grad_w_o', 'grad_final_norm_w', 'delta_meta', 'delta_norm_w', 'delta_w_in', 'delta_conv_qkv_w', 'delta_a_log', 'delta_dt_bias', 'delta_dn_norm_w', 'delta_w_dn_out', 'delta_dw_w', 'delta_dw_b', 'delta_ln_w', 'delta_ln_b', 'delta_w_cf_out', 'delta_b_cf_out', 'delta_w_o', 'delta_final_norm_w', 'new_m_meta', 'new_m_norm_w', 'new_m_w_in', 'new_m_conv_qkv_w', 'new_m_a_log', 'new_m_dt_bias', 'new_m_dn_norm_w', 'new_m_w_dn_out', 'new_m_dw_w', 'new_m_dw_b', 'new_m_ln_w', 'new_m_ln_b', 'new_m_w_cf_out', 'new_m_b_cf_out', 'new_m_w_o', 'new_m_final_norm_w', 'new_v_meta', 'new_v_norm_w', 'new_v_w_in', 'new_v_conv_qkv_w', 'new_v_a_log', 'new_v_dt_bias', 'new_v_dn_norm_w', 'new_v_w_dn_out', 'new_v_dw_w', 'new_v_dw_b', 'new_v_ln_w', 'new_v_ln_b', 'new_v_w_cf_out', 'new_v_b_cf_out', 'new_v_w_o', 'new_v_final_norm_w']
TWIN_LEAF_KINDS = {'loss': 'loss', 'grad_x': 'grad_x', 'grad_meta': 'grad_w', 'grad_norm_w': 'grad_w', 'grad_w_in': 'grad_w', 'grad_conv_qkv_w': 'grad_w', 'grad_a_log': 'grad_w', 'grad_dt_bias': 'grad_w', 'grad_dn_norm_w': 'grad_w', 'grad_w_dn_out': 'grad_w', 'grad_dw_w': 'grad_w', 'grad_dw_b': 'grad_w', 'grad_ln_w': 'grad_w', 'grad_ln_b': 'grad_w', 'grad_w_cf_out': 'grad_w', 'grad_b_cf_out': 'grad_w', 'grad_w_o': 'grad_w', 'grad_final_norm_w': 'grad_w', 'delta_meta': 'delta_w', 'delta_norm_w': 'delta_w', 'delta_w_in': 'delta_w', 'delta_conv_qkv_w': 'delta_w', 'delta_a_log': 'delta_w', 'delta_dt_bias': 'delta_w', 'delta_dn_norm_w': 'delta_w', 'delta_w_dn_out': 'delta_w', 'delta_dw_w': 'delta_w', 'delta_dw_b': 'delta_w', 'delta_ln_w': 'delta_w', 'delta_ln_b': 'delta_w', 'delta_w_cf_out': 'delta_w', 'delta_b_cf_out': 'delta_w', 'delta_w_o': 'delta_w', 'delta_final_norm_w': 'delta_w', 'new_m_meta': 'new_m', 'new_m_norm_w': 'new_m', 'new_m_w_in': 'new_m', 'new_m_conv_qkv_w': 'new_m', 'new_m_a_log': 'new_m', 'new_m_dt_bias': 'new_m', 'new_m_dn_norm_w': 'new_m', 'new_m_w_dn_out': 'new_m', 'new_m_dw_w': 'new_m', 'new_m_dw_b': 'new_m', 'new_m_ln_w': 'new_m', 'new_m_ln_b': 'new_m', 'new_m_w_cf_out': 'new_m', 'new_m_b_cf_out': 'new_m', 'new_m_w_o': 'new_m', 'new_m_final_norm_w': 'new_m', 'new_v_meta': 'new_v', 'new_v_norm_w': 'new_v', 'new_v_w_in': 'new_v', 'new_v_conv_qkv_w': 'new_v', 'new_v_a_log': 'new_v', 'new_v_dt_bias': 'new_v', 'new_v_dn_norm_w': 'new_v', 'new_v_w_dn_out': 'new_v', 'new_v_dw_w': 'new_v', 'new_v_dw_b': 'new_v', 'new_v_ln_w': 'new_v', 'new_v_ln_b': 'new_v', 'new_v_w_cf_out': 'new_v', 'new_v_b_cf_out': 'new_v', 'new_v_w_o': 'new_v', 'new_v_final_norm_w': 'new_v'}


def _forward(args):
    return _fwd_reference(*[args[k] for k in FWD_PARAMS])


def _output_shape():
    def fwd():
        inp = _fwd_setup_inputs(0)
        return _fwd_reference(*[inp[k] for k in FWD_PARAMS])
    out = _jax.eval_shape(fwd)
    return out.shape, out.dtype

N_MICROBATCH = 1
ADAM_LR = 0.001
ADAM_B1 = 0.9
ADAM_B2 = 0.999
ADAM_EPS = 1e-08
ADAM_WD = 0.01
ADAM_STEP = 10
PER_EXAMPLE_BATCH_AXIS = {'x': 0, 'loss_target': 0}
SHARED_INPUTS = []
_WEIGHT_DTYPES = {'meta': _jnp.float32, 'norm_w': _jnp.float32, 'w_in': _jnp.float32, 'conv_qkv_w': _jnp.float32, 'a_log': _jnp.float32, 'dt_bias': _jnp.float32, 'dn_norm_w': _jnp.float32, 'w_dn_out': _jnp.float32, 'dw_w': _jnp.float32, 'dw_b': _jnp.float32, 'ln_w': _jnp.float32, 'ln_b': _jnp.float32, 'w_cf_out': _jnp.float32, 'b_cf_out': _jnp.float32, 'w_o': _jnp.float32, 'final_norm_w': _jnp.float32}
MOMENT_SCALE = {'meta': 4.155832e-03, 'norm_w': 1.463104e-01, 'w_in': 4.899655e-02, 'conv_qkv_w': 5.747747e-02, 'a_log': 8.235387e-01, 'dt_bias': 7.890063e-01, 'dn_norm_w': 2.197676e-01, 'w_dn_out': 7.345636e-02, 'dw_w': 4.725356e-02, 'dw_b': 8.603768e-02, 'ln_w': 5.571296e-02, 'ln_b': 4.607870e-02, 'w_cf_out': 4.581623e-02, 'b_cf_out': 1.399647e-01, 'w_o': 8.720474e-02, 'final_norm_w': 6.399940e+01}


def _to_microbatches(a, axis):
    t = _jnp.moveaxis(a, axis, 0)
    t = t.reshape((N_MICROBATCH, t.shape[0] // N_MICROBATCH) + t.shape[1:])
    return _jnp.moveaxis(t, 1, axis + 1)


def setup_inputs(seed: int = 0) -> dict:
    inp = _fwd_setup_inputs(seed)
    key = _jax.random.fold_in(_jax.random.key(seed), 7919)
    shape, _ = _output_shape()
    out = dict(inp)
    out["loss_target"] = _jax.random.normal(_jax.random.fold_in(key, 0), shape, _jnp.float32)
    for i, name in enumerate(TWIN_WEIGHTS):
        w = inp[name].astype(_jnp.float32)
        if MOMENT_SCALE is None:
            s = _jnp.sqrt(_jnp.mean(_jnp.square(w)) + 1e-30)
        else:
            s = MOMENT_SCALE[name]
        km, kv = _jax.random.split(_jax.random.fold_in(key, i + 1))
        out[name] = w
        out["m_" + name] = s * _jax.random.normal(km, w.shape, _jnp.float32)
        out["v_" + name] = (s * s) * _jax.random.uniform(kv, w.shape, _jnp.float32, 0.5, 1.5)
    if N_MICROBATCH > 1:
        for name, axis in PER_EXAMPLE_BATCH_AXIS.items():
            out[name] = _to_microbatches(out[name], axis)
    return {'x': out['x'], 'meta': out['meta'], 'norm_w': out['norm_w'], 'w_in': out['w_in'], 'conv_qkv_w': out['conv_qkv_w'], 'a_log': out['a_log'], 'dt_bias': out['dt_bias'], 'dn_norm_w': out['dn_norm_w'], 'w_dn_out': out['w_dn_out'], 'dw_w': out['dw_w'], 'dw_b': out['dw_b'], 'ln_w': out['ln_w'], 'ln_b': out['ln_b'], 'w_cf_out': out['w_cf_out'], 'b_cf_out': out['b_cf_out'], 'w_o': out['w_o'], 'final_norm_w': out['final_norm_w'], 'loss_target': out['loss_target'], 'm_meta': out['m_meta'], 'm_norm_w': out['m_norm_w'], 'm_w_in': out['m_w_in'], 'm_conv_qkv_w': out['m_conv_qkv_w'], 'm_a_log': out['m_a_log'], 'm_dt_bias': out['m_dt_bias'], 'm_dn_norm_w': out['m_dn_norm_w'], 'm_w_dn_out': out['m_w_dn_out'], 'm_dw_w': out['m_dw_w'], 'm_dw_b': out['m_dw_b'], 'm_ln_w': out['m_ln_w'], 'm_ln_b': out['m_ln_b'], 'm_w_cf_out': out['m_w_cf_out'], 'm_b_cf_out': out['m_b_cf_out'], 'm_w_o': out['m_w_o'], 'm_final_norm_w': out['m_final_norm_w'], 'v_meta': out['v_meta'], 'v_norm_w': out['v_norm_w'], 'v_w_in': out['v_w_in'], 'v_conv_qkv_w': out['v_conv_qkv_w'], 'v_a_log': out['v_a_log'], 'v_dt_bias': out['v_dt_bias'], 'v_dn_norm_w': out['v_dn_norm_w'], 'v_w_dn_out': out['v_w_dn_out'], 'v_dw_w': out['v_dw_w'], 'v_dw_b': out['v_dw_b'], 'v_ln_w': out['v_ln_w'], 'v_ln_b': out['v_ln_b'], 'v_w_cf_out': out['v_w_cf_out'], 'v_b_cf_out': out['v_b_cf_out'], 'v_w_o': out['v_w_o'], 'v_final_norm_w': out['v_final_norm_w']}


def _loss(weights, diff, rest, loss_target):
    with _jax.named_scope("forward"):
        args = {**rest, TWIN_DIFF_INPUT: diff, **{k: w.astype(_WEIGHT_DTYPES[k]) for k, w in weights.items()}}
        y = _forward(args)
    with _jax.named_scope("loss_head"):
        err = _jnp.square(y.astype(_jnp.float32) - loss_target)
        return 0.5 * _jnp.sum(_jnp.mean(err, axis=-1)) if err.ndim else 0.5 * err


def _adamw(w, g, m, v):
    m = ADAM_B1 * m + (1.0 - ADAM_B1) * g
    v = ADAM_B2 * v + (1.0 - ADAM_B2) * _jnp.square(g)
    m_hat = m / (1.0 - ADAM_B1 ** ADAM_STEP)
    v_hat = v / (1.0 - ADAM_B2 ** ADAM_STEP)
    delta = -ADAM_LR * (m_hat / (_jnp.sqrt(v_hat) + ADAM_EPS) + ADAM_WD * w)
    return delta, m, v


def reference(x, meta, norm_w, w_in, conv_qkv_w, a_log, dt_bias, dn_norm_w, w_dn_out, dw_w, dw_b, ln_w, ln_b, w_cf_out, b_cf_out, w_o, final_norm_w, loss_target, m_meta, m_norm_w, m_w_in, m_conv_qkv_w, m_a_log, m_dt_bias, m_dn_norm_w, m_w_dn_out, m_dw_w, m_dw_b, m_ln_w, m_ln_b, m_w_cf_out, m_b_cf_out, m_w_o, m_final_norm_w, v_meta, v_norm_w, v_w_in, v_conv_qkv_w, v_a_log, v_dt_bias, v_dn_norm_w, v_w_dn_out, v_dw_w, v_dw_b, v_ln_w, v_ln_b, v_w_cf_out, v_b_cf_out, v_w_o, v_final_norm_w):
    given = dict(x=x, meta=meta, norm_w=norm_w, w_in=w_in, conv_qkv_w=conv_qkv_w, a_log=a_log, dt_bias=dt_bias, dn_norm_w=dn_norm_w, w_dn_out=w_dn_out, dw_w=dw_w, dw_b=dw_b, ln_w=ln_w, ln_b=ln_b, w_cf_out=w_cf_out, b_cf_out=b_cf_out, w_o=w_o, final_norm_w=final_norm_w, loss_target=loss_target, m_meta=m_meta, m_norm_w=m_norm_w, m_w_in=m_w_in, m_conv_qkv_w=m_conv_qkv_w, m_a_log=m_a_log, m_dt_bias=m_dt_bias, m_dn_norm_w=m_dn_norm_w, m_w_dn_out=m_w_dn_out, m_dw_w=m_dw_w, m_dw_b=m_dw_b, m_ln_w=m_ln_w, m_ln_b=m_ln_b, m_w_cf_out=m_w_cf_out, m_b_cf_out=m_b_cf_out, m_w_o=m_w_o, m_final_norm_w=m_final_norm_w, v_meta=v_meta, v_norm_w=v_norm_w, v_w_in=v_w_in, v_conv_qkv_w=v_conv_qkv_w, v_a_log=v_a_log, v_dt_bias=v_dt_bias, v_dn_norm_w=v_dn_norm_w, v_w_dn_out=v_w_dn_out, v_dw_w=v_dw_w, v_dw_b=v_dw_b, v_ln_w=v_ln_w, v_ln_b=v_ln_b, v_w_cf_out=v_w_cf_out, v_b_cf_out=v_b_cf_out, v_w_o=v_w_o, v_final_norm_w=v_final_norm_w)
    weights = {n: given[n] for n in TWIN_WEIGHTS}
    shared = {n: given[n] for n in SHARED_INPUTS}
    per_example = {n: given[n] for n in ['x']}
    grad_fn = _jax.value_and_grad(_loss, argnums=(0, 1))

    def one_microbatch(ex, loss_target):
        ex = dict(ex)
        diff = ex.pop(TWIN_DIFF_INPUT)
        return grad_fn(weights, diff, {**shared, **ex}, loss_target)

    if N_MICROBATCH == 1:
        loss, (grad_w, grad_x) = one_microbatch(per_example, given["loss_target"])
    else:
        def body(carry, xs):
            loss_sum, grad_sum = carry
            l_k, (gw_k, gx_k) = one_microbatch(xs[0], xs[1])
            with _jax.named_scope("update"):
                return (loss_sum + l_k, _jax.tree.map(_jnp.add, grad_sum, gw_k)), gx_k

        init = (_jnp.zeros((), _jnp.float32), _jax.tree.map(_jnp.zeros_like, weights))
        (loss, grad_w), grad_x = _jax.lax.scan(body, init, (per_example, given["loss_target"]))
    with _jax.named_scope("update"):
        delta_w, new_m, new_v = {}, {}, {}
        for n in TWIN_WEIGHTS:
            delta_w[n], new_m[n], new_v[n] = _adamw(weights[n], grad_w[n], given["m_" + n], given["v_" + n])
    return (loss, grad_x, *[grad_w[n] for n in TWIN_WEIGHTS], *[delta_w[n] for n in TWIN_WEIGHTS],
            *[new_m[n] for n in TWIN_WEIGHTS], *[new_v[n] for n in TWIN_WEIGHTS])
```

```python
import functools

import jax
import jax.numpy as jnp
from jax import lax
from jax.experimental import pallas as pl
from jax.experimental.pallas import tpu as pltpu

F32 = jnp.float32
BF16 = jnp.bfloat16
HI = lax.Precision.HIGHEST

D = 1024
H = 8
DK = 128
C = 64
NMETA = 16
KQ = 4
KD = 31
HALO_Q = 8
HALO_D = 32
EPS = 1e-6
NDEV = 8
LANE = 128
MIB = 1024 * 1024

ADAM_LR, ADAM_B1, ADAM_B2, ADAM_EPS, ADAM_WD, ADAM_STEP = 0.001, 0.9, 0.999, 1e-08, 0.01, 10

CB_Q, CB_K, CB_V, CB_ZA, CB_GA_, CB_GB_, CB_MA, CB_MB, CB_ZB = range(9)
NCB = 9


def _pick(n, cands):
    for c in cands:
        if n % c == 0:
            return c
    raise ValueError(f"no tile for {n}")


def _cp(sem=None, vmem_mib=40):
    kw = dict(vmem_limit_bytes=vmem_mib * MIB)
    if sem is not None:
        kw["dimension_semantics"] = sem
    return pltpu.CompilerParams(**kw)


def _call(body, **kw):
    return pl.pallas_call(body, **kw)


def _dot(a, b):
    return jnp.dot(a.astype(BF16), b.astype(BF16), preferred_element_type=F32)


def _dot_nt(a, b):
    return lax.dot_general(a.astype(BF16), b.astype(BF16), (((1,), (1,)), ((), ())), preferred_element_type=F32)


def _dot_tn(a, b):
    return lax.dot_general(a.astype(BF16), b.astype(BF16), (((0,), (0,)), ((), ())), preferred_element_type=F32)


def _dot_hi(a, b):
    return jnp.dot(a, b, precision=HI, preferred_element_type=F32)


def _dot_nt_hi(a, b):
    return lax.dot_general(a, b, (((1,), (1,)), ((), ())), precision=HI, preferred_element_type=F32)


def _sig(x):
    return jax.nn.sigmoid(x)


def _dsilu(x, s):
    return s * (1.0 + x * (1.0 - s))


def _rowsum(x):
    return jnp.sum(x, axis=-1, keepdims=True)


def _colsum(x):
    return jnp.sum(x, axis=0, keepdims=True)


def _exchange(arrs, scatter, name):
    n = len(arrs)
    out_shape = []
    for a, sc in zip(arrs, scatter):
        shp = a.shape if sc else (NDEV,) + a.shape
        out_shape.append(jax.ShapeDtypeStruct(shp, a.dtype))

    def body(*refs):
        ins, outs = refs[:n], refs[n:2 * n]
        send_sems, recv_sems, loc_sems = refs[2 * n:]
        x, y, c = lax.axis_index("x"), lax.axis_index("y"), lax.axis_index("c")
        me = 4 * x + 2 * y + c
        copies = []
        for a in range(n):
            for k in range(1, NDEV):
                px = 1 - x if (k >> 2) & 1 else x
                py = 1 - y if (k >> 1) & 1 else y
                pc = 1 - c if k & 1 else c
                src = ins[a].at[4 * px + 2 * py + pc] if scatter[a] else ins[a]
                cp = pltpu.make_async_remote_copy(
                    src_ref=src, dst_ref=outs[a].at[me],
                    send_sem=send_sems.at[a * (NDEV - 1) + k - 1], recv_sem=recv_sems.at[a * (NDEV - 1) + k - 1],
                    device_id=(px, py, pc), device_id_type=pl.DeviceIdType.MESH)
                cp.start()
                copies.append(cp)
            loc = pltpu.make_async_copy(ins[a].at[me] if scatter[a] else ins[a], outs[a].at[me], loc_sems.at[a])
            loc.start()
            copies.append(loc)
        for cp in copies:
            cp.wait()

    any_spec = pl.BlockSpec(memory_space=pl.ANY)
    return _call(
        body, name=name, out_shape=tuple(out_shape),
        in_specs=[any_spec] * n, out_specs=tuple([any_spec] * n),
        scratch_shapes=[pltpu.SemaphoreType.DMA((n * (NDEV - 1),)), pltpu.SemaphoreType.DMA((n * (NDEV - 1),)),
                        pltpu.SemaphoreType.DMA((n,))],
    )(*arrs)


def _mm(a, b, name, out_dtype=F32):
    m, k = a.shape
    n = b.shape[1]
    tm = _pick(m, (1664, 832, 640, 320, 128))
    tn = _pick(n, (512, 256, 128))
    tk = _pick(k, (1024, 512, 128))
    nk = k // tk

    def body(a_ref, b_ref, o_ref, *acc):
        part = _dot(a_ref[...], b_ref[...])
        if nk == 1:
            o_ref[...] = part.astype(out_dtype)
        else:
            kk = pl.program_id(2)

            @pl.when(kk == 0)
            def _():
                acc[0][...] = part

            @pl.when(kk > 0)
            def _():
                acc[0][...] += part

            @pl.when(kk == nk - 1)
            def _():
                o_ref[...] = acc[0][...].astype(out_dtype)

    return _call(
        body, name=name, grid=(m // tm, n // tn, nk),
        in_specs=[pl.BlockSpec((tm, tk), lambda i, j, kk: (i, kk)), pl.BlockSpec((tk, tn), lambda i, j, kk: (kk, j))],
        out_specs=pl.BlockSpec((tm, tn), lambda i, j, kk: (i, j)),
        out_shape=jax.ShapeDtypeStruct((m, n), out_dtype),
        scratch_shapes=[pltpu.VMEM((tm, tn), F32)] if nk > 1 else [],
        compiler_params=_cp(("parallel", "parallel", "arbitrary")),
    )(a, b)


def _mm_tn(a, b, name):
    t, m = a.shape
    n = b.shape[1]
    tt = _pick(t, (640, 128))
    tm = _pick(m, (1024, 512, 128))
    tn = _pick(n, (1152, 1024, 512, 128))
    nt = t // tt

    def body(a_ref, b_ref, o_ref):
        s = pl.program_id(2)
        part = _dot_tn(a_ref[...], b_ref[...])

        @pl.when(s == 0)
        def _():
            o_ref[...] = part

        @pl.when(s > 0)
        def _():
            o_ref[...] += part

    return _call(
        body, name=name, grid=(m // tm, n // tn, nt),
        in_specs=[pl.BlockSpec((tt, tm), lambda i, j, s: (s, i)), pl.BlockSpec((tt, tn), lambda i, j, s: (s, j))],
        out_specs=pl.BlockSpec((tm, tn), lambda i, j, s: (i, j)),
        out_shape=jax.ShapeDtypeStruct((m, n), F32),
        compiler_params=_cp(("parallel", "parallel", "arbitrary")),
    )(a, b)


def _proj_fwd(x_ext, norm_w, w_main, w_ba):
    lp = x_ext.shape[0]
    n = w_main.shape[1]
    tm = _pick(lp, (832, 640, 320))
    tn = 512

    def body(x_ref, nw_ref, w_ref, wba_ref, proj_ref, ba_ref, h_ref):
        @pl.when(pl.program_id(1) == 0)
        def _():
            x = x_ref[...]
            r = lax.rsqrt(jnp.mean(x * x, axis=-1, keepdims=True) + EPS)
            h = (x * r * nw_ref[...]).astype(BF16)
            h_ref[...] = h
            ba_ref[...] = jnp.dot(h, wba_ref[...], preferred_element_type=F32)

        proj_ref[...] = jnp.dot(h_ref[...], w_ref[...], preferred_element_type=F32)

    return _call(
        body, name="proj_fwd", grid=(lp // tm, n // tn),
        in_specs=[pl.BlockSpec((tm, D), lambda i, j: (i, 0)), pl.BlockSpec((1, D), lambda i, j: (0, 0)),
                  pl.BlockSpec((D, tn), lambda i, j: (0, j)), pl.BlockSpec((D, LANE), lambda i, j: (0, 0))],
        out_specs=(pl.BlockSpec((tm, tn), lambda i, j: (i, j)), pl.BlockSpec((tm, LANE), lambda i, j: (i, 0)),
                   pl.BlockSpec((tm, D), lambda i, j: (i, 0))),
        out_shape=(jax.ShapeDtypeStruct((lp, n), F32), jax.ShapeDtypeStruct((lp, LANE), F32),
                   jax.ShapeDtypeStruct((lp, D), BF16)),
        compiler_params=_cp(("parallel", "arbitrary")),
    )(x_ext, norm_w, w_main, w_ba)


def _dh_mm(dproj, dba, w_main_t, w_ba_t):
    lp, n = dproj.shape
    tm = _pick(lp, (1664, 640, 320))
    tn = 512
    tk = 1024
    nk = n // tk

    def body(a_ref, ba_ref, b_ref, bba_ref, o_ref, acc):
        kk = pl.program_id(2)

        @pl.when(kk == 0)
        def _():
            acc[...] = jnp.dot(ba_ref[...], bba_ref[...], preferred_element_type=F32)

        acc[...] += jnp.dot(a_ref[...], b_ref[...], preferred_element_type=F32)

        @pl.when(kk == nk - 1)
        def _():
            o_ref[...] = acc[...]

    return _call(
        body, name="dh_mm", grid=(lp // tm, D // tn, nk),
        in_specs=[pl.BlockSpec((tm, tk), lambda i, j, kk: (i, kk)), pl.BlockSpec((tm, LANE), lambda i, j, kk: (i, 0)),
                  pl.BlockSpec((tk, tn), lambda i, j, kk: (kk, j)), pl.BlockSpec((LANE, tn), lambda i, j, kk: (0, j))],
        out_specs=pl.BlockSpec((tm, tn), lambda i, j, kk: (i, j)),
        out_shape=jax.ShapeDtypeStruct((lp, D), F32),
        scratch_shapes=[pltpu.VMEM((tm, tn), F32)],
        compiler_params=_cp(("parallel", "parallel", "arbitrary")),
    )(dproj, dba, w_main_t, w_ba_t)


def _l2norm_heads(a):
    outs, rs = [], []
    for h in range(H):
        blk = a[:, h * DK:(h + 1) * DK]
        r = lax.rsqrt(_rowsum(blk * blk) + EPS)
        outs.append(blk * r)
        rs.append(r)
    return jnp.concatenate(outs, axis=1), rs


def _beta_g(ba, ab, row0, pad):
    lane = lax.broadcasted_iota(jnp.int32, ba.shape, 1)
    rows = row0 + lax.broadcasted_iota(jnp.int32, ba.shape, 0)
    z = ba + ab[1:2, :]
    sp = jnp.maximum(z, 0.0) + jnp.log(1.0 + jnp.exp(-jnp.abs(z)))
    val = jnp.where(lane < H, _sig(ba), -jnp.exp(ab[0:1, :]) * sp)
    return jnp.where((lane < 2 * H) & (rows >= pad), val, 0.0)


def _qkv_conv_fwd(proj, ba, conv_w, ab, pad):
    lp = proj.shape[0]
    te = _pick(lp, (320,))
    hb = te // HALO_Q

    def body(main_ref, halo_ref, cw_ref, ba_ref, ab_ref, out_ref, bg_ref):
        i, s = pl.program_id(0), pl.program_id(1)
        halo = jnp.where(i > 0, halo_ref[...], 0.0)
        pre = jnp.concatenate([halo, main_ref[...]], axis=0)
        cw = cw_ref[...]
        co = cw[0:1, :] * pre[HALO_Q - 3:HALO_Q - 3 + te, :]
        for j in range(1, KQ):
            co = co + cw[j:j + 1, :] * pre[HALO_Q - 3 + j:HALO_Q - 3 + j + te, :]
        a = co * _sig(co)
        nrm, _ = _l2norm_heads(a)
        nrm = nrm * jnp.where(s == 0, DK ** -0.5, 1.0)
        out_ref[...] = jnp.where(s == 2, a, nrm)

        @pl.when(s == 0)
        def _():
            bg_ref[...] = _beta_g(ba_ref[...], ab_ref[...], i * te, pad)

    return _call(
        body, name="qkv_conv_fwd", grid=(lp // te, 3),
        in_specs=[pl.BlockSpec((te, D), lambda i, s: (i, s)),
                  pl.BlockSpec((HALO_Q, D), lambda i, s: (jnp.maximum(i * hb - 1, 0), s)),
                  pl.BlockSpec((KQ, D), lambda i, s: (0, s)),
                  pl.BlockSpec((te, LANE), lambda i, s: (i, 0)),
                  pl.BlockSpec((2, LANE), lambda i, s: (0, 0))],
        out_specs=(pl.BlockSpec((te, D), lambda i, s: (i, s)), pl.BlockSpec((te, LANE), lambda i, s: (i, 0))),
        out_shape=(jax.ShapeDtypeStruct((lp, 3 * D), F32), jax.ShapeDtypeStruct((lp, LANE), F32)),
        compiler_params=_cp(("parallel", "arbitrary")),
    )(proj, proj, conv_w, ba, ab)


def _tri_masks():
    row = lax.broadcasted_iota(jnp.int32, (C, C), 0)
    col = lax.broadcasted_iota(jnp.int32, (C, C), 1)
    return row, col


def _tinv(y, eye):
    t = eye + y
    yk = _dot_hi(y, y)
    for _ in range(4):
        r = _dot_hi(yk, jnp.concatenate([yk, t], axis=1))
        yk = r[:, :C]
        t = t + r[:, C:]
    return t + _dot_hi(yk, t)


def _chunk_common(q, k, v, bcol, gcc, gcr, incl, strict):
    dm = jnp.where(incl, jnp.exp(gcc - gcr), 0.0)
    kk = _dot_nt(k, k)
    qk = _dot_nt(q, k)
    egc = jnp.exp(gcc)
    glast = gcc[C - 1:C, :]
    eend = jnp.exp(glast - gcc)
    elast = jnp.exp(glast)
    rhs = jnp.concatenate([v * bcol, k * (bcol * egc)], axis=1)
    return dm, kk, qk, egc, eend, elast, rhs


def _delta_fwd(qkv, bg):
    lp = qkv.shape[0]
    nc = lp // C

    def body(q_ref, k_ref, v_ref, bg_ref, o_ref, sall_ref, tall_ref, s_scr):
        @pl.when(pl.program_id(0) == 0)
        def _():
            s_scr[...] = jnp.zeros_like(s_scr)

        bgt = bg_ref[...]
        row, col = _tri_masks()
        incl, strict = row >= col, row > col
        eye = (row == col).astype(F32)
        gc_all = _dot_hi(incl.astype(F32), bgt)
        gc_t = _dot_hi(bgt.T, (row <= col).astype(F32))
        for h in range(H):
            sl = slice(h * DK, (h + 1) * DK)
            q, k, v = q_ref[:, sl], k_ref[:, sl], v_ref[:, sl]
            bcol = bgt[:, h:h + 1]
            gcc = gc_all[:, H + h:H + h + 1]
            gcr = gc_t[H + h:H + h + 1, :]
            dm, kk, qk, egc, eend, elast, rhs = _chunk_common(q, k, v, bcol, gcc, gcr, incl, strict)
            t = _tinv(jnp.where(strict, -(bcol * kk * dm), 0.0), eye)
            sol = _dot_hi(t, rhs)
            u, w = sol[:, :DK], sol[:, DK:]
            s = s_scr[h]
            wv = u - _dot(w, s)
            o_ref[:, sl] = _dot(q * egc, s) + _dot(qk * dm, wv)
            sall_ref[0, h] = s
            tall_ref[0, h] = t
            s_scr[h] = s * elast + _dot_tn(k * eend, wv)

    blk = lambda j: pl.BlockSpec((C, D), lambda n: (n, j))
    return _call(
        body, name="delta_fwd", grid=(nc,),
        in_specs=[blk(0), blk(1), blk(2), pl.BlockSpec((C, LANE), lambda n: (n, 0))],
        out_specs=(pl.BlockSpec((C, D), lambda n: (n, 0)),
                   pl.BlockSpec((1, H, DK, DK), lambda n: (n, 0, 0, 0)),
                   pl.BlockSpec((1, H, C, C), lambda n: (n, 0, 0, 0))),
        out_shape=(jax.ShapeDtypeStruct((lp, D), F32), jax.ShapeDtypeStruct((nc, H, DK, DK), F32),
                   jax.ShapeDtypeStruct((nc, H, C, C), F32)),
        scratch_shapes=[pltpu.VMEM((H, DK, DK), F32)],
        compiler_params=_cp(("arbitrary",)),
    )(qkv, qkv, qkv, bg)


def _delta_bwd(qkv, bg, sall, tall, do):
    lp = qkv.shape[0]
    nc = lp // C

    def body(q_ref, k_ref, v_ref, bg_ref, sall_ref, tall_ref, do_ref, dqkv_ref, dbg_ref, ds_scr):
        @pl.when(pl.program_id(0) == 0)
        def _():
            ds_scr[...] = jnp.zeros_like(ds_scr)

        bgt = bg_ref[...]
        row, col = _tri_masks()
        incl, strict = row >= col, row > col
        upper = (row <= col).astype(F32)
        gc_all = _dot_hi(incl.astype(F32), bgt)
        gc_t = _dot_hi(bgt.T, upper)
        lane = lax.broadcasted_iota(jnp.int32, (C, LANE), 1)
        lastrow = lax.broadcasted_iota(jnp.int32, (C, 1), 0) == C - 1
        dbeta_t = jnp.zeros((C, LANE), F32)
        dgc_t = jnp.zeros((C, LANE), F32)
        for h in range(H):
            sl = slice(h * DK, (h + 1) * DK)
            q, k, v = q_ref[:, sl], k_ref[:, sl], v_ref[:, sl]
            do_h = do_ref[:, sl]
            bcol = bgt[:, h:h + 1]
            gcc = gc_all[:, H + h:H + h + 1]
            gcr = gc_t[H + h:H + h + 1, :]
            dm, kk, qk, egc, eend, elast, rhs = _chunk_common(q, k, v, bcol, gcc, gcr, incl, strict)
            s, t, dsn = sall_ref[0, h], tall_ref[0, h], ds_scr[h]
            sol = _dot_hi(t, rhs)
            w = sol[:, DK:]
            qg, kend = q * egc, k * eend
            wv = sol[:, :DK] - _dot(w, s)
            p = qk * dm
            dwv = _dot_tn(p, do_h) + _dot(kend, dsn)
            dp = jnp.where(incl, _dot_nt(do_h, wv), 0.0)
            dqg = _dot_nt(do_h, s)
            dkend = _dot_nt(wv, dsn)
            ds_scr[h] = _dot_tn(qg, do_h) + elast * dsn - _dot_tn(w, dwv)
            dglast = elast * jnp.sum(s * dsn, keepdims=True)
            dw = -_dot_nt(dwv, s)
            drhs = _dot_hi(t.T, jnp.concatenate([dwv, dw], axis=1))
            drv, drk = drhs[:, :DK], drhs[:, DK:]
            dn = jnp.where(strict, -_dot_nt_hi(drhs, sol), 0.0)
            rk = _rowsum(drk * k)
            dkk = dn * (bcol * dm)
            dqk = dp * dm
            e = (dn * (bcol * kk) + dp * qk) * dm
            tk = _rowsum(dkend * kend)
            dgc = rk * bcol * egc + _rowsum(e) - _rowsum(e.T) + _rowsum(dqg * qg) - tk
            dgc = dgc + jnp.where(lastrow, dglast + jnp.sum(tk, keepdims=True), 0.0)
            dbeta = _rowsum(drv * v) + rk * egc + _rowsum(dn * kk * dm)
            dqkv_ref[:, sl] = _dot(dqk, k) + dqg * egc
            dqkv_ref[:, D + h * DK:D + (h + 1) * DK] = (drk * (bcol * egc) + _dot(dkk, k) + _dot_tn(dkk, k)
                                                       + _dot_tn(dqk, q) + dkend * eend)
            dqkv_ref[:, 2 * D + h * DK:2 * D + (h + 1) * DK] = bcol * drv
            dbeta_t = jnp.where(lane == h, dbeta, dbeta_t)
            dgc_t = jnp.where(lane == H + h, dgc, dgc_t)
        dbg_ref[...] = dbeta_t + _dot_hi(upper, dgc_t)

    rev = lambda n: nc - 1 - n
    blk = lambda j: pl.BlockSpec((C, D), lambda n: (rev(n), j))
    return _call(
        body, name="delta_bwd", grid=(nc,),
        in_specs=[blk(0), blk(1), blk(2), pl.BlockSpec((C, LANE), lambda n: (rev(n), 0)),
                  pl.BlockSpec((1, H, DK, DK), lambda n: (rev(n), 0, 0, 0)),
                  pl.BlockSpec((1, H, C, C), lambda n: (rev(n), 0, 0, 0)),
                  pl.BlockSpec((C, D), lambda n: (rev(n), 0))],
        out_specs=(pl.BlockSpec((C, 3 * D), lambda n: (rev(n), 0)), pl.BlockSpec((C, LANE), lambda n: (rev(n), 0))),
        out_shape=(jax.ShapeDtypeStruct((lp, 3 * D), F32), jax.ShapeDtypeStruct((lp, LANE), F32)),
        scratch_shapes=[pltpu.VMEM((H, DK, DK), F32)],
        compiler_params=_cp(("arbitrary",)),
    )(qkv, qkv, qkv, bg, sall, tall, do)


def _o_post_fwd(o, proj, dn_w):
    lp = o.shape[0]
    te = _pick(lp, (640, 320))

    def body(o_ref, za_ref, w_ref, out_ref):
        za = za_ref[...]
        gate = za * _sig(za)
        for h in range(H):
            sl = slice(h * DK, (h + 1) * DK)
            oh = o_ref[:, sl]
            r = lax.rsqrt(jnp.mean(oh * oh, axis=-1, keepdims=True) + EPS)
            out_ref[:, sl] = (oh * r * w_ref[...] * gate[:, sl]).astype(BF16)

    return _call(
        body, name="o_post_fwd", grid=(lp // te,),
        in_specs=[pl.BlockSpec((te, D), lambda i: (i, 0)), pl.BlockSpec((te, D), lambda i: (i, CB_ZA)),
                  pl.BlockSpec((1, DK), lambda i: (0, 0))],
        out_specs=pl.BlockSpec((te, D), lambda i: (i, 0)),
        out_shape=jax.ShapeDtypeStruct((lp, D), BF16),
        compiler_params=_cp(("parallel",)),
    )(o, proj, dn_w)


def _o_post_bwd(do_n, o, proj, dn_w, dproj):
    lp = o.shape[0]
    te = _pick(lp, (640, 320))

    def body(don_ref, o_ref, za_ref, w_ref, _, do_ref, dza_ref, dw_ref):
        @pl.when(pl.program_id(0) == 0)
        def _():
            dw_ref[...] = jnp.zeros_like(dw_ref)

        za = za_ref[...]
        sz = _sig(za)
        gate, dgate = za * sz, _dsilu(za, sz)
        w = w_ref[...]
        dw = jnp.zeros((1, DK), F32)
        for h in range(H):
            sl = slice(h * DK, (h + 1) * DK)
            oh, g = o_ref[:, sl], don_ref[:, sl]
            r = lax.rsqrt(jnp.mean(oh * oh, axis=-1, keepdims=True) + EPS)
            ohat = oh * r
            dza_ref[:, sl] = (g * ohat * w * dgate[:, sl]).astype(BF16)
            don = g * gate[:, sl]
            dw = dw + _colsum(don * ohat)
            dohat = don * w
            do_ref[:, sl] = r * (dohat - ohat * jnp.mean(dohat * ohat, axis=-1, keepdims=True))
        dw_ref[...] += dw

    return _call(
        body, name="o_post_bwd", grid=(lp // te,),
        in_specs=[pl.BlockSpec((te, D), lambda i: (i, 0)), pl.BlockSpec((te, D), lambda i: (i, 0)),
                  pl.BlockSpec((te, D), lambda i: (i, CB_ZA)), pl.BlockSpec((1, DK), lambda i: (0, 0)),
                  pl.BlockSpec(memory_space=pl.ANY)],
        out_specs=(pl.BlockSpec((te, D), lambda i: (i, 0)), pl.BlockSpec((te, D), lambda i: (i, CB_ZA)),
                   pl.BlockSpec((1, DK), lambda i: (0, 0))),
        out_shape=(jax.ShapeDtypeStruct((lp, D), F32), jax.ShapeDtypeStruct(dproj.shape, dproj.dtype),
                   jax.ShapeDtypeStruct((1, DK), F32)),
        input_output_aliases={4: 1},
        compiler_params=_cp(("arbitrary",)),
    )(do_n, o, proj, dn_w, dproj)


def _qkv_conv_bwd(proj, dqkv, conv_w, dproj):
    lp = proj.shape[0]
    te = _pick(lp, (320,))
    hb = te // HALO_Q
    nt = lp // te
    last_hb = lp // HALO_Q - 1

    def body(main_ref, prev_ref, next_ref, dmain_ref, dnext_ref, cw_ref, _, dpre_ref, dcw_ref):
        s, i = pl.program_id(0), pl.program_id(1)

        @pl.when(i == 0)
        def _():
            dcw_ref[...] = jnp.zeros_like(dcw_ref)

        prev = jnp.where(i > 0, prev_ref[...], 0.0)
        nxt = jnp.where(i < nt - 1, next_ref[...], 0.0)
        dnxt = jnp.where(i < nt - 1, dnext_ref[...], 0.0)
        pre = jnp.concatenate([prev, main_ref[...], nxt], axis=0)
        dn = jnp.concatenate([dmain_ref[...], dnxt], axis=0)
        cw = cw_ref[...]
        ne = te + HALO_Q
        co = cw[0:1, :] * pre[HALO_Q - 3:HALO_Q - 3 + ne, :]
        for j in range(1, KQ):
            co = co + cw[j:j + 1, :] * pre[HALO_Q - 3 + j:HALO_Q - 3 + j + ne, :]
        sg = _sig(co)
        a = co * sg
        scale = jnp.where(s == 0, DK ** -0.5, 1.0)
        das = []
        for h in range(H):
            sl = slice(h * DK, (h + 1) * DK)
            blk, g = a[:, sl], dn[:, sl]
            r = lax.rsqrt(_rowsum(blk * blk) + EPS)
            yhat = blk * r
            das.append(scale * r * (g - yhat * _rowsum(g * yhat)))
        da = jnp.where(s == 2, dn, jnp.concatenate(das, axis=1))
        dco = da * _dsilu(co, sg)
        dpre = cw[0:1, :] * dco[3:3 + te, :]
        for j in range(1, KQ):
            dpre = dpre + cw[j:j + 1, :] * dco[3 - j:3 - j + te, :]
        dpre_ref[...] = dpre.astype(BF16)
        dcw_ref[...] += jnp.concatenate(
            [_colsum(dco[:te, :] * pre[HALO_Q - 3 + j:HALO_Q - 3 + j + te, :]) for j in range(KQ)], axis=0)

    return _call(
        body, name="qkv_conv_bwd", grid=(3, nt),
        in_specs=[pl.BlockSpec((te, D), lambda s, i: (i, s)),
                  pl.BlockSpec((HALO_Q, D), lambda s, i: (jnp.maximum(i * hb - 1, 0), s)),
                  pl.BlockSpec((HALO_Q, D), lambda s, i: (jnp.minimum((i + 1) * hb, last_hb), s)),
                  pl.BlockSpec((te, D), lambda s, i: (i, s)),
                  pl.BlockSpec((HALO_Q, D), lambda s, i: (jnp.minimum((i + 1) * hb, last_hb), s)),
                  pl.BlockSpec((KQ, D), lambda s, i: (0, s)),
                  pl.BlockSpec(memory_space=pl.ANY)],
        out_specs=(pl.BlockSpec((te, D), lambda s, i: (i, s)), pl.BlockSpec((KQ, D), lambda s, i: (0, s))),
        out_shape=(jax.ShapeDtypeStruct(dproj.shape, dproj.dtype), jax.ShapeDtypeStruct((KQ, 3 * D), F32)),
        input_output_aliases={6: 0},
        compiler_params=_cp(("arbitrary", "arbitrary")),
    )(proj, proj, proj, dqkv, dqkv, conv_w, dproj)


def _ba_bwd(dbg, ba, ab, pad):
    lp = ba.shape[0]
    te = _pick(lp, (640, 320))

    def body(dbg_ref, ba_ref, ab_ref, dba_ref, dab_ref):
        i = pl.program_id(0)

        @pl.when(i == 0)
        def _():
            dab_ref[...] = jnp.zeros_like(dab_ref)

        ba, ab = ba_ref[...], ab_ref[...]
        lane = lax.broadcasted_iota(jnp.int32, ba.shape, 1)
        rows = i * te + lax.broadcasted_iota(jnp.int32, ba.shape, 0)
        g = jnp.where((lane < 2 * H) & (rows >= pad), dbg_ref[...], 0.0)
        sb = _sig(ba)
        z = ba + ab[1:2, :]
        sp = jnp.maximum(z, 0.0) + jnp.log(1.0 + jnp.exp(-jnp.abs(z)))
        nea = -jnp.exp(ab[0:1, :])
        dz = g * nea * _sig(z)
        dba_ref[...] = jnp.where(lane < H, g * sb * (1.0 - sb), dz).astype(BF16)
        is_g = (lane >= H) & (lane < 2 * H)
        dab_ref[...] += jnp.concatenate([_colsum(jnp.where(is_g, g * nea * sp, 0.0)),
                                         _colsum(jnp.where(is_g, dz, 0.0))], axis=0)

    return _call(
        body, name="ba_bwd", grid=(lp // te,),
        in_specs=[pl.BlockSpec((te, LANE), lambda i: (i, 0)), pl.BlockSpec((te, LANE), lambda i: (i, 0)),
                  pl.BlockSpec((2, LANE), lambda i: (0, 0))],
        out_specs=(pl.BlockSpec((te, LANE), lambda i: (i, 0)), pl.BlockSpec((2, LANE), lambda i: (0, 0))),
        out_shape=(jax.ShapeDtypeStruct((lp, LANE), BF16), jax.ShapeDtypeStruct((2, LANE), F32)),
        compiler_params=_cp(("arbitrary",)),
    )(dbg, ba, ab)


def _conv_b_fwd(proj, dw_w, dw_b, ln_w, ln_b):
    lp = proj.shape[0]
    te = _pick(lp, (320,))
    hb = te // HALO_D

    def body(a_ref, b_ref, ha_ref, hb_ref, zb_ref, w_ref, wb_ref, lw_ref, lb_ref, c1_ref, c3_ref, c0_scr):
        i = pl.program_id(0)
        c0_scr[:HALO_D, :] = jnp.where(i > 0, ha_ref[...] * _sig(hb_ref[...]), 0.0)
        c0_scr[HALO_D:, :] = a_ref[...] * _sig(b_ref[...])
        off = HALO_D - (KD - 1)
        for cb in range(D // LANE):
            cs = slice(cb * LANE, (cb + 1) * LANE)
            acc = w_ref[0:1, cs] * c0_scr[off:off + te, cs]
            for j in range(1, KD):
                acc = acc + w_ref[j:j + 1, cs] * c0_scr[off + j:off + j + te, cs]
            c1_ref[:, cs] = acc + wb_ref[:, cs]
        c1 = c1_ref[...]
        mu = jnp.mean(c1, axis=-1, keepdims=True)
        xc = c1 - mu
        c2 = xc * lax.rsqrt(jnp.mean(xc * xc, axis=-1, keepdims=True) + EPS) * lw_ref[...] + lb_ref[...]
        zb = zb_ref[...]
        c3_ref[...] = (c2 * _sig(c2) * zb * _sig(zb)).astype(BF16)

    vec = pl.BlockSpec((1, D), lambda i: (0, 0))
    return _call(
        body, name="conv_b_fwd", grid=(lp // te,),
        in_specs=[pl.BlockSpec((te, D), lambda i: (i, CB_GA_)), pl.BlockSpec((te, D), lambda i: (i, CB_GB_)),
                  pl.BlockSpec((HALO_D, D), lambda i: (jnp.maximum(i * hb - 1, 0), CB_GA_)),
                  pl.BlockSpec((HALO_D, D), lambda i: (jnp.maximum(i * hb - 1, 0), CB_GB_)),
                  pl.BlockSpec((te, D), lambda i: (i, CB_ZB)),
                  pl.BlockSpec((KD, D), lambda i: (0, 0)), vec, vec, vec],
        out_specs=(pl.BlockSpec((te, D), lambda i: (i, 0)), pl.BlockSpec((te, D), lambda i: (i, 0))),
        out_shape=(jax.ShapeDtypeStruct((lp, D), F32), jax.ShapeDtypeStruct((lp, D), BF16)),
        scratch_shapes=[pltpu.VMEM((te + HALO_D, D), F32)],
        compiler_params=_cp(("parallel",)),
    )(proj, proj, proj, proj, proj, dw_w, dw_b, ln_w, ln_b)


def _conv_b_bwd1(dc3, c1, proj, ln_w, ln_b, dproj):
    lp = c1.shape[0]
    te = _pick(lp, (640, 320))

    def body(dc3_ref, c1_ref, zb_ref, lw_ref, lb_ref, _, dc1_ref, dzb_ref, sums_ref):
        @pl.when(pl.program_id(0) == 0)
        def _():
            sums_ref[...] = jnp.zeros_like(sums_ref)

        c1, g = c1_ref[...], dc3_ref[...]
        mu = jnp.mean(c1, axis=-1, keepdims=True)
        xc = c1 - mu
        rstd = lax.rsqrt(jnp.mean(xc * xc, axis=-1, keepdims=True) + EPS)
        xh = xc * rstd
        lw = lw_ref[...]
        c2 = xh * lw + lb_ref[...]
        s2 = _sig(c2)
        zb = zb_ref[...]
        sz = _sig(zb)
        dc2 = g * (zb * sz) * _dsilu(c2, s2)
        dzb_ref[...] = (g * (c2 * s2) * _dsilu(zb, sz)).astype(BF16)
        dxh = dc2 * lw
        dc1 = rstd * (dxh - jnp.mean(dxh, axis=-1, keepdims=True) - xh * jnp.mean(dxh * xh, axis=-1, keepdims=True))
        dc1_ref[...] = dc1
        sums_ref[...] += jnp.concatenate([_colsum(dc2 * xh), _colsum(dc2), _colsum(dc1)], axis=0)

    vec = pl.BlockSpec((1, D), lambda i: (0, 0))
    return _call(
        body, name="conv_b_bwd1", grid=(lp // te,),
        in_specs=[pl.BlockSpec((te, D), lambda i: (i, 0)), pl.BlockSpec((te, D), lambda i: (i, 0)),
                  pl.BlockSpec((te, D), lambda i: (i, CB_ZB)), vec, vec, pl.BlockSpec(memory_space=pl.ANY)],
        out_specs=(pl.BlockSpec((te, D), lambda i: (i, 0)), pl.BlockSpec((te, D), lambda i: (i, CB_ZB)),
                   pl.BlockSpec((3, D), lambda i: (0, 0))),
        out_shape=(jax.ShapeDtypeStruct((lp, D), F32), jax.ShapeDtypeStruct(dproj.shape, dproj.dtype),
                   jax.ShapeDtypeStruct((3, D), F32)),
        input_output_aliases={5: 1},
        compiler_params=_cp(("arbitrary",)),
    )(dc3, c1, proj, ln_w, ln_b, dproj)


def _conv_b_bwd2(dc1, proj, dw_w, dproj):
    lp = dc1.shape[0]
    te = _pick(lp, (320,))
    hb = te // HALO_D
    nt = lp // te
    last_hb = lp // HALO_D - 1

    def body(g_ref, gn_ref, a_ref, b_ref, ha_ref, hb_ref, w_ref, _, dab_ref, dw_ref, c0_scr, g_scr, dc0_scr):
        i = pl.program_id(0)

        @pl.when(i == 0)
        def _():
            dw_ref[...] = jnp.zeros_like(dw_ref)

        a, b = a_ref[...], b_ref[...]
        sb = _sig(b)
        c0_scr[:HALO_D, :] = jnp.where(i > 0, ha_ref[...] * _sig(hb_ref[...]), 0.0)
        c0_scr[HALO_D:, :] = a * sb
        g_scr[:te, :] = g_ref[...]
        g_scr[te:, :] = jnp.where(i < nt - 1, gn_ref[...], 0.0)
        off = HALO_D - (KD - 1)
        for cb in range(D // LANE):
            cs = slice(cb * LANE, (cb + 1) * LANE)
            g = g_scr[:te, cs]
            acc = w_ref[KD - 1:KD, cs] * g
            rows = [_colsum(g * c0_scr[off + KD - 1:off + KD - 1 + te, cs])]
            for j in range(KD - 2, -1, -1):
                acc = acc + w_ref[j:j + 1, cs] * g_scr[KD - 1 - j:KD - 1 - j + te, cs]
                rows.append(_colsum(g * c0_scr[off + j:off + j + te, cs]))
            dc0_scr[:, cs] = acc
            dw_ref[:, cs] += jnp.concatenate(rows[::-1], axis=0)
        dc0 = dc0_scr[...]
        dab_ref[:, :D] = (dc0 * sb).astype(BF16)
        dab_ref[:, D:] = (dc0 * a * sb * (1.0 - sb)).astype(BF16)

    return _call(
        body, name="conv_b_bwd2", grid=(nt,),
        in_specs=[pl.BlockSpec((te, D), lambda i: (i, 0)),
                  pl.BlockSpec((HALO_D, D), lambda i: (jnp.minimum((i + 1) * hb, last_hb), 0)),
                  pl.BlockSpec((te, D), lambda i: (i, CB_GA_)), pl.BlockSpec((te, D), lambda i: (i, CB_GB_)),
                  pl.BlockSpec((HALO_D, D), lambda i: (jnp.maximum(i * hb - 1, 0), CB_GA_)),
                  pl.BlockSpec((HALO_D, D), lambda i: (jnp.maximum(i * hb - 1, 0), CB_GB_)),
                  pl.BlockSpec((KD, D), lambda i: (0, 0)), pl.BlockSpec(memory_space=pl.ANY)],
        out_specs=(pl.BlockSpec((te, 2 * D), lambda i: (i, CB_GA_ // 2)), pl.BlockSpec((KD, D), lambda i: (0, 0))),
        out_shape=(jax.ShapeDtypeStruct(dproj.shape, dproj.dtype), jax.ShapeDtypeStruct((KD, D), F32)),
        input_output_aliases={7: 0},
        scratch_shapes=[pltpu.VMEM((te + HALO_D, D), F32), pltpu.VMEM((te + HALO_D, D), F32), pltpu.VMEM((te, D), F32)],
        compiler_params=_cp(("arbitrary",)),
    )(dc1, dc1, proj, proj, proj, proj, dw_w, dproj)


def _merge_fwd(y_a, y_b, proj, b_cf):
    lp = y_a.shape[0]
    te = _pick(lp, (640, 320))

    def body(ya_ref, yb_ref, ga_ref, gb_ref, bias_ref, out_ref):
        out_ref[...] = (_sig(ga_ref[...]) * ya_ref[...] + _sig(gb_ref[...]) * (yb_ref[...] + bias_ref[...])).astype(BF16)

    row = lambda j: pl.BlockSpec((te, D), lambda i: (i, j))
    return _call(
        body, name="merge_fwd", grid=(lp // te,),
        in_specs=[row(0), row(0), row(CB_MA), row(CB_MB), pl.BlockSpec((1, D), lambda i: (0, 0))],
        out_specs=row(0), out_shape=jax.ShapeDtypeStruct((lp, D), BF16),
        compiler_params=_cp(("parallel",)),
    )(y_a, y_b, proj, proj, b_cf)


def _merge_bwd(dmerged, y_a, y_b, proj, b_cf):
    lp = y_a.shape[0]
    te = _pick(lp, (640, 320))

    def body(dm_ref, ya_ref, yb_ref, ga_ref, gb_ref, bias_ref, dya_ref, dyb_ref, dg_ref, db_ref):
        @pl.when(pl.program_id(0) == 0)
        def _():
            db_ref[...] = jnp.zeros_like(db_ref)

        dm = dm_ref[...]
        sa, sb = _sig(ga_ref[...]), _sig(gb_ref[...])
        dyb = sb * dm
        dya_ref[...] = (sa * dm).astype(BF16)
        dyb_ref[...] = dyb.astype(BF16)
        dg_ref[:, :D] = (dm * ya_ref[...] * sa * (1.0 - sa)).astype(BF16)
        dg_ref[:, D:] = (dm * (yb_ref[...] + bias_ref[...]) * sb * (1.0 - sb)).astype(BF16)
        db_ref[...] += _colsum(dyb)

    row = lambda j: pl.BlockSpec((te, D), lambda i: (i, j))
    act = jax.ShapeDtypeStruct((lp, D), BF16)
    return _call(
        body, name="merge_bwd", grid=(lp // te,),
        in_specs=[row(0), row(0), row(0), row(CB_MA), row(CB_MB), pl.BlockSpec((1, D), lambda i: (0, 0))],
        out_specs=(row(0), row(0), pl.BlockSpec((te, 2 * D), lambda i: (i, CB_MA // 2)),
                   pl.BlockSpec((1, D), lambda i: (0, 0))),
        out_shape=(act, act, jax.ShapeDtypeStruct((lp, NCB * D), BF16), jax.ShapeDtypeStruct((1, D), F32)),
        compiler_params=_cp(("arbitrary",)),
    )(dmerged, y_a, y_b, proj, proj, b_cf)


def _final_fwd_bwd(x_ext, z, target, final_w):
    lp = x_ext.shape[0]
    te = LANE

    def body(x_ref, z_ref, t_ref, w_ref, dx_ref, loss_ref, dw_ref):
        i = pl.program_id(0)

        @pl.when(i == 0)
        def _():
            loss_ref[...] = jnp.zeros_like(loss_ref)
            dw_ref[...] = jnp.zeros_like(dw_ref)

        xo = x_ref[...] + z_ref[...]
        r = lax.rsqrt(jnp.mean(xo * xo, axis=-1, keepdims=True) + EPS)
        xhat = xo * r
        w = w_ref[...]
        err = jnp.where(i > 0, xhat * w - t_ref[...], 0.0)
        loss_ref[...] += 0.5 * jnp.sum(jnp.mean(err * err, axis=-1, keepdims=True), keepdims=True)
        dy = err * (1.0 / D)
        dw_ref[...] += _colsum(dy * xhat)
        dxn = dy * w
        dx_ref[...] = r * (dxn - xhat * jnp.mean(dxn * xhat, axis=-1, keepdims=True))

    return _call(
        body, name="final_fwd_bwd", grid=(lp // te,),
        in_specs=[pl.BlockSpec((te, D), lambda i: (i, 0)), pl.BlockSpec((te, D), lambda i: (i, 0)),
                  pl.BlockSpec((te, D), lambda i: (jnp.maximum(i - 1, 0), 0)), pl.BlockSpec((1, D), lambda i: (0, 0))],
        out_specs=(pl.BlockSpec((te, D), lambda i: (i, 0)), pl.BlockSpec((1, 1), lambda i: (0, 0)),
                   pl.BlockSpec((1, D), lambda i: (0, 0))),
        out_shape=(jax.ShapeDtypeStruct((lp, D), F32), jax.ShapeDtypeStruct((1, 1), F32),
                   jax.ShapeDtypeStruct((1, D), F32)),
        compiler_params=_cp(("arbitrary",)),
    )(x_ext, z, target, final_w)


def _prenorm_bwd(dh, x_ext, dx_out, norm_w, seq):
    lp = x_ext.shape[0]
    te = LANE

    def body(dh_ref, x_ref, dxo_ref, w_ref, gx_ref, head_ref, dw_ref):
        i = pl.program_id(0)

        @pl.when(i == 0)
        def _():
            dw_ref[...] = jnp.zeros_like(dw_ref)

        x, dh = x_ref[...], dh_ref[...]
        r = lax.rsqrt(jnp.mean(x * x, axis=-1, keepdims=True) + EPS)
        xhat = x * r
        dxn = dh * w_ref[...]
        dx = dxo_ref[...] + r * (dxn - xhat * jnp.mean(dxn * xhat, axis=-1, keepdims=True))
        dw_ref[...] += _colsum(dh * xhat)

        @pl.when(i == 0)
        def _():
            head_ref[...] = dx

        @pl.when(i > 0)
        def _():
            gx_ref[...] = dx

    row = pl.BlockSpec((te, D), lambda i: (i, 0))
    return _call(
        body, name="prenorm_bwd", grid=(lp // te,),
        in_specs=[row, row, row, pl.BlockSpec((1, D), lambda i: (0, 0))],
        out_specs=(pl.BlockSpec((te, D), lambda i: (jnp.maximum(i - 1, 0), 0)), pl.BlockSpec((te, D), lambda i: (0, 0)),
                   pl.BlockSpec((1, D), lambda i: (0, 0))),
        out_shape=(jax.ShapeDtypeStruct((seq, D), F32), jax.ShapeDtypeStruct((te, D), F32),
                   jax.ShapeDtypeStruct((1, D), F32)),
        compiler_params=_cp(("arbitrary",)),
    )(dh, x_ext, dx_out, norm_w)


def _adam_reduce(parts, w, m, v, name):
    r, n = w.shape
    tr = _pick(r, (128,)) if r % 128 == 0 else r
    c1 = 1.0 - ADAM_B1 ** ADAM_STEP
    c2 = 1.0 - ADAM_B2 ** ADAM_STEP

    def body(p_ref, w_ref, m_ref, v_ref, g_ref, d_ref, m2_ref, v2_ref):
        g = p_ref[0]
        for s in range(1, NDEV):
            g = g + p_ref[s]
        m2 = ADAM_B1 * m_ref[...] + (1.0 - ADAM_B1) * g
        v2 = ADAM_B2 * v_ref[...] + (1.0 - ADAM_B2) * (g * g)
        g_ref[...] = g
        m2_ref[...] = m2
        v2_ref[...] = v2
        d_ref[...] = -ADAM_LR * ((m2 / c1) / (jnp.sqrt(v2 / c2) + ADAM_EPS) + ADAM_WD * w_ref[...])

    blk = pl.BlockSpec((tr, n), lambda i: (i, 0))
    out = jax.ShapeDtypeStruct((r, n), F32)
    return _call(
        body, name=name, grid=(r // tr,),
        in_specs=[pl.BlockSpec((NDEV, tr, n), lambda i: (0, i, 0)), blk, blk, blk],
        out_specs=(blk, blk, blk, blk), out_shape=(out, out, out, out),
        compiler_params=_cp(("parallel",)),
    )(parts, w, m, v)


SMALL = ("norm_w", "a_log", "dt_bias", "dn_norm_w", "dw_b", "ln_w", "ln_b", "b_cf_out", "final_norm_w")


def kernel(x, meta, norm_w, w_in, conv_qkv_w, a_log, dt_bias, dn_norm_w, w_dn_out, dw_w, dw_b, ln_w, ln_b, w_cf_out, b_cf_out, w_o, final_norm_w, loss_target, m_meta, m_norm_w, m_w_in, m_conv_qkv_w, m_a_log, m_dt_bias, m_dn_norm_w, m_w_dn_out, m_dw_w, m_dw_b, m_ln_w, m_ln_b, m_w_cf_out, m_b_cf_out, m_w_o, m_final_norm_w, v_meta, v_norm_w, v_w_in, v_conv_qkv_w, v_a_log, v_dt_bias, v_dn_norm_w, v_w_dn_out, v_dw_w, v_dw_b, v_ln_w, v_ln_b, v_w_cf_out, v_b_cf_out, v_w_o, v_final_norm_w):
    seq = x.shape[1]
    pad = (-(seq + NMETA)) % LANE
    in_w = w_in.shape[2] * NDEV
    n_qkvz = 4 * D
    n_ba = 2 * H

    w_in_g, w_dn_g, w_cf_g, w_o_g, meta_g, cqw_g, dww_g = _exchange(
        [w_in[0].astype(BF16), w_dn_out[0].astype(BF16), w_cf_out[0].astype(BF16), w_o[0].astype(BF16),
         meta, conv_qkv_w[0], dw_w[0]], [False] * 7, "gather_weights")
    w_full = jnp.transpose(w_in_g, (1, 0, 2)).reshape(D, in_w)
    c_glu = n_qkvz + n_ba
    c_zb, c_mg = c_glu + 2 * D, c_glu + 3 * D
    w_main = jnp.concatenate([w_full[:, :n_qkvz], w_full[:, c_glu:c_zb], w_full[:, c_mg:], w_full[:, c_zb:c_mg]],
                             axis=1)
    w_ba = jnp.pad(w_full[:, n_qkvz:n_qkvz + n_ba], ((0, 0), (0, LANE - n_ba)))
    w_dn, w_cf, w_oo = (t.reshape(D, D) for t in (w_dn_g, w_cf_g, w_o_g))
    meta_full = jnp.transpose(meta_g, (1, 0, 2)).reshape(NMETA, D)
    cqw = jnp.transpose(cqw_g, (1, 0, 2)).reshape(KQ, 3 * D)
    dww = jnp.transpose(dww_g, (1, 0, 2)).reshape(KD, D)
    ab = jnp.pad(jnp.concatenate([a_log, dt_bias], axis=0), ((0, 0), (H, LANE - 2 * H)))

    x_ext = jnp.concatenate([jnp.zeros((pad, D), F32), meta_full, x[0]], axis=0)

    proj, ba, h = _proj_fwd(x_ext, norm_w, w_main, w_ba)
    qkv, bg = _qkv_conv_fwd(proj, ba, cqw, ab, pad)
    o, sall, tall = _delta_fwd(qkv, bg)
    o_n = _o_post_fwd(o, proj, dn_norm_w)
    y_a = _mm(o_n, w_dn, "y_a_mm")
    c1, c3 = _conv_b_fwd(proj, dww, dw_b, ln_w, ln_b)
    y_b = _mm(c3, w_cf, "y_b_mm")
    merged = _merge_fwd(y_a, y_b, proj, b_cf_out)
    z = _mm(merged, w_oo, "z_mm")
    dx_out, loss_part, g_final_w = _final_fwd_bwd(x_ext, z, loss_target[0], final_norm_w.reshape(1, D))

    dx_out_b = dx_out.astype(BF16)
    dmerged = _mm(dx_out_b, w_oo.T, "dmerged_mm")
    g_w_o = _mm_tn(merged, dx_out_b, "g_w_o_mm")
    dy_a, dy_b, dproj, g_b_cf = _merge_bwd(dmerged, y_a, y_b, proj, b_cf_out)
    dc3 = _mm(dy_b, w_cf.T, "dc3_mm")
    g_w_cf = _mm_tn(c3, dy_b, "g_w_cf_mm")
    do_n = _mm(dy_a, w_dn.T, "do_n_mm")
    g_w_dn = _mm_tn(o_n, dy_a, "g_w_dn_mm")
    dc1, dproj, sums_b = _conv_b_bwd1(dc3, c1, proj, ln_w, ln_b, dproj)
    dproj, g_dw_w = _conv_b_bwd2(dc1, proj, dww, dproj)
    do, dproj, g_dn_w = _o_post_bwd(do_n, o, proj, dn_norm_w, dproj)
    dqkv, dbg = _delta_bwd(qkv, bg, sall, tall, do)
    dproj, g_cqw = _qkv_conv_bwd(proj, dqkv, cqw, dproj)
    dba, dab = _ba_bwd(dbg, ba, ab, pad)
    dh = _dh_mm(dproj, dba, w_main.T, w_ba.T)
    g_w_main = _mm_tn(h, dproj, "g_w_main_mm")
    g_w_ba = _mm_tn(h, dba, "g_w_ba_mm")
    grad_x, dhead, g_norm_w = _prenorm_bwd(dh, x_ext, dx_out, norm_w, seq)

    g_w_full = jnp.concatenate([g_w_main[:, :n_qkvz], g_w_ba[:, :n_ba], g_w_main[:, CB_GA_ * D:CB_MA * D],
                                g_w_main[:, CB_ZB * D:], g_w_main[:, CB_MA * D:CB_ZB * D]], axis=1)
    split_cols = lambda t: jnp.transpose(t.reshape(t.shape[0], NDEV, t.shape[1] // NDEV), (1, 0, 2))
    small = {"norm_w": g_norm_w, "a_log": dab[0:1, H:2 * H], "dt_bias": dab[1:2, H:2 * H], "dn_norm_w": g_dn_w,
             "dw_b": sums_b[2:3], "ln_w": sums_b[0:1], "ln_b": sums_b[1:2], "b_cf_out": g_b_cf,
             "final_norm_w": g_final_w}
    small_vec = jnp.concatenate([small[k] for k in SMALL], axis=1)
    ns = small_vec.shape[1]
    ns_pad = (-ns) % LANE
    small_vec = jnp.pad(small_vec, ((0, 0), (0, ns_pad)))
    p_w_in, p_w_dn, p_w_cf, p_w_o, p_meta, p_cqw, p_dww, p_small = _exchange(
        [split_cols(g_w_full), g_w_dn.reshape(NDEV, D // NDEV, D), g_w_cf.reshape(NDEV, D // NDEV, D),
         g_w_o.reshape(NDEV, D // NDEV, D), split_cols(dhead[pad:pad + NMETA]), split_cols(g_cqw), split_cols(g_dw_w),
         small_vec], [True] * 7 + [False], "exchange_grads")

    res = {}
    res["w_in"] = _adam_reduce(p_w_in, w_in[0], m_w_in[0], v_w_in[0], "adam_w_in")
    res["w_dn_out"] = _adam_reduce(p_w_dn, w_dn_out[0], m_w_dn_out[0], v_w_dn_out[0], "adam_w_dn")
    res["w_cf_out"] = _adam_reduce(p_w_cf, w_cf_out[0], m_w_cf_out[0], v_w_cf_out[0], "adam_w_cf")
    res["w_o"] = _adam_reduce(p_w_o, w_o[0], m_w_o[0], v_w_o[0], "adam_w_o")
    res["meta"] = _adam_reduce(p_meta, meta, m_meta, v_meta, "adam_meta")
    res["conv_qkv_w"] = _adam_reduce(p_cqw, conv_qkv_w[0], m_conv_qkv_w[0], v_conv_qkv_w[0], "adam_conv_qkv_w")
    res["dw_w"] = _adam_reduce(p_dww, dw_w[0], m_dw_w[0], v_dw_w[0], "adam_dw_w")
    loc = dict(norm_w=(norm_w, m_norm_w, v_norm_w), a_log=(a_log, m_a_log, v_a_log), dt_bias=(dt_bias, m_dt_bias, v_dt_bias),
               dn_norm_w=(dn_norm_w, m_dn_norm_w, v_dn_norm_w), dw_b=(dw_b, m_dw_b, v_dw_b), ln_w=(ln_w, m_ln_w, v_ln_w),
               ln_b=(ln_b, m_ln_b, v_ln_b), b_cf_out=(b_cf_out, m_b_cf_out, v_b_cf_out),
               final_norm_w=(final_norm_w, m_final_norm_w, v_final_norm_w))
    cat = lambda j: jnp.pad(jnp.concatenate([loc[k][j].reshape(1, -1) for k in SMALL], axis=1), ((0, 0), (0, ns_pad)))
    small_res = _adam_reduce(p_small, cat(0), cat(1), cat(2), "adam_small")
    off = 0
    for k in SMALL:
        wshape = loc[k][0].shape
        nk = loc[k][0].size
        res[k] = tuple(t[:, off:off + nk].reshape(wshape) for t in small_res)
        off += nk
    shaped = dict(w_in=w_in.shape, w_dn_out=w_dn_out.shape, w_cf_out=w_cf_out.shape, w_o=w_o.shape, meta=meta.shape,
                  conv_qkv_w=conv_qkv_w.shape, dw_w=dw_w.shape)
    for k, shp in shaped.items():
        res[k] = tuple(t.reshape(shp) for t in res[k])

    loss = lax.psum(loss_part[0, 0], ("x", "y", "c"))
    order = ("meta", "norm_w", "w_in", "conv_qkv_w", "a_log", "dt_bias", "dn_norm_w", "w_dn_out", "dw_w", "dw_b", "ln_w",
             "ln_b", "w_cf_out", "b_cf_out", "w_o", "final_norm_w")
    outs = [loss, grad_x[None]]
    for j in range(4):
        outs += [res[k][j] for k in order]
    return tuple(outs)
```

```python
import functools

import jax
import jax.numpy as jnp
from jax import lax
from jax.experimental import pallas as pl
from jax.experimental.pallas import tpu as pltpu

F32 = jnp.float32
BF16 = jnp.bfloat16
HI = lax.Precision.HIGHEST

D = 1024
H = 8
DK = 128
C = 64
NMETA = 16
KQ = 4
KD = 31
HALO_Q = 8
HALO_D = 32
EPS = 1e-6
NDEV = 8
LANE = 128
MIB = 1024 * 1024

ADAM_LR, ADAM_B1, ADAM_B2, ADAM_EPS, ADAM_WD, ADAM_STEP = 0.001, 0.9, 0.999, 1e-08, 0.01, 10

CB_Q, CB_K, CB_V, CB_ZA, CB_GA_, CB_GB_, CB_MA, CB_MB, CB_ZB = range(9)
NCB = 9


def _pick(n, cands):
    for c in cands:
        if n % c == 0:
            return c
    raise ValueError(f"no tile for {n}")


def _cp(sem=None, vmem_mib=40):
    kw = dict(vmem_limit_bytes=vmem_mib * MIB)
    if sem is not None:
        kw["dimension_semantics"] = sem
    return pltpu.CompilerParams(**kw)


def _call(body, **kw):
    return pl.pallas_call(body, **kw)


def _dot(a, b):
    return jnp.dot(a.astype(BF16), b.astype(BF16), preferred_element_type=F32)


def _dot_nt(a, b):
    return lax.dot_general(a.astype(BF16), b.astype(BF16), (((1,), (1,)), ((), ())), preferred_element_type=F32)


def _dot_tn(a, b):
    return lax.dot_general(a.astype(BF16), b.astype(BF16), (((0,), (0,)), ((), ())), preferred_element_type=F32)


def _dot_hi(a, b):
    return jnp.dot(a, b, precision=HI, preferred_element_type=F32)


def _dot_nt_hi(a, b):
    return lax.dot_general(a, b, (((1,), (1,)), ((), ())), precision=HI, preferred_element_type=F32)


def _sig(x):
    return 0.5 * jnp.tanh(0.5 * x) + 0.5


def _dsilu(x, s):
    return s * (1.0 + x * (1.0 - s))


def _rowsum(x):
    return jnp.sum(x, axis=-1, keepdims=True)


def _colsum(x):
    return jnp.sum(x, axis=0, keepdims=True)


def _exchange(arrs, scatter, name):
    n = len(arrs)
    out_shape = []
    for a, sc in zip(arrs, scatter):
        shp = a.shape if sc else (NDEV,) + a.shape
        out_shape.append(jax.ShapeDtypeStruct(shp, a.dtype))

    def body(*refs):
        ins, outs = refs[:n], refs[n:2 * n]
        send_sems, recv_sems, loc_sems = refs[2 * n:]
        x, y, c = lax.axis_index("x"), lax.axis_index("y"), lax.axis_index("c")
        me = 4 * x + 2 * y + c
        copies = []
        for a in range(n):
            for k in range(1, NDEV):
                px = 1 - x if (k >> 2) & 1 else x
                py = 1 - y if (k >> 1) & 1 else y
                pc = 1 - c if k & 1 else c
                src = ins[a].at[4 * px + 2 * py + pc] if scatter[a] else ins[a]
                cp = pltpu.make_async_remote_copy(
                    src_ref=src, dst_ref=outs[a].at[me],
                    send_sem=send_sems.at[a * (NDEV - 1) + k - 1], recv_sem=recv_sems.at[a * (NDEV - 1) + k - 1],
                    device_id=(px, py, pc), device_id_type=pl.DeviceIdType.MESH)
                cp.start()
                copies.append(cp)
            loc = pltpu.make_async_copy(ins[a].at[me] if scatter[a] else ins[a], outs[a].at[me], loc_sems.at[a])
            loc.start()
            copies.append(loc)
        for cp in copies:
            cp.wait()

    any_spec = pl.BlockSpec(memory_space=pl.ANY)
    return _call(
        body, name=name, out_shape=tuple(out_shape),
        in_specs=[any_spec] * n, out_specs=tuple([any_spec] * n),
        scratch_shapes=[pltpu.SemaphoreType.DMA((n * (NDEV - 1),)), pltpu.SemaphoreType.DMA((n * (NDEV - 1),)),
                        pltpu.SemaphoreType.DMA((n,))],
    )(*arrs)


def _mm(a, b, name, out_dtype=F32):
    m, k = a.shape
    n = b.shape[1]
    tm = _pick(m, (1664, 832, 640, 320, 128))
    tn = _pick(n, (512, 256, 128))
    tk = _pick(k, (1024, 512, 128))
    nk = k // tk

    def body(a_ref, b_ref, o_ref, *acc):
        part = _dot(a_ref[...], b_ref[...])
        if nk == 1:
            o_ref[...] = part.astype(out_dtype)
        else:
            kk = pl.program_id(2)

            @pl.when(kk == 0)
            def _():
                acc[0][...] = part

            @pl.when(kk > 0)
            def _():
                acc[0][...] += part

            @pl.when(kk == nk - 1)
            def _():
                o_ref[...] = acc[0][...].astype(out_dtype)

    return _call(
        body, name=name, grid=(m // tm, n // tn, nk),
        in_specs=[pl.BlockSpec((tm, tk), lambda i, j, kk: (i, kk)), pl.BlockSpec((tk, tn), lambda i, j, kk: (kk, j))],
        out_specs=pl.BlockSpec((tm, tn), lambda i, j, kk: (i, j)),
        out_shape=jax.ShapeDtypeStruct((m, n), out_dtype),
        scratch_shapes=[pltpu.VMEM((tm, tn), F32)] if nk > 1 else [],
        compiler_params=_cp(("parallel", "parallel", "arbitrary")),
    )(a, b)


def _mm_tn(a, b, name):
    t, m = a.shape
    n = b.shape[1]
    tt = _pick(t, (640, 128))
    tm = _pick(m, (1024, 512, 128))
    tn = _pick(n, (1152, 1024, 512, 128))
    nt = t // tt

    def body(a_ref, b_ref, o_ref):
        s = pl.program_id(2)
        part = _dot_tn(a_ref[...], b_ref[...])

        @pl.when(s == 0)
        def _():
            o_ref[...] = part

        @pl.when(s > 0)
        def _():
            o_ref[...] += part

    return _call(
        body, name=name, grid=(m // tm, n // tn, nt),
        in_specs=[pl.BlockSpec((tt, tm), lambda i, j, s: (s, i)), pl.BlockSpec((tt, tn), lambda i, j, s: (s, j))],
        out_specs=pl.BlockSpec((tm, tn), lambda i, j, s: (i, j)),
        out_shape=jax.ShapeDtypeStruct((m, n), F32),
        compiler_params=_cp(("parallel", "parallel", "arbitrary")),
    )(a, b)


def _proj_fwd(x_ext, norm_w, w_main, w_ba):
    lp = x_ext.shape[0]
    n = w_main.shape[1]
    tm = _pick(lp, (832, 640, 320))
    tn = 512

    def body(x_ref, nw_ref, w_ref, wba_ref, proj_ref, ba_ref, h_ref):
        @pl.when(pl.program_id(1) == 0)
        def _():
            x = x_ref[...]
            r = lax.rsqrt(jnp.mean(x * x, axis=-1, keepdims=True) + EPS)
            h = (x * r * nw_ref[...]).astype(BF16)
            h_ref[...] = h
            ba_ref[...] = jnp.dot(h, wba_ref[...], preferred_element_type=F32)

        proj_ref[...] = jnp.dot(h_ref[...], w_ref[...], preferred_element_type=F32)

    return _call(
        body, name="proj_fwd", grid=(lp // tm, n // tn),
        in_specs=[pl.BlockSpec((tm, D), lambda i, j: (i, 0)), pl.BlockSpec((1, D), lambda i, j: (0, 0)),
                  pl.BlockSpec((D, tn), lambda i, j: (0, j)), pl.BlockSpec((D, LANE), lambda i, j: (0, 0))],
        out_specs=(pl.BlockSpec((tm, tn), lambda i, j: (i, j)), pl.BlockSpec((tm, LANE), lambda i, j: (i, 0)),
                   pl.BlockSpec((tm, D), lambda i, j: (i, 0))),
        out_shape=(jax.ShapeDtypeStruct((lp, n), F32), jax.ShapeDtypeStruct((lp, LANE), F32),
                   jax.ShapeDtypeStruct((lp, D), BF16)),
        compiler_params=_cp(("parallel", "arbitrary")),
    )(x_ext, norm_w, w_main, w_ba)


def _dh_mm(dproj, dba, w_main_t, w_ba_t):
    lp, n = dproj.shape
    tm = _pick(lp, (1664, 640, 320))
    tn = 512
    tk = 1024
    nk = n // tk

    def body(a_ref, ba_ref, b_ref, bba_ref, o_ref, acc):
        kk = pl.program_id(2)

        @pl.when(kk == 0)
        def _():
            acc[...] = jnp.dot(ba_ref[...], bba_ref[...], preferred_element_type=F32)

        acc[...] += jnp.dot(a_ref[...], b_ref[...], preferred_element_type=F32)

        @pl.when(kk == nk - 1)
        def _():
            o_ref[...] = acc[...]

    return _call(
        body, name="dh_mm", grid=(lp // tm, D // tn, nk),
        in_specs=[pl.BlockSpec((tm, tk), lambda i, j, kk: (i, kk)), pl.BlockSpec((tm, LANE), lambda i, j, kk: (i, 0)),
                  pl.BlockSpec((tk, tn), lambda i, j, kk: (kk, j)), pl.BlockSpec((LANE, tn), lambda i, j, kk: (0, j))],
        out_specs=pl.BlockSpec((tm, tn), lambda i, j, kk: (i, j)),
        out_shape=jax.ShapeDtypeStruct((lp, D), F32),
        scratch_shapes=[pltpu.VMEM((tm, tn), F32)],
        compiler_params=_cp(("parallel", "parallel", "arbitrary")),
    )(dproj, dba, w_main_t, w_ba_t)


def _l2norm_heads(a):
    outs, rs = [], []
    for h in range(H):
        blk = a[:, h * DK:(h + 1) * DK]
        r = lax.rsqrt(_rowsum(blk * blk) + EPS)
        outs.append(blk * r)
        rs.append(r)
    return jnp.concatenate(outs, axis=1), rs


def _beta_g(ba, ab, row0, pad):
    lane = lax.broadcasted_iota(jnp.int32, ba.shape, 1)
    rows = row0 + lax.broadcasted_iota(jnp.int32, ba.shape, 0)
    z = ba + ab[1:2, :]
    sp = jnp.maximum(z, 0.0) + jnp.log(1.0 + jnp.exp(-jnp.abs(z)))
    val = jnp.where(lane < H, _sig(ba), -jnp.exp(ab[0:1, :]) * sp)
    return jnp.where((lane < 2 * H) & (rows >= pad), val, 0.0)


def _qkv_conv_fwd(proj, ba, conv_w, ab, pad):
    lp = proj.shape[0]
    te = _pick(lp, (320,))
    hb = te // HALO_Q

    def body(main_ref, halo_ref, cw_ref, ba_ref, ab_ref, out_ref, bg_ref):
        i, s = pl.program_id(0), pl.program_id(1)
        halo = jnp.where(i > 0, halo_ref[...], 0.0)
        pre = jnp.concatenate([halo, main_ref[...]], axis=0)
        cw = cw_ref[...]
        co = cw[0:1, :] * pre[HALO_Q - 3:HALO_Q - 3 + te, :]
        for j in range(1, KQ):
            co = co + cw[j:j + 1, :] * pre[HALO_Q - 3 + j:HALO_Q - 3 + j + te, :]
        a = co * _sig(co)
        nrm, _ = _l2norm_heads(a)
        nrm = nrm * jnp.where(s == 0, DK ** -0.5, 1.0)
        out_ref[...] = jnp.where(s == 2, a, nrm)

        @pl.when(s == 0)
        def _():
            bg_ref[...] = _beta_g(ba_ref[...], ab_ref[...], i * te, pad)

    return _call(
        body, name="qkv_conv_fwd", grid=(lp // te, 3),
        in_specs=[pl.BlockSpec((te, D), lambda i, s: (i, s)),
                  pl.BlockSpec((HALO_Q, D), lambda i, s: (jnp.maximum(i * hb - 1, 0), s)),
                  pl.BlockSpec((KQ, D), lambda i, s: (0, s)),
                  pl.BlockSpec((te, LANE), lambda i, s: (i, 0)),
                  pl.BlockSpec((2, LANE), lambda i, s: (0, 0))],
        out_specs=(pl.BlockSpec((te, D), lambda i, s: (i, s)), pl.BlockSpec((te, LANE), lambda i, s: (i, 0))),
        out_shape=(jax.ShapeDtypeStruct((lp, 3 * D), F32), jax.ShapeDtypeStruct((lp, LANE), F32)),
        compiler_params=_cp(("parallel", "arbitrary")),
    )(proj, proj, conv_w, ba, ab)


def _tri_masks():
    row = lax.broadcasted_iota(jnp.int32, (C, C), 0)
    col = lax.broadcasted_iota(jnp.int32, (C, C), 1)
    return row, col


def _split(a):
    hi = a.astype(BF16)
    return hi, (a - hi.astype(F32)).astype(BF16)


def _dot3(a, b, dims=(((1,), (0,)), ((), ()))):
    (ah, al), (bh, bl) = a, b
    mm = lambda x, y: lax.dot_general(x, y, dims, preferred_element_type=F32)
    return mm(ah, bh) + (mm(ah, bl) + mm(al, bh))


def _tinv(ys, eye):
    ts = [eye + y for y in ys]
    sp = [_split(y) for y in ys]
    yks = [_dot3(s, s) for s in sp]
    for _ in range(4):
        sp = [_split(yk) for yk in yks]
        ts = [t + _dot3(s, _split(t)) for s, t in zip(sp, ts)]
        yks = [_dot3(s, s) for s in sp]
    return [t + _dot3(_split(yk), _split(t)) for yk, t in zip(yks, ts)]


def _chunk_common(q, k, v, bcol, gcc, gcr, incl, strict):
    dm = jnp.where(incl, jnp.exp(gcc - gcr), 0.0)
    kk = _dot_nt(k, k)
    qk = _dot_nt(q, k)
    egc = jnp.exp(gcc)
    glast = gcc[C - 1:C, :]
    eend = jnp.exp(glast - gcc)
    elast = jnp.exp(glast)
    rhs = jnp.concatenate([v * bcol, k * (bcol * egc)], axis=1)
    return dm, kk, qk, egc, eend, elast, rhs


def _delta_fwd(qkv, bg):
    lp = qkv.shape[0]
    nc = lp // C

    def body(q_ref, k_ref, v_ref, bg_ref, o_ref, sall_ref, tall_ref, s_scr):
        @pl.when(pl.program_id(0) == 0)
        def _():
            s_scr[...] = jnp.zeros_like(s_scr)

        bgt = bg_ref[...]
        row, col = _tri_masks()
        incl, strict = row >= col, row > col
        eye = (row == col).astype(F32)
        gc_all = _dot_hi(incl.astype(F32), bgt)
        gc_t = _dot_hi(bgt.T, (row <= col).astype(F32))
        heads = range(H)
        sls = [slice(h * DK, (h + 1) * DK) for h in heads]
        qs, ks, vs = ([r[:, sl] for sl in sls] for r in (q_ref, k_ref, v_ref))
        bcols = [bgt[:, h:h + 1] for h in heads]
        cm = [_chunk_common(qs[h], ks[h], vs[h], bcols[h], gc_all[:, H + h:H + h + 1], gc_t[H + h:H + h + 1, :],
                            incl, strict) for h in heads]
        dms, kks, qks, egcs, eends, elasts, rhss = zip(*cm)
        ts = _tinv([jnp.where(strict, -(bcols[h] * kks[h] * dms[h]), 0.0) for h in heads], eye)
        sols = [_dot3(_split(ts[h]), _split(rhss[h])) for h in heads]
        ss = [s_scr[h] for h in heads]
        sb = [s.astype(BF16) for s in ss]
        wvs = [sols[h][:, :DK] - _dot(sols[h][:, DK:], sb[h]) for h in heads]
        wvb = [wv.astype(BF16) for wv in wvs]
        for h in heads:
            o_ref[:, sls[h]] = _dot(qs[h] * egcs[h], sb[h]) + _dot(qks[h] * dms[h], wvb[h])
            sall_ref[0, h] = ss[h]
            tall_ref[0, h] = ts[h]
        for h in heads:
            s_scr[h] = ss[h] * elasts[h] + _dot_tn(ks[h] * eends[h], wvb[h])

    blk = lambda j: pl.BlockSpec((C, D), lambda n: (n, j))
    return _call(
        body, name="delta_fwd", grid=(nc,),
        in_specs=[blk(0), blk(1), blk(2), pl.BlockSpec((C, LANE), lambda n: (n, 0))],
        out_specs=(pl.BlockSpec((C, D), lambda n: (n, 0)),
                   pl.BlockSpec((1, H, DK, DK), lambda n: (n, 0, 0, 0)),
                   pl.BlockSpec((1, H, C, C), lambda n: (n, 0, 0, 0))),
        out_shape=(jax.ShapeDtypeStruct((lp, D), F32), jax.ShapeDtypeStruct((nc, H, DK, DK), F32),
                   jax.ShapeDtypeStruct((nc, H, C, C), F32)),
        scratch_shapes=[pltpu.VMEM((H, DK, DK), F32)],
        compiler_params=_cp(("arbitrary",)),
    )(qkv, qkv, qkv, bg)


def _delta_bwd(qkv, bg, sall, tall, do):
    lp = qkv.shape[0]
    nc = lp // C

    def body(q_ref, k_ref, v_ref, bg_ref, sall_ref, tall_ref, do_ref, dqkv_ref, dbg_ref, ds_scr):
        @pl.when(pl.program_id(0) == 0)
        def _():
            ds_scr[...] = jnp.zeros_like(ds_scr)

        bgt = bg_ref[...]
        row, col = _tri_masks()
        incl, strict = row >= col, row > col
        upper = (row <= col).astype(F32)
        gc_all = _dot_hi(incl.astype(F32), bgt)
        gc_t = _dot_hi(bgt.T, upper)
        lane = lax.broadcasted_iota(jnp.int32, (C, LANE), 1)
        lastrow = lax.broadcasted_iota(jnp.int32, (C, 1), 0) == C - 1
        heads = range(H)
        sls = [slice(h * DK, (h + 1) * DK) for h in heads]
        qs, ks, vs, dos = ([r[:, sl] for sl in sls] for r in (q_ref, k_ref, v_ref, do_ref))
        bcols = [bgt[:, h:h + 1] for h in heads]
        cm = [_chunk_common(qs[h], ks[h], vs[h], bcols[h], gc_all[:, H + h:H + h + 1], gc_t[H + h:H + h + 1, :],
                            incl, strict) for h in heads]
        dms, kks, qks, egcs, eends, elasts, rhss = zip(*cm)
        ss = [sall_ref[0, h] for h in heads]
        ts = [tall_ref[0, h] for h in heads]
        dsns = [ds_scr[h] for h in heads]
        sb = [s.astype(BF16) for s in ss]
        dsb = [d.astype(BF16) for d in dsns]
        dob = [d.astype(BF16) for d in dos]
        sols = [_dot3(_split(ts[h]), _split(rhss[h])) for h in heads]
        ws = [sol[:, DK:] for sol in sols]
        qgs = [qs[h] * egcs[h] for h in heads]
        kends = [ks[h] * eends[h] for h in heads]
        wvs = [sols[h][:, :DK] - _dot(ws[h], sb[h]) for h in heads]
        wvb = [wv.astype(BF16) for wv in wvs]
        dwvs = [_dot_tn(qks[h] * dms[h], dob[h]) + _dot(kends[h], dsb[h]) for h in heads]
        dps = [jnp.where(incl, _dot_nt(dob[h], wvb[h]), 0.0) for h in heads]
        dqgs = [_dot_nt(dob[h], sb[h]) for h in heads]
        dkends = [_dot_nt(wvb[h], dsb[h]) for h in heads]
        for h in heads:
            ds_scr[h] = _dot_tn(qgs[h], dob[h]) + elasts[h] * dsns[h] - _dot_tn(ws[h], dwvs[h])
        dglasts = [elasts[h] * jnp.sum(ss[h] * dsns[h], keepdims=True) for h in heads]
        dws = [-_dot_nt(dwvs[h], sb[h]) for h in heads]
        tts = [_split(ts[h].T) for h in heads]
        drhss = [_dot3(tts[h], _split(jnp.concatenate([dwvs[h], dws[h]], axis=1))) for h in heads]
        nt_dims = (((1,), (1,)), ((), ()))
        dns = [jnp.where(strict, -_dot3(_split(drhss[h]), _split(sols[h]), nt_dims), 0.0) for h in heads]
        dbeta_t = jnp.zeros((C, LANE), F32)
        dgc_t = jnp.zeros((C, LANE), F32)
        for h in heads:
            q, k, v, bcol, dm, kk, qk, egc, eend = qs[h], ks[h], vs[h], bcols[h], dms[h], kks[h], qks[h], egcs[h], eends[h]
            drv, drk = drhss[h][:, :DK], drhss[h][:, DK:]
            dn, dp, dqg, dkend = dns[h], dps[h], dqgs[h], dkends[h]
            rk = _rowsum(drk * k)
            dkk = dn * (bcol * dm)
            dqk = dp * dm
            e = (dn * (bcol * kk) + dp * qk) * dm
            tk = _rowsum(dkend * kends[h])
            dgc = rk * bcol * egc + _rowsum(e) - _rowsum(e.T) + _rowsum(dqg * qgs[h]) - tk
            dgc = dgc + jnp.where(lastrow, dglasts[h] + jnp.sum(tk, keepdims=True), 0.0)
            dbeta = _rowsum(drv * v) + rk * egc + _rowsum(dn * kk * dm)
            dqkv_ref[:, sls[h]] = _dot(dqk, k) + dqg * egc
            dqkv_ref[:, D + h * DK:D + (h + 1) * DK] = (drk * (bcol * egc) + _dot(dkk, k) + _dot_tn(dkk, k)
                                                       + _dot_tn(dqk, q) + dkend * eend)
            dqkv_ref[:, 2 * D + h * DK:2 * D + (h + 1) * DK] = bcol * drv
            dbeta_t = jnp.where(lane == h, dbeta, dbeta_t)
            dgc_t = jnp.where(lane == H + h, dgc, dgc_t)
        dbg_ref[...] = dbeta_t + _dot_hi(upper, dgc_t)

    rev = lambda n: nc - 1 - n
    blk = lambda j: pl.BlockSpec((C, D), lambda n: (rev(n), j))
    return _call(
        body, name="delta_bwd", grid=(nc,),
        in_specs=[blk(0), blk(1), blk(2), pl.BlockSpec((C, LANE), lambda n: (rev(n), 0)),
                  pl.BlockSpec((1, H, DK, DK), lambda n: (rev(n), 0, 0, 0)),
                  pl.BlockSpec((1, H, C, C), lambda n: (rev(n), 0, 0, 0)),
                  pl.BlockSpec((C, D), lambda n: (rev(n), 0))],
        out_specs=(pl.BlockSpec((C, 3 * D), lambda n: (rev(n), 0)), pl.BlockSpec((C, LANE), lambda n: (rev(n), 0))),
        out_shape=(jax.ShapeDtypeStruct((lp, 3 * D), F32), jax.ShapeDtypeStruct((lp, LANE), F32)),
        scratch_shapes=[pltpu.VMEM((H, DK, DK), F32)],
        compiler_params=_cp(("arbitrary",)),
    )(qkv, qkv, qkv, bg, sall, tall, do)


def _o_post_fwd(o, proj, dn_w):
    lp = o.shape[0]
    te = _pick(lp, (640, 320))

    def body(o_ref, za_ref, w_ref, out_ref):
        za = za_ref[...]
        gate = za * _sig(za)
        for h in range(H):
            sl = slice(h * DK, (h + 1) * DK)
            oh = o_ref[:, sl]
            r = lax.rsqrt(jnp.mean(oh * oh, axis=-1, keepdims=True) + EPS)
            out_ref[:, sl] = (oh * r * w_ref[...] * gate[:, sl]).astype(BF16)

    return _call(
        body, name="o_post_fwd", grid=(lp // te,),
        in_specs=[pl.BlockSpec((te, D), lambda i: (i, 0)), pl.BlockSpec((te, D), lambda i: (i, CB_ZA)),
                  pl.BlockSpec((1, DK), lambda i: (0, 0))],
        out_specs=pl.BlockSpec((te, D), lambda i: (i, 0)),
        out_shape=jax.ShapeDtypeStruct((lp, D), BF16),
        compiler_params=_cp(("parallel",)),
    )(o, proj, dn_w)


def _o_post_bwd(do_n, o, proj, dn_w, dproj):
    lp = o.shape[0]
    te = _pick(lp, (640, 320))

    def body(don_ref, o_ref, za_ref, w_ref, _, do_ref, dza_ref, dw_ref):
        @pl.when(pl.program_id(0) == 0)
        def _():
            dw_ref[...] = jnp.zeros_like(dw_ref)

        za = za_ref[...]
        sz = _sig(za)
        gate, dgate = za * sz, _dsilu(za, sz)
        w = w_ref[...]
        dw = jnp.zeros((1, DK), F32)
        for h in range(H):
            sl = slice(h * DK, (h + 1) * DK)
            oh, g = o_ref[:, sl], don_ref[:, sl]
            r = lax.rsqrt(jnp.mean(oh * oh, axis=-1, keepdims=True) + EPS)
            ohat = oh * r
            dza_ref[:, sl] = (g * ohat * w * dgate[:, sl]).astype(BF16)
            don = g * gate[:, sl]
            dw = dw + _colsum(don * ohat)
            dohat = don * w
            do_ref[:, sl] = r * (dohat - ohat * jnp.mean(dohat * ohat, axis=-1, keepdims=True))
        dw_ref[...] += dw

    return _call(
        body, name="o_post_bwd", grid=(lp // te,),
        in_specs=[pl.BlockSpec((te, D), lambda i: (i, 0)), pl.BlockSpec((te, D), lambda i: (i, 0)),
                  pl.BlockSpec((te, D), lambda i: (i, CB_ZA)), pl.BlockSpec((1, DK), lambda i: (0, 0)),
                  pl.BlockSpec(memory_space=pl.ANY)],
        out_specs=(pl.BlockSpec((te, D), lambda i: (i, 0)), pl.BlockSpec((te, D), lambda i: (i, CB_ZA)),
                   pl.BlockSpec((1, DK), lambda i: (0, 0))),
        out_shape=(jax.ShapeDtypeStruct((lp, D), F32), jax.ShapeDtypeStruct(dproj.shape, dproj.dtype),
                   jax.ShapeDtypeStruct((1, DK), F32)),
        input_output_aliases={4: 1},
        compiler_params=_cp(("arbitrary",)),
    )(do_n, o, proj, dn_w, dproj)


def _qkv_conv_bwd(proj, dqkv, conv_w, dproj):
    lp = proj.shape[0]
    te = _pick(lp, (320,))
    hb = te // HALO_Q
    nt = lp // te
    last_hb = lp // HALO_Q - 1

    def body(main_ref, prev_ref, next_ref, dmain_ref, dnext_ref, cw_ref, _, dpre_ref, dcw_ref):
        s, i = pl.program_id(0), pl.program_id(1)

        @pl.when(i == 0)
        def _():
            dcw_ref[...] = jnp.zeros_like(dcw_ref)

        prev = jnp.where(i > 0, prev_ref[...], 0.0)
        nxt = jnp.where(i < nt - 1, next_ref[...], 0.0)
        dnxt = jnp.where(i < nt - 1, dnext_ref[...], 0.0)
        pre = jnp.concatenate([prev, main_ref[...], nxt], axis=0)
        dn = jnp.concatenate([dmain_ref[...], dnxt], axis=0)
        cw = cw_ref[...]
        ne = te + HALO_Q
        co = cw[0:1, :] * pre[HALO_Q - 3:HALO_Q - 3 + ne, :]
        for j in range(1, KQ):
            co = co + cw[j:j + 1, :] * pre[HALO_Q - 3 + j:HALO_Q - 3 + j + ne, :]
        sg = _sig(co)
        a = co * sg
        scale = jnp.where(s == 0, DK ** -0.5, 1.0)
        das = []
        for h in range(H):
            sl = slice(h * DK, (h + 1) * DK)
            blk, g = a[:, sl], dn[:, sl]
            r = lax.rsqrt(_rowsum(blk * blk) + EPS)
            yhat = blk * r
            das.append(scale * r * (g - yhat * _rowsum(g * yhat)))
        da = jnp.where(s == 2, dn, jnp.concatenate(das, axis=1))
        dco = da * _dsilu(co, sg)
        dpre = cw[0:1, :] * dco[3:3 + te, :]
        for j in range(1, KQ):
            dpre = dpre + cw[j:j + 1, :] * dco[3 - j:3 - j + te, :]
        dpre_ref[...] = dpre.astype(BF16)
        dcw_ref[...] += jnp.concatenate(
            [_colsum(dco[:te, :] * pre[HALO_Q - 3 + j:HALO_Q - 3 + j + te, :]) for j in range(KQ)], axis=0)

    return _call(
        body, name="qkv_conv_bwd", grid=(3, nt),
        in_specs=[pl.BlockSpec((te, D), lambda s, i: (i, s)),
                  pl.BlockSpec((HALO_Q, D), lambda s, i: (jnp.maximum(i * hb - 1, 0), s)),
                  pl.BlockSpec((HALO_Q, D), lambda s, i: (jnp.minimum((i + 1) * hb, last_hb), s)),
                  pl.BlockSpec((te, D), lambda s, i: (i, s)),
                  pl.BlockSpec((HALO_Q, D), lambda s, i: (jnp.minimum((i + 1) * hb, last_hb), s)),
                  pl.BlockSpec((KQ, D), lambda s, i: (0, s)),
                  pl.BlockSpec(memory_space=pl.ANY)],
        out_specs=(pl.BlockSpec((te, D), lambda s, i: (i, s)), pl.BlockSpec((KQ, D), lambda s, i: (0, s))),
        out_shape=(jax.ShapeDtypeStruct(dproj.shape, dproj.dtype), jax.ShapeDtypeStruct((KQ, 3 * D), F32)),
        input_output_aliases={6: 0},
        compiler_params=_cp(("arbitrary", "arbitrary")),
    )(proj, proj, proj, dqkv, dqkv, conv_w, dproj)


def _ba_bwd(dbg, ba, ab, pad):
    lp = ba.shape[0]
    te = _pick(lp, (640, 320))

    def body(dbg_ref, ba_ref, ab_ref, dba_ref, dab_ref):
        i = pl.program_id(0)

        @pl.when(i == 0)
        def _():
            dab_ref[...] = jnp.zeros_like(dab_ref)

        ba, ab = ba_ref[...], ab_ref[...]
        lane = lax.broadcasted_iota(jnp.int32, ba.shape, 1)
        rows = i * te + lax.broadcasted_iota(jnp.int32, ba.shape, 0)
        g = jnp.where((lane < 2 * H) & (rows >= pad), dbg_ref[...], 0.0)
        sb = _sig(ba)
        z = ba + ab[1:2, :]
        sp = jnp.maximum(z, 0.0) + jnp.log(1.0 + jnp.exp(-jnp.abs(z)))
        nea = -jnp.exp(ab[0:1, :])
        dz = g * nea * _sig(z)
        dba_ref[...] = jnp.where(lane < H, g * sb * (1.0 - sb), dz).astype(BF16)
        is_g = (lane >= H) & (lane < 2 * H)
        dab_ref[...] += jnp.concatenate([_colsum(jnp.where(is_g, g * nea * sp, 0.0)),
                                         _colsum(jnp.where(is_g, dz, 0.0))], axis=0)

    return _call(
        body, name="ba_bwd", grid=(lp // te,),
        in_specs=[pl.BlockSpec((te, LANE), lambda i: (i, 0)), pl.BlockSpec((te, LANE), lambda i: (i, 0)),
                  pl.BlockSpec((2, LANE), lambda i: (0, 0))],
        out_specs=(pl.BlockSpec((te, LANE), lambda i: (i, 0)), pl.BlockSpec((2, LANE), lambda i: (0, 0))),
        out_shape=(jax.ShapeDtypeStruct((lp, LANE), BF16), jax.ShapeDtypeStruct((2, LANE), F32)),
        compiler_params=_cp(("arbitrary",)),
    )(dbg, ba, ab)


SUBLANES = 8
CONV_RB = 64


def _fill_shifted(sh_scr, src_scr, cs):
    n = sh_scr.shape[1]
    for s in range(1, SUBLANES):
        sh_scr[s] = src_scr[s:s + n, cs]


def _shifted(sh_scr, src_scr, cs, r, r0, n):
    s, a8 = r % SUBLANES, r - r % SUBLANES
    if s == 0:
        return src_scr[r0 + a8:r0 + a8 + n, cs]
    return sh_scr[s, r0 + a8:r0 + a8 + n, :]


def _conv_b_fwd(proj, dw_w, dw_b, ln_w, ln_b):
    lp = proj.shape[0]
    te = _pick(lp, (320,))
    hb = te // HALO_D

    def body(a_ref, b_ref, ha_ref, hb_ref, zb_ref, w_ref, wb_ref, lw_ref, lb_ref, c1_ref, c3_ref, c0_scr, sh_scr):
        i = pl.program_id(0)
        c0_scr[:HALO_D, :] = jnp.where(i > 0, ha_ref[...] * _sig(hb_ref[...]), 0.0)
        c0_scr[HALO_D:, :] = a_ref[...] * _sig(b_ref[...])
        off = HALO_D - (KD - 1)
        def lane_block(cb, carry):
            cs = pl.ds(pl.multiple_of(cb * LANE, LANE), LANE)
            _fill_shifted(sh_scr, c0_scr, cs)
            for r0 in range(0, te, CONV_RB):
                acc = None
                for j in range(KD):
                    term = w_ref[j:j + 1, cs] * _shifted(sh_scr, c0_scr, cs, off + j, r0, CONV_RB)
                    acc = term if acc is None else acc + term
                c1_ref[r0:r0 + CONV_RB, cs] = acc + wb_ref[:, cs]
            return carry

        lax.fori_loop(0, D // LANE, lane_block, 0)
        c1 = c1_ref[...]
        mu = jnp.mean(c1, axis=-1, keepdims=True)
        xc = c1 - mu
        c2 = xc * lax.rsqrt(jnp.mean(xc * xc, axis=-1, keepdims=True) + EPS) * lw_ref[...] + lb_ref[...]
        zb = zb_ref[...]
        c3_ref[...] = (c2 * _sig(c2) * zb * _sig(zb)).astype(BF16)

    vec = pl.BlockSpec((1, D), lambda i: (0, 0))
    return _call(
        body, name="conv_b_fwd", grid=(lp // te,),
        in_specs=[pl.BlockSpec((te, D), lambda i: (i, CB_GA_)), pl.BlockSpec((te, D), lambda i: (i, CB_GB_)),
                  pl.BlockSpec((HALO_D, D), lambda i: (jnp.maximum(i * hb - 1, 0), CB_GA_)),
                  pl.BlockSpec((HALO_D, D), lambda i: (jnp.maximum(i * hb - 1, 0), CB_GB_)),
                  pl.BlockSpec((te, D), lambda i: (i, CB_ZB)),
                  pl.BlockSpec((KD, D), lambda i: (0, 0)), vec, vec, vec],
        out_specs=(pl.BlockSpec((te, D), lambda i: (i, 0)), pl.BlockSpec((te, D), lambda i: (i, 0))),
        out_shape=(jax.ShapeDtypeStruct((lp, D), F32), jax.ShapeDtypeStruct((lp, D), BF16)),
        scratch_shapes=[pltpu.VMEM((te + HALO_D, D), F32), pltpu.VMEM((SUBLANES, te + HALO_D - SUBLANES, LANE), F32)],
        compiler_params=_cp(("parallel",)),
    )(proj, proj, proj, proj, proj, dw_w, dw_b, ln_w, ln_b)


def _conv_b_bwd1(dc3, c1, proj, ln_w, ln_b, dproj):
    lp = c1.shape[0]
    te = _pick(lp, (640, 320))

    def body(dc3_ref, c1_ref, zb_ref, lw_ref, lb_ref, _, dc1_ref, dzb_ref, sums_ref):
        @pl.when(pl.program_id(0) == 0)
        def _():
            sums_ref[...] = jnp.zeros_like(sums_ref)

        c1, g = c1_ref[...], dc3_ref[...]
        mu = jnp.mean(c1, axis=-1, keepdims=True)
        xc = c1 - mu
        rstd = lax.rsqrt(jnp.mean(xc * xc, axis=-1, keepdims=True) + EPS)
        xh = xc * rstd
        lw = lw_ref[...]
        c2 = xh * lw + lb_ref[...]
        s2 = _sig(c2)
        zb = zb_ref[...]
        sz = _sig(zb)
        dc2 = g * (zb * sz) * _dsilu(c2, s2)
        dzb_ref[...] = (g * (c2 * s2) * _dsilu(zb, sz)).astype(BF16)
        dxh = dc2 * lw
        dc1 = rstd * (dxh - jnp.mean(dxh, axis=-1, keepdims=True) - xh * jnp.mean(dxh * xh, axis=-1, keepdims=True))
        dc1_ref[...] = dc1
        sums_ref[...] += jnp.concatenate([_colsum(dc2 * xh), _colsum(dc2), _colsum(dc1)], axis=0)

    vec = pl.BlockSpec((1, D), lambda i: (0, 0))
    return _call(
        body, name="conv_b_bwd1", grid=(lp // te,),
        in_specs=[pl.BlockSpec((te, D), lambda i: (i, 0)), pl.BlockSpec((te, D), lambda i: (i, 0)),
                  pl.BlockSpec((te, D), lambda i: (i, CB_ZB)), vec, vec, pl.BlockSpec(memory_space=pl.ANY)],
        out_specs=(pl.BlockSpec((te, D), lambda i: (i, 0)), pl.BlockSpec((te, D), lambda i: (i, CB_ZB)),
                   pl.BlockSpec((3, D), lambda i: (0, 0))),
        out_shape=(jax.ShapeDtypeStruct((lp, D), F32), jax.ShapeDtypeStruct(dproj.shape, dproj.dtype),
                   jax.ShapeDtypeStruct((3, D), F32)),
        input_output_aliases={5: 1},
        compiler_params=_cp(("arbitrary",)),
    )(dc3, c1, proj, ln_w, ln_b, dproj)


def _conv_b_bwd2(dc1, proj, dw_w, dproj):
    lp = dc1.shape[0]
    te = _pick(lp, (320,))
    hb = te // HALO_D
    nt = lp // te
    last_hb = lp // HALO_D - 1

    def body(g_ref, gn_ref, a_ref, b_ref, ha_ref, hb_ref, w_ref, _, dab_ref, dw_ref, c0_scr, g_scr, dc0_scr,
             csh_scr, gsh_scr):
        i = pl.program_id(0)

        @pl.when(i == 0)
        def _():
            dw_ref[...] = jnp.zeros_like(dw_ref)

        a, b = a_ref[...], b_ref[...]
        sb = _sig(b)
        c0_scr[:HALO_D, :] = jnp.where(i > 0, ha_ref[...] * _sig(hb_ref[...]), 0.0)
        c0_scr[HALO_D:, :] = a * sb
        g_scr[:te, :] = g_ref[...]
        g_scr[te:, :] = jnp.where(i < nt - 1, gn_ref[...], 0.0)
        off = HALO_D - (KD - 1)
        def lane_block(cb, carry):
            cs = pl.ds(pl.multiple_of(cb * LANE, LANE), LANE)
            _fill_shifted(csh_scr, c0_scr, cs)
            _fill_shifted(gsh_scr, g_scr, cs)
            for r0 in range(0, te, CONV_RB):
                acc = None
                for j in range(KD):
                    term = w_ref[j:j + 1, cs] * _shifted(gsh_scr, g_scr, cs, KD - 1 - j, r0, CONV_RB)
                    acc = term if acc is None else acc + term
                dc0_scr[r0:r0 + CONV_RB, cs] = acc
            parts = [None] * KD
            for r0 in range(0, te, CONV_RB):
                g = g_scr[r0:r0 + CONV_RB, cs].reshape(CONV_RB // SUBLANES, SUBLANES, LANE)
                for j in range(KD):
                    x = _shifted(csh_scr, c0_scr, cs, off + j, r0, CONV_RB)
                    p = jnp.sum(g * x.reshape(CONV_RB // SUBLANES, SUBLANES, LANE), axis=0)
                    parts[j] = p if parts[j] is None else parts[j] + p
            dw_ref[:, cs] += jnp.concatenate([_colsum(p) for p in parts], axis=0)
            return carry

        lax.fori_loop(0, D // LANE, lane_block, 0)
        dc0 = dc0_scr[...]
        dab_ref[:, :D] = (dc0 * sb).astype(BF16)
        dab_ref[:, D:] = (dc0 * a * sb * (1.0 - sb)).astype(BF16)

    return _call(
        body, name="conv_b_bwd2", grid=(nt,),
        in_specs=[pl.BlockSpec((te, D), lambda i: (i, 0)),
                  pl.BlockSpec((HALO_D, D), lambda i: (jnp.minimum((i + 1) * hb, last_hb), 0)),
                  pl.BlockSpec((te, D), lambda i: (i, CB_GA_)), pl.BlockSpec((te, D), lambda i: (i, CB_GB_)),
                  pl.BlockSpec((HALO_D, D), lambda i: (jnp.maximum(i * hb - 1, 0), CB_GA_)),
                  pl.BlockSpec((HALO_D, D), lambda i: (jnp.maximum(i * hb - 1, 0), CB_GB_)),
                  pl.BlockSpec((KD, D), lambda i: (0, 0)), pl.BlockSpec(memory_space=pl.ANY)],
        out_specs=(pl.BlockSpec((te, 2 * D), lambda i: (i, CB_GA_ // 2)), pl.BlockSpec((KD, D), lambda i: (0, 0))),
        out_shape=(jax.ShapeDtypeStruct(dproj.shape, dproj.dtype), jax.ShapeDtypeStruct((KD, D), F32)),
        input_output_aliases={7: 0},
        scratch_shapes=[pltpu.VMEM((te + HALO_D, D), F32), pltpu.VMEM((te + HALO_D, D), F32), pltpu.VMEM((te, D), F32),
                        pltpu.VMEM((SUBLANES, te + HALO_D - SUBLANES, LANE), F32),
                        pltpu.VMEM((SUBLANES, te + HALO_D - SUBLANES, LANE), F32)],
        compiler_params=_cp(("arbitrary",)),
    )(dc1, dc1, proj, proj, proj, proj, dw_w, dproj)


def _merge_fwd(y_a, y_b, proj, b_cf):
    lp = y_a.shape[0]
    te = _pick(lp, (640, 320))

    def body(ya_ref, yb_ref, ga_ref, gb_ref, bias_ref, out_ref):
        out_ref[...] = (_sig(ga_ref[...]) * ya_ref[...] + _sig(gb_ref[...]) * (yb_ref[...] + bias_ref[...])).astype(BF16)

    row = lambda j: pl.BlockSpec((te, D), lambda i: (i, j))
    return _call(
        body, name="merge_fwd", grid=(lp // te,),
        in_specs=[row(0), row(0), row(CB_MA), row(CB_MB), pl.BlockSpec((1, D), lambda i: (0, 0))],
        out_specs=row(0), out_shape=jax.ShapeDtypeStruct((lp, D), BF16),
        compiler_params=_cp(("parallel",)),
    )(y_a, y_b, proj, proj, b_cf)


def _merge_bwd(dmerged, y_a, y_b, proj, b_cf):
    lp = y_a.shape[0]
    te = _pick(lp, (640, 320))

    def body(dm_ref, ya_ref, yb_ref, ga_ref, gb_ref, bias_ref, dya_ref, dyb_ref, dg_ref, db_ref):
        @pl.when(pl.program_id(0) == 0)
        def _():
            db_ref[...] = jnp.zeros_like(db_ref)

        dm = dm_ref[...]
        sa, sb = _sig(ga_ref[...]), _sig(gb_ref[...])
        dyb = sb * dm
        dya_ref[...] = (sa * dm).astype(BF16)
        dyb_ref[...] = dyb.astype(BF16)
        dg_ref[:, :D] = (dm * ya_ref[...] * sa * (1.0 - sa)).astype(BF16)
        dg_ref[:, D:] = (dm * (yb_ref[...] + bias_ref[...]) * sb * (1.0 - sb)).astype(BF16)
        db_ref[...] += _colsum(dyb)

    row = lambda j: pl.BlockSpec((te, D), lambda i: (i, j))
    act = jax.ShapeDtypeStruct((lp, D), BF16)
    return _call(
        body, name="merge_bwd", grid=(lp // te,),
        in_specs=[row(0), row(0), row(0), row(CB_MA), row(CB_MB), pl.BlockSpec((1, D), lambda i: (0, 0))],
        out_specs=(row(0), row(0), pl.BlockSpec((te, 2 * D), lambda i: (i, CB_MA // 2)),
                   pl.BlockSpec((1, D), lambda i: (0, 0))),
        out_shape=(act, act, jax.ShapeDtypeStruct((lp, NCB * D), BF16), jax.ShapeDtypeStruct((1, D), F32)),
        compiler_params=_cp(("arbitrary",)),
    )(dmerged, y_a, y_b, proj, proj, b_cf)


def _final_fwd_bwd(x_ext, z, target, final_w):
    lp = x_ext.shape[0]
    te = LANE

    def body(x_ref, z_ref, t_ref, w_ref, dx_ref, loss_ref, dw_ref):
        i = pl.program_id(0)

        @pl.when(i == 0)
        def _():
            loss_ref[...] = jnp.zeros_like(loss_ref)
            dw_ref[...] = jnp.zeros_like(dw_ref)

        xo = x_ref[...] + z_ref[...]
        r = lax.rsqrt(jnp.mean(xo * xo, axis=-1, keepdims=True) + EPS)
        xhat = xo * r
        w = w_ref[...]
        err = jnp.where(i > 0, xhat * w - t_ref[...], 0.0)
        loss_ref[...] += 0.5 * jnp.sum(jnp.mean(err * err, axis=-1, keepdims=True), keepdims=True)
        dy = err * (1.0 / D)
        dw_ref[...] += _colsum(dy * xhat)
        dxn = dy * w
        dx_ref[...] = r * (dxn - xhat * jnp.mean(dxn * xhat, axis=-1, keepdims=True))

    return _call(
        body, name="final_fwd_bwd", grid=(lp // te,),
        in_specs=[pl.BlockSpec((te, D), lambda i: (i, 0)), pl.BlockSpec((te, D), lambda i: (i, 0)),
                  pl.BlockSpec((te, D), lambda i: (jnp.maximum(i - 1, 0), 0)), pl.BlockSpec((1, D), lambda i: (0, 0))],
        out_specs=(pl.BlockSpec((te, D), lambda i: (i, 0)), pl.BlockSpec((1, 1), lambda i: (0, 0)),
                   pl.BlockSpec((1, D), lambda i: (0, 0))),
        out_shape=(jax.ShapeDtypeStruct((lp, D), F32), jax.ShapeDtypeStruct((1, 1), F32),
                   jax.ShapeDtypeStruct((1, D), F32)),
        compiler_params=_cp(("arbitrary",)),
    )(x_ext, z, target, final_w)


def _prenorm_bwd(dh, x_ext, dx_out, norm_w, seq):
    lp = x_ext.shape[0]
    te = LANE

    def body(dh_ref, x_ref, dxo_ref, w_ref, gx_ref, head_ref, dw_ref):
        i = pl.program_id(0)

        @pl.when(i == 0)
        def _():
            dw_ref[...] = jnp.zeros_like(dw_ref)

        x, dh = x_ref[...], dh_ref[...]
        r = lax.rsqrt(jnp.mean(x * x, axis=-1, keepdims=True) + EPS)
        xhat = x * r
        dxn = dh * w_ref[...]
        dx = dxo_ref[...] + r * (dxn - xhat * jnp.mean(dxn * xhat, axis=-1, keepdims=True))
        dw_ref[...] += _colsum(dh * xhat)

        @pl.when(i == 0)
        def _():
            head_ref[...] = dx

        @pl.when(i > 0)
        def _():
            gx_ref[...] = dx

    row = pl.BlockSpec((te, D), lambda i: (i, 0))
    return _call(
        body, name="prenorm_bwd", grid=(lp // te,),
        in_specs=[row, row, row, pl.BlockSpec((1, D), lambda i: (0, 0))],
        out_specs=(pl.BlockSpec((te, D), lambda i: (jnp.maximum(i - 1, 0), 0)), pl.BlockSpec((te, D), lambda i: (0, 0)),
                   pl.BlockSpec((1, D), lambda i: (0, 0))),
        out_shape=(jax.ShapeDtypeStruct((seq, D), F32), jax.ShapeDtypeStruct((te, D), F32),
                   jax.ShapeDtypeStruct((1, D), F32)),
        compiler_params=_cp(("arbitrary",)),
    )(dh, x_ext, dx_out, norm_w)


def _adam_reduce(parts, w, m, v, name):
    r, n = w.shape
    tr = _pick(r, (128,)) if r % 128 == 0 else r
    c1 = 1.0 - ADAM_B1 ** ADAM_STEP
    c2 = 1.0 - ADAM_B2 ** ADAM_STEP

    def body(p_ref, w_ref, m_ref, v_ref, g_ref, d_ref, m2_ref, v2_ref):
        g = p_ref[0]
        for s in range(1, NDEV):
            g = g + p_ref[s]
        m2 = ADAM_B1 * m_ref[...] + (1.0 - ADAM_B1) * g
        v2 = ADAM_B2 * v_ref[...] + (1.0 - ADAM_B2) * (g * g)
        g_ref[...] = g
        m2_ref[...] = m2
        v2_ref[...] = v2
        d_ref[...] = -ADAM_LR * ((m2 / c1) / (jnp.sqrt(v2 / c2) + ADAM_EPS) + ADAM_WD * w_ref[...])

    blk = pl.BlockSpec((tr, n), lambda i: (i, 0))
    out = jax.ShapeDtypeStruct((r, n), F32)
    return _call(
        body, name=name, grid=(r // tr,),
        in_specs=[pl.BlockSpec((NDEV, tr, n), lambda i: (0, i, 0)), blk, blk, blk],
        out_specs=(blk, blk, blk, blk), out_shape=(out, out, out, out),
        compiler_params=_cp(("parallel",)),
    )(parts, w, m, v)


SMALL = ("norm_w", "a_log", "dt_bias", "dn_norm_w", "dw_b", "ln_w", "ln_b", "b_cf_out", "final_norm_w")


def kernel(x, meta, norm_w, w_in, conv_qkv_w, a_log, dt_bias, dn_norm_w, w_dn_out, dw_w, dw_b, ln_w, ln_b, w_cf_out, b_cf_out, w_o, final_norm_w, loss_target, m_meta, m_norm_w, m_w_in, m_conv_qkv_w, m_a_log, m_dt_bias, m_dn_norm_w, m_w_dn_out, m_dw_w, m_dw_b, m_ln_w, m_ln_b, m_w_cf_out, m_b_cf_out, m_w_o, m_final_norm_w, v_meta, v_norm_w, v_w_in, v_conv_qkv_w, v_a_log, v_dt_bias, v_dn_norm_w, v_w_dn_out, v_dw_w, v_dw_b, v_ln_w, v_ln_b, v_w_cf_out, v_b_cf_out, v_w_o, v_final_norm_w):
    seq = x.shape[1]
    pad = (-(seq + NMETA)) % LANE
    in_w = w_in.shape[2] * NDEV
    n_qkvz = 4 * D
    n_ba = 2 * H

    w_in_g, w_dn_g, w_cf_g, w_o_g, meta_g, cqw_g, dww_g = _exchange(
        [w_in[0].astype(BF16), w_dn_out[0].astype(BF16), w_cf_out[0].astype(BF16), w_o[0].astype(BF16),
         meta, conv_qkv_w[0], dw_w[0]], [False] * 7, "gather_weights")
    w_full = jnp.transpose(w_in_g, (1, 0, 2)).reshape(D, in_w)
    c_glu = n_qkvz + n_ba
    c_zb, c_mg = c_glu + 2 * D, c_glu + 3 * D
    w_main = jnp.concatenate([w_full[:, :n_qkvz], w_full[:, c_glu:c_zb], w_full[:, c_mg:], w_full[:, c_zb:c_mg]],
                             axis=1)
    w_ba = jnp.pad(w_full[:, n_qkvz:n_qkvz + n_ba], ((0, 0), (0, LANE - n_ba)))
    w_dn, w_cf, w_oo = (t.reshape(D, D) for t in (w_dn_g, w_cf_g, w_o_g))
    meta_full = jnp.transpose(meta_g, (1, 0, 2)).reshape(NMETA, D)
    cqw = jnp.transpose(cqw_g, (1, 0, 2)).reshape(KQ, 3 * D)
    dww = jnp.transpose(dww_g, (1, 0, 2)).reshape(KD, D)
    ab = jnp.pad(jnp.concatenate([a_log, dt_bias], axis=0), ((0, 0), (H, LANE - 2 * H)))

    x_ext = jnp.concatenate([jnp.zeros((pad, D), F32), meta_full, x[0]], axis=0)

    proj, ba, h = _proj_fwd(x_ext, norm_w, w_main, w_ba)
    qkv, bg = _qkv_conv_fwd(proj, ba, cqw, ab, pad)
    o, sall, tall = _delta_fwd(qkv, bg)
    o_n = _o_post_fwd(o, proj, dn_norm_w)
    y_a = _mm(o_n, w_dn, "y_a_mm")
    c1, c3 = _conv_b_fwd(proj, dww, dw_b, ln_w, ln_b)
    y_b = _mm(c3, w_cf, "y_b_mm")
    merged = _merge_fwd(y_a, y_b, proj, b_cf_out)
    z = _mm(merged, w_oo, "z_mm")
    dx_out, loss_part, g_final_w = _final_fwd_bwd(x_ext, z, loss_target[0], final_norm_w.reshape(1, D))

    dx_out_b = dx_out.astype(BF16)
    dmerged = _mm(dx_out_b, w_oo.T, "dmerged_mm")
    g_w_o = _mm_tn(merged, dx_out_b, "g_w_o_mm")
    dy_a, dy_b, dproj, g_b_cf = _merge_bwd(dmerged, y_a, y_b, proj, b_cf_out)
    dc3 = _mm(dy_b, w_cf.T, "dc3_mm")
    g_w_cf = _mm_tn(c3, dy_b, "g_w_cf_mm")
    do_n = _mm(dy_a, w_dn.T, "do_n_mm")
    g_w_dn = _mm_tn(o_n, dy_a, "g_w_dn_mm")
    dc1, dproj, sums_b = _conv_b_bwd1(dc3, c1, proj, ln_w, ln_b, dproj)
    dproj, g_dw_w = _conv_b_bwd2(dc1, proj, dww, dproj)
    do, dproj, g_dn_w = _o_post_bwd(do_n, o, proj, dn_norm_w, dproj)
    dqkv, dbg = _delta_bwd(qkv, bg, sall, tall, do)
    dproj, g_cqw = _qkv_conv_bwd(proj, dqkv, cqw, dproj)
    dba, dab = _ba_bwd(dbg, ba, ab, pad)
    dh = _dh_mm(dproj, dba, w_main.T, w_ba.T)
    g_w_main = _mm_tn(h, dproj, "g_w_main_mm")
    g_w_ba = _mm_tn(h, dba, "g_w_ba_mm")
    grad_x, dhead, g_norm_w = _prenorm_bwd(dh, x_ext, dx_out, norm_w, seq)

    g_w_full = jnp.concatenate([g_w_main[:, :n_qkvz], g_w_ba[:, :n_ba], g_w_main[:, CB_GA_ * D:CB_MA * D],
                                g_w_main[:, CB_ZB * D:], g_w_main[:, CB_MA * D:CB_ZB * D]], axis=1)
    split_cols = lambda t: jnp.transpose(t.reshape(t.shape[0], NDEV, t.shape[1] // NDEV), (1, 0, 2))
    small = {"norm_w": g_norm_w, "a_log": dab[0:1, H:2 * H], "dt_bias": dab[1:2, H:2 * H], "dn_norm_w": g_dn_w,
             "dw_b": sums_b[2:3], "ln_w": sums_b[0:1], "ln_b": sums_b[1:2], "b_cf_out": g_b_cf,
             "final_norm_w": g_final_w}
    small_vec = jnp.concatenate([small[k] for k in SMALL], axis=1)
    ns = small_vec.shape[1]
    ns_pad = (-ns) % LANE
    small_vec = jnp.pad(small_vec, ((0, 0), (0, ns_pad)))
    p_w_in, p_w_dn, p_w_cf, p_w_o, p_meta, p_cqw, p_dww, p_small = _exchange(
        [split_cols(g_w_full), g_w_dn.reshape(NDEV, D // NDEV, D), g_w_cf.reshape(NDEV, D // NDEV, D),
         g_w_o.reshape(NDEV, D // NDEV, D), split_cols(dhead[pad:pad + NMETA]), split_cols(g_cqw), split_cols(g_dw_w),
         small_vec], [True] * 7 + [False], "exchange_grads")

    res = {}
    res["w_in"] = _adam_reduce(p_w_in, w_in[0], m_w_in[0], v_w_in[0], "adam_w_in")
    res["w_dn_out"] = _adam_reduce(p_w_dn, w_dn_out[0], m_w_dn_out[0], v_w_dn_out[0], "adam_w_dn")
    res["w_cf_out"] = _adam_reduce(p_w_cf, w_cf_out[0], m_w_cf_out[0], v_w_cf_out[0], "adam_w_cf")
    res["w_o"] = _adam_reduce(p_w_o, w_o[0], m_w_o[0], v_w_o[0], "adam_w_o")
    res["meta"] = _adam_reduce(p_meta, meta, m_meta, v_meta, "adam_meta")
    res["conv_qkv_w"] = _adam_reduce(p_cqw, conv_qkv_w[0], m_conv_qkv_w[0], v_conv_qkv_w[0], "adam_conv_qkv_w")
    res["dw_w"] = _adam_reduce(p_dww, dw_w[0], m_dw_w[0], v_dw_w[0], "adam_dw_w")
    loc = dict(norm_w=(norm_w, m_norm_w, v_norm_w), a_log=(a_log, m_a_log, v_a_log), dt_bias=(dt_bias, m_dt_bias, v_dt_bias),
               dn_norm_w=(dn_norm_w, m_dn_norm_w, v_dn_norm_w), dw_b=(dw_b, m_dw_b, v_dw_b), ln_w=(ln_w, m_ln_w, v_ln_w),
               ln_b=(ln_b, m_ln_b, v_ln_b), b_cf_out=(b_cf_out, m_b_cf_out, v_b_cf_out),
               final_norm_w=(final_norm_w, m_final_norm_w, v_final_norm_w))
    cat = lambda j: jnp.pad(jnp.concatenate([loc[k][j].reshape(1, -1) for k in SMALL], axis=1), ((0, 0), (0, ns_pad)))
    small_res = _adam_reduce(p_small, cat(0), cat(1), cat(2), "adam_small")
    off = 0
    for k in SMALL:
        wshape = loc[k][0].shape
        nk = loc[k][0].size
        res[k] = tuple(t[:, off:off + nk].reshape(wshape) for t in small_res)
        off += nk
    shaped = dict(w_in=w_in.shape, w_dn_out=w_dn_out.shape, w_cf_out=w_cf_out.shape, w_o=w_o.shape, meta=meta.shape,
                  conv_qkv_w=conv_qkv_w.shape, dw_w=dw_w.shape)
    for k, shp in shaped.items():
        res[k] = tuple(t.reshape(shp) for t in res[k])

    loss = lax.psum(loss_part[0, 0], ("x", "y", "c"))
    order = ("meta", "norm_w", "w_in", "conv_qkv_w", "a_log", "dt_bias", "dn_norm_w", "w_dn_out", "dw_w", "dw_b", "ln_w",
             "ln_b", "w_cf_out", "b_cf_out", "w_o", "final_norm_w")
    outs = [loss, grad_x[None]]
    for j in range(4):
        outs += [res[k][j] for k in order]
    return tuple(outs)
```

```python
import functools

import jax
import jax.numpy as jnp
from jax import lax
from jax.experimental import pallas as pl
from jax.experimental.pallas import tpu as pltpu

F32 = jnp.float32
BF16 = jnp.bfloat16
HI = lax.Precision.HIGHEST

D = 1024
H = 8
DK = 128
C = 64
NMETA = 16
KQ = 4
KD = 31
HALO_Q = 8
HALO_D = 32
EPS = 1e-6
NDEV = 8
LANE = 128
MIB = 1024 * 1024

ADAM_LR, ADAM_B1, ADAM_B2, ADAM_EPS, ADAM_WD, ADAM_STEP = 0.001, 0.9, 0.999, 1e-08, 0.01, 10

CB_Q, CB_K, CB_V, CB_ZA, CB_GA_, CB_GB_, CB_MA, CB_MB, CB_ZB = range(9)
NCB = 9


def _pick(n, cands):
    for c in cands:
        if n % c == 0:
            return c
    raise ValueError(f"no tile for {n}")


def _cp(sem=None, vmem_mib=40):
    kw = dict(vmem_limit_bytes=vmem_mib * MIB)
    if sem is not None:
        kw["dimension_semantics"] = sem
    return pltpu.CompilerParams(**kw)


def _call(body, **kw):
    return pl.pallas_call(body, **kw)


def _dot(a, b):
    return jnp.dot(a.astype(BF16), b.astype(BF16), preferred_element_type=F32)


def _dot_nt(a, b):
    return lax.dot_general(a.astype(BF16), b.astype(BF16), (((1,), (1,)), ((), ())), preferred_element_type=F32)


def _dot_tn(a, b):
    return lax.dot_general(a.astype(BF16), b.astype(BF16), (((0,), (0,)), ((), ())), preferred_element_type=F32)


def _dot_hi(a, b):
    return jnp.dot(a, b, precision=HI, preferred_element_type=F32)


def _sig(x):
    return 0.5 * jnp.tanh(0.5 * x) + 0.5


def _dsilu(x, s):
    return s * (1.0 + x * (1.0 - s))


def _rowsum(x):
    return jnp.sum(x, axis=-1, keepdims=True)


def _colsum(x):
    return jnp.sum(x, axis=0, keepdims=True)


def _exchange(arrs, scatter, name):
    n = len(arrs)
    out_shape = []
    for a, sc in zip(arrs, scatter):
        shp = a.shape if sc else (NDEV,) + a.shape
        out_shape.append(jax.ShapeDtypeStruct(shp, a.dtype))

    def body(*refs):
        ins, outs = refs[:n], refs[n:2 * n]
        send_sems, recv_sems, loc_sems = refs[2 * n:]
        x, y, c = lax.axis_index("x"), lax.axis_index("y"), lax.axis_index("c")
        me = 4 * x + 2 * y + c
        copies = []
        for a in range(n):
            for k in range(1, NDEV):
                px = 1 - x if (k >> 2) & 1 else x
                py = 1 - y if (k >> 1) & 1 else y
                pc = 1 - c if k & 1 else c
                src = ins[a].at[4 * px + 2 * py + pc] if scatter[a] else ins[a]
                cp = pltpu.make_async_remote_copy(
                    src_ref=src, dst_ref=outs[a].at[me],
                    send_sem=send_sems.at[a * (NDEV - 1) + k - 1], recv_sem=recv_sems.at[a * (NDEV - 1) + k - 1],
                    device_id=(px, py, pc), device_id_type=pl.DeviceIdType.MESH)
                cp.start()
                copies.append(cp)
            loc = pltpu.make_async_copy(ins[a].at[me] if scatter[a] else ins[a], outs[a].at[me], loc_sems.at[a])
            loc.start()
            copies.append(loc)
        for cp in copies:
            cp.wait()

    any_spec = pl.BlockSpec(memory_space=pl.ANY)
    return _call(
        body, name=name, out_shape=tuple(out_shape),
        in_specs=[any_spec] * n, out_specs=tuple([any_spec] * n),
        scratch_shapes=[pltpu.SemaphoreType.DMA((n * (NDEV - 1),)), pltpu.SemaphoreType.DMA((n * (NDEV - 1),)),
                        pltpu.SemaphoreType.DMA((n,))],
    )(*arrs)


NCHIP = 4


def _gather_two_level(arrs, name):
    n = len(arrs)
    per = NDEV - 1

    def body(*refs):
        ins, outs = refs[:n], refs[n:2 * n]
        send_sems, recv_sems, loc_sems = refs[2 * n:]
        x, y, c = lax.axis_index("x"), lax.axis_index("y"), lax.axis_index("c")
        me, sibling = (x, y, c), (x, y, 1 - c)
        chips = [(1 - x, y), (x, 1 - y), (1 - x, 1 - y)]

        def slot(a, px, py, pc):
            return outs[a].at[4 * px + 2 * py + pc]

        def copy(a, k, block, to, src=None):
            return pltpu.make_async_remote_copy(
                src_ref=slot(a, *block) if src is None else src, dst_ref=slot(a, *block),
                send_sem=send_sems.at[a * per + k], recv_sem=recv_sems.at[a * per + k],
                device_id=to, device_id_type=pl.DeviceIdType.MESH)

        local, sent = [], []
        for a in range(n):
            mine = pltpu.make_async_copy(ins[a], slot(a, *me), loc_sems.at[a])
            mine.start()
            local.append(mine)
            first = [copy(a, 1 + j, me, (*chip, c), src=ins[a]) for j, chip in enumerate(chips)]
            first.append(copy(a, 0, me, sibling, src=ins[a]))
            for cp in first:
                cp.start()
            sent += first
        for j, chip in enumerate(chips):
            for a in range(n):
                copy(a, 1 + j, (*chip, c), me).wait_recv()
                cp = copy(a, 4 + j, (*chip, c), sibling)
                cp.start()
                sent.append(cp)
        for a in range(n):
            copy(a, 0, sibling, me).wait_recv()
            for j, chip in enumerate(chips):
                copy(a, 4 + j, (*chip, 1 - c), me).wait_recv()
        for cp in sent:
            cp.wait_send()
        for cp in local:
            cp.wait()

    any_spec = pl.BlockSpec(memory_space=pl.ANY)
    return _call(
        body, name=name, out_shape=tuple(jax.ShapeDtypeStruct((NDEV,) + a.shape, a.dtype) for a in arrs),
        in_specs=[any_spec] * n, out_specs=tuple([any_spec] * n),
        scratch_shapes=[pltpu.SemaphoreType.DMA((n * per,)), pltpu.SemaphoreType.DMA((n * per,)),
                        pltpu.SemaphoreType.DMA((n,))],
    )(*arrs)


def _swap_sibling(arrs, name):
    n = len(arrs)

    def body(*refs):
        ins, outs = refs[:n], refs[n:2 * n]
        send_sems, recv_sems = refs[2 * n:]
        x, y, c = lax.axis_index("x"), lax.axis_index("y"), lax.axis_index("c")
        copies = []
        for a in range(n):
            for j in range(NCHIP):
                cp = pltpu.make_async_remote_copy(
                    src_ref=ins[a].at[2 * j + (1 - c)], dst_ref=outs[a].at[j],
                    send_sem=send_sems.at[a * NCHIP + j], recv_sem=recv_sems.at[a * NCHIP + j],
                    device_id=(x, y, 1 - c), device_id_type=pl.DeviceIdType.MESH)
                cp.start()
                copies.append(cp)
        for cp in copies:
            cp.wait()

    any_spec = pl.BlockSpec(memory_space=pl.ANY)
    return _call(
        body, name=name, out_shape=tuple(jax.ShapeDtypeStruct((NCHIP,) + a.shape[1:], a.dtype) for a in arrs),
        in_specs=[any_spec] * n, out_specs=tuple([any_spec] * n),
        scratch_shapes=[pltpu.SemaphoreType.DMA((n * NCHIP,)), pltpu.SemaphoreType.DMA((n * NCHIP,))],
    )(*arrs)


def _pair_add(arr, got, name):
    _, r, n = arr.shape
    tr = _pick(r, (128,))
    arr4 = arr.reshape(NCHIP, 2, r, n)

    def body(a_ref, g_ref, p_ref, own_ref):
        c = lax.axis_index("c")
        my_chip = 2 * lax.axis_index("x") + lax.axis_index("y")
        s = jnp.where(c == 0, a_ref[0, 0], a_ref[0, 1]) + g_ref[0]
        p_ref[0] = s.astype(BF16)

        @pl.when(pl.program_id(1) == my_chip)
        def _():
            own_ref[...] = s

    return _call(
        body, name=name, grid=(r // tr, NCHIP),
        in_specs=[pl.BlockSpec((1, 2, tr, n), lambda i, j: (j, 0, i, 0)), pl.BlockSpec((1, tr, n), lambda i, j: (j, i, 0))],
        out_specs=(pl.BlockSpec((1, tr, n), lambda i, j: (j, i, 0)), pl.BlockSpec((tr, n), lambda i, j: (i, 0))),
        out_shape=(jax.ShapeDtypeStruct((NCHIP, r, n), BF16), jax.ShapeDtypeStruct((r, n), F32)),
        compiler_params=_cp(("parallel", "arbitrary")),
    )(arr4, got)


def _scatter_chips(arrs, name):
    n = len(arrs)
    per = NCHIP - 1

    def body(*refs):
        ins, outs = refs[:n], refs[n:2 * n]
        send_sems, recv_sems, loc_sems = refs[2 * n:]
        x, y, c = lax.axis_index("x"), lax.axis_index("y"), lax.axis_index("c")
        copies = []
        for a in range(n):
            loc = pltpu.make_async_copy(ins[a].at[2 * x + y], outs[a].at[2 * x + y], loc_sems.at[a])
            loc.start()
            copies.append(loc)
            for k in range(1, NCHIP):
                px = 1 - x if (k >> 1) & 1 else x
                py = 1 - y if k & 1 else y
                cp = pltpu.make_async_remote_copy(
                    src_ref=ins[a].at[2 * px + py], dst_ref=outs[a].at[2 * x + y],
                    send_sem=send_sems.at[a * per + k - 1], recv_sem=recv_sems.at[a * per + k - 1],
                    device_id=(px, py, c), device_id_type=pl.DeviceIdType.MESH)
                cp.start()
                copies.append(cp)
        for cp in copies:
            cp.wait()

    any_spec = pl.BlockSpec(memory_space=pl.ANY)
    return _call(
        body, name=name, out_shape=tuple(jax.ShapeDtypeStruct(a.shape, a.dtype) for a in arrs),
        in_specs=[any_spec] * n, out_specs=tuple([any_spec] * n),
        scratch_shapes=[pltpu.SemaphoreType.DMA((n * per,)), pltpu.SemaphoreType.DMA((n * per,)),
                        pltpu.SemaphoreType.DMA((n,))],
    )(*arrs)


def _mm(a, b, name, out_dtype=F32):
    m, k = a.shape
    n = b.shape[1]
    tm = _pick(m, (1664, 832, 640, 320, 128))
    tn = _pick(n, (512, 256, 128))
    tk = _pick(k, (1024, 512, 128))
    nk = k // tk

    def body(a_ref, b_ref, o_ref, *acc):
        part = _dot(a_ref[...], b_ref[...])
        if nk == 1:
            o_ref[...] = part.astype(out_dtype)
        else:
            kk = pl.program_id(2)

            @pl.when(kk == 0)
            def _():
                acc[0][...] = part

            @pl.when(kk > 0)
            def _():
                acc[0][...] += part

            @pl.when(kk == nk - 1)
            def _():
                o_ref[...] = acc[0][...].astype(out_dtype)

    return _call(
        body, name=name, grid=(m // tm, n // tn, nk),
        in_specs=[pl.BlockSpec((tm, tk), lambda i, j, kk: (i, kk)), pl.BlockSpec((tk, tn), lambda i, j, kk: (kk, j))],
        out_specs=pl.BlockSpec((tm, tn), lambda i, j, kk: (i, j)),
        out_shape=jax.ShapeDtypeStruct((m, n), out_dtype),
        scratch_shapes=[pltpu.VMEM((tm, tn), F32)] if nk > 1 else [],
        compiler_params=_cp(("parallel", "parallel", "arbitrary")),
    )(a, b)


def _mm_tn(a, b, name):
    t, m = a.shape
    n = b.shape[1]
    tt = _pick(t, (640, 128))
    tm = _pick(m, (1024, 512, 128))
    tn = _pick(n, (1152, 1024, 512, 128))
    nt = t // tt

    def body(a_ref, b_ref, o_ref):
        s = pl.program_id(2)
        part = _dot_tn(a_ref[...], b_ref[...])

        @pl.when(s == 0)
        def _():
            o_ref[...] = part

        @pl.when(s > 0)
        def _():
            o_ref[...] += part

    return _call(
        body, name=name, grid=(m // tm, n // tn, nt),
        in_specs=[pl.BlockSpec((tt, tm), lambda i, j, s: (s, i)), pl.BlockSpec((tt, tn), lambda i, j, s: (s, j))],
        out_specs=pl.BlockSpec((tm, tn), lambda i, j, s: (i, j)),
        out_shape=jax.ShapeDtypeStruct((m, n), F32),
        compiler_params=_cp(("parallel", "parallel", "arbitrary")),
    )(a, b)


def _proj_fwd(x_ext, norm_w, w_main, w_ba):
    lp = x_ext.shape[0]
    n = w_main.shape[1]
    tm = _pick(lp, (832, 640, 320))
    tn = 512

    def body(x_ref, nw_ref, w_ref, wba_ref, proj_ref, ba_ref, h_ref):
        @pl.when(pl.program_id(1) == 0)
        def _():
            x = x_ref[...]
            r = lax.rsqrt(jnp.mean(x * x, axis=-1, keepdims=True) + EPS)
            h = (x * r * nw_ref[...]).astype(BF16)
            h_ref[...] = h
            ba_ref[...] = jnp.dot(h, wba_ref[...], preferred_element_type=F32)

        proj_ref[...] = jnp.dot(h_ref[...], w_ref[...], preferred_element_type=F32)

    return _call(
        body, name="proj_fwd", grid=(lp // tm, n // tn),
        in_specs=[pl.BlockSpec((tm, D), lambda i, j: (i, 0)), pl.BlockSpec((1, D), lambda i, j: (0, 0)),
                  pl.BlockSpec((D, tn), lambda i, j: (0, j)), pl.BlockSpec((D, LANE), lambda i, j: (0, 0))],
        out_specs=(pl.BlockSpec((tm, tn), lambda i, j: (i, j)), pl.BlockSpec((tm, LANE), lambda i, j: (i, 0)),
                   pl.BlockSpec((tm, D), lambda i, j: (i, 0))),
        out_shape=(jax.ShapeDtypeStruct((lp, n), F32), jax.ShapeDtypeStruct((lp, LANE), F32),
                   jax.ShapeDtypeStruct((lp, D), BF16)),
        compiler_params=_cp(("parallel", "arbitrary")),
    )(x_ext, norm_w, w_main, w_ba)


def _dh_mm(dproj, dba, w_main_t, w_ba_t):
    lp, n = dproj.shape
    tm = _pick(lp, (1664, 640, 320))
    tn = 512
    tk = 1024
    nk = n // tk

    def body(a_ref, ba_ref, b_ref, bba_ref, o_ref, acc):
        kk = pl.program_id(2)

        @pl.when(kk == 0)
        def _():
            acc[...] = jnp.dot(ba_ref[...], bba_ref[...], preferred_element_type=F32)

        acc[...] += jnp.dot(a_ref[...], b_ref[...], preferred_element_type=F32)

        @pl.when(kk == nk - 1)
        def _():
            o_ref[...] = acc[...]

    return _call(
        body, name="dh_mm", grid=(lp // tm, D // tn, nk),
        in_specs=[pl.BlockSpec((tm, tk), lambda i, j, kk: (i, kk)), pl.BlockSpec((tm, LANE), lambda i, j, kk: (i, 0)),
                  pl.BlockSpec((tk, tn), lambda i, j, kk: (kk, j)), pl.BlockSpec((LANE, tn), lambda i, j, kk: (0, j))],
        out_specs=pl.BlockSpec((tm, tn), lambda i, j, kk: (i, j)),
        out_shape=jax.ShapeDtypeStruct((lp, D), F32),
        scratch_shapes=[pltpu.VMEM((tm, tn), F32)],
        compiler_params=_cp(("parallel", "parallel", "arbitrary")),
    )(dproj, dba, w_main_t, w_ba_t)


def _l2norm_heads(a):
    outs, rs = [], []
    for h in range(H):
        blk = a[:, h * DK:(h + 1) * DK]
        r = lax.rsqrt(_rowsum(blk * blk) + EPS)
        outs.append(blk * r)
        rs.append(r)
    return jnp.concatenate(outs, axis=1), rs


def _beta_g(ba, ab, row0, pad):
    lane = lax.broadcasted_iota(jnp.int32, ba.shape, 1)
    rows = row0 + lax.broadcasted_iota(jnp.int32, ba.shape, 0)
    z = ba + ab[1:2, :]
    sp = jnp.maximum(z, 0.0) + jnp.log(1.0 + jnp.exp(-jnp.abs(z)))
    val = jnp.where(lane < H, _sig(ba), -jnp.exp(ab[0:1, :]) * sp)
    return jnp.where((lane < 2 * H) & (rows >= pad), val, 0.0)


def _qkv_conv_fwd(proj, ba, conv_w, ab, pad):
    lp = proj.shape[0]
    te = _pick(lp, (320,))
    hb = te // HALO_Q

    def body(main_ref, halo_ref, cw_ref, ba_ref, ab_ref, out_ref, bg_ref):
        i, s = pl.program_id(0), pl.program_id(1)
        halo = jnp.where(i > 0, halo_ref[...], 0.0)
        pre = jnp.concatenate([halo, main_ref[...]], axis=0)
        cw = cw_ref[...]
        co = cw[0:1, :] * pre[HALO_Q - 3:HALO_Q - 3 + te, :]
        for j in range(1, KQ):
            co = co + cw[j:j + 1, :] * pre[HALO_Q - 3 + j:HALO_Q - 3 + j + te, :]
        a = co * _sig(co)
        nrm, _ = _l2norm_heads(a)
        nrm = nrm * jnp.where(s == 0, DK ** -0.5, 1.0)
        out_ref[...] = jnp.where(s == 2, a, nrm)

        @pl.when(s == 0)
        def _():
            bg_ref[...] = _beta_g(ba_ref[...], ab_ref[...], i * te, pad)

    return _call(
        body, name="qkv_conv_fwd", grid=(lp // te, 3),
        in_specs=[pl.BlockSpec((te, D), lambda i, s: (i, s)),
                  pl.BlockSpec((HALO_Q, D), lambda i, s: (jnp.maximum(i * hb - 1, 0), s)),
                  pl.BlockSpec((KQ, D), lambda i, s: (0, s)),
                  pl.BlockSpec((te, LANE), lambda i, s: (i, 0)),
                  pl.BlockSpec((2, LANE), lambda i, s: (0, 0))],
        out_specs=(pl.BlockSpec((te, D), lambda i, s: (i, s)), pl.BlockSpec((te, LANE), lambda i, s: (i, 0))),
        out_shape=(jax.ShapeDtypeStruct((lp, 3 * D), F32), jax.ShapeDtypeStruct((lp, LANE), F32)),
        compiler_params=_cp(("parallel", "arbitrary")),
    )(proj, proj, conv_w, ba, ab)


def _tri_masks():
    row = lax.broadcasted_iota(jnp.int32, (C, C), 0)
    col = lax.broadcasted_iota(jnp.int32, (C, C), 1)
    return row, col


def _split(a):
    hi = a.astype(BF16)
    return hi, (a - hi.astype(F32)).astype(BF16)


def _dot3(a, b, dims=(((1,), (0,)), ((), ()))):
    (ah, al), (bh, bl) = a, b
    mm = lambda x, y: lax.dot_general(x, y, dims, preferred_element_type=F32)
    return mm(ah, bh) + (mm(ah, bl) + mm(al, bh))


def _tinv(ys, eye):
    ts = [eye + y for y in ys]
    sp = [_split(y) for y in ys]
    yks = [_dot3(s, s) for s in sp]
    for _ in range(4):
        sp = [_split(yk) for yk in yks]
        ts = [t + _dot3(s, _split(t)) for s, t in zip(sp, ts)]
        yks = [_dot3(s, s) for s in sp]
    return [t + _dot3(_split(yk), _split(t)) for yk, t in zip(yks, ts)]


def _chunk_common(q, k, v, bcol, gcc, gcr, incl, strict):
    dm = jnp.where(incl, jnp.exp(gcc - gcr), 0.0)
    kk = _dot_nt(k, k)
    qk = _dot_nt(q, k)
    egc = jnp.exp(gcc)
    glast = gcc[C - 1:C, :]
    eend = jnp.exp(glast - gcc)
    elast = jnp.exp(glast)
    rhs = jnp.concatenate([v * bcol, k * (bcol * egc)], axis=1)
    return dm, kk, qk, egc, eend, elast, rhs


def _delta_fwd(qkv, bg):
    lp = qkv.shape[0]
    nc = lp // C

    def body(q_ref, k_ref, v_ref, bg_ref, o_ref, sall_ref, tall_ref, s_scr):
        @pl.when(pl.program_id(0) == 0)
        def _():
            s_scr[...] = jnp.zeros_like(s_scr)

        bgt = bg_ref[...]
        row, col = _tri_masks()
        incl, strict = row >= col, row > col
        eye = (row == col).astype(F32)
        gc_all = _dot_hi(incl.astype(F32), bgt)
        gc_t = _dot_hi(bgt.T, (row <= col).astype(F32))
        heads = range(H)
        sls = [slice(h * DK, (h + 1) * DK) for h in heads]
        qs, ks, vs = ([r[:, sl] for sl in sls] for r in (q_ref, k_ref, v_ref))
        bcols = [bgt[:, h:h + 1] for h in heads]
        cm = [_chunk_common(qs[h], ks[h], vs[h], bcols[h], gc_all[:, H + h:H + h + 1], gc_t[H + h:H + h + 1, :],
                            incl, strict) for h in heads]
        dms, kks, qks, egcs, eends, elasts, rhss = zip(*cm)
        ts = _tinv([jnp.where(strict, -(bcols[h] * kks[h] * dms[h]), 0.0) for h in heads], eye)
        sols = [_dot3(_split(ts[h]), _split(rhss[h])) for h in heads]
        ss = [s_scr[h] for h in heads]
        sb = [s.astype(BF16) for s in ss]
        wvs = [sols[h][:, :DK] - _dot(sols[h][:, DK:], sb[h]) for h in heads]
        wvb = [wv.astype(BF16) for wv in wvs]
        for h in heads:
            o_ref[:, sls[h]] = _dot(qs[h] * egcs[h], sb[h]) + _dot(qks[h] * dms[h], wvb[h])
            sall_ref[0, h] = ss[h]
            tall_ref[0, h] = ts[h]
        for h in heads:
            s_scr[h] = ss[h] * elasts[h] + _dot_tn(ks[h] * eends[h], wvb[h])

    blk = lambda j: pl.BlockSpec((C, D), lambda n: (n, j))
    return _call(
        body, name="delta_fwd", grid=(nc,),
        in_specs=[blk(0), blk(1), blk(2), pl.BlockSpec((C, LANE), lambda n: (n, 0))],
        out_specs=(pl.BlockSpec((C, D), lambda n: (n, 0)),
                   pl.BlockSpec((1, H, DK, DK), lambda n: (n, 0, 0, 0)),
                   pl.BlockSpec((1, H, C, C), lambda n: (n, 0, 0, 0))),
        out_shape=(jax.ShapeDtypeStruct((lp, D), F32), jax.ShapeDtypeStruct((nc, H, DK, DK), F32),
                   jax.ShapeDtypeStruct((nc, H, C, C), F32)),
        scratch_shapes=[pltpu.VMEM((H, DK, DK), F32)],
        compiler_params=_cp(("arbitrary",)),
    )(qkv, qkv, qkv, bg)


def _delta_bwd(qkv, bg, sall, tall, do):
    lp = qkv.shape[0]
    nc = lp // C

    def body(q_ref, k_ref, v_ref, bg_ref, sall_ref, tall_ref, do_ref, dqkv_ref, dbg_ref, ds_scr):
        @pl.when(pl.program_id(0) == 0)
        def _():
            ds_scr[...] = jnp.zeros_like(ds_scr)

        bgt = bg_ref[...]
        row, col = _tri_masks()
        incl, strict = row >= col, row > col
        upper = (row <= col).astype(F32)
        gc_all = _dot_hi(incl.astype(F32), bgt)
        gc_t = _dot_hi(bgt.T, upper)
        lane = lax.broadcasted_iota(jnp.int32, (C, LANE), 1)
        lastrow = lax.broadcasted_iota(jnp.int32, (C, 1), 0) == C - 1
        heads = range(H)
        sls = [slice(h * DK, (h + 1) * DK) for h in heads]
        qs, ks, vs, dos = ([r[:, sl] for sl in sls] for r in (q_ref, k_ref, v_ref, do_ref))
        bcols = [bgt[:, h:h + 1] for h in heads]
        cm = [_chunk_common(qs[h], ks[h], vs[h], bcols[h], gc_all[:, H + h:H + h + 1], gc_t[H + h:H + h + 1, :],
                            incl, strict) for h in heads]
        dms, kks, qks, egcs, eends, elasts, rhss = zip(*cm)
        ss = [sall_ref[0, h] for h in heads]
        ts = [tall_ref[0, h] for h in heads]
        dsns = [ds_scr[h] for h in heads]
        sb = [s.astype(BF16) for s in ss]
        dsb = [d.astype(BF16) for d in dsns]
        dob = [d.astype(BF16) for d in dos]
        sols = [_dot3(_split(ts[h]), _split(rhss[h])) for h in heads]
        ws = [sol[:, DK:] for sol in sols]
        qgs = [qs[h] * egcs[h] for h in heads]
        kends = [ks[h] * eends[h] for h in heads]
        wvs = [sols[h][:, :DK] - _dot(ws[h], sb[h]) for h in heads]
        wvb = [wv.astype(BF16) for wv in wvs]
        dwvs = [_dot_tn(qks[h] * dms[h], dob[h]) + _dot(kends[h], dsb[h]) for h in heads]
        dps = [jnp.where(incl, _dot_nt(dob[h], wvb[h]), 0.0) for h in heads]
        dqgs = [_dot_nt(dob[h], sb[h]) for h in heads]
        dkends = [_dot_nt(wvb[h], dsb[h]) for h in heads]
        for h in heads:
            ds_scr[h] = _dot_tn(qgs[h], dob[h]) + elasts[h] * dsns[h] - _dot_tn(ws[h], dwvs[h])
        dglasts = [elasts[h] * jnp.sum(ss[h] * dsns[h], keepdims=True) for h in heads]
        dws = [-_dot_nt(dwvs[h], sb[h]) for h in heads]
        tts = [_split(ts[h].T) for h in heads]
        drhss = [_dot3(tts[h], _split(jnp.concatenate([dwvs[h], dws[h]], axis=1))) for h in heads]
        nt_dims = (((1,), (1,)), ((), ()))
        dns = [jnp.where(strict, -_dot3(_split(drhss[h]), _split(sols[h]), nt_dims), 0.0) for h in heads]
        dbeta_t = jnp.zeros((C, LANE), F32)
        dgc_t = jnp.zeros((C, LANE), F32)
        for h in heads:
            q, k, v, bcol, dm, kk, qk, egc, eend = qs[h], ks[h], vs[h], bcols[h], dms[h], kks[h], qks[h], egcs[h], eends[h]
            drv, drk = drhss[h][:, :DK], drhss[h][:, DK:]
            dn, dp, dqg, dkend = dns[h], dps[h], dqgs[h], dkends[h]
            rk = _rowsum(drk * k)
            dkk = dn * (bcol * dm)
            dqk = dp * dm
            e = (dn * (bcol * kk) + dp * qk) * dm
            tk = _rowsum(dkend * kends[h])
            dgc = rk * bcol * egc + _rowsum(e) - _rowsum(e.T) + _rowsum(dqg * qgs[h]) - tk
            dgc = dgc + jnp.where(lastrow, dglasts[h] + jnp.sum(tk, keepdims=True), 0.0)
            dbeta = _rowsum(drv * v) + rk * egc + _rowsum(dn * kk * dm)
            dqkv_ref[:, sls[h]] = _dot(dqk, k) + dqg * egc
            dqkv_ref[:, D + h * DK:D + (h + 1) * DK] = (drk * (bcol * egc) + _dot(dkk, k) + _dot_tn(dkk, k)
                                                       + _dot_tn(dqk, q) + dkend * eend)
            dqkv_ref[:, 2 * D + h * DK:2 * D + (h + 1) * DK] = bcol * drv
            dbeta_t = jnp.where(lane == h, dbeta, dbeta_t)
            dgc_t = jnp.where(lane == H + h, dgc, dgc_t)
        dbg_ref[...] = dbeta_t + _dot_hi(upper, dgc_t)

    rev = lambda n: nc - 1 - n
    blk = lambda j: pl.BlockSpec((C, D), lambda n: (rev(n), j))
    return _call(
        body, name="delta_bwd", grid=(nc,),
        in_specs=[blk(0), blk(1), blk(2), pl.BlockSpec((C, LANE), lambda n: (rev(n), 0)),
                  pl.BlockSpec((1, H, DK, DK), lambda n: (rev(n), 0, 0, 0)),
                  pl.BlockSpec((1, H, C, C), lambda n: (rev(n), 0, 0, 0)),
                  pl.BlockSpec((C, D), lambda n: (rev(n), 0))],
        out_specs=(pl.BlockSpec((C, 3 * D), lambda n: (rev(n), 0)), pl.BlockSpec((C, LANE), lambda n: (rev(n), 0))),
        out_shape=(jax.ShapeDtypeStruct((lp, 3 * D), F32), jax.ShapeDtypeStruct((lp, LANE), F32)),
        scratch_shapes=[pltpu.VMEM((H, DK, DK), F32)],
        compiler_params=_cp(("arbitrary",)),
    )(qkv, qkv, qkv, bg, sall, tall, do)


def _o_post_fwd(o, proj, dn_w):
    lp = o.shape[0]
    te = _pick(lp, (640, 320))

    def body(o_ref, za_ref, w_ref, out_ref):
        za = za_ref[...]
        gate = za * _sig(za)
        for h in range(H):
            sl = slice(h * DK, (h + 1) * DK)
            oh = o_ref[:, sl]
            r = lax.rsqrt(jnp.mean(oh * oh, axis=-1, keepdims=True) + EPS)
            out_ref[:, sl] = (oh * r * w_ref[...] * gate[:, sl]).astype(BF16)

    return _call(
        body, name="o_post_fwd", grid=(lp // te,),
        in_specs=[pl.BlockSpec((te, D), lambda i: (i, 0)), pl.BlockSpec((te, D), lambda i: (i, CB_ZA)),
                  pl.BlockSpec((1, DK), lambda i: (0, 0))],
        out_specs=pl.BlockSpec((te, D), lambda i: (i, 0)),
        out_shape=jax.ShapeDtypeStruct((lp, D), BF16),
        compiler_params=_cp(("parallel",)),
    )(o, proj, dn_w)


def _o_post_bwd(do_n, o, proj, dn_w, dproj):
    lp = o.shape[0]
    te = _pick(lp, (640, 320))

    def body(don_ref, o_ref, za_ref, w_ref, _, do_ref, dza_ref, dw_ref):
        @pl.when(pl.program_id(0) == 0)
        def _():
            dw_ref[...] = jnp.zeros_like(dw_ref)

        za = za_ref[...]
        sz = _sig(za)
        gate, dgate = za * sz, _dsilu(za, sz)
        w = w_ref[...]
        dw = jnp.zeros((1, DK), F32)
        for h in range(H):
            sl = slice(h * DK, (h + 1) * DK)
            oh, g = o_ref[:, sl], don_ref[:, sl]
            r = lax.rsqrt(jnp.mean(oh * oh, axis=-1, keepdims=True) + EPS)
            ohat = oh * r
            dza_ref[:, sl] = (g * ohat * w * dgate[:, sl]).astype(BF16)
            don = g * gate[:, sl]
            dw = dw + _colsum(don * ohat)
            dohat = don * w
            do_ref[:, sl] = r * (dohat - ohat * jnp.mean(dohat * ohat, axis=-1, keepdims=True))
        dw_ref[...] += dw

    return _call(
        body, name="o_post_bwd", grid=(lp // te,),
        in_specs=[pl.BlockSpec((te, D), lambda i: (i, 0)), pl.BlockSpec((te, D), lambda i: (i, 0)),
                  pl.BlockSpec((te, D), lambda i: (i, CB_ZA)), pl.BlockSpec((1, DK), lambda i: (0, 0)),
                  pl.BlockSpec(memory_space=pl.ANY)],
        out_specs=(pl.BlockSpec((te, D), lambda i: (i, 0)), pl.BlockSpec((te, D), lambda i: (i, CB_ZA)),
                   pl.BlockSpec((1, DK), lambda i: (0, 0))),
        out_shape=(jax.ShapeDtypeStruct((lp, D), F32), jax.ShapeDtypeStruct(dproj.shape, dproj.dtype),
                   jax.ShapeDtypeStruct((1, DK), F32)),
        input_output_aliases={4: 1},
        compiler_params=_cp(("arbitrary",)),
    )(do_n, o, proj, dn_w, dproj)


def _qkv_conv_bwd(proj, dqkv, conv_w, dproj):
    lp = proj.shape[0]
    te = _pick(lp, (320,))
    hb = te // HALO_Q
    nt = lp // te
    last_hb = lp // HALO_Q - 1

    def body(main_ref, prev_ref, next_ref, dmain_ref, dnext_ref, cw_ref, _, dpre_ref, dcw_ref):
        s, i = pl.program_id(0), pl.program_id(1)

        @pl.when(i == 0)
        def _():
            dcw_ref[...] = jnp.zeros_like(dcw_ref)

        prev = jnp.where(i > 0, prev_ref[...], 0.0)
        nxt = jnp.where(i < nt - 1, next_ref[...], 0.0)
        dnxt = jnp.where(i < nt - 1, dnext_ref[...], 0.0)
        pre = jnp.concatenate([prev, main_ref[...], nxt], axis=0)
        dn = jnp.concatenate([dmain_ref[...], dnxt], axis=0)
        cw = cw_ref[...]
        ne = te + HALO_Q
        co = cw[0:1, :] * pre[HALO_Q - 3:HALO_Q - 3 + ne, :]
        for j in range(1, KQ):
            co = co + cw[j:j + 1, :] * pre[HALO_Q - 3 + j:HALO_Q - 3 + j + ne, :]
        sg = _sig(co)
        a = co * sg
        scale = jnp.where(s == 0, DK ** -0.5, 1.0)
        das = []
        for h in range(H):
            sl = slice(h * DK, (h + 1) * DK)
            blk, g = a[:, sl], dn[:, sl]
            r = lax.rsqrt(_rowsum(blk * blk) + EPS)
            yhat = blk * r
            das.append(scale * r * (g - yhat * _rowsum(g * yhat)))
        da = jnp.where(s == 2, dn, jnp.concatenate(das, axis=1))
        dco = da * _dsilu(co, sg)
        dpre = cw[0:1, :] * dco[3:3 + te, :]
        for j in range(1, KQ):
            dpre = dpre + cw[j:j + 1, :] * dco[3 - j:3 - j + te, :]
        dpre_ref[...] = dpre.astype(BF16)
        dcw_ref[...] += jnp.concatenate(
            [_colsum(dco[:te, :] * pre[HALO_Q - 3 + j:HALO_Q - 3 + j + te, :]) for j in range(KQ)], axis=0)

    return _call(
        body, name="qkv_conv_bwd", grid=(3, nt),
        in_specs=[pl.BlockSpec((te, D), lambda s, i: (i, s)),
                  pl.BlockSpec((HALO_Q, D), lambda s, i: (jnp.maximum(i * hb - 1, 0), s)),
                  pl.BlockSpec((HALO_Q, D), lambda s, i: (jnp.minimum((i + 1) * hb, last_hb), s)),
                  pl.BlockSpec((te, D), lambda s, i: (i, s)),
                  pl.BlockSpec((HALO_Q, D), lambda s, i: (jnp.minimum((i + 1) * hb, last_hb), s)),
                  pl.BlockSpec((KQ, D), lambda s, i: (0, s)),
                  pl.BlockSpec(memory_space=pl.ANY)],
        out_specs=(pl.BlockSpec((te, D), lambda s, i: (i, s)), pl.BlockSpec((KQ, D), lambda s, i: (0, s))),
        out_shape=(jax.ShapeDtypeStruct(dproj.shape, dproj.dtype), jax.ShapeDtypeStruct((KQ, 3 * D), F32)),
        input_output_aliases={6: 0},
        compiler_params=_cp(("arbitrary", "arbitrary")),
    )(proj, proj, proj, dqkv, dqkv, conv_w, dproj)


def _ba_bwd(dbg, ba, ab, pad):
    lp = ba.shape[0]
    te = _pick(lp, (640, 320))

    def body(dbg_ref, ba_ref, ab_ref, dba_ref, dab_ref):
        i = pl.program_id(0)

        @pl.when(i == 0)
        def _():
            dab_ref[...] = jnp.zeros_like(dab_ref)

        ba, ab = ba_ref[...], ab_ref[...]
        lane = lax.broadcasted_iota(jnp.int32, ba.shape, 1)
        rows = i * te + lax.broadcasted_iota(jnp.int32, ba.shape, 0)
        g = jnp.where((lane < 2 * H) & (rows >= pad), dbg_ref[...], 0.0)
        sb = _sig(ba)
        z = ba + ab[1:2, :]
        sp = jnp.maximum(z, 0.0) + jnp.log(1.0 + jnp.exp(-jnp.abs(z)))
        nea = -jnp.exp(ab[0:1, :])
        dz = g * nea * _sig(z)
        dba_ref[...] = jnp.where(lane < H, g * sb * (1.0 - sb), dz).astype(BF16)
        is_g = (lane >= H) & (lane < 2 * H)
        dab_ref[...] += jnp.concatenate([_colsum(jnp.where(is_g, g * nea * sp, 0.0)),
                                         _colsum(jnp.where(is_g, dz, 0.0))], axis=0)

    return _call(
        body, name="ba_bwd", grid=(lp // te,),
        in_specs=[pl.BlockSpec((te, LANE), lambda i: (i, 0)), pl.BlockSpec((te, LANE), lambda i: (i, 0)),
                  pl.BlockSpec((2, LANE), lambda i: (0, 0))],
        out_specs=(pl.BlockSpec((te, LANE), lambda i: (i, 0)), pl.BlockSpec((2, LANE), lambda i: (0, 0))),
        out_shape=(jax.ShapeDtypeStruct((lp, LANE), BF16), jax.ShapeDtypeStruct((2, LANE), F32)),
        compiler_params=_cp(("arbitrary",)),
    )(dbg, ba, ab)


SUBLANES = 8
CONV_RB = 64


def _fill_shifted(sh_scr, src_scr, cs):
    n = sh_scr.shape[1]
    for s in range(1, SUBLANES):
        sh_scr[s] = src_scr[s:s + n, cs]


def _shifted(sh_scr, src_scr, cs, r, r0, n):
    s, a8 = r % SUBLANES, r - r % SUBLANES
    if s == 0:
        return src_scr[r0 + a8:r0 + a8 + n, cs]
    return sh_scr[s, r0 + a8:r0 + a8 + n, :]


def _conv_b_fwd(proj, dw_w, dw_b, ln_w, ln_b):
    lp = proj.shape[0]
    te = _pick(lp, (320,))
    hb = te // HALO_D

    def body(a_ref, b_ref, ha_ref, hb_ref, zb_ref, w_ref, wb_ref, lw_ref, lb_ref, c1_ref, c3_ref, c0_scr, sh_scr):
        i = pl.program_id(0)
        c0_scr[:HALO_D, :] = jnp.where(i > 0, ha_ref[...] * _sig(hb_ref[...]), 0.0)
        c0_scr[HALO_D:, :] = a_ref[...] * _sig(b_ref[...])
        off = HALO_D - (KD - 1)
        def lane_block(cb, carry):
            cs = pl.ds(pl.multiple_of(cb * LANE, LANE), LANE)
            _fill_shifted(sh_scr, c0_scr, cs)
            for r0 in range(0, te, CONV_RB):
                acc = None
                for j in range(KD):
                    term = w_ref[j:j + 1, cs] * _shifted(sh_scr, c0_scr, cs, off + j, r0, CONV_RB)
                    acc = term if acc is None else acc + term
                c1_ref[r0:r0 + CONV_RB, cs] = acc + wb_ref[:, cs]
            return carry

        lax.fori_loop(0, D // LANE, lane_block, 0)
        c1 = c1_ref[...]
        mu = jnp.mean(c1, axis=-1, keepdims=True)
        xc = c1 - mu
        c2 = xc * lax.rsqrt(jnp.mean(xc * xc, axis=-1, keepdims=True) + EPS) * lw_ref[...] + lb_ref[...]
        zb = zb_ref[...]
        c3_ref[...] = (c2 * _sig(c2) * zb * _sig(zb)).astype(BF16)

    vec = pl.BlockSpec((1, D), lambda i: (0, 0))
    return _call(
        body, name="conv_b_fwd", grid=(lp // te,),
        in_specs=[pl.BlockSpec((te, D), lambda i: (i, CB_GA_)), pl.BlockSpec((te, D), lambda i: (i, CB_GB_)),
                  pl.BlockSpec((HALO_D, D), lambda i: (jnp.maximum(i * hb - 1, 0), CB_GA_)),
                  pl.BlockSpec((HALO_D, D), lambda i: (jnp.maximum(i * hb - 1, 0), CB_GB_)),
                  pl.BlockSpec((te, D), lambda i: (i, CB_ZB)),
                  pl.BlockSpec((KD, D), lambda i: (0, 0)), vec, vec, vec],
        out_specs=(pl.BlockSpec((te, D), lambda i: (i, 0)), pl.BlockSpec((te, D), lambda i: (i, 0))),
        out_shape=(jax.ShapeDtypeStruct((lp, D), F32), jax.ShapeDtypeStruct((lp, D), BF16)),
        scratch_shapes=[pltpu.VMEM((te + HALO_D, D), F32), pltpu.VMEM((SUBLANES, te + HALO_D - SUBLANES, LANE), F32)],
        compiler_params=_cp(("parallel",)),
    )(proj, proj, proj, proj, proj, dw_w, dw_b, ln_w, ln_b)


def _conv_b_bwd1(dc3, c1, proj, ln_w, ln_b, dproj):
    lp = c1.shape[0]
    te = _pick(lp, (640, 320))

    def body(dc3_ref, c1_ref, zb_ref, lw_ref, lb_ref, _, dc1_ref, dzb_ref, sums_ref):
        @pl.when(pl.program_id(0) == 0)
        def _():
            sums_ref[...] = jnp.zeros_like(sums_ref)

        c1, g = c1_ref[...], dc3_ref[...]
        mu = jnp.mean(c1, axis=-1, keepdims=True)
        xc = c1 - mu
        rstd = lax.rsqrt(jnp.mean(xc * xc, axis=-1, keepdims=True) + EPS)
        xh = xc * rstd
        lw = lw_ref[...]
        c2 = xh * lw + lb_ref[...]
        s2 = _sig(c2)
        zb = zb_ref[...]
        sz = _sig(zb)
        dc2 = g * (zb * sz) * _dsilu(c2, s2)
        dzb_ref[...] = (g * (c2 * s2) * _dsilu(zb, sz)).astype(BF16)
        dxh = dc2 * lw
        dc1 = rstd * (dxh - jnp.mean(dxh, axis=-1, keepdims=True) - xh * jnp.mean(dxh * xh, axis=-1, keepdims=True))
        dc1_ref[...] = dc1
        sums_ref[...] += jnp.concatenate([_colsum(dc2 * xh), _colsum(dc2), _colsum(dc1)], axis=0)

    vec = pl.BlockSpec((1, D), lambda i: (0, 0))
    return _call(
        body, name="conv_b_bwd1", grid=(lp // te,),
        in_specs=[pl.BlockSpec((te, D), lambda i: (i, 0)), pl.BlockSpec((te, D), lambda i: (i, 0)),
                  pl.BlockSpec((te, D), lambda i: (i, CB_ZB)), vec, vec, pl.BlockSpec(memory_space=pl.ANY)],
        out_specs=(pl.BlockSpec((te, D), lambda i: (i, 0)), pl.BlockSpec((te, D), lambda i: (i, CB_ZB)),
                   pl.BlockSpec((3, D), lambda i: (0, 0))),
        out_shape=(jax.ShapeDtypeStruct((lp, D), F32), jax.ShapeDtypeStruct(dproj.shape, dproj.dtype),
                   jax.ShapeDtypeStruct((3, D), F32)),
        input_output_aliases={5: 1},
        compiler_params=_cp(("arbitrary",)),
    )(dc3, c1, proj, ln_w, ln_b, dproj)


def _conv_b_bwd2(dc1, proj, dw_w, dproj):
    lp = dc1.shape[0]
    te = _pick(lp, (320,))
    hb = te // HALO_D
    nt = lp // te
    last_hb = lp // HALO_D - 1

    def body(g_ref, gn_ref, a_ref, b_ref, ha_ref, hb_ref, w_ref, _, dab_ref, dw_ref, c0_scr, g_scr, dc0_scr,
             csh_scr, gsh_scr):
        i = pl.program_id(0)

        @pl.when(i == 0)
        def _():
            dw_ref[...] = jnp.zeros_like(dw_ref)

        a, b = a_ref[...], b_ref[...]
        sb = _sig(b)
        c0_scr[:HALO_D, :] = jnp.where(i > 0, ha_ref[...] * _sig(hb_ref[...]), 0.0)
        c0_scr[HALO_D:, :] = a * sb
        g_scr[:te, :] = g_ref[...]
        g_scr[te:, :] = jnp.where(i < nt - 1, gn_ref[...], 0.0)
        off = HALO_D - (KD - 1)
        def lane_block(cb, carry):
            cs = pl.ds(pl.multiple_of(cb * LANE, LANE), LANE)
            _fill_shifted(csh_scr, c0_scr, cs)
            _fill_shifted(gsh_scr, g_scr, cs)
            for r0 in range(0, te, CONV_RB):
                acc = None
                for j in range(KD):
                    term = w_ref[j:j + 1, cs] * _shifted(gsh_scr, g_scr, cs, KD - 1 - j, r0, CONV_RB)
                    acc = term if acc is None else acc + term
                dc0_scr[r0:r0 + CONV_RB, cs] = acc
            parts = [None] * KD
            for r0 in range(0, te, CONV_RB):
                g = g_scr[r0:r0 + CONV_RB, cs].reshape(CONV_RB // SUBLANES, SUBLANES, LANE)
                for j in range(KD):
                    x = _shifted(csh_scr, c0_scr, cs, off + j, r0, CONV_RB)
                    p = jnp.sum(g * x.reshape(CONV_RB // SUBLANES, SUBLANES, LANE), axis=0)
                    parts[j] = p if parts[j] is None else parts[j] + p
            dw_ref[:, cs] += jnp.concatenate([_colsum(p) for p in parts], axis=0)
            return carry

        lax.fori_loop(0, D // LANE, lane_block, 0)
        dc0 = dc0_scr[...]
        dab_ref[:, :D] = (dc0 * sb).astype(BF16)
        dab_ref[:, D:] = (dc0 * a * sb * (1.0 - sb)).astype(BF16)

    return _call(
        body, name="conv_b_bwd2", grid=(nt,),
        in_specs=[pl.BlockSpec((te, D), lambda i: (i, 0)),
                  pl.BlockSpec((HALO_D, D), lambda i: (jnp.minimum((i + 1) * hb, last_hb), 0)),
                  pl.BlockSpec((te, D), lambda i: (i, CB_GA_)), pl.BlockSpec((te, D), lambda i: (i, CB_GB_)),
                  pl.BlockSpec((HALO_D, D), lambda i: (jnp.maximum(i * hb - 1, 0), CB_GA_)),
                  pl.BlockSpec((HALO_D, D), lambda i: (jnp.maximum(i * hb - 1, 0), CB_GB_)),
                  pl.BlockSpec((KD, D), lambda i: (0, 0)), pl.BlockSpec(memory_space=pl.ANY)],
        out_specs=(pl.BlockSpec((te, 2 * D), lambda i: (i, CB_GA_ // 2)), pl.BlockSpec((KD, D), lambda i: (0, 0))),
        out_shape=(jax.ShapeDtypeStruct(dproj.shape, dproj.dtype), jax.ShapeDtypeStruct((KD, D), F32)),
        input_output_aliases={7: 0},
        scratch_shapes=[pltpu.VMEM((te + HALO_D, D), F32), pltpu.VMEM((te + HALO_D, D), F32), pltpu.VMEM((te, D), F32),
                        pltpu.VMEM((SUBLANES, te + HALO_D - SUBLANES, LANE), F32),
                        pltpu.VMEM((SUBLANES, te + HALO_D - SUBLANES, LANE), F32)],
        compiler_params=_cp(("arbitrary",)),
    )(dc1, dc1, proj, proj, proj, proj, dw_w, dproj)


def _merge_fwd(y_a, y_b, proj, b_cf):
    lp = y_a.shape[0]
    te = _pick(lp, (640, 320))

    def body(ya_ref, yb_ref, ga_ref, gb_ref, bias_ref, out_ref):
        out_ref[...] = (_sig(ga_ref[...]) * ya_ref[...] + _sig(gb_ref[...]) * (yb_ref[...] + bias_ref[...])).astype(BF16)

    row = lambda j: pl.BlockSpec((te, D), lambda i: (i, j))
    return _call(
        body, name="merge_fwd", grid=(lp // te,),
        in_specs=[row(0), row(0), row(CB_MA), row(CB_MB), pl.BlockSpec((1, D), lambda i: (0, 0))],
        out_specs=row(0), out_shape=jax.ShapeDtypeStruct((lp, D), BF16),
        compiler_params=_cp(("parallel",)),
    )(y_a, y_b, proj, proj, b_cf)


def _merge_bwd(dmerged, y_a, y_b, proj, b_cf):
    lp = y_a.shape[0]
    te = _pick(lp, (640, 320))

    def body(dm_ref, ya_ref, yb_ref, ga_ref, gb_ref, bias_ref, dya_ref, dyb_ref, dg_ref, db_ref):
        @pl.when(pl.program_id(0) == 0)
        def _():
            db_ref[...] = jnp.zeros_like(db_ref)

        dm = dm_ref[...]
        sa, sb = _sig(ga_ref[...]), _sig(gb_ref[...])
        dyb = sb * dm
        dya_ref[...] = (sa * dm).astype(BF16)
        dyb_ref[...] = dyb.astype(BF16)
        dg_ref[:, :D] = (dm * ya_ref[...] * sa * (1.0 - sa)).astype(BF16)
        dg_ref[:, D:] = (dm * (yb_ref[...] + bias_ref[...]) * sb * (1.0 - sb)).astype(BF16)
        db_ref[...] += _colsum(dyb)

    row = lambda j: pl.BlockSpec((te, D), lambda i: (i, j))
    act = jax.ShapeDtypeStruct((lp, D), BF16)
    return _call(
        body, name="merge_bwd", grid=(lp // te,),
        in_specs=[row(0), row(0), row(0), row(CB_MA), row(CB_MB), pl.BlockSpec((1, D), lambda i: (0, 0))],
        out_specs=(row(0), row(0), pl.BlockSpec((te, 2 * D), lambda i: (i, CB_MA // 2)),
                   pl.BlockSpec((1, D), lambda i: (0, 0))),
        out_shape=(act, act, jax.ShapeDtypeStruct((lp, NCB * D), BF16), jax.ShapeDtypeStruct((1, D), F32)),
        compiler_params=_cp(("arbitrary",)),
    )(dmerged, y_a, y_b, proj, proj, b_cf)


def _final_fwd_bwd(x_ext, z, target, final_w):
    lp = x_ext.shape[0]
    te = LANE

    def body(x_ref, z_ref, t_ref, w_ref, dx_ref, loss_ref, dw_ref):
        i = pl.program_id(0)

        @pl.when(i == 0)
        def _():
            loss_ref[...] = jnp.zeros_like(loss_ref)
            dw_ref[...] = jnp.zeros_like(dw_ref)

        xo = x_ref[...] + z_ref[...]
        r = lax.rsqrt(jnp.mean(xo * xo, axis=-1, keepdims=True) + EPS)
        xhat = xo * r
        w = w_ref[...]
        err = jnp.where(i > 0, xhat * w - t_ref[...], 0.0)
        loss_ref[...] += 0.5 * jnp.sum(jnp.mean(err * err, axis=-1, keepdims=True), keepdims=True)
        dy = err * (1.0 / D)
        dw_ref[...] += _colsum(dy * xhat)
        dxn = dy * w
        dx_ref[...] = r * (dxn - xhat * jnp.mean(dxn * xhat, axis=-1, keepdims=True))

    return _call(
        body, name="final_fwd_bwd", grid=(lp // te,),
        in_specs=[pl.BlockSpec((te, D), lambda i: (i, 0)), pl.BlockSpec((te, D), lambda i: (i, 0)),
                  pl.BlockSpec((te, D), lambda i: (jnp.maximum(i - 1, 0), 0)), pl.BlockSpec((1, D), lambda i: (0, 0))],
        out_specs=(pl.BlockSpec((te, D), lambda i: (i, 0)), pl.BlockSpec((1, 1), lambda i: (0, 0)),
                   pl.BlockSpec((1, D), lambda i: (0, 0))),
        out_shape=(jax.ShapeDtypeStruct((lp, D), F32), jax.ShapeDtypeStruct((1, 1), F32),
                   jax.ShapeDtypeStruct((1, D), F32)),
        compiler_params=_cp(("arbitrary",)),
    )(x_ext, z, target, final_w)


def _prenorm_bwd(dh, x_ext, dx_out, norm_w, seq):
    lp = x_ext.shape[0]
    te = LANE

    def body(dh_ref, x_ref, dxo_ref, w_ref, gx_ref, head_ref, dw_ref):
        i = pl.program_id(0)

        @pl.when(i == 0)
        def _():
            dw_ref[...] = jnp.zeros_like(dw_ref)

        x, dh = x_ref[...], dh_ref[...]
        r = lax.rsqrt(jnp.mean(x * x, axis=-1, keepdims=True) + EPS)
        xhat = x * r
        dxn = dh * w_ref[...]
        dx = dxo_ref[...] + r * (dxn - xhat * jnp.mean(dxn * xhat, axis=-1, keepdims=True))
        dw_ref[...] += _colsum(dh * xhat)

        @pl.when(i == 0)
        def _():
            head_ref[...] = dx

        @pl.when(i > 0)
        def _():
            gx_ref[...] = dx

    row = pl.BlockSpec((te, D), lambda i: (i, 0))
    return _call(
        body, name="prenorm_bwd", grid=(lp // te,),
        in_specs=[row, row, row, pl.BlockSpec((1, D), lambda i: (0, 0))],
        out_specs=(pl.BlockSpec((te, D), lambda i: (jnp.maximum(i - 1, 0), 0)), pl.BlockSpec((te, D), lambda i: (0, 0)),
                   pl.BlockSpec((1, D), lambda i: (0, 0))),
        out_shape=(jax.ShapeDtypeStruct((seq, D), F32), jax.ShapeDtypeStruct((te, D), F32),
                   jax.ShapeDtypeStruct((1, D), F32)),
        compiler_params=_cp(("arbitrary",)),
    )(dh, x_ext, dx_out, norm_w)


def _adam_reduce(parts, w, m, v, name):
    r, n = w.shape
    tr = _pick(r, (128,)) if r % 128 == 0 else r

    def body(p_ref, w_ref, m_ref, v_ref, g_ref, d_ref, m2_ref, v2_ref):
        g = p_ref[0]
        for s in range(1, NDEV):
            g = g + p_ref[s]
        _adam_write(g, w_ref, m_ref, v_ref, g_ref, d_ref, m2_ref, v2_ref)

    blk = pl.BlockSpec((tr, n), lambda i: (i, 0))
    out = jax.ShapeDtypeStruct((r, n), F32)
    return _call(
        body, name=name, grid=(r // tr,),
        in_specs=[pl.BlockSpec((NDEV, tr, n), lambda i: (0, i, 0)), blk, blk, blk],
        out_specs=(blk, blk, blk, blk), out_shape=(out, out, out, out),
        compiler_params=_cp(("parallel",)),
    )(parts, w, m, v)


def _adam_write(g, w_ref, m_ref, v_ref, g_ref, d_ref, m2_ref, v2_ref):
    c1 = 1.0 - ADAM_B1 ** ADAM_STEP
    c2 = 1.0 - ADAM_B2 ** ADAM_STEP
    m2 = ADAM_B1 * m_ref[...] + (1.0 - ADAM_B1) * g
    v2 = ADAM_B2 * v_ref[...] + (1.0 - ADAM_B2) * (g * g)
    g_ref[...] = g
    m2_ref[...] = m2
    v2_ref[...] = v2
    d_ref[...] = -ADAM_LR * ((m2 / c1) / (jnp.sqrt(v2 / c2) + ADAM_EPS) + ADAM_WD * w_ref[...])


def _adam_chips(own, recv, w, m, v, name):
    r, n = w.shape
    tr = _pick(r, (128,))

    def body(own_ref, p_ref, w_ref, m_ref, v_ref, g_ref, d_ref, m2_ref, v2_ref):
        my_chip = 2 * lax.axis_index("x") + lax.axis_index("y")
        g = None
        for j in range(NCHIP):
            part = jnp.where(my_chip == j, own_ref[...], p_ref[j].astype(F32))
            g = part if g is None else g + part
        _adam_write(g, w_ref, m_ref, v_ref, g_ref, d_ref, m2_ref, v2_ref)

    blk = pl.BlockSpec((tr, n), lambda i: (i, 0))
    out = jax.ShapeDtypeStruct((r, n), F32)
    return _call(
        body, name=name, grid=(r // tr,),
        in_specs=[blk, pl.BlockSpec((NCHIP, tr, n), lambda i: (0, i, 0)), blk, blk, blk],
        out_specs=(blk, blk, blk, blk), out_shape=(out, out, out, out),
        compiler_params=_cp(("parallel",)),
    )(own, recv, w, m, v)


SMALL = ("norm_w", "a_log", "dt_bias", "dn_norm_w", "dw_b", "ln_w", "ln_b", "b_cf_out", "final_norm_w")


def kernel(x, meta, norm_w, w_in, conv_qkv_w, a_log, dt_bias, dn_norm_w, w_dn_out, dw_w, dw_b, ln_w, ln_b, w_cf_out, b_cf_out, w_o, final_norm_w, loss_target, m_meta, m_norm_w, m_w_in, m_conv_qkv_w, m_a_log, m_dt_bias, m_dn_norm_w, m_w_dn_out, m_dw_w, m_dw_b, m_ln_w, m_ln_b, m_w_cf_out, m_b_cf_out, m_w_o, m_final_norm_w, v_meta, v_norm_w, v_w_in, v_conv_qkv_w, v_a_log, v_dt_bias, v_dn_norm_w, v_w_dn_out, v_dw_w, v_dw_b, v_ln_w, v_ln_b, v_w_cf_out, v_b_cf_out, v_w_o, v_final_norm_w):
    seq = x.shape[1]
    pad = (-(seq + NMETA)) % LANE
    in_w = w_in.shape[2] * NDEV
    n_qkvz = 4 * D
    n_ba = 2 * H

    w_in_g, w_dn_g, w_cf_g, w_o_g, meta_g, cqw_g, dww_g = _gather_two_level(
        [w_in[0].astype(BF16), w_dn_out[0].astype(BF16), w_cf_out[0].astype(BF16), w_o[0].astype(BF16),
         meta, conv_qkv_w[0], dw_w[0]], "gather_weights")
    w_full = jnp.transpose(w_in_g, (1, 0, 2)).reshape(D, in_w)
    c_glu = n_qkvz + n_ba
    c_zb, c_mg = c_glu + 2 * D, c_glu + 3 * D
    w_main = jnp.concatenate([w_full[:, :n_qkvz], w_full[:, c_glu:c_zb], w_full[:, c_mg:], w_full[:, c_zb:c_mg]],
                             axis=1)
    w_ba = jnp.pad(w_full[:, n_qkvz:n_qkvz + n_ba], ((0, 0), (0, LANE - n_ba)))
    w_dn, w_cf, w_oo = (t.reshape(D, D) for t in (w_dn_g, w_cf_g, w_o_g))
    meta_full = jnp.transpose(meta_g, (1, 0, 2)).reshape(NMETA, D)
    cqw = jnp.transpose(cqw_g, (1, 0, 2)).reshape(KQ, 3 * D)
    dww = jnp.transpose(dww_g, (1, 0, 2)).reshape(KD, D)
    ab = jnp.pad(jnp.concatenate([a_log, dt_bias], axis=0), ((0, 0), (H, LANE - 2 * H)))

    x_ext = jnp.concatenate([jnp.zeros((pad, D), F32), meta_full, x[0]], axis=0)

    proj, ba, h = _proj_fwd(x_ext, norm_w, w_main, w_ba)
    qkv, bg = _qkv_conv_fwd(proj, ba, cqw, ab, pad)
    o, sall, tall = _delta_fwd(qkv, bg)
    o_n = _o_post_fwd(o, proj, dn_norm_w)
    y_a = _mm(o_n, w_dn, "y_a_mm")
    c1, c3 = _conv_b_fwd(proj, dww, dw_b, ln_w, ln_b)
    y_b = _mm(c3, w_cf, "y_b_mm")
    merged = _merge_fwd(y_a, y_b, proj, b_cf_out)
    z = _mm(merged, w_oo, "z_mm")
    dx_out, loss_part, g_final_w = _final_fwd_bwd(x_ext, z, loss_target[0], final_norm_w.reshape(1, D))

    dx_out_b = dx_out.astype(BF16)
    dmerged = _mm(dx_out_b, w_oo.T, "dmerged_mm")
    g_w_o = _mm_tn(merged, dx_out_b, "g_w_o_mm")
    dy_a, dy_b, dproj, g_b_cf = _merge_bwd(dmerged, y_a, y_b, proj, b_cf_out)
    dc3 = _mm(dy_b, w_cf.T, "dc3_mm")
    g_w_cf = _mm_tn(c3, dy_b, "g_w_cf_mm")
    do_n = _mm(dy_a, w_dn.T, "do_n_mm")
    g_w_dn = _mm_tn(o_n, dy_a, "g_w_dn_mm")
    dc1, dproj, sums_b = _conv_b_bwd1(dc3, c1, proj, ln_w, ln_b, dproj)
    dproj, g_dw_w = _conv_b_bwd2(dc1, proj, dww, dproj)
    do, dproj, g_dn_w = _o_post_bwd(do_n, o, proj, dn_norm_w, dproj)
    dqkv, dbg = _delta_bwd(qkv, bg, sall, tall, do)
    dproj, g_cqw = _qkv_conv_bwd(proj, dqkv, cqw, dproj)
    dba, dab = _ba_bwd(dbg, ba, ab, pad)
    dh = _dh_mm(dproj, dba, w_main.T, w_ba.T)
    g_w_main = _mm_tn(h, dproj, "g_w_main_mm")
    g_w_ba = _mm_tn(h, dba, "g_w_ba_mm")
    grad_x, dhead, g_norm_w = _prenorm_bwd(dh, x_ext, dx_out, norm_w, seq)

    g_w_full = jnp.concatenate([g_w_main[:, :n_qkvz], g_w_ba[:, :n_ba], g_w_main[:, CB_GA_ * D:CB_MA * D],
                                g_w_main[:, CB_ZB * D:], g_w_main[:, CB_MA * D:CB_ZB * D]], axis=1)
    split_cols = lambda t: jnp.transpose(t.reshape(t.shape[0], NDEV, t.shape[1] // NDEV), (1, 0, 2))
    small = {"norm_w": g_norm_w, "a_log": dab[0:1, H:2 * H], "dt_bias": dab[1:2, H:2 * H], "dn_norm_w": g_dn_w,
             "dw_b": sums_b[2:3], "ln_w": sums_b[0:1], "ln_b": sums_b[1:2], "b_cf_out": g_b_cf,
             "final_norm_w": g_final_w}
    small_vec = jnp.concatenate([small[k] for k in SMALL], axis=1)
    ns = small_vec.shape[1]
    ns_pad = (-ns) % LANE
    small_vec = jnp.pad(small_vec, ((0, 0), (0, ns_pad)))
    big = [split_cols(g_w_full), g_w_dn.reshape(NDEV, D // NDEV, D), g_w_cf.reshape(NDEV, D // NDEV, D),
           g_w_o.reshape(NDEV, D // NDEV, D)]
    from_sibling = _swap_sibling(big, "swap_sibling")
    pairs = [_pair_add(a, g, f"pair_add_{i}") for i, (a, g) in enumerate(zip(big, from_sibling))]
    from_chips = _scatter_chips([p for p, _ in pairs], "scatter_chips")
    p_meta, p_cqw, p_dww, p_small = _exchange(
        [split_cols(dhead[pad:pad + NMETA]), split_cols(g_cqw), split_cols(g_dw_w), small_vec],
        [True] * 3 + [False], "exchange_small")

    res = {}
    res["w_in"] = _adam_chips(pairs[0][1], from_chips[0], w_in[0], m_w_in[0], v_w_in[0], "adam_w_in")
    res["w_dn_out"] = _adam_chips(pairs[1][1], from_chips[1], w_dn_out[0], m_w_dn_out[0], v_w_dn_out[0], "adam_w_dn")
    res["w_cf_out"] = _adam_chips(pairs[2][1], from_chips[2], w_cf_out[0], m_w_cf_out[0], v_w_cf_out[0], "adam_w_cf")
    res["w_o"] = _adam_chips(pairs[3][1], from_chips[3], w_o[0], m_w_o[0], v_w_o[0], "adam_w_o")
    res["meta"] = _adam_reduce(p_meta, meta, m_meta, v_meta, "adam_meta")
    res["conv_qkv_w"] = _adam_reduce(p_cqw, conv_qkv_w[0], m_conv_qkv_w[0], v_conv_qkv_w[0], "adam_conv_qkv_w")
    res["dw_w"] = _adam_reduce(p_dww, dw_w[0], m_dw_w[0], v_dw_w[0], "adam_dw_w")
    loc = dict(norm_w=(norm_w, m_norm_w, v_norm_w), a_log=(a_log, m_a_log, v_a_log), dt_bias=(dt_bias, m_dt_bias, v_dt_bias),
               dn_norm_w=(dn_norm_w, m_dn_norm_w, v_dn_norm_w), dw_b=(dw_b, m_dw_b, v_dw_b), ln_w=(ln_w, m_ln_w, v_ln_w),
               ln_b=(ln_b, m_ln_b, v_ln_b), b_cf_out=(b_cf_out, m_b_cf_out, v_b_cf_out),
               final_norm_w=(final_norm_w, m_final_norm_w, v_final_norm_w))
    cat = lambda j: jnp.pad(jnp.concatenate([loc[k][j].reshape(1, -1) for k in SMALL], axis=1), ((0, 0), (0, ns_pad)))
    small_res = _adam_reduce(p_small, cat(0), cat(1), cat(2), "adam_small")
    off = 0
    for k in SMALL:
        wshape = loc[k][0].shape
        nk = loc[k][0].size
        res[k] = tuple(t[:, off:off + nk].reshape(wshape) for t in small_res)
        off += nk
    shaped = dict(w_in=w_in.shape, w_dn_out=w_dn_out.shape, w_cf_out=w_cf_out.shape, w_o=w_o.shape, meta=meta.shape,
                  conv_qkv_w=conv_qkv_w.shape, dw_w=dw_w.shape)
    for k, shp in shaped.items():
        res[k] = tuple(t.reshape(shp) for t in res[k])

    loss = lax.psum(loss_part[0, 0], ("x", "y", "c"))
    order = ("meta", "norm_w", "w_in", "conv_qkv_w", "a_log", "dt_bias", "dn_norm_w", "w_dn_out", "dw_w", "dw_b", "ln_w",
             "ln_b", "w_cf_out", "b_cf_out", "w_o", "final_norm_w")
    outs = [loss, grad_x[None]]
    for j in range(4):
        outs += [res[k][j] for k in order]
    return tuple(outs)
```

```python
import functools

import jax
import jax.numpy as jnp
from jax import lax
from jax.experimental import pallas as pl
from jax.experimental.pallas import tpu as pltpu

F32 = jnp.float32
BF16 = jnp.bfloat16
HI = lax.Precision.HIGHEST

D = 1024
H = 8
DK = 128
C = 64
NMETA = 16
KQ = 4
KD = 31
HALO_Q = 8
HALO_D = 32
EPS = 1e-6
NDEV = 8
LANE = 128
MIB = 1024 * 1024

ADAM_LR, ADAM_B1, ADAM_B2, ADAM_EPS, ADAM_WD, ADAM_STEP = 0.001, 0.9, 0.999, 1e-08, 0.01, 10

CB_Q, CB_K, CB_V, CB_ZA, CB_GA_, CB_GB_, CB_MA, CB_MB, CB_ZB = range(9)
NCB = 9


def _pick(n, cands):
    for c in cands:
        if n % c == 0:
            return c
    raise ValueError(f"no tile for {n}")


def _cp(sem=None, vmem_mib=40):
    kw = dict(vmem_limit_bytes=vmem_mib * MIB)
    if sem is not None:
        kw["dimension_semantics"] = sem
    return pltpu.CompilerParams(**kw)


def _call(body, **kw):
    return pl.pallas_call(body, **kw)


def _dot(a, b):
    return jnp.dot(a.astype(BF16), b.astype(BF16), preferred_element_type=F32)


def _dot_nt(a, b):
    return lax.dot_general(a.astype(BF16), b.astype(BF16), (((1,), (1,)), ((), ())), preferred_element_type=F32)


def _dot_tn(a, b):
    return lax.dot_general(a.astype(BF16), b.astype(BF16), (((0,), (0,)), ((), ())), preferred_element_type=F32)


def _dot_hi(a, b):
    return jnp.dot(a, b, precision=HI, preferred_element_type=F32)


def _sig(x):
    return 0.5 * jnp.tanh(0.5 * x) + 0.5


def _dsilu(x, s):
    return s * (1.0 + x * (1.0 - s))


def _rowsum(x):
    return jnp.sum(x, axis=-1, keepdims=True)


def _colsum(x):
    return jnp.sum(x, axis=0, keepdims=True)


def _exchange(arrs, scatter, name):
    n = len(arrs)
    out_shape = []
    for a, sc in zip(arrs, scatter):
        shp = a.shape if sc else (NDEV,) + a.shape
        out_shape.append(jax.ShapeDtypeStruct(shp, a.dtype))

    def body(*refs):
        ins, outs = refs[:n], refs[n:2 * n]
        send_sems, recv_sems, loc_sems = refs[2 * n:]
        x, y, c = lax.axis_index("x"), lax.axis_index("y"), lax.axis_index("c")
        me = 4 * x + 2 * y + c
        copies = []
        for a in range(n):
            for k in range(1, NDEV):
                px = 1 - x if (k >> 2) & 1 else x
                py = 1 - y if (k >> 1) & 1 else y
                pc = 1 - c if k & 1 else c
                src = ins[a].at[4 * px + 2 * py + pc] if scatter[a] else ins[a]
                cp = pltpu.make_async_remote_copy(
                    src_ref=src, dst_ref=outs[a].at[me],
                    send_sem=send_sems.at[a * (NDEV - 1) + k - 1], recv_sem=recv_sems.at[a * (NDEV - 1) + k - 1],
                    device_id=(px, py, pc), device_id_type=pl.DeviceIdType.MESH)
                cp.start()
                copies.append(cp)
            loc = pltpu.make_async_copy(ins[a].at[me] if scatter[a] else ins[a], outs[a].at[me], loc_sems.at[a])
            loc.start()
            copies.append(loc)
        for cp in copies:
            cp.wait()

    any_spec = pl.BlockSpec(memory_space=pl.ANY)
    return _call(
        body, name=name, out_shape=tuple(out_shape),
        in_specs=[any_spec] * n, out_specs=tuple([any_spec] * n),
        scratch_shapes=[pltpu.SemaphoreType.DMA((n * (NDEV - 1),)), pltpu.SemaphoreType.DMA((n * (NDEV - 1),)),
                        pltpu.SemaphoreType.DMA((n,))],
    )(*arrs)


NCHIP = 4


def _gather_two_level(arrs, name):
    n = len(arrs)
    per = NDEV - 1

    def body(*refs):
        ins, outs = refs[:n], refs[n:2 * n]
        send_sems, recv_sems, loc_sems = refs[2 * n:]
        x, y, c = lax.axis_index("x"), lax.axis_index("y"), lax.axis_index("c")
        me, sibling = (x, y, c), (x, y, 1 - c)
        chips = [(1 - x, y), (x, 1 - y), (1 - x, 1 - y)]

        def slot(a, px, py, pc):
            return outs[a].at[4 * px + 2 * py + pc]

        def copy(a, k, block, to, src=None):
            return pltpu.make_async_remote_copy(
                src_ref=slot(a, *block) if src is None else src, dst_ref=slot(a, *block),
                send_sem=send_sems.at[a * per + k], recv_sem=recv_sems.at[a * per + k],
                device_id=to, device_id_type=pl.DeviceIdType.MESH)

        local, sent = [], []
        for a in range(n):
            mine = pltpu.make_async_copy(ins[a], slot(a, *me), loc_sems.at[a])
            mine.start()
            local.append(mine)
            first = [copy(a, 1 + j, me, (*chip, c), src=ins[a]) for j, chip in enumerate(chips)]
            first.append(copy(a, 0, me, sibling, src=ins[a]))
            for cp in first:
                cp.start()
            sent += first
        for j, chip in enumerate(chips):
            for a in range(n):
                copy(a, 1 + j, (*chip, c), me).wait_recv()
                cp = copy(a, 4 + j, (*chip, c), sibling)
                cp.start()
                sent.append(cp)
        for a in range(n):
            copy(a, 0, sibling, me).wait_recv()
            for j, chip in enumerate(chips):
                copy(a, 4 + j, (*chip, 1 - c), me).wait_recv()
        for cp in sent:
            cp.wait_send()
        for cp in local:
            cp.wait()

    any_spec = pl.BlockSpec(memory_space=pl.ANY)
    return _call(
        body, name=name, out_shape=tuple(jax.ShapeDtypeStruct((NDEV,) + a.shape, a.dtype) for a in arrs),
        in_specs=[any_spec] * n, out_specs=tuple([any_spec] * n),
        scratch_shapes=[pltpu.SemaphoreType.DMA((n * per,)), pltpu.SemaphoreType.DMA((n * per,)),
                        pltpu.SemaphoreType.DMA((n,))],
    )(*arrs)


def _swap_sibling(arrs, name):
    n = len(arrs)

    def body(*refs):
        ins, outs = refs[:n], refs[n:2 * n]
        send_sems, recv_sems = refs[2 * n:]
        x, y, c = lax.axis_index("x"), lax.axis_index("y"), lax.axis_index("c")
        copies = []
        for a in range(n):
            for j in range(NCHIP):
                cp = pltpu.make_async_remote_copy(
                    src_ref=ins[a].at[2 * j + (1 - c)], dst_ref=outs[a].at[j],
                    send_sem=send_sems.at[a * NCHIP + j], recv_sem=recv_sems.at[a * NCHIP + j],
                    device_id=(x, y, 1 - c), device_id_type=pl.DeviceIdType.MESH)
                cp.start()
                copies.append(cp)
        for cp in copies:
            cp.wait()

    any_spec = pl.BlockSpec(memory_space=pl.ANY)
    return _call(
        body, name=name, out_shape=tuple(jax.ShapeDtypeStruct((NCHIP,) + a.shape[1:], a.dtype) for a in arrs),
        in_specs=[any_spec] * n, out_specs=tuple([any_spec] * n),
        scratch_shapes=[pltpu.SemaphoreType.DMA((n * NCHIP,)), pltpu.SemaphoreType.DMA((n * NCHIP,))],
    )(*arrs)


def _pair_add(arr, got, name):
    _, r, n = arr.shape
    tr = _pick(r, (128,))
    arr4 = arr.reshape(NCHIP, 2, r, n)

    def body(a_ref, g_ref, p_ref, own_ref):
        c = lax.axis_index("c")
        my_chip = 2 * lax.axis_index("x") + lax.axis_index("y")
        s = jnp.where(c == 0, a_ref[0, 0], a_ref[0, 1]) + g_ref[0]
        p_ref[0] = s.astype(BF16)

        @pl.when(pl.program_id(1) == my_chip)
        def _():
            own_ref[...] = s

    return _call(
        body, name=name, grid=(r // tr, NCHIP),
        in_specs=[pl.BlockSpec((1, 2, tr, n), lambda i, j: (j, 0, i, 0)), pl.BlockSpec((1, tr, n), lambda i, j: (j, i, 0))],
        out_specs=(pl.BlockSpec((1, tr, n), lambda i, j: (j, i, 0)), pl.BlockSpec((tr, n), lambda i, j: (i, 0))),
        out_shape=(jax.ShapeDtypeStruct((NCHIP, r, n), BF16), jax.ShapeDtypeStruct((r, n), F32)),
        compiler_params=_cp(("parallel", "arbitrary")),
    )(arr4, got)


def _scatter_chips(arrs, name):
    n = len(arrs)
    per = NCHIP - 1

    def body(*refs):
        ins, outs = refs[:n], refs[n:2 * n]
        send_sems, recv_sems, loc_sems = refs[2 * n:]
        x, y, c = lax.axis_index("x"), lax.axis_index("y"), lax.axis_index("c")
        copies = []
        for a in range(n):
            loc = pltpu.make_async_copy(ins[a].at[2 * x + y], outs[a].at[2 * x + y], loc_sems.at[a])
            loc.start()
            copies.append(loc)
            for k in range(1, NCHIP):
                px = 1 - x if (k >> 1) & 1 else x
                py = 1 - y if k & 1 else y
                cp = pltpu.make_async_remote_copy(
                    src_ref=ins[a].at[2 * px + py], dst_ref=outs[a].at[2 * x + y],
                    send_sem=send_sems.at[a * per + k - 1], recv_sem=recv_sems.at[a * per + k - 1],
                    device_id=(px, py, c), device_id_type=pl.DeviceIdType.MESH)
                cp.start()
                copies.append(cp)
        for cp in copies:
            cp.wait()

    any_spec = pl.BlockSpec(memory_space=pl.ANY)
    return _call(
        body, name=name, out_shape=tuple(jax.ShapeDtypeStruct(a.shape, a.dtype) for a in arrs),
        in_specs=[any_spec] * n, out_specs=tuple([any_spec] * n),
        scratch_shapes=[pltpu.SemaphoreType.DMA((n * per,)), pltpu.SemaphoreType.DMA((n * per,)),
                        pltpu.SemaphoreType.DMA((n,))],
    )(*arrs)


def _mm(a, b, name, out_dtype=F32, nt=False):
    m, k = a.shape
    n = b.shape[0] if nt else b.shape[1]
    tm = _pick(m, (1664, 832, 640, 320, 128))
    tn = _pick(n, (1024, 512, 256, 128))
    tk = _pick(k, (1024, 512, 128))
    nk = k // tk
    b_spec = (pl.BlockSpec((tn, tk), lambda i, j, kk: (j, kk)) if nt else
              pl.BlockSpec((tk, tn), lambda i, j, kk: (kk, j)))

    def body(a_ref, b_ref, o_ref, *acc):
        part = (_dot_nt if nt else _dot)(a_ref[...], b_ref[...])
        if nk == 1:
            o_ref[...] = part.astype(out_dtype)
        else:
            kk = pl.program_id(2)

            @pl.when(kk == 0)
            def _():
                acc[0][...] = part

            @pl.when(kk > 0)
            def _():
                acc[0][...] += part

            @pl.when(kk == nk - 1)
            def _():
                o_ref[...] = acc[0][...].astype(out_dtype)

    return _call(
        body, name=name, grid=(m // tm, n // tn, nk),
        in_specs=[pl.BlockSpec((tm, tk), lambda i, j, kk: (i, kk)), b_spec],
        out_specs=pl.BlockSpec((tm, tn), lambda i, j, kk: (i, j)),
        out_shape=jax.ShapeDtypeStruct((m, n), out_dtype),
        scratch_shapes=[pltpu.VMEM((tm, tn), F32)] if nk > 1 else [],
        compiler_params=_cp(("parallel", "parallel", "arbitrary")),
    )(a, b)


def _mm_tn(a, b, name):
    t, m = a.shape
    n = b.shape[1]
    tt = _pick(t, (1664, 640, 128))
    tm = _pick(m, (1024, 512, 128))
    tn = _pick(n, (1152, 1024, 512, 128))
    nt = t // tt

    def body(a_ref, b_ref, o_ref):
        s = pl.program_id(2)
        part = _dot_tn(a_ref[...], b_ref[...])

        @pl.when(s == 0)
        def _():
            o_ref[...] = part

        @pl.when(s > 0)
        def _():
            o_ref[...] += part

    return _call(
        body, name=name, grid=(m // tm, n // tn, nt),
        in_specs=[pl.BlockSpec((tt, tm), lambda i, j, s: (s, i)), pl.BlockSpec((tt, tn), lambda i, j, s: (s, j))],
        out_specs=pl.BlockSpec((tm, tn), lambda i, j, s: (i, j)),
        out_shape=jax.ShapeDtypeStruct((m, n), F32),
        compiler_params=_cp(("parallel", "parallel", "arbitrary")),
    )(a, b)


def _proj_fwd(x_ext, norm_w, w_main_t, w_ba_t):
    lp = x_ext.shape[0]
    n = w_main_t.shape[0]
    tm = _pick(lp, (832, 640, 320))
    tn = 1024

    def body(x_ref, nw_ref, w_ref, wba_ref, proj_ref, ba_ref, h_ref):
        @pl.when(pl.program_id(1) == 0)
        def _():
            x = x_ref[...]
            r = lax.rsqrt(jnp.mean(x * x, axis=-1, keepdims=True) + EPS)
            h = (x * r * nw_ref[...]).astype(BF16)
            h_ref[...] = h
            ba_ref[...] = _dot_nt(h, wba_ref[...])

        proj_ref[...] = _dot_nt(h_ref[...], w_ref[...])

    return _call(
        body, name="proj_fwd", grid=(lp // tm, n // tn),
        in_specs=[pl.BlockSpec((tm, D), lambda i, j: (i, 0)), pl.BlockSpec((1, D), lambda i, j: (0, 0)),
                  pl.BlockSpec((tn, D), lambda i, j: (j, 0)), pl.BlockSpec((LANE, D), lambda i, j: (0, 0))],
        out_specs=(pl.BlockSpec((tm, tn), lambda i, j: (i, j)), pl.BlockSpec((tm, LANE), lambda i, j: (i, 0)),
                   pl.BlockSpec((tm, D), lambda i, j: (i, 0))),
        out_shape=(jax.ShapeDtypeStruct((lp, n), F32), jax.ShapeDtypeStruct((lp, LANE), F32),
                   jax.ShapeDtypeStruct((lp, D), BF16)),
        compiler_params=_cp(("parallel", "arbitrary")),
    )(x_ext, norm_w, w_main_t, w_ba_t)


def _dh_mm(dproj, dba, w_main_t, w_ba_t):
    lp, n = dproj.shape
    tm = _pick(lp, (832, 640, 320))
    tn = 1024
    tk = 2304
    nk = n // tk

    def body(a_ref, ba_ref, b_ref, bba_ref, o_ref, acc):
        kk = pl.program_id(2)

        @pl.when(kk == 0)
        def _():
            acc[...] = jnp.dot(ba_ref[...], bba_ref[...], preferred_element_type=F32)

        acc[...] += jnp.dot(a_ref[...], b_ref[...], preferred_element_type=F32)

        @pl.when(kk == nk - 1)
        def _():
            o_ref[...] = acc[...]

    return _call(
        body, name="dh_mm", grid=(lp // tm, D // tn, nk),
        in_specs=[pl.BlockSpec((tm, tk), lambda i, j, kk: (i, kk)), pl.BlockSpec((tm, LANE), lambda i, j, kk: (i, 0)),
                  pl.BlockSpec((tk, tn), lambda i, j, kk: (kk, j)), pl.BlockSpec((LANE, tn), lambda i, j, kk: (0, j))],
        out_specs=pl.BlockSpec((tm, tn), lambda i, j, kk: (i, j)),
        out_shape=jax.ShapeDtypeStruct((lp, D), F32),
        scratch_shapes=[pltpu.VMEM((tm, tn), F32)],
        compiler_params=_cp(("parallel", "parallel", "arbitrary")),
    )(dproj, dba, w_main_t, w_ba_t)


def _beta_g(ba, ab, row0, pad):
    lane = lax.broadcasted_iota(jnp.int32, ba.shape, 1)
    rows = row0 + lax.broadcasted_iota(jnp.int32, ba.shape, 0)
    z = ba + ab[1:2, :]
    sp = jnp.maximum(z, 0.0) + jnp.log(1.0 + jnp.exp(-jnp.abs(z)))
    val = jnp.where(lane < H, _sig(ba), -jnp.exp(ab[0:1, :]) * sp)
    return jnp.where((lane < 2 * H) & (rows >= pad), val, 0.0)


def _qkv_conv_fwd(proj, ba, conv_w, ab, pad):
    lp = proj.shape[0]
    te = _pick(lp, (320,))
    hb = te // HALO_Q

    def body(main_ref, halo_ref, cw_ref, ba_ref, ab_ref, out_ref, bg_ref, pre_scr):
        i, s = pl.program_id(0), pl.program_id(1)
        pre_scr[:HALO_Q, :] = jnp.where(i > 0, halo_ref[...], 0.0)
        pre_scr[HALO_Q:, :] = main_ref[...]
        scale = jnp.where(s == 0, DK ** -0.5, 1.0)
        off = HALO_Q - (KQ - 1)

        def head(h, carry):
            cs = pl.ds(pl.multiple_of(h * DK, DK), DK)
            co = cw_ref[0:1, cs] * pre_scr[off:off + te, cs]
            for j in range(1, KQ):
                co = co + cw_ref[j:j + 1, cs] * pre_scr[off + j:off + j + te, cs]
            a = co * _sig(co)
            r = lax.rsqrt(_rowsum(a * a) + EPS)
            out_ref[:, cs] = jnp.where(s == 2, a, a * (r * scale))
            return carry

        lax.fori_loop(0, H, head, 0, unroll=True)

        @pl.when(s == 0)
        def _():
            bg_ref[...] = _beta_g(ba_ref[...], ab_ref[...], i * te, pad)

    return _call(
        body, name="qkv_conv_fwd", grid=(lp // te, 3),
        in_specs=[pl.BlockSpec((te, D), lambda i, s: (i, s)),
                  pl.BlockSpec((HALO_Q, D), lambda i, s: (jnp.maximum(i * hb - 1, 0), s)),
                  pl.BlockSpec((KQ, D), lambda i, s: (0, s)),
                  pl.BlockSpec((te, LANE), lambda i, s: (i, 0)),
                  pl.BlockSpec((2, LANE), lambda i, s: (0, 0))],
        out_specs=(pl.BlockSpec((te, D), lambda i, s: (i, s)), pl.BlockSpec((te, LANE), lambda i, s: (i, 0))),
        out_shape=(jax.ShapeDtypeStruct((lp, 3 * D), F32), jax.ShapeDtypeStruct((lp, LANE), F32)),
        scratch_shapes=[pltpu.VMEM((te + HALO_Q, D), F32)],
        compiler_params=_cp(("parallel", "arbitrary")),
    )(proj, proj, conv_w, ba, ab)


def _tri_masks():
    row = lax.broadcasted_iota(jnp.int32, (C, C), 0)
    col = lax.broadcasted_iota(jnp.int32, (C, C), 1)
    return row, col


def _split(a):
    hi = a.astype(BF16)
    return hi, (a - hi.astype(F32)).astype(BF16)


def _dot3(a, b, dims=(((1,), (0,)), ((), ()))):
    (ah, al), (bh, bl) = a, b
    mm = lambda x, y: lax.dot_general(x, y, dims, preferred_element_type=F32)
    return mm(ah, bh) + (mm(ah, bl) + mm(al, bh))


def _tinv(ys, eye):
    ts = [eye + y for y in ys]
    sp = [_split(y) for y in ys]
    yks = [_dot3(s, s) for s in sp]
    for _ in range(4):
        sp = [_split(yk) for yk in yks]
        ts = [t + _dot3(s, _split(t)) for s, t in zip(sp, ts)]
        yks = [_dot3(s, s) for s in sp]
    return [t + _dot3(_split(yk), _split(t)) for yk, t in zip(yks, ts)]


def _chunk_common(q, k, v, bcol, gcc, gcr, incl, strict):
    dm = jnp.where(incl, jnp.exp(gcc - gcr), 0.0)
    kk = _dot_nt(k, k)
    qk = _dot_nt(q, k)
    egc = jnp.exp(gcc)
    glast = gcc[C - 1:C, :]
    eend = jnp.exp(glast - gcc)
    elast = jnp.exp(glast)
    rhs = jnp.concatenate([v * bcol, k * (bcol * egc)], axis=1)
    return dm, kk, qk, egc, eend, elast, rhs


def _delta_fwd(qkv, bg):
    lp = qkv.shape[0]
    nc = lp // C

    def body(q_ref, k_ref, v_ref, bg_ref, o_ref, sall_ref, tall_ref, s_scr):
        @pl.when(pl.program_id(0) == 0)
        def _():
            s_scr[...] = jnp.zeros_like(s_scr)

        bgt = bg_ref[...]
        row, col = _tri_masks()
        incl, strict = row >= col, row > col
        eye = (row == col).astype(F32)
        gc_all = _dot_hi(incl.astype(F32), bgt)
        gc_t = _dot_hi(bgt.T, (row <= col).astype(F32))
        heads = range(H)
        sls = [slice(h * DK, (h + 1) * DK) for h in heads]
        qs, ks, vs = ([r[:, sl] for sl in sls] for r in (q_ref, k_ref, v_ref))
        bcols = [bgt[:, h:h + 1] for h in heads]
        cm = [_chunk_common(qs[h], ks[h], vs[h], bcols[h], gc_all[:, H + h:H + h + 1], gc_t[H + h:H + h + 1, :],
                            incl, strict) for h in heads]
        dms, kks, qks, egcs, eends, elasts, rhss = zip(*cm)
        ts = _tinv([jnp.where(strict, -(bcols[h] * kks[h] * dms[h]), 0.0) for h in heads], eye)
        sols = [_dot3(_split(ts[h]), _split(rhss[h])) for h in heads]
        ss = [s_scr[h] for h in heads]
        sb = [s.astype(BF16) for s in ss]
        wvs = [sols[h][:, :DK] - _dot(sols[h][:, DK:], sb[h]) for h in heads]
        wvb = [wv.astype(BF16) for wv in wvs]
        for h in heads:
            o_ref[:, sls[h]] = _dot(qs[h] * egcs[h], sb[h]) + _dot(qks[h] * dms[h], wvb[h])
            sall_ref[0, h] = ss[h]
            tall_ref[0, h] = ts[h]
        for h in heads:
            s_scr[h] = ss[h] * elasts[h] + _dot_tn(ks[h] * eends[h], wvb[h])

    blk = lambda j: pl.BlockSpec((C, D), lambda n: (n, j))
    return _call(
        body, name="delta_fwd", grid=(nc,),
        in_specs=[blk(0), blk(1), blk(2), pl.BlockSpec((C, LANE), lambda n: (n, 0))],
        out_specs=(pl.BlockSpec((C, D), lambda n: (n, 0)),
                   pl.BlockSpec((1, H, DK, DK), lambda n: (n, 0, 0, 0)),
                   pl.BlockSpec((1, H, C, C), lambda n: (n, 0, 0, 0))),
        out_shape=(jax.ShapeDtypeStruct((lp, D), F32), jax.ShapeDtypeStruct((nc, H, DK, DK), F32),
                   jax.ShapeDtypeStruct((nc, H, C, C), F32)),
        scratch_shapes=[pltpu.VMEM((H, DK, DK), F32)],
        compiler_params=_cp(("arbitrary",)),
    )(qkv, qkv, qkv, bg)


def _delta_bwd(qkv, bg, sall, tall, do):
    lp = qkv.shape[0]
    nc = lp // C

    def body(q_ref, k_ref, v_ref, bg_ref, sall_ref, tall_ref, do_ref, dqkv_ref, dbg_ref, ds_scr):
        @pl.when(pl.program_id(0) == 0)
        def _():
            ds_scr[...] = jnp.zeros_like(ds_scr)

        bgt = bg_ref[...]
        row, col = _tri_masks()
        incl, strict = row >= col, row > col
        upper = (row <= col).astype(F32)
        gc_all = _dot_hi(incl.astype(F32), bgt)
        gc_t = _dot_hi(bgt.T, upper)
        lane = lax.broadcasted_iota(jnp.int32, (C, LANE), 1)
        lastrow = lax.broadcasted_iota(jnp.int32, (C, 1), 0) == C - 1
        heads = range(H)
        sls = [slice(h * DK, (h + 1) * DK) for h in heads]
        qs, ks, vs, dos = ([r[:, sl] for sl in sls] for r in (q_ref, k_ref, v_ref, do_ref))
        bcols = [bgt[:, h:h + 1] for h in heads]
        cm = [_chunk_common(qs[h], ks[h], vs[h], bcols[h], gc_all[:, H + h:H + h + 1], gc_t[H + h:H + h + 1, :],
                            incl, strict) for h in heads]
        dms, kks, qks, egcs, eends, elasts, rhss = zip(*cm)
        ss = [sall_ref[0, h] for h in heads]
        ts = [tall_ref[0, h] for h in heads]
        dsns = [ds_scr[h] for h in heads]
        sb = [s.astype(BF16) for s in ss]
        dsb = [d.astype(BF16) for d in dsns]
        dob = [d.astype(BF16) for d in dos]
        sols = [_dot3(_split(ts[h]), _split(rhss[h])) for h in heads]
        ws = [sol[:, DK:] for sol in sols]
        qgs = [qs[h] * egcs[h] for h in heads]
        kends = [ks[h] * eends[h] for h in heads]
        wvs = [sols[h][:, :DK] - _dot(ws[h], sb[h]) for h in heads]
        wvb = [wv.astype(BF16) for wv in wvs]
        dwvs = [_dot_tn(qks[h] * dms[h], dob[h]) + _dot(kends[h], dsb[h]) for h in heads]
        dps = [jnp.where(incl, _dot_nt(dob[h], wvb[h]), 0.0) for h in heads]
        dqgs = [_dot_nt(dob[h], sb[h]) for h in heads]
        dkends = [_dot_nt(wvb[h], dsb[h]) for h in heads]
        for h in heads:
            ds_scr[h] = _dot_tn(qgs[h], dob[h]) + elasts[h] * dsns[h] - _dot_tn(ws[h], dwvs[h])
        dglasts = [elasts[h] * jnp.sum(ss[h] * dsns[h], keepdims=True) for h in heads]
        dws = [-_dot_nt(dwvs[h], sb[h]) for h in heads]
        tts = [_split(ts[h].T) for h in heads]
        drhss = [_dot3(tts[h], _split(jnp.concatenate([dwvs[h], dws[h]], axis=1))) for h in heads]
        nt_dims = (((1,), (1,)), ((), ()))
        dns = [jnp.where(strict, -_dot3(_split(drhss[h]), _split(sols[h]), nt_dims), 0.0) for h in heads]
        dbeta_t = jnp.zeros((C, LANE), F32)
        dgc_t = jnp.zeros((C, LANE), F32)
        for h in heads:
            q, k, v, bcol, dm, kk, qk, egc, eend = qs[h], ks[h], vs[h], bcols[h], dms[h], kks[h], qks[h], egcs[h], eends[h]
            drv, drk = drhss[h][:, :DK], drhss[h][:, DK:]
            dn, dp, dqg, dkend = dns[h], dps[h], dqgs[h], dkends[h]
            rk = _rowsum(drk * k)
            dkk = dn * (bcol * dm)
            dqk = dp * dm
            e = (dn * (bcol * kk) + dp * qk) * dm
            tk = _rowsum(dkend * kends[h])
            dgc = rk * bcol * egc + _rowsum(e) - _rowsum(e.T) + _rowsum(dqg * qgs[h]) - tk
            dgc = dgc + jnp.where(lastrow, dglasts[h] + jnp.sum(tk, keepdims=True), 0.0)
            dbeta = _rowsum(drv * v) + rk * egc + _rowsum(dn * kk * dm)
            dqkv_ref[:, sls[h]] = _dot(dqk, k) + dqg * egc
            dqkv_ref[:, D + h * DK:D + (h + 1) * DK] = (drk * (bcol * egc) + _dot(dkk, k) + _dot_tn(dkk, k)
                                                       + _dot_tn(dqk, q) + dkend * eend)
            dqkv_ref[:, 2 * D + h * DK:2 * D + (h + 1) * DK] = bcol * drv
            dbeta_t = jnp.where(lane == h, dbeta, dbeta_t)
            dgc_t = jnp.where(lane == H + h, dgc, dgc_t)
        dbg_ref[...] = dbeta_t + _dot_hi(upper, dgc_t)

    rev = lambda n: nc - 1 - n
    blk = lambda j: pl.BlockSpec((C, D), lambda n: (rev(n), j))
    return _call(
        body, name="delta_bwd", grid=(nc,),
        in_specs=[blk(0), blk(1), blk(2), pl.BlockSpec((C, LANE), lambda n: (rev(n), 0)),
                  pl.BlockSpec((1, H, DK, DK), lambda n: (rev(n), 0, 0, 0)),
                  pl.BlockSpec((1, H, C, C), lambda n: (rev(n), 0, 0, 0)),
                  pl.BlockSpec((C, D), lambda n: (rev(n), 0))],
        out_specs=(pl.BlockSpec((C, 3 * D), lambda n: (rev(n), 0)), pl.BlockSpec((C, LANE), lambda n: (rev(n), 0))),
        out_shape=(jax.ShapeDtypeStruct((lp, 3 * D), F32), jax.ShapeDtypeStruct((lp, LANE), F32)),
        scratch_shapes=[pltpu.VMEM((H, DK, DK), F32)],
        compiler_params=_cp(("arbitrary",)),
    )(qkv, qkv, qkv, bg, sall, tall, do)


def _o_post_fwd(o, proj, dn_w):
    lp = o.shape[0]
    te = _pick(lp, (640, 320))

    def body(o_ref, za_ref, w_ref, out_ref):
        za = za_ref[...]
        gate = za * _sig(za)
        for h in range(H):
            sl = slice(h * DK, (h + 1) * DK)
            oh = o_ref[:, sl]
            r = lax.rsqrt(jnp.mean(oh * oh, axis=-1, keepdims=True) + EPS)
            out_ref[:, sl] = (oh * r * w_ref[...] * gate[:, sl]).astype(BF16)

    return _call(
        body, name="o_post_fwd", grid=(lp // te,),
        in_specs=[pl.BlockSpec((te, D), lambda i: (i, 0)), pl.BlockSpec((te, D), lambda i: (i, CB_ZA)),
                  pl.BlockSpec((1, DK), lambda i: (0, 0))],
        out_specs=pl.BlockSpec((te, D), lambda i: (i, 0)),
        out_shape=jax.ShapeDtypeStruct((lp, D), BF16),
        compiler_params=_cp(("parallel",)),
    )(o, proj, dn_w)


def _o_post_bwd(do_n, o, proj, dn_w, dproj):
    lp = o.shape[0]
    te = _pick(lp, (640, 320))

    def body(don_ref, o_ref, za_ref, w_ref, _, do_ref, dza_ref, dw_ref):
        @pl.when(pl.program_id(0) == 0)
        def _():
            dw_ref[...] = jnp.zeros_like(dw_ref)

        za = za_ref[...]
        sz = _sig(za)
        gate, dgate = za * sz, _dsilu(za, sz)
        w = w_ref[...]
        dw = jnp.zeros((1, DK), F32)
        for h in range(H):
            sl = slice(h * DK, (h + 1) * DK)
            oh, g = o_ref[:, sl], don_ref[:, sl]
            r = lax.rsqrt(jnp.mean(oh * oh, axis=-1, keepdims=True) + EPS)
            ohat = oh * r
            dza_ref[:, sl] = (g * ohat * w * dgate[:, sl]).astype(BF16)
            don = g * gate[:, sl]
            dw = dw + _colsum(don * ohat)
            dohat = don * w
            do_ref[:, sl] = r * (dohat - ohat * jnp.mean(dohat * ohat, axis=-1, keepdims=True))
        dw_ref[...] += dw

    return _call(
        body, name="o_post_bwd", grid=(lp // te,),
        in_specs=[pl.BlockSpec((te, D), lambda i: (i, 0)), pl.BlockSpec((te, D), lambda i: (i, 0)),
                  pl.BlockSpec((te, D), lambda i: (i, CB_ZA)), pl.BlockSpec((1, DK), lambda i: (0, 0)),
                  pl.BlockSpec(memory_space=pl.ANY)],
        out_specs=(pl.BlockSpec((te, D), lambda i: (i, 0)), pl.BlockSpec((te, D), lambda i: (i, CB_ZA)),
                   pl.BlockSpec((1, DK), lambda i: (0, 0))),
        out_shape=(jax.ShapeDtypeStruct((lp, D), F32), jax.ShapeDtypeStruct(dproj.shape, dproj.dtype),
                   jax.ShapeDtypeStruct((1, DK), F32)),
        input_output_aliases={4: 1},
        compiler_params=_cp(("arbitrary",)),
    )(do_n, o, proj, dn_w, dproj)


def _qkv_conv_bwd(proj, dqkv, conv_w, dproj):
    lp = proj.shape[0]
    te = _pick(lp, (320,))
    hb = te // HALO_Q
    nt = lp // te
    last_hb = lp // HALO_Q - 1

    def body(main_ref, prev_ref, next_ref, dmain_ref, dnext_ref, cw_ref, _, dpre_ref, dcw_ref, pre_scr, dn_scr, dco_scr):
        s, i = pl.program_id(0), pl.program_id(1)

        @pl.when(i == 0)
        def _():
            dcw_ref[...] = jnp.zeros_like(dcw_ref)

        ne = te + HALO_Q
        pre_scr[:HALO_Q, :] = jnp.where(i > 0, prev_ref[...], 0.0)
        pre_scr[HALO_Q:ne, :] = main_ref[...]
        pre_scr[ne:, :] = jnp.where(i < nt - 1, next_ref[...], 0.0)
        dn_scr[:te, :] = dmain_ref[...]
        dn_scr[te:, :] = jnp.where(i < nt - 1, dnext_ref[...], 0.0)
        scale = jnp.where(s == 0, DK ** -0.5, 1.0)
        off = HALO_Q - (KQ - 1)

        def head(h, carry):
            cs = pl.ds(pl.multiple_of(h * DK, DK), DK)
            taps = [pre_scr[off + j:off + j + ne, cs] for j in range(KQ)]
            co = cw_ref[0:1, cs] * taps[0]
            for j in range(1, KQ):
                co = co + cw_ref[j:j + 1, cs] * taps[j]
            sg = _sig(co)
            a = co * sg
            g = dn_scr[:, cs]
            r = lax.rsqrt(_rowsum(a * a) + EPS)
            yhat = a * r
            da = jnp.where(s == 2, g, (scale * r) * (g - yhat * _rowsum(g * yhat)))
            dco = da * _dsilu(co, sg)
            dco_scr[:, cs] = dco
            dpre = cw_ref[0:1, cs] * dco_scr[KQ - 1:KQ - 1 + te, cs]
            for j in range(1, KQ):
                dpre = dpre + cw_ref[j:j + 1, cs] * dco_scr[KQ - 1 - j:KQ - 1 - j + te, cs]
            dpre_ref[:, cs] = dpre.astype(BF16)
            dcw_ref[:, cs] += jnp.concatenate([_colsum(dco[:te] * taps[j][:te]) for j in range(KQ)], axis=0)
            return carry

        lax.fori_loop(0, H, head, 0, unroll=True)

    return _call(
        body, name="qkv_conv_bwd", grid=(3, nt),
        in_specs=[pl.BlockSpec((te, D), lambda s, i: (i, s)),
                  pl.BlockSpec((HALO_Q, D), lambda s, i: (jnp.maximum(i * hb - 1, 0), s)),
                  pl.BlockSpec((HALO_Q, D), lambda s, i: (jnp.minimum((i + 1) * hb, last_hb), s)),
                  pl.BlockSpec((te, D), lambda s, i: (i, s)),
                  pl.BlockSpec((HALO_Q, D), lambda s, i: (jnp.minimum((i + 1) * hb, last_hb), s)),
                  pl.BlockSpec((KQ, D), lambda s, i: (0, s)),
                  pl.BlockSpec(memory_space=pl.ANY)],
        out_specs=(pl.BlockSpec((te, D), lambda s, i: (i, s)), pl.BlockSpec((KQ, D), lambda s, i: (0, s))),
        out_shape=(jax.ShapeDtypeStruct(dproj.shape, dproj.dtype), jax.ShapeDtypeStruct((KQ, 3 * D), F32)),
        input_output_aliases={6: 0},
        scratch_shapes=[pltpu.VMEM((te + 2 * HALO_Q, D), F32), pltpu.VMEM((te + HALO_Q, D), F32),
                        pltpu.VMEM((te + HALO_Q, D), F32)],
        compiler_params=_cp(("arbitrary", "arbitrary")),
    )(proj, proj, proj, dqkv, dqkv, conv_w, dproj)


def _ba_bwd(dbg, ba, ab, pad):
    lp = ba.shape[0]
    te = _pick(lp, (640, 320))

    def body(dbg_ref, ba_ref, ab_ref, dba_ref, dab_ref):
        i = pl.program_id(0)

        @pl.when(i == 0)
        def _():
            dab_ref[...] = jnp.zeros_like(dab_ref)

        ba, ab = ba_ref[...], ab_ref[...]
        lane = lax.broadcasted_iota(jnp.int32, ba.shape, 1)
        rows = i * te + lax.broadcasted_iota(jnp.int32, ba.shape, 0)
        g = jnp.where((lane < 2 * H) & (rows >= pad), dbg_ref[...], 0.0)
        sb = _sig(ba)
        z = ba + ab[1:2, :]
        sp = jnp.maximum(z, 0.0) + jnp.log(1.0 + jnp.exp(-jnp.abs(z)))
        nea = -jnp.exp(ab[0:1, :])
        dz = g * nea * _sig(z)
        dba_ref[...] = jnp.where(lane < H, g * sb * (1.0 - sb), dz).astype(BF16)
        is_g = (lane >= H) & (lane < 2 * H)
        dab_ref[...] += jnp.concatenate([_colsum(jnp.where(is_g, g * nea * sp, 0.0)),
                                         _colsum(jnp.where(is_g, dz, 0.0))], axis=0)

    return _call(
        body, name="ba_bwd", grid=(lp // te,),
        in_specs=[pl.BlockSpec((te, LANE), lambda i: (i, 0)), pl.BlockSpec((te, LANE), lambda i: (i, 0)),
                  pl.BlockSpec((2, LANE), lambda i: (0, 0))],
        out_specs=(pl.BlockSpec((te, LANE), lambda i: (i, 0)), pl.BlockSpec((2, LANE), lambda i: (0, 0))),
        out_shape=(jax.ShapeDtypeStruct((lp, LANE), BF16), jax.ShapeDtypeStruct((2, LANE), F32)),
        compiler_params=_cp(("arbitrary",)),
    )(dbg, ba, ab)


SUBLANES = 8
CONV_RB = 64


def _fill_shifted(sh_scr, src_scr, cs):
    n = sh_scr.shape[1]
    for s in range(1, SUBLANES):
        sh_scr[s] = src_scr[s:s + n, cs]


def _shifted(sh_scr, src_scr, cs, r, r0, n):
    s, a8 = r % SUBLANES, r - r % SUBLANES
    if s == 0:
        return src_scr[r0 + a8:r0 + a8 + n, cs]
    return sh_scr[s, r0 + a8:r0 + a8 + n, :]


def _conv_b_fwd(proj, dw_w, dw_b, ln_w, ln_b):
    lp = proj.shape[0]
    te = _pick(lp, (320,))
    hb = te // HALO_D

    def body(a_ref, b_ref, ha_ref, hb_ref, zb_ref, w_ref, wb_ref, lw_ref, lb_ref, c1_ref, c3_ref, c0_scr, sh_scr):
        i = pl.program_id(0)
        c0_scr[:HALO_D, :] = jnp.where(i > 0, ha_ref[...] * _sig(hb_ref[...]), 0.0)
        c0_scr[HALO_D:, :] = a_ref[...] * _sig(b_ref[...])
        off = HALO_D - (KD - 1)
        def lane_block(cb, carry):
            cs = pl.ds(pl.multiple_of(cb * LANE, LANE), LANE)
            _fill_shifted(sh_scr, c0_scr, cs)
            for r0 in range(0, te, CONV_RB):
                acc = None
                for j in range(KD):
                    term = w_ref[j:j + 1, cs] * _shifted(sh_scr, c0_scr, cs, off + j, r0, CONV_RB)
                    acc = term if acc is None else acc + term
                c1_ref[r0:r0 + CONV_RB, cs] = acc + wb_ref[:, cs]
            return carry

        lax.fori_loop(0, D // LANE, lane_block, 0)
        c1 = c1_ref[...]
        mu = jnp.mean(c1, axis=-1, keepdims=True)
        xc = c1 - mu
        c2 = xc * lax.rsqrt(jnp.mean(xc * xc, axis=-1, keepdims=True) + EPS) * lw_ref[...] + lb_ref[...]
        zb = zb_ref[...]
        c3_ref[...] = (c2 * _sig(c2) * zb * _sig(zb)).astype(BF16)

    vec = pl.BlockSpec((1, D), lambda i: (0, 0))
    return _call(
        body, name="conv_b_fwd", grid=(lp // te,),
        in_specs=[pl.BlockSpec((te, D), lambda i: (i, CB_GA_)), pl.BlockSpec((te, D), lambda i: (i, CB_GB_)),
                  pl.BlockSpec((HALO_D, D), lambda i: (jnp.maximum(i * hb - 1, 0), CB_GA_)),
                  pl.BlockSpec((HALO_D, D), lambda i: (jnp.maximum(i * hb - 1, 0), CB_GB_)),
                  pl.BlockSpec((te, D), lambda i: (i, CB_ZB)),
                  pl.BlockSpec((KD, D), lambda i: (0, 0)), vec, vec, vec],
        out_specs=(pl.BlockSpec((te, D), lambda i: (i, 0)), pl.BlockSpec((te, D), lambda i: (i, 0))),
        out_shape=(jax.ShapeDtypeStruct((lp, D), F32), jax.ShapeDtypeStruct((lp, D), BF16)),
        scratch_shapes=[pltpu.VMEM((te + HALO_D, D), F32), pltpu.VMEM((SUBLANES, te + HALO_D - SUBLANES, LANE), F32)],
        compiler_params=_cp(("parallel",)),
    )(proj, proj, proj, proj, proj, dw_w, dw_b, ln_w, ln_b)


def _conv_b_bwd1(dc3, c1, proj, ln_w, ln_b, dproj):
    lp = c1.shape[0]
    te = _pick(lp, (640, 320))

    def body(dc3_ref, c1_ref, zb_ref, lw_ref, lb_ref, _, dc1_ref, dzb_ref, sums_ref):
        @pl.when(pl.program_id(0) == 0)
        def _():
            sums_ref[...] = jnp.zeros_like(sums_ref)

        c1, g = c1_ref[...], dc3_ref[...]
        mu = jnp.mean(c1, axis=-1, keepdims=True)
        xc = c1 - mu
        rstd = lax.rsqrt(jnp.mean(xc * xc, axis=-1, keepdims=True) + EPS)
        xh = xc * rstd
        lw = lw_ref[...]
        c2 = xh * lw + lb_ref[...]
        s2 = _sig(c2)
        zb = zb_ref[...]
        sz = _sig(zb)
        dc2 = g * (zb * sz) * _dsilu(c2, s2)
        dzb_ref[...] = (g * (c2 * s2) * _dsilu(zb, sz)).astype(BF16)
        dxh = dc2 * lw
        dc1 = rstd * (dxh - jnp.mean(dxh, axis=-1, keepdims=True) - xh * jnp.mean(dxh * xh, axis=-1, keepdims=True))
        dc1_ref[...] = dc1
        sums_ref[...] += jnp.concatenate([_colsum(dc2 * xh), _colsum(dc2), _colsum(dc1)], axis=0)

    vec = pl.BlockSpec((1, D), lambda i: (0, 0))
    return _call(
        body, name="conv_b_bwd1", grid=(lp // te,),
        in_specs=[pl.BlockSpec((te, D), lambda i: (i, 0)), pl.BlockSpec((te, D), lambda i: (i, 0)),
                  pl.BlockSpec((te, D), lambda i: (i, CB_ZB)), vec, vec, pl.BlockSpec(memory_space=pl.ANY)],
        out_specs=(pl.BlockSpec((te, D), lambda i: (i, 0)), pl.BlockSpec((te, D), lambda i: (i, CB_ZB)),
                   pl.BlockSpec((3, D), lambda i: (0, 0))),
        out_shape=(jax.ShapeDtypeStruct((lp, D), F32), jax.ShapeDtypeStruct(dproj.shape, dproj.dtype),
                   jax.ShapeDtypeStruct((3, D), F32)),
        input_output_aliases={5: 1},
        compiler_params=_cp(("arbitrary",)),
    )(dc3, c1, proj, ln_w, ln_b, dproj)


def _conv_b_bwd2(dc1, proj, dw_w, dproj):
    lp = dc1.shape[0]
    te = _pick(lp, (320,))
    hb = te // HALO_D
    nt = lp // te
    last_hb = lp // HALO_D - 1

    def body(g_ref, gn_ref, a_ref, b_ref, ha_ref, hb_ref, w_ref, _, dab_ref, dw_ref, c0_scr, g_scr, dc0_scr,
             csh_scr, gsh_scr):
        i = pl.program_id(0)

        @pl.when(i == 0)
        def _():
            dw_ref[...] = jnp.zeros_like(dw_ref)

        a, b = a_ref[...], b_ref[...]
        sb = _sig(b)
        c0_scr[:HALO_D, :] = jnp.where(i > 0, ha_ref[...] * _sig(hb_ref[...]), 0.0)
        c0_scr[HALO_D:, :] = a * sb
        g_scr[:te, :] = g_ref[...]
        g_scr[te:, :] = jnp.where(i < nt - 1, gn_ref[...], 0.0)
        off = HALO_D - (KD - 1)
        def lane_block(cb, carry):
            cs = pl.ds(pl.multiple_of(cb * LANE, LANE), LANE)
            _fill_shifted(csh_scr, c0_scr, cs)
            _fill_shifted(gsh_scr, g_scr, cs)
            for r0 in range(0, te, CONV_RB):
                acc = None
                for j in range(KD):
                    term = w_ref[j:j + 1, cs] * _shifted(gsh_scr, g_scr, cs, KD - 1 - j, r0, CONV_RB)
                    acc = term if acc is None else acc + term
                dc0_scr[r0:r0 + CONV_RB, cs] = acc
            parts = [None] * KD
            for r0 in range(0, te, CONV_RB):
                g = g_scr[r0:r0 + CONV_RB, cs].reshape(CONV_RB // SUBLANES, SUBLANES, LANE)
                for j in range(KD):
                    x = _shifted(csh_scr, c0_scr, cs, off + j, r0, CONV_RB)
                    p = jnp.sum(g * x.reshape(CONV_RB // SUBLANES, SUBLANES, LANE), axis=0)
                    parts[j] = p if parts[j] is None else parts[j] + p
            dw_ref[:, cs] += jnp.concatenate([_colsum(p) for p in parts], axis=0)
            return carry

        lax.fori_loop(0, D // LANE, lane_block, 0)
        dc0 = dc0_scr[...]
        dab_ref[:, :D] = (dc0 * sb).astype(BF16)
        dab_ref[:, D:] = (dc0 * a * sb * (1.0 - sb)).astype(BF16)

    return _call(
        body, name="conv_b_bwd2", grid=(nt,),
        in_specs=[pl.BlockSpec((te, D), lambda i: (i, 0)),
                  pl.BlockSpec((HALO_D, D), lambda i: (jnp.minimum((i + 1) * hb, last_hb), 0)),
                  pl.BlockSpec((te, D), lambda i: (i, CB_GA_)), pl.BlockSpec((te, D), lambda i: (i, CB_GB_)),
                  pl.BlockSpec((HALO_D, D), lambda i: (jnp.maximum(i * hb - 1, 0), CB_GA_)),
                  pl.BlockSpec((HALO_D, D), lambda i: (jnp.maximum(i * hb - 1, 0), CB_GB_)),
                  pl.BlockSpec((KD, D), lambda i: (0, 0)), pl.BlockSpec(memory_space=pl.ANY)],
        out_specs=(pl.BlockSpec((te, 2 * D), lambda i: (i, CB_GA_ // 2)), pl.BlockSpec((KD, D), lambda i: (0, 0))),
        out_shape=(jax.ShapeDtypeStruct(dproj.shape, dproj.dtype), jax.ShapeDtypeStruct((KD, D), F32)),
        input_output_aliases={7: 0},
        scratch_shapes=[pltpu.VMEM((te + HALO_D, D), F32), pltpu.VMEM((te + HALO_D, D), F32), pltpu.VMEM((te, D), F32),
                        pltpu.VMEM((SUBLANES, te + HALO_D - SUBLANES, LANE), F32),
                        pltpu.VMEM((SUBLANES, te + HALO_D - SUBLANES, LANE), F32)],
        compiler_params=_cp(("arbitrary",)),
    )(dc1, dc1, proj, proj, proj, proj, dw_w, dproj)


def _merge_fwd(y_a, y_b, proj, b_cf):
    lp = y_a.shape[0]
    te = _pick(lp, (640, 320))

    def body(ya_ref, yb_ref, ga_ref, gb_ref, bias_ref, out_ref):
        out_ref[...] = (_sig(ga_ref[...]) * ya_ref[...] + _sig(gb_ref[...]) * (yb_ref[...] + bias_ref[...])).astype(BF16)

    row = lambda j: pl.BlockSpec((te, D), lambda i: (i, j))
    return _call(
        body, name="merge_fwd", grid=(lp // te,),
        in_specs=[row(0), row(0), row(CB_MA), row(CB_MB), pl.BlockSpec((1, D), lambda i: (0, 0))],
        out_specs=row(0), out_shape=jax.ShapeDtypeStruct((lp, D), BF16),
        compiler_params=_cp(("parallel",)),
    )(y_a, y_b, proj, proj, b_cf)


def _merge_bwd(dmerged, y_a, y_b, proj, b_cf):
    lp = y_a.shape[0]
    te = _pick(lp, (640, 320))

    def body(dm_ref, ya_ref, yb_ref, ga_ref, gb_ref, bias_ref, dya_ref, dyb_ref, dg_ref, db_ref):
        @pl.when(pl.program_id(0) == 0)
        def _():
            db_ref[...] = jnp.zeros_like(db_ref)

        dm = dm_ref[...]
        sa, sb = _sig(ga_ref[...]), _sig(gb_ref[...])
        dyb = sb * dm
        dya_ref[...] = (sa * dm).astype(BF16)
        dyb_ref[...] = dyb.astype(BF16)
        dg_ref[:, :D] = (dm * ya_ref[...] * sa * (1.0 - sa)).astype(BF16)
        dg_ref[:, D:] = (dm * (yb_ref[...] + bias_ref[...]) * sb * (1.0 - sb)).astype(BF16)
        db_ref[...] += _colsum(dyb)

    row = lambda j: pl.BlockSpec((te, D), lambda i: (i, j))
    act = jax.ShapeDtypeStruct((lp, D), BF16)
    return _call(
        body, name="merge_bwd", grid=(lp // te,),
        in_specs=[row(0), row(0), row(0), row(CB_MA), row(CB_MB), pl.BlockSpec((1, D), lambda i: (0, 0))],
        out_specs=(row(0), row(0), pl.BlockSpec((te, 2 * D), lambda i: (i, CB_MA // 2)),
                   pl.BlockSpec((1, D), lambda i: (0, 0))),
        out_shape=(act, act, jax.ShapeDtypeStruct((lp, NCB * D), BF16), jax.ShapeDtypeStruct((1, D), F32)),
        compiler_params=_cp(("arbitrary",)),
    )(dmerged, y_a, y_b, proj, proj, b_cf)


def _final_fwd_bwd(x_ext, z, target, final_w):
    lp = x_ext.shape[0]
    te = LANE

    def body(x_ref, z_ref, t_ref, w_ref, dx_ref, dxb_ref, loss_ref, dw_ref):
        i = pl.program_id(0)

        @pl.when(i == 0)
        def _():
            loss_ref[...] = jnp.zeros_like(loss_ref)
            dw_ref[...] = jnp.zeros_like(dw_ref)

        xo = x_ref[...] + z_ref[...]
        r = lax.rsqrt(jnp.mean(xo * xo, axis=-1, keepdims=True) + EPS)
        xhat = xo * r
        w = w_ref[...]
        err = jnp.where(i > 0, xhat * w - t_ref[...], 0.0)
        loss_ref[...] += 0.5 * jnp.sum(jnp.mean(err * err, axis=-1, keepdims=True), keepdims=True)
        dy = err * (1.0 / D)
        dw_ref[...] += _colsum(dy * xhat)
        dxn = dy * w
        dx = r * (dxn - xhat * jnp.mean(dxn * xhat, axis=-1, keepdims=True))
        dx_ref[...] = dx
        dxb_ref[...] = dx.astype(BF16)

    return _call(
        body, name="final_fwd_bwd", grid=(lp // te,),
        in_specs=[pl.BlockSpec((te, D), lambda i: (i, 0)), pl.BlockSpec((te, D), lambda i: (i, 0)),
                  pl.BlockSpec((te, D), lambda i: (jnp.maximum(i - 1, 0), 0)), pl.BlockSpec((1, D), lambda i: (0, 0))],
        out_specs=(pl.BlockSpec((te, D), lambda i: (i, 0)), pl.BlockSpec((te, D), lambda i: (i, 0)),
                   pl.BlockSpec((1, 1), lambda i: (0, 0)), pl.BlockSpec((1, D), lambda i: (0, 0))),
        out_shape=(jax.ShapeDtypeStruct((lp, D), F32), jax.ShapeDtypeStruct((lp, D), BF16),
                   jax.ShapeDtypeStruct((1, 1), F32), jax.ShapeDtypeStruct((1, D), F32)),
        compiler_params=_cp(("arbitrary",)),
    )(x_ext, z, target, final_w)


def _prenorm_bwd(dh, x_ext, dx_out, norm_w, seq):
    lp = x_ext.shape[0]
    te = LANE

    def body(dh_ref, x_ref, dxo_ref, w_ref, gx_ref, head_ref, dw_ref):
        i = pl.program_id(0)

        @pl.when(i == 0)
        def _():
            dw_ref[...] = jnp.zeros_like(dw_ref)

        x, dh = x_ref[...], dh_ref[...]
        r = lax.rsqrt(jnp.mean(x * x, axis=-1, keepdims=True) + EPS)
        xhat = x * r
        dxn = dh * w_ref[...]
        dx = dxo_ref[...] + r * (dxn - xhat * jnp.mean(dxn * xhat, axis=-1, keepdims=True))
        dw_ref[...] += _colsum(dh * xhat)

        @pl.when(i == 0)
        def _():
            head_ref[...] = dx

        @pl.when(i > 0)
        def _():
            gx_ref[...] = dx

    row = pl.BlockSpec((te, D), lambda i: (i, 0))
    return _call(
        body, name="prenorm_bwd", grid=(lp // te,),
        in_specs=[row, row, row, pl.BlockSpec((1, D), lambda i: (0, 0))],
        out_specs=(pl.BlockSpec((te, D), lambda i: (jnp.maximum(i - 1, 0), 0)), pl.BlockSpec((te, D), lambda i: (0, 0)),
                   pl.BlockSpec((1, D), lambda i: (0, 0))),
        out_shape=(jax.ShapeDtypeStruct((seq, D), F32), jax.ShapeDtypeStruct((te, D), F32),
                   jax.ShapeDtypeStruct((1, D), F32)),
        compiler_params=_cp(("arbitrary",)),
    )(dh, x_ext, dx_out, norm_w)


def _adam_reduce(parts, w, m, v, name):
    r, n = w.shape
    tr = _pick(r, (128,)) if r % 128 == 0 else r

    def body(p_ref, w_ref, m_ref, v_ref, g_ref, d_ref, m2_ref, v2_ref):
        g = p_ref[0]
        for s in range(1, NDEV):
            g = g + p_ref[s]
        _adam_write(g, w_ref, m_ref, v_ref, g_ref, d_ref, m2_ref, v2_ref)

    blk = pl.BlockSpec((tr, n), lambda i: (i, 0))
    out = jax.ShapeDtypeStruct((r, n), F32)
    return _call(
        body, name=name, grid=(r // tr,),
        in_specs=[pl.BlockSpec((NDEV, tr, n), lambda i: (0, i, 0)), blk, blk, blk],
        out_specs=(blk, blk, blk, blk), out_shape=(out, out, out, out),
        compiler_params=_cp(("parallel",)),
    )(parts, w, m, v)


def _adam_write(g, w_ref, m_ref, v_ref, g_ref, d_ref, m2_ref, v2_ref):
    c1 = 1.0 - ADAM_B1 ** ADAM_STEP
    c2 = 1.0 - ADAM_B2 ** ADAM_STEP
    m2 = ADAM_B1 * m_ref[...] + (1.0 - ADAM_B1) * g
    v2 = ADAM_B2 * v_ref[...] + (1.0 - ADAM_B2) * (g * g)
    g_ref[...] = g
    m2_ref[...] = m2
    v2_ref[...] = v2
    d_ref[...] = -ADAM_LR * ((m2 / c1) / (jnp.sqrt(v2 / c2) + ADAM_EPS) + ADAM_WD * w_ref[...])


def _adam_chips(own, recv, w, m, v, name):
    r, n = w.shape
    tr = _pick(r, (128,))

    def body(own_ref, p_ref, w_ref, m_ref, v_ref, g_ref, d_ref, m2_ref, v2_ref):
        my_chip = 2 * lax.axis_index("x") + lax.axis_index("y")
        g = None
        for j in range(NCHIP):
            part = jnp.where(my_chip == j, own_ref[...], p_ref[j].astype(F32))
            g = part if g is None else g + part
        _adam_write(g, w_ref, m_ref, v_ref, g_ref, d_ref, m2_ref, v2_ref)

    blk = pl.BlockSpec((tr, n), lambda i: (i, 0))
    out = jax.ShapeDtypeStruct((r, n), F32)
    return _call(
        body, name=name, grid=(r // tr,),
        in_specs=[blk, pl.BlockSpec((NCHIP, tr, n), lambda i: (0, i, 0)), blk, blk, blk],
        out_specs=(blk, blk, blk, blk), out_shape=(out, out, out, out),
        compiler_params=_cp(("parallel",)),
    )(own, recv, w, m, v)


SMALL = ("norm_w", "a_log", "dt_bias", "dn_norm_w", "dw_b", "ln_w", "ln_b", "b_cf_out", "final_norm_w")


def kernel(x, meta, norm_w, w_in, conv_qkv_w, a_log, dt_bias, dn_norm_w, w_dn_out, dw_w, dw_b, ln_w, ln_b, w_cf_out, b_cf_out, w_o, final_norm_w, loss_target, m_meta, m_norm_w, m_w_in, m_conv_qkv_w, m_a_log, m_dt_bias, m_dn_norm_w, m_w_dn_out, m_dw_w, m_dw_b, m_ln_w, m_ln_b, m_w_cf_out, m_b_cf_out, m_w_o, m_final_norm_w, v_meta, v_norm_w, v_w_in, v_conv_qkv_w, v_a_log, v_dt_bias, v_dn_norm_w, v_w_dn_out, v_dw_w, v_dw_b, v_ln_w, v_ln_b, v_w_cf_out, v_b_cf_out, v_w_o, v_final_norm_w):
    seq = x.shape[1]
    pad = (-(seq + NMETA)) % LANE
    in_w = w_in.shape[2] * NDEV
    n_qkvz = 4 * D
    n_ba = 2 * H

    w_in_g, w_dn_g, w_cf_g, w_o_g, meta_g, cqw_g, dww_g = _gather_two_level(
        [w_in[0].astype(BF16).T, w_dn_out[0].astype(BF16), w_cf_out[0].astype(BF16), w_o[0].astype(BF16),
         meta, conv_qkv_w[0], dw_w[0]], "gather_weights")
    w_full_t = w_in_g.reshape(in_w, D)
    c_glu = n_qkvz + n_ba
    c_zb, c_mg = c_glu + 2 * D, c_glu + 3 * D
    w_main_t = jnp.concatenate([w_full_t[:n_qkvz], w_full_t[c_glu:c_zb], w_full_t[c_mg:], w_full_t[c_zb:c_mg]],
                               axis=0)
    w_ba_t = jnp.pad(w_full_t[n_qkvz:n_qkvz + n_ba], ((0, LANE - n_ba), (0, 0)))
    w_dn, w_cf, w_oo = (t.reshape(D, D) for t in (w_dn_g, w_cf_g, w_o_g))
    meta_full = jnp.transpose(meta_g, (1, 0, 2)).reshape(NMETA, D)
    cqw = jnp.transpose(cqw_g, (1, 0, 2)).reshape(KQ, 3 * D)
    dww = jnp.transpose(dww_g, (1, 0, 2)).reshape(KD, D)
    ab = jnp.pad(jnp.concatenate([a_log, dt_bias], axis=0), ((0, 0), (H, LANE - 2 * H)))

    x_ext = jnp.concatenate([jnp.zeros((pad, D), F32), meta_full, x[0]], axis=0)

    proj, ba, h = _proj_fwd(x_ext, norm_w, w_main_t, w_ba_t)
    qkv, bg = _qkv_conv_fwd(proj, ba, cqw, ab, pad)
    o, sall, tall = _delta_fwd(qkv, bg)
    o_n = _o_post_fwd(o, proj, dn_norm_w)
    y_a = _mm(o_n, w_dn, "y_a_mm")
    c1, c3 = _conv_b_fwd(proj, dww, dw_b, ln_w, ln_b)
    y_b = _mm(c3, w_cf, "y_b_mm")
    merged = _merge_fwd(y_a, y_b, proj, b_cf_out)
    z = _mm(merged, w_oo, "z_mm")
    dx_out, dx_out_b, loss_part, g_final_w = _final_fwd_bwd(x_ext, z, loss_target[0], final_norm_w.reshape(1, D))

    dmerged = _mm(dx_out_b, w_oo, "dmerged_mm", nt=True)
    g_w_o = _mm_tn(merged, dx_out_b, "g_w_o_mm")
    dy_a, dy_b, dproj, g_b_cf = _merge_bwd(dmerged, y_a, y_b, proj, b_cf_out)
    dc3 = _mm(dy_b, w_cf, "dc3_mm", nt=True)
    g_w_cf = _mm_tn(c3, dy_b, "g_w_cf_mm")
    do_n = _mm(dy_a, w_dn, "do_n_mm", nt=True)
    g_w_dn = _mm_tn(o_n, dy_a, "g_w_dn_mm")
    dc1, dproj, sums_b = _conv_b_bwd1(dc3, c1, proj, ln_w, ln_b, dproj)
    dproj, g_dw_w = _conv_b_bwd2(dc1, proj, dww, dproj)
    do, dproj, g_dn_w = _o_post_bwd(do_n, o, proj, dn_norm_w, dproj)
    dqkv, dbg = _delta_bwd(qkv, bg, sall, tall, do)
    dproj, g_cqw = _qkv_conv_bwd(proj, dqkv, cqw, dproj)
    dba, dab = _ba_bwd(dbg, ba, ab, pad)
    dh = _dh_mm(dproj, dba, w_main_t, w_ba_t)
    g_w_main = _mm_tn(h, dproj, "g_w_main_mm")
    g_w_ba = _mm_tn(h, dba, "g_w_ba_mm")
    grad_x, dhead, g_norm_w = _prenorm_bwd(dh, x_ext, dx_out, norm_w, seq)

    g_w_full = jnp.concatenate([g_w_main[:, :n_qkvz], g_w_ba[:, :n_ba], g_w_main[:, CB_GA_ * D:CB_MA * D],
                                g_w_main[:, CB_ZB * D:], g_w_main[:, CB_MA * D:CB_ZB * D]], axis=1)
    split_cols = lambda t: jnp.transpose(t.reshape(t.shape[0], NDEV, t.shape[1] // NDEV), (1, 0, 2))
    small = {"norm_w": g_norm_w, "a_log": dab[0:1, H:2 * H], "dt_bias": dab[1:2, H:2 * H], "dn_norm_w": g_dn_w,
             "dw_b": sums_b[2:3], "ln_w": sums_b[0:1], "ln_b": sums_b[1:2], "b_cf_out": g_b_cf,
             "final_norm_w": g_final_w}
    small_vec = jnp.concatenate([small[k] for k in SMALL], axis=1)
    ns = small_vec.shape[1]
    ns_pad = (-ns) % LANE
    small_vec = jnp.pad(small_vec, ((0, 0), (0, ns_pad)))
    big = [split_cols(g_w_full), g_w_dn.reshape(NDEV, D // NDEV, D), g_w_cf.reshape(NDEV, D // NDEV, D),
           g_w_o.reshape(NDEV, D // NDEV, D)]
    from_sibling = _swap_sibling(big, "swap_sibling")
    pairs = [_pair_add(a, g, f"pair_add_{i}") for i, (a, g) in enumerate(zip(big, from_sibling))]
    from_chips = _scatter_chips([p for p, _ in pairs], "scatter_chips")
    p_meta, p_cqw, p_dww, p_small = _exchange(
        [split_cols(dhead[pad:pad + NMETA]), split_cols(g_cqw), split_cols(g_dw_w), small_vec],
        [True] * 3 + [False], "exchange_small")

    res = {}
    res["w_in"] = _adam_chips(pairs[0][1], from_chips[0], w_in[0], m_w_in[0], v_w_in[0], "adam_w_in")
    res["w_dn_out"] = _adam_chips(pairs[1][1], from_chips[1], w_dn_out[0], m_w_dn_out[0], v_w_dn_out[0], "adam_w_dn")
    res["w_cf_out"] = _adam_chips(pairs[2][1], from_chips[2], w_cf_out[0], m_w_cf_out[0], v_w_cf_out[0], "adam_w_cf")
    res["w_o"] = _adam_chips(pairs[3][1], from_chips[3], w_o[0], m_w_o[0], v_w_o[0], "adam_w_o")
    res["meta"] = _adam_reduce(p_meta, meta, m_meta, v_meta, "adam_meta")
    res["conv_qkv_w"] = _adam_reduce(p_cqw, conv_qkv_w[0], m_conv_qkv_w[0], v_conv_qkv_w[0], "adam_conv_qkv_w")
    res["dw_w"] = _adam_reduce(p_dww, dw_w[0], m_dw_w[0], v_dw_w[0], "adam_dw_w")
    loc = dict(norm_w=(norm_w, m_norm_w, v_norm_w), a_log=(a_log, m_a_log, v_a_log), dt_bias=(dt_bias, m_dt_bias, v_dt_bias),
               dn_norm_w=(dn_norm_w, m_dn_norm_w, v_dn_norm_w), dw_b=(dw_b, m_dw_b, v_dw_b), ln_w=(ln_w, m_ln_w, v_ln_w),
               ln_b=(ln_b, m_ln_b, v_ln_b), b_cf_out=(b_cf_out, m_b_cf_out, v_b_cf_out),
               final_norm_w=(final_norm_w, m_final_norm_w, v_final_norm_w))
    cat = lambda j: jnp.pad(jnp.concatenate([loc[k][j].reshape(1, -1) for k in SMALL], axis=1), ((0, 0), (0, ns_pad)))
    small_res = _adam_reduce(p_small, cat(0), cat(1), cat(2), "adam_small")
    off = 0
    for k in SMALL:
        wshape = loc[k][0].shape
        nk = loc[k][0].size
        res[k] = tuple(t[:, off:off + nk].reshape(wshape) for t in small_res)
        off += nk
    shaped = dict(w_in=w_in.shape, w_dn_out=w_dn_out.shape, w_cf_out=w_cf_out.shape, w_o=w_o.shape, meta=meta.shape,
                  conv_qkv_w=conv_qkv_w.shape, dw_w=dw_w.shape)
    for k, shp in shaped.items():
        res[k] = tuple(t.reshape(shp) for t in res[k])

    loss = lax.psum(loss_part[0, 0], ("x", "y", "c"))
    order = ("meta", "norm_w", "w_in", "conv_qkv_w", "a_log", "dt_bias", "dn_norm_w", "w_dn_out", "dw_w", "dw_b", "ln_w",
             "ln_b", "w_cf_out", "b_cf_out", "w_o", "final_norm_w")
    outs = [loss, grad_x[None]]
    for j in range(4):
        outs += [res[k][j] for k in order]
    return tuple(outs)
```

```python
import functools

import jax
import jax.numpy as jnp
from jax import lax
from jax.experimental import pallas as pl
from jax.experimental.pallas import tpu as pltpu

F32 = jnp.float32
BF16 = jnp.bfloat16
HI = lax.Precision.HIGHEST

D = 1024
H = 8
DK = 128
C = 64
NMETA = 16
KQ = 4
KD = 31
HALO_Q = 8
HALO_D = 32
EPS = 1e-6
NDEV = 8
LANE = 128
MIB = 1024 * 1024

ADAM_LR, ADAM_B1, ADAM_B2, ADAM_EPS, ADAM_WD, ADAM_STEP = 0.001, 0.9, 0.999, 1e-08, 0.01, 10

CB_Q, CB_K, CB_V, CB_ZA, CB_GA_, CB_GB_, CB_MA, CB_MB, CB_ZB = range(9)
NCB = 9


def _pick(n, cands):
    for c in cands:
        if n % c == 0:
            return c
    raise ValueError(f"no tile for {n}")


def _cp(sem=None, vmem_mib=40):
    kw = dict(vmem_limit_bytes=vmem_mib * MIB)
    if sem is not None:
        kw["dimension_semantics"] = sem
    return pltpu.CompilerParams(**kw)


def _call(body, **kw):
    return pl.pallas_call(body, **kw)


def _dot(a, b):
    return jnp.dot(a.astype(BF16), b.astype(BF16), preferred_element_type=F32)


def _dot_nt(a, b):
    return lax.dot_general(a.astype(BF16), b.astype(BF16), (((1,), (1,)), ((), ())), preferred_element_type=F32)


def _dot_tn(a, b):
    return lax.dot_general(a.astype(BF16), b.astype(BF16), (((0,), (0,)), ((), ())), preferred_element_type=F32)


def _dot_hi(a, b):
    return jnp.dot(a, b, precision=HI, preferred_element_type=F32)


def _sig(x):
    return 0.5 * jnp.tanh(0.5 * x) + 0.5


def _dsilu(x, s):
    return s * (1.0 + x * (1.0 - s))


def _rowsum(x):
    return jnp.sum(x, axis=-1, keepdims=True)


def _colsum(x):
    return jnp.sum(x, axis=0, keepdims=True)


def _exchange(arrs, scatter, name):
    n = len(arrs)
    out_shape = []
    for a, sc in zip(arrs, scatter):
        shp = a.shape if sc else (NDEV,) + a.shape
        out_shape.append(jax.ShapeDtypeStruct(shp, a.dtype))

    def body(*refs):
        ins, outs = refs[:n], refs[n:2 * n]
        send_sems, recv_sems, loc_sems = refs[2 * n:]
        x, y, c = lax.axis_index("x"), lax.axis_index("y"), lax.axis_index("c")
        me = 4 * x + 2 * y + c
        copies = []
        for a in range(n):
            for k in range(1, NDEV):
                px = 1 - x if (k >> 2) & 1 else x
                py = 1 - y if (k >> 1) & 1 else y
                pc = 1 - c if k & 1 else c
                src = ins[a].at[4 * px + 2 * py + pc] if scatter[a] else ins[a]
                cp = pltpu.make_async_remote_copy(
                    src_ref=src, dst_ref=outs[a].at[me],
                    send_sem=send_sems.at[a * (NDEV - 1) + k - 1], recv_sem=recv_sems.at[a * (NDEV - 1) + k - 1],
                    device_id=(px, py, pc), device_id_type=pl.DeviceIdType.MESH)
                cp.start()
                copies.append(cp)
            loc = pltpu.make_async_copy(ins[a].at[me] if scatter[a] else ins[a], outs[a].at[me], loc_sems.at[a])
            loc.start()
            copies.append(loc)
        for cp in copies:
            cp.wait()

    any_spec = pl.BlockSpec(memory_space=pl.ANY)
    return _call(
        body, name=name, out_shape=tuple(out_shape),
        in_specs=[any_spec] * n, out_specs=tuple([any_spec] * n),
        scratch_shapes=[pltpu.SemaphoreType.DMA((n * (NDEV - 1),)), pltpu.SemaphoreType.DMA((n * (NDEV - 1),)),
                        pltpu.SemaphoreType.DMA((n,))],
    )(*arrs)


NCHIP = 4


def _gather_two_level(arrs, name):
    n = len(arrs)
    per = NDEV - 1

    def body(*refs):
        ins, outs = refs[:n], refs[n:2 * n]
        send_sems, recv_sems, loc_sems = refs[2 * n:]
        x, y, c = lax.axis_index("x"), lax.axis_index("y"), lax.axis_index("c")
        me, sibling = (x, y, c), (x, y, 1 - c)
        chips = [(1 - x, y), (x, 1 - y), (1 - x, 1 - y)]

        def slot(a, px, py, pc):
            return outs[a].at[4 * px + 2 * py + pc]

        def copy(a, k, block, to, src=None):
            return pltpu.make_async_remote_copy(
                src_ref=slot(a, *block) if src is None else src, dst_ref=slot(a, *block),
                send_sem=send_sems.at[a * per + k], recv_sem=recv_sems.at[a * per + k],
                device_id=to, device_id_type=pl.DeviceIdType.MESH)

        local, sent = [], []
        for a in range(n):
            mine = pltpu.make_async_copy(ins[a], slot(a, *me), loc_sems.at[a])
            mine.start()
            local.append(mine)
            first = [copy(a, 1 + j, me, (*chip, c), src=ins[a]) for j, chip in enumerate(chips)]
            first.append(copy(a, 0, me, sibling, src=ins[a]))
            for cp in first:
                cp.start()
            sent += first
        for j, chip in enumerate(chips):
            for a in range(n):
                copy(a, 1 + j, (*chip, c), me).wait_recv()
                cp = copy(a, 4 + j, (*chip, c), sibling)
                cp.start()
                sent.append(cp)
        for a in range(n):
            copy(a, 0, sibling, me).wait_recv()
            for j, chip in enumerate(chips):
                copy(a, 4 + j, (*chip, 1 - c), me).wait_recv()
        for cp in sent:
            cp.wait_send()
        for cp in local:
            cp.wait()

    any_spec = pl.BlockSpec(memory_space=pl.ANY)
    return _call(
        body, name=name, out_shape=tuple(jax.ShapeDtypeStruct((NDEV,) + a.shape, a.dtype) for a in arrs),
        in_specs=[any_spec] * n, out_specs=tuple([any_spec] * n),
        scratch_shapes=[pltpu.SemaphoreType.DMA((n * per,)), pltpu.SemaphoreType.DMA((n * per,)),
                        pltpu.SemaphoreType.DMA((n,))],
    )(*arrs)


def _swap_sibling(arrs, name):
    n = len(arrs)

    def body(*refs):
        ins, outs = refs[:n], refs[n:2 * n]
        send_sems, recv_sems = refs[2 * n:]
        x, y, c = lax.axis_index("x"), lax.axis_index("y"), lax.axis_index("c")
        copies = []
        for a in range(n):
            for j in range(NCHIP):
                cp = pltpu.make_async_remote_copy(
                    src_ref=ins[a].at[2 * j + (1 - c)], dst_ref=outs[a].at[j],
                    send_sem=send_sems.at[a * NCHIP + j], recv_sem=recv_sems.at[a * NCHIP + j],
                    device_id=(x, y, 1 - c), device_id_type=pl.DeviceIdType.MESH)
                cp.start()
                copies.append(cp)
        for cp in copies:
            cp.wait()

    any_spec = pl.BlockSpec(memory_space=pl.ANY)
    return _call(
        body, name=name, out_shape=tuple(jax.ShapeDtypeStruct((NCHIP,) + a.shape[1:], a.dtype) for a in arrs),
        in_specs=[any_spec] * n, out_specs=tuple([any_spec] * n),
        scratch_shapes=[pltpu.SemaphoreType.DMA((n * NCHIP,)), pltpu.SemaphoreType.DMA((n * NCHIP,))],
    )(*arrs)


def _pair_add(arr, got, name):
    _, r, n = arr.shape
    tr = _pick(r, (128,))
    arr4 = arr.reshape(NCHIP, 2, r, n)

    def body(a_ref, g_ref, p_ref, own_ref):
        c = lax.axis_index("c")
        my_chip = 2 * lax.axis_index("x") + lax.axis_index("y")
        s = jnp.where(c == 0, a_ref[0, 0], a_ref[0, 1]) + g_ref[0]
        p_ref[0] = s.astype(BF16)

        @pl.when(pl.program_id(1) == my_chip)
        def _():
            own_ref[...] = s

    return _call(
        body, name=name, grid=(r // tr, NCHIP),
        in_specs=[pl.BlockSpec((1, 2, tr, n), lambda i, j: (j, 0, i, 0)), pl.BlockSpec((1, tr, n), lambda i, j: (j, i, 0))],
        out_specs=(pl.BlockSpec((1, tr, n), lambda i, j: (j, i, 0)), pl.BlockSpec((tr, n), lambda i, j: (i, 0))),
        out_shape=(jax.ShapeDtypeStruct((NCHIP, r, n), BF16), jax.ShapeDtypeStruct((r, n), F32)),
        compiler_params=_cp(("parallel", "arbitrary")),
    )(arr4, got)


def _scatter_chips(arrs, name):
    n = len(arrs)
    per = NCHIP - 1

    def body(*refs):
        ins, outs = refs[:n], refs[n:2 * n]
        send_sems, recv_sems, loc_sems = refs[2 * n:]
        x, y, c = lax.axis_index("x"), lax.axis_index("y"), lax.axis_index("c")
        copies = []
        for a in range(n):
            loc = pltpu.make_async_copy(ins[a].at[2 * x + y], outs[a].at[2 * x + y], loc_sems.at[a])
            loc.start()
            copies.append(loc)
            for k in range(1, NCHIP):
                px = 1 - x if (k >> 1) & 1 else x
                py = 1 - y if k & 1 else y
                cp = pltpu.make_async_remote_copy(
                    src_ref=ins[a].at[2 * px + py], dst_ref=outs[a].at[2 * x + y],
                    send_sem=send_sems.at[a * per + k - 1], recv_sem=recv_sems.at[a * per + k - 1],
                    device_id=(px, py, c), device_id_type=pl.DeviceIdType.MESH)
                cp.start()
                copies.append(cp)
        for cp in copies:
            cp.wait()

    any_spec = pl.BlockSpec(memory_space=pl.ANY)
    return _call(
        body, name=name, out_shape=tuple(jax.ShapeDtypeStruct(a.shape, a.dtype) for a in arrs),
        in_specs=[any_spec] * n, out_specs=tuple([any_spec] * n),
        scratch_shapes=[pltpu.SemaphoreType.DMA((n * per,)), pltpu.SemaphoreType.DMA((n * per,)),
                        pltpu.SemaphoreType.DMA((n,))],
    )(*arrs)


def _mm_tn(a, b, name):
    t, m = a.shape
    n = b.shape[1]
    tt = _pick(t, (1664, 640, 128))
    tm = _pick(m, (1024, 512, 128))
    tn = _pick(n, (1152, 1024, 512, 128))
    nt = t // tt

    def body(a_ref, b_ref, o_ref):
        s = pl.program_id(2)
        part = _dot_tn(a_ref[...], b_ref[...])

        @pl.when(s == 0)
        def _():
            o_ref[...] = part

        @pl.when(s > 0)
        def _():
            o_ref[...] += part

    return _call(
        body, name=name, grid=(m // tm, n // tn, nt),
        in_specs=[pl.BlockSpec((tt, tm), lambda i, j, s: (s, i)), pl.BlockSpec((tt, tn), lambda i, j, s: (s, j))],
        out_specs=pl.BlockSpec((tm, tn), lambda i, j, s: (i, j)),
        out_shape=jax.ShapeDtypeStruct((m, n), F32),
        compiler_params=_cp(("parallel", "parallel", "arbitrary")),
    )(a, b)


def _proj_fwd(x_ext, norm_w, w_main_t, w_ba_t):
    lp = x_ext.shape[0]
    n = w_main_t.shape[0]
    tm = _pick(lp, (832, 640, 320))
    tn = 1024

    def body(x_ref, nw_ref, w_ref, wba_ref, proj_ref, ba_ref, h_ref):
        @pl.when(pl.program_id(1) == 0)
        def _():
            x = x_ref[...]
            r = lax.rsqrt(jnp.mean(x * x, axis=-1, keepdims=True) + EPS)
            h = (x * r * nw_ref[...]).astype(BF16)
            h_ref[...] = h
            ba_ref[...] = _dot_nt(h, wba_ref[...])

        proj_ref[...] = _dot_nt(h_ref[...], w_ref[...])

    return _call(
        body, name="proj_fwd", grid=(lp // tm, n // tn),
        in_specs=[pl.BlockSpec((tm, D), lambda i, j: (i, 0)), pl.BlockSpec((1, D), lambda i, j: (0, 0)),
                  pl.BlockSpec((tn, D), lambda i, j: (j, 0)), pl.BlockSpec((LANE, D), lambda i, j: (0, 0))],
        out_specs=(pl.BlockSpec((tm, tn), lambda i, j: (i, j)), pl.BlockSpec((tm, LANE), lambda i, j: (i, 0)),
                   pl.BlockSpec((tm, D), lambda i, j: (i, 0))),
        out_shape=(jax.ShapeDtypeStruct((lp, n), F32), jax.ShapeDtypeStruct((lp, LANE), F32),
                   jax.ShapeDtypeStruct((lp, D), BF16)),
        compiler_params=_cp(("parallel", "arbitrary")),
    )(x_ext, norm_w, w_main_t, w_ba_t)


def _dh_mm(dproj, dba, w_main_t, w_ba_t):
    lp, n = dproj.shape
    tm = _pick(lp, (832, 640, 320))
    tn = 1024
    tk = 2304
    nk = n // tk

    def body(a_ref, ba_ref, b_ref, bba_ref, o_ref, acc):
        kk = pl.program_id(2)

        @pl.when(kk == 0)
        def _():
            acc[...] = jnp.dot(ba_ref[...], bba_ref[...], preferred_element_type=F32)

        acc[...] += jnp.dot(a_ref[...], b_ref[...], preferred_element_type=F32)

        @pl.when(kk == nk - 1)
        def _():
            o_ref[...] = acc[...]

    return _call(
        body, name="dh_mm", grid=(lp // tm, D // tn, nk),
        in_specs=[pl.BlockSpec((tm, tk), lambda i, j, kk: (i, kk)), pl.BlockSpec((tm, LANE), lambda i, j, kk: (i, 0)),
                  pl.BlockSpec((tk, tn), lambda i, j, kk: (kk, j)), pl.BlockSpec((LANE, tn), lambda i, j, kk: (0, j))],
        out_specs=pl.BlockSpec((tm, tn), lambda i, j, kk: (i, j)),
        out_shape=jax.ShapeDtypeStruct((lp, D), F32),
        scratch_shapes=[pltpu.VMEM((tm, tn), F32)],
        compiler_params=_cp(("parallel", "parallel", "arbitrary")),
    )(dproj, dba, w_main_t, w_ba_t)


def _beta_g(ba, ab, row0, pad):
    lane = lax.broadcasted_iota(jnp.int32, ba.shape, 1)
    rows = row0 + lax.broadcasted_iota(jnp.int32, ba.shape, 0)
    z = ba + ab[1:2, :]
    sp = jnp.maximum(z, 0.0) + jnp.log(1.0 + jnp.exp(-jnp.abs(z)))
    val = jnp.where(lane < H, _sig(ba), -jnp.exp(ab[0:1, :]) * sp)
    return jnp.where((lane < 2 * H) & (rows >= pad), val, 0.0)


def _qkv_conv_fwd(proj, ba, conv_w, ab, pad):
    lp = proj.shape[0]
    te = _pick(lp, (320,))
    hb = te // HALO_Q

    def body(main_ref, halo_ref, cw_ref, ba_ref, ab_ref, out_ref, bg_ref, pre_scr, tap_scr):
        i, s = pl.program_id(0), pl.program_id(1)
        pre_scr[:HALO_Q, :] = jnp.where(i > 0, halo_ref[...], 0.0)
        pre_scr[HALO_Q:, :] = main_ref[...]
        scale = jnp.where(s == 0, DK ** -0.5, 1.0)
        off = HALO_Q - (KQ - 1)

        def head(h, carry):
            cs = pl.ds(pl.multiple_of(h * DK, DK), DK)
            for j in range(KQ - 1):
                tap_scr[j] = pre_scr[off + j:off + j + te, cs]
            co = cw_ref[KQ - 1:KQ, cs] * pre_scr[HALO_Q:, cs]
            for j in range(KQ - 1):
                co = co + cw_ref[j:j + 1, cs] * tap_scr[j]
            a = co * _sig(co)
            r = lax.rsqrt(_rowsum(a * a) + EPS)
            out_ref[:, cs] = jnp.where(s == 2, a, a * (r * scale))
            return carry

        lax.fori_loop(0, H, head, 0, unroll=True)

        @pl.when(s == 0)
        def _():
            bg_ref[...] = _beta_g(ba_ref[...], ab_ref[...], i * te, pad)

    return _call(
        body, name="qkv_conv_fwd", grid=(lp // te, 3),
        in_specs=[pl.BlockSpec((te, D), lambda i, s: (i, s)),
                  pl.BlockSpec((HALO_Q, D), lambda i, s: (jnp.maximum(i * hb - 1, 0), s)),
                  pl.BlockSpec((KQ, D), lambda i, s: (0, s)),
                  pl.BlockSpec((te, LANE), lambda i, s: (i, 0)),
                  pl.BlockSpec((2, LANE), lambda i, s: (0, 0))],
        out_specs=(pl.BlockSpec((te, D), lambda i, s: (i, s)), pl.BlockSpec((te, LANE), lambda i, s: (i, 0))),
        out_shape=(jax.ShapeDtypeStruct((lp, 3 * D), F32), jax.ShapeDtypeStruct((lp, LANE), F32)),
        scratch_shapes=[pltpu.VMEM((te + HALO_Q, D), F32), pltpu.VMEM((KQ - 1, te, DK), F32)],
        compiler_params=_cp(("parallel", "arbitrary")),
    )(proj, proj, conv_w, ba, ab)


def _tri_masks():
    row = lax.broadcasted_iota(jnp.int32, (C, C), 0)
    col = lax.broadcasted_iota(jnp.int32, (C, C), 1)
    return row, col


def _split(a):
    hi = a.astype(BF16)
    return hi, (a - hi.astype(F32)).astype(BF16)


def _dot3(a, b, dims=(((1,), (0,)), ((), ()))):
    (ah, al), (bh, bl) = a, b
    mm = lambda x, y: lax.dot_general(x, y, dims, preferred_element_type=F32)
    return mm(ah, bh) + (mm(ah, bl) + mm(al, bh))


def _tinv(ys, eye):
    ts = [eye + y for y in ys]
    sp = [_split(y) for y in ys]
    yks = [_dot3(s, s) for s in sp]
    for _ in range(4):
        sp = [_split(yk) for yk in yks]
        ts = [t + _dot3(s, _split(t)) for s, t in zip(sp, ts)]
        yks = [_dot3(s, s) for s in sp]
    return [t + _dot3(_split(yk), _split(t)) for yk, t in zip(yks, ts)]


def _chunk_common(q, k, v, bcol, gcc, gcr, incl, strict):
    dm = jnp.where(incl, jnp.exp(gcc - gcr), 0.0)
    kk = _dot_nt(k, k)
    qk = _dot_nt(q, k)
    egc = jnp.exp(gcc)
    glast = gcc[C - 1:C, :]
    eend = jnp.exp(glast - gcc)
    elast = jnp.exp(glast)
    rhs = jnp.concatenate([v * bcol, k * (bcol * egc)], axis=1)
    return dm, kk, qk, egc, eend, elast, rhs


def _delta_fwd(qkv, bg):
    lp = qkv.shape[0]
    nc = lp // C

    def body(q_ref, k_ref, v_ref, bg_ref, o_ref, sall_ref, tall_ref, s_scr):
        @pl.when(pl.program_id(0) == 0)
        def _():
            s_scr[...] = jnp.zeros_like(s_scr)

        bgt = bg_ref[...]
        row, col = _tri_masks()
        incl, strict = row >= col, row > col
        eye = (row == col).astype(F32)
        gc_all = _dot_hi(incl.astype(F32), bgt)
        gc_t = _dot_hi(bgt.T, (row <= col).astype(F32))
        heads = range(H)
        sls = [slice(h * DK, (h + 1) * DK) for h in heads]
        qs, ks, vs = ([r[:, sl] for sl in sls] for r in (q_ref, k_ref, v_ref))
        bcols = [bgt[:, h:h + 1] for h in heads]
        cm = [_chunk_common(qs[h], ks[h], vs[h], bcols[h], gc_all[:, H + h:H + h + 1], gc_t[H + h:H + h + 1, :],
                            incl, strict) for h in heads]
        dms, kks, qks, egcs, eends, elasts, rhss = zip(*cm)
        ts = _tinv([jnp.where(strict, -(bcols[h] * kks[h] * dms[h]), 0.0) for h in heads], eye)
        sols = [_dot3(_split(ts[h]), _split(rhss[h])) for h in heads]
        ss = [s_scr[h] for h in heads]
        sb = [s.astype(BF16) for s in ss]
        wvs = [sols[h][:, :DK] - _dot(sols[h][:, DK:], sb[h]) for h in heads]
        wvb = [wv.astype(BF16) for wv in wvs]
        for h in heads:
            o_ref[:, sls[h]] = _dot(qs[h] * egcs[h], sb[h]) + _dot(qks[h] * dms[h], wvb[h])
            sall_ref[0, h] = ss[h]
            tall_ref[0, h] = ts[h]
        for h in heads:
            s_scr[h] = ss[h] * elasts[h] + _dot_tn(ks[h] * eends[h], wvb[h])

    blk = lambda j: pl.BlockSpec((C, D), lambda n: (n, j))
    return _call(
        body, name="delta_fwd", grid=(nc,),
        in_specs=[blk(0), blk(1), blk(2), pl.BlockSpec((C, LANE), lambda n: (n, 0))],
        out_specs=(pl.BlockSpec((C, D), lambda n: (n, 0)),
                   pl.BlockSpec((1, H, DK, DK), lambda n: (n, 0, 0, 0)),
                   pl.BlockSpec((1, H, C, C), lambda n: (n, 0, 0, 0))),
        out_shape=(jax.ShapeDtypeStruct((lp, D), F32), jax.ShapeDtypeStruct((nc, H, DK, DK), F32),
                   jax.ShapeDtypeStruct((nc, H, C, C), F32)),
        scratch_shapes=[pltpu.VMEM((H, DK, DK), F32)],
        compiler_params=_cp(("arbitrary",)),
    )(qkv, qkv, qkv, bg)


def _delta_bwd(qkv, bg, sall, tall, do):
    lp = qkv.shape[0]
    nc = lp // C

    def body(q_ref, k_ref, v_ref, bg_ref, sall_ref, tall_ref, do_ref, dqkv_ref, dbg_ref, ds_scr):
        @pl.when(pl.program_id(0) == 0)
        def _():
            ds_scr[...] = jnp.zeros_like(ds_scr)

        bgt = bg_ref[...]
        row, col = _tri_masks()
        incl, strict = row >= col, row > col
        upper = (row <= col).astype(F32)
        gc_all = _dot_hi(incl.astype(F32), bgt)
        gc_t = _dot_hi(bgt.T, upper)
        lane = lax.broadcasted_iota(jnp.int32, (C, LANE), 1)
        lastrow = lax.broadcasted_iota(jnp.int32, (C, 1), 0) == C - 1
        heads = range(H)
        sls = [slice(h * DK, (h + 1) * DK) for h in heads]
        qs, ks, vs, dos = ([r[:, sl] for sl in sls] for r in (q_ref, k_ref, v_ref, do_ref))
        bcols = [bgt[:, h:h + 1] for h in heads]
        cm = [_chunk_common(qs[h], ks[h], vs[h], bcols[h], gc_all[:, H + h:H + h + 1], gc_t[H + h:H + h + 1, :],
                            incl, strict) for h in heads]
        dms, kks, qks, egcs, eends, elasts, rhss = zip(*cm)
        ss = [sall_ref[0, h] for h in heads]
        ts = [tall_ref[0, h] for h in heads]
        dsns = [ds_scr[h] for h in heads]
        sb = [s.astype(BF16) for s in ss]
        dsb = [d.astype(BF16) for d in dsns]
        dob = [d.astype(BF16) for d in dos]
        sols = [_dot3(_split(ts[h]), _split(rhss[h])) for h in heads]
        ws = [sol[:, DK:] for sol in sols]
        qgs = [qs[h] * egcs[h] for h in heads]
        kends = [ks[h] * eends[h] for h in heads]
        wvs = [sols[h][:, :DK] - _dot(ws[h], sb[h]) for h in heads]
        wvb = [wv.astype(BF16) for wv in wvs]
        dwvs = [_dot_tn(qks[h] * dms[h], dob[h]) + _dot(kends[h], dsb[h]) for h in heads]
        dps = [jnp.where(incl, _dot_nt(dob[h], wvb[h]), 0.0) for h in heads]
        dqgs = [_dot_nt(dob[h], sb[h]) for h in heads]
        dkends = [_dot_nt(wvb[h], dsb[h]) for h in heads]
        for h in heads:
            ds_scr[h] = _dot_tn(qgs[h], dob[h]) + elasts[h] * dsns[h] - _dot_tn(ws[h], dwvs[h])
        dglasts = [elasts[h] * jnp.sum(ss[h] * dsns[h], keepdims=True) for h in heads]
        dws = [-_dot_nt(dwvs[h], sb[h]) for h in heads]
        tts = [_split(ts[h].T) for h in heads]
        drhss = [_dot3(tts[h], _split(jnp.concatenate([dwvs[h], dws[h]], axis=1))) for h in heads]
        nt_dims = (((1,), (1,)), ((), ()))
        dns = [jnp.where(strict, -_dot3(_split(drhss[h]), _split(sols[h]), nt_dims), 0.0) for h in heads]
        dbeta_t = jnp.zeros((C, LANE), F32)
        dgc_t = jnp.zeros((C, LANE), F32)
        for h in heads:
            q, k, v, bcol, dm, kk, qk, egc, eend = qs[h], ks[h], vs[h], bcols[h], dms[h], kks[h], qks[h], egcs[h], eends[h]
            drv, drk = drhss[h][:, :DK], drhss[h][:, DK:]
            dn, dp, dqg, dkend = dns[h], dps[h], dqgs[h], dkends[h]
            rk = _rowsum(drk * k)
            dkk = dn * (bcol * dm)
            dqk = dp * dm
            e = (dn * (bcol * kk) + dp * qk) * dm
            tk = _rowsum(dkend * kends[h])
            dgc = rk * bcol * egc + _rowsum(e) - _rowsum(e.T) + _rowsum(dqg * qgs[h]) - tk
            dgc = dgc + jnp.where(lastrow, dglasts[h] + jnp.sum(tk, keepdims=True), 0.0)
            dbeta = _rowsum(drv * v) + rk * egc + _rowsum(dn * kk * dm)
            dqkv_ref[:, sls[h]] = _dot(dqk, k) + dqg * egc
            dqkv_ref[:, D + h * DK:D + (h + 1) * DK] = (drk * (bcol * egc) + _dot(dkk, k) + _dot_tn(dkk, k)
                                                       + _dot_tn(dqk, q) + dkend * eend)
            dqkv_ref[:, 2 * D + h * DK:2 * D + (h + 1) * DK] = bcol * drv
            dbeta_t = jnp.where(lane == h, dbeta, dbeta_t)
            dgc_t = jnp.where(lane == H + h, dgc, dgc_t)
        dbg_ref[...] = dbeta_t + _dot_hi(upper, dgc_t)

    rev = lambda n: nc - 1 - n
    blk = lambda j: pl.BlockSpec((C, D), lambda n: (rev(n), j))
    return _call(
        body, name="delta_bwd", grid=(nc,),
        in_specs=[blk(0), blk(1), blk(2), pl.BlockSpec((C, LANE), lambda n: (rev(n), 0)),
                  pl.BlockSpec((1, H, DK, DK), lambda n: (rev(n), 0, 0, 0)),
                  pl.BlockSpec((1, H, C, C), lambda n: (rev(n), 0, 0, 0)),
                  pl.BlockSpec((C, D), lambda n: (rev(n), 0))],
        out_specs=(pl.BlockSpec((C, 3 * D), lambda n: (rev(n), 0)), pl.BlockSpec((C, LANE), lambda n: (rev(n), 0))),
        out_shape=(jax.ShapeDtypeStruct((lp, 3 * D), F32), jax.ShapeDtypeStruct((lp, LANE), F32)),
        scratch_shapes=[pltpu.VMEM((H, DK, DK), F32)],
        compiler_params=_cp(("arbitrary",)),
    )(qkv, qkv, qkv, bg, sall, tall, do)


def _o_post_fwd(o, proj, dn_w, w_dn):
    lp = o.shape[0]
    te = _pick(lp, (640, 320))

    def body(o_ref, za_ref, w_ref, wdn_ref, out_ref, ya_ref):
        za = za_ref[...]
        gate = za * _sig(za)
        for h in range(H):
            sl = slice(h * DK, (h + 1) * DK)
            oh = o_ref[:, sl]
            r = lax.rsqrt(jnp.mean(oh * oh, axis=-1, keepdims=True) + EPS)
            out_ref[:, sl] = (oh * r * w_ref[...] * gate[:, sl]).astype(BF16)
        ya_ref[...] = _dot(out_ref[...], wdn_ref[...])

    row = pl.BlockSpec((te, D), lambda i: (i, 0))
    return _call(
        body, name="o_post_fwd", grid=(lp // te,),
        in_specs=[row, pl.BlockSpec((te, D), lambda i: (i, CB_ZA)), pl.BlockSpec((1, DK), lambda i: (0, 0)),
                  pl.BlockSpec((D, D), lambda i: (0, 0))],
        out_specs=(row, row),
        out_shape=(jax.ShapeDtypeStruct((lp, D), BF16), jax.ShapeDtypeStruct((lp, D), F32)),
        compiler_params=_cp(("parallel",)),
    )(o, proj, dn_w, w_dn)


def _o_post_bwd(dy_a, w_dn, o, proj, dn_w, dproj):
    lp = o.shape[0]
    te = _pick(lp, (320,))

    def body(dya_ref, wdn_ref, o_ref, za_ref, w_ref, _, do_ref, dza_ref, dw_ref, don_ref):
        @pl.when(pl.program_id(0) == 0)
        def _():
            dw_ref[...] = jnp.zeros_like(dw_ref)

        don_ref[...] = _dot_nt(dya_ref[...], wdn_ref[...])
        za = za_ref[...]
        sz = _sig(za)
        gate, dgate = za * sz, _dsilu(za, sz)
        w = w_ref[...]
        dw = jnp.zeros((1, DK), F32)
        for h in range(H):
            sl = slice(h * DK, (h + 1) * DK)
            oh, g = o_ref[:, sl], don_ref[:, sl]
            r = lax.rsqrt(jnp.mean(oh * oh, axis=-1, keepdims=True) + EPS)
            ohat = oh * r
            dza_ref[:, sl] = (g * ohat * w * dgate[:, sl]).astype(BF16)
            don = g * gate[:, sl]
            dw = dw + _colsum(don * ohat)
            dohat = don * w
            do_ref[:, sl] = r * (dohat - ohat * jnp.mean(dohat * ohat, axis=-1, keepdims=True))
        dw_ref[...] += dw

    return _call(
        body, name="o_post_bwd", grid=(lp // te,),
        in_specs=[pl.BlockSpec((te, D), lambda i: (i, 0)), pl.BlockSpec((D, D), lambda i: (0, 0)),
                  pl.BlockSpec((te, D), lambda i: (i, 0)),
                  pl.BlockSpec((te, D), lambda i: (i, CB_ZA)), pl.BlockSpec((1, DK), lambda i: (0, 0)),
                  pl.BlockSpec(memory_space=pl.ANY)],
        out_specs=(pl.BlockSpec((te, D), lambda i: (i, 0)), pl.BlockSpec((te, D), lambda i: (i, CB_ZA)),
                   pl.BlockSpec((1, DK), lambda i: (0, 0))),
        out_shape=(jax.ShapeDtypeStruct((lp, D), F32), jax.ShapeDtypeStruct(dproj.shape, dproj.dtype),
                   jax.ShapeDtypeStruct((1, DK), F32)),
        input_output_aliases={5: 1},
        scratch_shapes=[pltpu.VMEM((te, D), F32)],
        compiler_params=_cp(("arbitrary",)),
    )(dy_a, w_dn, o, proj, dn_w, dproj)


def _qkv_conv_bwd(proj, dqkv, conv_w, dproj):
    lp = proj.shape[0]
    te = _pick(lp, (320,))
    hb = te // HALO_Q
    nt = lp // te
    last_hb = lp // HALO_Q - 1

    def body(main_ref, prev_ref, next_ref, dmain_ref, dnext_ref, cw_ref, _, dpre_ref, dcw_ref, pre_scr, dn_scr,
             tap_scr, dco_scr, dsh_scr):
        s, i = pl.program_id(0), pl.program_id(1)

        @pl.when(i == 0)
        def _():
            dcw_ref[...] = jnp.zeros_like(dcw_ref)

        ne = te + HALO_Q
        pre_scr[:HALO_Q, :] = jnp.where(i > 0, prev_ref[...], 0.0)
        pre_scr[HALO_Q:ne, :] = main_ref[...]
        pre_scr[ne:, :] = jnp.where(i < nt - 1, next_ref[...], 0.0)
        dn_scr[:te, :] = dmain_ref[...]
        dn_scr[te:, :] = jnp.where(i < nt - 1, dnext_ref[...], 0.0)
        scale = jnp.where(s == 0, DK ** -0.5, 1.0)
        off = HALO_Q - (KQ - 1)

        def head(h, carry):
            cs = pl.ds(pl.multiple_of(h * DK, DK), DK)
            for j in range(KQ - 1):
                tap_scr[j] = pre_scr[off + j:off + j + ne, cs]
            taps = [tap_scr[j] for j in range(KQ - 1)] + [pre_scr[HALO_Q:, cs]]
            co = cw_ref[0:1, cs] * taps[0]
            for j in range(1, KQ):
                co = co + cw_ref[j:j + 1, cs] * taps[j]
            sg = _sig(co)
            a = co * sg
            g = dn_scr[:, cs]
            r = lax.rsqrt(_rowsum(a * a) + EPS)
            yhat = a * r
            da = jnp.where(s == 2, g, (scale * r) * (g - yhat * _rowsum(g * yhat)))
            dco = da * _dsilu(co, sg)
            dco_scr[...] = dco
            for j in range(KQ - 1):
                dsh_scr[j] = dco_scr[KQ - 1 - j:KQ - 1 - j + te, :]
            dpre = cw_ref[KQ - 1:KQ, cs] * dco[:te, :]
            for j in range(KQ - 1):
                dpre = dpre + cw_ref[j:j + 1, cs] * dsh_scr[j]
            dpre_ref[:, cs] = dpre.astype(BF16)
            dcw_ref[:, cs] += jnp.concatenate([_colsum(dco[:te] * taps[j][:te]) for j in range(KQ)], axis=0)
            return carry

        lax.fori_loop(0, H, head, 0, unroll=True)

    return _call(
        body, name="qkv_conv_bwd", grid=(3, nt),
        in_specs=[pl.BlockSpec((te, D), lambda s, i: (i, s)),
                  pl.BlockSpec((HALO_Q, D), lambda s, i: (jnp.maximum(i * hb - 1, 0), s)),
                  pl.BlockSpec((HALO_Q, D), lambda s, i: (jnp.minimum((i + 1) * hb, last_hb), s)),
                  pl.BlockSpec((te, D), lambda s, i: (i, s)),
                  pl.BlockSpec((HALO_Q, D), lambda s, i: (jnp.minimum((i + 1) * hb, last_hb), s)),
                  pl.BlockSpec((KQ, D), lambda s, i: (0, s)),
                  pl.BlockSpec(memory_space=pl.ANY)],
        out_specs=(pl.BlockSpec((te, D), lambda s, i: (i, s)), pl.BlockSpec((KQ, D), lambda s, i: (0, s))),
        out_shape=(jax.ShapeDtypeStruct(dproj.shape, dproj.dtype), jax.ShapeDtypeStruct((KQ, 3 * D), F32)),
        input_output_aliases={6: 0},
        scratch_shapes=[pltpu.VMEM((te + 2 * HALO_Q, D), F32), pltpu.VMEM((te + HALO_Q, D), F32),
                        pltpu.VMEM((KQ - 1, te + HALO_Q, DK), F32), pltpu.VMEM((te + HALO_Q, DK), F32),
                        pltpu.VMEM((KQ - 1, te, DK), F32)],
        compiler_params=_cp(("arbitrary", "arbitrary")),
    )(proj, proj, proj, dqkv, dqkv, conv_w, dproj)


def _ba_bwd(dbg, ba, ab, pad):
    lp = ba.shape[0]
    te = _pick(lp, (640, 320))

    def body(dbg_ref, ba_ref, ab_ref, dba_ref, dab_ref):
        i = pl.program_id(0)

        @pl.when(i == 0)
        def _():
            dab_ref[...] = jnp.zeros_like(dab_ref)

        ba, ab = ba_ref[...], ab_ref[...]
        lane = lax.broadcasted_iota(jnp.int32, ba.shape, 1)
        rows = i * te + lax.broadcasted_iota(jnp.int32, ba.shape, 0)
        g = jnp.where((lane < 2 * H) & (rows >= pad), dbg_ref[...], 0.0)
        sb = _sig(ba)
        z = ba + ab[1:2, :]
        sp = jnp.maximum(z, 0.0) + jnp.log(1.0 + jnp.exp(-jnp.abs(z)))
        nea = -jnp.exp(ab[0:1, :])
        dz = g * nea * _sig(z)
        dba_ref[...] = jnp.where(lane < H, g * sb * (1.0 - sb), dz).astype(BF16)
        is_g = (lane >= H) & (lane < 2 * H)
        dab_ref[...] += jnp.concatenate([_colsum(jnp.where(is_g, g * nea * sp, 0.0)),
                                         _colsum(jnp.where(is_g, dz, 0.0))], axis=0)

    return _call(
        body, name="ba_bwd", grid=(lp // te,),
        in_specs=[pl.BlockSpec((te, LANE), lambda i: (i, 0)), pl.BlockSpec((te, LANE), lambda i: (i, 0)),
                  pl.BlockSpec((2, LANE), lambda i: (0, 0))],
        out_specs=(pl.BlockSpec((te, LANE), lambda i: (i, 0)), pl.BlockSpec((2, LANE), lambda i: (0, 0))),
        out_shape=(jax.ShapeDtypeStruct((lp, LANE), BF16), jax.ShapeDtypeStruct((2, LANE), F32)),
        compiler_params=_cp(("arbitrary",)),
    )(dbg, ba, ab)


SUBLANES = 8
CONV_RB = 64


def _fill_shifted(sh_scr, src_scr, cs):
    n = sh_scr.shape[1]
    for s in range(1, SUBLANES):
        sh_scr[s] = src_scr[s:s + n, cs]


def _shifted(sh_scr, src_scr, cs, r, r0, n):
    s, a8 = r % SUBLANES, r - r % SUBLANES
    if s == 0:
        return src_scr[r0 + a8:r0 + a8 + n, cs]
    return sh_scr[s, r0 + a8:r0 + a8 + n, :]


def _conv_b_fwd(proj, dw_w, dw_b, ln_w, ln_b, w_cf):
    lp = proj.shape[0]
    te = _pick(lp, (320,))
    hb = te // HALO_D

    def body(a_ref, b_ref, ha_ref, hb_ref, zb_ref, w_ref, wb_ref, lw_ref, lb_ref, wcf_ref, c1_ref, c3_ref, yb_ref,
             c0_scr, sh_scr):
        i = pl.program_id(0)
        c0_scr[:HALO_D, :] = jnp.where(i > 0, ha_ref[...] * _sig(hb_ref[...]), 0.0)
        c0_scr[HALO_D:, :] = a_ref[...] * _sig(b_ref[...])
        off = HALO_D - (KD - 1)
        def lane_block(cb, carry):
            cs = pl.ds(pl.multiple_of(cb * LANE, LANE), LANE)
            _fill_shifted(sh_scr, c0_scr, cs)
            for r0 in range(0, te, CONV_RB):
                acc = None
                for j in range(KD):
                    term = w_ref[j:j + 1, cs] * _shifted(sh_scr, c0_scr, cs, off + j, r0, CONV_RB)
                    acc = term if acc is None else acc + term
                c1_ref[r0:r0 + CONV_RB, cs] = acc + wb_ref[:, cs]
            return carry

        lax.fori_loop(0, D // LANE, lane_block, 0)
        c1 = c1_ref[...]
        mu = jnp.mean(c1, axis=-1, keepdims=True)
        xc = c1 - mu
        c2 = xc * lax.rsqrt(jnp.mean(xc * xc, axis=-1, keepdims=True) + EPS) * lw_ref[...] + lb_ref[...]
        zb = zb_ref[...]
        c3 = (c2 * _sig(c2) * zb * _sig(zb)).astype(BF16)
        c3_ref[...] = c3
        yb_ref[...] = _dot(c3, wcf_ref[...])

    vec = pl.BlockSpec((1, D), lambda i: (0, 0))
    row = pl.BlockSpec((te, D), lambda i: (i, 0))
    return _call(
        body, name="conv_b_fwd", grid=(lp // te,),
        in_specs=[pl.BlockSpec((te, D), lambda i: (i, CB_GA_)), pl.BlockSpec((te, D), lambda i: (i, CB_GB_)),
                  pl.BlockSpec((HALO_D, D), lambda i: (jnp.maximum(i * hb - 1, 0), CB_GA_)),
                  pl.BlockSpec((HALO_D, D), lambda i: (jnp.maximum(i * hb - 1, 0), CB_GB_)),
                  pl.BlockSpec((te, D), lambda i: (i, CB_ZB)),
                  pl.BlockSpec((KD, D), lambda i: (0, 0)), vec, vec, vec, pl.BlockSpec((D, D), lambda i: (0, 0))],
        out_specs=(row, row, row),
        out_shape=(jax.ShapeDtypeStruct((lp, D), F32), jax.ShapeDtypeStruct((lp, D), BF16),
                   jax.ShapeDtypeStruct((lp, D), F32)),
        scratch_shapes=[pltpu.VMEM((te + HALO_D, D), F32), pltpu.VMEM((SUBLANES, te + HALO_D - SUBLANES, LANE), F32)],
        compiler_params=_cp(("parallel",)),
    )(proj, proj, proj, proj, proj, dw_w, dw_b, ln_w, ln_b, w_cf)


def _conv_b_bwd1(dy_b, w_cf, c1, proj, ln_w, ln_b, dproj):
    lp = c1.shape[0]
    te = _pick(lp, (320,))

    def body(dyb_ref, wcf_ref, c1_ref, zb_ref, lw_ref, lb_ref, _, dc1_ref, dzb_ref, sums_ref):
        @pl.when(pl.program_id(0) == 0)
        def _():
            sums_ref[...] = jnp.zeros_like(sums_ref)

        c1, g = c1_ref[...], _dot_nt(dyb_ref[...], wcf_ref[...])
        mu = jnp.mean(c1, axis=-1, keepdims=True)
        xc = c1 - mu
        rstd = lax.rsqrt(jnp.mean(xc * xc, axis=-1, keepdims=True) + EPS)
        xh = xc * rstd
        lw = lw_ref[...]
        c2 = xh * lw + lb_ref[...]
        s2 = _sig(c2)
        zb = zb_ref[...]
        sz = _sig(zb)
        dc2 = g * (zb * sz) * _dsilu(c2, s2)
        dzb_ref[...] = (g * (c2 * s2) * _dsilu(zb, sz)).astype(BF16)
        dxh = dc2 * lw
        dc1 = rstd * (dxh - jnp.mean(dxh, axis=-1, keepdims=True) - xh * jnp.mean(dxh * xh, axis=-1, keepdims=True))
        dc1_ref[...] = dc1
        sums_ref[...] += jnp.concatenate([_colsum(dc2 * xh), _colsum(dc2), _colsum(dc1)], axis=0)

    vec = pl.BlockSpec((1, D), lambda i: (0, 0))
    return _call(
        body, name="conv_b_bwd1", grid=(lp // te,),
        in_specs=[pl.BlockSpec((te, D), lambda i: (i, 0)), pl.BlockSpec((D, D), lambda i: (0, 0)),
                  pl.BlockSpec((te, D), lambda i: (i, 0)),
                  pl.BlockSpec((te, D), lambda i: (i, CB_ZB)), vec, vec, pl.BlockSpec(memory_space=pl.ANY)],
        out_specs=(pl.BlockSpec((te, D), lambda i: (i, 0)), pl.BlockSpec((te, D), lambda i: (i, CB_ZB)),
                   pl.BlockSpec((3, D), lambda i: (0, 0))),
        out_shape=(jax.ShapeDtypeStruct((lp, D), F32), jax.ShapeDtypeStruct(dproj.shape, dproj.dtype),
                   jax.ShapeDtypeStruct((3, D), F32)),
        input_output_aliases={6: 1},
        compiler_params=_cp(("arbitrary",)),
    )(dy_b, w_cf, c1, proj, ln_w, ln_b, dproj)


def _conv_b_bwd2(dc1, proj, dw_w, dproj):
    lp = dc1.shape[0]
    te = _pick(lp, (320,))
    hb = te // HALO_D
    nt = lp // te
    last_hb = lp // HALO_D - 1

    def body(g_ref, gn_ref, a_ref, b_ref, ha_ref, hb_ref, w_ref, _, dab_ref, dw_ref, c0_scr, g_scr, dc0_scr,
             csh_scr, gsh_scr):
        i = pl.program_id(0)

        @pl.when(i == 0)
        def _():
            dw_ref[...] = jnp.zeros_like(dw_ref)

        a, b = a_ref[...], b_ref[...]
        sb = _sig(b)
        c0_scr[:HALO_D, :] = jnp.where(i > 0, ha_ref[...] * _sig(hb_ref[...]), 0.0)
        c0_scr[HALO_D:, :] = a * sb
        g_scr[:te, :] = g_ref[...]
        g_scr[te:, :] = jnp.where(i < nt - 1, gn_ref[...], 0.0)
        off = HALO_D - (KD - 1)
        def lane_block(cb, carry):
            cs = pl.ds(pl.multiple_of(cb * LANE, LANE), LANE)
            _fill_shifted(csh_scr, c0_scr, cs)
            _fill_shifted(gsh_scr, g_scr, cs)
            for r0 in range(0, te, CONV_RB):
                acc = None
                for j in range(KD):
                    term = w_ref[j:j + 1, cs] * _shifted(gsh_scr, g_scr, cs, KD - 1 - j, r0, CONV_RB)
                    acc = term if acc is None else acc + term
                dc0_scr[r0:r0 + CONV_RB, cs] = acc
            parts = [None] * KD
            for r0 in range(0, te, CONV_RB):
                g = g_scr[r0:r0 + CONV_RB, cs].reshape(CONV_RB // SUBLANES, SUBLANES, LANE)
                for j in range(KD):
                    x = _shifted(csh_scr, c0_scr, cs, off + j, r0, CONV_RB)
                    p = jnp.sum(g * x.reshape(CONV_RB // SUBLANES, SUBLANES, LANE), axis=0)
                    parts[j] = p if parts[j] is None else parts[j] + p
            dw_ref[:, cs] += jnp.concatenate([_colsum(p) for p in parts], axis=0)
            return carry

        lax.fori_loop(0, D // LANE, lane_block, 0)
        dc0 = dc0_scr[...]
        dab_ref[:, :D] = (dc0 * sb).astype(BF16)
        dab_ref[:, D:] = (dc0 * a * sb * (1.0 - sb)).astype(BF16)

    return _call(
        body, name="conv_b_bwd2", grid=(nt,),
        in_specs=[pl.BlockSpec((te, D), lambda i: (i, 0)),
                  pl.BlockSpec((HALO_D, D), lambda i: (jnp.minimum((i + 1) * hb, last_hb), 0)),
                  pl.BlockSpec((te, D), lambda i: (i, CB_GA_)), pl.BlockSpec((te, D), lambda i: (i, CB_GB_)),
                  pl.BlockSpec((HALO_D, D), lambda i: (jnp.maximum(i * hb - 1, 0), CB_GA_)),
                  pl.BlockSpec((HALO_D, D), lambda i: (jnp.maximum(i * hb - 1, 0), CB_GB_)),
                  pl.BlockSpec((KD, D), lambda i: (0, 0)), pl.BlockSpec(memory_space=pl.ANY)],
        out_specs=(pl.BlockSpec((te, 2 * D), lambda i: (i, CB_GA_ // 2)), pl.BlockSpec((KD, D), lambda i: (0, 0))),
        out_shape=(jax.ShapeDtypeStruct(dproj.shape, dproj.dtype), jax.ShapeDtypeStruct((KD, D), F32)),
        input_output_aliases={7: 0},
        scratch_shapes=[pltpu.VMEM((te + HALO_D, D), F32), pltpu.VMEM((te + HALO_D, D), F32), pltpu.VMEM((te, D), F32),
                        pltpu.VMEM((SUBLANES, te + HALO_D - SUBLANES, LANE), F32),
                        pltpu.VMEM((SUBLANES, te + HALO_D - SUBLANES, LANE), F32)],
        compiler_params=_cp(("arbitrary",)),
    )(dc1, dc1, proj, proj, proj, proj, dw_w, dproj)


def _merge_fwd(y_a, y_b, proj, b_cf, w_o):
    lp = y_a.shape[0]
    te = _pick(lp, (320,))

    def body(ya_ref, yb_ref, ga_ref, gb_ref, bias_ref, wo_ref, out_ref, z_ref):
        merged = (_sig(ga_ref[...]) * ya_ref[...] + _sig(gb_ref[...]) * (yb_ref[...] + bias_ref[...])).astype(BF16)
        out_ref[...] = merged
        z_ref[...] = _dot(merged, wo_ref[...])

    row = lambda j: pl.BlockSpec((te, D), lambda i: (i, j))
    return _call(
        body, name="merge_fwd", grid=(lp // te,),
        in_specs=[row(0), row(0), row(CB_MA), row(CB_MB), pl.BlockSpec((1, D), lambda i: (0, 0)),
                  pl.BlockSpec((D, D), lambda i: (0, 0))],
        out_specs=(row(0), row(0)),
        out_shape=(jax.ShapeDtypeStruct((lp, D), BF16), jax.ShapeDtypeStruct((lp, D), F32)),
        compiler_params=_cp(("parallel",)),
    )(y_a, y_b, proj, proj, b_cf, w_o)


def _merge_bwd(dx_out_b, w_o, y_a, y_b, proj, b_cf):
    lp = y_a.shape[0]
    te = _pick(lp, (320,))

    def body(dx_ref, wo_ref, ya_ref, yb_ref, ga_ref, gb_ref, bias_ref, dya_ref, dyb_ref, dg_ref, db_ref):
        @pl.when(pl.program_id(0) == 0)
        def _():
            db_ref[...] = jnp.zeros_like(db_ref)

        dm = _dot_nt(dx_ref[...], wo_ref[...])
        sa, sb = _sig(ga_ref[...]), _sig(gb_ref[...])
        dyb = sb * dm
        dya_ref[...] = (sa * dm).astype(BF16)
        dyb_ref[...] = dyb.astype(BF16)
        dg_ref[:, :D] = (dm * ya_ref[...] * sa * (1.0 - sa)).astype(BF16)
        dg_ref[:, D:] = (dm * (yb_ref[...] + bias_ref[...]) * sb * (1.0 - sb)).astype(BF16)
        db_ref[...] += _colsum(dyb)

    row = lambda j: pl.BlockSpec((te, D), lambda i: (i, j))
    act = jax.ShapeDtypeStruct((lp, D), BF16)
    return _call(
        body, name="merge_bwd", grid=(lp // te,),
        in_specs=[row(0), pl.BlockSpec((D, D), lambda i: (0, 0)), row(0), row(0), row(CB_MA), row(CB_MB),
                  pl.BlockSpec((1, D), lambda i: (0, 0))],
        out_specs=(row(0), row(0), pl.BlockSpec((te, 2 * D), lambda i: (i, CB_MA // 2)),
                   pl.BlockSpec((1, D), lambda i: (0, 0))),
        out_shape=(act, act, jax.ShapeDtypeStruct((lp, NCB * D), BF16), jax.ShapeDtypeStruct((1, D), F32)),
        compiler_params=_cp(("arbitrary",)),
    )(dx_out_b, w_o, y_a, y_b, proj, proj, b_cf)


def _final_fwd_bwd(x_ext, z, target, final_w):
    lp = x_ext.shape[0]
    te = LANE

    def body(x_ref, z_ref, t_ref, w_ref, dx_ref, dxb_ref, loss_ref, dw_ref):
        i = pl.program_id(0)

        @pl.when(i == 0)
        def _():
            loss_ref[...] = jnp.zeros_like(loss_ref)
            dw_ref[...] = jnp.zeros_like(dw_ref)

        xo = x_ref[...] + z_ref[...]
        r = lax.rsqrt(jnp.mean(xo * xo, axis=-1, keepdims=True) + EPS)
        xhat = xo * r
        w = w_ref[...]
        err = jnp.where(i > 0, xhat * w - t_ref[...], 0.0)
        loss_ref[...] += 0.5 * jnp.sum(jnp.mean(err * err, axis=-1, keepdims=True), keepdims=True)
        dy = err * (1.0 / D)
        dw_ref[...] += _colsum(dy * xhat)
        dxn = dy * w
        dx = r * (dxn - xhat * jnp.mean(dxn * xhat, axis=-1, keepdims=True))
        dx_ref[...] = dx
        dxb_ref[...] = dx.astype(BF16)

    return _call(
        body, name="final_fwd_bwd", grid=(lp // te,),
        in_specs=[pl.BlockSpec((te, D), lambda i: (i, 0)), pl.BlockSpec((te, D), lambda i: (i, 0)),
                  pl.BlockSpec((te, D), lambda i: (jnp.maximum(i - 1, 0), 0)), pl.BlockSpec((1, D), lambda i: (0, 0))],
        out_specs=(pl.BlockSpec((te, D), lambda i: (i, 0)), pl.BlockSpec((te, D), lambda i: (i, 0)),
                   pl.BlockSpec((1, 1), lambda i: (0, 0)), pl.BlockSpec((1, D), lambda i: (0, 0))),
        out_shape=(jax.ShapeDtypeStruct((lp, D), F32), jax.ShapeDtypeStruct((lp, D), BF16),
                   jax.ShapeDtypeStruct((1, 1), F32), jax.ShapeDtypeStruct((1, D), F32)),
        compiler_params=_cp(("arbitrary",)),
    )(x_ext, z, target, final_w)


def _prenorm_bwd(dh, x_ext, dx_out, norm_w, seq):
    lp = x_ext.shape[0]
    te = LANE

    def body(dh_ref, x_ref, dxo_ref, w_ref, gx_ref, head_ref, dw_ref):
        i = pl.program_id(0)

        @pl.when(i == 0)
        def _():
            dw_ref[...] = jnp.zeros_like(dw_ref)

        x, dh = x_ref[...], dh_ref[...]
        r = lax.rsqrt(jnp.mean(x * x, axis=-1, keepdims=True) + EPS)
        xhat = x * r
        dxn = dh * w_ref[...]
        dx = dxo_ref[...] + r * (dxn - xhat * jnp.mean(dxn * xhat, axis=-1, keepdims=True))
        dw_ref[...] += _colsum(dh * xhat)

        @pl.when(i == 0)
        def _():
            head_ref[...] = dx

        @pl.when(i > 0)
        def _():
            gx_ref[...] = dx

    row = pl.BlockSpec((te, D), lambda i: (i, 0))
    return _call(
        body, name="prenorm_bwd", grid=(lp // te,),
        in_specs=[row, row, row, pl.BlockSpec((1, D), lambda i: (0, 0))],
        out_specs=(pl.BlockSpec((te, D), lambda i: (jnp.maximum(i - 1, 0), 0)), pl.BlockSpec((te, D), lambda i: (0, 0)),
                   pl.BlockSpec((1, D), lambda i: (0, 0))),
        out_shape=(jax.ShapeDtypeStruct((seq, D), F32), jax.ShapeDtypeStruct((te, D), F32),
                   jax.ShapeDtypeStruct((1, D), F32)),
        compiler_params=_cp(("arbitrary",)),
    )(dh, x_ext, dx_out, norm_w)


def _adam_reduce(parts, w, m, v, name):
    r, n = w.shape
    tr = _pick(r, (128,)) if r % 128 == 0 else r

    def body(p_ref, w_ref, m_ref, v_ref, g_ref, d_ref, m2_ref, v2_ref):
        g = p_ref[0]
        for s in range(1, NDEV):
            g = g + p_ref[s]
        _adam_write(g, w_ref, m_ref, v_ref, g_ref, d_ref, m2_ref, v2_ref)

    blk = pl.BlockSpec((tr, n), lambda i: (i, 0))
    out = jax.ShapeDtypeStruct((r, n), F32)
    return _call(
        body, name=name, grid=(r // tr,),
        in_specs=[pl.BlockSpec((NDEV, tr, n), lambda i: (0, i, 0)), blk, blk, blk],
        out_specs=(blk, blk, blk, blk), out_shape=(out, out, out, out),
        compiler_params=_cp(("parallel",)),
    )(parts, w, m, v)


def _adam_write(g, w_ref, m_ref, v_ref, g_ref, d_ref, m2_ref, v2_ref):
    c1 = 1.0 - ADAM_B1 ** ADAM_STEP
    c2 = 1.0 - ADAM_B2 ** ADAM_STEP
    m2 = ADAM_B1 * m_ref[...] + (1.0 - ADAM_B1) * g
    v2 = ADAM_B2 * v_ref[...] + (1.0 - ADAM_B2) * (g * g)
    g_ref[...] = g
    m2_ref[...] = m2
    v2_ref[...] = v2
    d_ref[...] = -ADAM_LR * ((m2 / c1) / (jnp.sqrt(v2 / c2) + ADAM_EPS) + ADAM_WD * w_ref[...])


def _adam_chips(own, recv, w, m, v, name):
    r, n = w.shape
    tr = _pick(r, (128,))

    def body(own_ref, p_ref, w_ref, m_ref, v_ref, g_ref, d_ref, m2_ref, v2_ref):
        my_chip = 2 * lax.axis_index("x") + lax.axis_index("y")
        g = None
        for j in range(NCHIP):
            part = jnp.where(my_chip == j, own_ref[...], p_ref[j].astype(F32))
            g = part if g is None else g + part
        _adam_write(g, w_ref, m_ref, v_ref, g_ref, d_ref, m2_ref, v2_ref)

    blk = pl.BlockSpec((tr, n), lambda i: (i, 0))
    out = jax.ShapeDtypeStruct((r, n), F32)
    return _call(
        body, name=name, grid=(r // tr,),
        in_specs=[blk, pl.BlockSpec((NCHIP, tr, n), lambda i: (0, i, 0)), blk, blk, blk],
        out_specs=(blk, blk, blk, blk), out_shape=(out, out, out, out),
        compiler_params=_cp(("parallel",)),
    )(own, recv, w, m, v)


SMALL = ("norm_w", "a_log", "dt_bias", "dn_norm_w", "dw_b", "ln_w", "ln_b", "b_cf_out", "final_norm_w")


def kernel(x, meta, norm_w, w_in, conv_qkv_w, a_log, dt_bias, dn_norm_w, w_dn_out, dw_w, dw_b, ln_w, ln_b, w_cf_out, b_cf_out, w_o, final_norm_w, loss_target, m_meta, m_norm_w, m_w_in, m_conv_qkv_w, m_a_log, m_dt_bias, m_dn_norm_w, m_w_dn_out, m_dw_w, m_dw_b, m_ln_w, m_ln_b, m_w_cf_out, m_b_cf_out, m_w_o, m_final_norm_w, v_meta, v_norm_w, v_w_in, v_conv_qkv_w, v_a_log, v_dt_bias, v_dn_norm_w, v_w_dn_out, v_dw_w, v_dw_b, v_ln_w, v_ln_b, v_w_cf_out, v_b_cf_out, v_w_o, v_final_norm_w):
    seq = x.shape[1]
    pad = (-(seq + NMETA)) % LANE
    in_w = w_in.shape[2] * NDEV
    n_qkvz = 4 * D
    n_ba = 2 * H

    w_in_g, w_dn_g, w_cf_g, w_o_g, meta_g, cqw_g, dww_g = _gather_two_level(
        [w_in[0].astype(BF16).T, w_dn_out[0].astype(BF16), w_cf_out[0].astype(BF16), w_o[0].astype(BF16),
         meta, conv_qkv_w[0], dw_w[0]], "gather_weights")
    w_full_t = w_in_g.reshape(in_w, D)
    c_glu = n_qkvz + n_ba
    c_zb, c_mg = c_glu + 2 * D, c_glu + 3 * D
    w_main_t = jnp.concatenate([w_full_t[:n_qkvz], w_full_t[c_glu:c_zb], w_full_t[c_mg:], w_full_t[c_zb:c_mg]],
                               axis=0)
    w_ba_t = jnp.pad(w_full_t[n_qkvz:n_qkvz + n_ba], ((0, LANE - n_ba), (0, 0)))
    w_dn, w_cf, w_oo = (t.reshape(D, D) for t in (w_dn_g, w_cf_g, w_o_g))
    meta_full = jnp.transpose(meta_g, (1, 0, 2)).reshape(NMETA, D)
    cqw = jnp.transpose(cqw_g, (1, 0, 2)).reshape(KQ, 3 * D)
    dww = jnp.transpose(dww_g, (1, 0, 2)).reshape(KD, D)
    ab = jnp.pad(jnp.concatenate([a_log, dt_bias], axis=0), ((0, 0), (H, LANE - 2 * H)))

    x_ext = jnp.concatenate([jnp.zeros((pad, D), F32), meta_full, x[0]], axis=0)

    proj, ba, h = _proj_fwd(x_ext, norm_w, w_main_t, w_ba_t)
    qkv, bg = _qkv_conv_fwd(proj, ba, cqw, ab, pad)
    o, sall, tall = _delta_fwd(qkv, bg)
    o_n, y_a = _o_post_fwd(o, proj, dn_norm_w, w_dn)
    c1, c3, y_b = _conv_b_fwd(proj, dww, dw_b, ln_w, ln_b, w_cf)
    merged, z = _merge_fwd(y_a, y_b, proj, b_cf_out, w_oo)
    dx_out, dx_out_b, loss_part, g_final_w = _final_fwd_bwd(x_ext, z, loss_target[0], final_norm_w.reshape(1, D))

    g_w_o = _mm_tn(merged, dx_out_b, "g_w_o_mm")
    dy_a, dy_b, dproj, g_b_cf = _merge_bwd(dx_out_b, w_oo, y_a, y_b, proj, b_cf_out)
    g_w_cf = _mm_tn(c3, dy_b, "g_w_cf_mm")
    g_w_dn = _mm_tn(o_n, dy_a, "g_w_dn_mm")
    dc1, dproj, sums_b = _conv_b_bwd1(dy_b, w_cf, c1, proj, ln_w, ln_b, dproj)
    dproj, g_dw_w = _conv_b_bwd2(dc1, proj, dww, dproj)
    do, dproj, g_dn_w = _o_post_bwd(dy_a, w_dn, o, proj, dn_norm_w, dproj)
    dqkv, dbg = _delta_bwd(qkv, bg, sall, tall, do)
    dproj, g_cqw = _qkv_conv_bwd(proj, dqkv, cqw, dproj)
    dba, dab = _ba_bwd(dbg, ba, ab, pad)
    dh = _dh_mm(dproj, dba, w_main_t, w_ba_t)
    g_w_main = _mm_tn(h, dproj, "g_w_main_mm")
    g_w_ba = _mm_tn(h, dba, "g_w_ba_mm")
    grad_x, dhead, g_norm_w = _prenorm_bwd(dh, x_ext, dx_out, norm_w, seq)

    g_w_full = jnp.concatenate([g_w_main[:, :n_qkvz], g_w_ba[:, :n_ba], g_w_main[:, CB_GA_ * D:CB_MA * D],
                                g_w_main[:, CB_ZB * D:], g_w_main[:, CB_MA * D:CB_ZB * D]], axis=1)
    split_cols = lambda t: jnp.transpose(t.reshape(t.shape[0], NDEV, t.shape[1] // NDEV), (1, 0, 2))
    small = {"norm_w": g_norm_w, "a_log": dab[0:1, H:2 * H], "dt_bias": dab[1:2, H:2 * H], "dn_norm_w": g_dn_w,
             "dw_b": sums_b[2:3], "ln_w": sums_b[0:1], "ln_b": sums_b[1:2], "b_cf_out": g_b_cf,
             "final_norm_w": g_final_w}
    small_vec = jnp.concatenate([small[k] for k in SMALL], axis=1)
    ns = small_vec.shape[1]
    ns_pad = (-ns) % LANE
    small_vec = jnp.pad(small_vec, ((0, 0), (0, ns_pad)))
    big = [split_cols(g_w_full), g_w_dn.reshape(NDEV, D // NDEV, D), g_w_cf.reshape(NDEV, D // NDEV, D),
           g_w_o.reshape(NDEV, D // NDEV, D)]
    from_sibling = _swap_sibling(big, "swap_sibling")
    pairs = [_pair_add(a, g, f"pair_add_{i}") for i, (a, g) in enumerate(zip(big, from_sibling))]
    from_chips = _scatter_chips([p for p, _ in pairs], "scatter_chips")
    p_meta, p_cqw, p_dww, p_small = _exchange(
        [split_cols(dhead[pad:pad + NMETA]), split_cols(g_cqw), split_cols(g_dw_w), small_vec],
        [True] * 3 + [False], "exchange_small")

    res = {}
    res["w_in"] = _adam_chips(pairs[0][1], from_chips[0], w_in[0], m_w_in[0], v_w_in[0], "adam_w_in")
    res["w_dn_out"] = _adam_chips(pairs[1][1], from_chips[1], w_dn_out[0], m_w_dn_out[0], v_w_dn_out[0], "adam_w_dn")
    res["w_cf_out"] = _adam_chips(pairs[2][1], from_chips[2], w_cf_out[0], m_w_cf_out[0], v_w_cf_out[0], "adam_w_cf")
    res["w_o"] = _adam_chips(pairs[3][1], from_chips[3], w_o[0], m_w_o[0], v_w_o[0], "adam_w_o")
    res["meta"] = _adam_reduce(p_meta, meta, m_meta, v_meta, "adam_meta")
    res["conv_qkv_w"] = _adam_reduce(p_cqw, conv_qkv_w[0], m_conv_qkv_w[0], v_conv_qkv_w[0], "adam_conv_qkv_w")
    res["dw_w"] = _adam_reduce(p_dww, dw_w[0], m_dw_w[0], v_dw_w[0], "adam_dw_w")
    loc = dict(norm_w=(norm_w, m_norm_w, v_norm_w), a_log=(a_log, m_a_log, v_a_log), dt_bias=(dt_bias, m_dt_bias, v_dt_bias),
               dn_norm_w=(dn_norm_w, m_dn_norm_w, v_dn_norm_w), dw_b=(dw_b, m_dw_b, v_dw_b), ln_w=(ln_w, m_ln_w, v_ln_w),
               ln_b=(ln_b, m_ln_b, v_ln_b), b_cf_out=(b_cf_out, m_b_cf_out, v_b_cf_out),
               final_norm_w=(final_norm_w, m_final_norm_w, v_final_norm_w))
    cat = lambda j: jnp.pad(jnp.concatenate([loc[k][j].reshape(1, -1) for k in SMALL], axis=1), ((0, 0), (0, ns_pad)))
    small_res = _adam_reduce(p_small, cat(0), cat(1), cat(2), "adam_small")
    off = 0
    for k in SMALL:
        wshape = loc[k][0].shape
        nk = loc[k][0].size
        res[k] = tuple(t[:, off:off + nk].reshape(wshape) for t in small_res)
        off += nk
    shaped = dict(w_in=w_in.shape, w_dn_out=w_dn_out.shape, w_cf_out=w_cf_out.shape, w_o=w_o.shape, meta=meta.shape,
                  conv_qkv_w=conv_qkv_w.shape, dw_w=dw_w.shape)
    for k, shp in shaped.items():
        res[k] = tuple(t.reshape(shp) for t in res[k])

    loss = lax.psum(loss_part[0, 0], ("x", "y", "c"))
    order = ("meta", "norm_w", "w_in", "conv_qkv_w", "a_log", "dt_bias", "dn_norm_w", "w_dn_out", "dw_w", "dw_b", "ln_w",
             "ln_b", "w_cf_out", "b_cf_out", "w_o", "final_norm_w")
    outs = [loss, grad_x[None]]
    for j in range(4):
        outs += [res[k][j] for k in order]
    return tuple(outs)
```

```python
import functools

import jax
import jax.numpy as jnp
from jax import lax
from jax.experimental import pallas as pl
from jax.experimental.pallas import tpu as pltpu

F32 = jnp.float32
BF16 = jnp.bfloat16
HI = lax.Precision.HIGHEST

D = 1024
H = 8
DK = 128
C = 64
NMETA = 16
KQ = 4
KD = 31
HALO_Q = 8
HALO_D = 32
EPS = 1e-6
NDEV = 8
LANE = 128
MIB = 1024 * 1024

ADAM_LR, ADAM_B1, ADAM_B2, ADAM_EPS, ADAM_WD, ADAM_STEP = 0.001, 0.9, 0.999, 1e-08, 0.01, 10

CB_Q, CB_K, CB_V, CB_ZA, CB_GA_, CB_GB_, CB_MA, CB_MB, CB_ZB = range(9)
NCB = 9


def _pick(n, cands):
    for c in cands:
        if n % c == 0:
            return c
    raise ValueError(f"no tile for {n}")


def _cp(sem=None, vmem_mib=40):
    kw = dict(vmem_limit_bytes=vmem_mib * MIB)
    if sem is not None:
        kw["dimension_semantics"] = sem
    return pltpu.CompilerParams(**kw)


def _call(body, **kw):
    return pl.pallas_call(body, **kw)


def _dot(a, b):
    return jnp.dot(a.astype(BF16), b.astype(BF16), preferred_element_type=F32)


def _dot_nt(a, b):
    return lax.dot_general(a.astype(BF16), b.astype(BF16), (((1,), (1,)), ((), ())), preferred_element_type=F32)


def _dot_tn(a, b):
    return lax.dot_general(a.astype(BF16), b.astype(BF16), (((0,), (0,)), ((), ())), preferred_element_type=F32)


def _dot_hi(a, b):
    return jnp.dot(a, b, precision=HI, preferred_element_type=F32)


def _sig(x):
    return 0.5 * jnp.tanh(0.5 * x) + 0.5


def _dsilu(x, s):
    return s * (1.0 + x * (1.0 - s))


def _rowsum(x):
    return jnp.sum(x, axis=-1, keepdims=True)


def _colsum(x):
    return jnp.sum(x, axis=0, keepdims=True)


def _exchange(arrs, scatter, name):
    n = len(arrs)
    out_shape = []
    for a, sc in zip(arrs, scatter):
        shp = a.shape if sc else (NDEV,) + a.shape
        out_shape.append(jax.ShapeDtypeStruct(shp, a.dtype))

    def body(*refs):
        ins, outs = refs[:n], refs[n:2 * n]
        send_sems, recv_sems, loc_sems = refs[2 * n:]
        x, y, c = lax.axis_index("x"), lax.axis_index("y"), lax.axis_index("c")
        me = 4 * x + 2 * y + c
        copies = []
        for a in range(n):
            for k in range(1, NDEV):
                px = 1 - x if (k >> 2) & 1 else x
                py = 1 - y if (k >> 1) & 1 else y
                pc = 1 - c if k & 1 else c
                src = ins[a].at[4 * px + 2 * py + pc] if scatter[a] else ins[a]
                cp = pltpu.make_async_remote_copy(
                    src_ref=src, dst_ref=outs[a].at[me],
                    send_sem=send_sems.at[a * (NDEV - 1) + k - 1], recv_sem=recv_sems.at[a * (NDEV - 1) + k - 1],
                    device_id=(px, py, pc), device_id_type=pl.DeviceIdType.MESH)
                cp.start()
                copies.append(cp)
            loc = pltpu.make_async_copy(ins[a].at[me] if scatter[a] else ins[a], outs[a].at[me], loc_sems.at[a])
            loc.start()
            copies.append(loc)
        for cp in copies:
            cp.wait()

    any_spec = pl.BlockSpec(memory_space=pl.ANY)
    return _call(
        body, name=name, out_shape=tuple(out_shape),
        in_specs=[any_spec] * n, out_specs=tuple([any_spec] * n),
        scratch_shapes=[pltpu.SemaphoreType.DMA((n * (NDEV - 1),)), pltpu.SemaphoreType.DMA((n * (NDEV - 1),)),
                        pltpu.SemaphoreType.DMA((n,))],
    )(*arrs)


NCHIP = 4


def _gather_two_level(arrs, name):
    n = len(arrs)
    per = NDEV - 1

    def body(*refs):
        ins, outs = refs[:n], refs[n:2 * n]
        send_sems, recv_sems, loc_sems = refs[2 * n:]
        x, y, c = lax.axis_index("x"), lax.axis_index("y"), lax.axis_index("c")
        me, sibling = (x, y, c), (x, y, 1 - c)
        chips = [(1 - x, y), (x, 1 - y), (1 - x, 1 - y)]

        def slot(a, px, py, pc):
            return outs[a].at[4 * px + 2 * py + pc]

        def copy(a, k, block, to, src=None):
            return pltpu.make_async_remote_copy(
                src_ref=slot(a, *block) if src is None else src, dst_ref=slot(a, *block),
                send_sem=send_sems.at[a * per + k], recv_sem=recv_sems.at[a * per + k],
                device_id=to, device_id_type=pl.DeviceIdType.MESH)

        local, sent = [], []
        for a in range(n):
            mine = pltpu.make_async_copy(ins[a], slot(a, *me), loc_sems.at[a])
            mine.start()
            local.append(mine)
            first = [copy(a, 1 + j, me, (*chip, c), src=ins[a]) for j, chip in enumerate(chips)]
            first.append(copy(a, 0, me, sibling, src=ins[a]))
            for cp in first:
                cp.start()
            sent += first
        for j, chip in enumerate(chips):
            for a in range(n):
                copy(a, 1 + j, (*chip, c), me).wait_recv()
                cp = copy(a, 4 + j, (*chip, c), sibling)
                cp.start()
                sent.append(cp)
        for a in range(n):
            copy(a, 0, sibling, me).wait_recv()
            for j, chip in enumerate(chips):
                copy(a, 4 + j, (*chip, 1 - c), me).wait_recv()
        for cp in sent:
            cp.wait_send()
        for cp in local:
            cp.wait()

    any_spec = pl.BlockSpec(memory_space=pl.ANY)
    return _call(
        body, name=name, out_shape=tuple(jax.ShapeDtypeStruct((NDEV,) + a.shape, a.dtype) for a in arrs),
        in_specs=[any_spec] * n, out_specs=tuple([any_spec] * n),
        scratch_shapes=[pltpu.SemaphoreType.DMA((n * per,)), pltpu.SemaphoreType.DMA((n * per,)),
                        pltpu.SemaphoreType.DMA((n,))],
    )(*arrs)


def _swap_sibling(arrs, name):
    n = len(arrs)

    def body(*refs):
        ins, outs = refs[:n], refs[n:2 * n]
        send_sems, recv_sems = refs[2 * n:]
        x, y, c = lax.axis_index("x"), lax.axis_index("y"), lax.axis_index("c")
        copies = []
        for a in range(n):
            for j in range(NCHIP):
                cp = pltpu.make_async_remote_copy(
                    src_ref=ins[a].at[2 * j + (1 - c)], dst_ref=outs[a].at[j],
                    send_sem=send_sems.at[a * NCHIP + j], recv_sem=recv_sems.at[a * NCHIP + j],
                    device_id=(x, y, 1 - c), device_id_type=pl.DeviceIdType.MESH)
                cp.start()
                copies.append(cp)
        for cp in copies:
            cp.wait()

    any_spec = pl.BlockSpec(memory_space=pl.ANY)
    return _call(
        body, name=name, out_shape=tuple(jax.ShapeDtypeStruct((NCHIP,) + a.shape[1:], a.dtype) for a in arrs),
        in_specs=[any_spec] * n, out_specs=tuple([any_spec] * n),
        scratch_shapes=[pltpu.SemaphoreType.DMA((n * NCHIP,)), pltpu.SemaphoreType.DMA((n * NCHIP,))],
    )(*arrs)


def _pair_add(arr, got, name):
    _, r, n = arr.shape
    tr, tc = _shard_tile(r, n)
    arr4 = arr.reshape(NCHIP, 2, r, n)

    def body(a_ref, g_ref, p_ref, own_ref):
        c = lax.axis_index("c")
        my_chip = 2 * lax.axis_index("x") + lax.axis_index("y")
        s = jnp.where(c == 0, a_ref[0, 0], a_ref[0, 1]) + g_ref[0]
        p_ref[0] = s.astype(BF16)

        @pl.when(pl.program_id(2) == my_chip)
        def _():
            own_ref[...] = s

    return _call(
        body, name=name, grid=(r // tr, n // tc, NCHIP),
        in_specs=[pl.BlockSpec((1, 2, tr, tc), lambda i, k, j: (j, 0, i, k)),
                  pl.BlockSpec((1, tr, tc), lambda i, k, j: (j, i, k))],
        out_specs=(pl.BlockSpec((1, tr, tc), lambda i, k, j: (j, i, k)), pl.BlockSpec((tr, tc), lambda i, k, j: (i, k))),
        out_shape=(jax.ShapeDtypeStruct((NCHIP, r, n), BF16), jax.ShapeDtypeStruct((r, n), F32)),
        compiler_params=_cp(("parallel", "parallel", "arbitrary")),
    )(arr4, got)


def _shard_tile(r, n):
    return (128, n) if r % 128 == 0 else (r, 256)


def _scatter_chips(arrs, name):
    n = len(arrs)
    per = NCHIP - 1

    def body(*refs):
        ins, outs = refs[:n], refs[n:2 * n]
        send_sems, recv_sems, loc_sems = refs[2 * n:]
        x, y, c = lax.axis_index("x"), lax.axis_index("y"), lax.axis_index("c")
        copies = []
        for a in range(n):
            loc = pltpu.make_async_copy(ins[a].at[2 * x + y], outs[a].at[2 * x + y], loc_sems.at[a])
            loc.start()
            copies.append(loc)
            for k in range(1, NCHIP):
                px = 1 - x if (k >> 1) & 1 else x
                py = 1 - y if k & 1 else y
                cp = pltpu.make_async_remote_copy(
                    src_ref=ins[a].at[2 * px + py], dst_ref=outs[a].at[2 * x + y],
                    send_sem=send_sems.at[a * per + k - 1], recv_sem=recv_sems.at[a * per + k - 1],
                    device_id=(px, py, c), device_id_type=pl.DeviceIdType.MESH)
                cp.start()
                copies.append(cp)
        for cp in copies:
            cp.wait()

    any_spec = pl.BlockSpec(memory_space=pl.ANY)
    return _call(
        body, name=name, out_shape=tuple(jax.ShapeDtypeStruct(a.shape, a.dtype) for a in arrs),
        in_specs=[any_spec] * n, out_specs=tuple([any_spec] * n),
        scratch_shapes=[pltpu.SemaphoreType.DMA((n * per,)), pltpu.SemaphoreType.DMA((n * per,)),
                        pltpu.SemaphoreType.DMA((n,))],
    )(*arrs)


def _mm_tn(a, b, name):
    t, m = a.shape
    n = b.shape[1]
    tt = _pick(t, (1664, 640, 128))
    tm = _pick(m, (1024, 512, 128))
    tn = _pick(n, (1152, 1024, 512, 128))
    nt = t // tt

    def body(a_ref, b_ref, o_ref):
        s = pl.program_id(2)
        part = _dot_tn(a_ref[...], b_ref[...])

        @pl.when(s == 0)
        def _():
            o_ref[...] = part

        @pl.when(s > 0)
        def _():
            o_ref[...] += part

    return _call(
        body, name=name, grid=(m // tm, n // tn, nt),
        in_specs=[pl.BlockSpec((tt, tm), lambda i, j, s: (s, i)), pl.BlockSpec((tt, tn), lambda i, j, s: (s, j))],
        out_specs=pl.BlockSpec((tm, tn), lambda i, j, s: (i, j)),
        out_shape=jax.ShapeDtypeStruct((m, n), F32),
        compiler_params=_cp(("parallel", "parallel", "arbitrary")),
    )(a, b)


def _proj_fwd(x_ext, norm_w, w_main_t, w_ba_t):
    lp = x_ext.shape[0]
    n = w_main_t.shape[0]
    tm = _pick(lp, (832, 640, 320))
    tn = 1024

    def body(x_ref, nw_ref, w_ref, wba_ref, proj_ref, ba_ref, h_ref):
        @pl.when(pl.program_id(1) == 0)
        def _():
            x = x_ref[...]
            r = lax.rsqrt(jnp.mean(x * x, axis=-1, keepdims=True) + EPS)
            h = (x * r * nw_ref[...]).astype(BF16)
            h_ref[...] = h
            ba_ref[...] = _dot_nt(h, wba_ref[...])

        proj_ref[...] = _dot_nt(h_ref[...], w_ref[...])

    return _call(
        body, name="proj_fwd", grid=(lp // tm, n // tn),
        in_specs=[pl.BlockSpec((tm, D), lambda i, j: (i, 0)), pl.BlockSpec((1, D), lambda i, j: (0, 0)),
                  pl.BlockSpec((tn, D), lambda i, j: (j, 0)), pl.BlockSpec((LANE, D), lambda i, j: (0, 0))],
        out_specs=(pl.BlockSpec((tm, tn), lambda i, j: (i, j)), pl.BlockSpec((tm, LANE), lambda i, j: (i, 0)),
                   pl.BlockSpec((tm, D), lambda i, j: (i, 0))),
        out_shape=(jax.ShapeDtypeStruct((lp, n), F32), jax.ShapeDtypeStruct((lp, LANE), F32),
                   jax.ShapeDtypeStruct((lp, D), BF16)),
        compiler_params=_cp(("parallel", "arbitrary")),
    )(x_ext, norm_w, w_main_t, w_ba_t)


def _dh_mm(dproj, dba, w_main_t, w_ba_t):
    lp, n = dproj.shape
    tm = _pick(lp, (832, 640, 320))
    tn = 1024
    tk = 2304
    nk = n // tk

    def body(a_ref, ba_ref, b_ref, bba_ref, o_ref, acc):
        kk = pl.program_id(2)

        @pl.when(kk == 0)
        def _():
            acc[...] = jnp.dot(ba_ref[...], bba_ref[...], preferred_element_type=F32)

        acc[...] += jnp.dot(a_ref[...], b_ref[...], preferred_element_type=F32)

        @pl.when(kk == nk - 1)
        def _():
            o_ref[...] = acc[...]

    return _call(
        body, name="dh_mm", grid=(lp // tm, D // tn, nk),
        in_specs=[pl.BlockSpec((tm, tk), lambda i, j, kk: (i, kk)), pl.BlockSpec((tm, LANE), lambda i, j, kk: (i, 0)),
                  pl.BlockSpec((tk, tn), lambda i, j, kk: (kk, j)), pl.BlockSpec((LANE, tn), lambda i, j, kk: (0, j))],
        out_specs=pl.BlockSpec((tm, tn), lambda i, j, kk: (i, j)),
        out_shape=jax.ShapeDtypeStruct((lp, D), F32),
        scratch_shapes=[pltpu.VMEM((tm, tn), F32)],
        compiler_params=_cp(("parallel", "parallel", "arbitrary")),
    )(dproj, dba, w_main_t, w_ba_t)


def _beta_g(ba, ab, row0, pad):
    lane = lax.broadcasted_iota(jnp.int32, ba.shape, 1)
    rows = row0 + lax.broadcasted_iota(jnp.int32, ba.shape, 0)
    z = ba + ab[1:2, :]
    sp = jnp.maximum(z, 0.0) + jnp.log(1.0 + jnp.exp(-jnp.abs(z)))
    val = jnp.where(lane < H, _sig(ba), -jnp.exp(ab[0:1, :]) * sp)
    return jnp.where((lane < 2 * H) & (rows >= pad), val, 0.0)


def _qkv_conv_fwd(proj, ba, conv_w, ab, pad):
    lp = proj.shape[0]
    te = _pick(lp, (320,))
    hb = te // HALO_Q

    def body(main_ref, halo_ref, cw_ref, ba_ref, ab_ref, out_ref, bg_ref, pre_scr, tap_scr):
        i, s = pl.program_id(0), pl.program_id(1)
        pre_scr[:HALO_Q, :] = jnp.where(i > 0, halo_ref[...], 0.0)
        pre_scr[HALO_Q:, :] = main_ref[...]
        scale = jnp.where(s == 0, DK ** -0.5, 1.0)
        off = HALO_Q - (KQ - 1)

        def head(h, carry):
            cs = pl.ds(pl.multiple_of(h * DK, DK), DK)
            for j in range(KQ - 1):
                tap_scr[j] = pre_scr[off + j:off + j + te, cs]
            co = cw_ref[KQ - 1:KQ, cs] * pre_scr[HALO_Q:, cs]
            for j in range(KQ - 1):
                co = co + cw_ref[j:j + 1, cs] * tap_scr[j]
            a = co * _sig(co)
            r = lax.rsqrt(_rowsum(a * a) + EPS)
            out_ref[:, cs] = jnp.where(s == 2, a, a * (r * scale))
            return carry

        lax.fori_loop(0, H, head, 0, unroll=True)

        @pl.when(s == 0)
        def _():
            bg_ref[...] = _beta_g(ba_ref[...], ab_ref[...], i * te, pad)

    return _call(
        body, name="qkv_conv_fwd", grid=(lp // te, 3),
        in_specs=[pl.BlockSpec((te, D), lambda i, s: (i, s)),
                  pl.BlockSpec((HALO_Q, D), lambda i, s: (jnp.maximum(i * hb - 1, 0), s)),
                  pl.BlockSpec((KQ, D), lambda i, s: (0, s)),
                  pl.BlockSpec((te, LANE), lambda i, s: (i, 0)),
                  pl.BlockSpec((2, LANE), lambda i, s: (0, 0))],
        out_specs=(pl.BlockSpec((te, D), lambda i, s: (i, s)), pl.BlockSpec((te, LANE), lambda i, s: (i, 0))),
        out_shape=(jax.ShapeDtypeStruct((lp, 3 * D), F32), jax.ShapeDtypeStruct((lp, LANE), F32)),
        scratch_shapes=[pltpu.VMEM((te + HALO_Q, D), F32), pltpu.VMEM((KQ - 1, te, DK), F32)],
        compiler_params=_cp(("parallel", "arbitrary")),
    )(proj, proj, conv_w, ba, ab)


def _tri_masks():
    row = lax.broadcasted_iota(jnp.int32, (C, C), 0)
    col = lax.broadcasted_iota(jnp.int32, (C, C), 1)
    return row, col


def _split(a):
    hi = a.astype(BF16)
    return hi, (a - hi.astype(F32)).astype(BF16)


def _dot3(a, b, dims=(((1,), (0,)), ((), ()))):
    (ah, al), (bh, bl) = a, b
    mm = lambda x, y: lax.dot_general(x, y, dims, preferred_element_type=F32)
    return mm(ah, bh) + (mm(ah, bl) + mm(al, bh))


TINV_BLOCK = 16


def _tinv(ns, row, col):
    eye = (row == col).astype(F32)
    sh = TINV_BLOCK.bit_length() - 1
    same16 = (row >> sh) == (col >> sh)
    same32 = (row >> (sh + 1)) == (col >> (sh + 1))
    ys = [jnp.where(same16, -n, 0.0) for n in ns]
    ts = [eye + y for y in ys]
    sp = [_split(y) for y in ys]
    for level in range(3):
        yks = [_dot3(s, s) for s in sp]
        sp = [_split(yk) for yk in yks]
        ts = [t + _dot3(s, _split(t)) for s, t in zip(sp, ts)]
    for mask in (same32 & ~same16, ~same32):
        tsp = [_split(t) for t in ts]
        inner = [_dot3(_split(jnp.where(mask, n, 0.0)), t) for n, t in zip(ns, tsp)]
        ts = [t - _dot3(tp, _split(a)) for t, tp, a in zip(ts, tsp, inner)]
    return ts


def _chunk_common(q, k, v, bcol, gcc, gcr, incl, strict):
    dm = jnp.where(incl, jnp.exp(gcc - gcr), 0.0)
    kk = _dot_nt(k, k)
    qk = _dot_nt(q, k)
    egc = jnp.exp(gcc)
    glast = gcc[C - 1:C, :]
    eend = jnp.exp(glast - gcc)
    elast = jnp.exp(glast)
    rhs = jnp.concatenate([v * bcol, k * (bcol * egc)], axis=1)
    return dm, kk, qk, egc, eend, elast, rhs


def _delta_fwd(qkv, bg):
    lp = qkv.shape[0]
    nc = lp // C

    def body(q_ref, k_ref, v_ref, bg_ref, o_ref, sall_ref, tall_ref, s_scr):
        @pl.when(pl.program_id(0) == 0)
        def _():
            s_scr[...] = jnp.zeros_like(s_scr)

        bgt = bg_ref[...]
        row, col = _tri_masks()
        incl, strict = row >= col, row > col
        gc_all = _dot_hi(incl.astype(F32), bgt)
        gc_t = _dot_hi(bgt.T, (row <= col).astype(F32))
        heads = range(H)
        sls = [slice(h * DK, (h + 1) * DK) for h in heads]
        qs, ks, vs = ([r[:, sl] for sl in sls] for r in (q_ref, k_ref, v_ref))
        bcols = [bgt[:, h:h + 1] for h in heads]
        cm = [_chunk_common(qs[h], ks[h], vs[h], bcols[h], gc_all[:, H + h:H + h + 1], gc_t[H + h:H + h + 1, :],
                            incl, strict) for h in heads]
        dms, kks, qks, egcs, eends, elasts, rhss = zip(*cm)
        ts = _tinv([jnp.where(strict, bcols[h] * kks[h] * dms[h], 0.0) for h in heads], row, col)
        sols = [_dot3(_split(ts[h]), _split(rhss[h])) for h in heads]
        ss = [s_scr[h] for h in heads]
        sb = [s.astype(BF16) for s in ss]
        wvs = [sols[h][:, :DK] - _dot(sols[h][:, DK:], sb[h]) for h in heads]
        wvb = [wv.astype(BF16) for wv in wvs]
        for h in heads:
            o_ref[:, sls[h]] = _dot(qs[h] * egcs[h], sb[h]) + _dot(qks[h] * dms[h], wvb[h])
            sall_ref[0, h] = ss[h]
            tall_ref[0, h] = ts[h]
        for h in heads:
            s_scr[h] = ss[h] * elasts[h] + _dot_tn(ks[h] * eends[h], wvb[h])

    blk = lambda j: pl.BlockSpec((C, D), lambda n: (n, j))
    return _call(
        body, name="delta_fwd", grid=(nc,),
        in_specs=[blk(0), blk(1), blk(2), pl.BlockSpec((C, LANE), lambda n: (n, 0))],
        out_specs=(pl.BlockSpec((C, D), lambda n: (n, 0)),
                   pl.BlockSpec((1, H, DK, DK), lambda n: (n, 0, 0, 0)),
                   pl.BlockSpec((1, H, C, C), lambda n: (n, 0, 0, 0))),
        out_shape=(jax.ShapeDtypeStruct((lp, D), F32), jax.ShapeDtypeStruct((nc, H, DK, DK), F32),
                   jax.ShapeDtypeStruct((nc, H, C, C), F32)),
        scratch_shapes=[pltpu.VMEM((H, DK, DK), F32)],
        compiler_params=_cp(("arbitrary",)),
    )(qkv, qkv, qkv, bg)


def _delta_bwd(qkv, bg, sall, tall, do):
    lp = qkv.shape[0]
    nc = lp // C

    def body(q_ref, k_ref, v_ref, bg_ref, sall_ref, tall_ref, do_ref, dqkv_ref, dbg_ref, ds_scr):
        @pl.when(pl.program_id(0) == 0)
        def _():
            ds_scr[...] = jnp.zeros_like(ds_scr)

        bgt = bg_ref[...]
        row, col = _tri_masks()
        incl, strict = row >= col, row > col
        upper = (row <= col).astype(F32)
        gc_all = _dot_hi(incl.astype(F32), bgt)
        gc_t = _dot_hi(bgt.T, upper)
        lane = lax.broadcasted_iota(jnp.int32, (C, LANE), 1)
        lastrow = lax.broadcasted_iota(jnp.int32, (C, 1), 0) == C - 1
        heads = range(H)
        sls = [slice(h * DK, (h + 1) * DK) for h in heads]
        qs, ks, vs, dos = ([r[:, sl] for sl in sls] for r in (q_ref, k_ref, v_ref, do_ref))
        bcols = [bgt[:, h:h + 1] for h in heads]
        cm = [_chunk_common(qs[h], ks[h], vs[h], bcols[h], gc_all[:, H + h:H + h + 1], gc_t[H + h:H + h + 1, :],
                            incl, strict) for h in heads]
        dms, kks, qks, egcs, eends, elasts, rhss = zip(*cm)
        ss = [sall_ref[0, h] for h in heads]
        ts = [tall_ref[0, h] for h in heads]
        dsns = [ds_scr[h] for h in heads]
        sb = [s.astype(BF16) for s in ss]
        dsb = [d.astype(BF16) for d in dsns]
        dob = [d.astype(BF16) for d in dos]
        sols = [_dot3(_split(ts[h]), _split(rhss[h])) for h in heads]
        ws = [sol[:, DK:] for sol in sols]
        qgs = [qs[h] * egcs[h] for h in heads]
        kends = [ks[h] * eends[h] for h in heads]
        wvs = [sols[h][:, :DK] - _dot(ws[h], sb[h]) for h in heads]
        wvb = [wv.astype(BF16) for wv in wvs]
        dwvs = [_dot_tn(qks[h] * dms[h], dob[h]) + _dot(kends[h], dsb[h]) for h in heads]
        dps = [jnp.where(incl, _dot_nt(dob[h], wvb[h]), 0.0) for h in heads]
        dqgs = [_dot_nt(dob[h], sb[h]) for h in heads]
        dkends = [_dot_nt(wvb[h], dsb[h]) for h in heads]
        for h in heads:
            ds_scr[h] = _dot_tn(qgs[h], dob[h]) + elasts[h] * dsns[h] - _dot_tn(ws[h], dwvs[h])
        dglasts = [elasts[h] * jnp.sum(ss[h] * dsns[h], keepdims=True) for h in heads]
        dws = [-_dot_nt(dwvs[h], sb[h]) for h in heads]
        tts = [_split(ts[h].T) for h in heads]
        drhss = [_dot3(tts[h], _split(jnp.concatenate([dwvs[h], dws[h]], axis=1))) for h in heads]
        nt_dims = (((1,), (1,)), ((), ()))
        dns = [jnp.where(strict, -_dot3(_split(drhss[h]), _split(sols[h]), nt_dims), 0.0) for h in heads]
        dbeta_t = jnp.zeros((C, LANE), F32)
        dgc_t = jnp.zeros((C, LANE), F32)
        for h in heads:
            q, k, v, bcol, dm, kk, qk, egc, eend = qs[h], ks[h], vs[h], bcols[h], dms[h], kks[h], qks[h], egcs[h], eends[h]
            drv, drk = drhss[h][:, :DK], drhss[h][:, DK:]
            dn, dp, dqg, dkend = dns[h], dps[h], dqgs[h], dkends[h]
            rk = _rowsum(drk * k)
            dkk = dn * (bcol * dm)
            dqk = dp * dm
            e = (dn * (bcol * kk) + dp * qk) * dm
            tk = _rowsum(dkend * kends[h])
            dgc = rk * bcol * egc + _rowsum(e) - _rowsum(e.T) + _rowsum(dqg * qgs[h]) - tk
            dgc = dgc + jnp.where(lastrow, dglasts[h] + jnp.sum(tk, keepdims=True), 0.0)
            dbeta = _rowsum(drv * v) + rk * egc + _rowsum(dn * kk * dm)
            dqkv_ref[:, sls[h]] = _dot(dqk, k) + dqg * egc
            dqkv_ref[:, D + h * DK:D + (h + 1) * DK] = (drk * (bcol * egc) + _dot(dkk, k) + _dot_tn(dkk, k)
                                                       + _dot_tn(dqk, q) + dkend * eend)
            dqkv_ref[:, 2 * D + h * DK:2 * D + (h + 1) * DK] = bcol * drv
            dbeta_t = jnp.where(lane == h, dbeta, dbeta_t)
            dgc_t = jnp.where(lane == H + h, dgc, dgc_t)
        dbg_ref[...] = dbeta_t + _dot_hi(upper, dgc_t)

    rev = lambda n: nc - 1 - n
    blk = lambda j: pl.BlockSpec((C, D), lambda n: (rev(n), j))
    return _call(
        body, name="delta_bwd", grid=(nc,),
        in_specs=[blk(0), blk(1), blk(2), pl.BlockSpec((C, LANE), lambda n: (rev(n), 0)),
                  pl.BlockSpec((1, H, DK, DK), lambda n: (rev(n), 0, 0, 0)),
                  pl.BlockSpec((1, H, C, C), lambda n: (rev(n), 0, 0, 0)),
                  pl.BlockSpec((C, D), lambda n: (rev(n), 0))],
        out_specs=(pl.BlockSpec((C, 3 * D), lambda n: (rev(n), 0)), pl.BlockSpec((C, LANE), lambda n: (rev(n), 0))),
        out_shape=(jax.ShapeDtypeStruct((lp, 3 * D), F32), jax.ShapeDtypeStruct((lp, LANE), F32)),
        scratch_shapes=[pltpu.VMEM((H, DK, DK), F32)],
        compiler_params=_cp(("arbitrary",)),
    )(qkv, qkv, qkv, bg, sall, tall, do)


def _o_post_fwd(o, proj, dn_w, w_dn):
    lp = o.shape[0]
    te = _pick(lp, (640, 320))

    def body(o_ref, za_ref, w_ref, wdn_ref, out_ref, ya_ref):
        za = za_ref[...]
        gate = za * _sig(za)
        for h in range(H):
            sl = slice(h * DK, (h + 1) * DK)
            oh = o_ref[:, sl]
            r = lax.rsqrt(jnp.mean(oh * oh, axis=-1, keepdims=True) + EPS)
            out_ref[:, sl] = (oh * r * w_ref[...] * gate[:, sl]).astype(BF16)
        ya_ref[...] = _dot(out_ref[...], wdn_ref[...])

    row = pl.BlockSpec((te, D), lambda i: (i, 0))
    return _call(
        body, name="o_post_fwd", grid=(lp // te,),
        in_specs=[row, pl.BlockSpec((te, D), lambda i: (i, CB_ZA)), pl.BlockSpec((1, DK), lambda i: (0, 0)),
                  pl.BlockSpec((D, D), lambda i: (0, 0))],
        out_specs=(row, row),
        out_shape=(jax.ShapeDtypeStruct((lp, D), BF16), jax.ShapeDtypeStruct((lp, D), F32)),
        compiler_params=_cp(("parallel",)),
    )(o, proj, dn_w, w_dn)


def _o_post_bwd(dy_a, w_dn, o, proj, dn_w, dproj):
    lp = o.shape[0]
    te = _pick(lp, (320,))

    def body(dya_ref, wdn_ref, o_ref, za_ref, w_ref, _, do_ref, dza_ref, dw_ref, don_ref):
        @pl.when(pl.program_id(0) == 0)
        def _():
            dw_ref[...] = jnp.zeros_like(dw_ref)

        don_ref[...] = _dot_nt(dya_ref[...], wdn_ref[...])
        za = za_ref[...]
        sz = _sig(za)
        gate, dgate = za * sz, _dsilu(za, sz)
        w = w_ref[...]
        dw = jnp.zeros((1, DK), F32)
        for h in range(H):
            sl = slice(h * DK, (h + 1) * DK)
            oh, g = o_ref[:, sl], don_ref[:, sl]
            r = lax.rsqrt(jnp.mean(oh * oh, axis=-1, keepdims=True) + EPS)
            ohat = oh * r
            dza_ref[:, sl] = (g * ohat * w * dgate[:, sl]).astype(BF16)
            don = g * gate[:, sl]
            dw = dw + _colsum(don * ohat)
            dohat = don * w
            do_ref[:, sl] = r * (dohat - ohat * jnp.mean(dohat * ohat, axis=-1, keepdims=True))
        dw_ref[...] += dw

    return _call(
        body, name="o_post_bwd", grid=(lp // te,),
        in_specs=[pl.BlockSpec((te, D), lambda i: (i, 0)), pl.BlockSpec((D, D), lambda i: (0, 0)),
                  pl.BlockSpec((te, D), lambda i: (i, 0)),
                  pl.BlockSpec((te, D), lambda i: (i, CB_ZA)), pl.BlockSpec((1, DK), lambda i: (0, 0)),
                  pl.BlockSpec(memory_space=pl.ANY)],
        out_specs=(pl.BlockSpec((te, D), lambda i: (i, 0)), pl.BlockSpec((te, D), lambda i: (i, CB_ZA)),
                   pl.BlockSpec((1, DK), lambda i: (0, 0))),
        out_shape=(jax.ShapeDtypeStruct((lp, D), F32), jax.ShapeDtypeStruct(dproj.shape, dproj.dtype),
                   jax.ShapeDtypeStruct((1, DK), F32)),
        input_output_aliases={5: 1},
        scratch_shapes=[pltpu.VMEM((te, D), F32)],
        compiler_params=_cp(("arbitrary",)),
    )(dy_a, w_dn, o, proj, dn_w, dproj)


def _qkv_conv_bwd(proj, dqkv, conv_w, dproj):
    lp = proj.shape[0]
    te = _pick(lp, (320,))
    hb = te // HALO_Q
    nt = lp // te
    last_hb = lp // HALO_Q - 1

    def body(main_ref, prev_ref, next_ref, dmain_ref, dnext_ref, cw_ref, _, dpre_ref, dcw_ref, pre_scr, dn_scr,
             tap_scr, dco_scr, dsh_scr):
        s, i = pl.program_id(0), pl.program_id(1)

        @pl.when(i == 0)
        def _():
            dcw_ref[...] = jnp.zeros_like(dcw_ref)

        ne = te + HALO_Q
        pre_scr[:HALO_Q, :] = jnp.where(i > 0, prev_ref[...], 0.0)
        pre_scr[HALO_Q:ne, :] = main_ref[...]
        pre_scr[ne:, :] = jnp.where(i < nt - 1, next_ref[...], 0.0)
        dn_scr[:te, :] = dmain_ref[...]
        dn_scr[te:, :] = jnp.where(i < nt - 1, dnext_ref[...], 0.0)
        scale = jnp.where(s == 0, DK ** -0.5, 1.0)
        off = HALO_Q - (KQ - 1)

        def head(h, carry):
            cs = pl.ds(pl.multiple_of(h * DK, DK), DK)
            for j in range(KQ - 1):
                tap_scr[j] = pre_scr[off + j:off + j + ne, cs]
            taps = [tap_scr[j] for j in range(KQ - 1)] + [pre_scr[HALO_Q:, cs]]
            co = cw_ref[0:1, cs] * taps[0]
            for j in range(1, KQ):
                co = co + cw_ref[j:j + 1, cs] * taps[j]
            sg = _sig(co)
            a = co * sg
            g = dn_scr[:, cs]
            r = lax.rsqrt(_rowsum(a * a) + EPS)
            yhat = a * r
            da = jnp.where(s == 2, g, (scale * r) * (g - yhat * _rowsum(g * yhat)))
            dco = da * _dsilu(co, sg)
            dco_scr[...] = dco
            for j in range(KQ - 1):
                dsh_scr[j] = dco_scr[KQ - 1 - j:KQ - 1 - j + te, :]
            dpre = cw_ref[KQ - 1:KQ, cs] * dco[:te, :]
            for j in range(KQ - 1):
                dpre = dpre + cw_ref[j:j + 1, cs] * dsh_scr[j]
            dpre_ref[:, cs] = dpre.astype(BF16)
            dcw_ref[:, cs] += jnp.concatenate([_colsum(dco[:te] * taps[j][:te]) for j in range(KQ)], axis=0)
            return carry

        lax.fori_loop(0, H, head, 0, unroll=True)

    return _call(
        body, name="qkv_conv_bwd", grid=(3, nt),
        in_specs=[pl.BlockSpec((te, D), lambda s, i: (i, s)),
                  pl.BlockSpec((HALO_Q, D), lambda s, i: (jnp.maximum(i * hb - 1, 0), s)),
                  pl.BlockSpec((HALO_Q, D), lambda s, i: (jnp.minimum((i + 1) * hb, last_hb), s)),
                  pl.BlockSpec((te, D), lambda s, i: (i, s)),
                  pl.BlockSpec((HALO_Q, D), lambda s, i: (jnp.minimum((i + 1) * hb, last_hb), s)),
                  pl.BlockSpec((KQ, D), lambda s, i: (0, s)),
                  pl.BlockSpec(memory_space=pl.ANY)],
        out_specs=(pl.BlockSpec((te, D), lambda s, i: (i, s)), pl.BlockSpec((KQ, D), lambda s, i: (0, s))),
        out_shape=(jax.ShapeDtypeStruct(dproj.shape, dproj.dtype), jax.ShapeDtypeStruct((KQ, 3 * D), F32)),
        input_output_aliases={6: 0},
        scratch_shapes=[pltpu.VMEM((te + 2 * HALO_Q, D), F32), pltpu.VMEM((te + HALO_Q, D), F32),
                        pltpu.VMEM((KQ - 1, te + HALO_Q, DK), F32), pltpu.VMEM((te + HALO_Q, DK), F32),
                        pltpu.VMEM((KQ - 1, te, DK), F32)],
        compiler_params=_cp(("arbitrary", "arbitrary")),
    )(proj, proj, proj, dqkv, dqkv, conv_w, dproj)


def _ba_bwd(dbg, ba, ab, pad):
    lp = ba.shape[0]
    te = _pick(lp, (640, 320))

    def body(dbg_ref, ba_ref, ab_ref, dba_ref, dab_ref):
        i = pl.program_id(0)

        @pl.when(i == 0)
        def _():
            dab_ref[...] = jnp.zeros_like(dab_ref)

        ba, ab = ba_ref[...], ab_ref[...]
        lane = lax.broadcasted_iota(jnp.int32, ba.shape, 1)
        rows = i * te + lax.broadcasted_iota(jnp.int32, ba.shape, 0)
        g = jnp.where((lane < 2 * H) & (rows >= pad), dbg_ref[...], 0.0)
        sb = _sig(ba)
        z = ba + ab[1:2, :]
        sp = jnp.maximum(z, 0.0) + jnp.log(1.0 + jnp.exp(-jnp.abs(z)))
        nea = -jnp.exp(ab[0:1, :])
        dz = g * nea * _sig(z)
        dba_ref[...] = jnp.where(lane < H, g * sb * (1.0 - sb), dz).astype(BF16)
        is_g = (lane >= H) & (lane < 2 * H)
        dab_ref[...] += jnp.concatenate([_colsum(jnp.where(is_g, g * nea * sp, 0.0)),
                                         _colsum(jnp.where(is_g, dz, 0.0))], axis=0)

    return _call(
        body, name="ba_bwd", grid=(lp // te,),
        in_specs=[pl.BlockSpec((te, LANE), lambda i: (i, 0)), pl.BlockSpec((te, LANE), lambda i: (i, 0)),
                  pl.BlockSpec((2, LANE), lambda i: (0, 0))],
        out_specs=(pl.BlockSpec((te, LANE), lambda i: (i, 0)), pl.BlockSpec((2, LANE), lambda i: (0, 0))),
        out_shape=(jax.ShapeDtypeStruct((lp, LANE), BF16), jax.ShapeDtypeStruct((2, LANE), F32)),
        compiler_params=_cp(("arbitrary",)),
    )(dbg, ba, ab)


SUBLANES = 8
CONV_RB = 64


def _fill_shifted(sh_scr, src_scr, cs):
    n = sh_scr.shape[1]
    for s in range(1, SUBLANES):
        sh_scr[s] = src_scr[s:s + n, cs]


def _shifted(sh_scr, src_scr, cs, r, r0, n):
    s, a8 = r % SUBLANES, r - r % SUBLANES
    if s == 0:
        return src_scr[r0 + a8:r0 + a8 + n, cs]
    return sh_scr[s, r0 + a8:r0 + a8 + n, :]


def _conv_b_fwd(proj, dw_w, dw_b, ln_w, ln_b, w_cf):
    lp = proj.shape[0]
    te = _pick(lp, (320,))
    hb = te // HALO_D

    def body(a_ref, b_ref, ha_ref, hb_ref, zb_ref, w_ref, wb_ref, lw_ref, lb_ref, wcf_ref, c1_ref, c3_ref, yb_ref,
             c0_scr, sh_scr):
        i = pl.program_id(0)
        c0_scr[:HALO_D, :] = jnp.where(i > 0, ha_ref[...] * _sig(hb_ref[...]), 0.0)
        c0_scr[HALO_D:, :] = a_ref[...] * _sig(b_ref[...])
        off = HALO_D - (KD - 1)
        def lane_block(cb, carry):
            cs = pl.ds(pl.multiple_of(cb * LANE, LANE), LANE)
            _fill_shifted(sh_scr, c0_scr, cs)
            for r0 in range(0, te, CONV_RB):
                acc = None
                for j in range(KD):
                    term = w_ref[j:j + 1, cs] * _shifted(sh_scr, c0_scr, cs, off + j, r0, CONV_RB)
                    acc = term if acc is None else acc + term
                c1_ref[r0:r0 + CONV_RB, cs] = acc + wb_ref[:, cs]
            return carry

        lax.fori_loop(0, D // LANE, lane_block, 0)
        c1 = c1_ref[...]
        mu = jnp.mean(c1, axis=-1, keepdims=True)
        xc = c1 - mu
        c2 = xc * lax.rsqrt(jnp.mean(xc * xc, axis=-1, keepdims=True) + EPS) * lw_ref[...] + lb_ref[...]
        zb = zb_ref[...]
        c3 = (c2 * _sig(c2) * zb * _sig(zb)).astype(BF16)
        c3_ref[...] = c3
        yb_ref[...] = _dot(c3, wcf_ref[...])

    vec = pl.BlockSpec((1, D), lambda i: (0, 0))
    row = pl.BlockSpec((te, D), lambda i: (i, 0))
    return _call(
        body, name="conv_b_fwd", grid=(lp // te,),
        in_specs=[pl.BlockSpec((te, D), lambda i: (i, CB_GA_)), pl.BlockSpec((te, D), lambda i: (i, CB_GB_)),
                  pl.BlockSpec((HALO_D, D), lambda i: (jnp.maximum(i * hb - 1, 0), CB_GA_)),
                  pl.BlockSpec((HALO_D, D), lambda i: (jnp.maximum(i * hb - 1, 0), CB_GB_)),
                  pl.BlockSpec((te, D), lambda i: (i, CB_ZB)),
                  pl.BlockSpec((KD, D), lambda i: (0, 0)), vec, vec, vec, pl.BlockSpec((D, D), lambda i: (0, 0))],
        out_specs=(row, row, row),
        out_shape=(jax.ShapeDtypeStruct((lp, D), F32), jax.ShapeDtypeStruct((lp, D), BF16),
                   jax.ShapeDtypeStruct((lp, D), F32)),
        scratch_shapes=[pltpu.VMEM((te + HALO_D, D), F32), pltpu.VMEM((SUBLANES, te + HALO_D - SUBLANES, LANE), F32)],
        compiler_params=_cp(("parallel",)),
    )(proj, proj, proj, proj, proj, dw_w, dw_b, ln_w, ln_b, w_cf)


def _conv_b_bwd1(dy_b, w_cf, c1, proj, ln_w, ln_b, dproj):
    lp = c1.shape[0]
    te = _pick(lp, (320,))

    def body(dyb_ref, wcf_ref, c1_ref, zb_ref, lw_ref, lb_ref, _, dc1_ref, dzb_ref, sums_ref):
        @pl.when(pl.program_id(0) == 0)
        def _():
            sums_ref[...] = jnp.zeros_like(sums_ref)

        c1, g = c1_ref[...], _dot_nt(dyb_ref[...], wcf_ref[...])
        mu = jnp.mean(c1, axis=-1, keepdims=True)
        xc = c1 - mu
        rstd = lax.rsqrt(jnp.mean(xc * xc, axis=-1, keepdims=True) + EPS)
        xh = xc * rstd
        lw = lw_ref[...]
        c2 = xh * lw + lb_ref[...]
        s2 = _sig(c2)
        zb = zb_ref[...]
        sz = _sig(zb)
        dc2 = g * (zb * sz) * _dsilu(c2, s2)
        dzb_ref[...] = (g * (c2 * s2) * _dsilu(zb, sz)).astype(BF16)
        dxh = dc2 * lw
        dc1 = rstd * (dxh - jnp.mean(dxh, axis=-1, keepdims=True) - xh * jnp.mean(dxh * xh, axis=-1, keepdims=True))
        dc1_ref[...] = dc1
        sums_ref[...] += jnp.concatenate([_colsum(dc2 * xh), _colsum(dc2), _colsum(dc1)], axis=0)

    vec = pl.BlockSpec((1, D), lambda i: (0, 0))
    return _call(
        body, name="conv_b_bwd1", grid=(lp // te,),
        in_specs=[pl.BlockSpec((te, D), lambda i: (i, 0)), pl.BlockSpec((D, D), lambda i: (0, 0)),
                  pl.BlockSpec((te, D), lambda i: (i, 0)),
                  pl.BlockSpec((te, D), lambda i: (i, CB_ZB)), vec, vec, pl.BlockSpec(memory_space=pl.ANY)],
        out_specs=(pl.BlockSpec((te, D), lambda i: (i, 0)), pl.BlockSpec((te, D), lambda i: (i, CB_ZB)),
                   pl.BlockSpec((3, D), lambda i: (0, 0))),
        out_shape=(jax.ShapeDtypeStruct((lp, D), F32), jax.ShapeDtypeStruct(dproj.shape, dproj.dtype),
                   jax.ShapeDtypeStruct((3, D), F32)),
        input_output_aliases={6: 1},
        compiler_params=_cp(("arbitrary",)),
    )(dy_b, w_cf, c1, proj, ln_w, ln_b, dproj)


def _conv_b_bwd2(dc1, proj, dw_w, dproj):
    lp = dc1.shape[0]
    te = _pick(lp, (320,))
    hb = te // HALO_D
    nt = lp // te
    last_hb = lp // HALO_D - 1

    def body(g_ref, gn_ref, a_ref, b_ref, ha_ref, hb_ref, w_ref, _, dab_ref, dw_ref, c0_scr, g_scr, dc0_scr,
             csh_scr, gsh_scr):
        i = pl.program_id(0)

        @pl.when(i == 0)
        def _():
            dw_ref[...] = jnp.zeros_like(dw_ref)

        a, b = a_ref[...], b_ref[...]
        sb = _sig(b)
        c0_scr[:HALO_D, :] = jnp.where(i > 0, ha_ref[...] * _sig(hb_ref[...]), 0.0)
        c0_scr[HALO_D:, :] = a * sb
        g_scr[:te, :] = g_ref[...]
        g_scr[te:, :] = jnp.where(i < nt - 1, gn_ref[...], 0.0)
        off = HALO_D - (KD - 1)
        def lane_block(cb, carry):
            cs = pl.ds(pl.multiple_of(cb * LANE, LANE), LANE)
            _fill_shifted(csh_scr, c0_scr, cs)
            _fill_shifted(gsh_scr, g_scr, cs)
            for r0 in range(0, te, CONV_RB):
                acc = None
                for j in range(KD):
                    term = w_ref[j:j + 1, cs] * _shifted(gsh_scr, g_scr, cs, KD - 1 - j, r0, CONV_RB)
                    acc = term if acc is None else acc + term
                dc0_scr[r0:r0 + CONV_RB, cs] = acc
            parts = [None] * KD
            for r0 in range(0, te, CONV_RB):
                g = g_scr[r0:r0 + CONV_RB, cs].reshape(CONV_RB // SUBLANES, SUBLANES, LANE)
                for j in range(KD):
                    x = _shifted(csh_scr, c0_scr, cs, off + j, r0, CONV_RB)
                    p = jnp.sum(g * x.reshape(CONV_RB // SUBLANES, SUBLANES, LANE), axis=0)
                    parts[j] = p if parts[j] is None else parts[j] + p
            dw_ref[:, cs] += jnp.concatenate([_colsum(p) for p in parts], axis=0)
            return carry

        lax.fori_loop(0, D // LANE, lane_block, 0)
        dc0 = dc0_scr[...]
        dab_ref[:, :D] = (dc0 * sb).astype(BF16)
        dab_ref[:, D:] = (dc0 * a * sb * (1.0 - sb)).astype(BF16)

    return _call(
        body, name="conv_b_bwd2", grid=(nt,),
        in_specs=[pl.BlockSpec((te, D), lambda i: (i, 0)),
                  pl.BlockSpec((HALO_D, D), lambda i: (jnp.minimum((i + 1) * hb, last_hb), 0)),
                  pl.BlockSpec((te, D), lambda i: (i, CB_GA_)), pl.BlockSpec((te, D), lambda i: (i, CB_GB_)),
                  pl.BlockSpec((HALO_D, D), lambda i: (jnp.maximum(i * hb - 1, 0), CB_GA_)),
                  pl.BlockSpec((HALO_D, D), lambda i: (jnp.maximum(i * hb - 1, 0), CB_GB_)),
                  pl.BlockSpec((KD, D), lambda i: (0, 0)), pl.BlockSpec(memory_space=pl.ANY)],
        out_specs=(pl.BlockSpec((te, 2 * D), lambda i: (i, CB_GA_ // 2)), pl.BlockSpec((KD, D), lambda i: (0, 0))),
        out_shape=(jax.ShapeDtypeStruct(dproj.shape, dproj.dtype), jax.ShapeDtypeStruct((KD, D), F32)),
        input_output_aliases={7: 0},
        scratch_shapes=[pltpu.VMEM((te + HALO_D, D), F32), pltpu.VMEM((te + HALO_D, D), F32), pltpu.VMEM((te, D), F32),
                        pltpu.VMEM((SUBLANES, te + HALO_D - SUBLANES, LANE), F32),
                        pltpu.VMEM((SUBLANES, te + HALO_D - SUBLANES, LANE), F32)],
        compiler_params=_cp(("arbitrary",)),
    )(dc1, dc1, proj, proj, proj, proj, dw_w, dproj)


def _merge_fwd(y_a, y_b, proj, b_cf, w_o):
    lp = y_a.shape[0]
    te = _pick(lp, (320,))

    def body(ya_ref, yb_ref, ga_ref, gb_ref, bias_ref, wo_ref, out_ref, z_ref):
        merged = (_sig(ga_ref[...]) * ya_ref[...] + _sig(gb_ref[...]) * (yb_ref[...] + bias_ref[...])).astype(BF16)
        out_ref[...] = merged
        z_ref[...] = _dot(merged, wo_ref[...])

    row = lambda j: pl.BlockSpec((te, D), lambda i: (i, j))
    return _call(
        body, name="merge_fwd", grid=(lp // te,),
        in_specs=[row(0), row(0), row(CB_MA), row(CB_MB), pl.BlockSpec((1, D), lambda i: (0, 0)),
                  pl.BlockSpec((D, D), lambda i: (0, 0))],
        out_specs=(row(0), row(0)),
        out_shape=(jax.ShapeDtypeStruct((lp, D), BF16), jax.ShapeDtypeStruct((lp, D), F32)),
        compiler_params=_cp(("parallel",)),
    )(y_a, y_b, proj, proj, b_cf, w_o)


def _merge_bwd(dx_out_b, w_o, y_a, y_b, proj, b_cf):
    lp = y_a.shape[0]
    te = _pick(lp, (320,))

    def body(dx_ref, wo_ref, ya_ref, yb_ref, ga_ref, gb_ref, bias_ref, dya_ref, dyb_ref, dg_ref, db_ref):
        @pl.when(pl.program_id(0) == 0)
        def _():
            db_ref[...] = jnp.zeros_like(db_ref)

        dm = _dot_nt(dx_ref[...], wo_ref[...])
        sa, sb = _sig(ga_ref[...]), _sig(gb_ref[...])
        dyb = sb * dm
        dya_ref[...] = (sa * dm).astype(BF16)
        dyb_ref[...] = dyb.astype(BF16)
        dg_ref[:, :D] = (dm * ya_ref[...] * sa * (1.0 - sa)).astype(BF16)
        dg_ref[:, D:] = (dm * (yb_ref[...] + bias_ref[...]) * sb * (1.0 - sb)).astype(BF16)
        db_ref[...] += _colsum(dyb)

    row = lambda j: pl.BlockSpec((te, D), lambda i: (i, j))
    act = jax.ShapeDtypeStruct((lp, D), BF16)
    return _call(
        body, name="merge_bwd", grid=(lp // te,),
        in_specs=[row(0), pl.BlockSpec((D, D), lambda i: (0, 0)), row(0), row(0), row(CB_MA), row(CB_MB),
                  pl.BlockSpec((1, D), lambda i: (0, 0))],
        out_specs=(row(0), row(0), pl.BlockSpec((te, 2 * D), lambda i: (i, CB_MA // 2)),
                   pl.BlockSpec((1, D), lambda i: (0, 0))),
        out_shape=(act, act, jax.ShapeDtypeStruct((lp, NCB * D), BF16), jax.ShapeDtypeStruct((1, D), F32)),
        compiler_params=_cp(("arbitrary",)),
    )(dx_out_b, w_o, y_a, y_b, proj, proj, b_cf)


def _final_fwd_bwd(x_ext, z, target, final_w):
    lp = x_ext.shape[0]
    te = LANE

    def body(x_ref, z_ref, t_ref, w_ref, dx_ref, dxb_ref, loss_ref, dw_ref):
        i = pl.program_id(0)

        @pl.when(i == 0)
        def _():
            loss_ref[...] = jnp.zeros_like(loss_ref)
            dw_ref[...] = jnp.zeros_like(dw_ref)

        xo = x_ref[...] + z_ref[...]
        r = lax.rsqrt(jnp.mean(xo * xo, axis=-1, keepdims=True) + EPS)
        xhat = xo * r
        w = w_ref[...]
        err = jnp.where(i > 0, xhat * w - t_ref[...], 0.0)
        loss_ref[...] += 0.5 * jnp.sum(jnp.mean(err * err, axis=-1, keepdims=True), keepdims=True)
        dy = err * (1.0 / D)
        dw_ref[...] += _colsum(dy * xhat)
        dxn = dy * w
        dx = r * (dxn - xhat * jnp.mean(dxn * xhat, axis=-1, keepdims=True))
        dx_ref[...] = dx
        dxb_ref[...] = dx.astype(BF16)

    return _call(
        body, name="final_fwd_bwd", grid=(lp // te,),
        in_specs=[pl.BlockSpec((te, D), lambda i: (i, 0)), pl.BlockSpec((te, D), lambda i: (i, 0)),
                  pl.BlockSpec((te, D), lambda i: (jnp.maximum(i - 1, 0), 0)), pl.BlockSpec((1, D), lambda i: (0, 0))],
        out_specs=(pl.BlockSpec((te, D), lambda i: (i, 0)), pl.BlockSpec((te, D), lambda i: (i, 0)),
                   pl.BlockSpec((1, 1), lambda i: (0, 0)), pl.BlockSpec((1, D), lambda i: (0, 0))),
        out_shape=(jax.ShapeDtypeStruct((lp, D), F32), jax.ShapeDtypeStruct((lp, D), BF16),
                   jax.ShapeDtypeStruct((1, 1), F32), jax.ShapeDtypeStruct((1, D), F32)),
        compiler_params=_cp(("arbitrary",)),
    )(x_ext, z, target, final_w)


def _prenorm_bwd(dh, x_ext, dx_out, norm_w, seq):
    lp = x_ext.shape[0]
    te = LANE

    def body(dh_ref, x_ref, dxo_ref, w_ref, gx_ref, head_ref, dw_ref):
        i = pl.program_id(0)

        @pl.when(i == 0)
        def _():
            dw_ref[...] = jnp.zeros_like(dw_ref)

        x, dh = x_ref[...], dh_ref[...]
        r = lax.rsqrt(jnp.mean(x * x, axis=-1, keepdims=True) + EPS)
        xhat = x * r
        dxn = dh * w_ref[...]
        dx = dxo_ref[...] + r * (dxn - xhat * jnp.mean(dxn * xhat, axis=-1, keepdims=True))
        dw_ref[...] += _colsum(dh * xhat)

        @pl.when(i == 0)
        def _():
            head_ref[...] = dx

        @pl.when(i > 0)
        def _():
            gx_ref[...] = dx

    row = pl.BlockSpec((te, D), lambda i: (i, 0))
    return _call(
        body, name="prenorm_bwd", grid=(lp // te,),
        in_specs=[row, row, row, pl.BlockSpec((1, D), lambda i: (0, 0))],
        out_specs=(pl.BlockSpec((te, D), lambda i: (jnp.maximum(i - 1, 0), 0)), pl.BlockSpec((te, D), lambda i: (0, 0)),
                   pl.BlockSpec((1, D), lambda i: (0, 0))),
        out_shape=(jax.ShapeDtypeStruct((seq, D), F32), jax.ShapeDtypeStruct((te, D), F32),
                   jax.ShapeDtypeStruct((1, D), F32)),
        compiler_params=_cp(("arbitrary",)),
    )(dh, x_ext, dx_out, norm_w)


def _adam_reduce(parts, w, m, v, name):
    r, n = w.shape
    tr = _pick(r, (128,)) if r % 128 == 0 else r

    def body(p_ref, w_ref, m_ref, v_ref, g_ref, d_ref, m2_ref, v2_ref):
        g = p_ref[0]
        for s in range(1, NDEV):
            g = g + p_ref[s]
        _adam_write(g, w_ref, m_ref, v_ref, g_ref, d_ref, m2_ref, v2_ref)

    blk = pl.BlockSpec((tr, n), lambda i: (i, 0))
    out = jax.ShapeDtypeStruct((r, n), F32)
    return _call(
        body, name=name, grid=(r // tr,),
        in_specs=[pl.BlockSpec((NDEV, tr, n), lambda i: (0, i, 0)), blk, blk, blk],
        out_specs=(blk, blk, blk, blk), out_shape=(out, out, out, out),
        compiler_params=_cp(("parallel",)),
    )(parts, w, m, v)


def _adam_write(g, w_ref, m_ref, v_ref, g_ref, d_ref, m2_ref, v2_ref):
    c1 = 1.0 - ADAM_B1 ** ADAM_STEP
    c2 = 1.0 - ADAM_B2 ** ADAM_STEP
    m2 = ADAM_B1 * m_ref[...] + (1.0 - ADAM_B1) * g
    v2 = ADAM_B2 * v_ref[...] + (1.0 - ADAM_B2) * (g * g)
    g_ref[...] = g
    m2_ref[...] = m2
    v2_ref[...] = v2
    d_ref[...] = -ADAM_LR * ((m2 / c1) / (jnp.sqrt(v2 / c2) + ADAM_EPS) + ADAM_WD * w_ref[...])


def _adam_chips(own, recv, w, m, v, name):
    r, n = w.shape
    tr, tc = _shard_tile(r, n)

    def body(own_ref, p_ref, w_ref, m_ref, v_ref, g_ref, d_ref, m2_ref, v2_ref):
        my_chip = 2 * lax.axis_index("x") + lax.axis_index("y")
        g = None
        for j in range(NCHIP):
            part = jnp.where(my_chip == j, own_ref[...], p_ref[j].astype(F32))
            g = part if g is None else g + part
        _adam_write(g, w_ref, m_ref, v_ref, g_ref, d_ref, m2_ref, v2_ref)

    blk = pl.BlockSpec((tr, tc), lambda i, k: (i, k))
    out = jax.ShapeDtypeStruct((r, n), F32)
    return _call(
        body, name=name, grid=(r // tr, n // tc),
        in_specs=[blk, pl.BlockSpec((NCHIP, tr, tc), lambda i, k: (0, i, k)), blk, blk, blk],
        out_specs=(blk, blk, blk, blk), out_shape=(out, out, out, out),
        compiler_params=_cp(("parallel", "parallel")),
    )(own, recv, w, m, v)


SMALL = ("norm_w", "a_log", "dt_bias", "dn_norm_w", "dw_b", "ln_w", "ln_b", "b_cf_out", "final_norm_w")


def kernel(x, meta, norm_w, w_in, conv_qkv_w, a_log, dt_bias, dn_norm_w, w_dn_out, dw_w, dw_b, ln_w, ln_b, w_cf_out, b_cf_out, w_o, final_norm_w, loss_target, m_meta, m_norm_w, m_w_in, m_conv_qkv_w, m_a_log, m_dt_bias, m_dn_norm_w, m_w_dn_out, m_dw_w, m_dw_b, m_ln_w, m_ln_b, m_w_cf_out, m_b_cf_out, m_w_o, m_final_norm_w, v_meta, v_norm_w, v_w_in, v_conv_qkv_w, v_a_log, v_dt_bias, v_dn_norm_w, v_w_dn_out, v_dw_w, v_dw_b, v_ln_w, v_ln_b, v_w_cf_out, v_b_cf_out, v_w_o, v_final_norm_w):
    seq = x.shape[1]
    pad = (-(seq + NMETA)) % LANE
    in_w = w_in.shape[2] * NDEV
    n_qkvz = 4 * D
    n_ba = 2 * H

    w_in_g, w_dn_g, w_cf_g, w_o_g, meta_g, cqw_g, dww_g = _gather_two_level(
        [w_in[0].astype(BF16).T, w_dn_out[0].astype(BF16), w_cf_out[0].astype(BF16), w_o[0].astype(BF16),
         meta, conv_qkv_w[0], dw_w[0]], "gather_weights")
    w_full_t = w_in_g.reshape(in_w, D)
    c_glu = n_qkvz + n_ba
    c_zb, c_mg = c_glu + 2 * D, c_glu + 3 * D
    w_main_t = jnp.concatenate([w_full_t[:n_qkvz], w_full_t[c_glu:c_zb], w_full_t[c_mg:], w_full_t[c_zb:c_mg]],
                               axis=0)
    w_ba_t = jnp.pad(w_full_t[n_qkvz:n_qkvz + n_ba], ((0, LANE - n_ba), (0, 0)))
    w_dn, w_cf, w_oo = (t.reshape(D, D) for t in (w_dn_g, w_cf_g, w_o_g))
    meta_full = jnp.transpose(meta_g, (1, 0, 2)).reshape(NMETA, D)
    cqw = jnp.transpose(cqw_g, (1, 0, 2)).reshape(KQ, 3 * D)
    dww = jnp.transpose(dww_g, (1, 0, 2)).reshape(KD, D)
    ab = jnp.pad(jnp.concatenate([a_log, dt_bias], axis=0), ((0, 0), (H, LANE - 2 * H)))

    x_ext = jnp.concatenate([jnp.zeros((pad, D), F32), meta_full, x[0]], axis=0)

    proj, ba, h = _proj_fwd(x_ext, norm_w, w_main_t, w_ba_t)
    qkv, bg = _qkv_conv_fwd(proj, ba, cqw, ab, pad)
    o, sall, tall = _delta_fwd(qkv, bg)
    o_n, y_a = _o_post_fwd(o, proj, dn_norm_w, w_dn)
    c1, c3, y_b = _conv_b_fwd(proj, dww, dw_b, ln_w, ln_b, w_cf)
    merged, z = _merge_fwd(y_a, y_b, proj, b_cf_out, w_oo)
    dx_out, dx_out_b, loss_part, g_final_w = _final_fwd_bwd(x_ext, z, loss_target[0], final_norm_w.reshape(1, D))

    g_w_o = _mm_tn(merged, dx_out_b, "g_w_o_mm")
    dy_a, dy_b, dproj, g_b_cf = _merge_bwd(dx_out_b, w_oo, y_a, y_b, proj, b_cf_out)
    g_w_cf = _mm_tn(c3, dy_b, "g_w_cf_mm")
    g_w_dn = _mm_tn(o_n, dy_a, "g_w_dn_mm")
    dc1, dproj, sums_b = _conv_b_bwd1(dy_b, w_cf, c1, proj, ln_w, ln_b, dproj)
    dproj, g_dw_w = _conv_b_bwd2(dc1, proj, dww, dproj)
    do, dproj, g_dn_w = _o_post_bwd(dy_a, w_dn, o, proj, dn_norm_w, dproj)
    dqkv, dbg = _delta_bwd(qkv, bg, sall, tall, do)
    dproj, g_cqw = _qkv_conv_bwd(proj, dqkv, cqw, dproj)
    dba, dab = _ba_bwd(dbg, ba, ab, pad)
    dh = _dh_mm(dproj, dba, w_main_t, w_ba_t)
    g_w_main_t = _mm_tn(dproj, h, "g_w_main_mm")
    g_w_ba_t = _mm_tn(dba, h, "g_w_ba_mm")
    grad_x, dhead, g_norm_w = _prenorm_bwd(dh, x_ext, dx_out, norm_w, seq)

    g_w_full_t = jnp.concatenate([g_w_main_t[:n_qkvz], g_w_ba_t[:n_ba], g_w_main_t[CB_GA_ * D:CB_MA * D],
                                  g_w_main_t[CB_ZB * D:], g_w_main_t[CB_MA * D:CB_ZB * D]], axis=0)
    split_cols = lambda t: jnp.transpose(t.reshape(t.shape[0], NDEV, t.shape[1] // NDEV), (1, 0, 2))
    small = {"norm_w": g_norm_w, "a_log": dab[0:1, H:2 * H], "dt_bias": dab[1:2, H:2 * H], "dn_norm_w": g_dn_w,
             "dw_b": sums_b[2:3], "ln_w": sums_b[0:1], "ln_b": sums_b[1:2], "b_cf_out": g_b_cf,
             "final_norm_w": g_final_w}
    small_vec = jnp.concatenate([small[k] for k in SMALL], axis=1)
    ns = small_vec.shape[1]
    ns_pad = (-ns) % LANE
    small_vec = jnp.pad(small_vec, ((0, 0), (0, ns_pad)))
    big = [g_w_full_t.reshape(NDEV, in_w // NDEV, D), g_w_dn.reshape(NDEV, D // NDEV, D),
           g_w_cf.reshape(NDEV, D // NDEV, D), g_w_o.reshape(NDEV, D // NDEV, D)]
    from_sibling = _swap_sibling(big, "swap_sibling")
    pairs = [_pair_add(a, g, f"pair_add_{i}") for i, (a, g) in enumerate(zip(big, from_sibling))]
    from_chips = _scatter_chips([p for p, _ in pairs], "scatter_chips")
    p_meta, p_cqw, p_dww, p_small = _exchange(
        [split_cols(dhead[pad:pad + NMETA]), split_cols(g_cqw), split_cols(g_dw_w), small_vec],
        [True] * 3 + [False], "exchange_small")

    res = {}
    res["w_in"] = tuple(t.T for t in _adam_chips(pairs[0][1], from_chips[0], w_in[0].T, m_w_in[0].T, v_w_in[0].T,
                                                   "adam_w_in"))
    res["w_dn_out"] = _adam_chips(pairs[1][1], from_chips[1], w_dn_out[0], m_w_dn_out[0], v_w_dn_out[0], "adam_w_dn")
    res["w_cf_out"] = _adam_chips(pairs[2][1], from_chips[2], w_cf_out[0], m_w_cf_out[0], v_w_cf_out[0], "adam_w_cf")
    res["w_o"] = _adam_chips(pairs[3][1], from_chips[3], w_o[0], m_w_o[0], v_w_o[0], "adam_w_o")
    res["meta"] = _adam_reduce(p_meta, meta, m_meta, v_meta, "adam_meta")
    res["conv_qkv_w"] = _adam_reduce(p_cqw, conv_qkv_w[0], m_conv_qkv_w[0], v_conv_qkv_w[0], "adam_conv_qkv_w")
    res["dw_w"] = _adam_reduce(p_dww, dw_w[0], m_dw_w[0], v_dw_w[0], "adam_dw_w")
    loc = dict(norm_w=(norm_w, m_norm_w, v_norm_w), a_log=(a_log, m_a_log, v_a_log), dt_bias=(dt_bias, m_dt_bias, v_dt_bias),
               dn_norm_w=(dn_norm_w, m_dn_norm_w, v_dn_norm_w), dw_b=(dw_b, m_dw_b, v_dw_b), ln_w=(ln_w, m_ln_w, v_ln_w),
               ln_b=(ln_b, m_ln_b, v_ln_b), b_cf_out=(b_cf_out, m_b_cf_out, v_b_cf_out),
               final_norm_w=(final_norm_w, m_final_norm_w, v_final_norm_w))
    cat = lambda j: jnp.pad(jnp.concatenate([loc[k][j].reshape(1, -1) for k in SMALL], axis=1), ((0, 0), (0, ns_pad)))
    small_res = _adam_reduce(p_small, cat(0), cat(1), cat(2), "adam_small")
    off = 0
    for k in SMALL:
        wshape = loc[k][0].shape
        nk = loc[k][0].size
        res[k] = tuple(t[:, off:off + nk].reshape(wshape) for t in small_res)
        off += nk
    shaped = dict(w_in=w_in.shape, w_dn_out=w_dn_out.shape, w_cf_out=w_cf_out.shape, w_o=w_o.shape, meta=meta.shape,
                  conv_qkv_w=conv_qkv_w.shape, dw_w=dw_w.shape)
    for k, shp in shaped.items():
        res[k] = tuple(t.reshape(shp) for t in res[k])

    loss = lax.psum(loss_part[0, 0], ("x", "y", "c"))
    order = ("meta", "norm_w", "w_in", "conv_qkv_w", "a_log", "dt_bias", "dn_norm_w", "w_dn_out", "dw_w", "dw_b", "ln_w",
             "ln_b", "w_cf_out", "b_cf_out", "w_o", "final_norm_w")
    outs = [loss, grad_x[None]]
    for j in range(4):
        outs += [res[k][j] for k in order]
    return tuple(outs)
```

```python
import functools

import jax
import jax.numpy as jnp
from jax import lax
from jax.experimental import pallas as pl
from jax.experimental.pallas import tpu as pltpu

F32 = jnp.float32
BF16 = jnp.bfloat16
HI = lax.Precision.HIGHEST

D = 1024
H = 8
DK = 128
C = 64
NMETA = 16
KQ = 4
KD = 31
HALO_Q = 8
HALO_D = 32
EPS = 1e-6
NDEV = 8
LANE = 128
MIB = 1024 * 1024

ADAM_LR, ADAM_B1, ADAM_B2, ADAM_EPS, ADAM_WD, ADAM_STEP = 0.001, 0.9, 0.999, 1e-08, 0.01, 10

CB_Q, CB_K, CB_V, CB_ZA, CB_GA_, CB_GB_, CB_MA, CB_MB, CB_ZB = range(9)
NCB = 9


def _pick(n, cands):
    for c in cands:
        if n % c == 0:
            return c
    raise ValueError(f"no tile for {n}")


def _cp(sem=None, vmem_mib=40):
    kw = dict(vmem_limit_bytes=vmem_mib * MIB)
    if sem is not None:
        kw["dimension_semantics"] = sem
    return pltpu.CompilerParams(**kw)


def _call(body, **kw):
    return pl.pallas_call(body, **kw)


def _dot(a, b):
    return jnp.dot(a.astype(BF16), b.astype(BF16), preferred_element_type=F32)


def _dot_nt(a, b):
    return lax.dot_general(a.astype(BF16), b.astype(BF16), (((1,), (1,)), ((), ())), preferred_element_type=F32)


def _dot_tn(a, b):
    return lax.dot_general(a.astype(BF16), b.astype(BF16), (((0,), (0,)), ((), ())), preferred_element_type=F32)


def _dot_hi(a, b):
    return jnp.dot(a, b, precision=HI, preferred_element_type=F32)


def _sig(x):
    return 0.5 * jnp.tanh(0.5 * x) + 0.5


def _dsilu(x, s):
    return s * (1.0 + x * (1.0 - s))


def _rowsum(x):
    return jnp.sum(x, axis=-1, keepdims=True)


def _colsum(x):
    return jnp.sum(x, axis=0, keepdims=True)


def _exchange(arrs, scatter, name):
    n = len(arrs)
    out_shape = []
    for a, sc in zip(arrs, scatter):
        shp = a.shape if sc else (NDEV,) + a.shape
        out_shape.append(jax.ShapeDtypeStruct(shp, a.dtype))

    def body(*refs):
        ins, outs = refs[:n], refs[n:2 * n]
        send_sems, recv_sems, loc_sems = refs[2 * n:]
        x, y, c = lax.axis_index("x"), lax.axis_index("y"), lax.axis_index("c")
        me = 4 * x + 2 * y + c
        copies = []
        for a in range(n):
            for k in range(1, NDEV):
                px = 1 - x if (k >> 2) & 1 else x
                py = 1 - y if (k >> 1) & 1 else y
                pc = 1 - c if k & 1 else c
                src = ins[a].at[4 * px + 2 * py + pc] if scatter[a] else ins[a]
                cp = pltpu.make_async_remote_copy(
                    src_ref=src, dst_ref=outs[a].at[me],
                    send_sem=send_sems.at[a * (NDEV - 1) + k - 1], recv_sem=recv_sems.at[a * (NDEV - 1) + k - 1],
                    device_id=(px, py, pc), device_id_type=pl.DeviceIdType.MESH)
                cp.start()
                copies.append(cp)
            loc = pltpu.make_async_copy(ins[a].at[me] if scatter[a] else ins[a], outs[a].at[me], loc_sems.at[a])
            loc.start()
            copies.append(loc)
        for cp in copies:
            cp.wait()

    any_spec = pl.BlockSpec(memory_space=pl.ANY)
    return _call(
        body, name=name, out_shape=tuple(out_shape),
        in_specs=[any_spec] * n, out_specs=tuple([any_spec] * n),
        scratch_shapes=[pltpu.SemaphoreType.DMA((n * (NDEV - 1),)), pltpu.SemaphoreType.DMA((n * (NDEV - 1),)),
                        pltpu.SemaphoreType.DMA((n,))],
    )(*arrs)


NCHIP = 4


def _gather_two_level(arrs, name):
    n = len(arrs)
    per = NDEV - 1

    def body(*refs):
        ins, outs = refs[:n], refs[n:2 * n]
        send_sems, recv_sems, loc_sems = refs[2 * n:]
        x, y, c = lax.axis_index("x"), lax.axis_index("y"), lax.axis_index("c")
        me, sibling = (x, y, c), (x, y, 1 - c)
        chips = [(1 - x, y), (x, 1 - y), (1 - x, 1 - y)]

        def slot(a, px, py, pc):
            return outs[a].at[4 * px + 2 * py + pc]

        def copy(a, k, block, to, src=None):
            return pltpu.make_async_remote_copy(
                src_ref=slot(a, *block) if src is None else src, dst_ref=slot(a, *block),
                send_sem=send_sems.at[a * per + k], recv_sem=recv_sems.at[a * per + k],
                device_id=to, device_id_type=pl.DeviceIdType.MESH)

        local, sent = [], []
        for a in range(n):
            mine = pltpu.make_async_copy(ins[a], slot(a, *me), loc_sems.at[a])
            mine.start()
            local.append(mine)
            first = [copy(a, 1 + j, me, (*chip, c), src=ins[a]) for j, chip in enumerate(chips)]
            first.append(copy(a, 0, me, sibling, src=ins[a]))
            for cp in first:
                cp.start()
            sent += first
        for j, chip in enumerate(chips):
            for a in range(n):
                copy(a, 1 + j, (*chip, c), me).wait_recv()
                cp = copy(a, 4 + j, (*chip, c), sibling)
                cp.start()
                sent.append(cp)
        for a in range(n):
            copy(a, 0, sibling, me).wait_recv()
            for j, chip in enumerate(chips):
                copy(a, 4 + j, (*chip, 1 - c), me).wait_recv()
        for cp in sent:
            cp.wait_send()
        for cp in local:
            cp.wait()

    any_spec = pl.BlockSpec(memory_space=pl.ANY)
    return _call(
        body, name=name, out_shape=tuple(jax.ShapeDtypeStruct((NDEV,) + a.shape, a.dtype) for a in arrs),
        in_specs=[any_spec] * n, out_specs=tuple([any_spec] * n),
        scratch_shapes=[pltpu.SemaphoreType.DMA((n * per,)), pltpu.SemaphoreType.DMA((n * per,)),
                        pltpu.SemaphoreType.DMA((n,))],
    )(*arrs)


def _swap_sibling(arrs, name):
    n = len(arrs)

    def body(*refs):
        ins, outs = refs[:n], refs[n:2 * n]
        send_sems, recv_sems = refs[2 * n:]
        x, y, c = lax.axis_index("x"), lax.axis_index("y"), lax.axis_index("c")
        copies = []
        for a in range(n):
            for j in range(NCHIP):
                cp = pltpu.make_async_remote_copy(
                    src_ref=ins[a].at[2 * j + (1 - c)], dst_ref=outs[a].at[j],
                    send_sem=send_sems.at[a * NCHIP + j], recv_sem=recv_sems.at[a * NCHIP + j],
                    device_id=(x, y, 1 - c), device_id_type=pl.DeviceIdType.MESH)
                cp.start()
                copies.append(cp)
        for cp in copies:
            cp.wait()

    any_spec = pl.BlockSpec(memory_space=pl.ANY)
    return _call(
        body, name=name, out_shape=tuple(jax.ShapeDtypeStruct((NCHIP,) + a.shape[1:], a.dtype) for a in arrs),
        in_specs=[any_spec] * n, out_specs=tuple([any_spec] * n),
        scratch_shapes=[pltpu.SemaphoreType.DMA((n * NCHIP,)), pltpu.SemaphoreType.DMA((n * NCHIP,))],
    )(*arrs)


def _pair_add(arr, got, name):
    _, r, n = arr.shape
    tr, tc = _shard_tile(r, n)
    arr4 = arr.reshape(NCHIP, 2, r, n)

    def body(a_ref, g_ref, p_ref, own_ref):
        c = lax.axis_index("c")
        my_chip = 2 * lax.axis_index("x") + lax.axis_index("y")
        s = jnp.where(c == 0, a_ref[0, 0], a_ref[0, 1]) + g_ref[0]
        p_ref[0] = s.astype(BF16)

        @pl.when(pl.program_id(2) == my_chip)
        def _():
            own_ref[...] = s

    return _call(
        body, name=name, grid=(r // tr, n // tc, NCHIP),
        in_specs=[pl.BlockSpec((1, 2, tr, tc), lambda i, k, j: (j, 0, i, k)),
                  pl.BlockSpec((1, tr, tc), lambda i, k, j: (j, i, k))],
        out_specs=(pl.BlockSpec((1, tr, tc), lambda i, k, j: (j, i, k)), pl.BlockSpec((tr, tc), lambda i, k, j: (i, k))),
        out_shape=(jax.ShapeDtypeStruct((NCHIP, r, n), BF16), jax.ShapeDtypeStruct((r, n), F32)),
        compiler_params=_cp(("parallel", "parallel", "arbitrary")),
    )(arr4, got)


def _shard_tile(r, n):
    return (128, n) if r % 128 == 0 else (r, 256)


def _scatter_chips(arrs, name):
    n = len(arrs)
    per = NCHIP - 1

    def body(*refs):
        ins, outs = refs[:n], refs[n:2 * n]
        send_sems, recv_sems, loc_sems = refs[2 * n:]
        x, y, c = lax.axis_index("x"), lax.axis_index("y"), lax.axis_index("c")
        copies = []
        for a in range(n):
            loc = pltpu.make_async_copy(ins[a].at[2 * x + y], outs[a].at[2 * x + y], loc_sems.at[a])
            loc.start()
            copies.append(loc)
            for k in range(1, NCHIP):
                px = 1 - x if (k >> 1) & 1 else x
                py = 1 - y if k & 1 else y
                cp = pltpu.make_async_remote_copy(
                    src_ref=ins[a].at[2 * px + py], dst_ref=outs[a].at[2 * x + y],
                    send_sem=send_sems.at[a * per + k - 1], recv_sem=recv_sems.at[a * per + k - 1],
                    device_id=(px, py, c), device_id_type=pl.DeviceIdType.MESH)
                cp.start()
                copies.append(cp)
        for cp in copies:
            cp.wait()

    any_spec = pl.BlockSpec(memory_space=pl.ANY)
    return _call(
        body, name=name, out_shape=tuple(jax.ShapeDtypeStruct(a.shape, a.dtype) for a in arrs),
        in_specs=[any_spec] * n, out_specs=tuple([any_spec] * n),
        scratch_shapes=[pltpu.SemaphoreType.DMA((n * per,)), pltpu.SemaphoreType.DMA((n * per,)),
                        pltpu.SemaphoreType.DMA((n,))],
    )(*arrs)


def _mm_tn(a, b, name):
    t, m = a.shape
    n = b.shape[1]
    tt = _pick(t, (1664, 640, 128))
    tm = _pick(m, (1024, 512, 128))
    tn = _pick(n, (1152, 1024, 512, 128))
    nt = t // tt

    def body(a_ref, b_ref, o_ref):
        s = pl.program_id(2)
        part = _dot_tn(a_ref[...], b_ref[...])

        @pl.when(s == 0)
        def _():
            o_ref[...] = part

        @pl.when(s > 0)
        def _():
            o_ref[...] += part

    return _call(
        body, name=name, grid=(m // tm, n // tn, nt),
        in_specs=[pl.BlockSpec((tt, tm), lambda i, j, s: (s, i)), pl.BlockSpec((tt, tn), lambda i, j, s: (s, j))],
        out_specs=pl.BlockSpec((tm, tn), lambda i, j, s: (i, j)),
        out_shape=jax.ShapeDtypeStruct((m, n), F32),
        compiler_params=_cp(("parallel", "parallel", "arbitrary")),
    )(a, b)


def _proj_fwd(x_ext, norm_w, w_main_t, w_ba_t):
    lp = x_ext.shape[0]
    n = w_main_t.shape[0]
    tm = _pick(lp, (832, 640, 320))
    tn = 1024

    def body(x_ref, nw_ref, w_ref, wba_ref, proj_ref, ba_ref, h_ref):
        @pl.when(pl.program_id(1) == 0)
        def _():
            x = x_ref[...]
            r = lax.rsqrt(jnp.mean(x * x, axis=-1, keepdims=True) + EPS)
            h = (x * r * nw_ref[...]).astype(BF16)
            h_ref[...] = h
            ba_ref[...] = _dot_nt(h, wba_ref[...])

        proj_ref[...] = _dot_nt(h_ref[...], w_ref[...])

    return _call(
        body, name="proj_fwd", grid=(lp // tm, n // tn),
        in_specs=[pl.BlockSpec((tm, D), lambda i, j: (i, 0)), pl.BlockSpec((1, D), lambda i, j: (0, 0)),
                  pl.BlockSpec((tn, D), lambda i, j: (j, 0)), pl.BlockSpec((LANE, D), lambda i, j: (0, 0))],
        out_specs=(pl.BlockSpec((tm, tn), lambda i, j: (i, j)), pl.BlockSpec((tm, LANE), lambda i, j: (i, 0)),
                   pl.BlockSpec((tm, D), lambda i, j: (i, 0))),
        out_shape=(jax.ShapeDtypeStruct((lp, n), F32), jax.ShapeDtypeStruct((lp, LANE), F32),
                   jax.ShapeDtypeStruct((lp, D), BF16)),
        compiler_params=_cp(("parallel", "arbitrary")),
    )(x_ext, norm_w, w_main_t, w_ba_t)


def _dh_mm(dproj, dba, w_main_t, w_ba_t):
    lp, n = dproj.shape
    tm = _pick(lp, (832, 640, 320))
    tn = 1024
    tk = 2304
    nk = n // tk

    def body(a_ref, ba_ref, b_ref, bba_ref, o_ref, acc):
        kk = pl.program_id(2)

        @pl.when(kk == 0)
        def _():
            acc[...] = jnp.dot(ba_ref[...], bba_ref[...], preferred_element_type=F32)

        acc[...] += jnp.dot(a_ref[...], b_ref[...], preferred_element_type=F32)

        @pl.when(kk == nk - 1)
        def _():
            o_ref[...] = acc[...]

    return _call(
        body, name="dh_mm", grid=(lp // tm, D // tn, nk),
        in_specs=[pl.BlockSpec((tm, tk), lambda i, j, kk: (i, kk)), pl.BlockSpec((tm, LANE), lambda i, j, kk: (i, 0)),
                  pl.BlockSpec((tk, tn), lambda i, j, kk: (kk, j)), pl.BlockSpec((LANE, tn), lambda i, j, kk: (0, j))],
        out_specs=pl.BlockSpec((tm, tn), lambda i, j, kk: (i, j)),
        out_shape=jax.ShapeDtypeStruct((lp, D), F32),
        scratch_shapes=[pltpu.VMEM((tm, tn), F32)],
        compiler_params=_cp(("parallel", "parallel", "arbitrary")),
    )(dproj, dba, w_main_t, w_ba_t)


def _beta_g(ba, ab, row0, pad):
    lane = lax.broadcasted_iota(jnp.int32, ba.shape, 1)
    rows = row0 + lax.broadcasted_iota(jnp.int32, ba.shape, 0)
    z = ba + ab[1:2, :]
    sp = jnp.maximum(z, 0.0) + jnp.log(1.0 + jnp.exp(-jnp.abs(z)))
    val = jnp.where(lane < H, _sig(ba), -jnp.exp(ab[0:1, :]) * sp)
    return jnp.where((lane < 2 * H) & (rows >= pad), val, 0.0)


def _qkv_conv_fwd(proj, ba, conv_w, ab, pad):
    lp = proj.shape[0]
    te = _pick(lp, (320,))
    hb = te // HALO_Q

    def body(main_ref, halo_ref, cw_ref, ba_ref, ab_ref, out_ref, bg_ref, pre_scr, tap_scr):
        i, s = pl.program_id(0), pl.program_id(1)
        pre_scr[:HALO_Q, :] = jnp.where(i > 0, halo_ref[...], 0.0)
        pre_scr[HALO_Q:, :] = main_ref[...]
        scale = jnp.where(s == 0, DK ** -0.5, 1.0)
        off = HALO_Q - (KQ - 1)

        def head(h, carry):
            cs = pl.ds(pl.multiple_of(h * DK, DK), DK)
            for j in range(KQ - 1):
                tap_scr[j] = pre_scr[off + j:off + j + te, cs]
            co = cw_ref[KQ - 1:KQ, cs] * pre_scr[HALO_Q:, cs]
            for j in range(KQ - 1):
                co = co + cw_ref[j:j + 1, cs] * tap_scr[j]
            a = co * _sig(co)
            r = lax.rsqrt(_rowsum(a * a) + EPS)
            out_ref[:, cs] = jnp.where(s == 2, a, a * (r * scale))
            return carry

        lax.fori_loop(0, H, head, 0, unroll=True)

        @pl.when(s == 0)
        def _():
            bg_ref[...] = _beta_g(ba_ref[...], ab_ref[...], i * te, pad)

    return _call(
        body, name="qkv_conv_fwd", grid=(lp // te, 3),
        in_specs=[pl.BlockSpec((te, D), lambda i, s: (i, s)),
                  pl.BlockSpec((HALO_Q, D), lambda i, s: (jnp.maximum(i * hb - 1, 0), s)),
                  pl.BlockSpec((KQ, D), lambda i, s: (0, s)),
                  pl.BlockSpec((te, LANE), lambda i, s: (i, 0)),
                  pl.BlockSpec((2, LANE), lambda i, s: (0, 0))],
        out_specs=(pl.BlockSpec((te, D), lambda i, s: (i, s)), pl.BlockSpec((te, LANE), lambda i, s: (i, 0))),
        out_shape=(jax.ShapeDtypeStruct((lp, 3 * D), F32), jax.ShapeDtypeStruct((lp, LANE), F32)),
        scratch_shapes=[pltpu.VMEM((te + HALO_Q, D), F32), pltpu.VMEM((KQ - 1, te, DK), F32)],
        compiler_params=_cp(("parallel", "arbitrary")),
    )(proj, proj, conv_w, ba, ab)


def _tri_masks():
    row = lax.broadcasted_iota(jnp.int32, (C, C), 0)
    col = lax.broadcasted_iota(jnp.int32, (C, C), 1)
    return row, col


def _split(a):
    hi = a.astype(BF16)
    return hi, (a - hi.astype(F32)).astype(BF16)


def _dot3(a, b, dims=(((1,), (0,)), ((), ()))):
    (ah, al), (bh, bl) = a, b
    mm = lambda x, y: lax.dot_general(x, y, dims, preferred_element_type=F32)
    return mm(ah, bh) + (mm(ah, bl) + mm(al, bh))


CHUNKS_PER_STEP = 2
CHUNKS_PER_STEP_BWD = 1
TINV_BLOCK = 16


def _tinv(ns, row, col):
    eye = (row == col).astype(F32)
    sh = TINV_BLOCK.bit_length() - 1
    same16 = (row >> sh) == (col >> sh)
    same32 = (row >> (sh + 1)) == (col >> (sh + 1))
    ys = [jnp.where(same16, -n, 0.0) for n in ns]
    ts = [eye + y for y in ys]
    sp = [_split(y) for y in ys]
    for level in range(3):
        yks = [_dot3(s, s) for s in sp]
        sp = [_split(yk) for yk in yks]
        ts = [t + _dot3(s, _split(t)) for s, t in zip(sp, ts)]
    for mask in (same32 & ~same16, ~same32):
        tsp = [_split(t) for t in ts]
        inner = [_dot3(_split(jnp.where(mask, n, 0.0)), t) for n, t in zip(ns, tsp)]
        ts = [t - _dot3(tp, _split(a)) for t, tp, a in zip(ts, tsp, inner)]
    return ts


def _chunk_common(q, k, v, bcol, gcc, gcr, incl, strict):
    dm = jnp.where(incl, jnp.exp(gcc - gcr), 0.0)
    kk = _dot_nt(k, k)
    qk = _dot_nt(q, k)
    gccw = jnp.broadcast_to(gcc, (C, DK))
    egc = jnp.exp(gccw)
    glast = gccw[C - 1:C, :]
    eend = jnp.exp(glast - gccw)
    elast = jnp.exp(glast)
    rhs = jnp.concatenate([v * bcol, k * (bcol * egc)], axis=1)
    return dm, kk, qk, egc, eend, elast, rhs


def _delta_fwd(qkv, bg):
    lp = qkv.shape[0]
    nc = lp // C
    heads = range(H)
    sls = [slice(h * DK, (h + 1) * DK) for h in heads]

    def body(q_ref, k_ref, v_ref, bg_ref, o_ref, sall_ref, tall_ref, s_scr):
        @pl.when(pl.program_id(0) == 0)
        def _():
            s_scr[...] = jnp.zeros_like(s_scr)

        row, col = _tri_masks()
        incl, strict = row >= col, row > col

        def prepare(sub):
            rs = slice(sub * C, (sub + 1) * C)
            bgt = bg_ref[rs, :]
            gc_all = _dot_hi(incl.astype(F32), bgt)
            gc_t = _dot_hi(bgt.T, (row <= col).astype(F32))
            qs, ks, vs = ([r[rs, sl] for sl in sls] for r in (q_ref, k_ref, v_ref))
            bcols = [jnp.broadcast_to(bgt[:, h:h + 1], (C, DK)) for h in heads]
            cm = [_chunk_common(qs[h], ks[h], vs[h], bcols[h], gc_all[:, H + h:H + h + 1], gc_t[H + h:H + h + 1, :],
                                incl, strict) for h in heads]
            dms, kks, qks, egcs, eends, elasts, rhss = zip(*cm)
            ts = _tinv([jnp.where(strict, bcols[h][:, :C] * kks[h] * dms[h], 0.0) for h in heads], row, col)
            sols = [_dot3(_split(ts[h]), _split(rhss[h])) for h in heads]
            qgs = [(qs[h] * egcs[h]).astype(BF16) for h in heads]
            ps = [(qks[h] * dms[h]).astype(BF16) for h in heads]
            kends = [(ks[h] * eends[h]).astype(BF16) for h in heads]
            return ts, sols, qgs, ps, kends, elasts

        prepared = [prepare(sub) for sub in range(CHUNKS_PER_STEP)]
        ss = [s_scr[h] for h in heads]
        for sub in range(CHUNKS_PER_STEP):
            rs = slice(sub * C, (sub + 1) * C)
            ts, sols, qgs, ps, kends, elasts = prepared[sub]
            sb = [s.astype(BF16) for s in ss]
            wvb = [(sols[h][:, :DK] - _dot(sols[h][:, DK:], sb[h])).astype(BF16) for h in heads]
            for h in heads:
                o_ref[rs, sls[h]] = _dot(qgs[h], sb[h]) + _dot(ps[h], wvb[h])
                sall_ref[sub, h] = ss[h]
                tall_ref[sub, h] = ts[h]
            ss = [ss[h] * elasts[h] + _dot_tn(kends[h], wvb[h]) for h in heads]
        for h in heads:
            s_scr[h] = ss[h]

    rows = CHUNKS_PER_STEP * C
    blk = lambda j: pl.BlockSpec((rows, D), lambda n: (n, j))
    return _call(
        body, name="delta_fwd", grid=(nc // CHUNKS_PER_STEP,),
        in_specs=[blk(0), blk(1), blk(2), pl.BlockSpec((rows, LANE), lambda n: (n, 0))],
        out_specs=(pl.BlockSpec((rows, D), lambda n: (n, 0)),
                   pl.BlockSpec((CHUNKS_PER_STEP, H, DK, DK), lambda n: (n, 0, 0, 0)),
                   pl.BlockSpec((CHUNKS_PER_STEP, H, C, C), lambda n: (n, 0, 0, 0))),
        out_shape=(jax.ShapeDtypeStruct((lp, D), F32), jax.ShapeDtypeStruct((nc, H, DK, DK), F32),
                   jax.ShapeDtypeStruct((nc, H, C, C), F32)),
        scratch_shapes=[pltpu.VMEM((H, DK, DK), F32)],
        compiler_params=_cp(("arbitrary",)),
    )(qkv, qkv, qkv, bg)


def _delta_bwd(qkv, bg, sall, tall, do):
    lp = qkv.shape[0]
    nc = lp // C

    heads = range(H)
    sls = [slice(h * DK, (h + 1) * DK) for h in heads]

    def body(q_ref, k_ref, v_ref, bg_ref, sall_ref, tall_ref, do_ref, dqkv_ref, dbg_ref, ds_scr):
        @pl.when(pl.program_id(0) == 0)
        def _():
            ds_scr[...] = jnp.zeros_like(ds_scr)

        dsns = [ds_scr[h] for h in heads]
        for sub in reversed(range(CHUNKS_PER_STEP_BWD)):
            dsns = chunk(sub, dsns, q_ref, k_ref, v_ref, bg_ref, sall_ref, tall_ref, do_ref, dqkv_ref, dbg_ref)
        for h in heads:
            ds_scr[h] = dsns[h]

    def chunk(sub, dsns, q_ref, k_ref, v_ref, bg_ref, sall_ref, tall_ref, do_ref, dqkv_ref, dbg_ref):
        rs = slice(sub * C, (sub + 1) * C)
        bgt = bg_ref[rs, :]
        row, col = _tri_masks()
        incl, strict = row >= col, row > col
        upper = (row <= col).astype(F32)
        gc_all = _dot_hi(incl.astype(F32), bgt)
        gc_t = _dot_hi(bgt.T, upper)
        lane = lax.broadcasted_iota(jnp.int32, (C, LANE), 1)
        lastrow = lax.broadcasted_iota(jnp.int32, (C, 1), 0) == C - 1
        qs, ks, vs, dos = ([r[rs, sl] for sl in sls] for r in (q_ref, k_ref, v_ref, do_ref))
        bcols = [jnp.broadcast_to(bgt[:, h:h + 1], (C, DK)) for h in heads]
        cm = [_chunk_common(qs[h], ks[h], vs[h], bcols[h], gc_all[:, H + h:H + h + 1], gc_t[H + h:H + h + 1, :],
                            incl, strict) for h in heads]
        dms, kks, qks, egcs, eends, elasts, rhss = zip(*cm)
        ss = [sall_ref[sub, h] for h in heads]
        ts = [tall_ref[sub, h] for h in heads]
        sb = [s.astype(BF16) for s in ss]
        dsb = [d.astype(BF16) for d in dsns]
        dob = [d.astype(BF16) for d in dos]
        sols = [_dot3(_split(ts[h]), _split(rhss[h])) for h in heads]
        ws = [sol[:, DK:] for sol in sols]
        qgs = [qs[h] * egcs[h] for h in heads]
        kends = [ks[h] * eends[h] for h in heads]
        wvs = [sols[h][:, :DK] - _dot(ws[h], sb[h]) for h in heads]
        wvb = [wv.astype(BF16) for wv in wvs]
        dwvs = [_dot_tn(qks[h] * dms[h], dob[h]) + _dot(kends[h], dsb[h]) for h in heads]
        dps = [jnp.where(incl, _dot_nt(dob[h], wvb[h]), 0.0) for h in heads]
        dqgs = [_dot_nt(dob[h], sb[h]) for h in heads]
        dkends = [_dot_nt(wvb[h], dsb[h]) for h in heads]
        ds_before = [_dot_tn(qgs[h], dob[h]) + elasts[h] * dsns[h] - _dot_tn(ws[h], dwvs[h]) for h in heads]
        dglasts = [elasts[h] * jnp.sum(ss[h] * dsns[h], keepdims=True) for h in heads]
        dws = [-_dot_nt(dwvs[h], sb[h]) for h in heads]
        tts = [_split(ts[h].T) for h in heads]
        drhss = [_dot3(tts[h], _split(jnp.concatenate([dwvs[h], dws[h]], axis=1))) for h in heads]
        nt_dims = (((1,), (1,)), ((), ()))
        dns = [jnp.where(strict, -_dot3(_split(drhss[h]), _split(sols[h]), nt_dims), 0.0) for h in heads]
        dbeta_t = jnp.zeros((C, LANE), F32)
        dgc_t = jnp.zeros((C, LANE), F32)
        for h in heads:
            q, k, v, bcol, dm, kk, qk, egc, eend = qs[h], ks[h], vs[h], bcols[h], dms[h], kks[h], qks[h], egcs[h], eends[h]
            drv, drk = drhss[h][:, :DK], drhss[h][:, DK:]
            dn, dp, dqg, dkend = dns[h], dps[h], dqgs[h], dkends[h]
            rk = _rowsum(drk * k)
            dkk = dn * (bcol[:, :C] * dm)
            dqk = dp * dm
            e = (dn * (bcol[:, :C] * kk) + dp * qk) * dm
            tk = _rowsum(dkend * kends[h])
            dgc = rk * bcol * egc + _rowsum(e) - _rowsum(e.T) + _rowsum(dqg * qgs[h]) - tk
            dgc = dgc + jnp.where(lastrow, dglasts[h] + jnp.sum(tk, keepdims=True), 0.0)
            dbeta = _rowsum(drv * v) + rk * egc + _rowsum(dn * kk * dm)
            dqkv_ref[rs, sls[h]] = _dot(dqk, k) + dqg * egc
            dqkv_ref[rs, D + h * DK:D + (h + 1) * DK] = (drk * (bcol * egc) + _dot(dkk, k) + _dot_tn(dkk, k)
                                                        + _dot_tn(dqk, q) + dkend * eend)
            dqkv_ref[rs, 2 * D + h * DK:2 * D + (h + 1) * DK] = bcol * drv
            dbeta_t = jnp.where(lane == h, dbeta, dbeta_t)
            dgc_t = jnp.where(lane == H + h, dgc, dgc_t)
        dbg_ref[rs, :] = dbeta_t + _dot_hi(upper, dgc_t)
        return ds_before

    steps = nc // CHUNKS_PER_STEP_BWD
    rows = CHUNKS_PER_STEP_BWD * C
    rev = lambda n: steps - 1 - n
    blk = lambda j: pl.BlockSpec((rows, D), lambda n: (rev(n), j))
    return _call(
        body, name="delta_bwd", grid=(steps,),
        in_specs=[blk(0), blk(1), blk(2), pl.BlockSpec((rows, LANE), lambda n: (rev(n), 0)),
                  pl.BlockSpec((CHUNKS_PER_STEP_BWD, H, DK, DK), lambda n: (rev(n), 0, 0, 0)),
                  pl.BlockSpec((CHUNKS_PER_STEP_BWD, H, C, C), lambda n: (rev(n), 0, 0, 0)),
                  pl.BlockSpec((rows, D), lambda n: (rev(n), 0))],
        out_specs=(pl.BlockSpec((rows, 3 * D), lambda n: (rev(n), 0)),
                   pl.BlockSpec((rows, LANE), lambda n: (rev(n), 0))),
        out_shape=(jax.ShapeDtypeStruct((lp, 3 * D), F32), jax.ShapeDtypeStruct((lp, LANE), F32)),
        scratch_shapes=[pltpu.VMEM((H, DK, DK), F32)],
        compiler_params=_cp(("arbitrary",)),
    )(qkv, qkv, qkv, bg, sall, tall, do)


def _o_post_fwd(o, proj, dn_w, w_dn):
    lp = o.shape[0]
    te = _pick(lp, (640, 320))

    def body(o_ref, za_ref, w_ref, wdn_ref, out_ref, ya_ref):
        za = za_ref[...]
        gate = za * _sig(za)
        for h in range(H):
            sl = slice(h * DK, (h + 1) * DK)
            oh = o_ref[:, sl]
            r = lax.rsqrt(jnp.mean(oh * oh, axis=-1, keepdims=True) + EPS)
            out_ref[:, sl] = (oh * r * w_ref[...] * gate[:, sl]).astype(BF16)
        ya_ref[...] = _dot(out_ref[...], wdn_ref[...])

    row = pl.BlockSpec((te, D), lambda i: (i, 0))
    return _call(
        body, name="o_post_fwd", grid=(lp // te,),
        in_specs=[row, pl.BlockSpec((te, D), lambda i: (i, CB_ZA)), pl.BlockSpec((1, DK), lambda i: (0, 0)),
                  pl.BlockSpec((D, D), lambda i: (0, 0))],
        out_specs=(row, row),
        out_shape=(jax.ShapeDtypeStruct((lp, D), BF16), jax.ShapeDtypeStruct((lp, D), F32)),
        compiler_params=_cp(("parallel",)),
    )(o, proj, dn_w, w_dn)


def _o_post_bwd(dy_a, w_dn, o, proj, dn_w, dproj):
    lp = o.shape[0]
    te = _pick(lp, (320,))

    def body(dya_ref, wdn_ref, o_ref, za_ref, w_ref, _, do_ref, dza_ref, dw_ref, don_ref):
        @pl.when(pl.program_id(0) == 0)
        def _():
            dw_ref[...] = jnp.zeros_like(dw_ref)

        don_ref[...] = _dot_nt(dya_ref[...], wdn_ref[...])
        za = za_ref[...]
        sz = _sig(za)
        gate, dgate = za * sz, _dsilu(za, sz)
        w = w_ref[...]
        dw = jnp.zeros((1, DK), F32)
        for h in range(H):
            sl = slice(h * DK, (h + 1) * DK)
            oh, g = o_ref[:, sl], don_ref[:, sl]
            r = lax.rsqrt(jnp.mean(oh * oh, axis=-1, keepdims=True) + EPS)
            ohat = oh * r
            dza_ref[:, sl] = (g * ohat * w * dgate[:, sl]).astype(BF16)
            don = g * gate[:, sl]
            dw = dw + _colsum(don * ohat)
            dohat = don * w
            do_ref[:, sl] = r * (dohat - ohat * jnp.mean(dohat * ohat, axis=-1, keepdims=True))
        dw_ref[...] += dw

    return _call(
        body, name="o_post_bwd", grid=(lp // te,),
        in_specs=[pl.BlockSpec((te, D), lambda i: (i, 0)), pl.BlockSpec((D, D), lambda i: (0, 0)),
                  pl.BlockSpec((te, D), lambda i: (i, 0)),
                  pl.BlockSpec((te, D), lambda i: (i, CB_ZA)), pl.BlockSpec((1, DK), lambda i: (0, 0)),
                  pl.BlockSpec(memory_space=pl.ANY)],
        out_specs=(pl.BlockSpec((te, D), lambda i: (i, 0)), pl.BlockSpec((te, D), lambda i: (i, CB_ZA)),
                   pl.BlockSpec((1, DK), lambda i: (0, 0))),
        out_shape=(jax.ShapeDtypeStruct((lp, D), F32), jax.ShapeDtypeStruct(dproj.shape, dproj.dtype),
                   jax.ShapeDtypeStruct((1, DK), F32)),
        input_output_aliases={5: 1},
        scratch_shapes=[pltpu.VMEM((te, D), F32)],
        compiler_params=_cp(("arbitrary",)),
    )(dy_a, w_dn, o, proj, dn_w, dproj)


def _qkv_conv_bwd(proj, dqkv, conv_w, dproj):
    lp = proj.shape[0]
    te = _pick(lp, (320,))
    hb = te // HALO_Q
    nt = lp // te
    last_hb = lp // HALO_Q - 1

    def body(main_ref, prev_ref, next_ref, dmain_ref, dnext_ref, cw_ref, _, dpre_ref, dcw_ref, pre_scr, dn_scr,
             tap_scr, dco_scr, dsh_scr):
        s, i = pl.program_id(0), pl.program_id(1)

        @pl.when(i == 0)
        def _():
            dcw_ref[...] = jnp.zeros_like(dcw_ref)

        ne = te + HALO_Q
        pre_scr[:HALO_Q, :] = jnp.where(i > 0, prev_ref[...], 0.0)
        pre_scr[HALO_Q:ne, :] = main_ref[...]
        pre_scr[ne:, :] = jnp.where(i < nt - 1, next_ref[...], 0.0)
        dn_scr[:te, :] = dmain_ref[...]
        dn_scr[te:, :] = jnp.where(i < nt - 1, dnext_ref[...], 0.0)
        scale = jnp.where(s == 0, DK ** -0.5, 1.0)
        off = HALO_Q - (KQ - 1)

        def head(h, carry):
            cs = pl.ds(pl.multiple_of(h * DK, DK), DK)
            for j in range(KQ - 1):
                tap_scr[j] = pre_scr[off + j:off + j + ne, cs]
            taps = [tap_scr[j] for j in range(KQ - 1)] + [pre_scr[HALO_Q:, cs]]
            co = cw_ref[0:1, cs] * taps[0]
            for j in range(1, KQ):
                co = co + cw_ref[j:j + 1, cs] * taps[j]
            sg = _sig(co)
            a = co * sg
            g = dn_scr[:, cs]
            r = lax.rsqrt(_rowsum(a * a) + EPS)
            yhat = a * r
            da = jnp.where(s == 2, g, (scale * r) * (g - yhat * _rowsum(g * yhat)))
            dco = da * _dsilu(co, sg)
            dco_scr[...] = dco
            for j in range(KQ - 1):
                dsh_scr[j] = dco_scr[KQ - 1 - j:KQ - 1 - j + te, :]
            dpre = cw_ref[KQ - 1:KQ, cs] * dco[:te, :]
            for j in range(KQ - 1):
                dpre = dpre + cw_ref[j:j + 1, cs] * dsh_scr[j]
            dpre_ref[:, cs] = dpre.astype(BF16)
            dcw_ref[:, cs] += jnp.concatenate([_colsum(dco[:te] * taps[j][:te]) for j in range(KQ)], axis=0)
            return carry

        lax.fori_loop(0, H, head, 0, unroll=True)

    return _call(
        body, name="qkv_conv_bwd", grid=(3, nt),
        in_specs=[pl.BlockSpec((te, D), lambda s, i: (i, s)),
                  pl.BlockSpec((HALO_Q, D), lambda s, i: (jnp.maximum(i * hb - 1, 0), s)),
                  pl.BlockSpec((HALO_Q, D), lambda s, i: (jnp.minimum((i + 1) * hb, last_hb), s)),
                  pl.BlockSpec((te, D), lambda s, i: (i, s)),
                  pl.BlockSpec((HALO_Q, D), lambda s, i: (jnp.minimum((i + 1) * hb, last_hb), s)),
                  pl.BlockSpec((KQ, D), lambda s, i: (0, s)),
                  pl.BlockSpec(memory_space=pl.ANY)],
        out_specs=(pl.BlockSpec((te, D), lambda s, i: (i, s)), pl.BlockSpec((KQ, D), lambda s, i: (0, s))),
        out_shape=(jax.ShapeDtypeStruct(dproj.shape, dproj.dtype), jax.ShapeDtypeStruct((KQ, 3 * D), F32)),
        input_output_aliases={6: 0},
        scratch_shapes=[pltpu.VMEM((te + 2 * HALO_Q, D), F32), pltpu.VMEM((te + HALO_Q, D), F32),
                        pltpu.VMEM((KQ - 1, te + HALO_Q, DK), F32), pltpu.VMEM((te + HALO_Q, DK), F32),
                        pltpu.VMEM((KQ - 1, te, DK), F32)],
        compiler_params=_cp(("arbitrary", "arbitrary")),
    )(proj, proj, proj, dqkv, dqkv, conv_w, dproj)


def _ba_bwd(dbg, ba, ab, pad):
    lp = ba.shape[0]
    te = _pick(lp, (640, 320))

    def body(dbg_ref, ba_ref, ab_ref, dba_ref, dab_ref):
        i = pl.program_id(0)

        @pl.when(i == 0)
        def _():
            dab_ref[...] = jnp.zeros_like(dab_ref)

        ba, ab = ba_ref[...], ab_ref[...]
        lane = lax.broadcasted_iota(jnp.int32, ba.shape, 1)
        rows = i * te + lax.broadcasted_iota(jnp.int32, ba.shape, 0)
        g = jnp.where((lane < 2 * H) & (rows >= pad), dbg_ref[...], 0.0)
        sb = _sig(ba)
        z = ba + ab[1:2, :]
        sp = jnp.maximum(z, 0.0) + jnp.log(1.0 + jnp.exp(-jnp.abs(z)))
        nea = -jnp.exp(ab[0:1, :])
        dz = g * nea * _sig(z)
        dba_ref[...] = jnp.where(lane < H, g * sb * (1.0 - sb), dz).astype(BF16)
        is_g = (lane >= H) & (lane < 2 * H)
        dab_ref[...] += jnp.concatenate([_colsum(jnp.where(is_g, g * nea * sp, 0.0)),
                                         _colsum(jnp.where(is_g, dz, 0.0))], axis=0)

    return _call(
        body, name="ba_bwd", grid=(lp // te,),
        in_specs=[pl.BlockSpec((te, LANE), lambda i: (i, 0)), pl.BlockSpec((te, LANE), lambda i: (i, 0)),
                  pl.BlockSpec((2, LANE), lambda i: (0, 0))],
        out_specs=(pl.BlockSpec((te, LANE), lambda i: (i, 0)), pl.BlockSpec((2, LANE), lambda i: (0, 0))),
        out_shape=(jax.ShapeDtypeStruct((lp, LANE), BF16), jax.ShapeDtypeStruct((2, LANE), F32)),
        compiler_params=_cp(("arbitrary",)),
    )(dbg, ba, ab)


SUBLANES = 8
CONV_RB = 64


def _fill_shifted(sh_scr, src_scr, cs):
    n = sh_scr.shape[1]
    for s in range(1, SUBLANES):
        sh_scr[s] = src_scr[s:s + n, cs]


def _shifted(sh_scr, src_scr, cs, r, r0, n):
    s, a8 = r % SUBLANES, r - r % SUBLANES
    if s == 0:
        return src_scr[r0 + a8:r0 + a8 + n, cs]
    return sh_scr[s, r0 + a8:r0 + a8 + n, :]


def _conv_b_fwd(proj, dw_w, dw_b, ln_w, ln_b, w_cf):
    lp = proj.shape[0]
    te = _pick(lp, (320,))
    hb = te // HALO_D

    def body(a_ref, b_ref, ha_ref, hb_ref, zb_ref, w_ref, wb_ref, lw_ref, lb_ref, wcf_ref, c1_ref, c3_ref, yb_ref,
             c0_scr, sh_scr):
        i = pl.program_id(0)
        c0_scr[:HALO_D, :] = jnp.where(i > 0, ha_ref[...] * _sig(hb_ref[...]), 0.0)
        c0_scr[HALO_D:, :] = a_ref[...] * _sig(b_ref[...])
        off = HALO_D - (KD - 1)
        def lane_block(cb, carry):
            cs = pl.ds(pl.multiple_of(cb * LANE, LANE), LANE)
            _fill_shifted(sh_scr, c0_scr, cs)
            for r0 in range(0, te, CONV_RB):
                acc = None
                for j in range(KD):
                    term = w_ref[j:j + 1, cs] * _shifted(sh_scr, c0_scr, cs, off + j, r0, CONV_RB)
                    acc = term if acc is None else acc + term
                c1_ref[r0:r0 + CONV_RB, cs] = acc + wb_ref[:, cs]
            return carry

        lax.fori_loop(0, D // LANE, lane_block, 0)
        c1 = c1_ref[...]
        mu = jnp.mean(c1, axis=-1, keepdims=True)
        xc = c1 - mu
        c2 = xc * lax.rsqrt(jnp.mean(xc * xc, axis=-1, keepdims=True) + EPS) * lw_ref[...] + lb_ref[...]
        zb = zb_ref[...]
        c3 = (c2 * _sig(c2) * zb * _sig(zb)).astype(BF16)
        c3_ref[...] = c3
        yb_ref[...] = _dot(c3, wcf_ref[...])

    vec = pl.BlockSpec((1, D), lambda i: (0, 0))
    row = pl.BlockSpec((te, D), lambda i: (i, 0))
    return _call(
        body, name="conv_b_fwd", grid=(lp // te,),
        in_specs=[pl.BlockSpec((te, D), lambda i: (i, CB_GA_)), pl.BlockSpec((te, D), lambda i: (i, CB_GB_)),
                  pl.BlockSpec((HALO_D, D), lambda i: (jnp.maximum(i * hb - 1, 0), CB_GA_)),
                  pl.BlockSpec((HALO_D, D), lambda i: (jnp.maximum(i * hb - 1, 0), CB_GB_)),
                  pl.BlockSpec((te, D), lambda i: (i, CB_ZB)),
                  pl.BlockSpec((KD, D), lambda i: (0, 0)), vec, vec, vec, pl.BlockSpec((D, D), lambda i: (0, 0))],
        out_specs=(row, row, row),
        out_shape=(jax.ShapeDtypeStruct((lp, D), F32), jax.ShapeDtypeStruct((lp, D), BF16),
                   jax.ShapeDtypeStruct((lp, D), F32)),
        scratch_shapes=[pltpu.VMEM((te + HALO_D, D), F32), pltpu.VMEM((SUBLANES, te + HALO_D - SUBLANES, LANE), F32)],
        compiler_params=_cp(("parallel",)),
    )(proj, proj, proj, proj, proj, dw_w, dw_b, ln_w, ln_b, w_cf)


def _conv_b_bwd1(dy_b, w_cf, c1, proj, ln_w, ln_b, dproj):
    lp = c1.shape[0]
    te = _pick(lp, (320,))

    def body(dyb_ref, wcf_ref, c1_ref, zb_ref, lw_ref, lb_ref, _, dc1_ref, dzb_ref, sums_ref):
        @pl.when(pl.program_id(0) == 0)
        def _():
            sums_ref[...] = jnp.zeros_like(sums_ref)

        c1, g = c1_ref[...], _dot_nt(dyb_ref[...], wcf_ref[...])
        mu = jnp.mean(c1, axis=-1, keepdims=True)
        xc = c1 - mu
        rstd = lax.rsqrt(jnp.mean(xc * xc, axis=-1, keepdims=True) + EPS)
        xh = xc * rstd
        lw = lw_ref[...]
        c2 = xh * lw + lb_ref[...]
        s2 = _sig(c2)
        zb = zb_ref[...]
        sz = _sig(zb)
        dc2 = g * (zb * sz) * _dsilu(c2, s2)
        dzb_ref[...] = (g * (c2 * s2) * _dsilu(zb, sz)).astype(BF16)
        dxh = dc2 * lw
        dc1 = rstd * (dxh - jnp.mean(dxh, axis=-1, keepdims=True) - xh * jnp.mean(dxh * xh, axis=-1, keepdims=True))
        dc1_ref[...] = dc1
        sums_ref[...] += jnp.concatenate([_colsum(dc2 * xh), _colsum(dc2), _colsum(dc1)], axis=0)

    vec = pl.BlockSpec((1, D), lambda i: (0, 0))
    return _call(
        body, name="conv_b_bwd1", grid=(lp // te,),
        in_specs=[pl.BlockSpec((te, D), lambda i: (i, 0)), pl.BlockSpec((D, D), lambda i: (0, 0)),
                  pl.BlockSpec((te, D), lambda i: (i, 0)),
                  pl.BlockSpec((te, D), lambda i: (i, CB_ZB)), vec, vec, pl.BlockSpec(memory_space=pl.ANY)],
        out_specs=(pl.BlockSpec((te, D), lambda i: (i, 0)), pl.BlockSpec((te, D), lambda i: (i, CB_ZB)),
                   pl.BlockSpec((3, D), lambda i: (0, 0))),
        out_shape=(jax.ShapeDtypeStruct((lp, D), F32), jax.ShapeDtypeStruct(dproj.shape, dproj.dtype),
                   jax.ShapeDtypeStruct((3, D), F32)),
        input_output_aliases={6: 1},
        compiler_params=_cp(("arbitrary",)),
    )(dy_b, w_cf, c1, proj, ln_w, ln_b, dproj)


def _conv_b_bwd2(dc1, proj, dw_w, dproj):
    lp = dc1.shape[0]
    te = _pick(lp, (320,))
    hb = te // HALO_D
    nt = lp // te
    last_hb = lp // HALO_D - 1

    def body(g_ref, gn_ref, a_ref, b_ref, ha_ref, hb_ref, w_ref, _, dab_ref, dw_ref, c0_scr, g_scr, dc0_scr,
             csh_scr, gsh_scr):
        i = pl.program_id(0)

        @pl.when(i == 0)
        def _():
            dw_ref[...] = jnp.zeros_like(dw_ref)

        a, b = a_ref[...], b_ref[...]
        sb = _sig(b)
        c0_scr[:HALO_D, :] = jnp.where(i > 0, ha_ref[...] * _sig(hb_ref[...]), 0.0)
        c0_scr[HALO_D:, :] = a * sb
        g_scr[:te, :] = g_ref[...]
        g_scr[te:, :] = jnp.where(i < nt - 1, gn_ref[...], 0.0)
        off = HALO_D - (KD - 1)
        def lane_block(cb, carry):
            cs = pl.ds(pl.multiple_of(cb * LANE, LANE), LANE)
            _fill_shifted(csh_scr, c0_scr, cs)
            _fill_shifted(gsh_scr, g_scr, cs)
            for r0 in range(0, te, CONV_RB):
                acc = None
                for j in range(KD):
                    term = w_ref[j:j + 1, cs] * _shifted(gsh_scr, g_scr, cs, KD - 1 - j, r0, CONV_RB)
                    acc = term if acc is None else acc + term
                dc0_scr[r0:r0 + CONV_RB, cs] = acc
            parts = [None] * KD
            for r0 in range(0, te, CONV_RB):
                g = g_scr[r0:r0 + CONV_RB, cs].reshape(CONV_RB // SUBLANES, SUBLANES, LANE)
                for j in range(KD):
                    x = _shifted(csh_scr, c0_scr, cs, off + j, r0, CONV_RB)
                    p = jnp.sum(g * x.reshape(CONV_RB // SUBLANES, SUBLANES, LANE), axis=0)
                    parts[j] = p if parts[j] is None else parts[j] + p
            dw_ref[:, cs] += jnp.concatenate([_colsum(p) for p in parts], axis=0)
            return carry

        lax.fori_loop(0, D // LANE, lane_block, 0)
        dc0 = dc0_scr[...]
        dab_ref[:, :D] = (dc0 * sb).astype(BF16)
        dab_ref[:, D:] = (dc0 * a * sb * (1.0 - sb)).astype(BF16)

    return _call(
        body, name="conv_b_bwd2", grid=(nt,),
        in_specs=[pl.BlockSpec((te, D), lambda i: (i, 0)),
                  pl.BlockSpec((HALO_D, D), lambda i: (jnp.minimum((i + 1) * hb, last_hb), 0)),
                  pl.BlockSpec((te, D), lambda i: (i, CB_GA_)), pl.BlockSpec((te, D), lambda i: (i, CB_GB_)),
                  pl.BlockSpec((HALO_D, D), lambda i: (jnp.maximum(i * hb - 1, 0), CB_GA_)),
                  pl.BlockSpec((HALO_D, D), lambda i: (jnp.maximum(i * hb - 1, 0), CB_GB_)),
                  pl.BlockSpec((KD, D), lambda i: (0, 0)), pl.BlockSpec(memory_space=pl.ANY)],
        out_specs=(pl.BlockSpec((te, 2 * D), lambda i: (i, CB_GA_ // 2)), pl.BlockSpec((KD, D), lambda i: (0, 0))),
        out_shape=(jax.ShapeDtypeStruct(dproj.shape, dproj.dtype), jax.ShapeDtypeStruct((KD, D), F32)),
        input_output_aliases={7: 0},
        scratch_shapes=[pltpu.VMEM((te + HALO_D, D), F32), pltpu.VMEM((te + HALO_D, D), F32), pltpu.VMEM((te, D), F32),
                        pltpu.VMEM((SUBLANES, te + HALO_D - SUBLANES, LANE), F32),
                        pltpu.VMEM((SUBLANES, te + HALO_D - SUBLANES, LANE), F32)],
        compiler_params=_cp(("arbitrary",)),
    )(dc1, dc1, proj, proj, proj, proj, dw_w, dproj)


def _merge_fwd(y_a, y_b, proj, b_cf, w_o):
    lp = y_a.shape[0]
    te = _pick(lp, (320,))

    def body(ya_ref, yb_ref, ga_ref, gb_ref, bias_ref, wo_ref, out_ref, z_ref):
        merged = (_sig(ga_ref[...]) * ya_ref[...] + _sig(gb_ref[...]) * (yb_ref[...] + bias_ref[...])).astype(BF16)
        out_ref[...] = merged
        z_ref[...] = _dot(merged, wo_ref[...])

    row = lambda j: pl.BlockSpec((te, D), lambda i: (i, j))
    return _call(
        body, name="merge_fwd", grid=(lp // te,),
        in_specs=[row(0), row(0), row(CB_MA), row(CB_MB), pl.BlockSpec((1, D), lambda i: (0, 0)),
                  pl.BlockSpec((D, D), lambda i: (0, 0))],
        out_specs=(row(0), row(0)),
        out_shape=(jax.ShapeDtypeStruct((lp, D), BF16), jax.ShapeDtypeStruct((lp, D), F32)),
        compiler_params=_cp(("parallel",)),
    )(y_a, y_b, proj, proj, b_cf, w_o)


def _merge_bwd(dx_out_b, w_o, y_a, y_b, proj, b_cf):
    lp = y_a.shape[0]
    te = _pick(lp, (320,))

    def body(dx_ref, wo_ref, ya_ref, yb_ref, ga_ref, gb_ref, bias_ref, dya_ref, dyb_ref, dg_ref, db_ref):
        @pl.when(pl.program_id(0) == 0)
        def _():
            db_ref[...] = jnp.zeros_like(db_ref)

        dm = _dot_nt(dx_ref[...], wo_ref[...])
        sa, sb = _sig(ga_ref[...]), _sig(gb_ref[...])
        dyb = sb * dm
        dya_ref[...] = (sa * dm).astype(BF16)
        dyb_ref[...] = dyb.astype(BF16)
        dg_ref[:, :D] = (dm * ya_ref[...] * sa * (1.0 - sa)).astype(BF16)
        dg_ref[:, D:] = (dm * (yb_ref[...] + bias_ref[...]) * sb * (1.0 - sb)).astype(BF16)
        db_ref[...] += _colsum(dyb)

    row = lambda j: pl.BlockSpec((te, D), lambda i: (i, j))
    act = jax.ShapeDtypeStruct((lp, D), BF16)
    return _call(
        body, name="merge_bwd", grid=(lp // te,),
        in_specs=[row(0), pl.BlockSpec((D, D), lambda i: (0, 0)), row(0), row(0), row(CB_MA), row(CB_MB),
                  pl.BlockSpec((1, D), lambda i: (0, 0))],
        out_specs=(row(0), row(0), pl.BlockSpec((te, 2 * D), lambda i: (i, CB_MA // 2)),
                   pl.BlockSpec((1, D), lambda i: (0, 0))),
        out_shape=(act, act, jax.ShapeDtypeStruct((lp, NCB * D), BF16), jax.ShapeDtypeStruct((1, D), F32)),
        compiler_params=_cp(("arbitrary",)),
    )(dx_out_b, w_o, y_a, y_b, proj, proj, b_cf)


def _final_fwd_bwd(x_ext, z, target, final_w):
    lp = x_ext.shape[0]
    te = _pick(lp, (640,))
    nsub = te // LANE

    def body(x_ref, z_ref, *rest):
        t_refs, (w_ref, dx_ref, dxb_ref, loss_ref, dw_ref) = rest[:nsub], rest[nsub:]
        i = pl.program_id(0)

        @pl.when(i == 0)
        def _():
            loss_ref[...] = jnp.zeros_like(loss_ref)
            dw_ref[...] = jnp.zeros_like(dw_ref)

        w = w_ref[...]
        for k in range(nsub):
            rs = slice(k * LANE, (k + 1) * LANE)
            xo = x_ref[rs, :] + z_ref[rs, :]
            r = lax.rsqrt(jnp.mean(xo * xo, axis=-1, keepdims=True) + EPS)
            xhat = xo * r
            err = xhat * w - t_refs[k][...]
            if k == 0:
                err = jnp.where(i > 0, err, 0.0)
            loss_ref[...] += 0.5 * jnp.sum(jnp.mean(err * err, axis=-1, keepdims=True), keepdims=True)
            dy = err * (1.0 / D)
            dw_ref[...] += _colsum(dy * xhat)
            dxn = dy * w
            dx = r * (dxn - xhat * jnp.mean(dxn * xhat, axis=-1, keepdims=True))
            dx_ref[rs, :] = dx
            dxb_ref[rs, :] = dx.astype(BF16)

    piece = lambda k: pl.BlockSpec((LANE, D), lambda i: (jnp.maximum(i * nsub + k - 1, 0), 0))
    row = pl.BlockSpec((te, D), lambda i: (i, 0))
    return _call(
        body, name="final_fwd_bwd", grid=(lp // te,),
        in_specs=[row, row] + [piece(k) for k in range(nsub)] + [pl.BlockSpec((1, D), lambda i: (0, 0))],
        out_specs=(row, row, pl.BlockSpec((1, 1), lambda i: (0, 0)), pl.BlockSpec((1, D), lambda i: (0, 0))),
        out_shape=(jax.ShapeDtypeStruct((lp, D), F32), jax.ShapeDtypeStruct((lp, D), BF16),
                   jax.ShapeDtypeStruct((1, 1), F32), jax.ShapeDtypeStruct((1, D), F32)),
        compiler_params=_cp(("arbitrary",)),
    )(x_ext, z, *([target] * nsub), final_w)


def _prenorm_bwd(dh, x_ext, dx_out, norm_w, seq):
    lp = x_ext.shape[0]
    te = LANE

    def body(dh_ref, x_ref, dxo_ref, w_ref, gx_ref, head_ref, dw_ref):
        i = pl.program_id(0)

        @pl.when(i == 0)
        def _():
            dw_ref[...] = jnp.zeros_like(dw_ref)

        x, dh = x_ref[...], dh_ref[...]
        r = lax.rsqrt(jnp.mean(x * x, axis=-1, keepdims=True) + EPS)
        xhat = x * r
        dxn = dh * w_ref[...]
        dx = dxo_ref[...] + r * (dxn - xhat * jnp.mean(dxn * xhat, axis=-1, keepdims=True))
        dw_ref[...] += _colsum(dh * xhat)

        @pl.when(i == 0)
        def _():
            head_ref[...] = dx

        @pl.when(i > 0)
        def _():
            gx_ref[...] = dx

    row = pl.BlockSpec((te, D), lambda i: (i, 0))
    return _call(
        body, name="prenorm_bwd", grid=(lp // te,),
        in_specs=[row, row, row, pl.BlockSpec((1, D), lambda i: (0, 0))],
        out_specs=(pl.BlockSpec((te, D), lambda i: (jnp.maximum(i - 1, 0), 0)), pl.BlockSpec((te, D), lambda i: (0, 0)),
                   pl.BlockSpec((1, D), lambda i: (0, 0))),
        out_shape=(jax.ShapeDtypeStruct((seq, D), F32), jax.ShapeDtypeStruct((te, D), F32),
                   jax.ShapeDtypeStruct((1, D), F32)),
        compiler_params=_cp(("arbitrary",)),
    )(dh, x_ext, dx_out, norm_w)


def _adam_reduce(parts, w, m, v, name):
    r, n = w.shape
    tr = _pick(r, (128,)) if r % 128 == 0 else r

    def body(p_ref, w_ref, m_ref, v_ref, g_ref, d_ref, m2_ref, v2_ref):
        g = p_ref[0]
        for s in range(1, NDEV):
            g = g + p_ref[s]
        _adam_write(g, w_ref, m_ref, v_ref, g_ref, d_ref, m2_ref, v2_ref)

    blk = pl.BlockSpec((tr, n), lambda i: (i, 0))
    out = jax.ShapeDtypeStruct((r, n), F32)
    return _call(
        body, name=name, grid=(r // tr,),
        in_specs=[pl.BlockSpec((NDEV, tr, n), lambda i: (0, i, 0)), blk, blk, blk],
        out_specs=(blk, blk, blk, blk), out_shape=(out, out, out, out),
        compiler_params=_cp(("parallel",)),
    )(parts, w, m, v)


def _adam_write(g, w_ref, m_ref, v_ref, g_ref, d_ref, m2_ref, v2_ref):
    c1 = 1.0 - ADAM_B1 ** ADAM_STEP
    c2 = 1.0 - ADAM_B2 ** ADAM_STEP
    m2 = ADAM_B1 * m_ref[...] + (1.0 - ADAM_B1) * g
    v2 = ADAM_B2 * v_ref[...] + (1.0 - ADAM_B2) * (g * g)
    g_ref[...] = g
    m2_ref[...] = m2
    v2_ref[...] = v2
    d_ref[...] = -ADAM_LR * ((m2 / c1) / (jnp.sqrt(v2 / c2) + ADAM_EPS) + ADAM_WD * w_ref[...])


def _adam_chips(own, recv, w, m, v, name):
    r, n = w.shape
    tr, tc = _shard_tile(r, n)

    def body(own_ref, p_ref, w_ref, m_ref, v_ref, g_ref, d_ref, m2_ref, v2_ref):
        my_chip = 2 * lax.axis_index("x") + lax.axis_index("y")
        g = None
        for j in range(NCHIP):
            part = jnp.where(my_chip == j, own_ref[...], p_ref[j].astype(F32))
            g = part if g is None else g + part
        _adam_write(g, w_ref, m_ref, v_ref, g_ref, d_ref, m2_ref, v2_ref)

    blk = pl.BlockSpec((tr, tc), lambda i, k: (i, k))
    out = jax.ShapeDtypeStruct((r, n), F32)
    return _call(
        body, name=name, grid=(r // tr, n // tc),
        in_specs=[blk, pl.BlockSpec((NCHIP, tr, tc), lambda i, k: (0, i, k)), blk, blk, blk],
        out_specs=(blk, blk, blk, blk), out_shape=(out, out, out, out),
        compiler_params=_cp(("parallel", "parallel")),
    )(own, recv, w, m, v)


SMALL = ("norm_w", "a_log", "dt_bias", "dn_norm_w", "dw_b", "ln_w", "ln_b", "b_cf_out", "final_norm_w")


def kernel(x, meta, norm_w, w_in, conv_qkv_w, a_log, dt_bias, dn_norm_w, w_dn_out, dw_w, dw_b, ln_w, ln_b, w_cf_out, b_cf_out, w_o, final_norm_w, loss_target, m_meta, m_norm_w, m_w_in, m_conv_qkv_w, m_a_log, m_dt_bias, m_dn_norm_w, m_w_dn_out, m_dw_w, m_dw_b, m_ln_w, m_ln_b, m_w_cf_out, m_b_cf_out, m_w_o, m_final_norm_w, v_meta, v_norm_w, v_w_in, v_conv_qkv_w, v_a_log, v_dt_bias, v_dn_norm_w, v_w_dn_out, v_dw_w, v_dw_b, v_ln_w, v_ln_b, v_w_cf_out, v_b_cf_out, v_w_o, v_final_norm_w):
    seq = x.shape[1]
    pad = (-(seq + NMETA)) % LANE
    in_w = w_in.shape[2] * NDEV
    n_qkvz = 4 * D
    n_ba = 2 * H

    w_in_g, w_dn_g, w_cf_g, w_o_g, meta_g, cqw_g, dww_g = _gather_two_level(
        [w_in[0].astype(BF16).T, w_dn_out[0].astype(BF16), w_cf_out[0].astype(BF16), w_o[0].astype(BF16),
         meta, conv_qkv_w[0], dw_w[0]], "gather_weights")
    w_full_t = w_in_g.reshape(in_w, D)
    c_glu = n_qkvz + n_ba
    c_zb, c_mg = c_glu + 2 * D, c_glu + 3 * D
    w_main_t = jnp.concatenate([w_full_t[:n_qkvz], w_full_t[c_glu:c_zb], w_full_t[c_mg:], w_full_t[c_zb:c_mg]],
                               axis=0)
    w_ba_t = jnp.pad(w_full_t[n_qkvz:n_qkvz + n_ba], ((0, LANE - n_ba), (0, 0)))
    w_dn, w_cf, w_oo = (t.reshape(D, D) for t in (w_dn_g, w_cf_g, w_o_g))
    meta_full = jnp.transpose(meta_g, (1, 0, 2)).reshape(NMETA, D)
    cqw = jnp.transpose(cqw_g, (1, 0, 2)).reshape(KQ, 3 * D)
    dww = jnp.transpose(dww_g, (1, 0, 2)).reshape(KD, D)
    ab = jnp.pad(jnp.concatenate([a_log, dt_bias], axis=0), ((0, 0), (H, LANE - 2 * H)))

    x_ext = jnp.concatenate([jnp.zeros((pad, D), F32), meta_full, x[0]], axis=0)

    proj, ba, h = _proj_fwd(x_ext, norm_w, w_main_t, w_ba_t)
    qkv, bg = _qkv_conv_fwd(proj, ba, cqw, ab, pad)
    o, sall, tall = _delta_fwd(qkv, bg)
    o_n, y_a = _o_post_fwd(o, proj, dn_norm_w, w_dn)
    c1, c3, y_b = _conv_b_fwd(proj, dww, dw_b, ln_w, ln_b, w_cf)
    merged, z = _merge_fwd(y_a, y_b, proj, b_cf_out, w_oo)
    dx_out, dx_out_b, loss_part, g_final_w = _final_fwd_bwd(x_ext, z, loss_target[0], final_norm_w.reshape(1, D))

    g_w_o = _mm_tn(merged, dx_out_b, "g_w_o_mm")
    dy_a, dy_b, dproj, g_b_cf = _merge_bwd(dx_out_b, w_oo, y_a, y_b, proj, b_cf_out)
    g_w_cf = _mm_tn(c3, dy_b, "g_w_cf_mm")
    g_w_dn = _mm_tn(o_n, dy_a, "g_w_dn_mm")
    dc1, dproj, sums_b = _conv_b_bwd1(dy_b, w_cf, c1, proj, ln_w, ln_b, dproj)
    dproj, g_dw_w = _conv_b_bwd2(dc1, proj, dww, dproj)
    do, dproj, g_dn_w = _o_post_bwd(dy_a, w_dn, o, proj, dn_norm_w, dproj)
    dqkv, dbg = _delta_bwd(qkv, bg, sall, tall, do)
    dproj, g_cqw = _qkv_conv_bwd(proj, dqkv, cqw, dproj)
    dba, dab = _ba_bwd(dbg, ba, ab, pad)
    dh = _dh_mm(dproj, dba, w_main_t, w_ba_t)
    g_w_main_t = _mm_tn(dproj, h, "g_w_main_mm")
    g_w_ba_t = _mm_tn(dba, h, "g_w_ba_mm")
    grad_x, dhead, g_norm_w = _prenorm_bwd(dh, x_ext, dx_out, norm_w, seq)

    g_w_full_t = jnp.concatenate([g_w_main_t[:n_qkvz], g_w_ba_t[:n_ba], g_w_main_t[CB_GA_ * D:CB_MA * D],
                                  g_w_main_t[CB_ZB * D:], g_w_main_t[CB_MA * D:CB_ZB * D]], axis=0)
    split_cols = lambda t: jnp.transpose(t.reshape(t.shape[0], NDEV, t.shape[1] // NDEV), (1, 0, 2))
    small = {"norm_w": g_norm_w, "a_log": dab[0:1, H:2 * H], "dt_bias": dab[1:2, H:2 * H], "dn_norm_w": g_dn_w,
             "dw_b": sums_b[2:3], "ln_w": sums_b[0:1], "ln_b": sums_b[1:2], "b_cf_out": g_b_cf,
             "final_norm_w": g_final_w}
    small_vec = jnp.concatenate([small[k] for k in SMALL], axis=1)
    ns = small_vec.shape[1]
    ns_pad = (-ns) % LANE
    small_vec = jnp.pad(small_vec, ((0, 0), (0, ns_pad)))
    big = [g_w_full_t.reshape(NDEV, in_w // NDEV, D), g_w_dn.reshape(NDEV, D // NDEV, D),
           g_w_cf.reshape(NDEV, D // NDEV, D), g_w_o.reshape(NDEV, D // NDEV, D)]
    from_sibling = _swap_sibling(big, "swap_sibling")
    pairs = [_pair_add(a, g, f"pair_add_{i}") for i, (a, g) in enumerate(zip(big, from_sibling))]
    from_chips = _scatter_chips([p for p, _ in pairs], "scatter_chips")
    p_meta, p_cqw, p_dww, p_small = _exchange(
        [split_cols(dhead[pad:pad + NMETA]), split_cols(g_cqw), split_cols(g_dw_w), small_vec],
        [True] * 3 + [False], "exchange_small")

    res = {}
    res["w_in"] = tuple(t.T for t in _adam_chips(pairs[0][1], from_chips[0], w_in[0].T, m_w_in[0].T, v_w_in[0].T,
                                                   "adam_w_in"))
    res["w_dn_out"] = _adam_chips(pairs[1][1], from_chips[1], w_dn_out[0], m_w_dn_out[0], v_w_dn_out[0], "adam_w_dn")
    res["w_cf_out"] = _adam_chips(pairs[2][1], from_chips[2], w_cf_out[0], m_w_cf_out[0], v_w_cf_out[0], "adam_w_cf")
    res["w_o"] = _adam_chips(pairs[3][1], from_chips[3], w_o[0], m_w_o[0], v_w_o[0], "adam_w_o")
    res["meta"] = _adam_reduce(p_meta, meta, m_meta, v_meta, "adam_meta")
    res["conv_qkv_w"] = _adam_reduce(p_cqw, conv_qkv_w[0], m_conv_qkv_w[0], v_conv_qkv_w[0], "adam_conv_qkv_w")
    res["dw_w"] = _adam_reduce(p_dww, dw_w[0], m_dw_w[0], v_dw_w[0], "adam_dw_w")
    loc = dict(norm_w=(norm_w, m_norm_w, v_norm_w), a_log=(a_log, m_a_log, v_a_log), dt_bias=(dt_bias, m_dt_bias, v_dt_bias),
               dn_norm_w=(dn_norm_w, m_dn_norm_w, v_dn_norm_w), dw_b=(dw_b, m_dw_b, v_dw_b), ln_w=(ln_w, m_ln_w, v_ln_w),
               ln_b=(ln_b, m_ln_b, v_ln_b), b_cf_out=(b_cf_out, m_b_cf_out, v_b_cf_out),
               final_norm_w=(final_norm_w, m_final_norm_w, v_final_norm_w))
    cat = lambda j: jnp.pad(jnp.concatenate([loc[k][j].reshape(1, -1) for k in SMALL], axis=1), ((0, 0), (0, ns_pad)))
    small_res = _adam_reduce(p_small, cat(0), cat(1), cat(2), "adam_small")
    off = 0
    for k in SMALL:
        wshape = loc[k][0].shape
        nk = loc[k][0].size
        res[k] = tuple(t[:, off:off + nk].reshape(wshape) for t in small_res)
        off += nk
    shaped = dict(w_in=w_in.shape, w_dn_out=w_dn_out.shape, w_cf_out=w_cf_out.shape, w_o=w_o.shape, meta=meta.shape,
                  conv_qkv_w=conv_qkv_w.shape, dw_w=dw_w.shape)
    for k, shp in shaped.items():
        res[k] = tuple(t.reshape(shp) for t in res[k])

    loss = lax.psum(loss_part[0, 0], ("x", "y", "c"))
    order = ("meta", "norm_w", "w_in", "conv_qkv_w", "a_log", "dt_bias", "dn_norm_w", "w_dn_out", "dw_w", "dw_b", "ln_w",
             "ln_b", "w_cf_out", "b_cf_out", "w_o", "final_norm_w")
    outs = [loss, grad_x[None]]
    for j in range(4):
        outs += [res[k][j] for k in order]
    return tuple(outs)
```

```python
import functools

import jax
import jax.numpy as jnp
from jax import lax
from jax.experimental import pallas as pl
from jax.experimental.pallas import tpu as pltpu

F32 = jnp.float32
BF16 = jnp.bfloat16
HI = lax.Precision.HIGHEST

D = 1024
H = 8
DK = 128
C = 64
NMETA = 16
KQ = 4
KD = 31
HALO_Q = 8
HALO_D = 32
EPS = 1e-6
NDEV = 8
LANE = 128
MIB = 1024 * 1024

ADAM_LR, ADAM_B1, ADAM_B2, ADAM_EPS, ADAM_WD, ADAM_STEP = 0.001, 0.9, 0.999, 1e-08, 0.01, 10

CB_Q, CB_K, CB_V, CB_ZA, CB_GA_, CB_GB_, CB_MA, CB_MB, CB_ZB = range(9)
NCB = 9


def _pick(n, cands):
    for c in cands:
        if n % c == 0:
            return c
    raise ValueError(f"no tile for {n}")


def _cp(sem=None, vmem_mib=40):
    kw = dict(vmem_limit_bytes=vmem_mib * MIB)
    if sem is not None:
        kw["dimension_semantics"] = sem
    return pltpu.CompilerParams(**kw)


def _call(body, **kw):
    return pl.pallas_call(body, **kw)


def _dot(a, b):
    return jnp.dot(a.astype(BF16), b.astype(BF16), preferred_element_type=F32)


def _dot_nt(a, b):
    return lax.dot_general(a.astype(BF16), b.astype(BF16), (((1,), (1,)), ((), ())), preferred_element_type=F32)


def _dot_tn(a, b):
    return lax.dot_general(a.astype(BF16), b.astype(BF16), (((0,), (0,)), ((), ())), preferred_element_type=F32)


def _dot_hi(a, b):
    return jnp.dot(a, b, precision=HI, preferred_element_type=F32)


def _sig(x):
    return 0.5 * jnp.tanh(0.5 * x) + 0.5


def _dsilu(x, s):
    return s * (1.0 + x * (1.0 - s))


def _rowsum(x):
    return jnp.sum(x, axis=-1, keepdims=True)


def _colsum(x):
    return jnp.sum(x, axis=0, keepdims=True)


def _exchange(arrs, scatter, name):
    n = len(arrs)
    out_shape = []
    for a, sc in zip(arrs, scatter):
        shp = a.shape if sc else (NDEV,) + a.shape
        out_shape.append(jax.ShapeDtypeStruct(shp, a.dtype))

    def body(*refs):
        ins, outs = refs[:n], refs[n:2 * n]
        send_sems, recv_sems, loc_sems = refs[2 * n:]
        x, y, c = lax.axis_index("x"), lax.axis_index("y"), lax.axis_index("c")
        me = 4 * x + 2 * y + c
        copies = []
        for a in range(n):
            for k in range(1, NDEV):
                px = 1 - x if (k >> 2) & 1 else x
                py = 1 - y if (k >> 1) & 1 else y
                pc = 1 - c if k & 1 else c
                src = ins[a].at[4 * px + 2 * py + pc] if scatter[a] else ins[a]
                cp = pltpu.make_async_remote_copy(
                    src_ref=src, dst_ref=outs[a].at[me],
                    send_sem=send_sems.at[a * (NDEV - 1) + k - 1], recv_sem=recv_sems.at[a * (NDEV - 1) + k - 1],
                    device_id=(px, py, pc), device_id_type=pl.DeviceIdType.MESH)
                cp.start()
                copies.append(cp)
            loc = pltpu.make_async_copy(ins[a].at[me] if scatter[a] else ins[a], outs[a].at[me], loc_sems.at[a])
            loc.start()
            copies.append(loc)
        for cp in copies:
            cp.wait()

    any_spec = pl.BlockSpec(memory_space=pl.ANY)
    return _call(
        body, name=name, out_shape=tuple(out_shape),
        in_specs=[any_spec] * n, out_specs=tuple([any_spec] * n),
        scratch_shapes=[pltpu.SemaphoreType.DMA((n * (NDEV - 1),)), pltpu.SemaphoreType.DMA((n * (NDEV - 1),)),
                        pltpu.SemaphoreType.DMA((n,))],
    )(*arrs)


NCHIP = 4


def _gather_two_level(arrs, name):
    n = len(arrs)
    per = NDEV - 1

    def body(*refs):
        ins, outs = refs[:n], refs[n:2 * n]
        send_sems, recv_sems, loc_sems = refs[2 * n:]
        x, y, c = lax.axis_index("x"), lax.axis_index("y"), lax.axis_index("c")
        me, sibling = (x, y, c), (x, y, 1 - c)
        chips = [(1 - x, y), (x, 1 - y), (1 - x, 1 - y)]

        def slot(a, px, py, pc):
            return outs[a].at[4 * px + 2 * py + pc]

        def copy(a, k, block, to, src=None):
            return pltpu.make_async_remote_copy(
                src_ref=slot(a, *block) if src is None else src, dst_ref=slot(a, *block),
                send_sem=send_sems.at[a * per + k], recv_sem=recv_sems.at[a * per + k],
                device_id=to, device_id_type=pl.DeviceIdType.MESH)

        local, sent = [], []
        for a in range(n):
            mine = pltpu.make_async_copy(ins[a], slot(a, *me), loc_sems.at[a])
            mine.start()
            local.append(mine)
            first = [copy(a, 1 + j, me, (*chip, c), src=ins[a]) for j, chip in enumerate(chips)]
            first.append(copy(a, 0, me, sibling, src=ins[a]))
            for cp in first:
                cp.start()
            sent += first
        for j, chip in enumerate(chips):
            for a in range(n):
                copy(a, 1 + j, (*chip, c), me).wait_recv()
                cp = copy(a, 4 + j, (*chip, c), sibling)
                cp.start()
                sent.append(cp)
        for a in range(n):
            copy(a, 0, sibling, me).wait_recv()
            for j, chip in enumerate(chips):
                copy(a, 4 + j, (*chip, 1 - c), me).wait_recv()
        for cp in sent:
            cp.wait_send()
        for cp in local:
            cp.wait()

    any_spec = pl.BlockSpec(memory_space=pl.ANY)
    return _call(
        body, name=name, out_shape=tuple(jax.ShapeDtypeStruct((NDEV,) + a.shape, a.dtype) for a in arrs),
        in_specs=[any_spec] * n, out_specs=tuple([any_spec] * n),
        scratch_shapes=[pltpu.SemaphoreType.DMA((n * per,)), pltpu.SemaphoreType.DMA((n * per,)),
                        pltpu.SemaphoreType.DMA((n,))],
    )(*arrs)


def _swap_sibling(arrs, name):
    n = len(arrs)

    def body(*refs):
        ins, outs = refs[:n], refs[n:2 * n]
        send_sems, recv_sems = refs[2 * n:]
        x, y, c = lax.axis_index("x"), lax.axis_index("y"), lax.axis_index("c")
        copies = []
        for a in range(n):
            for j in range(NCHIP):
                cp = pltpu.make_async_remote_copy(
                    src_ref=ins[a].at[2 * j + (1 - c)], dst_ref=outs[a].at[j],
                    send_sem=send_sems.at[a * NCHIP + j], recv_sem=recv_sems.at[a * NCHIP + j],
                    device_id=(x, y, 1 - c), device_id_type=pl.DeviceIdType.MESH)
                cp.start()
                copies.append(cp)
        for cp in copies:
            cp.wait()

    any_spec = pl.BlockSpec(memory_space=pl.ANY)
    return _call(
        body, name=name, out_shape=tuple(jax.ShapeDtypeStruct((NCHIP,) + a.shape[1:], a.dtype) for a in arrs),
        in_specs=[any_spec] * n, out_specs=tuple([any_spec] * n),
        scratch_shapes=[pltpu.SemaphoreType.DMA((n * NCHIP,)), pltpu.SemaphoreType.DMA((n * NCHIP,))],
    )(*arrs)


def _pair_add(arr, got, name):
    _, r, n = arr.shape
    tr, tc = _shard_tile(r, n)
    arr4 = arr.reshape(NCHIP, 2, r, n)

    def body(a_ref, g_ref, p_ref, own_ref):
        c = lax.axis_index("c")
        my_chip = 2 * lax.axis_index("x") + lax.axis_index("y")
        s = jnp.where(c == 0, a_ref[0, 0], a_ref[0, 1]) + g_ref[0]
        p_ref[0] = s.astype(BF16)

        @pl.when(pl.program_id(2) == my_chip)
        def _():
            own_ref[...] = s

    return _call(
        body, name=name, grid=(r // tr, n // tc, NCHIP),
        in_specs=[pl.BlockSpec((1, 2, tr, tc), lambda i, k, j: (j, 0, i, k)),
                  pl.BlockSpec((1, tr, tc), lambda i, k, j: (j, i, k))],
        out_specs=(pl.BlockSpec((1, tr, tc), lambda i, k, j: (j, i, k)), pl.BlockSpec((tr, tc), lambda i, k, j: (i, k))),
        out_shape=(jax.ShapeDtypeStruct((NCHIP, r, n), BF16), jax.ShapeDtypeStruct((r, n), F32)),
        compiler_params=_cp(("parallel", "parallel", "arbitrary")),
    )(arr4, got)


def _shard_tile(r, n):
    return (128, n) if r % 128 == 0 else (r, 256)


def _chip_copies(srcs, lands, send_sems, recv_sems):
    x, y, c = lax.axis_index("x"), lax.axis_index("y"), lax.axis_index("c")
    per = NCHIP - 1
    copies = []
    for a in range(len(srcs)):
        for k in range(1, NCHIP):
            px = 1 - x if (k >> 1) & 1 else x
            py = 1 - y if k & 1 else y
            copies.append(pltpu.make_async_remote_copy(
                src_ref=srcs[a].at[2 * px + py], dst_ref=lands[a].at[2 * x + y],
                send_sem=send_sems.at[a * per + k - 1], recv_sem=recv_sems.at[a * per + k - 1],
                device_id=(px, py, c), device_id_type=pl.DeviceIdType.MESH))
    return copies


def _scatter_chips_start(arrs, lands):
    n = len(arrs)
    nsem = n * (NCHIP - 1)

    def body(*refs):
        srcs, land_in = refs[:n], refs[n:2 * n]
        send_sems, recv_sems = refs[2 * n:2 * n + 2]
        token = refs[-1]
        for cp in _chip_copies(srcs, land_in, send_sems, recv_sems):
            cp.start()
        token[...] = jnp.zeros_like(token)

    hbm = pl.BlockSpec(memory_space=pltpu.HBM)
    sem = pl.BlockSpec(memory_space=pltpu.SEMAPHORE)
    both = list(arrs) + list(lands)
    outs = _call(
        body, name="scatter_chips_start",
        out_shape=(pltpu.SemaphoreType.DMA((nsem,)), pltpu.SemaphoreType.DMA((nsem,)),
                   *[pltpu.HBM(a.shape, a.dtype) for a in both], jax.ShapeDtypeStruct((SUBLANES, LANE), F32)),
        in_specs=[hbm] * (2 * n), out_specs=(sem, sem, *[hbm] * (2 * n), pl.BlockSpec(memory_space=pltpu.VMEM)),
        input_output_aliases={i: i + 2 for i in range(2 * n)},
        compiler_params=pltpu.CompilerParams(has_side_effects=pltpu.SideEffectType.DATAFLOW_SIDE_EFFECTING),
    )(*[pltpu.with_memory_space_constraint(t, pltpu.HBM) for t in both])
    return outs[0], outs[1], outs[2:2 + n], outs[2 + n:2 + 2 * n], outs[-1]


def _scatter_chips_wait(send_sems, recv_sems, arrs, lands, after):
    n = len(arrs)

    def body(*refs):
        srcs, land_in = refs[:n], refs[n:2 * n]
        send, recv = refs[2 * n], refs[2 * n + 1]
        for cp in _chip_copies(srcs, land_in, send, recv):
            cp.wait_send()
            cp.wait_recv()

    hbm = pl.BlockSpec(memory_space=pltpu.HBM)
    sem = pl.BlockSpec(memory_space=pltpu.SEMAPHORE)
    both = list(arrs) + list(lands)
    outs = _call(
        body, name="scatter_chips_wait",
        out_shape=tuple(pltpu.HBM(a.shape, a.dtype) for a in both),
        in_specs=[hbm] * (2 * n) + [sem, sem, pl.BlockSpec(memory_space=pl.ANY)], out_specs=tuple([hbm] * (2 * n)),
        input_output_aliases={i: i for i in range(2 * n)},
        compiler_params=pltpu.CompilerParams(has_side_effects=pltpu.SideEffectType.DATAFLOW_SIDE_EFFECTING),
    )(*both, send_sems, recv_sems, after)
    return outs[n:]


def _mm_tn(a, b, name):
    t, m = a.shape
    n = b.shape[1]
    tt = _pick(t, (1664, 640, 128))
    tm = _pick(m, (1024, 512, 128))
    tn = _pick(n, (1152, 1024, 512, 128))
    nt = t // tt

    def body(a_ref, b_ref, o_ref):
        s = pl.program_id(2)
        part = _dot_tn(a_ref[...], b_ref[...])

        @pl.when(s == 0)
        def _():
            o_ref[...] = part

        @pl.when(s > 0)
        def _():
            o_ref[...] += part

    return _call(
        body, name=name, grid=(m // tm, n // tn, nt),
        in_specs=[pl.BlockSpec((tt, tm), lambda i, j, s: (s, i)), pl.BlockSpec((tt, tn), lambda i, j, s: (s, j))],
        out_specs=pl.BlockSpec((tm, tn), lambda i, j, s: (i, j)),
        out_shape=jax.ShapeDtypeStruct((m, n), F32),
        compiler_params=_cp(("parallel", "parallel", "arbitrary")),
    )(a, b)


def _proj_fwd(x_ext, norm_w, w_main_t, w_ba_t):
    lp = x_ext.shape[0]
    n = w_main_t.shape[0]
    tm = _pick(lp, (832, 640, 320))
    tn = 1024

    def body(x_ref, nw_ref, w_ref, wba_ref, proj_ref, ba_ref, h_ref):
        @pl.when(pl.program_id(1) == 0)
        def _():
            x = x_ref[...]
            r = lax.rsqrt(jnp.mean(x * x, axis=-1, keepdims=True) + EPS)
            h = (x * r * nw_ref[...]).astype(BF16)
            h_ref[...] = h
            ba_ref[...] = _dot_nt(h, wba_ref[...])

        proj_ref[...] = _dot_nt(h_ref[...], w_ref[...])

    return _call(
        body, name="proj_fwd", grid=(lp // tm, n // tn),
        in_specs=[pl.BlockSpec((tm, D), lambda i, j: (i, 0)), pl.BlockSpec((1, D), lambda i, j: (0, 0)),
                  pl.BlockSpec((tn, D), lambda i, j: (j, 0)), pl.BlockSpec((LANE, D), lambda i, j: (0, 0))],
        out_specs=(pl.BlockSpec((tm, tn), lambda i, j: (i, j)), pl.BlockSpec((tm, LANE), lambda i, j: (i, 0)),
                   pl.BlockSpec((tm, D), lambda i, j: (i, 0))),
        out_shape=(jax.ShapeDtypeStruct((lp, n), F32), jax.ShapeDtypeStruct((lp, LANE), F32),
                   jax.ShapeDtypeStruct((lp, D), BF16)),
        compiler_params=_cp(("parallel", "arbitrary")),
    )(x_ext, norm_w, w_main_t, w_ba_t)


def _dh_mm(dproj, dba, w_main_t, w_ba_t):
    lp, n = dproj.shape
    tm = _pick(lp, (832, 640, 320))
    tn = 1024
    tk = 2304
    nk = n // tk

    def body(a_ref, ba_ref, b_ref, bba_ref, o_ref, acc):
        kk = pl.program_id(2)

        @pl.when(kk == 0)
        def _():
            acc[...] = jnp.dot(ba_ref[...], bba_ref[...], preferred_element_type=F32)

        acc[...] += jnp.dot(a_ref[...], b_ref[...], preferred_element_type=F32)

        @pl.when(kk == nk - 1)
        def _():
            o_ref[...] = acc[...]

    return _call(
        body, name="dh_mm", grid=(lp // tm, D // tn, nk),
        in_specs=[pl.BlockSpec((tm, tk), lambda i, j, kk: (i, kk)), pl.BlockSpec((tm, LANE), lambda i, j, kk: (i, 0)),
                  pl.BlockSpec((tk, tn), lambda i, j, kk: (kk, j)), pl.BlockSpec((LANE, tn), lambda i, j, kk: (0, j))],
        out_specs=pl.BlockSpec((tm, tn), lambda i, j, kk: (i, j)),
        out_shape=jax.ShapeDtypeStruct((lp, D), F32),
        scratch_shapes=[pltpu.VMEM((tm, tn), F32)],
        compiler_params=_cp(("parallel", "parallel", "arbitrary")),
    )(dproj, dba, w_main_t, w_ba_t)


def _beta_g(ba, ab, row0, pad):
    lane = lax.broadcasted_iota(jnp.int32, ba.shape, 1)
    rows = row0 + lax.broadcasted_iota(jnp.int32, ba.shape, 0)
    z = ba + ab[1:2, :]
    sp = jnp.maximum(z, 0.0) + jnp.log(1.0 + jnp.exp(-jnp.abs(z)))
    val = jnp.where(lane < H, _sig(ba), -jnp.exp(ab[0:1, :]) * sp)
    return jnp.where((lane < 2 * H) & (rows >= pad), val, 0.0)


def _qkv_conv_fwd(proj, ba, conv_w, ab, pad):
    lp = proj.shape[0]
    te = _pick(lp, (320,))
    hb = te // HALO_Q

    def body(main_ref, halo_ref, cw_ref, ba_ref, ab_ref, out_ref, bg_ref, pre_scr, tap_scr):
        i, s = pl.program_id(0), pl.program_id(1)
        pre_scr[:HALO_Q, :] = jnp.where(i > 0, halo_ref[...], 0.0)
        pre_scr[HALO_Q:, :] = main_ref[...]
        scale = jnp.where(s == 0, DK ** -0.5, 1.0)
        off = HALO_Q - (KQ - 1)

        def head(h, carry):
            cs = pl.ds(pl.multiple_of(h * DK, DK), DK)
            for j in range(KQ - 1):
                tap_scr[j] = pre_scr[off + j:off + j + te, cs]
            co = cw_ref[KQ - 1:KQ, cs] * pre_scr[HALO_Q:, cs]
            for j in range(KQ - 1):
                co = co + cw_ref[j:j + 1, cs] * tap_scr[j]
            a = co * _sig(co)
            r = lax.rsqrt(_rowsum(a * a) + EPS)
            out_ref[:, cs] = jnp.where(s == 2, a, a * (r * scale))
            return carry

        lax.fori_loop(0, H, head, 0, unroll=True)

        @pl.when(s == 0)
        def _():
            bg_ref[...] = _beta_g(ba_ref[...], ab_ref[...], i * te, pad)

    return _call(
        body, name="qkv_conv_fwd", grid=(lp // te, 3),
        in_specs=[pl.BlockSpec((te, D), lambda i, s: (i, s)),
                  pl.BlockSpec((HALO_Q, D), lambda i, s: (jnp.maximum(i * hb - 1, 0), s)),
                  pl.BlockSpec((KQ, D), lambda i, s: (0, s)),
                  pl.BlockSpec((te, LANE), lambda i, s: (i, 0)),
                  pl.BlockSpec((2, LANE), lambda i, s: (0, 0))],
        out_specs=(pl.BlockSpec((te, D), lambda i, s: (i, s)), pl.BlockSpec((te, LANE), lambda i, s: (i, 0))),
        out_shape=(jax.ShapeDtypeStruct((lp, 3 * D), F32), jax.ShapeDtypeStruct((lp, LANE), F32)),
        scratch_shapes=[pltpu.VMEM((te + HALO_Q, D), F32), pltpu.VMEM((KQ - 1, te, DK), F32)],
        compiler_params=_cp(("parallel", "arbitrary")),
    )(proj, proj, conv_w, ba, ab)


def _tri_masks():
    row = lax.broadcasted_iota(jnp.int32, (C, C), 0)
    col = lax.broadcasted_iota(jnp.int32, (C, C), 1)
    return row, col


def _split(a):
    hi = a.astype(BF16)
    return hi, (a - hi.astype(F32)).astype(BF16)


def _dot3(a, b, dims=(((1,), (0,)), ((), ()))):
    (ah, al), (bh, bl) = a, b
    mm = lambda x, y: lax.dot_general(x, y, dims, preferred_element_type=F32)
    return mm(ah, bh) + (mm(ah, bl) + mm(al, bh))


CHUNKS_PER_STEP = 2
CHUNKS_PER_STEP_BWD = 1
TINV_BLOCK = 16


def _tinv(ns, row, col):
    eye = (row == col).astype(F32)
    sh = TINV_BLOCK.bit_length() - 1
    same16 = (row >> sh) == (col >> sh)
    same32 = (row >> (sh + 1)) == (col >> (sh + 1))
    ys = [jnp.where(same16, -n, 0.0) for n in ns]
    ts = [eye + y for y in ys]
    sp = [_split(y) for y in ys]
    for level in range(3):
        yks = [_dot3(s, s) for s in sp]
        sp = [_split(yk) for yk in yks]
        ts = [t + _dot3(s, _split(t)) for s, t in zip(sp, ts)]
    for mask in (same32 & ~same16, ~same32):
        tsp = [_split(t) for t in ts]
        inner = [_dot3(_split(jnp.where(mask, n, 0.0)), t) for n, t in zip(ns, tsp)]
        ts = [t - _dot3(tp, _split(a)) for t, tp, a in zip(ts, tsp, inner)]
    return ts


def _chunk_common(q, k, v, bcol, gcc, gcr, incl, strict):
    dm = jnp.where(incl, jnp.exp(gcc - gcr), 0.0)
    kk = _dot_nt(k, k)
    qk = _dot_nt(q, k)
    gccw = jnp.broadcast_to(gcc, (C, DK))
    egc = jnp.exp(gccw)
    glast = gccw[C - 1:C, :]
    eend = jnp.exp(glast - gccw)
    elast = jnp.exp(glast)
    rhs = jnp.concatenate([v * bcol, k * (bcol * egc)], axis=1)
    return dm, kk, qk, egc, eend, elast, rhs


def _delta_fwd(qkv, bg):
    lp = qkv.shape[0]
    nc = lp // C
    heads = range(H)
    sls = [slice(h * DK, (h + 1) * DK) for h in heads]

    def body(q_ref, k_ref, v_ref, bg_ref, o_ref, sall_ref, tall_ref, s_scr):
        @pl.when(pl.program_id(0) == 0)
        def _():
            s_scr[...] = jnp.zeros_like(s_scr)

        row, col = _tri_masks()
        incl, strict = row >= col, row > col

        def prepare(sub):
            rs = slice(sub * C, (sub + 1) * C)
            bgt = bg_ref[rs, :]
            gc_all = _dot_hi(incl.astype(F32), bgt)
            gc_t = _dot_hi(bgt.T, (row <= col).astype(F32))
            qs, ks, vs = ([r[rs, sl] for sl in sls] for r in (q_ref, k_ref, v_ref))
            bcols = [jnp.broadcast_to(bgt[:, h:h + 1], (C, DK)) for h in heads]
            cm = [_chunk_common(qs[h], ks[h], vs[h], bcols[h], gc_all[:, H + h:H + h + 1], gc_t[H + h:H + h + 1, :],
                                incl, strict) for h in heads]
            dms, kks, qks, egcs, eends, elasts, rhss = zip(*cm)
            ts = _tinv([jnp.where(strict, bcols[h][:, :C] * kks[h] * dms[h], 0.0) for h in heads], row, col)
            sols = [_dot3(_split(ts[h]), _split(rhss[h])) for h in heads]
            qgs = [(qs[h] * egcs[h]).astype(BF16) for h in heads]
            ps = [(qks[h] * dms[h]).astype(BF16) for h in heads]
            kends = [(ks[h] * eends[h]).astype(BF16) for h in heads]
            return ts, sols, qgs, ps, kends, elasts

        prepared = [prepare(sub) for sub in range(CHUNKS_PER_STEP)]
        ss = [s_scr[h] for h in heads]
        for sub in range(CHUNKS_PER_STEP):
            rs = slice(sub * C, (sub + 1) * C)
            ts, sols, qgs, ps, kends, elasts = prepared[sub]
            sb = [s.astype(BF16) for s in ss]
            wvb = [(sols[h][:, :DK] - _dot(sols[h][:, DK:], sb[h])).astype(BF16) for h in heads]
            for h in heads:
                o_ref[rs, sls[h]] = _dot(qgs[h], sb[h]) + _dot(ps[h], wvb[h])
                sall_ref[sub, h] = ss[h]
                tall_ref[sub, h] = ts[h]
            ss = [ss[h] * elasts[h] + _dot_tn(kends[h], wvb[h]) for h in heads]
        for h in heads:
            s_scr[h] = ss[h]

    rows = CHUNKS_PER_STEP * C
    blk = lambda j: pl.BlockSpec((rows, D), lambda n: (n, j))
    return _call(
        body, name="delta_fwd", grid=(nc // CHUNKS_PER_STEP,),
        in_specs=[blk(0), blk(1), blk(2), pl.BlockSpec((rows, LANE), lambda n: (n, 0))],
        out_specs=(pl.BlockSpec((rows, D), lambda n: (n, 0)),
                   pl.BlockSpec((CHUNKS_PER_STEP, H, DK, DK), lambda n: (n, 0, 0, 0)),
                   pl.BlockSpec((CHUNKS_PER_STEP, H, C, C), lambda n: (n, 0, 0, 0))),
        out_shape=(jax.ShapeDtypeStruct((lp, D), F32), jax.ShapeDtypeStruct((nc, H, DK, DK), F32),
                   jax.ShapeDtypeStruct((nc, H, C, C), F32)),
        scratch_shapes=[pltpu.VMEM((H, DK, DK), F32)],
        compiler_params=_cp(("arbitrary",)),
    )(qkv, qkv, qkv, bg)


def _delta_bwd(qkv, bg, sall, tall, do):
    lp = qkv.shape[0]
    nc = lp // C

    heads = range(H)
    sls = [slice(h * DK, (h + 1) * DK) for h in heads]

    def body(q_ref, k_ref, v_ref, bg_ref, sall_ref, tall_ref, do_ref, dqkv_ref, dbg_ref, ds_scr):
        @pl.when(pl.program_id(0) == 0)
        def _():
            ds_scr[...] = jnp.zeros_like(ds_scr)

        dsns = [ds_scr[h] for h in heads]
        for sub in reversed(range(CHUNKS_PER_STEP_BWD)):
            dsns = chunk(sub, dsns, q_ref, k_ref, v_ref, bg_ref, sall_ref, tall_ref, do_ref, dqkv_ref, dbg_ref)
        for h in heads:
            ds_scr[h] = dsns[h]

    def chunk(sub, dsns, q_ref, k_ref, v_ref, bg_ref, sall_ref, tall_ref, do_ref, dqkv_ref, dbg_ref):
        rs = slice(sub * C, (sub + 1) * C)
        bgt = bg_ref[rs, :]
        row, col = _tri_masks()
        incl, strict = row >= col, row > col
        upper = (row <= col).astype(F32)
        gc_all = _dot_hi(incl.astype(F32), bgt)
        gc_t = _dot_hi(bgt.T, upper)
        lane = lax.broadcasted_iota(jnp.int32, (C, LANE), 1)
        lastrow = lax.broadcasted_iota(jnp.int32, (C, 1), 0) == C - 1
        qs, ks, vs, dos = ([r[rs, sl] for sl in sls] for r in (q_ref, k_ref, v_ref, do_ref))
        bcols = [jnp.broadcast_to(bgt[:, h:h + 1], (C, DK)) for h in heads]
        cm = [_chunk_common(qs[h], ks[h], vs[h], bcols[h], gc_all[:, H + h:H + h + 1], gc_t[H + h:H + h + 1, :],
                            incl, strict) for h in heads]
        dms, kks, qks, egcs, eends, elasts, rhss = zip(*cm)
        ss = [sall_ref[sub, h] for h in heads]
        ts = [tall_ref[sub, h] for h in heads]
        sb = [s.astype(BF16) for s in ss]
        dsb = [d.astype(BF16) for d in dsns]
        dob = [d.astype(BF16) for d in dos]
        sols = [_dot3(_split(ts[h]), _split(rhss[h])) for h in heads]
        ws = [sol[:, DK:] for sol in sols]
        qgs = [qs[h] * egcs[h] for h in heads]
        kends = [ks[h] * eends[h] for h in heads]
        wvs = [sols[h][:, :DK] - _dot(ws[h], sb[h]) for h in heads]
        wvb = [wv.astype(BF16) for wv in wvs]
        dwvs = [_dot_tn(qks[h] * dms[h], dob[h]) + _dot(kends[h], dsb[h]) for h in heads]
        dps = [jnp.where(incl, _dot_nt(dob[h], wvb[h]), 0.0) for h in heads]
        dqgs = [_dot_nt(dob[h], sb[h]) for h in heads]
        dkends = [_dot_nt(wvb[h], dsb[h]) for h in heads]
        ds_before = [_dot_tn(qgs[h], dob[h]) + elasts[h] * dsns[h] - _dot_tn(ws[h], dwvs[h]) for h in heads]
        dglasts = [elasts[h] * jnp.sum(ss[h] * dsns[h], keepdims=True) for h in heads]
        dws = [-_dot_nt(dwvs[h], sb[h]) for h in heads]
        tts = [_split(ts[h].T) for h in heads]
        drhss = [_dot3(tts[h], _split(jnp.concatenate([dwvs[h], dws[h]], axis=1))) for h in heads]
        nt_dims = (((1,), (1,)), ((), ()))
        dns = [jnp.where(strict, -_dot3(_split(drhss[h]), _split(sols[h]), nt_dims), 0.0) for h in heads]
        dbeta_t = jnp.zeros((C, LANE), F32)
        dgc_t = jnp.zeros((C, LANE), F32)
        for h in heads:
            q, k, v, bcol, dm, kk, qk, egc, eend = qs[h], ks[h], vs[h], bcols[h], dms[h], kks[h], qks[h], egcs[h], eends[h]
            drv, drk = drhss[h][:, :DK], drhss[h][:, DK:]
            dn, dp, dqg, dkend = dns[h], dps[h], dqgs[h], dkends[h]
            rk = _rowsum(drk * k)
            dkk = dn * (bcol[:, :C] * dm)
            dqk = dp * dm
            e = (dn * (bcol[:, :C] * kk) + dp * qk) * dm
            tk = _rowsum(dkend * kends[h])
            dgc = rk * bcol * egc + _rowsum(e) - _rowsum(e.T) + _rowsum(dqg * qgs[h]) - tk
            dgc = dgc + jnp.where(lastrow, dglasts[h] + jnp.sum(tk, keepdims=True), 0.0)
            dbeta = _rowsum(drv * v) + rk * egc + _rowsum(dn * kk * dm)
            dqkv_ref[rs, sls[h]] = _dot(dqk, k) + dqg * egc
            dqkv_ref[rs, D + h * DK:D + (h + 1) * DK] = (drk * (bcol * egc) + _dot(dkk, k) + _dot_tn(dkk, k)
                                                        + _dot_tn(dqk, q) + dkend * eend)
            dqkv_ref[rs, 2 * D + h * DK:2 * D + (h + 1) * DK] = bcol * drv
            dbeta_t = jnp.where(lane == h, dbeta, dbeta_t)
            dgc_t = jnp.where(lane == H + h, dgc, dgc_t)
        dbg_ref[rs, :] = dbeta_t + _dot_hi(upper, dgc_t)
        return ds_before

    steps = nc // CHUNKS_PER_STEP_BWD
    rows = CHUNKS_PER_STEP_BWD * C
    rev = lambda n: steps - 1 - n
    blk = lambda j: pl.BlockSpec((rows, D), lambda n: (rev(n), j))
    return _call(
        body, name="delta_bwd", grid=(steps,),
        in_specs=[blk(0), blk(1), blk(2), pl.BlockSpec((rows, LANE), lambda n: (rev(n), 0)),
                  pl.BlockSpec((CHUNKS_PER_STEP_BWD, H, DK, DK), lambda n: (rev(n), 0, 0, 0)),
                  pl.BlockSpec((CHUNKS_PER_STEP_BWD, H, C, C), lambda n: (rev(n), 0, 0, 0)),
                  pl.BlockSpec((rows, D), lambda n: (rev(n), 0))],
        out_specs=(pl.BlockSpec((rows, 3 * D), lambda n: (rev(n), 0)),
                   pl.BlockSpec((rows, LANE), lambda n: (rev(n), 0))),
        out_shape=(jax.ShapeDtypeStruct((lp, 3 * D), F32), jax.ShapeDtypeStruct((lp, LANE), F32)),
        scratch_shapes=[pltpu.VMEM((H, DK, DK), F32)],
        compiler_params=_cp(("arbitrary",)),
    )(qkv, qkv, qkv, bg, sall, tall, do)


def _o_post_fwd(o, proj, dn_w, w_dn):
    lp = o.shape[0]
    te = _pick(lp, (640, 320))

    def body(o_ref, za_ref, w_ref, wdn_ref, out_ref, ya_ref):
        za = za_ref[...]
        gate = za * _sig(za)
        for h in range(H):
            sl = slice(h * DK, (h + 1) * DK)
            oh = o_ref[:, sl]
            r = lax.rsqrt(jnp.mean(oh * oh, axis=-1, keepdims=True) + EPS)
            out_ref[:, sl] = (oh * r * w_ref[...] * gate[:, sl]).astype(BF16)
        ya_ref[...] = _dot(out_ref[...], wdn_ref[...])

    row = pl.BlockSpec((te, D), lambda i: (i, 0))
    return _call(
        body, name="o_post_fwd", grid=(lp // te,),
        in_specs=[row, pl.BlockSpec((te, D), lambda i: (i, CB_ZA)), pl.BlockSpec((1, DK), lambda i: (0, 0)),
                  pl.BlockSpec((D, D), lambda i: (0, 0))],
        out_specs=(row, row),
        out_shape=(jax.ShapeDtypeStruct((lp, D), BF16), jax.ShapeDtypeStruct((lp, D), F32)),
        compiler_params=_cp(("parallel",)),
    )(o, proj, dn_w, w_dn)


def _o_post_bwd(dy_a, w_dn, o, proj, dn_w, dproj):
    lp = o.shape[0]
    te = _pick(lp, (320,))

    def body(dya_ref, wdn_ref, o_ref, za_ref, w_ref, _, do_ref, dza_ref, dw_ref, don_ref):
        @pl.when(pl.program_id(0) == 0)
        def _():
            dw_ref[...] = jnp.zeros_like(dw_ref)

        don_ref[...] = _dot_nt(dya_ref[...], wdn_ref[...])
        za = za_ref[...]
        sz = _sig(za)
        gate, dgate = za * sz, _dsilu(za, sz)
        w = w_ref[...]
        dw = jnp.zeros((1, DK), F32)
        for h in range(H):
            sl = slice(h * DK, (h + 1) * DK)
            oh, g = o_ref[:, sl], don_ref[:, sl]
            r = lax.rsqrt(jnp.mean(oh * oh, axis=-1, keepdims=True) + EPS)
            ohat = oh * r
            dza_ref[:, sl] = (g * ohat * w * dgate[:, sl]).astype(BF16)
            don = g * gate[:, sl]
            dw = dw + _colsum(don * ohat)
            dohat = don * w
            do_ref[:, sl] = r * (dohat - ohat * jnp.mean(dohat * ohat, axis=-1, keepdims=True))
        dw_ref[...] += dw

    return _call(
        body, name="o_post_bwd", grid=(lp // te,),
        in_specs=[pl.BlockSpec((te, D), lambda i: (i, 0)), pl.BlockSpec((D, D), lambda i: (0, 0)),
                  pl.BlockSpec((te, D), lambda i: (i, 0)),
                  pl.BlockSpec((te, D), lambda i: (i, CB_ZA)), pl.BlockSpec((1, DK), lambda i: (0, 0)),
                  pl.BlockSpec(memory_space=pl.ANY)],
        out_specs=(pl.BlockSpec((te, D), lambda i: (i, 0)), pl.BlockSpec((te, D), lambda i: (i, CB_ZA)),
                   pl.BlockSpec((1, DK), lambda i: (0, 0))),
        out_shape=(jax.ShapeDtypeStruct((lp, D), F32), jax.ShapeDtypeStruct(dproj.shape, dproj.dtype),
                   jax.ShapeDtypeStruct((1, DK), F32)),
        input_output_aliases={5: 1},
        scratch_shapes=[pltpu.VMEM((te, D), F32)],
        compiler_params=_cp(("arbitrary",)),
    )(dy_a, w_dn, o, proj, dn_w, dproj)


def _qkv_conv_bwd(proj, dqkv, conv_w, dproj):
    lp = proj.shape[0]
    te = _pick(lp, (320,))
    hb = te // HALO_Q
    nt = lp // te
    last_hb = lp // HALO_Q - 1

    def body(main_ref, prev_ref, next_ref, dmain_ref, dnext_ref, cw_ref, _, dpre_ref, dcw_ref, pre_scr, dn_scr,
             tap_scr, dco_scr, dsh_scr):
        s, i = pl.program_id(0), pl.program_id(1)

        @pl.when(i == 0)
        def _():
            dcw_ref[...] = jnp.zeros_like(dcw_ref)

        ne = te + HALO_Q
        pre_scr[:HALO_Q, :] = jnp.where(i > 0, prev_ref[...], 0.0)
        pre_scr[HALO_Q:ne, :] = main_ref[...]
        pre_scr[ne:, :] = jnp.where(i < nt - 1, next_ref[...], 0.0)
        dn_scr[:te, :] = dmain_ref[...]
        dn_scr[te:, :] = jnp.where(i < nt - 1, dnext_ref[...], 0.0)
        scale = jnp.where(s == 0, DK ** -0.5, 1.0)
        off = HALO_Q - (KQ - 1)

        def head(h, carry):
            cs = pl.ds(pl.multiple_of(h * DK, DK), DK)
            for j in range(KQ - 1):
                tap_scr[j] = pre_scr[off + j:off + j + ne, cs]
            taps = [tap_scr[j] for j in range(KQ - 1)] + [pre_scr[HALO_Q:, cs]]
            co = cw_ref[0:1, cs] * taps[0]
            for j in range(1, KQ):
                co = co + cw_ref[j:j + 1, cs] * taps[j]
            sg = _sig(co)
            a = co * sg
            g = dn_scr[:, cs]
            r = lax.rsqrt(_rowsum(a * a) + EPS)
            yhat = a * r
            da = jnp.where(s == 2, g, (scale * r) * (g - yhat * _rowsum(g * yhat)))
            dco = da * _dsilu(co, sg)
            dco_scr[...] = dco
            for j in range(KQ - 1):
                dsh_scr[j] = dco_scr[KQ - 1 - j:KQ - 1 - j + te, :]
            dpre = cw_ref[KQ - 1:KQ, cs] * dco[:te, :]
            for j in range(KQ - 1):
                dpre = dpre + cw_ref[j:j + 1, cs] * dsh_scr[j]
            dpre_ref[:, cs] = dpre.astype(BF16)
            dcw_ref[:, cs] += jnp.concatenate([_colsum(dco[:te] * taps[j][:te]) for j in range(KQ)], axis=0)
            return carry

        lax.fori_loop(0, H, head, 0, unroll=True)

    return _call(
        body, name="qkv_conv_bwd", grid=(3, nt),
        in_specs=[pl.BlockSpec((te, D), lambda s, i: (i, s)),
                  pl.BlockSpec((HALO_Q, D), lambda s, i: (jnp.maximum(i * hb - 1, 0), s)),
                  pl.BlockSpec((HALO_Q, D), lambda s, i: (jnp.minimum((i + 1) * hb, last_hb), s)),
                  pl.BlockSpec((te, D), lambda s, i: (i, s)),
                  pl.BlockSpec((HALO_Q, D), lambda s, i: (jnp.minimum((i + 1) * hb, last_hb), s)),
                  pl.BlockSpec((KQ, D), lambda s, i: (0, s)),
                  pl.BlockSpec(memory_space=pl.ANY)],
        out_specs=(pl.BlockSpec((te, D), lambda s, i: (i, s)), pl.BlockSpec((KQ, D), lambda s, i: (0, s))),
        out_shape=(jax.ShapeDtypeStruct(dproj.shape, dproj.dtype), jax.ShapeDtypeStruct((KQ, 3 * D), F32)),
        input_output_aliases={6: 0},
        scratch_shapes=[pltpu.VMEM((te + 2 * HALO_Q, D), F32), pltpu.VMEM((te + HALO_Q, D), F32),
                        pltpu.VMEM((KQ - 1, te + HALO_Q, DK), F32), pltpu.VMEM((te + HALO_Q, DK), F32),
                        pltpu.VMEM((KQ - 1, te, DK), F32)],
        compiler_params=_cp(("arbitrary", "arbitrary")),
    )(proj, proj, proj, dqkv, dqkv, conv_w, dproj)


def _ba_bwd(dbg, ba, ab, pad):
    lp = ba.shape[0]
    te = _pick(lp, (640, 320))

    def body(dbg_ref, ba_ref, ab_ref, dba_ref, dab_ref):
        i = pl.program_id(0)

        @pl.when(i == 0)
        def _():
            dab_ref[...] = jnp.zeros_like(dab_ref)

        ba, ab = ba_ref[...], ab_ref[...]
        lane = lax.broadcasted_iota(jnp.int32, ba.shape, 1)
        rows = i * te + lax.broadcasted_iota(jnp.int32, ba.shape, 0)
        g = jnp.where((lane < 2 * H) & (rows >= pad), dbg_ref[...], 0.0)
        sb = _sig(ba)
        z = ba + ab[1:2, :]
        sp = jnp.maximum(z, 0.0) + jnp.log(1.0 + jnp.exp(-jnp.abs(z)))
        nea = -jnp.exp(ab[0:1, :])
        dz = g * nea * _sig(z)
        dba_ref[...] = jnp.where(lane < H, g * sb * (1.0 - sb), dz).astype(BF16)
        is_g = (lane >= H) & (lane < 2 * H)
        dab_ref[...] += jnp.concatenate([_colsum(jnp.where(is_g, g * nea * sp, 0.0)),
                                         _colsum(jnp.where(is_g, dz, 0.0))], axis=0)

    return _call(
        body, name="ba_bwd", grid=(lp // te,),
        in_specs=[pl.BlockSpec((te, LANE), lambda i: (i, 0)), pl.BlockSpec((te, LANE), lambda i: (i, 0)),
                  pl.BlockSpec((2, LANE), lambda i: (0, 0))],
        out_specs=(pl.BlockSpec((te, LANE), lambda i: (i, 0)), pl.BlockSpec((2, LANE), lambda i: (0, 0))),
        out_shape=(jax.ShapeDtypeStruct((lp, LANE), BF16), jax.ShapeDtypeStruct((2, LANE), F32)),
        compiler_params=_cp(("arbitrary",)),
    )(dbg, ba, ab)


SUBLANES = 8
CONV_RB = 64


def _fill_shifted(sh_scr, src_scr, cs):
    n = sh_scr.shape[1]
    for s in range(1, SUBLANES):
        sh_scr[s] = src_scr[s:s + n, cs]


def _shifted(sh_scr, src_scr, cs, r, r0, n):
    s, a8 = r % SUBLANES, r - r % SUBLANES
    if s == 0:
        return src_scr[r0 + a8:r0 + a8 + n, cs]
    return sh_scr[s, r0 + a8:r0 + a8 + n, :]


def _conv_b_fwd(proj, dw_w, dw_b, ln_w, ln_b, w_cf):
    lp = proj.shape[0]
    te = _pick(lp, (320,))
    hb = te // HALO_D

    def body(a_ref, b_ref, ha_ref, hb_ref, zb_ref, w_ref, wb_ref, lw_ref, lb_ref, wcf_ref, c1_ref, c3_ref, yb_ref,
             c0_scr, sh_scr):
        i = pl.program_id(0)
        c0_scr[:HALO_D, :] = jnp.where(i > 0, ha_ref[...] * _sig(hb_ref[...]), 0.0)
        c0_scr[HALO_D:, :] = a_ref[...] * _sig(b_ref[...])
        off = HALO_D - (KD - 1)
        def lane_block(cb, carry):
            cs = pl.ds(pl.multiple_of(cb * LANE, LANE), LANE)
            _fill_shifted(sh_scr, c0_scr, cs)
            for r0 in range(0, te, CONV_RB):
                acc = None
                for j in range(KD):
                    term = w_ref[j:j + 1, cs] * _shifted(sh_scr, c0_scr, cs, off + j, r0, CONV_RB)
                    acc = term if acc is None else acc + term
                c1_ref[r0:r0 + CONV_RB, cs] = acc + wb_ref[:, cs]
            return carry

        lax.fori_loop(0, D // LANE, lane_block, 0)
        c1 = c1_ref[...]
        mu = jnp.mean(c1, axis=-1, keepdims=True)
        xc = c1 - mu
        c2 = xc * lax.rsqrt(jnp.mean(xc * xc, axis=-1, keepdims=True) + EPS) * lw_ref[...] + lb_ref[...]
        zb = zb_ref[...]
        c3 = (c2 * _sig(c2) * zb * _sig(zb)).astype(BF16)
        c3_ref[...] = c3
        yb_ref[...] = _dot(c3, wcf_ref[...])

    vec = pl.BlockSpec((1, D), lambda i: (0, 0))
    row = pl.BlockSpec((te, D), lambda i: (i, 0))
    return _call(
        body, name="conv_b_fwd", grid=(lp // te,),
        in_specs=[pl.BlockSpec((te, D), lambda i: (i, CB_GA_)), pl.BlockSpec((te, D), lambda i: (i, CB_GB_)),
                  pl.BlockSpec((HALO_D, D), lambda i: (jnp.maximum(i * hb - 1, 0), CB_GA_)),
                  pl.BlockSpec((HALO_D, D), lambda i: (jnp.maximum(i * hb - 1, 0), CB_GB_)),
                  pl.BlockSpec((te, D), lambda i: (i, CB_ZB)),
                  pl.BlockSpec((KD, D), lambda i: (0, 0)), vec, vec, vec, pl.BlockSpec((D, D), lambda i: (0, 0))],
        out_specs=(row, row, row),
        out_shape=(jax.ShapeDtypeStruct((lp, D), F32), jax.ShapeDtypeStruct((lp, D), BF16),
                   jax.ShapeDtypeStruct((lp, D), F32)),
        scratch_shapes=[pltpu.VMEM((te + HALO_D, D), F32), pltpu.VMEM((SUBLANES, te + HALO_D - SUBLANES, LANE), F32)],
        compiler_params=_cp(("parallel",)),
    )(proj, proj, proj, proj, proj, dw_w, dw_b, ln_w, ln_b, w_cf)


def _conv_b_bwd1(dy_b, w_cf, c1, proj, ln_w, ln_b, dproj):
    lp = c1.shape[0]
    te = _pick(lp, (320,))

    def body(dyb_ref, wcf_ref, c1_ref, zb_ref, lw_ref, lb_ref, _, dc1_ref, dzb_ref, sums_ref):
        @pl.when(pl.program_id(0) == 0)
        def _():
            sums_ref[...] = jnp.zeros_like(sums_ref)

        c1, g = c1_ref[...], _dot_nt(dyb_ref[...], wcf_ref[...])
        mu = jnp.mean(c1, axis=-1, keepdims=True)
        xc = c1 - mu
        rstd = lax.rsqrt(jnp.mean(xc * xc, axis=-1, keepdims=True) + EPS)
        xh = xc * rstd
        lw = lw_ref[...]
        c2 = xh * lw + lb_ref[...]
        s2 = _sig(c2)
        zb = zb_ref[...]
        sz = _sig(zb)
        dc2 = g * (zb * sz) * _dsilu(c2, s2)
        dzb_ref[...] = (g * (c2 * s2) * _dsilu(zb, sz)).astype(BF16)
        dxh = dc2 * lw
        dc1 = rstd * (dxh - jnp.mean(dxh, axis=-1, keepdims=True) - xh * jnp.mean(dxh * xh, axis=-1, keepdims=True))
        dc1_ref[...] = dc1
        sums_ref[...] += jnp.concatenate([_colsum(dc2 * xh), _colsum(dc2), _colsum(dc1)], axis=0)

    vec = pl.BlockSpec((1, D), lambda i: (0, 0))
    return _call(
        body, name="conv_b_bwd1", grid=(lp // te,),
        in_specs=[pl.BlockSpec((te, D), lambda i: (i, 0)), pl.BlockSpec((D, D), lambda i: (0, 0)),
                  pl.BlockSpec((te, D), lambda i: (i, 0)),
                  pl.BlockSpec((te, D), lambda i: (i, CB_ZB)), vec, vec, pl.BlockSpec(memory_space=pl.ANY)],
        out_specs=(pl.BlockSpec((te, D), lambda i: (i, 0)), pl.BlockSpec((te, D), lambda i: (i, CB_ZB)),
                   pl.BlockSpec((3, D), lambda i: (0, 0))),
        out_shape=(jax.ShapeDtypeStruct((lp, D), F32), jax.ShapeDtypeStruct(dproj.shape, dproj.dtype),
                   jax.ShapeDtypeStruct((3, D), F32)),
        input_output_aliases={6: 1},
        compiler_params=_cp(("arbitrary",)),
    )(dy_b, w_cf, c1, proj, ln_w, ln_b, dproj)


def _conv_b_bwd2(dc1, proj, dw_w, dproj):
    lp = dc1.shape[0]
    te = _pick(lp, (320,))
    hb = te // HALO_D
    nt = lp // te
    last_hb = lp // HALO_D - 1

    def body(g_ref, gn_ref, a_ref, b_ref, ha_ref, hb_ref, w_ref, _, dab_ref, dw_ref, c0_scr, g_scr, dc0_scr,
             csh_scr, gsh_scr):
        i = pl.program_id(0)

        @pl.when(i == 0)
        def _():
            dw_ref[...] = jnp.zeros_like(dw_ref)

        a, b = a_ref[...], b_ref[...]
        sb = _sig(b)
        c0_scr[:HALO_D, :] = jnp.where(i > 0, ha_ref[...] * _sig(hb_ref[...]), 0.0)
        c0_scr[HALO_D:, :] = a * sb
        g_scr[:te, :] = g_ref[...]
        g_scr[te:, :] = jnp.where(i < nt - 1, gn_ref[...], 0.0)
        off = HALO_D - (KD - 1)
        def lane_block(cb, carry):
            cs = pl.ds(pl.multiple_of(cb * LANE, LANE), LANE)
            _fill_shifted(csh_scr, c0_scr, cs)
            _fill_shifted(gsh_scr, g_scr, cs)
            for r0 in range(0, te, CONV_RB):
                acc = None
                for j in range(KD):
                    term = w_ref[j:j + 1, cs] * _shifted(gsh_scr, g_scr, cs, KD - 1 - j, r0, CONV_RB)
                    acc = term if acc is None else acc + term
                dc0_scr[r0:r0 + CONV_RB, cs] = acc
            parts = [None] * KD
            for r0 in range(0, te, CONV_RB):
                g = g_scr[r0:r0 + CONV_RB, cs].reshape(CONV_RB // SUBLANES, SUBLANES, LANE)
                for j in range(KD):
                    x = _shifted(csh_scr, c0_scr, cs, off + j, r0, CONV_RB)
                    p = jnp.sum(g * x.reshape(CONV_RB // SUBLANES, SUBLANES, LANE), axis=0)
                    parts[j] = p if parts[j] is None else parts[j] + p
            dw_ref[:, cs] += jnp.concatenate([_colsum(p) for p in parts], axis=0)
            return carry

        lax.fori_loop(0, D // LANE, lane_block, 0)
        dc0 = dc0_scr[...]
        dab_ref[:, :D] = (dc0 * sb).astype(BF16)
        dab_ref[:, D:] = (dc0 * a * sb * (1.0 - sb)).astype(BF16)

    return _call(
        body, name="conv_b_bwd2", grid=(nt,),
        in_specs=[pl.BlockSpec((te, D), lambda i: (i, 0)),
                  pl.BlockSpec((HALO_D, D), lambda i: (jnp.minimum((i + 1) * hb, last_hb), 0)),
                  pl.BlockSpec((te, D), lambda i: (i, CB_GA_)), pl.BlockSpec((te, D), lambda i: (i, CB_GB_)),
                  pl.BlockSpec((HALO_D, D), lambda i: (jnp.maximum(i * hb - 1, 0), CB_GA_)),
                  pl.BlockSpec((HALO_D, D), lambda i: (jnp.maximum(i * hb - 1, 0), CB_GB_)),
                  pl.BlockSpec((KD, D), lambda i: (0, 0)), pl.BlockSpec(memory_space=pl.ANY)],
        out_specs=(pl.BlockSpec((te, 2 * D), lambda i: (i, CB_GA_ // 2)), pl.BlockSpec((KD, D), lambda i: (0, 0))),
        out_shape=(jax.ShapeDtypeStruct(dproj.shape, dproj.dtype), jax.ShapeDtypeStruct((KD, D), F32)),
        input_output_aliases={7: 0},
        scratch_shapes=[pltpu.VMEM((te + HALO_D, D), F32), pltpu.VMEM((te + HALO_D, D), F32), pltpu.VMEM((te, D), F32),
                        pltpu.VMEM((SUBLANES, te + HALO_D - SUBLANES, LANE), F32),
                        pltpu.VMEM((SUBLANES, te + HALO_D - SUBLANES, LANE), F32)],
        compiler_params=_cp(("arbitrary",)),
    )(dc1, dc1, proj, proj, proj, proj, dw_w, dproj)


def _merge_fwd(y_a, y_b, proj, b_cf, w_o):
    lp = y_a.shape[0]
    te = _pick(lp, (320,))

    def body(ya_ref, yb_ref, ga_ref, gb_ref, bias_ref, wo_ref, out_ref, z_ref):
        merged = (_sig(ga_ref[...]) * ya_ref[...] + _sig(gb_ref[...]) * (yb_ref[...] + bias_ref[...])).astype(BF16)
        out_ref[...] = merged
        z_ref[...] = _dot(merged, wo_ref[...])

    row = lambda j: pl.BlockSpec((te, D), lambda i: (i, j))
    return _call(
        body, name="merge_fwd", grid=(lp // te,),
        in_specs=[row(0), row(0), row(CB_MA), row(CB_MB), pl.BlockSpec((1, D), lambda i: (0, 0)),
                  pl.BlockSpec((D, D), lambda i: (0, 0))],
        out_specs=(row(0), row(0)),
        out_shape=(jax.ShapeDtypeStruct((lp, D), BF16), jax.ShapeDtypeStruct((lp, D), F32)),
        compiler_params=_cp(("parallel",)),
    )(y_a, y_b, proj, proj, b_cf, w_o)


def _merge_bwd(dx_out_b, w_o, y_a, y_b, proj, b_cf):
    lp = y_a.shape[0]
    te = _pick(lp, (320,))

    def body(dx_ref, wo_ref, ya_ref, yb_ref, ga_ref, gb_ref, bias_ref, dya_ref, dyb_ref, dg_ref, db_ref):
        @pl.when(pl.program_id(0) == 0)
        def _():
            db_ref[...] = jnp.zeros_like(db_ref)

        dm = _dot_nt(dx_ref[...], wo_ref[...])
        sa, sb = _sig(ga_ref[...]), _sig(gb_ref[...])
        dyb = sb * dm
        dya_ref[...] = (sa * dm).astype(BF16)
        dyb_ref[...] = dyb.astype(BF16)
        dg_ref[:, :D] = (dm * ya_ref[...] * sa * (1.0 - sa)).astype(BF16)
        dg_ref[:, D:] = (dm * (yb_ref[...] + bias_ref[...]) * sb * (1.0 - sb)).astype(BF16)
        db_ref[...] += _colsum(dyb)

    row = lambda j: pl.BlockSpec((te, D), lambda i: (i, j))
    act = jax.ShapeDtypeStruct((lp, D), BF16)
    return _call(
        body, name="merge_bwd", grid=(lp // te,),
        in_specs=[row(0), pl.BlockSpec((D, D), lambda i: (0, 0)), row(0), row(0), row(CB_MA), row(CB_MB),
                  pl.BlockSpec((1, D), lambda i: (0, 0))],
        out_specs=(row(0), row(0), pl.BlockSpec((te, 2 * D), lambda i: (i, CB_MA // 2)),
                   pl.BlockSpec((1, D), lambda i: (0, 0))),
        out_shape=(act, act, jax.ShapeDtypeStruct((lp, NCB * D), BF16), jax.ShapeDtypeStruct((1, D), F32)),
        compiler_params=_cp(("arbitrary",)),
    )(dx_out_b, w_o, y_a, y_b, proj, proj, b_cf)


def _final_fwd_bwd(x_ext, z, target, final_w):
    lp = x_ext.shape[0]
    te = _pick(lp, (640,))
    nsub = te // LANE

    def body(x_ref, z_ref, *rest):
        t_refs, (w_ref, dx_ref, dxb_ref, loss_ref, dw_ref) = rest[:nsub], rest[nsub:]
        i = pl.program_id(0)

        @pl.when(i == 0)
        def _():
            loss_ref[...] = jnp.zeros_like(loss_ref)
            dw_ref[...] = jnp.zeros_like(dw_ref)

        w = w_ref[...]
        for k in range(nsub):
            rs = slice(k * LANE, (k + 1) * LANE)
            xo = x_ref[rs, :] + z_ref[rs, :]
            r = lax.rsqrt(jnp.mean(xo * xo, axis=-1, keepdims=True) + EPS)
            xhat = xo * r
            err = xhat * w - t_refs[k][...]
            if k == 0:
                err = jnp.where(i > 0, err, 0.0)
            loss_ref[...] += 0.5 * jnp.sum(jnp.mean(err * err, axis=-1, keepdims=True), keepdims=True)
            dy = err * (1.0 / D)
            dw_ref[...] += _colsum(dy * xhat)
            dxn = dy * w
            dx = r * (dxn - xhat * jnp.mean(dxn * xhat, axis=-1, keepdims=True))
            dx_ref[rs, :] = dx
            dxb_ref[rs, :] = dx.astype(BF16)

    piece = lambda k: pl.BlockSpec((LANE, D), lambda i: (jnp.maximum(i * nsub + k - 1, 0), 0))
    row = pl.BlockSpec((te, D), lambda i: (i, 0))
    return _call(
        body, name="final_fwd_bwd", grid=(lp // te,),
        in_specs=[row, row] + [piece(k) for k in range(nsub)] + [pl.BlockSpec((1, D), lambda i: (0, 0))],
        out_specs=(row, row, pl.BlockSpec((1, 1), lambda i: (0, 0)), pl.BlockSpec((1, D), lambda i: (0, 0))),
        out_shape=(jax.ShapeDtypeStruct((lp, D), F32), jax.ShapeDtypeStruct((lp, D), BF16),
                   jax.ShapeDtypeStruct((1, 1), F32), jax.ShapeDtypeStruct((1, D), F32)),
        compiler_params=_cp(("arbitrary",)),
    )(x_ext, z, *([target] * nsub), final_w)


def _prenorm_bwd(dh, x_ext, dx_out, norm_w, seq):
    lp = x_ext.shape[0]
    te = LANE

    def body(dh_ref, x_ref, dxo_ref, w_ref, gx_ref, head_ref, dw_ref):
        i = pl.program_id(0)

        @pl.when(i == 0)
        def _():
            dw_ref[...] = jnp.zeros_like(dw_ref)

        x, dh = x_ref[...], dh_ref[...]
        r = lax.rsqrt(jnp.mean(x * x, axis=-1, keepdims=True) + EPS)
        xhat = x * r
        dxn = dh * w_ref[...]
        dx = dxo_ref[...] + r * (dxn - xhat * jnp.mean(dxn * xhat, axis=-1, keepdims=True))
        dw_ref[...] += _colsum(dh * xhat)

        @pl.when(i == 0)
        def _():
            head_ref[...] = dx

        @pl.when(i > 0)
        def _():
            gx_ref[...] = dx

    row = pl.BlockSpec((te, D), lambda i: (i, 0))
    return _call(
        body, name="prenorm_bwd", grid=(lp // te,),
        in_specs=[row, row, row, pl.BlockSpec((1, D), lambda i: (0, 0))],
        out_specs=(pl.BlockSpec((te, D), lambda i: (jnp.maximum(i - 1, 0), 0)), pl.BlockSpec((te, D), lambda i: (0, 0)),
                   pl.BlockSpec((1, D), lambda i: (0, 0))),
        out_shape=(jax.ShapeDtypeStruct((seq, D), F32), jax.ShapeDtypeStruct((te, D), F32),
                   jax.ShapeDtypeStruct((1, D), F32)),
        compiler_params=_cp(("arbitrary",)),
    )(dh, x_ext, dx_out, norm_w)


def _adam_reduce(parts, w, m, v, name):
    r, n = w.shape
    tr = _pick(r, (128,)) if r % 128 == 0 else r

    def body(p_ref, w_ref, m_ref, v_ref, g_ref, d_ref, m2_ref, v2_ref):
        g = p_ref[0]
        for s in range(1, NDEV):
            g = g + p_ref[s]
        _adam_write(g, w_ref, m_ref, v_ref, g_ref, d_ref, m2_ref, v2_ref)

    blk = pl.BlockSpec((tr, n), lambda i: (i, 0))
    out = jax.ShapeDtypeStruct((r, n), F32)
    return _call(
        body, name=name, grid=(r // tr,),
        in_specs=[pl.BlockSpec((NDEV, tr, n), lambda i: (0, i, 0)), blk, blk, blk],
        out_specs=(blk, blk, blk, blk), out_shape=(out, out, out, out),
        compiler_params=_cp(("parallel",)),
    )(parts, w, m, v)


def _adam_write(g, w_ref, m_ref, v_ref, g_ref, d_ref, m2_ref, v2_ref):
    c1 = 1.0 - ADAM_B1 ** ADAM_STEP
    c2 = 1.0 - ADAM_B2 ** ADAM_STEP
    m2 = ADAM_B1 * m_ref[...] + (1.0 - ADAM_B1) * g
    v2 = ADAM_B2 * v_ref[...] + (1.0 - ADAM_B2) * (g * g)
    g_ref[...] = g
    m2_ref[...] = m2
    v2_ref[...] = v2
    d_ref[...] = -ADAM_LR * ((m2 / c1) / (jnp.sqrt(v2 / c2) + ADAM_EPS) + ADAM_WD * w_ref[...])


def _adam_chips(own, recv, w, m, v, name):
    r, n = w.shape
    tr, tc = _shard_tile(r, n)

    def body(own_ref, p_ref, w_ref, m_ref, v_ref, g_ref, d_ref, m2_ref, v2_ref):
        my_chip = 2 * lax.axis_index("x") + lax.axis_index("y")
        g = None
        for j in range(NCHIP):
            part = jnp.where(my_chip == j, own_ref[...], p_ref[j].astype(F32))
            g = part if g is None else g + part
        _adam_write(g, w_ref, m_ref, v_ref, g_ref, d_ref, m2_ref, v2_ref)

    blk = pl.BlockSpec((tr, tc), lambda i, k: (i, k))
    out = jax.ShapeDtypeStruct((r, n), F32)
    return _call(
        body, name=name, grid=(r // tr, n // tc),
        in_specs=[blk, pl.BlockSpec((NCHIP, tr, tc), lambda i, k: (0, i, k)), blk, blk, blk],
        out_specs=(blk, blk, blk, blk), out_shape=(out, out, out, out),
        compiler_params=_cp(("parallel", "parallel")),
    )(own, recv, w, m, v)


SMALL = ("norm_w", "a_log", "dt_bias", "dn_norm_w", "dw_b", "ln_w", "ln_b", "b_cf_out", "final_norm_w")


def kernel(x, meta, norm_w, w_in, conv_qkv_w, a_log, dt_bias, dn_norm_w, w_dn_out, dw_w, dw_b, ln_w, ln_b, w_cf_out, b_cf_out, w_o, final_norm_w, loss_target, m_meta, m_norm_w, m_w_in, m_conv_qkv_w, m_a_log, m_dt_bias, m_dn_norm_w, m_w_dn_out, m_dw_w, m_dw_b, m_ln_w, m_ln_b, m_w_cf_out, m_b_cf_out, m_w_o, m_final_norm_w, v_meta, v_norm_w, v_w_in, v_conv_qkv_w, v_a_log, v_dt_bias, v_dn_norm_w, v_w_dn_out, v_dw_w, v_dw_b, v_ln_w, v_ln_b, v_w_cf_out, v_b_cf_out, v_w_o, v_final_norm_w):
    seq = x.shape[1]
    pad = (-(seq + NMETA)) % LANE
    in_w = w_in.shape[2] * NDEV
    n_qkvz = 4 * D
    n_ba = 2 * H

    w_in_g, w_dn_g, w_cf_g, w_o_g, meta_g, cqw_g, dww_g = _gather_two_level(
        [w_in[0].astype(BF16).T, w_dn_out[0].astype(BF16), w_cf_out[0].astype(BF16), w_o[0].astype(BF16),
         meta, conv_qkv_w[0], dw_w[0]], "gather_weights")
    w_full_t = w_in_g.reshape(in_w, D)
    c_glu = n_qkvz + n_ba
    c_zb, c_mg = c_glu + 2 * D, c_glu + 3 * D
    w_main_t = jnp.concatenate([w_full_t[:n_qkvz], w_full_t[c_glu:c_zb], w_full_t[c_mg:], w_full_t[c_zb:c_mg]],
                               axis=0)
    w_ba_t = jnp.pad(w_full_t[n_qkvz:n_qkvz + n_ba], ((0, LANE - n_ba), (0, 0)))
    w_dn, w_cf, w_oo = (t.reshape(D, D) for t in (w_dn_g, w_cf_g, w_o_g))
    meta_full = jnp.transpose(meta_g, (1, 0, 2)).reshape(NMETA, D)
    cqw = jnp.transpose(cqw_g, (1, 0, 2)).reshape(KQ, 3 * D)
    dww = jnp.transpose(dww_g, (1, 0, 2)).reshape(KD, D)
    ab = jnp.pad(jnp.concatenate([a_log, dt_bias], axis=0), ((0, 0), (H, LANE - 2 * H)))

    x_ext = jnp.concatenate([jnp.zeros((pad, D), F32), meta_full, x[0]], axis=0)

    proj, ba, h = _proj_fwd(x_ext, norm_w, w_main_t, w_ba_t)
    qkv, bg = _qkv_conv_fwd(proj, ba, cqw, ab, pad)
    o, sall, tall = _delta_fwd(qkv, bg)
    o_n, y_a = _o_post_fwd(o, proj, dn_norm_w, w_dn)
    c1, c3, y_b = _conv_b_fwd(proj, dww, dw_b, ln_w, ln_b, w_cf)
    merged, z = _merge_fwd(y_a, y_b, proj, b_cf_out, w_oo)
    dx_out, dx_out_b, loss_part, g_final_w = _final_fwd_bwd(x_ext, z, loss_target[0], final_norm_w.reshape(1, D))

    g_w_o = _mm_tn(merged, dx_out_b, "g_w_o_mm")
    dy_a, dy_b, dproj, g_b_cf = _merge_bwd(dx_out_b, w_oo, y_a, y_b, proj, b_cf_out)
    g_w_cf = _mm_tn(c3, dy_b, "g_w_cf_mm")
    g_w_dn = _mm_tn(o_n, dy_a, "g_w_dn_mm")
    dc1, dproj, sums_b = _conv_b_bwd1(dy_b, w_cf, c1, proj, ln_w, ln_b, dproj)
    dproj, g_dw_w = _conv_b_bwd2(dc1, proj, dww, dproj)
    do, dproj, g_dn_w = _o_post_bwd(dy_a, w_dn, o, proj, dn_norm_w, dproj)
    dqkv, dbg = _delta_bwd(qkv, bg, sall, tall, do)
    dproj, g_cqw = _qkv_conv_bwd(proj, dqkv, cqw, dproj)
    dba, dab = _ba_bwd(dbg, ba, ab, pad)
    g_w_main_t = _mm_tn(dproj, h, "g_w_main_mm")
    g_w_ba_t = _mm_tn(dba, h, "g_w_ba_mm")

    g_w_full_t = jnp.concatenate([g_w_main_t[:n_qkvz], g_w_ba_t[:n_ba], g_w_main_t[CB_GA_ * D:CB_MA * D],
                                  g_w_main_t[CB_ZB * D:], g_w_main_t[CB_MA * D:CB_ZB * D]], axis=0)
    big = [g_w_full_t.reshape(NDEV, in_w // NDEV, D), g_w_dn.reshape(NDEV, D // NDEV, D),
           g_w_cf.reshape(NDEV, D // NDEV, D), g_w_o.reshape(NDEV, D // NDEV, D)]
    from_sibling = _swap_sibling(big, "swap_sibling")
    pairs = [_pair_add(a, g, f"pair_add_{i}") for i, (a, g) in enumerate(zip(big, from_sibling))]
    send_sems, recv_sems, pair_thru, land_thru, token = _scatter_chips_start(
        [p for p, _ in pairs], [jnp.zeros(p.shape, p.dtype) for p, _ in pairs])

    dh = _dh_mm(dproj, dba, w_main_t, w_ba_t + token[0, 0].astype(BF16))
    grad_x, dhead, g_norm_w = _prenorm_bwd(dh, x_ext, dx_out, norm_w, seq)
    from_chips = _scatter_chips_wait(send_sems, recv_sems, pair_thru, land_thru, g_norm_w)

    split_cols = lambda t: jnp.transpose(t.reshape(t.shape[0], NDEV, t.shape[1] // NDEV), (1, 0, 2))
    small = {"norm_w": g_norm_w, "a_log": dab[0:1, H:2 * H], "dt_bias": dab[1:2, H:2 * H], "dn_norm_w": g_dn_w,
             "dw_b": sums_b[2:3], "ln_w": sums_b[0:1], "ln_b": sums_b[1:2], "b_cf_out": g_b_cf,
             "final_norm_w": g_final_w}
    small_vec = jnp.concatenate([small[k] for k in SMALL], axis=1)
    ns = small_vec.shape[1]
    ns_pad = (-ns) % LANE
    small_vec = jnp.pad(small_vec, ((0, 0), (0, ns_pad)))
    p_meta, p_cqw, p_dww, p_small = _exchange(
        [split_cols(dhead[pad:pad + NMETA]), split_cols(g_cqw), split_cols(g_dw_w), small_vec],
        [True] * 3 + [False], "exchange_small")

    res = {}
    res["w_in"] = tuple(t.T for t in _adam_chips(pairs[0][1], from_chips[0], w_in[0].T, m_w_in[0].T, v_w_in[0].T,
                                                   "adam_w_in"))
    res["w_dn_out"] = _adam_chips(pairs[1][1], from_chips[1], w_dn_out[0], m_w_dn_out[0], v_w_dn_out[0], "adam_w_dn")
    res["w_cf_out"] = _adam_chips(pairs[2][1], from_chips[2], w_cf_out[0], m_w_cf_out[0], v_w_cf_out[0], "adam_w_cf")
    res["w_o"] = _adam_chips(pairs[3][1], from_chips[3], w_o[0], m_w_o[0], v_w_o[0], "adam_w_o")
    res["meta"] = _adam_reduce(p_meta, meta, m_meta, v_meta, "adam_meta")
    res["conv_qkv_w"] = _adam_reduce(p_cqw, conv_qkv_w[0], m_conv_qkv_w[0], v_conv_qkv_w[0], "adam_conv_qkv_w")
    res["dw_w"] = _adam_reduce(p_dww, dw_w[0], m_dw_w[0], v_dw_w[0], "adam_dw_w")
    loc = dict(norm_w=(norm_w, m_norm_w, v_norm_w), a_log=(a_log, m_a_log, v_a_log), dt_bias=(dt_bias, m_dt_bias, v_dt_bias),
               dn_norm_w=(dn_norm_w, m_dn_norm_w, v_dn_norm_w), dw_b=(dw_b, m_dw_b, v_dw_b), ln_w=(ln_w, m_ln_w, v_ln_w),
               ln_b=(ln_b, m_ln_b, v_ln_b), b_cf_out=(b_cf_out, m_b_cf_out, v_b_cf_out),
               final_norm_w=(final_norm_w, m_final_norm_w, v_final_norm_w))
    cat = lambda j: jnp.pad(jnp.concatenate([loc[k][j].reshape(1, -1) for k in SMALL], axis=1), ((0, 0), (0, ns_pad)))
    small_res = _adam_reduce(p_small, cat(0), cat(1), cat(2), "adam_small")
    off = 0
    for k in SMALL:
        wshape = loc[k][0].shape
        nk = loc[k][0].size
        res[k] = tuple(t[:, off:off + nk].reshape(wshape) for t in small_res)
        off += nk
    shaped = dict(w_in=w_in.shape, w_dn_out=w_dn_out.shape, w_cf_out=w_cf_out.shape, w_o=w_o.shape, meta=meta.shape,
                  conv_qkv_w=conv_qkv_w.shape, dw_w=dw_w.shape)
    for k, shp in shaped.items():
        res[k] = tuple(t.reshape(shp) for t in res[k])

    loss = lax.psum(loss_part[0, 0], ("x", "y", "c"))
    order = ("meta", "norm_w", "w_in", "conv_qkv_w", "a_log", "dt_bias", "dn_norm_w", "w_dn_out", "dw_w", "dw_b", "ln_w",
             "ln_b", "w_cf_out", "b_cf_out", "w_o", "final_norm_w")
    outs = [loss, grad_x[None]]
    for j in range(4):
        outs += [res[k][j] for k in order]
    return tuple(outs)
```

```python
import functools

import jax
import jax.numpy as jnp
from jax import lax
from jax.experimental import pallas as pl
from jax.experimental.pallas import tpu as pltpu

F32 = jnp.float32
BF16 = jnp.bfloat16
HI = lax.Precision.HIGHEST

D = 1024
H = 8
DK = 128
C = 64
NMETA = 16
KQ = 4
KD = 31
HALO_Q = 8
HALO_D = 32
EPS = 1e-6
NDEV = 8
LANE = 128
MIB = 1024 * 1024

ADAM_LR, ADAM_B1, ADAM_B2, ADAM_EPS, ADAM_WD, ADAM_STEP = 0.001, 0.9, 0.999, 1e-08, 0.01, 10

CB_Q, CB_K, CB_V, CB_ZA, CB_GA_, CB_GB_, CB_MA, CB_MB, CB_ZB = range(9)
NCB = 9


def _pick(n, cands):
    for c in cands:
        if n % c == 0:
            return c
    raise ValueError(f"no tile for {n}")


def _cp(sem=None, vmem_mib=40):
    kw = dict(vmem_limit_bytes=vmem_mib * MIB)
    if sem is not None:
        kw["dimension_semantics"] = sem
    return pltpu.CompilerParams(**kw)


def _call(body, **kw):
    return pl.pallas_call(body, **kw)


def _dot(a, b):
    return jnp.dot(a.astype(BF16), b.astype(BF16), preferred_element_type=F32)


def _dot_nt(a, b):
    return lax.dot_general(a.astype(BF16), b.astype(BF16), (((1,), (1,)), ((), ())), preferred_element_type=F32)


def _dot_tn(a, b):
    return lax.dot_general(a.astype(BF16), b.astype(BF16), (((0,), (0,)), ((), ())), preferred_element_type=F32)


def _dot_hi(a, b):
    return jnp.dot(a, b, precision=HI, preferred_element_type=F32)


def _sig(x):
    return 0.5 * jnp.tanh(0.5 * x) + 0.5


def _dsilu(x, s):
    return s * (1.0 + x * (1.0 - s))


def _rowsum(x):
    return jnp.sum(x, axis=-1, keepdims=True)


def _colsum(x):
    return jnp.sum(x, axis=0, keepdims=True)


def _exchange(arrs, scatter, name):
    n = len(arrs)
    out_shape = []
    for a, sc in zip(arrs, scatter):
        shp = a.shape if sc else (NDEV,) + a.shape
        out_shape.append(jax.ShapeDtypeStruct(shp, a.dtype))

    def body(*refs):
        ins, outs = refs[:n], refs[n:2 * n]
        send_sems, recv_sems, loc_sems = refs[2 * n:]
        x, y, c = lax.axis_index("x"), lax.axis_index("y"), lax.axis_index("c")
        me = 4 * x + 2 * y + c
        copies = []
        for a in range(n):
            for k in range(1, NDEV):
                px = 1 - x if (k >> 2) & 1 else x
                py = 1 - y if (k >> 1) & 1 else y
                pc = 1 - c if k & 1 else c
                src = ins[a].at[4 * px + 2 * py + pc] if scatter[a] else ins[a]
                cp = pltpu.make_async_remote_copy(
                    src_ref=src, dst_ref=outs[a].at[me],
                    send_sem=send_sems.at[a * (NDEV - 1) + k - 1], recv_sem=recv_sems.at[a * (NDEV - 1) + k - 1],
                    device_id=(px, py, pc), device_id_type=pl.DeviceIdType.MESH)
                cp.start()
                copies.append(cp)
            loc = pltpu.make_async_copy(ins[a].at[me] if scatter[a] else ins[a], outs[a].at[me], loc_sems.at[a])
            loc.start()
            copies.append(loc)
        for cp in copies:
            cp.wait()

    any_spec = pl.BlockSpec(memory_space=pl.ANY)
    return _call(
        body, name=name, out_shape=tuple(out_shape),
        in_specs=[any_spec] * n, out_specs=tuple([any_spec] * n),
        scratch_shapes=[pltpu.SemaphoreType.DMA((n * (NDEV - 1),)), pltpu.SemaphoreType.DMA((n * (NDEV - 1),)),
                        pltpu.SemaphoreType.DMA((n,))],
    )(*arrs)


NCHIP = 4


def _gather_two_level(arrs, name):
    n = len(arrs)
    per = NDEV - 1

    def body(*refs):
        ins, outs = refs[:n], refs[n:2 * n]
        send_sems, recv_sems, loc_sems = refs[2 * n:]
        x, y, c = lax.axis_index("x"), lax.axis_index("y"), lax.axis_index("c")
        me, sibling = (x, y, c), (x, y, 1 - c)
        chips = [(1 - x, y), (x, 1 - y), (1 - x, 1 - y)]

        def slot(a, px, py, pc):
            return outs[a].at[4 * px + 2 * py + pc]

        def copy(a, k, block, to, src=None):
            return pltpu.make_async_remote_copy(
                src_ref=slot(a, *block) if src is None else src, dst_ref=slot(a, *block),
                send_sem=send_sems.at[a * per + k], recv_sem=recv_sems.at[a * per + k],
                device_id=to, device_id_type=pl.DeviceIdType.MESH)

        local, sent = [], []
        for a in range(n):
            mine = pltpu.make_async_copy(ins[a], slot(a, *me), loc_sems.at[a])
            mine.start()
            local.append(mine)
            first = [copy(a, 1 + j, me, (*chip, c), src=ins[a]) for j, chip in enumerate(chips)]
            first.append(copy(a, 0, me, sibling, src=ins[a]))
            for cp in first:
                cp.start()
            sent += first
        for j, chip in enumerate(chips):
            for a in range(n):
                copy(a, 1 + j, (*chip, c), me).wait_recv()
                cp = copy(a, 4 + j, (*chip, c), sibling)
                cp.start()
                sent.append(cp)
        for a in range(n):
            copy(a, 0, sibling, me).wait_recv()
            for j, chip in enumerate(chips):
                copy(a, 4 + j, (*chip, 1 - c), me).wait_recv()
        for cp in sent:
            cp.wait_send()
        for cp in local:
            cp.wait()

    any_spec = pl.BlockSpec(memory_space=pl.ANY)
    return _call(
        body, name=name, out_shape=tuple(jax.ShapeDtypeStruct((NDEV,) + a.shape, a.dtype) for a in arrs),
        in_specs=[any_spec] * n, out_specs=tuple([any_spec] * n),
        scratch_shapes=[pltpu.SemaphoreType.DMA((n * per,)), pltpu.SemaphoreType.DMA((n * per,)),
                        pltpu.SemaphoreType.DMA((n,))],
    )(*arrs)


def _swap_sibling(arrs, name):
    n = len(arrs)

    def body(*refs):
        ins, outs = refs[:n], refs[n:2 * n]
        send_sems, recv_sems = refs[2 * n:]
        x, y, c = lax.axis_index("x"), lax.axis_index("y"), lax.axis_index("c")
        copies = []
        for a in range(n):
            for j in range(NCHIP):
                cp = pltpu.make_async_remote_copy(
                    src_ref=ins[a].at[j, 1 - c], dst_ref=outs[a].at[j],
                    send_sem=send_sems.at[a * NCHIP + j], recv_sem=recv_sems.at[a * NCHIP + j],
                    device_id=(x, y, 1 - c), device_id_type=pl.DeviceIdType.MESH)
                cp.start()
                copies.append(cp)
        for cp in copies:
            cp.wait()

    any_spec = pl.BlockSpec(memory_space=pl.ANY)
    return _call(
        body, name=name, out_shape=tuple(jax.ShapeDtypeStruct((NCHIP,) + a.shape[2:], a.dtype) for a in arrs),
        in_specs=[any_spec] * n, out_specs=tuple([any_spec] * n),
        scratch_shapes=[pltpu.SemaphoreType.DMA((n * NCHIP,)), pltpu.SemaphoreType.DMA((n * NCHIP,))],
    )(*arrs)


def _pair_add(arr4, got, name):
    _, _, r, n = arr4.shape
    tr, tc = _shard_tile(r, n)

    def body(a_ref, g_ref, p_ref, own_ref):
        c = lax.axis_index("c")
        my_chip = 2 * lax.axis_index("x") + lax.axis_index("y")
        s = jnp.where(c == 0, a_ref[0, 0], a_ref[0, 1]) + g_ref[0]
        p_ref[0] = s.astype(BF16)

        @pl.when(pl.program_id(2) == my_chip)
        def _():
            own_ref[...] = s

    return _call(
        body, name=name, grid=(r // tr, n // tc, NCHIP),
        in_specs=[pl.BlockSpec((1, 2, tr, tc), lambda i, k, j: (j, 0, i, k)),
                  pl.BlockSpec((1, tr, tc), lambda i, k, j: (j, i, k))],
        out_specs=(pl.BlockSpec((1, tr, tc), lambda i, k, j: (j, i, k)), pl.BlockSpec((tr, tc), lambda i, k, j: (i, k))),
        out_shape=(jax.ShapeDtypeStruct((NCHIP, r, n), BF16), jax.ShapeDtypeStruct((r, n), F32)),
        compiler_params=_cp(("parallel", "parallel", "arbitrary")),
    )(arr4, got)


def _shard_tile(r, n):
    return (128, n) if r % 128 == 0 else (r, 256)


def _all_copies(srcs, lands, send_sems, recv_sems):
    x, y, c = lax.axis_index("x"), lax.axis_index("y"), lax.axis_index("c")
    per = NDEV - 1
    copies = []
    for a in range(len(srcs)):
        for k in range(1, NDEV):
            px = 1 - x if (k >> 2) & 1 else x
            py = 1 - y if (k >> 1) & 1 else y
            pc = 1 - c if k & 1 else c
            copies.append(pltpu.make_async_remote_copy(
                src_ref=srcs[a], dst_ref=lands[a].at[4 * x + 2 * y + c],
                send_sem=send_sems.at[a * per + k - 1], recv_sem=recv_sems.at[a * per + k - 1],
                device_id=(px, py, pc), device_id_type=pl.DeviceIdType.MESH))
    return copies


def _split_start(make_copies, peers, name, arrs, lands):
    n = len(arrs)
    nsem = n * peers

    def body(*refs):
        srcs, land_in = refs[:n], refs[n:2 * n]
        send_sems, recv_sems = refs[2 * n:2 * n + 2]
        token = refs[-1]
        for cp in make_copies(srcs, land_in, send_sems, recv_sems):
            cp.start()
        token[...] = jnp.zeros_like(token)

    hbm = pl.BlockSpec(memory_space=pltpu.HBM)
    sem = pl.BlockSpec(memory_space=pltpu.SEMAPHORE)
    both = list(arrs) + list(lands)
    outs = _call(
        body, name=name,
        out_shape=(pltpu.SemaphoreType.DMA((nsem,)), pltpu.SemaphoreType.DMA((nsem,)),
                   *[pltpu.HBM(a.shape, a.dtype) for a in both], jax.ShapeDtypeStruct((SUBLANES, LANE), F32)),
        in_specs=[hbm] * (2 * n), out_specs=(sem, sem, *[hbm] * (2 * n), pl.BlockSpec(memory_space=pltpu.VMEM)),
        input_output_aliases={i: i + 2 for i in range(2 * n)},
        compiler_params=pltpu.CompilerParams(has_side_effects=pltpu.SideEffectType.DATAFLOW_SIDE_EFFECTING),
    )(*[pltpu.with_memory_space_constraint(t, pltpu.HBM) for t in both])
    return outs[0], outs[1], outs[2:2 + n], outs[2 + n:2 + 2 * n], outs[-1]


def _split_wait(make_copies, name, send_sems, recv_sems, arrs, lands, after):
    n = len(arrs)

    def body(*refs):
        srcs, land_in = refs[:n], refs[n:2 * n]
        send, recv = refs[2 * n], refs[2 * n + 1]
        for cp in make_copies(srcs, land_in, send, recv):
            cp.wait_send()
            cp.wait_recv()

    hbm = pl.BlockSpec(memory_space=pltpu.HBM)
    sem = pl.BlockSpec(memory_space=pltpu.SEMAPHORE)
    both = list(arrs) + list(lands)
    outs = _call(
        body, name=name,
        out_shape=tuple(pltpu.HBM(a.shape, a.dtype) for a in both),
        in_specs=[hbm] * (2 * n) + [sem, sem, pl.BlockSpec(memory_space=pl.ANY)], out_specs=tuple([hbm] * (2 * n)),
        input_output_aliases={i: i for i in range(2 * n)},
        compiler_params=pltpu.CompilerParams(has_side_effects=pltpu.SideEffectType.DATAFLOW_SIDE_EFFECTING),
    )(*both, send_sems, recv_sems, after)
    return outs[n:]


def _chip_copies(srcs, lands, send_sems, recv_sems):
    x, y, c = lax.axis_index("x"), lax.axis_index("y"), lax.axis_index("c")
    per = NCHIP - 1
    copies = []
    for a in range(len(srcs)):
        for k in range(1, NCHIP):
            px = 1 - x if (k >> 1) & 1 else x
            py = 1 - y if k & 1 else y
            copies.append(pltpu.make_async_remote_copy(
                src_ref=srcs[a].at[2 * px + py], dst_ref=lands[a].at[2 * x + y],
                send_sem=send_sems.at[a * per + k - 1], recv_sem=recv_sems.at[a * per + k - 1],
                device_id=(px, py, c), device_id_type=pl.DeviceIdType.MESH))
    return copies


def _mm_tn(a, b, name):
    t, m = a.shape
    n = b.shape[1]
    tt = _pick(t, (1664, 640, 128))
    tm = _pick(m, (1024, 512, 128))
    tn = _pick(n, (1152, 1024, 512, 128))
    nt = t // tt

    def body(a_ref, b_ref, o_ref):
        s = pl.program_id(2)
        part = _dot_tn(a_ref[...], b_ref[...])

        @pl.when(s == 0)
        def _():
            o_ref[...] = part

        @pl.when(s > 0)
        def _():
            o_ref[...] += part

    return _call(
        body, name=name, grid=(m // tm, n // tn, nt),
        in_specs=[pl.BlockSpec((tt, tm), lambda i, j, s: (s, i)), pl.BlockSpec((tt, tn), lambda i, j, s: (s, j))],
        out_specs=pl.BlockSpec((tm, tn), lambda i, j, s: (i, j)),
        out_shape=jax.ShapeDtypeStruct((m, n), F32),
        compiler_params=_cp(("parallel", "parallel", "arbitrary")),
    )(a, b)


def _proj_fwd(x_ext, norm_w, w_main_t, w_ba_t):
    lp = x_ext.shape[0]
    n = w_main_t.shape[0]
    tm = _pick(lp, (832, 640, 320))
    tn = 1024

    def body(x_ref, nw_ref, w_ref, wba_ref, proj_ref, ba_ref, h_ref):
        @pl.when(pl.program_id(1) == 0)
        def _():
            x = x_ref[...]
            r = lax.rsqrt(jnp.mean(x * x, axis=-1, keepdims=True) + EPS)
            h = (x * r * nw_ref[...]).astype(BF16)
            h_ref[...] = h
            ba_ref[...] = _dot_nt(h, wba_ref[...])

        proj_ref[...] = _dot_nt(h_ref[...], w_ref[...])

    return _call(
        body, name="proj_fwd", grid=(lp // tm, n // tn),
        in_specs=[pl.BlockSpec((tm, D), lambda i, j: (i, 0)), pl.BlockSpec((1, D), lambda i, j: (0, 0)),
                  pl.BlockSpec((tn, D), lambda i, j: (j, 0)), pl.BlockSpec((LANE, D), lambda i, j: (0, 0))],
        out_specs=(pl.BlockSpec((tm, tn), lambda i, j: (i, j)), pl.BlockSpec((tm, LANE), lambda i, j: (i, 0)),
                   pl.BlockSpec((tm, D), lambda i, j: (i, 0))),
        out_shape=(jax.ShapeDtypeStruct((lp, n), F32), jax.ShapeDtypeStruct((lp, LANE), F32),
                   jax.ShapeDtypeStruct((lp, D), BF16)),
        compiler_params=_cp(("parallel", "arbitrary")),
    )(x_ext, norm_w, w_main_t, w_ba_t)


def _dh_mm(dproj, dba, w_main_t, w_ba_t):
    lp, n = dproj.shape
    tm = _pick(lp, (832, 640, 320))
    tn = 1024
    tk = 2304
    nk = n // tk

    def body(a_ref, ba_ref, b_ref, bba_ref, o_ref, acc):
        kk = pl.program_id(2)

        @pl.when(kk == 0)
        def _():
            acc[...] = jnp.dot(ba_ref[...], bba_ref[...], preferred_element_type=F32)

        acc[...] += jnp.dot(a_ref[...], b_ref[...], preferred_element_type=F32)

        @pl.when(kk == nk - 1)
        def _():
            o_ref[...] = acc[...]

    return _call(
        body, name="dh_mm", grid=(lp // tm, D // tn, nk),
        in_specs=[pl.BlockSpec((tm, tk), lambda i, j, kk: (i, kk)), pl.BlockSpec((tm, LANE), lambda i, j, kk: (i, 0)),
                  pl.BlockSpec((tk, tn), lambda i, j, kk: (kk, j)), pl.BlockSpec((LANE, tn), lambda i, j, kk: (0, j))],
        out_specs=pl.BlockSpec((tm, tn), lambda i, j, kk: (i, j)),
        out_shape=jax.ShapeDtypeStruct((lp, D), F32),
        scratch_shapes=[pltpu.VMEM((tm, tn), F32)],
        compiler_params=_cp(("parallel", "parallel", "arbitrary")),
    )(dproj, dba, w_main_t, w_ba_t)


def _beta_g(ba, ab, row0, pad):
    lane = lax.broadcasted_iota(jnp.int32, ba.shape, 1)
    rows = row0 + lax.broadcasted_iota(jnp.int32, ba.shape, 0)
    z = ba + ab[1:2, :]
    sp = jnp.maximum(z, 0.0) + jnp.log(1.0 + jnp.exp(-jnp.abs(z)))
    val = jnp.where(lane < H, _sig(ba), -jnp.exp(ab[0:1, :]) * sp)
    return jnp.where((lane < 2 * H) & (rows >= pad), val, 0.0)


def _qkv_conv_fwd(proj, ba, conv_w, ab, pad):
    lp = proj.shape[0]
    te = _pick(lp, (320,))
    hb = te // HALO_Q

    def body(main_ref, halo_ref, cw_ref, ba_ref, ab_ref, out_ref, bg_ref, pre_scr, tap_scr):
        i, s = pl.program_id(0), pl.program_id(1)
        pre_scr[:HALO_Q, :] = jnp.where(i > 0, halo_ref[...], 0.0)
        pre_scr[HALO_Q:, :] = main_ref[...]
        scale = jnp.where(s == 0, DK ** -0.5, 1.0)
        off = HALO_Q - (KQ - 1)

        def head(h, carry):
            cs = pl.ds(pl.multiple_of(h * DK, DK), DK)
            for j in range(KQ - 1):
                tap_scr[j] = pre_scr[off + j:off + j + te, cs]
            co = cw_ref[KQ - 1:KQ, cs] * pre_scr[HALO_Q:, cs]
            for j in range(KQ - 1):
                co = co + cw_ref[j:j + 1, cs] * tap_scr[j]
            a = co * _sig(co)
            r = lax.rsqrt(_rowsum(a * a) + EPS)
            out_ref[:, cs] = jnp.where(s == 2, a, a * (r * scale))
            return carry

        lax.fori_loop(0, H, head, 0, unroll=True)

        @pl.when(s == 0)
        def _():
            bg_ref[...] = _beta_g(ba_ref[...], ab_ref[...], i * te, pad)

    return _call(
        body, name="qkv_conv_fwd", grid=(lp // te, 3),
        in_specs=[pl.BlockSpec((te, D), lambda i, s: (i, s)),
                  pl.BlockSpec((HALO_Q, D), lambda i, s: (jnp.maximum(i * hb - 1, 0), s)),
                  pl.BlockSpec((KQ, D), lambda i, s: (0, s)),
                  pl.BlockSpec((te, LANE), lambda i, s: (i, 0)),
                  pl.BlockSpec((2, LANE), lambda i, s: (0, 0))],
        out_specs=(pl.BlockSpec((te, D), lambda i, s: (i, s)), pl.BlockSpec((te, LANE), lambda i, s: (i, 0))),
        out_shape=(jax.ShapeDtypeStruct((lp, 3 * D), F32), jax.ShapeDtypeStruct((lp, LANE), F32)),
        scratch_shapes=[pltpu.VMEM((te + HALO_Q, D), F32), pltpu.VMEM((KQ - 1, te, DK), F32)],
        compiler_params=_cp(("parallel", "arbitrary")),
    )(proj, proj, conv_w, ba, ab)


def _tri_masks():
    row = lax.broadcasted_iota(jnp.int32, (C, C), 0)
    col = lax.broadcasted_iota(jnp.int32, (C, C), 1)
    return row, col


def _split(a):
    hi = a.astype(BF16)
    return hi, (a - hi.astype(F32)).astype(BF16)


def _dot3(a, b, dims=(((1,), (0,)), ((), ()))):
    (ah, al), (bh, bl) = a, b
    mm = lambda x, y: lax.dot_general(x, y, dims, preferred_element_type=F32)
    return mm(ah, bh) + (mm(ah, bl) + mm(al, bh))


CHUNKS_PER_STEP = 2
CHUNKS_PER_STEP_BWD = 1
TINV_BLOCK = 16


def _tinv(ns, row, col):
    eye = (row == col).astype(F32)
    sh = TINV_BLOCK.bit_length() - 1
    same16 = (row >> sh) == (col >> sh)
    same32 = (row >> (sh + 1)) == (col >> (sh + 1))
    ys = [jnp.where(same16, -n, 0.0) for n in ns]
    ts = [eye + y for y in ys]
    sp = [_split(y) for y in ys]
    for level in range(3):
        yks = [_dot3(s, s) for s in sp]
        sp = [_split(yk) for yk in yks]
        ts = [t + _dot3(s, _split(t)) for s, t in zip(sp, ts)]
    for mask in (same32 & ~same16, ~same32):
        tsp = [_split(t) for t in ts]
        inner = [_dot3(_split(jnp.where(mask, n, 0.0)), t) for n, t in zip(ns, tsp)]
        ts = [t - _dot3(tp, _split(a)) for t, tp, a in zip(ts, tsp, inner)]
    return ts


def _chunk_common(q, k, v, bcol, gcc, gcr, incl, strict):
    dm = jnp.where(incl, jnp.exp(gcc - gcr), 0.0)
    kk = _dot_nt(k, k)
    qk = _dot_nt(q, k)
    gccw = jnp.broadcast_to(gcc, (C, DK))
    egc = jnp.exp(gccw)
    glast = gccw[C - 1:C, :]
    eend = jnp.exp(glast - gccw)
    elast = jnp.exp(glast)
    rhs = jnp.concatenate([v * bcol, k * (bcol * egc)], axis=1)
    return dm, kk, qk, egc, eend, elast, rhs


def _delta_fwd(qkv, bg):
    lp = qkv.shape[0]
    nc = lp // C
    heads = range(H)
    sls = [slice(h * DK, (h + 1) * DK) for h in heads]

    def body(q_ref, k_ref, v_ref, bg_ref, o_ref, sall_ref, tall_ref, s_scr):
        @pl.when(pl.program_id(0) == 0)
        def _():
            s_scr[...] = jnp.zeros_like(s_scr)

        row, col = _tri_masks()
        incl, strict = row >= col, row > col

        def prepare(sub):
            rs = slice(sub * C, (sub + 1) * C)
            bgt = bg_ref[rs, :]
            gc_all = _dot_hi(incl.astype(F32), bgt)
            gc_t = _dot_hi(bgt.T, (row <= col).astype(F32))
            qs, ks, vs = ([r[rs, sl] for sl in sls] for r in (q_ref, k_ref, v_ref))
            bcols = [jnp.broadcast_to(bgt[:, h:h + 1], (C, DK)) for h in heads]
            cm = [_chunk_common(qs[h], ks[h], vs[h], bcols[h], gc_all[:, H + h:H + h + 1], gc_t[H + h:H + h + 1, :],
                                incl, strict) for h in heads]
            dms, kks, qks, egcs, eends, elasts, rhss = zip(*cm)
            ts = _tinv([jnp.where(strict, bcols[h][:, :C] * kks[h] * dms[h], 0.0) for h in heads], row, col)
            sols = [_dot3(_split(ts[h]), _split(rhss[h])) for h in heads]
            qgs = [(qs[h] * egcs[h]).astype(BF16) for h in heads]
            ps = [(qks[h] * dms[h]).astype(BF16) for h in heads]
            kends = [(ks[h] * eends[h]).astype(BF16) for h in heads]
            return ts, sols, qgs, ps, kends, elasts

        prepared = [prepare(sub) for sub in range(CHUNKS_PER_STEP)]
        ss = [s_scr[h] for h in heads]
        for sub in range(CHUNKS_PER_STEP):
            rs = slice(sub * C, (sub + 1) * C)
            ts, sols, qgs, ps, kends, elasts = prepared[sub]
            sb = [s.astype(BF16) for s in ss]
            wvb = [(sols[h][:, :DK] - _dot(sols[h][:, DK:], sb[h])).astype(BF16) for h in heads]
            for h in heads:
                o_ref[rs, sls[h]] = _dot(qgs[h], sb[h]) + _dot(ps[h], wvb[h])
                sall_ref[sub, h] = ss[h]
                tall_ref[sub, h] = ts[h]
            ss = [ss[h] * elasts[h] + _dot_tn(kends[h], wvb[h]) for h in heads]
        for h in heads:
            s_scr[h] = ss[h]

    rows = CHUNKS_PER_STEP * C
    blk = lambda j: pl.BlockSpec((rows, D), lambda n: (n, j))
    return _call(
        body, name="delta_fwd", grid=(nc // CHUNKS_PER_STEP,),
        in_specs=[blk(0), blk(1), blk(2), pl.BlockSpec((rows, LANE), lambda n: (n, 0))],
        out_specs=(pl.BlockSpec((rows, D), lambda n: (n, 0)),
                   pl.BlockSpec((CHUNKS_PER_STEP, H, DK, DK), lambda n: (n, 0, 0, 0)),
                   pl.BlockSpec((CHUNKS_PER_STEP, H, C, C), lambda n: (n, 0, 0, 0))),
        out_shape=(jax.ShapeDtypeStruct((lp, D), F32), jax.ShapeDtypeStruct((nc, H, DK, DK), F32),
                   jax.ShapeDtypeStruct((nc, H, C, C), F32)),
        scratch_shapes=[pltpu.VMEM((H, DK, DK), F32)],
        compiler_params=_cp(("arbitrary",)),
    )(qkv, qkv, qkv, bg)


def _delta_bwd(qkv, bg, sall, tall, do):
    lp = qkv.shape[0]
    nc = lp // C

    heads = range(H)
    sls = [slice(h * DK, (h + 1) * DK) for h in heads]

    def body(q_ref, k_ref, v_ref, bg_ref, sall_ref, tall_ref, do_ref, dqkv_ref, dbg_ref, ds_scr):
        @pl.when(pl.program_id(0) == 0)
        def _():
            ds_scr[...] = jnp.zeros_like(ds_scr)

        dsns = [ds_scr[h] for h in heads]
        for sub in reversed(range(CHUNKS_PER_STEP_BWD)):
            dsns = chunk(sub, dsns, q_ref, k_ref, v_ref, bg_ref, sall_ref, tall_ref, do_ref, dqkv_ref, dbg_ref)
        for h in heads:
            ds_scr[h] = dsns[h]

    def chunk(sub, dsns, q_ref, k_ref, v_ref, bg_ref, sall_ref, tall_ref, do_ref, dqkv_ref, dbg_ref):
        rs = slice(sub * C, (sub + 1) * C)
        bgt = bg_ref[rs, :]
        row, col = _tri_masks()
        incl, strict = row >= col, row > col
        upper = (row <= col).astype(F32)
        gc_all = _dot_hi(incl.astype(F32), bgt)
        gc_t = _dot_hi(bgt.T, upper)
        lane = lax.broadcasted_iota(jnp.int32, (C, LANE), 1)
        lastrow = lax.broadcasted_iota(jnp.int32, (C, 1), 0) == C - 1
        qs, ks, vs, dos = ([r[rs, sl] for sl in sls] for r in (q_ref, k_ref, v_ref, do_ref))
        bcols = [jnp.broadcast_to(bgt[:, h:h + 1], (C, DK)) for h in heads]
        cm = [_chunk_common(qs[h], ks[h], vs[h], bcols[h], gc_all[:, H + h:H + h + 1], gc_t[H + h:H + h + 1, :],
                            incl, strict) for h in heads]
        dms, kks, qks, egcs, eends, elasts, rhss = zip(*cm)
        ss = [sall_ref[sub, h] for h in heads]
        ts = [tall_ref[sub, h] for h in heads]
        sb = [s.astype(BF16) for s in ss]
        dsb = [d.astype(BF16) for d in dsns]
        dob = [d.astype(BF16) for d in dos]
        sols = [_dot3(_split(ts[h]), _split(rhss[h])) for h in heads]
        ws = [sol[:, DK:] for sol in sols]
        qgs = [qs[h] * egcs[h] for h in heads]
        kends = [ks[h] * eends[h] for h in heads]
        wvs = [sols[h][:, :DK] - _dot(ws[h], sb[h]) for h in heads]
        wvb = [wv.astype(BF16) for wv in wvs]
        dwvs = [_dot_tn(qks[h] * dms[h], dob[h]) + _dot(kends[h], dsb[h]) for h in heads]
        dps = [jnp.where(incl, _dot_nt(dob[h], wvb[h]), 0.0) for h in heads]
        dqgs = [_dot_nt(dob[h], sb[h]) for h in heads]
        dkends = [_dot_nt(wvb[h], dsb[h]) for h in heads]
        ds_before = [_dot_tn(qgs[h], dob[h]) + elasts[h] * dsns[h] - _dot_tn(ws[h], dwvs[h]) for h in heads]
        dglasts = [elasts[h] * jnp.sum(ss[h] * dsns[h], keepdims=True) for h in heads]
        dws = [-_dot_nt(dwvs[h], sb[h]) for h in heads]
        tts = [_split(ts[h].T) for h in heads]
        drhss = [_dot3(tts[h], _split(jnp.concatenate([dwvs[h], dws[h]], axis=1))) for h in heads]
        nt_dims = (((1,), (1,)), ((), ()))
        dns = [jnp.where(strict, -_dot3(_split(drhss[h]), _split(sols[h]), nt_dims), 0.0) for h in heads]
        dbeta_t = jnp.zeros((C, LANE), F32)
        dgc_t = jnp.zeros((C, LANE), F32)
        for h in heads:
            q, k, v, bcol, dm, kk, qk, egc, eend = qs[h], ks[h], vs[h], bcols[h], dms[h], kks[h], qks[h], egcs[h], eends[h]
            drv, drk = drhss[h][:, :DK], drhss[h][:, DK:]
            dn, dp, dqg, dkend = dns[h], dps[h], dqgs[h], dkends[h]
            rk = _rowsum(drk * k)
            dkk = dn * (bcol[:, :C] * dm)
            dqk = dp * dm
            e = (dn * (bcol[:, :C] * kk) + dp * qk) * dm
            tk = _rowsum(dkend * kends[h])
            dgc = rk * bcol * egc + _rowsum(e) - _rowsum(e.T) + _rowsum(dqg * qgs[h]) - tk
            dgc = dgc + jnp.where(lastrow, dglasts[h] + jnp.sum(tk, keepdims=True), 0.0)
            dbeta = _rowsum(drv * v) + rk * egc + _rowsum(dn * kk * dm)
            dqkv_ref[rs, sls[h]] = _dot(dqk, k) + dqg * egc
            dqkv_ref[rs, D + h * DK:D + (h + 1) * DK] = (drk * (bcol * egc) + _dot(dkk, k) + _dot_tn(dkk, k)
                                                        + _dot_tn(dqk, q) + dkend * eend)
            dqkv_ref[rs, 2 * D + h * DK:2 * D + (h + 1) * DK] = bcol * drv
            dbeta_t = jnp.where(lane == h, dbeta, dbeta_t)
            dgc_t = jnp.where(lane == H + h, dgc, dgc_t)
        dbg_ref[rs, :] = dbeta_t + _dot_hi(upper, dgc_t)
        return ds_before

    steps = nc // CHUNKS_PER_STEP_BWD
    rows = CHUNKS_PER_STEP_BWD * C
    rev = lambda n: steps - 1 - n
    blk = lambda j: pl.BlockSpec((rows, D), lambda n: (rev(n), j))
    return _call(
        body, name="delta_bwd", grid=(steps,),
        in_specs=[blk(0), blk(1), blk(2), pl.BlockSpec((rows, LANE), lambda n: (rev(n), 0)),
                  pl.BlockSpec((CHUNKS_PER_STEP_BWD, H, DK, DK), lambda n: (rev(n), 0, 0, 0)),
                  pl.BlockSpec((CHUNKS_PER_STEP_BWD, H, C, C), lambda n: (rev(n), 0, 0, 0)),
                  pl.BlockSpec((rows, D), lambda n: (rev(n), 0))],
        out_specs=(pl.BlockSpec((rows, 3 * D), lambda n: (rev(n), 0)),
                   pl.BlockSpec((rows, LANE), lambda n: (rev(n), 0))),
        out_shape=(jax.ShapeDtypeStruct((lp, 3 * D), F32), jax.ShapeDtypeStruct((lp, LANE), F32)),
        scratch_shapes=[pltpu.VMEM((H, DK, DK), F32)],
        compiler_params=_cp(("arbitrary",)),
    )(qkv, qkv, qkv, bg, sall, tall, do)


def _o_post_fwd(o, proj, dn_w, w_dn):
    lp = o.shape[0]
    te = _pick(lp, (640, 320))

    def body(o_ref, za_ref, w_ref, wdn_ref, out_ref, ya_ref):
        za = za_ref[...]
        gate = za * _sig(za)
        for h in range(H):
            sl = slice(h * DK, (h + 1) * DK)
            oh = o_ref[:, sl]
            r = lax.rsqrt(jnp.mean(oh * oh, axis=-1, keepdims=True) + EPS)
            out_ref[:, sl] = (oh * r * w_ref[...] * gate[:, sl]).astype(BF16)
        ya_ref[...] = _dot(out_ref[...], wdn_ref[...])

    row = pl.BlockSpec((te, D), lambda i: (i, 0))
    return _call(
        body, name="o_post_fwd", grid=(lp // te,),
        in_specs=[row, pl.BlockSpec((te, D), lambda i: (i, CB_ZA)), pl.BlockSpec((1, DK), lambda i: (0, 0)),
                  pl.BlockSpec((D, D), lambda i: (0, 0))],
        out_specs=(row, row),
        out_shape=(jax.ShapeDtypeStruct((lp, D), BF16), jax.ShapeDtypeStruct((lp, D), F32)),
        compiler_params=_cp(("parallel",)),
    )(o, proj, dn_w, w_dn)


def _o_post_bwd(dy_a, w_dn, o, proj, dn_w, dproj):
    lp = o.shape[0]
    te = _pick(lp, (320,))

    def body(dya_ref, wdn_ref, o_ref, za_ref, w_ref, _, do_ref, dza_ref, dw_ref, don_ref):
        @pl.when(pl.program_id(0) == 0)
        def _():
            dw_ref[...] = jnp.zeros_like(dw_ref)

        don_ref[...] = _dot_nt(dya_ref[...], wdn_ref[...])
        za = za_ref[...]
        sz = _sig(za)
        gate, dgate = za * sz, _dsilu(za, sz)
        w = w_ref[...]
        dw = jnp.zeros((1, DK), F32)
        for h in range(H):
            sl = slice(h * DK, (h + 1) * DK)
            oh, g = o_ref[:, sl], don_ref[:, sl]
            r = lax.rsqrt(jnp.mean(oh * oh, axis=-1, keepdims=True) + EPS)
            ohat = oh * r
            dza_ref[:, sl] = (g * ohat * w * dgate[:, sl]).astype(BF16)
            don = g * gate[:, sl]
            dw = dw + _colsum(don * ohat)
            dohat = don * w
            do_ref[:, sl] = r * (dohat - ohat * jnp.mean(dohat * ohat, axis=-1, keepdims=True))
        dw_ref[...] += dw

    return _call(
        body, name="o_post_bwd", grid=(lp // te,),
        in_specs=[pl.BlockSpec((te, D), lambda i: (i, 0)), pl.BlockSpec((D, D), lambda i: (0, 0)),
                  pl.BlockSpec((te, D), lambda i: (i, 0)),
                  pl.BlockSpec((te, D), lambda i: (i, CB_ZA)), pl.BlockSpec((1, DK), lambda i: (0, 0)),
                  pl.BlockSpec(memory_space=pl.ANY)],
        out_specs=(pl.BlockSpec((te, D), lambda i: (i, 0)), pl.BlockSpec((te, D), lambda i: (i, CB_ZA)),
                   pl.BlockSpec((1, DK), lambda i: (0, 0))),
        out_shape=(jax.ShapeDtypeStruct((lp, D), F32), jax.ShapeDtypeStruct(dproj.shape, dproj.dtype),
                   jax.ShapeDtypeStruct((1, DK), F32)),
        input_output_aliases={5: 1},
        scratch_shapes=[pltpu.VMEM((te, D), F32)],
        compiler_params=_cp(("arbitrary",)),
    )(dy_a, w_dn, o, proj, dn_w, dproj)


def _qkv_conv_bwd(proj, dqkv, conv_w, dproj):
    lp = proj.shape[0]
    te = _pick(lp, (320,))
    hb = te // HALO_Q
    nt = lp // te
    last_hb = lp // HALO_Q - 1

    def body(main_ref, prev_ref, next_ref, dmain_ref, dnext_ref, cw_ref, _, dpre_ref, dcw_ref, pre_scr, dn_scr,
             tap_scr, dco_scr, dsh_scr):
        s, i = pl.program_id(0), pl.program_id(1)

        @pl.when(i == 0)
        def _():
            dcw_ref[...] = jnp.zeros_like(dcw_ref)

        ne = te + HALO_Q
        pre_scr[:HALO_Q, :] = jnp.where(i > 0, prev_ref[...], 0.0)
        pre_scr[HALO_Q:ne, :] = main_ref[...]
        pre_scr[ne:, :] = jnp.where(i < nt - 1, next_ref[...], 0.0)
        dn_scr[:te, :] = dmain_ref[...]
        dn_scr[te:, :] = jnp.where(i < nt - 1, dnext_ref[...], 0.0)
        scale = jnp.where(s == 0, DK ** -0.5, 1.0)
        off = HALO_Q - (KQ - 1)

        def head(h, carry):
            cs = pl.ds(pl.multiple_of(h * DK, DK), DK)
            for j in range(KQ - 1):
                tap_scr[j] = pre_scr[off + j:off + j + ne, cs]
            taps = [tap_scr[j] for j in range(KQ - 1)] + [pre_scr[HALO_Q:, cs]]
            co = cw_ref[0:1, cs] * taps[0]
            for j in range(1, KQ):
                co = co + cw_ref[j:j + 1, cs] * taps[j]
            sg = _sig(co)
            a = co * sg
            g = dn_scr[:, cs]
            r = lax.rsqrt(_rowsum(a * a) + EPS)
            yhat = a * r
            da = jnp.where(s == 2, g, (scale * r) * (g - yhat * _rowsum(g * yhat)))
            dco = da * _dsilu(co, sg)
            dco_scr[...] = dco
            for j in range(KQ - 1):
                dsh_scr[j] = dco_scr[KQ - 1 - j:KQ - 1 - j + te, :]
            dpre = cw_ref[KQ - 1:KQ, cs] * dco[:te, :]
            for j in range(KQ - 1):
                dpre = dpre + cw_ref[j:j + 1, cs] * dsh_scr[j]
            dpre_ref[:, cs] = dpre.astype(BF16)
            dcw_ref[:, cs] += jnp.concatenate([_colsum(dco[:te] * taps[j][:te]) for j in range(KQ)], axis=0)
            return carry

        lax.fori_loop(0, H, head, 0, unroll=True)

    return _call(
        body, name="qkv_conv_bwd", grid=(3, nt),
        in_specs=[pl.BlockSpec((te, D), lambda s, i: (i, s)),
                  pl.BlockSpec((HALO_Q, D), lambda s, i: (jnp.maximum(i * hb - 1, 0), s)),
                  pl.BlockSpec((HALO_Q, D), lambda s, i: (jnp.minimum((i + 1) * hb, last_hb), s)),
                  pl.BlockSpec((te, D), lambda s, i: (i, s)),
                  pl.BlockSpec((HALO_Q, D), lambda s, i: (jnp.minimum((i + 1) * hb, last_hb), s)),
                  pl.BlockSpec((KQ, D), lambda s, i: (0, s)),
                  pl.BlockSpec(memory_space=pl.ANY)],
        out_specs=(pl.BlockSpec((te, D), lambda s, i: (i, s)), pl.BlockSpec((KQ, D), lambda s, i: (0, s))),
        out_shape=(jax.ShapeDtypeStruct(dproj.shape, dproj.dtype), jax.ShapeDtypeStruct((KQ, 3 * D), F32)),
        input_output_aliases={6: 0},
        scratch_shapes=[pltpu.VMEM((te + 2 * HALO_Q, D), F32), pltpu.VMEM((te + HALO_Q, D), F32),
                        pltpu.VMEM((KQ - 1, te + HALO_Q, DK), F32), pltpu.VMEM((te + HALO_Q, DK), F32),
                        pltpu.VMEM((KQ - 1, te, DK), F32)],
        compiler_params=_cp(("arbitrary", "arbitrary")),
    )(proj, proj, proj, dqkv, dqkv, conv_w, dproj)


def _ba_bwd(dbg, ba, ab, pad):
    lp = ba.shape[0]
    te = _pick(lp, (640, 320))

    def body(dbg_ref, ba_ref, ab_ref, dba_ref, dab_ref):
        i = pl.program_id(0)

        @pl.when(i == 0)
        def _():
            dab_ref[...] = jnp.zeros_like(dab_ref)

        ba, ab = ba_ref[...], ab_ref[...]
        lane = lax.broadcasted_iota(jnp.int32, ba.shape, 1)
        rows = i * te + lax.broadcasted_iota(jnp.int32, ba.shape, 0)
        g = jnp.where((lane < 2 * H) & (rows >= pad), dbg_ref[...], 0.0)
        sb = _sig(ba)
        z = ba + ab[1:2, :]
        sp = jnp.maximum(z, 0.0) + jnp.log(1.0 + jnp.exp(-jnp.abs(z)))
        nea = -jnp.exp(ab[0:1, :])
        dz = g * nea * _sig(z)
        dba_ref[...] = jnp.where(lane < H, g * sb * (1.0 - sb), dz).astype(BF16)
        is_g = (lane >= H) & (lane < 2 * H)
        dab_ref[...] += jnp.concatenate([_colsum(jnp.where(is_g, g * nea * sp, 0.0)),
                                         _colsum(jnp.where(is_g, dz, 0.0))], axis=0)

    return _call(
        body, name="ba_bwd", grid=(lp // te,),
        in_specs=[pl.BlockSpec((te, LANE), lambda i: (i, 0)), pl.BlockSpec((te, LANE), lambda i: (i, 0)),
                  pl.BlockSpec((2, LANE), lambda i: (0, 0))],
        out_specs=(pl.BlockSpec((te, LANE), lambda i: (i, 0)), pl.BlockSpec((2, LANE), lambda i: (0, 0))),
        out_shape=(jax.ShapeDtypeStruct((lp, LANE), BF16), jax.ShapeDtypeStruct((2, LANE), F32)),
        compiler_params=_cp(("arbitrary",)),
    )(dbg, ba, ab)


SUBLANES = 8
CONV_RB = 64


def _fill_shifted(sh_scr, src_scr, cs):
    n = sh_scr.shape[1]
    for s in range(1, SUBLANES):
        sh_scr[s] = src_scr[s:s + n, cs]


def _shifted(sh_scr, src_scr, cs, r, r0, n):
    s, a8 = r % SUBLANES, r - r % SUBLANES
    if s == 0:
        return src_scr[r0 + a8:r0 + a8 + n, cs]
    return sh_scr[s, r0 + a8:r0 + a8 + n, :]


def _conv_b_fwd(proj, dw_w, dw_b, ln_w, ln_b, w_cf):
    lp = proj.shape[0]
    te = _pick(lp, (320,))
    hb = te // HALO_D

    def body(a_ref, b_ref, ha_ref, hb_ref, zb_ref, w_ref, wb_ref, lw_ref, lb_ref, wcf_ref, c1_ref, c3_ref, yb_ref,
             c0_scr, sh_scr):
        i = pl.program_id(0)
        c0_scr[:HALO_D, :] = jnp.where(i > 0, ha_ref[...] * _sig(hb_ref[...]), 0.0)
        c0_scr[HALO_D:, :] = a_ref[...] * _sig(b_ref[...])
        off = HALO_D - (KD - 1)
        def lane_block(cb, carry):
            cs = pl.ds(pl.multiple_of(cb * LANE, LANE), LANE)
            _fill_shifted(sh_scr, c0_scr, cs)
            for r0 in range(0, te, CONV_RB):
                acc = None
                for j in range(KD):
                    term = w_ref[j:j + 1, cs] * _shifted(sh_scr, c0_scr, cs, off + j, r0, CONV_RB)
                    acc = term if acc is None else acc + term
                c1_ref[r0:r0 + CONV_RB, cs] = acc + wb_ref[:, cs]
            return carry

        lax.fori_loop(0, D // LANE, lane_block, 0)
        c1 = c1_ref[...]
        mu = jnp.mean(c1, axis=-1, keepdims=True)
        xc = c1 - mu
        c2 = xc * lax.rsqrt(jnp.mean(xc * xc, axis=-1, keepdims=True) + EPS) * lw_ref[...] + lb_ref[...]
        zb = zb_ref[...]
        c3 = (c2 * _sig(c2) * zb * _sig(zb)).astype(BF16)
        c3_ref[...] = c3
        yb_ref[...] = _dot(c3, wcf_ref[...])

    vec = pl.BlockSpec((1, D), lambda i: (0, 0))
    row = pl.BlockSpec((te, D), lambda i: (i, 0))
    return _call(
        body, name="conv_b_fwd", grid=(lp // te,),
        in_specs=[pl.BlockSpec((te, D), lambda i: (i, CB_GA_)), pl.BlockSpec((te, D), lambda i: (i, CB_GB_)),
                  pl.BlockSpec((HALO_D, D), lambda i: (jnp.maximum(i * hb - 1, 0), CB_GA_)),
                  pl.BlockSpec((HALO_D, D), lambda i: (jnp.maximum(i * hb - 1, 0), CB_GB_)),
                  pl.BlockSpec((te, D), lambda i: (i, CB_ZB)),
                  pl.BlockSpec((KD, D), lambda i: (0, 0)), vec, vec, vec, pl.BlockSpec((D, D), lambda i: (0, 0))],
        out_specs=(row, row, row),
        out_shape=(jax.ShapeDtypeStruct((lp, D), F32), jax.ShapeDtypeStruct((lp, D), BF16),
                   jax.ShapeDtypeStruct((lp, D), F32)),
        scratch_shapes=[pltpu.VMEM((te + HALO_D, D), F32), pltpu.VMEM((SUBLANES, te + HALO_D - SUBLANES, LANE), F32)],
        compiler_params=_cp(("parallel",)),
    )(proj, proj, proj, proj, proj, dw_w, dw_b, ln_w, ln_b, w_cf)


def _conv_b_bwd1(dy_b, w_cf, c1, proj, ln_w, ln_b, dproj):
    lp = c1.shape[0]
    te = _pick(lp, (320,))

    def body(dyb_ref, wcf_ref, c1_ref, zb_ref, lw_ref, lb_ref, _, dc1_ref, dzb_ref, sums_ref):
        @pl.when(pl.program_id(0) == 0)
        def _():
            sums_ref[...] = jnp.zeros_like(sums_ref)

        c1, g = c1_ref[...], _dot_nt(dyb_ref[...], wcf_ref[...])
        mu = jnp.mean(c1, axis=-1, keepdims=True)
        xc = c1 - mu
        rstd = lax.rsqrt(jnp.mean(xc * xc, axis=-1, keepdims=True) + EPS)
        xh = xc * rstd
        lw = lw_ref[...]
        c2 = xh * lw + lb_ref[...]
        s2 = _sig(c2)
        zb = zb_ref[...]
        sz = _sig(zb)
        dc2 = g * (zb * sz) * _dsilu(c2, s2)
        dzb_ref[...] = (g * (c2 * s2) * _dsilu(zb, sz)).astype(BF16)
        dxh = dc2 * lw
        dc1 = rstd * (dxh - jnp.mean(dxh, axis=-1, keepdims=True) - xh * jnp.mean(dxh * xh, axis=-1, keepdims=True))
        dc1_ref[...] = dc1
        sums_ref[...] += jnp.concatenate([_colsum(dc2 * xh), _colsum(dc2), _colsum(dc1)], axis=0)

    vec = pl.BlockSpec((1, D), lambda i: (0, 0))
    return _call(
        body, name="conv_b_bwd1", grid=(lp // te,),
        in_specs=[pl.BlockSpec((te, D), lambda i: (i, 0)), pl.BlockSpec((D, D), lambda i: (0, 0)),
                  pl.BlockSpec((te, D), lambda i: (i, 0)),
                  pl.BlockSpec((te, D), lambda i: (i, CB_ZB)), vec, vec, pl.BlockSpec(memory_space=pl.ANY)],
        out_specs=(pl.BlockSpec((te, D), lambda i: (i, 0)), pl.BlockSpec((te, D), lambda i: (i, CB_ZB)),
                   pl.BlockSpec((3, D), lambda i: (0, 0))),
        out_shape=(jax.ShapeDtypeStruct((lp, D), F32), jax.ShapeDtypeStruct(dproj.shape, dproj.dtype),
                   jax.ShapeDtypeStruct((3, D), F32)),
        input_output_aliases={6: 1},
        compiler_params=_cp(("arbitrary",)),
    )(dy_b, w_cf, c1, proj, ln_w, ln_b, dproj)


def _conv_b_bwd2(dc1, proj, dw_w, dproj):
    lp = dc1.shape[0]
    te = _pick(lp, (320,))
    hb = te // HALO_D
    nt = lp // te
    last_hb = lp // HALO_D - 1

    def body(g_ref, gn_ref, a_ref, b_ref, ha_ref, hb_ref, w_ref, _, dab_ref, dw_ref, c0_scr, g_scr, dc0_scr,
             csh_scr, gsh_scr):
        i = pl.program_id(0)

        @pl.when(i == 0)
        def _():
            dw_ref[...] = jnp.zeros_like(dw_ref)

        a, b = a_ref[...], b_ref[...]
        sb = _sig(b)
        c0_scr[:HALO_D, :] = jnp.where(i > 0, ha_ref[...] * _sig(hb_ref[...]), 0.0)
        c0_scr[HALO_D:, :] = a * sb
        g_scr[:te, :] = g_ref[...]
        g_scr[te:, :] = jnp.where(i < nt - 1, gn_ref[...], 0.0)
        off = HALO_D - (KD - 1)
        def lane_block(cb, carry):
            cs = pl.ds(pl.multiple_of(cb * LANE, LANE), LANE)
            _fill_shifted(csh_scr, c0_scr, cs)
            _fill_shifted(gsh_scr, g_scr, cs)
            for r0 in range(0, te, CONV_RB):
                acc = None
                for j in range(KD):
                    term = w_ref[j:j + 1, cs] * _shifted(gsh_scr, g_scr, cs, KD - 1 - j, r0, CONV_RB)
                    acc = term if acc is None else acc + term
                dc0_scr[r0:r0 + CONV_RB, cs] = acc
            parts = [None] * KD
            for r0 in range(0, te, CONV_RB):
                g = g_scr[r0:r0 + CONV_RB, cs].reshape(CONV_RB // SUBLANES, SUBLANES, LANE)
                for j in range(KD):
                    x = _shifted(csh_scr, c0_scr, cs, off + j, r0, CONV_RB)
                    p = jnp.sum(g * x.reshape(CONV_RB // SUBLANES, SUBLANES, LANE), axis=0)
                    parts[j] = p if parts[j] is None else parts[j] + p
            dw_ref[:, cs] += jnp.concatenate([_colsum(p) for p in parts], axis=0)
            return carry

        lax.fori_loop(0, D // LANE, lane_block, 0)
        dc0 = dc0_scr[...]
        dab_ref[:, :D] = (dc0 * sb).astype(BF16)
        dab_ref[:, D:] = (dc0 * a * sb * (1.0 - sb)).astype(BF16)

    return _call(
        body, name="conv_b_bwd2", grid=(nt,),
        in_specs=[pl.BlockSpec((te, D), lambda i: (i, 0)),
                  pl.BlockSpec((HALO_D, D), lambda i: (jnp.minimum((i + 1) * hb, last_hb), 0)),
                  pl.BlockSpec((te, D), lambda i: (i, CB_GA_)), pl.BlockSpec((te, D), lambda i: (i, CB_GB_)),
                  pl.BlockSpec((HALO_D, D), lambda i: (jnp.maximum(i * hb - 1, 0), CB_GA_)),
                  pl.BlockSpec((HALO_D, D), lambda i: (jnp.maximum(i * hb - 1, 0), CB_GB_)),
                  pl.BlockSpec((KD, D), lambda i: (0, 0)), pl.BlockSpec(memory_space=pl.ANY)],
        out_specs=(pl.BlockSpec((te, 2 * D), lambda i: (i, CB_GA_ // 2)), pl.BlockSpec((KD, D), lambda i: (0, 0))),
        out_shape=(jax.ShapeDtypeStruct(dproj.shape, dproj.dtype), jax.ShapeDtypeStruct((KD, D), F32)),
        input_output_aliases={7: 0},
        scratch_shapes=[pltpu.VMEM((te + HALO_D, D), F32), pltpu.VMEM((te + HALO_D, D), F32), pltpu.VMEM((te, D), F32),
                        pltpu.VMEM((SUBLANES, te + HALO_D - SUBLANES, LANE), F32),
                        pltpu.VMEM((SUBLANES, te + HALO_D - SUBLANES, LANE), F32)],
        compiler_params=_cp(("arbitrary",)),
    )(dc1, dc1, proj, proj, proj, proj, dw_w, dproj)


def _merge_fwd(y_a, y_b, proj, b_cf, w_o):
    lp = y_a.shape[0]
    te = _pick(lp, (320,))

    def body(ya_ref, yb_ref, ga_ref, gb_ref, bias_ref, wo_ref, out_ref, z_ref):
        merged = (_sig(ga_ref[...]) * ya_ref[...] + _sig(gb_ref[...]) * (yb_ref[...] + bias_ref[...])).astype(BF16)
        out_ref[...] = merged
        z_ref[...] = _dot(merged, wo_ref[...])

    row = lambda j: pl.BlockSpec((te, D), lambda i: (i, j))
    return _call(
        body, name="merge_fwd", grid=(lp // te,),
        in_specs=[row(0), row(0), row(CB_MA), row(CB_MB), pl.BlockSpec((1, D), lambda i: (0, 0)),
                  pl.BlockSpec((D, D), lambda i: (0, 0))],
        out_specs=(row(0), row(0)),
        out_shape=(jax.ShapeDtypeStruct((lp, D), BF16), jax.ShapeDtypeStruct((lp, D), F32)),
        compiler_params=_cp(("parallel",)),
    )(y_a, y_b, proj, proj, b_cf, w_o)


def _merge_bwd(dx_out_b, w_o, y_a, y_b, proj, b_cf):
    lp = y_a.shape[0]
    te = _pick(lp, (320,))

    def body(dx_ref, wo_ref, ya_ref, yb_ref, ga_ref, gb_ref, bias_ref, dya_ref, dyb_ref, dg_ref, db_ref):
        @pl.when(pl.program_id(0) == 0)
        def _():
            db_ref[...] = jnp.zeros_like(db_ref)

        dm = _dot_nt(dx_ref[...], wo_ref[...])
        sa, sb = _sig(ga_ref[...]), _sig(gb_ref[...])
        dyb = sb * dm
        dya_ref[...] = (sa * dm).astype(BF16)
        dyb_ref[...] = dyb.astype(BF16)
        dg_ref[:, :D] = (dm * ya_ref[...] * sa * (1.0 - sa)).astype(BF16)
        dg_ref[:, D:] = (dm * (yb_ref[...] + bias_ref[...]) * sb * (1.0 - sb)).astype(BF16)
        db_ref[...] += _colsum(dyb)

    row = lambda j: pl.BlockSpec((te, D), lambda i: (i, j))
    act = jax.ShapeDtypeStruct((lp, D), BF16)
    return _call(
        body, name="merge_bwd", grid=(lp // te,),
        in_specs=[row(0), pl.BlockSpec((D, D), lambda i: (0, 0)), row(0), row(0), row(CB_MA), row(CB_MB),
                  pl.BlockSpec((1, D), lambda i: (0, 0))],
        out_specs=(row(0), row(0), pl.BlockSpec((te, 2 * D), lambda i: (i, CB_MA // 2)),
                   pl.BlockSpec((1, D), lambda i: (0, 0))),
        out_shape=(act, act, jax.ShapeDtypeStruct((lp, NCB * D), BF16), jax.ShapeDtypeStruct((1, D), F32)),
        compiler_params=_cp(("arbitrary",)),
    )(dx_out_b, w_o, y_a, y_b, proj, proj, b_cf)


def _final_fwd_bwd(x_ext, z, target, final_w):
    lp = x_ext.shape[0]
    te = _pick(lp, (640,))
    nsub = te // LANE

    def body(x_ref, z_ref, *rest):
        t_refs, (w_ref, dx_ref, dxb_ref, loss_ref, dw_ref) = rest[:nsub], rest[nsub:]
        i = pl.program_id(0)

        @pl.when(i == 0)
        def _():
            loss_ref[...] = jnp.zeros_like(loss_ref)
            dw_ref[...] = jnp.zeros_like(dw_ref)

        w = w_ref[...]
        for k in range(nsub):
            rs = slice(k * LANE, (k + 1) * LANE)
            xo = x_ref[rs, :] + z_ref[rs, :]
            r = lax.rsqrt(jnp.mean(xo * xo, axis=-1, keepdims=True) + EPS)
            xhat = xo * r
            err = xhat * w - t_refs[k][...]
            if k == 0:
                err = jnp.where(i > 0, err, 0.0)
            loss_ref[...] += 0.5 * jnp.sum(jnp.mean(err * err, axis=-1, keepdims=True), keepdims=True)
            dy = err * (1.0 / D)
            dw_ref[...] += _colsum(dy * xhat)
            dxn = dy * w
            dx = r * (dxn - xhat * jnp.mean(dxn * xhat, axis=-1, keepdims=True))
            dx_ref[rs, :] = dx
            dxb_ref[rs, :] = dx.astype(BF16)

    piece = lambda k: pl.BlockSpec((LANE, D), lambda i: (jnp.maximum(i * nsub + k - 1, 0), 0))
    row = pl.BlockSpec((te, D), lambda i: (i, 0))
    return _call(
        body, name="final_fwd_bwd", grid=(lp // te,),
        in_specs=[row, row] + [piece(k) for k in range(nsub)] + [pl.BlockSpec((1, D), lambda i: (0, 0))],
        out_specs=(row, row, pl.BlockSpec((1, 1), lambda i: (0, 0)), pl.BlockSpec((1, D), lambda i: (0, 0))),
        out_shape=(jax.ShapeDtypeStruct((lp, D), F32), jax.ShapeDtypeStruct((lp, D), BF16),
                   jax.ShapeDtypeStruct((1, 1), F32), jax.ShapeDtypeStruct((1, D), F32)),
        compiler_params=_cp(("arbitrary",)),
    )(x_ext, z, *([target] * nsub), final_w)


def _prenorm_bwd(dh, x_ext, dx_out, norm_w, seq):
    lp = x_ext.shape[0]
    te = LANE

    def body(dh_ref, x_ref, dxo_ref, w_ref, gx_ref, head_ref, dw_ref):
        i = pl.program_id(0)

        @pl.when(i == 0)
        def _():
            dw_ref[...] = jnp.zeros_like(dw_ref)

        x, dh = x_ref[...], dh_ref[...]
        r = lax.rsqrt(jnp.mean(x * x, axis=-1, keepdims=True) + EPS)
        xhat = x * r
        dxn = dh * w_ref[...]
        dx = dxo_ref[...] + r * (dxn - xhat * jnp.mean(dxn * xhat, axis=-1, keepdims=True))
        dw_ref[...] += _colsum(dh * xhat)

        @pl.when(i == 0)
        def _():
            head_ref[...] = dx

        @pl.when(i > 0)
        def _():
            gx_ref[...] = dx

    row = pl.BlockSpec((te, D), lambda i: (i, 0))
    return _call(
        body, name="prenorm_bwd", grid=(lp // te,),
        in_specs=[row, row, row, pl.BlockSpec((1, D), lambda i: (0, 0))],
        out_specs=(pl.BlockSpec((te, D), lambda i: (jnp.maximum(i - 1, 0), 0)), pl.BlockSpec((te, D), lambda i: (0, 0)),
                   pl.BlockSpec((1, D), lambda i: (0, 0))),
        out_shape=(jax.ShapeDtypeStruct((seq, D), F32), jax.ShapeDtypeStruct((te, D), F32),
                   jax.ShapeDtypeStruct((1, D), F32)),
        compiler_params=_cp(("arbitrary",)),
    )(dh, x_ext, dx_out, norm_w)


def _adam_reduce(parts, w, m, v, name):
    r, n = w.shape
    tr = _pick(r, (128,)) if r % 128 == 0 else r

    def body(p_ref, w_ref, m_ref, v_ref, g_ref, d_ref, m2_ref, v2_ref):
        g = p_ref[0]
        for s in range(1, NDEV):
            g = g + p_ref[s]
        _adam_write(g, w_ref, m_ref, v_ref, g_ref, d_ref, m2_ref, v2_ref)

    blk = pl.BlockSpec((tr, n), lambda i: (i, 0))
    out = jax.ShapeDtypeStruct((r, n), F32)
    return _call(
        body, name=name, grid=(r // tr,),
        in_specs=[pl.BlockSpec((NDEV, tr, n), lambda i: (0, i, 0)), blk, blk, blk],
        out_specs=(blk, blk, blk, blk), out_shape=(out, out, out, out),
        compiler_params=_cp(("parallel",)),
    )(parts, w, m, v)


def _adam_write(g, w_ref, m_ref, v_ref, g_ref, d_ref, m2_ref, v2_ref):
    c1 = 1.0 - ADAM_B1 ** ADAM_STEP
    c2 = 1.0 - ADAM_B2 ** ADAM_STEP
    m2 = ADAM_B1 * m_ref[...] + (1.0 - ADAM_B1) * g
    v2 = ADAM_B2 * v_ref[...] + (1.0 - ADAM_B2) * (g * g)
    g_ref[...] = g
    m2_ref[...] = m2
    v2_ref[...] = v2
    d_ref[...] = -ADAM_LR * ((m2 / c1) / (jnp.sqrt(v2 / c2) + ADAM_EPS) + ADAM_WD * w_ref[...])


def _adam_chips(own, recv, w, m, v, name):
    r, n = w.shape
    tr, tc = _shard_tile(r, n)

    def body(own_ref, p_ref, w_ref, m_ref, v_ref, g_ref, d_ref, m2_ref, v2_ref):
        my_chip = 2 * lax.axis_index("x") + lax.axis_index("y")
        g = None
        for j in range(NCHIP):
            part = jnp.where(my_chip == j, own_ref[...], p_ref[j].astype(F32))
            g = part if g is None else g + part
        _adam_write(g, w_ref, m_ref, v_ref, g_ref, d_ref, m2_ref, v2_ref)

    blk = pl.BlockSpec((tr, tc), lambda i, k: (i, k))
    out = jax.ShapeDtypeStruct((r, n), F32)
    return _call(
        body, name=name, grid=(r // tr, n // tc),
        in_specs=[blk, pl.BlockSpec((NCHIP, tr, tc), lambda i, k: (0, i, k)), blk, blk, blk],
        out_specs=(blk, blk, blk, blk), out_shape=(out, out, out, out),
        compiler_params=_cp(("parallel", "parallel")),
    )(own, recv, w, m, v)


SMALL = ("norm_w", "a_log", "dt_bias", "dn_norm_w", "dw_b", "ln_w", "ln_b", "b_cf_out", "final_norm_w")


def kernel(x, meta, norm_w, w_in, conv_qkv_w, a_log, dt_bias, dn_norm_w, w_dn_out, dw_w, dw_b, ln_w, ln_b, w_cf_out, b_cf_out, w_o, final_norm_w, loss_target, m_meta, m_norm_w, m_w_in, m_conv_qkv_w, m_a_log, m_dt_bias, m_dn_norm_w, m_w_dn_out, m_dw_w, m_dw_b, m_ln_w, m_ln_b, m_w_cf_out, m_b_cf_out, m_w_o, m_final_norm_w, v_meta, v_norm_w, v_w_in, v_conv_qkv_w, v_a_log, v_dt_bias, v_dn_norm_w, v_w_dn_out, v_dw_w, v_dw_b, v_ln_w, v_ln_b, v_w_cf_out, v_b_cf_out, v_w_o, v_final_norm_w):
    seq = x.shape[1]
    pad = (-(seq + NMETA)) % LANE
    in_w = w_in.shape[2] * NDEV
    n_qkvz = 4 * D
    n_ba = 2 * H

    me = 4 * lax.axis_index("x") + 2 * lax.axis_index("y") + lax.axis_index("c")
    late = [w_dn_out[0].astype(BF16), w_cf_out[0].astype(BF16), w_o[0].astype(BF16), conv_qkv_w[0], dw_w[0]]
    late_lands = [lax.dynamic_update_index_in_dim(jnp.zeros((NDEV,) + b.shape, b.dtype), b, me, 0) for b in late]
    late_send, late_recv, late_thru, late_land_thru, late_token = _split_start(
        _all_copies, NDEV - 1, "gather_late_start", late, late_lands)
    w_in_g, meta_g = _gather_two_level([w_in[0].astype(BF16).T, meta + late_token[0, 0]], "gather_weights")
    w_full_t = w_in_g.reshape(in_w, D)
    c_glu = n_qkvz + n_ba
    c_zb, c_mg = c_glu + 2 * D, c_glu + 3 * D
    w_main_t = jnp.concatenate([w_full_t[:n_qkvz], w_full_t[c_glu:c_zb], w_full_t[c_mg:], w_full_t[c_zb:c_mg]],
                               axis=0)
    w_ba_t = jnp.pad(w_full_t[n_qkvz:n_qkvz + n_ba], ((0, LANE - n_ba), (0, 0)))
    meta_full = jnp.transpose(meta_g, (1, 0, 2)).reshape(NMETA, D)
    ab = jnp.pad(jnp.concatenate([a_log, dt_bias], axis=0), ((0, 0), (H, LANE - 2 * H)))

    x_ext = jnp.concatenate([jnp.zeros((pad, D), F32), meta_full, x[0]], axis=0)

    proj, ba, h = _proj_fwd(x_ext, norm_w, w_main_t, w_ba_t)
    w_dn_g, w_cf_g, w_o_g, cqw_g, dww_g = _split_wait(
        _all_copies, "gather_late_wait", late_send, late_recv, late_thru, late_land_thru, ba)
    w_dn, w_cf, w_oo = (t.reshape(D, D) for t in (w_dn_g, w_cf_g, w_o_g))
    cqw = jnp.transpose(cqw_g, (1, 0, 2)).reshape(KQ, 3 * D)
    dww = jnp.transpose(dww_g, (1, 0, 2)).reshape(KD, D)
    qkv, bg = _qkv_conv_fwd(proj, ba, cqw, ab, pad)
    o, sall, tall = _delta_fwd(qkv, bg)
    o_n, y_a = _o_post_fwd(o, proj, dn_norm_w, w_dn)
    c1, c3, y_b = _conv_b_fwd(proj, dww, dw_b, ln_w, ln_b, w_cf)
    merged, z = _merge_fwd(y_a, y_b, proj, b_cf_out, w_oo)
    dx_out, dx_out_b, loss_part, g_final_w = _final_fwd_bwd(x_ext, z, loss_target[0], final_norm_w.reshape(1, D))

    g_w_o = _mm_tn(merged, dx_out_b, "g_w_o_mm")
    dy_a, dy_b, dproj, g_b_cf = _merge_bwd(dx_out_b, w_oo, y_a, y_b, proj, b_cf_out)
    g_w_cf = _mm_tn(c3, dy_b, "g_w_cf_mm")
    g_w_dn = _mm_tn(o_n, dy_a, "g_w_dn_mm")
    dc1, dproj, sums_b = _conv_b_bwd1(dy_b, w_cf, c1, proj, ln_w, ln_b, dproj)
    dproj, g_dw_w = _conv_b_bwd2(dc1, proj, dww, dproj)
    do, dproj, g_dn_w = _o_post_bwd(dy_a, w_dn, o, proj, dn_norm_w, dproj)
    dqkv, dbg = _delta_bwd(qkv, bg, sall, tall, do)
    dproj, g_cqw = _qkv_conv_bwd(proj, dqkv, cqw, dproj)
    dba, dab = _ba_bwd(dbg, ba, ab, pad)
    g_w_main_t = _mm_tn(dproj, h, "g_w_main_mm")
    g_w_ba_t = _mm_tn(dba, h, "g_w_ba_mm")

    g_w_full_t = jnp.concatenate([g_w_main_t[:n_qkvz], g_w_ba_t[:n_ba], g_w_main_t[CB_GA_ * D:CB_MA * D],
                                  g_w_main_t[CB_ZB * D:], g_w_main_t[CB_MA * D:CB_ZB * D]], axis=0)
    big = [t.reshape(NCHIP, 2, t.shape[0] // NDEV, D) for t in (g_w_full_t, g_w_dn, g_w_cf, g_w_o)]
    from_sibling = _swap_sibling(big, "swap_sibling")
    pairs = [_pair_add(a, g, f"pair_add_{i}") for i, (a, g) in enumerate(zip(big, from_sibling))]
    send_sems, recv_sems, pair_thru, land_thru, token = _split_start(
        _chip_copies, NCHIP - 1, "scatter_chips_start",
        [p for p, _ in pairs], [jnp.zeros(p.shape, p.dtype) for p, _ in pairs])

    dh = _dh_mm(dproj, dba, w_main_t, w_ba_t + token[0, 0].astype(BF16))
    grad_x, dhead, g_norm_w = _prenorm_bwd(dh, x_ext, dx_out, norm_w, seq)
    from_chips = _split_wait(_chip_copies, "scatter_chips_wait", send_sems, recv_sems, pair_thru, land_thru, g_norm_w)

    split_cols = lambda t: jnp.transpose(t.reshape(t.shape[0], NDEV, t.shape[1] // NDEV), (1, 0, 2))
    small = {"norm_w": g_norm_w, "a_log": dab[0:1, H:2 * H], "dt_bias": dab[1:2, H:2 * H], "dn_norm_w": g_dn_w,
             "dw_b": sums_b[2:3], "ln_w": sums_b[0:1], "ln_b": sums_b[1:2], "b_cf_out": g_b_cf,
             "final_norm_w": g_final_w}
    small_vec = jnp.concatenate([small[k] for k in SMALL], axis=1)
    ns = small_vec.shape[1]
    ns_pad = (-ns) % LANE
    small_vec = jnp.pad(small_vec, ((0, 0), (0, ns_pad)))
    p_meta, p_cqw, p_dww, p_small = _exchange(
        [split_cols(dhead[pad:pad + NMETA]), split_cols(g_cqw), split_cols(g_dw_w), small_vec],
        [True] * 3 + [False], "exchange_small")

    res = {}
    res["w_in"] = tuple(t.T for t in _adam_chips(pairs[0][1], from_chips[0], w_in[0].T, m_w_in[0].T, v_w_in[0].T,
                                                   "adam_w_in"))
    res["w_dn_out"] = _adam_chips(pairs[1][1], from_chips[1], w_dn_out[0], m_w_dn_out[0], v_w_dn_out[0], "adam_w_dn")
    res["w_cf_out"] = _adam_chips(pairs[2][1], from_chips[2], w_cf_out[0], m_w_cf_out[0], v_w_cf_out[0], "adam_w_cf")
    res["w_o"] = _adam_chips(pairs[3][1], from_chips[3], w_o[0], m_w_o[0], v_w_o[0], "adam_w_o")
    res["meta"] = _adam_reduce(p_meta, meta, m_meta, v_meta, "adam_meta")
    res["conv_qkv_w"] = _adam_reduce(p_cqw, conv_qkv_w[0], m_conv_qkv_w[0], v_conv_qkv_w[0], "adam_conv_qkv_w")
    res["dw_w"] = _adam_reduce(p_dww, dw_w[0], m_dw_w[0], v_dw_w[0], "adam_dw_w")
    loc = dict(norm_w=(norm_w, m_norm_w, v_norm_w), a_log=(a_log, m_a_log, v_a_log), dt_bias=(dt_bias, m_dt_bias, v_dt_bias),
               dn_norm_w=(dn_norm_w, m_dn_norm_w, v_dn_norm_w), dw_b=(dw_b, m_dw_b, v_dw_b), ln_w=(ln_w, m_ln_w, v_ln_w),
               ln_b=(ln_b, m_ln_b, v_ln_b), b_cf_out=(b_cf_out, m_b_cf_out, v_b_cf_out),
               final_norm_w=(final_norm_w, m_final_norm_w, v_final_norm_w))
    cat = lambda j: jnp.pad(jnp.concatenate([loc[k][j].reshape(1, -1) for k in SMALL], axis=1), ((0, 0), (0, ns_pad)))
    small_res = _adam_reduce(p_small, cat(0), cat(1), cat(2), "adam_small")
    off = 0
    for k in SMALL:
        wshape = loc[k][0].shape
        nk = loc[k][0].size
        res[k] = tuple(t[:, off:off + nk].reshape(wshape) for t in small_res)
        off += nk
    shaped = dict(w_in=w_in.shape, w_dn_out=w_dn_out.shape, w_cf_out=w_cf_out.shape, w_o=w_o.shape, meta=meta.shape,
                  conv_qkv_w=conv_qkv_w.shape, dw_w=dw_w.shape)
    for k, shp in shaped.items():
        res[k] = tuple(t.reshape(shp) for t in res[k])

    loss = lax.psum(loss_part[0, 0], ("x", "y", "c"))
    order = ("meta", "norm_w", "w_in", "conv_qkv_w", "a_log", "dt_bias", "dn_norm_w", "w_dn_out", "dw_w", "dw_b", "ln_w",
             "ln_b", "w_cf_out", "b_cf_out", "w_o", "final_norm_w")
    outs = [loss, grad_x[None]]
    for j in range(4):
        outs += [res[k][j] for k in order]
    return tuple(outs)
```

```python
import functools

import jax
import jax.numpy as jnp
from jax import lax
from jax.experimental import pallas as pl
from jax.experimental.pallas import tpu as pltpu

F32 = jnp.float32
BF16 = jnp.bfloat16
HI = lax.Precision.HIGHEST

D = 1024
H = 8
DK = 128
C = 64
NMETA = 16
KQ = 4
KD = 31
HALO_Q = 8
HALO_D = 32
EPS = 1e-6
NDEV = 8
LANE = 128
MIB = 1024 * 1024

ADAM_LR, ADAM_B1, ADAM_B2, ADAM_EPS, ADAM_WD, ADAM_STEP = 0.001, 0.9, 0.999, 1e-08, 0.01, 10

CB_Q, CB_K, CB_V, CB_ZA, CB_GA_, CB_GB_, CB_MA, CB_MB, CB_ZB = range(9)
NCB = 9


def _pick(n, cands):
    for c in cands:
        if n % c == 0:
            return c
    raise ValueError(f"no tile for {n}")


def _cp(sem=None, vmem_mib=40):
    kw = dict(vmem_limit_bytes=vmem_mib * MIB)
    if sem is not None:
        kw["dimension_semantics"] = sem
    return pltpu.CompilerParams(**kw)


def _call(body, **kw):
    return pl.pallas_call(body, **kw)


def _dot(a, b):
    return jnp.dot(a.astype(BF16), b.astype(BF16), preferred_element_type=F32)


def _dot_nt(a, b):
    return lax.dot_general(a.astype(BF16), b.astype(BF16), (((1,), (1,)), ((), ())), preferred_element_type=F32)


def _dot_tn(a, b):
    return lax.dot_general(a.astype(BF16), b.astype(BF16), (((0,), (0,)), ((), ())), preferred_element_type=F32)


def _dot_hi(a, b):
    return jnp.dot(a, b, precision=HI, preferred_element_type=F32)


def _sig(x):
    return 0.5 * jnp.tanh(0.5 * x) + 0.5


def _dsilu(x, s):
    return s * (1.0 + x * (1.0 - s))


def _rowsum(x):
    return jnp.sum(x, axis=-1, keepdims=True)


def _colsum(x):
    return jnp.sum(x, axis=0, keepdims=True)


def _exchange(arrs, scatter, name):
    n = len(arrs)
    out_shape = []
    for a, sc in zip(arrs, scatter):
        shp = a.shape if sc else (NDEV,) + a.shape
        out_shape.append(jax.ShapeDtypeStruct(shp, a.dtype))

    def body(*refs):
        ins, outs = refs[:n], refs[n:2 * n]
        send_sems, recv_sems, loc_sems = refs[2 * n:]
        x, y, c = lax.axis_index("x"), lax.axis_index("y"), lax.axis_index("c")
        me = 4 * x + 2 * y + c
        copies = []
        for a in range(n):
            for k in range(1, NDEV):
                px = 1 - x if (k >> 2) & 1 else x
                py = 1 - y if (k >> 1) & 1 else y
                pc = 1 - c if k & 1 else c
                src = ins[a].at[4 * px + 2 * py + pc] if scatter[a] else ins[a]
                cp = pltpu.make_async_remote_copy(
                    src_ref=src, dst_ref=outs[a].at[me],
                    send_sem=send_sems.at[a * (NDEV - 1) + k - 1], recv_sem=recv_sems.at[a * (NDEV - 1) + k - 1],
                    device_id=(px, py, pc), device_id_type=pl.DeviceIdType.MESH)
                cp.start()
                copies.append(cp)
            loc = pltpu.make_async_copy(ins[a].at[me] if scatter[a] else ins[a], outs[a].at[me], loc_sems.at[a])
            loc.start()
            copies.append(loc)
        for cp in copies:
            cp.wait()

    any_spec = pl.BlockSpec(memory_space=pl.ANY)
    return _call(
        body, name=name, out_shape=tuple(out_shape),
        in_specs=[any_spec] * n, out_specs=tuple([any_spec] * n),
        scratch_shapes=[pltpu.SemaphoreType.DMA((n * (NDEV - 1),)), pltpu.SemaphoreType.DMA((n * (NDEV - 1),)),
                        pltpu.SemaphoreType.DMA((n,))],
    )(*arrs)


NCHIP = 4


def _gather_two_level(arrs, name):
    n = len(arrs)
    per = NDEV - 1

    def body(*refs):
        ins, outs = refs[:n], refs[n:2 * n]
        send_sems, recv_sems, loc_sems = refs[2 * n:]
        x, y, c = lax.axis_index("x"), lax.axis_index("y"), lax.axis_index("c")
        me, sibling = (x, y, c), (x, y, 1 - c)
        chips = [(1 - x, y), (x, 1 - y), (1 - x, 1 - y)]

        def slot(a, px, py, pc):
            return outs[a].at[4 * px + 2 * py + pc]

        def copy(a, k, block, to, src=None):
            return pltpu.make_async_remote_copy(
                src_ref=slot(a, *block) if src is None else src, dst_ref=slot(a, *block),
                send_sem=send_sems.at[a * per + k], recv_sem=recv_sems.at[a * per + k],
                device_id=to, device_id_type=pl.DeviceIdType.MESH)

        local, sent = [], []
        for a in range(n):
            mine = pltpu.make_async_copy(ins[a], slot(a, *me), loc_sems.at[a])
            mine.start()
            local.append(mine)
            first = [copy(a, 1 + j, me, (*chip, c), src=ins[a]) for j, chip in enumerate(chips)]
            first.append(copy(a, 0, me, sibling, src=ins[a]))
            for cp in first:
                cp.start()
            sent += first
        for j, chip in enumerate(chips):
            for a in range(n):
                copy(a, 1 + j, (*chip, c), me).wait_recv()
                cp = copy(a, 4 + j, (*chip, c), sibling)
                cp.start()
                sent.append(cp)
        for a in range(n):
            copy(a, 0, sibling, me).wait_recv()
            for j, chip in enumerate(chips):
                copy(a, 4 + j, (*chip, 1 - c), me).wait_recv()
        for cp in sent:
            cp.wait_send()
        for cp in local:
            cp.wait()

    any_spec = pl.BlockSpec(memory_space=pl.ANY)
    return _call(
        body, name=name, out_shape=tuple(jax.ShapeDtypeStruct((NDEV,) + a.shape, a.dtype) for a in arrs),
        in_specs=[any_spec] * n, out_specs=tuple([any_spec] * n),
        scratch_shapes=[pltpu.SemaphoreType.DMA((n * per,)), pltpu.SemaphoreType.DMA((n * per,)),
                        pltpu.SemaphoreType.DMA((n,))],
    )(*arrs)


def _swap_sibling(arrs, name):
    n = len(arrs)

    def body(*refs):
        ins, outs = refs[:n], refs[n:2 * n]
        send_sems, recv_sems = refs[2 * n:]
        x, y, c = lax.axis_index("x"), lax.axis_index("y"), lax.axis_index("c")
        copies = []
        for a in range(n):
            for j in range(NCHIP):
                cp = pltpu.make_async_remote_copy(
                    src_ref=ins[a].at[j, 1 - c], dst_ref=outs[a].at[j],
                    send_sem=send_sems.at[a * NCHIP + j], recv_sem=recv_sems.at[a * NCHIP + j],
                    device_id=(x, y, 1 - c), device_id_type=pl.DeviceIdType.MESH)
                cp.start()
                copies.append(cp)
        for cp in copies:
            cp.wait()

    any_spec = pl.BlockSpec(memory_space=pl.ANY)
    return _call(
        body, name=name, out_shape=tuple(jax.ShapeDtypeStruct((NCHIP,) + a.shape[2:], a.dtype) for a in arrs),
        in_specs=[any_spec] * n, out_specs=tuple([any_spec] * n),
        scratch_shapes=[pltpu.SemaphoreType.DMA((n * NCHIP,)), pltpu.SemaphoreType.DMA((n * NCHIP,))],
    )(*arrs)


def _pair_add(arr4, got, name):
    _, _, r, n = arr4.shape
    tr, tc = _shard_tile(r, n)

    def body(a_ref, g_ref, p_ref, own_ref):
        c = lax.axis_index("c")
        my_chip = 2 * lax.axis_index("x") + lax.axis_index("y")
        s = jnp.where(c == 0, a_ref[0, 0], a_ref[0, 1]) + g_ref[0]
        p_ref[0] = s.astype(BF16)

        @pl.when(pl.program_id(2) == my_chip)
        def _():
            own_ref[...] = s

    return _call(
        body, name=name, grid=(r // tr, n // tc, NCHIP),
        in_specs=[pl.BlockSpec((1, 2, tr, tc), lambda i, k, j: (j, 0, i, k)),
                  pl.BlockSpec((1, tr, tc), lambda i, k, j: (j, i, k))],
        out_specs=(pl.BlockSpec((1, tr, tc), lambda i, k, j: (j, i, k)), pl.BlockSpec((tr, tc), lambda i, k, j: (i, k))),
        out_shape=(jax.ShapeDtypeStruct((NCHIP, r, n), BF16), jax.ShapeDtypeStruct((r, n), F32)),
        compiler_params=_cp(("parallel", "parallel", "arbitrary")),
    )(arr4, got)


def _shard_tile(r, n):
    return (128, n) if r % 128 == 0 else (r, 256)


def _all_copies(srcs, lands, send_sems, recv_sems):
    x, y, c = lax.axis_index("x"), lax.axis_index("y"), lax.axis_index("c")
    per = NDEV - 1
    copies = []
    for a in range(len(srcs)):
        for k in range(1, NDEV):
            px = 1 - x if (k >> 2) & 1 else x
            py = 1 - y if (k >> 1) & 1 else y
            pc = 1 - c if k & 1 else c
            copies.append(pltpu.make_async_remote_copy(
                src_ref=srcs[a], dst_ref=lands[a].at[4 * x + 2 * y + c],
                send_sem=send_sems.at[a * per + k - 1], recv_sem=recv_sems.at[a * per + k - 1],
                device_id=(px, py, pc), device_id_type=pl.DeviceIdType.MESH))
    return copies


def _split_start(make_copies, peers, name, arrs, lands, after):
    n = len(arrs)
    nsem = n * peers

    def body(*refs):
        srcs, land_in = refs[:n], refs[n:2 * n]
        send_sems, recv_sems = refs[2 * n + 1:2 * n + 3]
        token = refs[-1]
        for cp in make_copies(srcs, land_in, send_sems, recv_sems):
            cp.start()
        token[...] = jnp.zeros_like(token)

    hbm = pl.BlockSpec(memory_space=pltpu.HBM)
    sem = pl.BlockSpec(memory_space=pltpu.SEMAPHORE)
    both = list(arrs) + list(lands)
    outs = _call(
        body, name=name,
        out_shape=(pltpu.SemaphoreType.DMA((nsem,)), pltpu.SemaphoreType.DMA((nsem,)),
                   *[pltpu.HBM(a.shape, a.dtype) for a in both], jax.ShapeDtypeStruct((SUBLANES, LANE), F32)),
        in_specs=[hbm] * (2 * n) + [pl.BlockSpec(memory_space=pl.ANY)],
        out_specs=(sem, sem, *[hbm] * (2 * n), pl.BlockSpec(memory_space=pltpu.VMEM)),
        input_output_aliases={i: i + 2 for i in range(2 * n)},
        compiler_params=pltpu.CompilerParams(has_side_effects=pltpu.SideEffectType.DATAFLOW_SIDE_EFFECTING),
    )(*[pltpu.with_memory_space_constraint(t, pltpu.HBM) for t in both], after)
    return outs[0], outs[1], outs[2:2 + n], outs[2 + n:2 + 2 * n], outs[-1]


def _split_wait(make_copies, name, send_sems, recv_sems, arrs, lands, after):
    n = len(arrs)

    def body(*refs):
        srcs, land_in = refs[:n], refs[n:2 * n]
        send, recv = refs[2 * n], refs[2 * n + 1]
        for cp in make_copies(srcs, land_in, send, recv):
            cp.wait_send()
            cp.wait_recv()

    hbm = pl.BlockSpec(memory_space=pltpu.HBM)
    sem = pl.BlockSpec(memory_space=pltpu.SEMAPHORE)
    both = list(arrs) + list(lands)
    outs = _call(
        body, name=name,
        out_shape=tuple(pltpu.HBM(a.shape, a.dtype) for a in both),
        in_specs=[hbm] * (2 * n) + [sem, sem, pl.BlockSpec(memory_space=pl.ANY)], out_specs=tuple([hbm] * (2 * n)),
        input_output_aliases={i: i for i in range(2 * n)},
        compiler_params=pltpu.CompilerParams(has_side_effects=pltpu.SideEffectType.DATAFLOW_SIDE_EFFECTING),
    )(*both, send_sems, recv_sems, after)
    return outs[n:]


def _chip_copies(srcs, lands, send_sems, recv_sems):
    x, y, c = lax.axis_index("x"), lax.axis_index("y"), lax.axis_index("c")
    per = NCHIP - 1
    copies = []
    for a in range(len(srcs)):
        for k in range(1, NCHIP):
            px = 1 - x if (k >> 1) & 1 else x
            py = 1 - y if k & 1 else y
            copies.append(pltpu.make_async_remote_copy(
                src_ref=srcs[a].at[2 * px + py], dst_ref=lands[a].at[2 * x + y],
                send_sem=send_sems.at[a * per + k - 1], recv_sem=recv_sems.at[a * per + k - 1],
                device_id=(px, py, c), device_id_type=pl.DeviceIdType.MESH))
    return copies


def _mm_tn(a, b, name):
    t, m = a.shape
    n = b.shape[1]
    tt = _pick(t, (1664, 640, 128))
    tm = _pick(m, (1024, 512, 128))
    tn = _pick(n, (1152, 1024, 512, 128))
    nt = t // tt

    def body(a_ref, b_ref, o_ref):
        s = pl.program_id(2)
        part = _dot_tn(a_ref[...], b_ref[...])

        @pl.when(s == 0)
        def _():
            o_ref[...] = part

        @pl.when(s > 0)
        def _():
            o_ref[...] += part

    return _call(
        body, name=name, grid=(m // tm, n // tn, nt),
        in_specs=[pl.BlockSpec((tt, tm), lambda i, j, s: (s, i)), pl.BlockSpec((tt, tn), lambda i, j, s: (s, j))],
        out_specs=pl.BlockSpec((tm, tn), lambda i, j, s: (i, j)),
        out_shape=jax.ShapeDtypeStruct((m, n), F32),
        compiler_params=_cp(("parallel", "parallel", "arbitrary")),
    )(a, b)


def _proj_fwd(x_ext, norm_w, w_main_t, w_ba_t):
    lp = x_ext.shape[0]
    n = w_main_t.shape[0]
    tm = _pick(lp, (832, 640, 320))
    tn = 1024

    def body(x_ref, nw_ref, w_ref, wba_ref, proj_ref, ba_ref, h_ref):
        @pl.when(pl.program_id(1) == 0)
        def _():
            x = x_ref[...]
            r = lax.rsqrt(jnp.mean(x * x, axis=-1, keepdims=True) + EPS)
            h = (x * r * nw_ref[...]).astype(BF16)
            h_ref[...] = h
            ba_ref[...] = _dot_nt(h, wba_ref[...])

        proj_ref[...] = _dot_nt(h_ref[...], w_ref[...])

    return _call(
        body, name="proj_fwd", grid=(lp // tm, n // tn),
        in_specs=[pl.BlockSpec((tm, D), lambda i, j: (i, 0)), pl.BlockSpec((1, D), lambda i, j: (0, 0)),
                  pl.BlockSpec((tn, D), lambda i, j: (j, 0)), pl.BlockSpec((LANE, D), lambda i, j: (0, 0))],
        out_specs=(pl.BlockSpec((tm, tn), lambda i, j: (i, j)), pl.BlockSpec((tm, LANE), lambda i, j: (i, 0)),
                   pl.BlockSpec((tm, D), lambda i, j: (i, 0))),
        out_shape=(jax.ShapeDtypeStruct((lp, n), F32), jax.ShapeDtypeStruct((lp, LANE), F32),
                   jax.ShapeDtypeStruct((lp, D), BF16)),
        compiler_params=_cp(("parallel", "arbitrary")),
    )(x_ext, norm_w, w_main_t, w_ba_t)


def _dh_mm(dproj, dba, w_main_t, w_ba_t):
    lp, n = dproj.shape
    tm = _pick(lp, (832, 640, 320))
    tn = 1024
    tk = 2304
    nk = n // tk

    def body(a_ref, ba_ref, b_ref, bba_ref, o_ref, acc):
        kk = pl.program_id(2)

        @pl.when(kk == 0)
        def _():
            acc[...] = jnp.dot(ba_ref[...], bba_ref[...], preferred_element_type=F32)

        acc[...] += jnp.dot(a_ref[...], b_ref[...], preferred_element_type=F32)

        @pl.when(kk == nk - 1)
        def _():
            o_ref[...] = acc[...]

    return _call(
        body, name="dh_mm", grid=(lp // tm, D // tn, nk),
        in_specs=[pl.BlockSpec((tm, tk), lambda i, j, kk: (i, kk)), pl.BlockSpec((tm, LANE), lambda i, j, kk: (i, 0)),
                  pl.BlockSpec((tk, tn), lambda i, j, kk: (kk, j)), pl.BlockSpec((LANE, tn), lambda i, j, kk: (0, j))],
        out_specs=pl.BlockSpec((tm, tn), lambda i, j, kk: (i, j)),
        out_shape=jax.ShapeDtypeStruct((lp, D), F32),
        scratch_shapes=[pltpu.VMEM((tm, tn), F32)],
        compiler_params=_cp(("parallel", "parallel", "arbitrary")),
    )(dproj, dba, w_main_t, w_ba_t)


def _beta_g(ba, ab, row0, pad):
    lane = lax.broadcasted_iota(jnp.int32, ba.shape, 1)
    rows = row0 + lax.broadcasted_iota(jnp.int32, ba.shape, 0)
    z = ba + ab[1:2, :]
    sp = jnp.maximum(z, 0.0) + jnp.log(1.0 + jnp.exp(-jnp.abs(z)))
    val = jnp.where(lane < H, _sig(ba), -jnp.exp(ab[0:1, :]) * sp)
    return jnp.where((lane < 2 * H) & (rows >= pad), val, 0.0)


def _qkv_conv_fwd(proj, ba, conv_w, ab, pad):
    lp = proj.shape[0]
    te = _pick(lp, (320,))
    hb = te // HALO_Q

    def body(main_ref, halo_ref, cw_ref, ba_ref, ab_ref, out_ref, bg_ref, pre_scr, tap_scr):
        i, s = pl.program_id(0), pl.program_id(1)
        pre_scr[:HALO_Q, :] = jnp.where(i > 0, halo_ref[...], 0.0)
        pre_scr[HALO_Q:, :] = main_ref[...]
        scale = jnp.where(s == 0, DK ** -0.5, 1.0)
        off = HALO_Q - (KQ - 1)

        def head(h, carry):
            cs = pl.ds(pl.multiple_of(h * DK, DK), DK)
            for j in range(KQ - 1):
                tap_scr[j] = pre_scr[off + j:off + j + te, cs]
            co = cw_ref[KQ - 1:KQ, cs] * pre_scr[HALO_Q:, cs]
            for j in range(KQ - 1):
                co = co + cw_ref[j:j + 1, cs] * tap_scr[j]
            a = co * _sig(co)
            r = lax.rsqrt(_rowsum(a * a) + EPS)
            out_ref[:, cs] = jnp.where(s == 2, a, a * (r * scale))
            return carry

        lax.fori_loop(0, H, head, 0, unroll=True)

        @pl.when(s == 0)
        def _():
            bg_ref[...] = _beta_g(ba_ref[...], ab_ref[...], i * te, pad)

    return _call(
        body, name="qkv_conv_fwd", grid=(lp // te, 3),
        in_specs=[pl.BlockSpec((te, D), lambda i, s: (i, s)),
                  pl.BlockSpec((HALO_Q, D), lambda i, s: (jnp.maximum(i * hb - 1, 0), s)),
                  pl.BlockSpec((KQ, D), lambda i, s: (0, s)),
                  pl.BlockSpec((te, LANE), lambda i, s: (i, 0)),
                  pl.BlockSpec((2, LANE), lambda i, s: (0, 0))],
        out_specs=(pl.BlockSpec((te, D), lambda i, s: (i, s)), pl.BlockSpec((te, LANE), lambda i, s: (i, 0))),
        out_shape=(jax.ShapeDtypeStruct((lp, 3 * D), F32), jax.ShapeDtypeStruct((lp, LANE), F32)),
        scratch_shapes=[pltpu.VMEM((te + HALO_Q, D), F32), pltpu.VMEM((KQ - 1, te, DK), F32)],
        compiler_params=_cp(("parallel", "arbitrary")),
    )(proj, proj, conv_w, ba, ab)


def _tri_masks():
    row = lax.broadcasted_iota(jnp.int32, (C, C), 0)
    col = lax.broadcasted_iota(jnp.int32, (C, C), 1)
    return row, col


def _split(a):
    hi = a.astype(BF16)
    return hi, (a - hi.astype(F32)).astype(BF16)


def _dot3(a, b, dims=(((1,), (0,)), ((), ()))):
    (ah, al), (bh, bl) = a, b
    mm = lambda x, y: lax.dot_general(x, y, dims, preferred_element_type=F32)
    return mm(ah, bh) + (mm(ah, bl) + mm(al, bh))


CHUNKS_PER_STEP = 2
CHUNKS_PER_STEP_BWD = 1
TINV_BLOCK = 16


def _tinv(ns, row, col):
    eye = (row == col).astype(F32)
    sh = TINV_BLOCK.bit_length() - 1
    same16 = (row >> sh) == (col >> sh)
    same32 = (row >> (sh + 1)) == (col >> (sh + 1))
    ys = [jnp.where(same16, -n, 0.0) for n in ns]
    ts = [eye + y for y in ys]
    sp = [_split(y) for y in ys]
    for level in range(3):
        yks = [_dot3(s, s) for s in sp]
        sp = [_split(yk) for yk in yks]
        ts = [t + _dot3(s, _split(t)) for s, t in zip(sp, ts)]
    for mask in (same32 & ~same16, ~same32):
        tsp = [_split(t) for t in ts]
        inner = [_dot3(_split(jnp.where(mask, n, 0.0)), t) for n, t in zip(ns, tsp)]
        ts = [t - _dot3(tp, _split(a)) for t, tp, a in zip(ts, tsp, inner)]
    return ts


def _chunk_common(q, k, v, bcol, gcc, gcr, incl, strict):
    dm = jnp.where(incl, jnp.exp(gcc - gcr), 0.0)
    kk = _dot_nt(k, k)
    qk = _dot_nt(q, k)
    gccw = jnp.broadcast_to(gcc, (C, DK))
    egc = jnp.exp(gccw)
    glast = gccw[C - 1:C, :]
    eend = jnp.exp(glast - gccw)
    elast = jnp.exp(glast)
    rhs = jnp.concatenate([v * bcol, k * (bcol * egc)], axis=1)
    return dm, kk, qk, egc, eend, elast, rhs


def _delta_fwd(qkv, bg):
    lp = qkv.shape[0]
    nc = lp // C
    heads = range(H)
    sls = [slice(h * DK, (h + 1) * DK) for h in heads]

    def body(q_ref, k_ref, v_ref, bg_ref, o_ref, sall_ref, tall_ref, s_scr):
        @pl.when(pl.program_id(0) == 0)
        def _():
            s_scr[...] = jnp.zeros_like(s_scr)

        row, col = _tri_masks()
        incl, strict = row >= col, row > col

        def prepare(sub):
            rs = slice(sub * C, (sub + 1) * C)
            bgt = bg_ref[rs, :]
            gc_all = _dot_hi(incl.astype(F32), bgt)
            gc_t = _dot_hi(bgt.T, (row <= col).astype(F32))
            qs, ks, vs = ([r[rs, sl] for sl in sls] for r in (q_ref, k_ref, v_ref))
            bcols = [jnp.broadcast_to(bgt[:, h:h + 1], (C, DK)) for h in heads]
            cm = [_chunk_common(qs[h], ks[h], vs[h], bcols[h], gc_all[:, H + h:H + h + 1], gc_t[H + h:H + h + 1, :],
                                incl, strict) for h in heads]
            dms, kks, qks, egcs, eends, elasts, rhss = zip(*cm)
            ts = _tinv([jnp.where(strict, bcols[h][:, :C] * kks[h] * dms[h], 0.0) for h in heads], row, col)
            sols = [_dot3(_split(ts[h]), _split(rhss[h])) for h in heads]
            qgs = [(qs[h] * egcs[h]).astype(BF16) for h in heads]
            ps = [(qks[h] * dms[h]).astype(BF16) for h in heads]
            kends = [(ks[h] * eends[h]).astype(BF16) for h in heads]
            return ts, sols, qgs, ps, kends, elasts

        prepared = [prepare(sub) for sub in range(CHUNKS_PER_STEP)]
        ss = [s_scr[h] for h in heads]
        for sub in range(CHUNKS_PER_STEP):
            rs = slice(sub * C, (sub + 1) * C)
            ts, sols, qgs, ps, kends, elasts = prepared[sub]
            sb = [s.astype(BF16) for s in ss]
            wvb = [(sols[h][:, :DK] - _dot(sols[h][:, DK:], sb[h])).astype(BF16) for h in heads]
            for h in heads:
                o_ref[rs, sls[h]] = _dot(qgs[h], sb[h]) + _dot(ps[h], wvb[h])
                sall_ref[sub, h] = ss[h]
                tall_ref[sub, h] = ts[h]
            ss = [ss[h] * elasts[h] + _dot_tn(kends[h], wvb[h]) for h in heads]
        for h in heads:
            s_scr[h] = ss[h]

    rows = CHUNKS_PER_STEP * C
    blk = lambda j: pl.BlockSpec((rows, D), lambda n: (n, j))
    return _call(
        body, name="delta_fwd", grid=(nc // CHUNKS_PER_STEP,),
        in_specs=[blk(0), blk(1), blk(2), pl.BlockSpec((rows, LANE), lambda n: (n, 0))],
        out_specs=(pl.BlockSpec((rows, D), lambda n: (n, 0)),
                   pl.BlockSpec((CHUNKS_PER_STEP, H, DK, DK), lambda n: (n, 0, 0, 0)),
                   pl.BlockSpec((CHUNKS_PER_STEP, H, C, C), lambda n: (n, 0, 0, 0))),
        out_shape=(jax.ShapeDtypeStruct((lp, D), F32), jax.ShapeDtypeStruct((nc, H, DK, DK), F32),
                   jax.ShapeDtypeStruct((nc, H, C, C), F32)),
        scratch_shapes=[pltpu.VMEM((H, DK, DK), F32)],
        compiler_params=_cp(("arbitrary",)),
    )(qkv, qkv, qkv, bg)


def _delta_bwd(qkv, bg, sall, tall, do):
    lp = qkv.shape[0]
    nc = lp // C

    heads = range(H)
    sls = [slice(h * DK, (h + 1) * DK) for h in heads]

    def body(q_ref, k_ref, v_ref, bg_ref, sall_ref, tall_ref, do_ref, dqkv_ref, dbg_ref, ds_scr):
        @pl.when(pl.program_id(0) == 0)
        def _():
            ds_scr[...] = jnp.zeros_like(ds_scr)

        dsns = [ds_scr[h] for h in heads]
        for sub in reversed(range(CHUNKS_PER_STEP_BWD)):
            dsns = chunk(sub, dsns, q_ref, k_ref, v_ref, bg_ref, sall_ref, tall_ref, do_ref, dqkv_ref, dbg_ref)
        for h in heads:
            ds_scr[h] = dsns[h]

    def chunk(sub, dsns, q_ref, k_ref, v_ref, bg_ref, sall_ref, tall_ref, do_ref, dqkv_ref, dbg_ref):
        rs = slice(sub * C, (sub + 1) * C)
        bgt = bg_ref[rs, :]
        row, col = _tri_masks()
        incl, strict = row >= col, row > col
        upper = (row <= col).astype(F32)
        gc_all = _dot_hi(incl.astype(F32), bgt)
        gc_t = _dot_hi(bgt.T, upper)
        lane = lax.broadcasted_iota(jnp.int32, (C, LANE), 1)
        lastrow = lax.broadcasted_iota(jnp.int32, (C, 1), 0) == C - 1
        qs, ks, vs, dos = ([r[rs, sl] for sl in sls] for r in (q_ref, k_ref, v_ref, do_ref))
        bcols = [jnp.broadcast_to(bgt[:, h:h + 1], (C, DK)) for h in heads]
        cm = [_chunk_common(qs[h], ks[h], vs[h], bcols[h], gc_all[:, H + h:H + h + 1], gc_t[H + h:H + h + 1, :],
                            incl, strict) for h in heads]
        dms, kks, qks, egcs, eends, elasts, rhss = zip(*cm)
        ss = [sall_ref[sub, h] for h in heads]
        ts = [tall_ref[sub, h] for h in heads]
        sb = [s.astype(BF16) for s in ss]
        dsb = [d.astype(BF16) for d in dsns]
        dob = [d.astype(BF16) for d in dos]
        sols = [_dot3(_split(ts[h]), _split(rhss[h])) for h in heads]
        ws = [sol[:, DK:] for sol in sols]
        qgs = [qs[h] * egcs[h] for h in heads]
        kends = [ks[h] * eends[h] for h in heads]
        wvs = [sols[h][:, :DK] - _dot(ws[h], sb[h]) for h in heads]
        wvb = [wv.astype(BF16) for wv in wvs]
        dwvs = [_dot_tn(qks[h] * dms[h], dob[h]) + _dot(kends[h], dsb[h]) for h in heads]
        dps = [jnp.where(incl, _dot_nt(dob[h], wvb[h]), 0.0) for h in heads]
        dqgs = [_dot_nt(dob[h], sb[h]) for h in heads]
        dkends = [_dot_nt(wvb[h], dsb[h]) for h in heads]
        ds_before = [_dot_tn(qgs[h], dob[h]) + elasts[h] * dsns[h] - _dot_tn(ws[h], dwvs[h]) for h in heads]
        dglasts = [elasts[h] * jnp.sum(ss[h] * dsns[h], keepdims=True) for h in heads]
        dws = [-_dot_nt(dwvs[h], sb[h]) for h in heads]
        tts = [_split(ts[h].T) for h in heads]
        drhss = [_dot3(tts[h], _split(jnp.concatenate([dwvs[h], dws[h]], axis=1))) for h in heads]
        nt_dims = (((1,), (1,)), ((), ()))
        dns = [jnp.where(strict, -_dot3(_split(drhss[h]), _split(sols[h]), nt_dims), 0.0) for h in heads]
        dbeta_t = jnp.zeros((C, LANE), F32)
        dgc_t = jnp.zeros((C, LANE), F32)
        for h in heads:
            q, k, v, bcol, dm, kk, qk, egc, eend = qs[h], ks[h], vs[h], bcols[h], dms[h], kks[h], qks[h], egcs[h], eends[h]
            drv, drk = drhss[h][:, :DK], drhss[h][:, DK:]
            dn, dp, dqg, dkend = dns[h], dps[h], dqgs[h], dkends[h]
            rk = _rowsum(drk * k)
            dkk = dn * (bcol[:, :C] * dm)
            dqk = dp * dm
            e = (dn * (bcol[:, :C] * kk) + dp * qk) * dm
            tk = _rowsum(dkend * kends[h])
            dgc = rk * bcol * egc + _rowsum(e) - _rowsum(e.T) + _rowsum(dqg * qgs[h]) - tk
            dgc = dgc + jnp.where(lastrow, dglasts[h] + jnp.sum(tk, keepdims=True), 0.0)
            dbeta = _rowsum(drv * v) + rk * egc + _rowsum(dn * kk * dm)
            dqkv_ref[rs, sls[h]] = _dot(dqk, k) + dqg * egc
            dqkv_ref[rs, D + h * DK:D + (h + 1) * DK] = (drk * (bcol * egc) + _dot(dkk, k) + _dot_tn(dkk, k)
                                                        + _dot_tn(dqk, q) + dkend * eend)
            dqkv_ref[rs, 2 * D + h * DK:2 * D + (h + 1) * DK] = bcol * drv
            dbeta_t = jnp.where(lane == h, dbeta, dbeta_t)
            dgc_t = jnp.where(lane == H + h, dgc, dgc_t)
        dbg_ref[rs, :] = dbeta_t + _dot_hi(upper, dgc_t)
        return ds_before

    steps = nc // CHUNKS_PER_STEP_BWD
    rows = CHUNKS_PER_STEP_BWD * C
    rev = lambda n: steps - 1 - n
    blk = lambda j: pl.BlockSpec((rows, D), lambda n: (rev(n), j))
    return _call(
        body, name="delta_bwd", grid=(steps,),
        in_specs=[blk(0), blk(1), blk(2), pl.BlockSpec((rows, LANE), lambda n: (rev(n), 0)),
                  pl.BlockSpec((CHUNKS_PER_STEP_BWD, H, DK, DK), lambda n: (rev(n), 0, 0, 0)),
                  pl.BlockSpec((CHUNKS_PER_STEP_BWD, H, C, C), lambda n: (rev(n), 0, 0, 0)),
                  pl.BlockSpec((rows, D), lambda n: (rev(n), 0))],
        out_specs=(pl.BlockSpec((rows, 3 * D), lambda n: (rev(n), 0)),
                   pl.BlockSpec((rows, LANE), lambda n: (rev(n), 0))),
        out_shape=(jax.ShapeDtypeStruct((lp, 3 * D), F32), jax.ShapeDtypeStruct((lp, LANE), F32)),
        scratch_shapes=[pltpu.VMEM((H, DK, DK), F32)],
        compiler_params=_cp(("arbitrary",)),
    )(qkv, qkv, qkv, bg, sall, tall, do)


def _o_post_fwd(o, proj, dn_w, w_dn):
    lp = o.shape[0]
    te = _pick(lp, (640, 320))

    def body(o_ref, za_ref, w_ref, wdn_ref, out_ref, ya_ref):
        za = za_ref[...]
        gate = za * _sig(za)
        for h in range(H):
            sl = slice(h * DK, (h + 1) * DK)
            oh = o_ref[:, sl]
            r = lax.rsqrt(jnp.mean(oh * oh, axis=-1, keepdims=True) + EPS)
            out_ref[:, sl] = (oh * r * w_ref[...] * gate[:, sl]).astype(BF16)
        ya_ref[...] = _dot(out_ref[...], wdn_ref[...])

    row = pl.BlockSpec((te, D), lambda i: (i, 0))
    return _call(
        body, name="o_post_fwd", grid=(lp // te,),
        in_specs=[row, pl.BlockSpec((te, D), lambda i: (i, CB_ZA)), pl.BlockSpec((1, DK), lambda i: (0, 0)),
                  pl.BlockSpec((D, D), lambda i: (0, 0))],
        out_specs=(row, row),
        out_shape=(jax.ShapeDtypeStruct((lp, D), BF16), jax.ShapeDtypeStruct((lp, D), F32)),
        compiler_params=_cp(("parallel",)),
    )(o, proj, dn_w, w_dn)


def _o_post_bwd(dy_a, w_dn, o, proj, dn_w, dproj):
    lp = o.shape[0]
    te = _pick(lp, (320,))

    def body(dya_ref, wdn_ref, o_ref, za_ref, w_ref, _, do_ref, dza_ref, dw_ref, don_ref):
        @pl.when(pl.program_id(0) == 0)
        def _():
            dw_ref[...] = jnp.zeros_like(dw_ref)

        don_ref[...] = _dot_nt(dya_ref[...], wdn_ref[...])
        za = za_ref[...]
        sz = _sig(za)
        gate, dgate = za * sz, _dsilu(za, sz)
        w = w_ref[...]
        dw = jnp.zeros((1, DK), F32)
        for h in range(H):
            sl = slice(h * DK, (h + 1) * DK)
            oh, g = o_ref[:, sl], don_ref[:, sl]
            r = lax.rsqrt(jnp.mean(oh * oh, axis=-1, keepdims=True) + EPS)
            ohat = oh * r
            dza_ref[:, sl] = (g * ohat * w * dgate[:, sl]).astype(BF16)
            don = g * gate[:, sl]
            dw = dw + _colsum(don * ohat)
            dohat = don * w
            do_ref[:, sl] = r * (dohat - ohat * jnp.mean(dohat * ohat, axis=-1, keepdims=True))
        dw_ref[...] += dw

    return _call(
        body, name="o_post_bwd", grid=(lp // te,),
        in_specs=[pl.BlockSpec((te, D), lambda i: (i, 0)), pl.BlockSpec((D, D), lambda i: (0, 0)),
                  pl.BlockSpec((te, D), lambda i: (i, 0)),
                  pl.BlockSpec((te, D), lambda i: (i, CB_ZA)), pl.BlockSpec((1, DK), lambda i: (0, 0)),
                  pl.BlockSpec(memory_space=pl.ANY)],
        out_specs=(pl.BlockSpec((te, D), lambda i: (i, 0)), pl.BlockSpec((te, D), lambda i: (i, CB_ZA)),
                   pl.BlockSpec((1, DK), lambda i: (0, 0))),
        out_shape=(jax.ShapeDtypeStruct((lp, D), F32), jax.ShapeDtypeStruct(dproj.shape, dproj.dtype),
                   jax.ShapeDtypeStruct((1, DK), F32)),
        input_output_aliases={5: 1},
        scratch_shapes=[pltpu.VMEM((te, D), F32)],
        compiler_params=_cp(("arbitrary",)),
    )(dy_a, w_dn, o, proj, dn_w, dproj)


def _qkv_conv_bwd(proj, dqkv, conv_w, dproj):
    lp = proj.shape[0]
    te = _pick(lp, (320,))
    hb = te // HALO_Q
    nt = lp // te
    last_hb = lp // HALO_Q - 1

    def body(main_ref, prev_ref, next_ref, dmain_ref, dnext_ref, cw_ref, _, dpre_ref, dcw_ref, pre_scr, dn_scr,
             tap_scr, dco_scr, dsh_scr):
        s, i = pl.program_id(0), pl.program_id(1)

        @pl.when(i == 0)
        def _():
            dcw_ref[...] = jnp.zeros_like(dcw_ref)

        ne = te + HALO_Q
        pre_scr[:HALO_Q, :] = jnp.where(i > 0, prev_ref[...], 0.0)
        pre_scr[HALO_Q:ne, :] = main_ref[...]
        pre_scr[ne:, :] = jnp.where(i < nt - 1, next_ref[...], 0.0)
        dn_scr[:te, :] = dmain_ref[...]
        dn_scr[te:, :] = jnp.where(i < nt - 1, dnext_ref[...], 0.0)
        scale = jnp.where(s == 0, DK ** -0.5, 1.0)
        off = HALO_Q - (KQ - 1)

        def head(h, carry):
            cs = pl.ds(pl.multiple_of(h * DK, DK), DK)
            for j in range(KQ - 1):
                tap_scr[j] = pre_scr[off + j:off + j + ne, cs]
            taps = [tap_scr[j] for j in range(KQ - 1)] + [pre_scr[HALO_Q:, cs]]
            co = cw_ref[0:1, cs] * taps[0]
            for j in range(1, KQ):
                co = co + cw_ref[j:j + 1, cs] * taps[j]
            sg = _sig(co)
            a = co * sg
            g = dn_scr[:, cs]
            r = lax.rsqrt(_rowsum(a * a) + EPS)
            yhat = a * r
            da = jnp.where(s == 2, g, (scale * r) * (g - yhat * _rowsum(g * yhat)))
            dco = da * _dsilu(co, sg)
            dco_scr[...] = dco
            for j in range(KQ - 1):
                dsh_scr[j] = dco_scr[KQ - 1 - j:KQ - 1 - j + te, :]
            dpre = cw_ref[KQ - 1:KQ, cs] * dco[:te, :]
            for j in range(KQ - 1):
                dpre = dpre + cw_ref[j:j + 1, cs] * dsh_scr[j]
            dpre_ref[:, cs] = dpre.astype(BF16)
            dcw_ref[:, cs] += jnp.concatenate([_colsum(dco[:te] * taps[j][:te]) for j in range(KQ)], axis=0)
            return carry

        lax.fori_loop(0, H, head, 0, unroll=True)

    return _call(
        body, name="qkv_conv_bwd", grid=(3, nt),
        in_specs=[pl.BlockSpec((te, D), lambda s, i: (i, s)),
                  pl.BlockSpec((HALO_Q, D), lambda s, i: (jnp.maximum(i * hb - 1, 0), s)),
                  pl.BlockSpec((HALO_Q, D), lambda s, i: (jnp.minimum((i + 1) * hb, last_hb), s)),
                  pl.BlockSpec((te, D), lambda s, i: (i, s)),
                  pl.BlockSpec((HALO_Q, D), lambda s, i: (jnp.minimum((i + 1) * hb, last_hb), s)),
                  pl.BlockSpec((KQ, D), lambda s, i: (0, s)),
                  pl.BlockSpec(memory_space=pl.ANY)],
        out_specs=(pl.BlockSpec((te, D), lambda s, i: (i, s)), pl.BlockSpec((KQ, D), lambda s, i: (0, s))),
        out_shape=(jax.ShapeDtypeStruct(dproj.shape, dproj.dtype), jax.ShapeDtypeStruct((KQ, 3 * D), F32)),
        input_output_aliases={6: 0},
        scratch_shapes=[pltpu.VMEM((te + 2 * HALO_Q, D), F32), pltpu.VMEM((te + HALO_Q, D), F32),
                        pltpu.VMEM((KQ - 1, te + HALO_Q, DK), F32), pltpu.VMEM((te + HALO_Q, DK), F32),
                        pltpu.VMEM((KQ - 1, te, DK), F32)],
        compiler_params=_cp(("arbitrary", "arbitrary")),
    )(proj, proj, proj, dqkv, dqkv, conv_w, dproj)


def _ba_bwd(dbg, ba, ab, pad):
    lp = ba.shape[0]
    te = _pick(lp, (640, 320))

    def body(dbg_ref, ba_ref, ab_ref, dba_ref, dab_ref):
        i = pl.program_id(0)

        @pl.when(i == 0)
        def _():
            dab_ref[...] = jnp.zeros_like(dab_ref)

        ba, ab = ba_ref[...], ab_ref[...]
        lane = lax.broadcasted_iota(jnp.int32, ba.shape, 1)
        rows = i * te + lax.broadcasted_iota(jnp.int32, ba.shape, 0)
        g = jnp.where((lane < 2 * H) & (rows >= pad), dbg_ref[...], 0.0)
        sb = _sig(ba)
        z = ba + ab[1:2, :]
        sp = jnp.maximum(z, 0.0) + jnp.log(1.0 + jnp.exp(-jnp.abs(z)))
        nea = -jnp.exp(ab[0:1, :])
        dz = g * nea * _sig(z)
        dba_ref[...] = jnp.where(lane < H, g * sb * (1.0 - sb), dz).astype(BF16)
        is_g = (lane >= H) & (lane < 2 * H)
        dab_ref[...] += jnp.concatenate([_colsum(jnp.where(is_g, g * nea * sp, 0.0)),
                                         _colsum(jnp.where(is_g, dz, 0.0))], axis=0)

    return _call(
        body, name="ba_bwd", grid=(lp // te,),
        in_specs=[pl.BlockSpec((te, LANE), lambda i: (i, 0)), pl.BlockSpec((te, LANE), lambda i: (i, 0)),
                  pl.BlockSpec((2, LANE), lambda i: (0, 0))],
        out_specs=(pl.BlockSpec((te, LANE), lambda i: (i, 0)), pl.BlockSpec((2, LANE), lambda i: (0, 0))),
        out_shape=(jax.ShapeDtypeStruct((lp, LANE), BF16), jax.ShapeDtypeStruct((2, LANE), F32)),
        compiler_params=_cp(("arbitrary",)),
    )(dbg, ba, ab)


SUBLANES = 8
CONV_RB = 64


def _fill_shifted(sh_scr, src_scr, cs):
    n = sh_scr.shape[1]
    for s in range(1, SUBLANES):
        sh_scr[s] = src_scr[s:s + n, cs]


def _shifted(sh_scr, src_scr, cs, r, r0, n):
    s, a8 = r % SUBLANES, r - r % SUBLANES
    if s == 0:
        return src_scr[r0 + a8:r0 + a8 + n, cs]
    return sh_scr[s, r0 + a8:r0 + a8 + n, :]


def _conv_b_fwd(proj, dw_w, dw_b, ln_w, ln_b, w_cf):
    lp = proj.shape[0]
    te = _pick(lp, (320,))
    hb = te // HALO_D

    def body(a_ref, b_ref, ha_ref, hb_ref, zb_ref, w_ref, wb_ref, lw_ref, lb_ref, wcf_ref, c1_ref, c3_ref, yb_ref,
             c0_scr, sh_scr):
        i = pl.program_id(0)
        c0_scr[:HALO_D, :] = jnp.where(i > 0, ha_ref[...] * _sig(hb_ref[...]), 0.0)
        c0_scr[HALO_D:, :] = a_ref[...] * _sig(b_ref[...])
        off = HALO_D - (KD - 1)
        def lane_block(cb, carry):
            cs = pl.ds(pl.multiple_of(cb * LANE, LANE), LANE)
            _fill_shifted(sh_scr, c0_scr, cs)
            for r0 in range(0, te, CONV_RB):
                acc = None
                for j in range(KD):
                    term = w_ref[j:j + 1, cs] * _shifted(sh_scr, c0_scr, cs, off + j, r0, CONV_RB)
                    acc = term if acc is None else acc + term
                c1_ref[r0:r0 + CONV_RB, cs] = acc + wb_ref[:, cs]
            return carry

        lax.fori_loop(0, D // LANE, lane_block, 0)
        c1 = c1_ref[...]
        mu = jnp.mean(c1, axis=-1, keepdims=True)
        xc = c1 - mu
        c2 = xc * lax.rsqrt(jnp.mean(xc * xc, axis=-1, keepdims=True) + EPS) * lw_ref[...] + lb_ref[...]
        zb = zb_ref[...]
        c3 = (c2 * _sig(c2) * zb * _sig(zb)).astype(BF16)
        c3_ref[...] = c3
        yb_ref[...] = _dot(c3, wcf_ref[...])

    vec = pl.BlockSpec((1, D), lambda i: (0, 0))
    row = pl.BlockSpec((te, D), lambda i: (i, 0))
    return _call(
        body, name="conv_b_fwd", grid=(lp // te,),
        in_specs=[pl.BlockSpec((te, D), lambda i: (i, CB_GA_)), pl.BlockSpec((te, D), lambda i: (i, CB_GB_)),
                  pl.BlockSpec((HALO_D, D), lambda i: (jnp.maximum(i * hb - 1, 0), CB_GA_)),
                  pl.BlockSpec((HALO_D, D), lambda i: (jnp.maximum(i * hb - 1, 0), CB_GB_)),
                  pl.BlockSpec((te, D), lambda i: (i, CB_ZB)),
                  pl.BlockSpec((KD, D), lambda i: (0, 0)), vec, vec, vec, pl.BlockSpec((D, D), lambda i: (0, 0))],
        out_specs=(row, row, row),
        out_shape=(jax.ShapeDtypeStruct((lp, D), F32), jax.ShapeDtypeStruct((lp, D), BF16),
                   jax.ShapeDtypeStruct((lp, D), F32)),
        scratch_shapes=[pltpu.VMEM((te + HALO_D, D), F32), pltpu.VMEM((SUBLANES, te + HALO_D - SUBLANES, LANE), F32)],
        compiler_params=_cp(("parallel",)),
    )(proj, proj, proj, proj, proj, dw_w, dw_b, ln_w, ln_b, w_cf)


def _conv_b_bwd1(dy_b, w_cf, c1, proj, ln_w, ln_b, dproj):
    lp = c1.shape[0]
    te = _pick(lp, (320,))

    def body(dyb_ref, wcf_ref, c1_ref, zb_ref, lw_ref, lb_ref, _, dc1_ref, dzb_ref, sums_ref):
        @pl.when(pl.program_id(0) == 0)
        def _():
            sums_ref[...] = jnp.zeros_like(sums_ref)

        c1, g = c1_ref[...], _dot_nt(dyb_ref[...], wcf_ref[...])
        mu = jnp.mean(c1, axis=-1, keepdims=True)
        xc = c1 - mu
        rstd = lax.rsqrt(jnp.mean(xc * xc, axis=-1, keepdims=True) + EPS)
        xh = xc * rstd
        lw = lw_ref[...]
        c2 = xh * lw + lb_ref[...]
        s2 = _sig(c2)
        zb = zb_ref[...]
        sz = _sig(zb)
        dc2 = g * (zb * sz) * _dsilu(c2, s2)
        dzb_ref[...] = (g * (c2 * s2) * _dsilu(zb, sz)).astype(BF16)
        dxh = dc2 * lw
        dc1 = rstd * (dxh - jnp.mean(dxh, axis=-1, keepdims=True) - xh * jnp.mean(dxh * xh, axis=-1, keepdims=True))
        dc1_ref[...] = dc1
        sums_ref[...] += jnp.concatenate([_colsum(dc2 * xh), _colsum(dc2), _colsum(dc1)], axis=0)

    vec = pl.BlockSpec((1, D), lambda i: (0, 0))
    return _call(
        body, name="conv_b_bwd1", grid=(lp // te,),
        in_specs=[pl.BlockSpec((te, D), lambda i: (i, 0)), pl.BlockSpec((D, D), lambda i: (0, 0)),
                  pl.BlockSpec((te, D), lambda i: (i, 0)),
                  pl.BlockSpec((te, D), lambda i: (i, CB_ZB)), vec, vec, pl.BlockSpec(memory_space=pl.ANY)],
        out_specs=(pl.BlockSpec((te, D), lambda i: (i, 0)), pl.BlockSpec((te, D), lambda i: (i, CB_ZB)),
                   pl.BlockSpec((3, D), lambda i: (0, 0))),
        out_shape=(jax.ShapeDtypeStruct((lp, D), F32), jax.ShapeDtypeStruct(dproj.shape, dproj.dtype),
                   jax.ShapeDtypeStruct((3, D), F32)),
        input_output_aliases={6: 1},
        compiler_params=_cp(("arbitrary",)),
    )(dy_b, w_cf, c1, proj, ln_w, ln_b, dproj)


def _conv_b_bwd2(dc1, proj, dw_w, dproj):
    lp = dc1.shape[0]
    te = _pick(lp, (320,))
    hb = te // HALO_D
    nt = lp // te
    last_hb = lp // HALO_D - 1

    def body(g_ref, gn_ref, a_ref, b_ref, ha_ref, hb_ref, w_ref, _, dab_ref, dw_ref, c0_scr, g_scr, dc0_scr,
             csh_scr, gsh_scr):
        i = pl.program_id(0)

        @pl.when(i == 0)
        def _():
            dw_ref[...] = jnp.zeros_like(dw_ref)

        a, b = a_ref[...], b_ref[...]
        sb = _sig(b)
        c0_scr[:HALO_D, :] = jnp.where(i > 0, ha_ref[...] * _sig(hb_ref[...]), 0.0)
        c0_scr[HALO_D:, :] = a * sb
        g_scr[:te, :] = g_ref[...]
        g_scr[te:, :] = jnp.where(i < nt - 1, gn_ref[...], 0.0)
        off = HALO_D - (KD - 1)
        def lane_block(cb, carry):
            cs = pl.ds(pl.multiple_of(cb * LANE, LANE), LANE)
            _fill_shifted(csh_scr, c0_scr, cs)
            _fill_shifted(gsh_scr, g_scr, cs)
            for r0 in range(0, te, CONV_RB):
                acc = None
                for j in range(KD):
                    term = w_ref[j:j + 1, cs] * _shifted(gsh_scr, g_scr, cs, KD - 1 - j, r0, CONV_RB)
                    acc = term if acc is None else acc + term
                dc0_scr[r0:r0 + CONV_RB, cs] = acc
            parts = [None] * KD
            for r0 in range(0, te, CONV_RB):
                g = g_scr[r0:r0 + CONV_RB, cs].reshape(CONV_RB // SUBLANES, SUBLANES, LANE)
                for j in range(KD):
                    x = _shifted(csh_scr, c0_scr, cs, off + j, r0, CONV_RB)
                    p = jnp.sum(g * x.reshape(CONV_RB // SUBLANES, SUBLANES, LANE), axis=0)
                    parts[j] = p if parts[j] is None else parts[j] + p
            dw_ref[:, cs] += jnp.concatenate([_colsum(p) for p in parts], axis=0)
            return carry

        lax.fori_loop(0, D // LANE, lane_block, 0)
        dc0 = dc0_scr[...]
        dab_ref[:, :D] = (dc0 * sb).astype(BF16)
        dab_ref[:, D:] = (dc0 * a * sb * (1.0 - sb)).astype(BF16)

    return _call(
        body, name="conv_b_bwd2", grid=(nt,),
        in_specs=[pl.BlockSpec((te, D), lambda i: (i, 0)),
                  pl.BlockSpec((HALO_D, D), lambda i: (jnp.minimum((i + 1) * hb, last_hb), 0)),
                  pl.BlockSpec((te, D), lambda i: (i, CB_GA_)), pl.BlockSpec((te, D), lambda i: (i, CB_GB_)),
                  pl.BlockSpec((HALO_D, D), lambda i: (jnp.maximum(i * hb - 1, 0), CB_GA_)),
                  pl.BlockSpec((HALO_D, D), lambda i: (jnp.maximum(i * hb - 1, 0), CB_GB_)),
                  pl.BlockSpec((KD, D), lambda i: (0, 0)), pl.BlockSpec(memory_space=pl.ANY)],
        out_specs=(pl.BlockSpec((te, 2 * D), lambda i: (i, CB_GA_ // 2)), pl.BlockSpec((KD, D), lambda i: (0, 0))),
        out_shape=(jax.ShapeDtypeStruct(dproj.shape, dproj.dtype), jax.ShapeDtypeStruct((KD, D), F32)),
        input_output_aliases={7: 0},
        scratch_shapes=[pltpu.VMEM((te + HALO_D, D), F32), pltpu.VMEM((te + HALO_D, D), F32), pltpu.VMEM((te, D), F32),
                        pltpu.VMEM((SUBLANES, te + HALO_D - SUBLANES, LANE), F32),
                        pltpu.VMEM((SUBLANES, te + HALO_D - SUBLANES, LANE), F32)],
        compiler_params=_cp(("arbitrary",)),
    )(dc1, dc1, proj, proj, proj, proj, dw_w, dproj)


def _merge_fwd(y_a, y_b, proj, b_cf, w_o):
    lp = y_a.shape[0]
    te = _pick(lp, (320,))

    def body(ya_ref, yb_ref, ga_ref, gb_ref, bias_ref, wo_ref, out_ref, z_ref):
        merged = (_sig(ga_ref[...]) * ya_ref[...] + _sig(gb_ref[...]) * (yb_ref[...] + bias_ref[...])).astype(BF16)
        out_ref[...] = merged
        z_ref[...] = _dot(merged, wo_ref[...])

    row = lambda j: pl.BlockSpec((te, D), lambda i: (i, j))
    return _call(
        body, name="merge_fwd", grid=(lp // te,),
        in_specs=[row(0), row(0), row(CB_MA), row(CB_MB), pl.BlockSpec((1, D), lambda i: (0, 0)),
                  pl.BlockSpec((D, D), lambda i: (0, 0))],
        out_specs=(row(0), row(0)),
        out_shape=(jax.ShapeDtypeStruct((lp, D), BF16), jax.ShapeDtypeStruct((lp, D), F32)),
        compiler_params=_cp(("parallel",)),
    )(y_a, y_b, proj, proj, b_cf, w_o)


def _merge_bwd(dx_out_b, w_o, y_a, y_b, proj, b_cf):
    lp = y_a.shape[0]
    te = _pick(lp, (320,))

    def body(dx_ref, wo_ref, ya_ref, yb_ref, ga_ref, gb_ref, bias_ref, dya_ref, dyb_ref, dg_ref, db_ref):
        @pl.when(pl.program_id(0) == 0)
        def _():
            db_ref[...] = jnp.zeros_like(db_ref)

        dm = _dot_nt(dx_ref[...], wo_ref[...])
        sa, sb = _sig(ga_ref[...]), _sig(gb_ref[...])
        dyb = sb * dm
        dya_ref[...] = (sa * dm).astype(BF16)
        dyb_ref[...] = dyb.astype(BF16)
        dg_ref[:, :D] = (dm * ya_ref[...] * sa * (1.0 - sa)).astype(BF16)
        dg_ref[:, D:] = (dm * (yb_ref[...] + bias_ref[...]) * sb * (1.0 - sb)).astype(BF16)
        db_ref[...] += _colsum(dyb)

    row = lambda j: pl.BlockSpec((te, D), lambda i: (i, j))
    act = jax.ShapeDtypeStruct((lp, D), BF16)
    return _call(
        body, name="merge_bwd", grid=(lp // te,),
        in_specs=[row(0), pl.BlockSpec((D, D), lambda i: (0, 0)), row(0), row(0), row(CB_MA), row(CB_MB),
                  pl.BlockSpec((1, D), lambda i: (0, 0))],
        out_specs=(row(0), row(0), pl.BlockSpec((te, 2 * D), lambda i: (i, CB_MA // 2)),
                   pl.BlockSpec((1, D), lambda i: (0, 0))),
        out_shape=(act, act, jax.ShapeDtypeStruct((lp, NCB * D), BF16), jax.ShapeDtypeStruct((1, D), F32)),
        compiler_params=_cp(("arbitrary",)),
    )(dx_out_b, w_o, y_a, y_b, proj, proj, b_cf)


def _final_fwd_bwd(x_ext, z, target, final_w):
    lp = x_ext.shape[0]
    te = _pick(lp, (640,))
    nsub = te // LANE

    def body(x_ref, z_ref, *rest):
        t_refs, (w_ref, dx_ref, dxb_ref, loss_ref, dw_ref) = rest[:nsub], rest[nsub:]
        i = pl.program_id(0)

        @pl.when(i == 0)
        def _():
            loss_ref[...] = jnp.zeros_like(loss_ref)
            dw_ref[...] = jnp.zeros_like(dw_ref)

        w = w_ref[...]
        for k in range(nsub):
            rs = slice(k * LANE, (k + 1) * LANE)
            xo = x_ref[rs, :] + z_ref[rs, :]
            r = lax.rsqrt(jnp.mean(xo * xo, axis=-1, keepdims=True) + EPS)
            xhat = xo * r
            err = xhat * w - t_refs[k][...]
            if k == 0:
                err = jnp.where(i > 0, err, 0.0)
            loss_ref[...] += 0.5 * jnp.sum(jnp.mean(err * err, axis=-1, keepdims=True), keepdims=True)
            dy = err * (1.0 / D)
            dw_ref[...] += _colsum(dy * xhat)
            dxn = dy * w
            dx = r * (dxn - xhat * jnp.mean(dxn * xhat, axis=-1, keepdims=True))
            dx_ref[rs, :] = dx
            dxb_ref[rs, :] = dx.astype(BF16)

    piece = lambda k: pl.BlockSpec((LANE, D), lambda i: (jnp.maximum(i * nsub + k - 1, 0), 0))
    row = pl.BlockSpec((te, D), lambda i: (i, 0))
    return _call(
        body, name="final_fwd_bwd", grid=(lp // te,),
        in_specs=[row, row] + [piece(k) for k in range(nsub)] + [pl.BlockSpec((1, D), lambda i: (0, 0))],
        out_specs=(row, row, pl.BlockSpec((1, 1), lambda i: (0, 0)), pl.BlockSpec((1, D), lambda i: (0, 0))),
        out_shape=(jax.ShapeDtypeStruct((lp, D), F32), jax.ShapeDtypeStruct((lp, D), BF16),
                   jax.ShapeDtypeStruct((1, 1), F32), jax.ShapeDtypeStruct((1, D), F32)),
        compiler_params=_cp(("arbitrary",)),
    )(x_ext, z, *([target] * nsub), final_w)


def _prenorm_bwd(dh, x_ext, dx_out, norm_w, seq):
    lp = x_ext.shape[0]
    te = _pick(lp, (640,))
    nt = lp // te
    head = lp - seq

    def body(dh_ref, x_ref, dxo_ref, w_ref, gx_ref, head_ref, dw_ref, stage, sems):
        i = pl.program_id(0)
        slot = i % 2

        def first_copy():
            return pltpu.make_async_copy(stage.at[0, pl.ds(head, te - head)], gx_ref.at[pl.ds(0, te - head)], sems.at[0])

        def tile_copy(step, s):
            return pltpu.make_async_copy(stage.at[s], gx_ref.at[pl.ds(pl.multiple_of(step * te - head, LANE), te)],
                                         sems.at[s])

        @pl.when(i == 0)
        def _():
            dw_ref[...] = jnp.zeros_like(dw_ref)

        @pl.when(i == 2)
        def _():
            first_copy().wait()

        @pl.when(i > 2)
        def _():
            tile_copy(i - 2, slot).wait()

        x, dh = x_ref[...], dh_ref[...]
        r = lax.rsqrt(jnp.mean(x * x, axis=-1, keepdims=True) + EPS)
        xhat = x * r
        dxn = dh * w_ref[...]
        stage[slot] = dxo_ref[...] + r * (dxn - xhat * jnp.mean(dxn * xhat, axis=-1, keepdims=True))
        dw_ref[...] += _colsum(dh * xhat)

        @pl.when(i == 0)
        def _():
            head_ref[...] = stage[0, :head, :]
            first_copy().start()

        @pl.when(i > 0)
        def _():
            tile_copy(i, slot).start()

        @pl.when(i == nt - 1)
        def _():
            if nt >= 2:
                (first_copy() if nt == 2 else tile_copy(nt - 2, (nt - 2) % 2)).wait()
            (first_copy() if nt == 1 else tile_copy(nt - 1, (nt - 1) % 2)).wait()

    row = pl.BlockSpec((te, D), lambda i: (i, 0))
    return _call(
        body, name="prenorm_bwd", grid=(nt,),
        in_specs=[row, row, row, pl.BlockSpec((1, D), lambda i: (0, 0))],
        out_specs=(pl.BlockSpec(memory_space=pl.ANY), pl.BlockSpec((head, D), lambda i: (0, 0)),
                   pl.BlockSpec((1, D), lambda i: (0, 0))),
        out_shape=(jax.ShapeDtypeStruct((seq, D), F32), jax.ShapeDtypeStruct((head, D), F32),
                   jax.ShapeDtypeStruct((1, D), F32)),
        scratch_shapes=[pltpu.VMEM((2, te, D), F32), pltpu.SemaphoreType.DMA((2,))],
        compiler_params=_cp(("arbitrary",)),
    )(dh, x_ext, dx_out, norm_w)


def _adam_reduce(parts, w, m, v, name):
    r, n = w.shape
    tr = _pick(r, (128,)) if r % 128 == 0 else r

    def body(p_ref, w_ref, m_ref, v_ref, g_ref, d_ref, m2_ref, v2_ref):
        g = p_ref[0]
        for s in range(1, NDEV):
            g = g + p_ref[s]
        _adam_write(g, w_ref, m_ref, v_ref, g_ref, d_ref, m2_ref, v2_ref)

    blk = pl.BlockSpec((tr, n), lambda i: (i, 0))
    out = jax.ShapeDtypeStruct((r, n), F32)
    return _call(
        body, name=name, grid=(r // tr,),
        in_specs=[pl.BlockSpec((NDEV, tr, n), lambda i: (0, i, 0)), blk, blk, blk],
        out_specs=(blk, blk, blk, blk), out_shape=(out, out, out, out),
        compiler_params=_cp(("parallel",)),
    )(parts, w, m, v)


def _adam_write(g, w_ref, m_ref, v_ref, g_ref, d_ref, m2_ref, v2_ref):
    c1 = 1.0 - ADAM_B1 ** ADAM_STEP
    c2 = 1.0 - ADAM_B2 ** ADAM_STEP
    m2 = ADAM_B1 * m_ref[...] + (1.0 - ADAM_B1) * g
    v2 = ADAM_B2 * v_ref[...] + (1.0 - ADAM_B2) * (g * g)
    g_ref[...] = g
    m2_ref[...] = m2
    v2_ref[...] = v2
    d_ref[...] = -ADAM_LR * ((m2 / c1) / (jnp.sqrt(v2 / c2) + ADAM_EPS) + ADAM_WD * w_ref[...])


def _adam_chips(own, recv, w, m, v, name):
    r, n = w.shape
    tr, tc = _shard_tile(r, n)

    def body(own_ref, p_ref, w_ref, m_ref, v_ref, g_ref, d_ref, m2_ref, v2_ref):
        my_chip = 2 * lax.axis_index("x") + lax.axis_index("y")
        g = None
        for j in range(NCHIP):
            part = jnp.where(my_chip == j, own_ref[...], p_ref[j].astype(F32))
            g = part if g is None else g + part
        _adam_write(g, w_ref, m_ref, v_ref, g_ref, d_ref, m2_ref, v2_ref)

    blk = pl.BlockSpec((tr, tc), lambda i, k: (i, k))
    out = jax.ShapeDtypeStruct((r, n), F32)
    return _call(
        body, name=name, grid=(r // tr, n // tc),
        in_specs=[blk, pl.BlockSpec((NCHIP, tr, tc), lambda i, k: (0, i, k)), blk, blk, blk],
        out_specs=(blk, blk, blk, blk), out_shape=(out, out, out, out),
        compiler_params=_cp(("parallel", "parallel")),
    )(own, recv, w, m, v)


SMALL = ("norm_w", "a_log", "dt_bias", "dn_norm_w", "dw_b", "ln_w", "ln_b", "b_cf_out", "final_norm_w")


def kernel(x, meta, norm_w, w_in, conv_qkv_w, a_log, dt_bias, dn_norm_w, w_dn_out, dw_w, dw_b, ln_w, ln_b, w_cf_out, b_cf_out, w_o, final_norm_w, loss_target, m_meta, m_norm_w, m_w_in, m_conv_qkv_w, m_a_log, m_dt_bias, m_dn_norm_w, m_w_dn_out, m_dw_w, m_dw_b, m_ln_w, m_ln_b, m_w_cf_out, m_b_cf_out, m_w_o, m_final_norm_w, v_meta, v_norm_w, v_w_in, v_conv_qkv_w, v_a_log, v_dt_bias, v_dn_norm_w, v_w_dn_out, v_dw_w, v_dw_b, v_ln_w, v_ln_b, v_w_cf_out, v_b_cf_out, v_w_o, v_final_norm_w):
    seq = x.shape[1]
    pad = (-(seq + NMETA)) % LANE
    in_w = w_in.shape[2] * NDEV
    n_qkvz = 4 * D
    n_ba = 2 * H

    me = 4 * lax.axis_index("x") + 2 * lax.axis_index("y") + lax.axis_index("c")
    late = [w_dn_out[0].astype(BF16), w_cf_out[0].astype(BF16), w_o[0].astype(BF16), conv_qkv_w[0], dw_w[0]]
    late_lands = [lax.dynamic_update_index_in_dim(jnp.zeros((NDEV,) + b.shape, b.dtype), b, me, 0) for b in late]
    w_in_g, meta_g = _gather_two_level([w_in[0].astype(BF16).T, meta], "gather_weights")
    late_send, late_recv, late_thru, late_land_thru, _ = _split_start(
        _all_copies, NDEV - 1, "gather_late_start", late, late_lands, meta_g)
    w_full_t = w_in_g.reshape(in_w, D)
    c_glu = n_qkvz + n_ba
    c_zb, c_mg = c_glu + 2 * D, c_glu + 3 * D
    w_main_t = jnp.concatenate([w_full_t[:n_qkvz], w_full_t[c_glu:c_zb], w_full_t[c_mg:], w_full_t[c_zb:c_mg]],
                               axis=0)
    w_ba_t = jnp.pad(w_full_t[n_qkvz:n_qkvz + n_ba], ((0, LANE - n_ba), (0, 0)))
    meta_full = jnp.transpose(meta_g, (1, 0, 2)).reshape(NMETA, D)
    ab = jnp.pad(jnp.concatenate([a_log, dt_bias], axis=0), ((0, 0), (H, LANE - 2 * H)))

    x_ext = jnp.concatenate([jnp.zeros((pad, D), F32), meta_full, x[0]], axis=0)

    proj, ba, h = _proj_fwd(x_ext, norm_w, w_main_t, w_ba_t)
    w_dn_g, w_cf_g, w_o_g, cqw_g, dww_g = _split_wait(
        _all_copies, "gather_late_wait", late_send, late_recv, late_thru, late_land_thru, ba)
    w_dn, w_cf, w_oo = (t.reshape(D, D) for t in (w_dn_g, w_cf_g, w_o_g))
    cqw = jnp.transpose(cqw_g, (1, 0, 2)).reshape(KQ, 3 * D)
    dww = jnp.transpose(dww_g, (1, 0, 2)).reshape(KD, D)
    qkv, bg = _qkv_conv_fwd(proj, ba, cqw, ab, pad)
    o, sall, tall = _delta_fwd(qkv, bg)
    o_n, y_a = _o_post_fwd(o, proj, dn_norm_w, w_dn)
    c1, c3, y_b = _conv_b_fwd(proj, dww, dw_b, ln_w, ln_b, w_cf)
    merged, z = _merge_fwd(y_a, y_b, proj, b_cf_out, w_oo)
    dx_out, dx_out_b, loss_part, g_final_w = _final_fwd_bwd(x_ext, z, loss_target[0], final_norm_w.reshape(1, D))

    g_w_o = _mm_tn(merged, dx_out_b, "g_w_o_mm")
    dy_a, dy_b, dproj, g_b_cf = _merge_bwd(dx_out_b, w_oo, y_a, y_b, proj, b_cf_out)
    g_w_cf = _mm_tn(c3, dy_b, "g_w_cf_mm")
    g_w_dn = _mm_tn(o_n, dy_a, "g_w_dn_mm")
    dc1, dproj, sums_b = _conv_b_bwd1(dy_b, w_cf, c1, proj, ln_w, ln_b, dproj)
    dproj, g_dw_w = _conv_b_bwd2(dc1, proj, dww, dproj)
    do, dproj, g_dn_w = _o_post_bwd(dy_a, w_dn, o, proj, dn_norm_w, dproj)
    dqkv, dbg = _delta_bwd(qkv, bg, sall, tall, do)
    dproj, g_cqw = _qkv_conv_bwd(proj, dqkv, cqw, dproj)
    dba, dab = _ba_bwd(dbg, ba, ab, pad)
    g_w_main_t = _mm_tn(dproj, h, "g_w_main_mm")
    g_w_ba_t = _mm_tn(dba, h, "g_w_ba_mm")

    g_w_full_t = jnp.concatenate([g_w_main_t[:n_qkvz], g_w_ba_t[:n_ba], g_w_main_t[CB_GA_ * D:CB_MA * D],
                                  g_w_main_t[CB_ZB * D:], g_w_main_t[CB_MA * D:CB_ZB * D]], axis=0)
    big = [t.reshape(NCHIP, 2, t.shape[0] // NDEV, D) for t in (g_w_full_t, g_w_dn, g_w_cf, g_w_o)]
    from_sibling = _swap_sibling(big, "swap_sibling")
    pairs = [_pair_add(a, g, f"pair_add_{i}") for i, (a, g) in enumerate(zip(big, from_sibling))]
    send_sems, recv_sems, pair_thru, land_thru, token = _split_start(
        _chip_copies, NCHIP - 1, "scatter_chips_start",
        [p for p, _ in pairs], [jnp.zeros(p.shape, p.dtype) for p, _ in pairs], g_w_ba_t)

    dh = _dh_mm(dproj, dba, w_main_t, w_ba_t + token[0, 0].astype(BF16))
    grad_x, dhead, g_norm_w = _prenorm_bwd(dh, x_ext, dx_out, norm_w, seq)
    from_chips = _split_wait(_chip_copies, "scatter_chips_wait", send_sems, recv_sems, pair_thru, land_thru, g_norm_w)

    split_cols = lambda t: jnp.transpose(t.reshape(t.shape[0], NDEV, t.shape[1] // NDEV), (1, 0, 2))
    small = {"norm_w": g_norm_w, "a_log": dab[0:1, H:2 * H], "dt_bias": dab[1:2, H:2 * H], "dn_norm_w": g_dn_w,
             "dw_b": sums_b[2:3], "ln_w": sums_b[0:1], "ln_b": sums_b[1:2], "b_cf_out": g_b_cf,
             "final_norm_w": g_final_w}
    small_vec = jnp.concatenate([small[k] for k in SMALL], axis=1)
    ns = small_vec.shape[1]
    ns_pad = (-ns) % LANE
    small_vec = jnp.pad(small_vec, ((0, 0), (0, ns_pad)))
    p_meta, p_cqw, p_dww, p_small = _exchange(
        [split_cols(dhead[pad:pad + NMETA]), split_cols(g_cqw), split_cols(g_dw_w), small_vec],
        [True] * 3 + [False], "exchange_small")

    res = {}
    res["w_in"] = tuple(t.T for t in _adam_chips(pairs[0][1], from_chips[0], w_in[0].T, m_w_in[0].T, v_w_in[0].T,
                                                   "adam_w_in"))
    res["w_dn_out"] = _adam_chips(pairs[1][1], from_chips[1], w_dn_out[0], m_w_dn_out[0], v_w_dn_out[0], "adam_w_dn")
    res["w_cf_out"] = _adam_chips(pairs[2][1], from_chips[2], w_cf_out[0], m_w_cf_out[0], v_w_cf_out[0], "adam_w_cf")
    res["w_o"] = _adam_chips(pairs[3][1], from_chips[3], w_o[0], m_w_o[0], v_w_o[0], "adam_w_o")
    res["meta"] = _adam_reduce(p_meta, meta, m_meta, v_meta, "adam_meta")
    res["conv_qkv_w"] = _adam_reduce(p_cqw, conv_qkv_w[0], m_conv_qkv_w[0], v_conv_qkv_w[0], "adam_conv_qkv_w")
    res["dw_w"] = _adam_reduce(p_dww, dw_w[0], m_dw_w[0], v_dw_w[0], "adam_dw_w")
    loc = dict(norm_w=(norm_w, m_norm_w, v_norm_w), a_log=(a_log, m_a_log, v_a_log), dt_bias=(dt_bias, m_dt_bias, v_dt_bias),
               dn_norm_w=(dn_norm_w, m_dn_norm_w, v_dn_norm_w), dw_b=(dw_b, m_dw_b, v_dw_b), ln_w=(ln_w, m_ln_w, v_ln_w),
               ln_b=(ln_b, m_ln_b, v_ln_b), b_cf_out=(b_cf_out, m_b_cf_out, v_b_cf_out),
               final_norm_w=(final_norm_w, m_final_norm_w, v_final_norm_w))
    cat = lambda j: jnp.pad(jnp.concatenate([loc[k][j].reshape(1, -1) for k in SMALL], axis=1), ((0, 0), (0, ns_pad)))
    small_res = _adam_reduce(p_small, cat(0), cat(1), cat(2), "adam_small")
    off = 0
    for k in SMALL:
        wshape = loc[k][0].shape
        nk = loc[k][0].size
        res[k] = tuple(t[:, off:off + nk].reshape(wshape) for t in small_res)
        off += nk
    shaped = dict(w_in=w_in.shape, w_dn_out=w_dn_out.shape, w_cf_out=w_cf_out.shape, w_o=w_o.shape, meta=meta.shape,
                  conv_qkv_w=conv_qkv_w.shape, dw_w=dw_w.shape)
    for k, shp in shaped.items():
        res[k] = tuple(t.reshape(shp) for t in res[k])

    loss = lax.psum(loss_part[0, 0], ("x", "y", "c"))
    order = ("meta", "norm_w", "w_in", "conv_qkv_w", "a_log", "dt_bias", "dn_norm_w", "w_dn_out", "dw_w", "dw_b", "ln_w",
             "ln_b", "w_cf_out", "b_cf_out", "w_o", "final_norm_w")
    outs = [loss, grad_x[None]]
    for j in range(4):
        outs += [res[k][j] for k in order]
    return tuple(outs)
```

```python
import functools

import jax
import jax.numpy as jnp
from jax import lax
from jax.experimental import pallas as pl
from jax.experimental.pallas import tpu as pltpu

F32 = jnp.float32
BF16 = jnp.bfloat16
HI = lax.Precision.HIGHEST

D = 1024
H = 8
DK = 128
C = 64
NMETA = 16
KQ = 4
KD = 31
HALO_Q = 8
HALO_D = 32
EPS = 1e-6
NDEV = 8
LANE = 128
MIB = 1024 * 1024

ADAM_LR, ADAM_B1, ADAM_B2, ADAM_EPS, ADAM_WD, ADAM_STEP = 0.001, 0.9, 0.999, 1e-08, 0.01, 10

CB_Q, CB_K, CB_V, CB_ZA, CB_GA_, CB_GB_, CB_MA, CB_MB, CB_ZB = range(9)
NCB = 9


def _pick(n, cands):
    for c in cands:
        if n % c == 0:
            return c
    raise ValueError(f"no tile for {n}")


def _cp(sem=None, vmem_mib=40):
    kw = dict(vmem_limit_bytes=vmem_mib * MIB)
    if sem is not None:
        kw["dimension_semantics"] = sem
    return pltpu.CompilerParams(**kw)


def _call(body, **kw):
    return pl.pallas_call(body, **kw)


def _dot(a, b):
    return jnp.dot(a.astype(BF16), b.astype(BF16), preferred_element_type=F32)


def _dot_nt(a, b):
    return lax.dot_general(a.astype(BF16), b.astype(BF16), (((1,), (1,)), ((), ())), preferred_element_type=F32)


def _dot_tn(a, b):
    return lax.dot_general(a.astype(BF16), b.astype(BF16), (((0,), (0,)), ((), ())), preferred_element_type=F32)


def _dot_hi(a, b):
    return jnp.dot(a, b, precision=HI, preferred_element_type=F32)


def _sig(x):
    return 0.5 * jnp.tanh(0.5 * x) + 0.5


def _dsilu(x, s):
    return s * (1.0 + x * (1.0 - s))


def _rowsum(x):
    return jnp.sum(x, axis=-1, keepdims=True)


def _colsum(x):
    return jnp.sum(x, axis=0, keepdims=True)


def _exchange(arrs, scatter, name):
    n = len(arrs)
    out_shape = []
    for a, sc in zip(arrs, scatter):
        shp = a.shape if sc else (NDEV,) + a.shape
        out_shape.append(jax.ShapeDtypeStruct(shp, a.dtype))

    def body(*refs):
        ins, outs = refs[:n], refs[n:2 * n]
        send_sems, recv_sems, loc_sems = refs[2 * n:]
        x, y, c = lax.axis_index("x"), lax.axis_index("y"), lax.axis_index("c")
        me = 4 * x + 2 * y + c
        copies = []
        for a in range(n):
            for k in range(1, NDEV):
                px = 1 - x if (k >> 2) & 1 else x
                py = 1 - y if (k >> 1) & 1 else y
                pc = 1 - c if k & 1 else c
                src = ins[a].at[4 * px + 2 * py + pc] if scatter[a] else ins[a]
                cp = pltpu.make_async_remote_copy(
                    src_ref=src, dst_ref=outs[a].at[me],
                    send_sem=send_sems.at[a * (NDEV - 1) + k - 1], recv_sem=recv_sems.at[a * (NDEV - 1) + k - 1],
                    device_id=(px, py, pc), device_id_type=pl.DeviceIdType.MESH)
                cp.start()
                copies.append(cp)
            loc = pltpu.make_async_copy(ins[a].at[me] if scatter[a] else ins[a], outs[a].at[me], loc_sems.at[a])
            loc.start()
            copies.append(loc)
        for cp in copies:
            cp.wait()

    any_spec = pl.BlockSpec(memory_space=pl.ANY)
    return _call(
        body, name=name, out_shape=tuple(out_shape),
        in_specs=[any_spec] * n, out_specs=tuple([any_spec] * n),
        scratch_shapes=[pltpu.SemaphoreType.DMA((n * (NDEV - 1),)), pltpu.SemaphoreType.DMA((n * (NDEV - 1),)),
                        pltpu.SemaphoreType.DMA((n,))],
    )(*arrs)


NCHIP = 4


def _gather_two_level(arrs, name):
    n = len(arrs)
    per = NDEV - 1

    def body(*refs):
        ins, outs = refs[:n], refs[n:2 * n]
        send_sems, recv_sems, loc_sems = refs[2 * n:]
        x, y, c = lax.axis_index("x"), lax.axis_index("y"), lax.axis_index("c")
        me, sibling = (x, y, c), (x, y, 1 - c)
        chips = [(1 - x, y), (x, 1 - y), (1 - x, 1 - y)]

        def slot(a, px, py, pc):
            return outs[a].at[4 * px + 2 * py + pc]

        def copy(a, k, block, to, src=None):
            return pltpu.make_async_remote_copy(
                src_ref=slot(a, *block) if src is None else src, dst_ref=slot(a, *block),
                send_sem=send_sems.at[a * per + k], recv_sem=recv_sems.at[a * per + k],
                device_id=to, device_id_type=pl.DeviceIdType.MESH)

        local, sent = [], []
        for a in range(n):
            mine = pltpu.make_async_copy(ins[a], slot(a, *me), loc_sems.at[a])
            mine.start()
            local.append(mine)
            first = [copy(a, 1 + j, me, (*chip, c), src=ins[a]) for j, chip in enumerate(chips)]
            first.append(copy(a, 0, me, sibling, src=ins[a]))
            for cp in first:
                cp.start()
            sent += first
        for j, chip in enumerate(chips):
            for a in range(n):
                copy(a, 1 + j, (*chip, c), me).wait_recv()
                cp = copy(a, 4 + j, (*chip, c), sibling)
                cp.start()
                sent.append(cp)
        for a in range(n):
            copy(a, 0, sibling, me).wait_recv()
            for j, chip in enumerate(chips):
                copy(a, 4 + j, (*chip, 1 - c), me).wait_recv()
        for cp in sent:
            cp.wait_send()
        for cp in local:
            cp.wait()

    any_spec = pl.BlockSpec(memory_space=pl.ANY)
    return _call(
        body, name=name, out_shape=tuple(jax.ShapeDtypeStruct((NDEV,) + a.shape, a.dtype) for a in arrs),
        in_specs=[any_spec] * n, out_specs=tuple([any_spec] * n),
        scratch_shapes=[pltpu.SemaphoreType.DMA((n * per,)), pltpu.SemaphoreType.DMA((n * per,)),
                        pltpu.SemaphoreType.DMA((n,))],
    )(*arrs)


def _pair_add(arr4, got, name):
    _, _, r, n = arr4.shape
    tr, tc = _shard_tile(r, n)

    def body(a_ref, g_ref, p_ref, own_ref):
        c = lax.axis_index("c")
        my_chip = 2 * lax.axis_index("x") + lax.axis_index("y")
        s = jnp.where(c == 0, a_ref[0, 0], a_ref[0, 1]) + g_ref[0]
        p_ref[0] = s.astype(BF16)

        @pl.when(pl.program_id(2) == my_chip)
        def _():
            own_ref[...] = s

    return _call(
        body, name=name, grid=(r // tr, n // tc, NCHIP),
        in_specs=[pl.BlockSpec((1, 2, tr, tc), lambda i, k, j: (j, 0, i, k)),
                  pl.BlockSpec((1, tr, tc), lambda i, k, j: (j, i, k))],
        out_specs=(pl.BlockSpec((1, tr, tc), lambda i, k, j: (j, i, k)), pl.BlockSpec((tr, tc), lambda i, k, j: (i, k))),
        out_shape=(jax.ShapeDtypeStruct((NCHIP, r, n), BF16), jax.ShapeDtypeStruct((r, n), F32)),
        compiler_params=_cp(("parallel", "parallel", "arbitrary")),
    )(arr4, got)


def _shard_tile(r, n):
    return (128, n) if r % 128 == 0 else (r, 256)


def _all_copies(srcs, lands, send_sems, recv_sems):
    x, y, c = lax.axis_index("x"), lax.axis_index("y"), lax.axis_index("c")
    per = NDEV - 1
    copies = []
    for a in range(len(srcs)):
        for k in range(1, NDEV):
            px = 1 - x if (k >> 2) & 1 else x
            py = 1 - y if (k >> 1) & 1 else y
            pc = 1 - c if k & 1 else c
            copies.append(pltpu.make_async_remote_copy(
                src_ref=srcs[a], dst_ref=lands[a].at[4 * x + 2 * y + c],
                send_sem=send_sems.at[a * per + k - 1], recv_sem=recv_sems.at[a * per + k - 1],
                device_id=(px, py, pc), device_id_type=pl.DeviceIdType.MESH))
    return copies


def _split_start(make_copies, peers, name, arrs, lands, after):
    n = len(arrs)
    nsem = n * peers

    def body(*refs):
        srcs, land_in = refs[:n], refs[n:2 * n]
        send_sems, recv_sems = refs[2 * n + 1:2 * n + 3]
        token = refs[-1]
        for cp in make_copies(srcs, land_in, send_sems, recv_sems):
            cp.start()
        token[...] = jnp.zeros_like(token)

    hbm = pl.BlockSpec(memory_space=pltpu.HBM)
    sem = pl.BlockSpec(memory_space=pltpu.SEMAPHORE)
    both = list(arrs) + list(lands)
    outs = _call(
        body, name=name,
        out_shape=(pltpu.SemaphoreType.DMA((nsem,)), pltpu.SemaphoreType.DMA((nsem,)),
                   *[pltpu.HBM(a.shape, a.dtype) for a in both], jax.ShapeDtypeStruct((SUBLANES, LANE), F32)),
        in_specs=[hbm] * (2 * n) + [pl.BlockSpec(memory_space=pl.ANY)],
        out_specs=(sem, sem, *[hbm] * (2 * n), pl.BlockSpec(memory_space=pltpu.VMEM)),
        input_output_aliases={i: i + 2 for i in range(2 * n)},
        compiler_params=pltpu.CompilerParams(has_side_effects=pltpu.SideEffectType.DATAFLOW_SIDE_EFFECTING),
    )(*[pltpu.with_memory_space_constraint(t, pltpu.HBM) for t in both], after)
    return outs[0], outs[1], outs[2:2 + n], outs[2 + n:2 + 2 * n], outs[-1]


def _split_wait(make_copies, name, send_sems, recv_sems, arrs, lands, after):
    n = len(arrs)

    def body(*refs):
        srcs, land_in = refs[:n], refs[n:2 * n]
        send, recv = refs[2 * n], refs[2 * n + 1]
        for cp in make_copies(srcs, land_in, send, recv):
            cp.wait_send()
            cp.wait_recv()

    hbm = pl.BlockSpec(memory_space=pltpu.HBM)
    sem = pl.BlockSpec(memory_space=pltpu.SEMAPHORE)
    both = list(arrs) + list(lands)
    outs = _call(
        body, name=name,
        out_shape=tuple(pltpu.HBM(a.shape, a.dtype) for a in both),
        in_specs=[hbm] * (2 * n) + [sem, sem, pl.BlockSpec(memory_space=pl.ANY)], out_specs=tuple([hbm] * (2 * n)),
        input_output_aliases={i: i for i in range(2 * n)},
        compiler_params=pltpu.CompilerParams(has_side_effects=pltpu.SideEffectType.DATAFLOW_SIDE_EFFECTING),
    )(*both, send_sems, recv_sems, after)
    return outs[:n], outs[n:]


def _sibling_copies(srcs, lands, send_sems, recv_sems):
    x, y, c = lax.axis_index("x"), lax.axis_index("y"), lax.axis_index("c")
    copies = []
    for a in range(len(srcs)):
        for j in range(NCHIP):
            copies.append(pltpu.make_async_remote_copy(
                src_ref=srcs[a].at[j, 1 - c], dst_ref=lands[a].at[j],
                send_sem=send_sems.at[a * NCHIP + j], recv_sem=recv_sems.at[a * NCHIP + j],
                device_id=(x, y, 1 - c), device_id_type=pl.DeviceIdType.MESH))
    return copies


def _chip_copies(srcs, lands, send_sems, recv_sems):
    x, y, c = lax.axis_index("x"), lax.axis_index("y"), lax.axis_index("c")
    per = NCHIP - 1
    copies = []
    for a in range(len(srcs)):
        for k in range(1, NCHIP):
            px = 1 - x if (k >> 1) & 1 else x
            py = 1 - y if k & 1 else y
            copies.append(pltpu.make_async_remote_copy(
                src_ref=srcs[a].at[2 * px + py], dst_ref=lands[a].at[2 * x + y],
                send_sem=send_sems.at[a * per + k - 1], recv_sem=recv_sems.at[a * per + k - 1],
                device_id=(px, py, c), device_id_type=pl.DeviceIdType.MESH))
    return copies


def _mm_tn(a, b, name):
    t, m = a.shape
    n = b.shape[1]
    tt = _pick(t, (1664, 640, 128))
    tm = _pick(m, (1024, 512, 128))
    tn = _pick(n, (1152, 1024, 512, 128))
    nt = t // tt

    def body(a_ref, b_ref, o_ref):
        s = pl.program_id(2)
        part = _dot_tn(a_ref[...], b_ref[...])

        @pl.when(s == 0)
        def _():
            o_ref[...] = part

        @pl.when(s > 0)
        def _():
            o_ref[...] += part

    return _call(
        body, name=name, grid=(m // tm, n // tn, nt),
        in_specs=[pl.BlockSpec((tt, tm), lambda i, j, s: (s, i)), pl.BlockSpec((tt, tn), lambda i, j, s: (s, j))],
        out_specs=pl.BlockSpec((tm, tn), lambda i, j, s: (i, j)),
        out_shape=jax.ShapeDtypeStruct((m, n), F32),
        compiler_params=_cp(("parallel", "parallel", "arbitrary")),
    )(a, b)


def _proj_fwd(x_ext, norm_w, w_main_t, w_ba_t):
    lp = x_ext.shape[0]
    n = w_main_t.shape[0]
    tm = _pick(lp, (832, 640, 320))
    tn = 1024

    def body(x_ref, nw_ref, w_ref, wba_ref, proj_ref, ba_ref, h_ref):
        @pl.when(pl.program_id(1) == 0)
        def _():
            x = x_ref[...]
            r = lax.rsqrt(jnp.mean(x * x, axis=-1, keepdims=True) + EPS)
            h = (x * r * nw_ref[...]).astype(BF16)
            h_ref[...] = h
            ba_ref[...] = _dot_nt(h, wba_ref[...])

        proj_ref[...] = _dot_nt(h_ref[...], w_ref[...])

    return _call(
        body, name="proj_fwd", grid=(lp // tm, n // tn),
        in_specs=[pl.BlockSpec((tm, D), lambda i, j: (i, 0)), pl.BlockSpec((1, D), lambda i, j: (0, 0)),
                  pl.BlockSpec((tn, D), lambda i, j: (j, 0)), pl.BlockSpec((LANE, D), lambda i, j: (0, 0))],
        out_specs=(pl.BlockSpec((tm, tn), lambda i, j: (i, j)), pl.BlockSpec((tm, LANE), lambda i, j: (i, 0)),
                   pl.BlockSpec((tm, D), lambda i, j: (i, 0))),
        out_shape=(jax.ShapeDtypeStruct((lp, n), F32), jax.ShapeDtypeStruct((lp, LANE), F32),
                   jax.ShapeDtypeStruct((lp, D), BF16)),
        compiler_params=_cp(("parallel", "arbitrary")),
    )(x_ext, norm_w, w_main_t, w_ba_t)


def _dh_mm(dproj, dba, w_main_t, w_ba_t, part, dh_so_far=None):
    lp, n = dproj.shape
    tm = _pick(lp, (832, 640, 320))
    tn = 1024
    tk = 2304
    nk = n // tk
    tiles = lp // tm
    first = (tiles + 1) // 2
    t0, nt = (0, first) if part == 0 else (first, tiles - first)
    if nt == 0:
        return dh_so_far

    def body(a_ref, ba_ref, b_ref, bba_ref, *rest):
        o_ref, acc = rest[-2:]
        kk = pl.program_id(2)

        @pl.when(kk == 0)
        def _():
            acc[...] = jnp.dot(ba_ref[...], bba_ref[...], preferred_element_type=F32)

        acc[...] += jnp.dot(a_ref[...], b_ref[...], preferred_element_type=F32)

        @pl.when(kk == nk - 1)
        def _():
            o_ref[...] = acc[...]

    prev = [] if dh_so_far is None else [dh_so_far]
    return _call(
        body, name=f"dh_mm_{part}", grid=(nt, D // tn, nk),
        in_specs=[pl.BlockSpec((tm, tk), lambda i, j, kk: (i + t0, kk)),
                  pl.BlockSpec((tm, LANE), lambda i, j, kk: (i + t0, 0)),
                  pl.BlockSpec((tk, tn), lambda i, j, kk: (kk, j)), pl.BlockSpec((LANE, tn), lambda i, j, kk: (0, j))]
                 + [pl.BlockSpec(memory_space=pl.ANY)] * len(prev),
        out_specs=pl.BlockSpec((tm, tn), lambda i, j, kk: (i + t0, j)),
        out_shape=jax.ShapeDtypeStruct((lp, D), F32),
        input_output_aliases={4: 0} if prev else {},
        scratch_shapes=[pltpu.VMEM((tm, tn), F32)],
        compiler_params=_cp(("parallel", "parallel", "arbitrary")),
    )(dproj, dba, w_main_t, w_ba_t, *prev)


def _beta_g(ba, ab, row0, pad):
    lane = lax.broadcasted_iota(jnp.int32, ba.shape, 1)
    rows = row0 + lax.broadcasted_iota(jnp.int32, ba.shape, 0)
    z = ba + ab[1:2, :]
    sp = jnp.maximum(z, 0.0) + jnp.log(1.0 + jnp.exp(-jnp.abs(z)))
    val = jnp.where(lane < H, _sig(ba), -jnp.exp(ab[0:1, :]) * sp)
    return jnp.where((lane < 2 * H) & (rows >= pad), val, 0.0)


def _qkv_conv_fwd(proj, ba, conv_w, ab, pad):
    lp = proj.shape[0]
    te = _pick(lp, (320,))
    hb = te // HALO_Q

    def body(main_ref, halo_ref, cw_ref, ba_ref, ab_ref, out_ref, bg_ref, pre_scr, tap_scr):
        i, s = pl.program_id(0), pl.program_id(1)
        pre_scr[:HALO_Q, :] = jnp.where(i > 0, halo_ref[...], 0.0)
        pre_scr[HALO_Q:, :] = main_ref[...]
        scale = jnp.where(s == 0, DK ** -0.5, 1.0)
        off = HALO_Q - (KQ - 1)

        def head(h, carry):
            cs = pl.ds(pl.multiple_of(h * DK, DK), DK)
            for j in range(KQ - 1):
                tap_scr[j] = pre_scr[off + j:off + j + te, cs]
            co = cw_ref[KQ - 1:KQ, cs] * pre_scr[HALO_Q:, cs]
            for j in range(KQ - 1):
                co = co + cw_ref[j:j + 1, cs] * tap_scr[j]
            a = co * _sig(co)
            r = lax.rsqrt(_rowsum(a * a) + EPS)
            out_ref[:, cs] = jnp.where(s == 2, a, a * (r * scale))
            return carry

        lax.fori_loop(0, H, head, 0, unroll=True)

        @pl.when(s == 0)
        def _():
            bg_ref[...] = _beta_g(ba_ref[...], ab_ref[...], i * te, pad)

    return _call(
        body, name="qkv_conv_fwd", grid=(lp // te, 3),
        in_specs=[pl.BlockSpec((te, D), lambda i, s: (i, s)),
                  pl.BlockSpec((HALO_Q, D), lambda i, s: (jnp.maximum(i * hb - 1, 0), s)),
                  pl.BlockSpec((KQ, D), lambda i, s: (0, s)),
                  pl.BlockSpec((te, LANE), lambda i, s: (i, 0)),
                  pl.BlockSpec((2, LANE), lambda i, s: (0, 0))],
        out_specs=(pl.BlockSpec((te, D), lambda i, s: (i, s)), pl.BlockSpec((te, LANE), lambda i, s: (i, 0))),
        out_shape=(jax.ShapeDtypeStruct((lp, 3 * D), F32), jax.ShapeDtypeStruct((lp, LANE), F32)),
        scratch_shapes=[pltpu.VMEM((te + HALO_Q, D), F32), pltpu.VMEM((KQ - 1, te, DK), F32)],
        compiler_params=_cp(("parallel", "arbitrary")),
    )(proj, proj, conv_w, ba, ab)


def _tri_masks():
    row = lax.broadcasted_iota(jnp.int32, (C, C), 0)
    col = lax.broadcasted_iota(jnp.int32, (C, C), 1)
    return row, col


def _split(a):
    hi = a.astype(BF16)
    return hi, (a - hi.astype(F32)).astype(BF16)


def _dot3(a, b, dims=(((1,), (0,)), ((), ()))):
    (ah, al), (bh, bl) = a, b
    mm = lambda x, y: lax.dot_general(x, y, dims, preferred_element_type=F32)
    return mm(ah, bh) + (mm(ah, bl) + mm(al, bh))


CHUNKS_PER_STEP = 2
CHUNKS_PER_STEP_BWD = 1
TINV_BLOCK = 16


def _tinv(ns, row, col):
    eye = (row == col).astype(F32)
    sh = TINV_BLOCK.bit_length() - 1
    same16 = (row >> sh) == (col >> sh)
    same32 = (row >> (sh + 1)) == (col >> (sh + 1))
    ys = [jnp.where(same16, -n, 0.0) for n in ns]
    ts = [eye + y for y in ys]
    sp = [_split(y) for y in ys]
    for level in range(3):
        yks = [_dot3(s, s) for s in sp]
        sp = [_split(yk) for yk in yks]
        ts = [t + _dot3(s, _split(t)) for s, t in zip(sp, ts)]
    for mask in (same32 & ~same16, ~same32):
        tsp = [_split(t) for t in ts]
        inner = [_dot3(_split(jnp.where(mask, n, 0.0)), t) for n, t in zip(ns, tsp)]
        ts = [t - _dot3(tp, _split(a)) for t, tp, a in zip(ts, tsp, inner)]
    return ts


def _chunk_common(q, k, v, bcol, gcc, gcr, incl, strict):
    dm = jnp.where(incl, jnp.exp(gcc - gcr), 0.0)
    kk = _dot_nt(k, k)
    qk = _dot_nt(q, k)
    gccw = jnp.broadcast_to(gcc, (C, DK))
    egc = jnp.exp(gccw)
    glast = gccw[C - 1:C, :]
    eend = jnp.exp(glast - gccw)
    elast = jnp.exp(glast)
    rhs = jnp.concatenate([v * bcol, k * (bcol * egc)], axis=1)
    return dm, kk, qk, egc, eend, elast, rhs


def _delta_fwd(qkv, bg):
    lp = qkv.shape[0]
    nc = lp // C
    heads = range(H)
    sls = [slice(h * DK, (h + 1) * DK) for h in heads]

    def body(q_ref, k_ref, v_ref, bg_ref, o_ref, sall_ref, tall_ref, s_scr):
        @pl.when(pl.program_id(0) == 0)
        def _():
            s_scr[...] = jnp.zeros_like(s_scr)

        row, col = _tri_masks()
        incl, strict = row >= col, row > col

        def prepare(sub):
            rs = slice(sub * C, (sub + 1) * C)
            bgt = bg_ref[rs, :]
            gc_all = _dot_hi(incl.astype(F32), bgt)
            gc_t = _dot_hi(bgt.T, (row <= col).astype(F32))
            qs, ks, vs = ([r[rs, sl] for sl in sls] for r in (q_ref, k_ref, v_ref))
            bcols = [jnp.broadcast_to(bgt[:, h:h + 1], (C, DK)) for h in heads]
            cm = [_chunk_common(qs[h], ks[h], vs[h], bcols[h], gc_all[:, H + h:H + h + 1], gc_t[H + h:H + h + 1, :],
                                incl, strict) for h in heads]
            dms, kks, qks, egcs, eends, elasts, rhss = zip(*cm)
            ts = _tinv([jnp.where(strict, bcols[h][:, :C] * kks[h] * dms[h], 0.0) for h in heads], row, col)
            sols = [_dot3(_split(ts[h]), _split(rhss[h])) for h in heads]
            qgs = [(qs[h] * egcs[h]).astype(BF16) for h in heads]
            ps = [(qks[h] * dms[h]).astype(BF16) for h in heads]
            kends = [(ks[h] * eends[h]).astype(BF16) for h in heads]
            return ts, sols, qgs, ps, kends, elasts

        prepared = [prepare(sub) for sub in range(CHUNKS_PER_STEP)]
        ss = [s_scr[h] for h in heads]
        for sub in range(CHUNKS_PER_STEP):
            rs = slice(sub * C, (sub + 1) * C)
            ts, sols, qgs, ps, kends, elasts = prepared[sub]
            sb = [s.astype(BF16) for s in ss]
            wvb = [(sols[h][:, :DK] - _dot(sols[h][:, DK:], sb[h])).astype(BF16) for h in heads]
            for h in heads:
                o_ref[rs, sls[h]] = _dot(qgs[h], sb[h]) + _dot(ps[h], wvb[h])
                sall_ref[sub, h] = ss[h]
                tall_ref[sub, h] = ts[h]
            ss = [ss[h] * elasts[h] + _dot_tn(kends[h], wvb[h]) for h in heads]
        for h in heads:
            s_scr[h] = ss[h]

    rows = CHUNKS_PER_STEP * C
    blk = lambda j: pl.BlockSpec((rows, D), lambda n: (n, j))
    return _call(
        body, name="delta_fwd", grid=(nc // CHUNKS_PER_STEP,),
        in_specs=[blk(0), blk(1), blk(2), pl.BlockSpec((rows, LANE), lambda n: (n, 0))],
        out_specs=(pl.BlockSpec((rows, D), lambda n: (n, 0)),
                   pl.BlockSpec((CHUNKS_PER_STEP, H, DK, DK), lambda n: (n, 0, 0, 0)),
                   pl.BlockSpec((CHUNKS_PER_STEP, H, C, C), lambda n: (n, 0, 0, 0))),
        out_shape=(jax.ShapeDtypeStruct((lp, D), F32), jax.ShapeDtypeStruct((nc, H, DK, DK), F32),
                   jax.ShapeDtypeStruct((nc, H, C, C), F32)),
        scratch_shapes=[pltpu.VMEM((H, DK, DK), F32)],
        compiler_params=_cp(("arbitrary",)),
    )(qkv, qkv, qkv, bg)


def _delta_bwd(qkv, bg, sall, tall, do):
    lp = qkv.shape[0]
    nc = lp // C

    heads = range(H)
    sls = [slice(h * DK, (h + 1) * DK) for h in heads]

    def body(q_ref, k_ref, v_ref, bg_ref, sall_ref, tall_ref, do_ref, dqkv_ref, dbg_ref, ds_scr):
        @pl.when(pl.program_id(0) == 0)
        def _():
            ds_scr[...] = jnp.zeros_like(ds_scr)

        dsns = [ds_scr[h] for h in heads]
        for sub in reversed(range(CHUNKS_PER_STEP_BWD)):
            dsns = chunk(sub, dsns, q_ref, k_ref, v_ref, bg_ref, sall_ref, tall_ref, do_ref, dqkv_ref, dbg_ref)
        for h in heads:
            ds_scr[h] = dsns[h]

    def chunk(sub, dsns, q_ref, k_ref, v_ref, bg_ref, sall_ref, tall_ref, do_ref, dqkv_ref, dbg_ref):
        rs = slice(sub * C, (sub + 1) * C)
        bgt = bg_ref[rs, :]
        row, col = _tri_masks()
        incl, strict = row >= col, row > col
        upper = (row <= col).astype(F32)
        gc_all = _dot_hi(incl.astype(F32), bgt)
        gc_t = _dot_hi(bgt.T, upper)
        lane = lax.broadcasted_iota(jnp.int32, (C, LANE), 1)
        lastrow = lax.broadcasted_iota(jnp.int32, (C, 1), 0) == C - 1
        qs, ks, vs, dos = ([r[rs, sl] for sl in sls] for r in (q_ref, k_ref, v_ref, do_ref))
        bcols = [jnp.broadcast_to(bgt[:, h:h + 1], (C, DK)) for h in heads]
        cm = [_chunk_common(qs[h], ks[h], vs[h], bcols[h], gc_all[:, H + h:H + h + 1], gc_t[H + h:H + h + 1, :],
                            incl, strict) for h in heads]
        dms, kks, qks, egcs, eends, elasts, rhss = zip(*cm)
        ss = [sall_ref[sub, h] for h in heads]
        ts = [tall_ref[sub, h] for h in heads]
        sb = [s.astype(BF16) for s in ss]
        dsb = [d.astype(BF16) for d in dsns]
        dob = [d.astype(BF16) for d in dos]
        sols = [_dot3(_split(ts[h]), _split(rhss[h])) for h in heads]
        ws = [sol[:, DK:] for sol in sols]
        qgs = [qs[h] * egcs[h] for h in heads]
        kends = [ks[h] * eends[h] for h in heads]
        wvs = [sols[h][:, :DK] - _dot(ws[h], sb[h]) for h in heads]
        wvb = [wv.astype(BF16) for wv in wvs]
        dwvs = [_dot_tn(qks[h] * dms[h], dob[h]) + _dot(kends[h], dsb[h]) for h in heads]
        dps = [jnp.where(incl, _dot_nt(dob[h], wvb[h]), 0.0) for h in heads]
        dqgs = [_dot_nt(dob[h], sb[h]) for h in heads]
        dkends = [_dot_nt(wvb[h], dsb[h]) for h in heads]
        ds_before = [_dot_tn(qgs[h], dob[h]) + elasts[h] * dsns[h] - _dot_tn(ws[h], dwvs[h]) for h in heads]
        dglasts = [elasts[h] * jnp.sum(ss[h] * dsns[h], keepdims=True) for h in heads]
        dws = [-_dot_nt(dwvs[h], sb[h]) for h in heads]
        tts = [_split(ts[h].T) for h in heads]
        drhss = [_dot3(tts[h], _split(jnp.concatenate([dwvs[h], dws[h]], axis=1))) for h in heads]
        nt_dims = (((1,), (1,)), ((), ()))
        dns = [jnp.where(strict, -_dot3(_split(drhss[h]), _split(sols[h]), nt_dims), 0.0) for h in heads]
        dbeta_t = jnp.zeros((C, LANE), F32)
        dgc_t = jnp.zeros((C, LANE), F32)
        for h in heads:
            q, k, v, bcol, dm, kk, qk, egc, eend = qs[h], ks[h], vs[h], bcols[h], dms[h], kks[h], qks[h], egcs[h], eends[h]
            drv, drk = drhss[h][:, :DK], drhss[h][:, DK:]
            dn, dp, dqg, dkend = dns[h], dps[h], dqgs[h], dkends[h]
            rk = _rowsum(drk * k)
            dkk = dn * (bcol[:, :C] * dm)
            dqk = dp * dm
            e = (dn * (bcol[:, :C] * kk) + dp * qk) * dm
            tk = _rowsum(dkend * kends[h])
            dgc = rk * bcol * egc + _rowsum(e) - _rowsum(e.T) + _rowsum(dqg * qgs[h]) - tk
            dgc = dgc + jnp.where(lastrow, dglasts[h] + jnp.sum(tk, keepdims=True), 0.0)
            dbeta = _rowsum(drv * v) + rk * egc + _rowsum(dn * kk * dm)
            dqkv_ref[rs, sls[h]] = _dot(dqk, k) + dqg * egc
            dqkv_ref[rs, D + h * DK:D + (h + 1) * DK] = (drk * (bcol * egc) + _dot(dkk, k) + _dot_tn(dkk, k)
                                                        + _dot_tn(dqk, q) + dkend * eend)
            dqkv_ref[rs, 2 * D + h * DK:2 * D + (h + 1) * DK] = bcol * drv
            dbeta_t = jnp.where(lane == h, dbeta, dbeta_t)
            dgc_t = jnp.where(lane == H + h, dgc, dgc_t)
        dbg_ref[rs, :] = dbeta_t + _dot_hi(upper, dgc_t)
        return ds_before

    steps = nc // CHUNKS_PER_STEP_BWD
    rows = CHUNKS_PER_STEP_BWD * C
    rev = lambda n: steps - 1 - n
    blk = lambda j: pl.BlockSpec((rows, D), lambda n: (rev(n), j))
    return _call(
        body, name="delta_bwd", grid=(steps,),
        in_specs=[blk(0), blk(1), blk(2), pl.BlockSpec((rows, LANE), lambda n: (rev(n), 0)),
                  pl.BlockSpec((CHUNKS_PER_STEP_BWD, H, DK, DK), lambda n: (rev(n), 0, 0, 0)),
                  pl.BlockSpec((CHUNKS_PER_STEP_BWD, H, C, C), lambda n: (rev(n), 0, 0, 0)),
                  pl.BlockSpec((rows, D), lambda n: (rev(n), 0))],
        out_specs=(pl.BlockSpec((rows, 3 * D), lambda n: (rev(n), 0)),
                   pl.BlockSpec((rows, LANE), lambda n: (rev(n), 0))),
        out_shape=(jax.ShapeDtypeStruct((lp, 3 * D), F32), jax.ShapeDtypeStruct((lp, LANE), F32)),
        scratch_shapes=[pltpu.VMEM((H, DK, DK), F32)],
        compiler_params=_cp(("arbitrary",)),
    )(qkv, qkv, qkv, bg, sall, tall, do)


def _o_post_fwd(o, proj, dn_w, w_dn):
    lp = o.shape[0]
    te = _pick(lp, (640, 320))

    def body(o_ref, za_ref, w_ref, wdn_ref, out_ref, ya_ref):
        za = za_ref[...]
        gate = za * _sig(za)
        for h in range(H):
            sl = slice(h * DK, (h + 1) * DK)
            oh = o_ref[:, sl]
            r = lax.rsqrt(jnp.mean(oh * oh, axis=-1, keepdims=True) + EPS)
            out_ref[:, sl] = (oh * r * w_ref[...] * gate[:, sl]).astype(BF16)
        ya_ref[...] = _dot(out_ref[...], wdn_ref[...])

    row = pl.BlockSpec((te, D), lambda i: (i, 0))
    return _call(
        body, name="o_post_fwd", grid=(lp // te,),
        in_specs=[row, pl.BlockSpec((te, D), lambda i: (i, CB_ZA)), pl.BlockSpec((1, DK), lambda i: (0, 0)),
                  pl.BlockSpec((D, D), lambda i: (0, 0))],
        out_specs=(row, row),
        out_shape=(jax.ShapeDtypeStruct((lp, D), BF16), jax.ShapeDtypeStruct((lp, D), F32)),
        compiler_params=_cp(("parallel",)),
    )(o, proj, dn_w, w_dn)


def _o_post_bwd(dy_a, w_dn, o, proj, dn_w, dproj):
    lp = o.shape[0]
    te = _pick(lp, (320,))

    def body(dya_ref, wdn_ref, o_ref, za_ref, w_ref, _, do_ref, dza_ref, dw_ref, don_ref):
        @pl.when(pl.program_id(0) == 0)
        def _():
            dw_ref[...] = jnp.zeros_like(dw_ref)

        don_ref[...] = _dot_nt(dya_ref[...], wdn_ref[...])
        za = za_ref[...]
        sz = _sig(za)
        gate, dgate = za * sz, _dsilu(za, sz)
        w = w_ref[...]
        dw = jnp.zeros((1, DK), F32)
        for h in range(H):
            sl = slice(h * DK, (h + 1) * DK)
            oh, g = o_ref[:, sl], don_ref[:, sl]
            r = lax.rsqrt(jnp.mean(oh * oh, axis=-1, keepdims=True) + EPS)
            ohat = oh * r
            dza_ref[:, sl] = (g * ohat * w * dgate[:, sl]).astype(BF16)
            don = g * gate[:, sl]
            dw = dw + _colsum(don * ohat)
            dohat = don * w
            do_ref[:, sl] = r * (dohat - ohat * jnp.mean(dohat * ohat, axis=-1, keepdims=True))
        dw_ref[...] += dw

    return _call(
        body, name="o_post_bwd", grid=(lp // te,),
        in_specs=[pl.BlockSpec((te, D), lambda i: (i, 0)), pl.BlockSpec((D, D), lambda i: (0, 0)),
                  pl.BlockSpec((te, D), lambda i: (i, 0)),
                  pl.BlockSpec((te, D), lambda i: (i, CB_ZA)), pl.BlockSpec((1, DK), lambda i: (0, 0)),
                  pl.BlockSpec(memory_space=pl.ANY)],
        out_specs=(pl.BlockSpec((te, D), lambda i: (i, 0)), pl.BlockSpec((te, D), lambda i: (i, CB_ZA)),
                   pl.BlockSpec((1, DK), lambda i: (0, 0))),
        out_shape=(jax.ShapeDtypeStruct((lp, D), F32), jax.ShapeDtypeStruct(dproj.shape, dproj.dtype),
                   jax.ShapeDtypeStruct((1, DK), F32)),
        input_output_aliases={5: 1},
        scratch_shapes=[pltpu.VMEM((te, D), F32)],
        compiler_params=_cp(("arbitrary",)),
    )(dy_a, w_dn, o, proj, dn_w, dproj)


def _qkv_conv_bwd(proj, dqkv, conv_w, dproj):
    lp = proj.shape[0]
    te = _pick(lp, (320,))
    hb = te // HALO_Q
    nt = lp // te
    last_hb = lp // HALO_Q - 1

    def body(main_ref, prev_ref, next_ref, dmain_ref, dnext_ref, cw_ref, _, dpre_ref, dcw_ref, pre_scr, dn_scr,
             tap_scr, dco_scr, dsh_scr):
        s, i = pl.program_id(0), pl.program_id(1)

        @pl.when(i == 0)
        def _():
            dcw_ref[...] = jnp.zeros_like(dcw_ref)

        ne = te + HALO_Q
        pre_scr[:HALO_Q, :] = jnp.where(i > 0, prev_ref[...], 0.0)
        pre_scr[HALO_Q:ne, :] = main_ref[...]
        pre_scr[ne:, :] = jnp.where(i < nt - 1, next_ref[...], 0.0)
        dn_scr[:te, :] = dmain_ref[...]
        dn_scr[te:, :] = jnp.where(i < nt - 1, dnext_ref[...], 0.0)
        scale = jnp.where(s == 0, DK ** -0.5, 1.0)
        off = HALO_Q - (KQ - 1)

        def head(h, carry):
            cs = pl.ds(pl.multiple_of(h * DK, DK), DK)
            for j in range(KQ - 1):
                tap_scr[j] = pre_scr[off + j:off + j + ne, cs]
            taps = [tap_scr[j] for j in range(KQ - 1)] + [pre_scr[HALO_Q:, cs]]
            co = cw_ref[0:1, cs] * taps[0]
            for j in range(1, KQ):
                co = co + cw_ref[j:j + 1, cs] * taps[j]
            sg = _sig(co)
            a = co * sg
            g = dn_scr[:, cs]
            r = lax.rsqrt(_rowsum(a * a) + EPS)
            yhat = a * r
            da = jnp.where(s == 2, g, (scale * r) * (g - yhat * _rowsum(g * yhat)))
            dco = da * _dsilu(co, sg)
            dco_scr[...] = dco
            for j in range(KQ - 1):
                dsh_scr[j] = dco_scr[KQ - 1 - j:KQ - 1 - j + te, :]
            dpre = cw_ref[KQ - 1:KQ, cs] * dco[:te, :]
            for j in range(KQ - 1):
                dpre = dpre + cw_ref[j:j + 1, cs] * dsh_scr[j]
            dpre_ref[:, cs] = dpre.astype(BF16)
            dcw_ref[:, cs] += jnp.concatenate([_colsum(dco[:te] * taps[j][:te]) for j in range(KQ)], axis=0)
            return carry

        lax.fori_loop(0, H, head, 0, unroll=True)

    return _call(
        body, name="qkv_conv_bwd", grid=(3, nt),
        in_specs=[pl.BlockSpec((te, D), lambda s, i: (i, s)),
                  pl.BlockSpec((HALO_Q, D), lambda s, i: (jnp.maximum(i * hb - 1, 0), s)),
                  pl.BlockSpec((HALO_Q, D), lambda s, i: (jnp.minimum((i + 1) * hb, last_hb), s)),
                  pl.BlockSpec((te, D), lambda s, i: (i, s)),
                  pl.BlockSpec((HALO_Q, D), lambda s, i: (jnp.minimum((i + 1) * hb, last_hb), s)),
                  pl.BlockSpec((KQ, D), lambda s, i: (0, s)),
                  pl.BlockSpec(memory_space=pl.ANY)],
        out_specs=(pl.BlockSpec((te, D), lambda s, i: (i, s)), pl.BlockSpec((KQ, D), lambda s, i: (0, s))),
        out_shape=(jax.ShapeDtypeStruct(dproj.shape, dproj.dtype), jax.ShapeDtypeStruct((KQ, 3 * D), F32)),
        input_output_aliases={6: 0},
        scratch_shapes=[pltpu.VMEM((te + 2 * HALO_Q, D), F32), pltpu.VMEM((te + HALO_Q, D), F32),
                        pltpu.VMEM((KQ - 1, te + HALO_Q, DK), F32), pltpu.VMEM((te + HALO_Q, DK), F32),
                        pltpu.VMEM((KQ - 1, te, DK), F32)],
        compiler_params=_cp(("arbitrary", "arbitrary")),
    )(proj, proj, proj, dqkv, dqkv, conv_w, dproj)


def _ba_bwd(dbg, ba, ab, pad):
    lp = ba.shape[0]
    te = _pick(lp, (640, 320))

    def body(dbg_ref, ba_ref, ab_ref, dba_ref, dab_ref):
        i = pl.program_id(0)

        @pl.when(i == 0)
        def _():
            dab_ref[...] = jnp.zeros_like(dab_ref)

        ba, ab = ba_ref[...], ab_ref[...]
        lane = lax.broadcasted_iota(jnp.int32, ba.shape, 1)
        rows = i * te + lax.broadcasted_iota(jnp.int32, ba.shape, 0)
        g = jnp.where((lane < 2 * H) & (rows >= pad), dbg_ref[...], 0.0)
        sb = _sig(ba)
        z = ba + ab[1:2, :]
        sp = jnp.maximum(z, 0.0) + jnp.log(1.0 + jnp.exp(-jnp.abs(z)))
        nea = -jnp.exp(ab[0:1, :])
        dz = g * nea * _sig(z)
        dba_ref[...] = jnp.where(lane < H, g * sb * (1.0 - sb), dz).astype(BF16)
        is_g = (lane >= H) & (lane < 2 * H)
        dab_ref[...] += jnp.concatenate([_colsum(jnp.where(is_g, g * nea * sp, 0.0)),
                                         _colsum(jnp.where(is_g, dz, 0.0))], axis=0)

    return _call(
        body, name="ba_bwd", grid=(lp // te,),
        in_specs=[pl.BlockSpec((te, LANE), lambda i: (i, 0)), pl.BlockSpec((te, LANE), lambda i: (i, 0)),
                  pl.BlockSpec((2, LANE), lambda i: (0, 0))],
        out_specs=(pl.BlockSpec((te, LANE), lambda i: (i, 0)), pl.BlockSpec((2, LANE), lambda i: (0, 0))),
        out_shape=(jax.ShapeDtypeStruct((lp, LANE), BF16), jax.ShapeDtypeStruct((2, LANE), F32)),
        compiler_params=_cp(("arbitrary",)),
    )(dbg, ba, ab)


SUBLANES = 8
CONV_RB = 64


def _fill_shifted(sh_scr, src_scr, cs):
    n = sh_scr.shape[1]
    for s in range(1, SUBLANES):
        sh_scr[s] = src_scr[s:s + n, cs]


def _shifted(sh_scr, src_scr, cs, r, r0, n):
    s, a8 = r % SUBLANES, r - r % SUBLANES
    if s == 0:
        return src_scr[r0 + a8:r0 + a8 + n, cs]
    return sh_scr[s, r0 + a8:r0 + a8 + n, :]


def _conv_b_fwd(proj, dw_w, dw_b, ln_w, ln_b, w_cf):
    lp = proj.shape[0]
    te = _pick(lp, (320,))
    hb = te // HALO_D

    def body(a_ref, b_ref, ha_ref, hb_ref, zb_ref, w_ref, wb_ref, lw_ref, lb_ref, wcf_ref, c1_ref, c3_ref, yb_ref,
             c0_scr, sh_scr):
        i = pl.program_id(0)
        c0_scr[:HALO_D, :] = jnp.where(i > 0, ha_ref[...] * _sig(hb_ref[...]), 0.0)
        c0_scr[HALO_D:, :] = a_ref[...] * _sig(b_ref[...])
        off = HALO_D - (KD - 1)
        def lane_block(cb, carry):
            cs = pl.ds(pl.multiple_of(cb * LANE, LANE), LANE)
            _fill_shifted(sh_scr, c0_scr, cs)
            for r0 in range(0, te, CONV_RB):
                acc = None
                for j in range(KD):
                    term = w_ref[j:j + 1, cs] * _shifted(sh_scr, c0_scr, cs, off + j, r0, CONV_RB)
                    acc = term if acc is None else acc + term
                c1_ref[r0:r0 + CONV_RB, cs] = acc + wb_ref[:, cs]
            return carry

        lax.fori_loop(0, D // LANE, lane_block, 0)
        c1 = c1_ref[...]
        mu = jnp.mean(c1, axis=-1, keepdims=True)
        xc = c1 - mu
        c2 = xc * lax.rsqrt(jnp.mean(xc * xc, axis=-1, keepdims=True) + EPS) * lw_ref[...] + lb_ref[...]
        zb = zb_ref[...]
        c3 = (c2 * _sig(c2) * zb * _sig(zb)).astype(BF16)
        c3_ref[...] = c3
        yb_ref[...] = _dot(c3, wcf_ref[...])

    vec = pl.BlockSpec((1, D), lambda i: (0, 0))
    row = pl.BlockSpec((te, D), lambda i: (i, 0))
    return _call(
        body, name="conv_b_fwd", grid=(lp // te,),
        in_specs=[pl.BlockSpec((te, D), lambda i: (i, CB_GA_)), pl.BlockSpec((te, D), lambda i: (i, CB_GB_)),
                  pl.BlockSpec((HALO_D, D), lambda i: (jnp.maximum(i * hb - 1, 0), CB_GA_)),
                  pl.BlockSpec((HALO_D, D), lambda i: (jnp.maximum(i * hb - 1, 0), CB_GB_)),
                  pl.BlockSpec((te, D), lambda i: (i, CB_ZB)),
                  pl.BlockSpec((KD, D), lambda i: (0, 0)), vec, vec, vec, pl.BlockSpec((D, D), lambda i: (0, 0))],
        out_specs=(row, row, row),
        out_shape=(jax.ShapeDtypeStruct((lp, D), F32), jax.ShapeDtypeStruct((lp, D), BF16),
                   jax.ShapeDtypeStruct((lp, D), F32)),
        scratch_shapes=[pltpu.VMEM((te + HALO_D, D), F32), pltpu.VMEM((SUBLANES, te + HALO_D - SUBLANES, LANE), F32)],
        compiler_params=_cp(("parallel",)),
    )(proj, proj, proj, proj, proj, dw_w, dw_b, ln_w, ln_b, w_cf)


def _conv_b_bwd1(dy_b, w_cf, c1, proj, ln_w, ln_b, dproj):
    lp = c1.shape[0]
    te = _pick(lp, (320,))

    def body(dyb_ref, wcf_ref, c1_ref, zb_ref, lw_ref, lb_ref, _, dc1_ref, dzb_ref, sums_ref):
        @pl.when(pl.program_id(0) == 0)
        def _():
            sums_ref[...] = jnp.zeros_like(sums_ref)

        c1, g = c1_ref[...], _dot_nt(dyb_ref[...], wcf_ref[...])
        mu = jnp.mean(c1, axis=-1, keepdims=True)
        xc = c1 - mu
        rstd = lax.rsqrt(jnp.mean(xc * xc, axis=-1, keepdims=True) + EPS)
        xh = xc * rstd
        lw = lw_ref[...]
        c2 = xh * lw + lb_ref[...]
        s2 = _sig(c2)
        zb = zb_ref[...]
        sz = _sig(zb)
        dc2 = g * (zb * sz) * _dsilu(c2, s2)
        dzb_ref[...] = (g * (c2 * s2) * _dsilu(zb, sz)).astype(BF16)
        dxh = dc2 * lw
        dc1 = rstd * (dxh - jnp.mean(dxh, axis=-1, keepdims=True) - xh * jnp.mean(dxh * xh, axis=-1, keepdims=True))
        dc1_ref[...] = dc1
        sums_ref[...] += jnp.concatenate([_colsum(dc2 * xh), _colsum(dc2), _colsum(dc1)], axis=0)

    vec = pl.BlockSpec((1, D), lambda i: (0, 0))
    return _call(
        body, name="conv_b_bwd1", grid=(lp // te,),
        in_specs=[pl.BlockSpec((te, D), lambda i: (i, 0)), pl.BlockSpec((D, D), lambda i: (0, 0)),
                  pl.BlockSpec((te, D), lambda i: (i, 0)),
                  pl.BlockSpec((te, D), lambda i: (i, CB_ZB)), vec, vec, pl.BlockSpec(memory_space=pl.ANY)],
        out_specs=(pl.BlockSpec((te, D), lambda i: (i, 0)), pl.BlockSpec((te, D), lambda i: (i, CB_ZB)),
                   pl.BlockSpec((3, D), lambda i: (0, 0))),
        out_shape=(jax.ShapeDtypeStruct((lp, D), F32), jax.ShapeDtypeStruct(dproj.shape, dproj.dtype),
                   jax.ShapeDtypeStruct((3, D), F32)),
        input_output_aliases={6: 1},
        compiler_params=_cp(("arbitrary",)),
    )(dy_b, w_cf, c1, proj, ln_w, ln_b, dproj)


def _conv_b_bwd2(dc1, proj, dw_w, dproj):
    lp = dc1.shape[0]
    te = _pick(lp, (320,))
    hb = te // HALO_D
    nt = lp // te
    last_hb = lp // HALO_D - 1

    def body(g_ref, gn_ref, a_ref, b_ref, ha_ref, hb_ref, w_ref, _, dab_ref, dw_ref, c0_scr, g_scr, dc0_scr,
             csh_scr, gsh_scr):
        i = pl.program_id(0)

        @pl.when(i == 0)
        def _():
            dw_ref[...] = jnp.zeros_like(dw_ref)

        a, b = a_ref[...], b_ref[...]
        sb = _sig(b)
        c0_scr[:HALO_D, :] = jnp.where(i > 0, ha_ref[...] * _sig(hb_ref[...]), 0.0)
        c0_scr[HALO_D:, :] = a * sb
        g_scr[:te, :] = g_ref[...]
        g_scr[te:, :] = jnp.where(i < nt - 1, gn_ref[...], 0.0)
        off = HALO_D - (KD - 1)
        def lane_block(cb, carry):
            cs = pl.ds(pl.multiple_of(cb * LANE, LANE), LANE)
            _fill_shifted(csh_scr, c0_scr, cs)
            _fill_shifted(gsh_scr, g_scr, cs)
            for r0 in range(0, te, CONV_RB):
                acc = None
                for j in range(KD):
                    term = w_ref[j:j + 1, cs] * _shifted(gsh_scr, g_scr, cs, KD - 1 - j, r0, CONV_RB)
                    acc = term if acc is None else acc + term
                dc0_scr[r0:r0 + CONV_RB, cs] = acc
            parts = [None] * KD
            for r0 in range(0, te, CONV_RB):
                g = g_scr[r0:r0 + CONV_RB, cs].reshape(CONV_RB // SUBLANES, SUBLANES, LANE)
                for j in range(KD):
                    x = _shifted(csh_scr, c0_scr, cs, off + j, r0, CONV_RB)
                    p = jnp.sum(g * x.reshape(CONV_RB // SUBLANES, SUBLANES, LANE), axis=0)
                    parts[j] = p if parts[j] is None else parts[j] + p
            dw_ref[:, cs] += jnp.concatenate([_colsum(p) for p in parts], axis=0)
            return carry

        lax.fori_loop(0, D // LANE, lane_block, 0)
        dc0 = dc0_scr[...]
        dab_ref[:, :D] = (dc0 * sb).astype(BF16)
        dab_ref[:, D:] = (dc0 * a * sb * (1.0 - sb)).astype(BF16)

    return _call(
        body, name="conv_b_bwd2", grid=(nt,),
        in_specs=[pl.BlockSpec((te, D), lambda i: (i, 0)),
                  pl.BlockSpec((HALO_D, D), lambda i: (jnp.minimum((i + 1) * hb, last_hb), 0)),
                  pl.BlockSpec((te, D), lambda i: (i, CB_GA_)), pl.BlockSpec((te, D), lambda i: (i, CB_GB_)),
                  pl.BlockSpec((HALO_D, D), lambda i: (jnp.maximum(i * hb - 1, 0), CB_GA_)),
                  pl.BlockSpec((HALO_D, D), lambda i: (jnp.maximum(i * hb - 1, 0), CB_GB_)),
                  pl.BlockSpec((KD, D), lambda i: (0, 0)), pl.BlockSpec(memory_space=pl.ANY)],
        out_specs=(pl.BlockSpec((te, 2 * D), lambda i: (i, CB_GA_ // 2)), pl.BlockSpec((KD, D), lambda i: (0, 0))),
        out_shape=(jax.ShapeDtypeStruct(dproj.shape, dproj.dtype), jax.ShapeDtypeStruct((KD, D), F32)),
        input_output_aliases={7: 0},
        scratch_shapes=[pltpu.VMEM((te + HALO_D, D), F32), pltpu.VMEM((te + HALO_D, D), F32), pltpu.VMEM((te, D), F32),
                        pltpu.VMEM((SUBLANES, te + HALO_D - SUBLANES, LANE), F32),
                        pltpu.VMEM((SUBLANES, te + HALO_D - SUBLANES, LANE), F32)],
        compiler_params=_cp(("arbitrary",)),
    )(dc1, dc1, proj, proj, proj, proj, dw_w, dproj)


def _merge_fwd(y_a, y_b, proj, b_cf, w_o):
    lp = y_a.shape[0]
    te = _pick(lp, (320,))

    def body(ya_ref, yb_ref, ga_ref, gb_ref, bias_ref, wo_ref, out_ref, z_ref):
        merged = (_sig(ga_ref[...]) * ya_ref[...] + _sig(gb_ref[...]) * (yb_ref[...] + bias_ref[...])).astype(BF16)
        out_ref[...] = merged
        z_ref[...] = _dot(merged, wo_ref[...])

    row = lambda j: pl.BlockSpec((te, D), lambda i: (i, j))
    return _call(
        body, name="merge_fwd", grid=(lp // te,),
        in_specs=[row(0), row(0), row(CB_MA), row(CB_MB), pl.BlockSpec((1, D), lambda i: (0, 0)),
                  pl.BlockSpec((D, D), lambda i: (0, 0))],
        out_specs=(row(0), row(0)),
        out_shape=(jax.ShapeDtypeStruct((lp, D), BF16), jax.ShapeDtypeStruct((lp, D), F32)),
        compiler_params=_cp(("parallel",)),
    )(y_a, y_b, proj, proj, b_cf, w_o)


def _merge_bwd(dx_out_b, w_o, y_a, y_b, proj, b_cf):
    lp = y_a.shape[0]
    te = _pick(lp, (320,))

    def body(dx_ref, wo_ref, ya_ref, yb_ref, ga_ref, gb_ref, bias_ref, dya_ref, dyb_ref, dg_ref, db_ref):
        @pl.when(pl.program_id(0) == 0)
        def _():
            db_ref[...] = jnp.zeros_like(db_ref)

        dm = _dot_nt(dx_ref[...], wo_ref[...])
        sa, sb = _sig(ga_ref[...]), _sig(gb_ref[...])
        dyb = sb * dm
        dya_ref[...] = (sa * dm).astype(BF16)
        dyb_ref[...] = dyb.astype(BF16)
        dg_ref[:, :D] = (dm * ya_ref[...] * sa * (1.0 - sa)).astype(BF16)
        dg_ref[:, D:] = (dm * (yb_ref[...] + bias_ref[...]) * sb * (1.0 - sb)).astype(BF16)
        db_ref[...] += _colsum(dyb)

    row = lambda j: pl.BlockSpec((te, D), lambda i: (i, j))
    act = jax.ShapeDtypeStruct((lp, D), BF16)
    return _call(
        body, name="merge_bwd", grid=(lp // te,),
        in_specs=[row(0), pl.BlockSpec((D, D), lambda i: (0, 0)), row(0), row(0), row(CB_MA), row(CB_MB),
                  pl.BlockSpec((1, D), lambda i: (0, 0))],
        out_specs=(row(0), row(0), pl.BlockSpec((te, 2 * D), lambda i: (i, CB_MA // 2)),
                   pl.BlockSpec((1, D), lambda i: (0, 0))),
        out_shape=(act, act, jax.ShapeDtypeStruct((lp, NCB * D), BF16), jax.ShapeDtypeStruct((1, D), F32)),
        compiler_params=_cp(("arbitrary",)),
    )(dx_out_b, w_o, y_a, y_b, proj, proj, b_cf)


def _final_fwd_bwd(x_ext, z, target, final_w):
    lp = x_ext.shape[0]
    te = _pick(lp, (640,))
    nsub = te // LANE

    def body(x_ref, z_ref, *rest):
        t_refs, (w_ref, dx_ref, dxb_ref, loss_ref, dw_ref) = rest[:nsub], rest[nsub:]
        i = pl.program_id(0)

        @pl.when(i == 0)
        def _():
            loss_ref[...] = jnp.zeros_like(loss_ref)
            dw_ref[...] = jnp.zeros_like(dw_ref)

        w = w_ref[...]
        for k in range(nsub):
            rs = slice(k * LANE, (k + 1) * LANE)
            xo = x_ref[rs, :] + z_ref[rs, :]
            r = lax.rsqrt(jnp.mean(xo * xo, axis=-1, keepdims=True) + EPS)
            xhat = xo * r
            err = xhat * w - t_refs[k][...]
            if k == 0:
                err = jnp.where(i > 0, err, 0.0)
            loss_ref[...] += 0.5 * jnp.sum(jnp.mean(err * err, axis=-1, keepdims=True), keepdims=True)
            dy = err * (1.0 / D)
            dw_ref[...] += _colsum(dy * xhat)
            dxn = dy * w
            dx = r * (dxn - xhat * jnp.mean(dxn * xhat, axis=-1, keepdims=True))
            dx_ref[rs, :] = dx
            dxb_ref[rs, :] = dx.astype(BF16)

    piece = lambda k: pl.BlockSpec((LANE, D), lambda i: (jnp.maximum(i * nsub + k - 1, 0), 0))
    row = pl.BlockSpec((te, D), lambda i: (i, 0))
    return _call(
        body, name="final_fwd_bwd", grid=(lp // te,),
        in_specs=[row, row] + [piece(k) for k in range(nsub)] + [pl.BlockSpec((1, D), lambda i: (0, 0))],
        out_specs=(row, row, pl.BlockSpec((1, 1), lambda i: (0, 0)), pl.BlockSpec((1, D), lambda i: (0, 0))),
        out_shape=(jax.ShapeDtypeStruct((lp, D), F32), jax.ShapeDtypeStruct((lp, D), BF16),
                   jax.ShapeDtypeStruct((1, 1), F32), jax.ShapeDtypeStruct((1, D), F32)),
        compiler_params=_cp(("arbitrary",)),
    )(x_ext, z, *([target] * nsub), final_w)


def _prenorm_bwd(dh, x_ext, dx_out, norm_w, seq):
    lp = x_ext.shape[0]
    te = _pick(lp, (640,))
    nt = lp // te
    head = lp - seq

    def body(dh_ref, x_ref, dxo_ref, w_ref, gx_ref, head_ref, dw_ref, stage, sems):
        i = pl.program_id(0)
        slot = i % 2

        def first_copy():
            return pltpu.make_async_copy(stage.at[0, pl.ds(head, te - head)], gx_ref.at[pl.ds(0, te - head)], sems.at[0])

        def tile_copy(step, s):
            return pltpu.make_async_copy(stage.at[s], gx_ref.at[pl.ds(pl.multiple_of(step * te - head, LANE), te)],
                                         sems.at[s])

        @pl.when(i == 0)
        def _():
            dw_ref[...] = jnp.zeros_like(dw_ref)

        @pl.when(i == 2)
        def _():
            first_copy().wait()

        @pl.when(i > 2)
        def _():
            tile_copy(i - 2, slot).wait()

        x, dh = x_ref[...], dh_ref[...]
        r = lax.rsqrt(jnp.mean(x * x, axis=-1, keepdims=True) + EPS)
        xhat = x * r
        dxn = dh * w_ref[...]
        stage[slot] = dxo_ref[...] + r * (dxn - xhat * jnp.mean(dxn * xhat, axis=-1, keepdims=True))
        dw_ref[...] += _colsum(dh * xhat)

        @pl.when(i == 0)
        def _():
            head_ref[...] = stage[0, :head, :]
            first_copy().start()

        @pl.when(i > 0)
        def _():
            tile_copy(i, slot).start()

        @pl.when(i == nt - 1)
        def _():
            if nt >= 2:
                (first_copy() if nt == 2 else tile_copy(nt - 2, (nt - 2) % 2)).wait()
            (first_copy() if nt == 1 else tile_copy(nt - 1, (nt - 1) % 2)).wait()

    row = pl.BlockSpec((te, D), lambda i: (i, 0))
    return _call(
        body, name="prenorm_bwd", grid=(nt,),
        in_specs=[row, row, row, pl.BlockSpec((1, D), lambda i: (0, 0))],
        out_specs=(pl.BlockSpec(memory_space=pl.ANY), pl.BlockSpec((head, D), lambda i: (0, 0)),
                   pl.BlockSpec((1, D), lambda i: (0, 0))),
        out_shape=(jax.ShapeDtypeStruct((seq, D), F32), jax.ShapeDtypeStruct((head, D), F32),
                   jax.ShapeDtypeStruct((1, D), F32)),
        scratch_shapes=[pltpu.VMEM((2, te, D), F32), pltpu.SemaphoreType.DMA((2,))],
        compiler_params=_cp(("arbitrary",)),
    )(dh, x_ext, dx_out, norm_w)


def _adam_reduce(parts, w, m, v, name):
    r, n = w.shape
    tr = _pick(r, (128,)) if r % 128 == 0 else r

    def body(p_ref, w_ref, m_ref, v_ref, g_ref, d_ref, m2_ref, v2_ref):
        g = p_ref[0]
        for s in range(1, NDEV):
            g = g + p_ref[s]
        _adam_write(g, w_ref, m_ref, v_ref, g_ref, d_ref, m2_ref, v2_ref)

    blk = pl.BlockSpec((tr, n), lambda i: (i, 0))
    out = jax.ShapeDtypeStruct((r, n), F32)
    return _call(
        body, name=name, grid=(r // tr,),
        in_specs=[pl.BlockSpec((NDEV, tr, n), lambda i: (0, i, 0)), blk, blk, blk],
        out_specs=(blk, blk, blk, blk), out_shape=(out, out, out, out),
        compiler_params=_cp(("parallel",)),
    )(parts, w, m, v)


def _adam_write(g, w_ref, m_ref, v_ref, g_ref, d_ref, m2_ref, v2_ref):
    c1 = 1.0 - ADAM_B1 ** ADAM_STEP
    c2 = 1.0 - ADAM_B2 ** ADAM_STEP
    m2 = ADAM_B1 * m_ref[...] + (1.0 - ADAM_B1) * g
    v2 = ADAM_B2 * v_ref[...] + (1.0 - ADAM_B2) * (g * g)
    g_ref[...] = g
    m2_ref[...] = m2
    v2_ref[...] = v2
    d_ref[...] = -ADAM_LR * ((m2 / c1) / (jnp.sqrt(v2 / c2) + ADAM_EPS) + ADAM_WD * w_ref[...])


def _adam_chips(own, recv, w, m, v, name):
    r, n = w.shape
    tr, tc = _shard_tile(r, n)

    def body(own_ref, p_ref, w_ref, m_ref, v_ref, g_ref, d_ref, m2_ref, v2_ref):
        my_chip = 2 * lax.axis_index("x") + lax.axis_index("y")
        g = None
        for j in range(NCHIP):
            part = jnp.where(my_chip == j, own_ref[...], p_ref[j].astype(F32))
            g = part if g is None else g + part
        _adam_write(g, w_ref, m_ref, v_ref, g_ref, d_ref, m2_ref, v2_ref)

    blk = pl.BlockSpec((tr, tc), lambda i, k: (i, k))
    out = jax.ShapeDtypeStruct((r, n), F32)
    return _call(
        body, name=name, grid=(r // tr, n // tc),
        in_specs=[blk, pl.BlockSpec((NCHIP, tr, tc), lambda i, k: (0, i, k)), blk, blk, blk],
        out_specs=(blk, blk, blk, blk), out_shape=(out, out, out, out),
        compiler_params=_cp(("parallel", "parallel")),
    )(own, recv, w, m, v)


SMALL = ("norm_w", "a_log", "dt_bias", "dn_norm_w", "dw_b", "ln_w", "ln_b", "b_cf_out", "final_norm_w")


def kernel(x, meta, norm_w, w_in, conv_qkv_w, a_log, dt_bias, dn_norm_w, w_dn_out, dw_w, dw_b, ln_w, ln_b, w_cf_out, b_cf_out, w_o, final_norm_w, loss_target, m_meta, m_norm_w, m_w_in, m_conv_qkv_w, m_a_log, m_dt_bias, m_dn_norm_w, m_w_dn_out, m_dw_w, m_dw_b, m_ln_w, m_ln_b, m_w_cf_out, m_b_cf_out, m_w_o, m_final_norm_w, v_meta, v_norm_w, v_w_in, v_conv_qkv_w, v_a_log, v_dt_bias, v_dn_norm_w, v_w_dn_out, v_dw_w, v_dw_b, v_ln_w, v_ln_b, v_w_cf_out, v_b_cf_out, v_w_o, v_final_norm_w):
    seq = x.shape[1]
    pad = (-(seq + NMETA)) % LANE
    in_w = w_in.shape[2] * NDEV
    n_qkvz = 4 * D
    n_ba = 2 * H

    me = 4 * lax.axis_index("x") + 2 * lax.axis_index("y") + lax.axis_index("c")
    late = [w_dn_out[0].astype(BF16), w_cf_out[0].astype(BF16), w_o[0].astype(BF16), conv_qkv_w[0], dw_w[0]]
    late_lands = [lax.dynamic_update_index_in_dim(jnp.zeros((NDEV,) + b.shape, b.dtype), b, me, 0) for b in late]
    w_in_g, meta_g = _gather_two_level([w_in[0].astype(BF16).T, meta], "gather_weights")
    late_send, late_recv, late_thru, late_land_thru, _ = _split_start(
        _all_copies, NDEV - 1, "gather_late_start", late, late_lands, meta_g)
    w_full_t = w_in_g.reshape(in_w, D)
    c_glu = n_qkvz + n_ba
    c_zb, c_mg = c_glu + 2 * D, c_glu + 3 * D
    w_main_t = jnp.concatenate([w_full_t[:n_qkvz], w_full_t[c_glu:c_zb], w_full_t[c_mg:], w_full_t[c_zb:c_mg]],
                               axis=0)
    w_ba_t = jnp.pad(w_full_t[n_qkvz:n_qkvz + n_ba], ((0, LANE - n_ba), (0, 0)))
    meta_full = jnp.transpose(meta_g, (1, 0, 2)).reshape(NMETA, D)
    ab = jnp.pad(jnp.concatenate([a_log, dt_bias], axis=0), ((0, 0), (H, LANE - 2 * H)))

    x_ext = jnp.concatenate([jnp.zeros((pad, D), F32), meta_full, x[0]], axis=0)

    proj, ba, h = _proj_fwd(x_ext, norm_w, w_main_t, w_ba_t)
    _, (w_dn_g, w_cf_g, w_o_g, cqw_g, dww_g) = _split_wait(
        _all_copies, "gather_late_wait", late_send, late_recv, late_thru, late_land_thru, ba)
    w_dn, w_cf, w_oo = (t.reshape(D, D) for t in (w_dn_g, w_cf_g, w_o_g))
    cqw = jnp.transpose(cqw_g, (1, 0, 2)).reshape(KQ, 3 * D)
    dww = jnp.transpose(dww_g, (1, 0, 2)).reshape(KD, D)
    qkv, bg = _qkv_conv_fwd(proj, ba, cqw, ab, pad)
    o, sall, tall = _delta_fwd(qkv, bg)
    o_n, y_a = _o_post_fwd(o, proj, dn_norm_w, w_dn)
    c1, c3, y_b = _conv_b_fwd(proj, dww, dw_b, ln_w, ln_b, w_cf)
    merged, z = _merge_fwd(y_a, y_b, proj, b_cf_out, w_oo)
    dx_out, dx_out_b, loss_part, g_final_w = _final_fwd_bwd(x_ext, z, loss_target[0], final_norm_w.reshape(1, D))

    g_w_o = _mm_tn(merged, dx_out_b, "g_w_o_mm")
    dy_a, dy_b, dproj, g_b_cf = _merge_bwd(dx_out_b, w_oo, y_a, y_b, proj, b_cf_out)
    g_w_cf = _mm_tn(c3, dy_b, "g_w_cf_mm")
    g_w_dn = _mm_tn(o_n, dy_a, "g_w_dn_mm")
    dc1, dproj, sums_b = _conv_b_bwd1(dy_b, w_cf, c1, proj, ln_w, ln_b, dproj)
    dproj, g_dw_w = _conv_b_bwd2(dc1, proj, dww, dproj)
    do, dproj, g_dn_w = _o_post_bwd(dy_a, w_dn, o, proj, dn_norm_w, dproj)
    dqkv, dbg = _delta_bwd(qkv, bg, sall, tall, do)
    dproj, g_cqw = _qkv_conv_bwd(proj, dqkv, cqw, dproj)
    dba, dab = _ba_bwd(dbg, ba, ab, pad)
    g_w_main_t = _mm_tn(dproj, h, "g_w_main_mm")
    g_w_ba_t = _mm_tn(dba, h, "g_w_ba_mm")

    g_w_full_t = jnp.concatenate([g_w_main_t[:n_qkvz], g_w_ba_t[:n_ba], g_w_main_t[CB_GA_ * D:CB_MA * D],
                                  g_w_main_t[CB_ZB * D:], g_w_main_t[CB_MA * D:CB_ZB * D]], axis=0)
    big = [t.reshape(NCHIP, 2, t.shape[0] // NDEV, D) for t in (g_w_full_t, g_w_dn, g_w_cf, g_w_o)]
    sw_send, sw_recv, big_thru, sw_land, sw_token = _split_start(
        _sibling_copies, NCHIP, "swap_sibling_start", big,
        [lax.empty((NCHIP,) + t.shape[2:], t.dtype) for t in big], g_w_ba_t)
    dh = _dh_mm(dproj, dba, w_main_t, w_ba_t + sw_token[0, 0].astype(BF16), 0)
    big_back, from_sibling = _split_wait(_sibling_copies, "swap_sibling_wait", sw_send, sw_recv, big_thru, sw_land, dh)
    pairs = [_pair_add(a, g, f"pair_add_{i}") for i, (a, g) in enumerate(zip(big_back, from_sibling))]
    send_sems, recv_sems, pair_thru, land_thru, token = _split_start(
        _chip_copies, NCHIP - 1, "scatter_chips_start",
        [p for p, _ in pairs], [jnp.zeros(p.shape, p.dtype) for p, _ in pairs], g_w_ba_t)
    dh = _dh_mm(dproj, dba, w_main_t, w_ba_t + token[0, 0].astype(BF16), 1, dh)
    grad_x, dhead, g_norm_w = _prenorm_bwd(dh, x_ext, dx_out, norm_w, seq)
    _, from_chips = _split_wait(_chip_copies, "scatter_chips_wait", send_sems, recv_sems, pair_thru, land_thru,
                                g_norm_w)

    split_cols = lambda t: jnp.transpose(t.reshape(t.shape[0], NDEV, t.shape[1] // NDEV), (1, 0, 2))
    small = {"norm_w": g_norm_w, "a_log": dab[0:1, H:2 * H], "dt_bias": dab[1:2, H:2 * H], "dn_norm_w": g_dn_w,
             "dw_b": sums_b[2:3], "ln_w": sums_b[0:1], "ln_b": sums_b[1:2], "b_cf_out": g_b_cf,
             "final_norm_w": g_final_w}
    small_vec = jnp.concatenate([small[k] for k in SMALL], axis=1)
    ns = small_vec.shape[1]
    ns_pad = (-ns) % LANE
    small_vec = jnp.pad(small_vec, ((0, 0), (0, ns_pad)))
    p_meta, p_cqw, p_dww, p_small = _exchange(
        [split_cols(dhead[pad:pad + NMETA]), split_cols(g_cqw), split_cols(g_dw_w), small_vec],
        [True] * 3 + [False], "exchange_small")

    res = {}
    res["w_in"] = tuple(t.T for t in _adam_chips(pairs[0][1], from_chips[0], w_in[0].T, m_w_in[0].T, v_w_in[0].T,
                                                   "adam_w_in"))
    res["w_dn_out"] = _adam_chips(pairs[1][1], from_chips[1], w_dn_out[0], m_w_dn_out[0], v_w_dn_out[0], "adam_w_dn")
    res["w_cf_out"] = _adam_chips(pairs[2][1], from_chips[2], w_cf_out[0], m_w_cf_out[0], v_w_cf_out[0], "adam_w_cf")
    res["w_o"] = _adam_chips(pairs[3][1], from_chips[3], w_o[0], m_w_o[0], v_w_o[0], "adam_w_o")
    res["meta"] = _adam_reduce(p_meta, meta, m_meta, v_meta, "adam_meta")
    res["conv_qkv_w"] = _adam_reduce(p_cqw, conv_qkv_w[0], m_conv_qkv_w[0], v_conv_qkv_w[0], "adam_conv_qkv_w")
    res["dw_w"] = _adam_reduce(p_dww, dw_w[0], m_dw_w[0], v_dw_w[0], "adam_dw_w")
    loc = dict(norm_w=(norm_w, m_norm_w, v_norm_w), a_log=(a_log, m_a_log, v_a_log), dt_bias=(dt_bias, m_dt_bias, v_dt_bias),
               dn_norm_w=(dn_norm_w, m_dn_norm_w, v_dn_norm_w), dw_b=(dw_b, m_dw_b, v_dw_b), ln_w=(ln_w, m_ln_w, v_ln_w),
               ln_b=(ln_b, m_ln_b, v_ln_b), b_cf_out=(b_cf_out, m_b_cf_out, v_b_cf_out),
               final_norm_w=(final_norm_w, m_final_norm_w, v_final_norm_w))
    cat = lambda j: jnp.pad(jnp.concatenate([loc[k][j].reshape(1, -1) for k in SMALL], axis=1), ((0, 0), (0, ns_pad)))
    small_res = _adam_reduce(p_small, cat(0), cat(1), cat(2), "adam_small")
    off = 0
    for k in SMALL:
        wshape = loc[k][0].shape
        nk = loc[k][0].size
        res[k] = tuple(t[:, off:off + nk].reshape(wshape) for t in small_res)
        off += nk
    shaped = dict(w_in=w_in.shape, w_dn_out=w_dn_out.shape, w_cf_out=w_cf_out.shape, w_o=w_o.shape, meta=meta.shape,
                  conv_qkv_w=conv_qkv_w.shape, dw_w=dw_w.shape)
    for k, shp in shaped.items():
        res[k] = tuple(t.reshape(shp) for t in res[k])

    loss = lax.psum(loss_part[0, 0], ("x", "y", "c"))
    order = ("meta", "norm_w", "w_in", "conv_qkv_w", "a_log", "dt_bias", "dn_norm_w", "w_dn_out", "dw_w", "dw_b", "ln_w",
             "ln_b", "w_cf_out", "b_cf_out", "w_o", "final_norm_w")
    outs = [loss, grad_x[None]]
    for j in range(4):
        outs += [res[k][j] for k in order]
    return tuple(outs)
```

```python
import functools

import jax
import jax.numpy as jnp
from jax import lax
from jax.experimental import pallas as pl
from jax.experimental.pallas import tpu as pltpu

F32 = jnp.float32
BF16 = jnp.bfloat16
HI = lax.Precision.HIGHEST

D = 1024
H = 8
DK = 128
C = 64
NMETA = 16
KQ = 4
KD = 31
HALO_Q = 8
HALO_D = 32
EPS = 1e-6
NDEV = 8
LANE = 128
MIB = 1024 * 1024

ADAM_LR, ADAM_B1, ADAM_B2, ADAM_EPS, ADAM_WD, ADAM_STEP = 0.001, 0.9, 0.999, 1e-08, 0.01, 10

CB_Q, CB_K, CB_V, CB_ZA, CB_GA_, CB_GB_, CB_MA, CB_MB, CB_ZB = range(9)
NCB = 9


def _pick(n, cands):
    for c in cands:
        if n % c == 0:
            return c
    raise ValueError(f"no tile for {n}")


def _cp(sem=None, vmem_mib=40):
    kw = dict(vmem_limit_bytes=vmem_mib * MIB)
    if sem is not None:
        kw["dimension_semantics"] = sem
    return pltpu.CompilerParams(**kw)


def _call(body, **kw):
    return pl.pallas_call(body, **kw)


def _dot(a, b):
    return jnp.dot(a.astype(BF16), b.astype(BF16), preferred_element_type=F32)


def _dot_nt(a, b):
    return lax.dot_general(a.astype(BF16), b.astype(BF16), (((1,), (1,)), ((), ())), preferred_element_type=F32)


def _dot_tn(a, b):
    return lax.dot_general(a.astype(BF16), b.astype(BF16), (((0,), (0,)), ((), ())), preferred_element_type=F32)


def _dot_hi(a, b):
    return jnp.dot(a, b, precision=HI, preferred_element_type=F32)


def _sig(x):
    return 0.5 * jnp.tanh(0.5 * x) + 0.5


def _dsilu(x, s):
    return s * (1.0 + x * (1.0 - s))


def _rowsum(x):
    return jnp.sum(x, axis=-1, keepdims=True)


def _colsum(x):
    return jnp.sum(x, axis=0, keepdims=True)


def _exchange(arrs, scatter, name):
    n = len(arrs)
    out_shape = []
    for a, sc in zip(arrs, scatter):
        shp = a.shape if sc else (NDEV,) + a.shape
        out_shape.append(jax.ShapeDtypeStruct(shp, a.dtype))

    def body(*refs):
        ins, outs = refs[:n], refs[n:2 * n]
        send_sems, recv_sems, loc_sems = refs[2 * n:]
        x, y, c = lax.axis_index("x"), lax.axis_index("y"), lax.axis_index("c")
        me = 4 * x + 2 * y + c
        copies = []
        for a in range(n):
            for k in range(1, NDEV):
                px = 1 - x if (k >> 2) & 1 else x
                py = 1 - y if (k >> 1) & 1 else y
                pc = 1 - c if k & 1 else c
                src = ins[a].at[4 * px + 2 * py + pc] if scatter[a] else ins[a]
                cp = pltpu.make_async_remote_copy(
                    src_ref=src, dst_ref=outs[a].at[me],
                    send_sem=send_sems.at[a * (NDEV - 1) + k - 1], recv_sem=recv_sems.at[a * (NDEV - 1) + k - 1],
                    device_id=(px, py, pc), device_id_type=pl.DeviceIdType.MESH)
                cp.start()
                copies.append(cp)
            loc = pltpu.make_async_copy(ins[a].at[me] if scatter[a] else ins[a], outs[a].at[me], loc_sems.at[a])
            loc.start()
            copies.append(loc)
        for cp in copies:
            cp.wait()

    any_spec = pl.BlockSpec(memory_space=pl.ANY)
    return _call(
        body, name=name, out_shape=tuple(out_shape),
        in_specs=[any_spec] * n, out_specs=tuple([any_spec] * n),
        scratch_shapes=[pltpu.SemaphoreType.DMA((n * (NDEV - 1),)), pltpu.SemaphoreType.DMA((n * (NDEV - 1),)),
                        pltpu.SemaphoreType.DMA((n,))],
    )(*arrs)


NCHIP = 4


def _gather_two_level(arrs, name):
    n = len(arrs)
    per = NDEV - 1

    def body(*refs):
        ins, outs = refs[:n], refs[n:2 * n]
        send_sems, recv_sems, loc_sems = refs[2 * n:]
        x, y, c = lax.axis_index("x"), lax.axis_index("y"), lax.axis_index("c")
        me, sibling = (x, y, c), (x, y, 1 - c)
        chips = [(1 - x, y), (x, 1 - y), (1 - x, 1 - y)]

        def slot(a, px, py, pc):
            return outs[a].at[4 * px + 2 * py + pc]

        def copy(a, k, block, to, src=None):
            return pltpu.make_async_remote_copy(
                src_ref=slot(a, *block) if src is None else src, dst_ref=slot(a, *block),
                send_sem=send_sems.at[a * per + k], recv_sem=recv_sems.at[a * per + k],
                device_id=to, device_id_type=pl.DeviceIdType.MESH)

        local, sent = [], []
        for a in range(n):
            mine = pltpu.make_async_copy(ins[a], slot(a, *me), loc_sems.at[a])
            mine.start()
            local.append(mine)
            first = [copy(a, 1 + j, me, (*chip, c), src=ins[a]) for j, chip in enumerate(chips)]
            first.append(copy(a, 0, me, sibling, src=ins[a]))
            for cp in first:
                cp.start()
            sent += first
        for j, chip in enumerate(chips):
            for a in range(n):
                copy(a, 1 + j, (*chip, c), me).wait_recv()
                cp = copy(a, 4 + j, (*chip, c), sibling)
                cp.start()
                sent.append(cp)
        for a in range(n):
            copy(a, 0, sibling, me).wait_recv()
            for j, chip in enumerate(chips):
                copy(a, 4 + j, (*chip, 1 - c), me).wait_recv()
        for cp in sent:
            cp.wait_send()
        for cp in local:
            cp.wait()

    any_spec = pl.BlockSpec(memory_space=pl.ANY)
    return _call(
        body, name=name, out_shape=tuple(jax.ShapeDtypeStruct((NDEV,) + a.shape, a.dtype) for a in arrs),
        in_specs=[any_spec] * n, out_specs=tuple([any_spec] * n),
        scratch_shapes=[pltpu.SemaphoreType.DMA((n * per,)), pltpu.SemaphoreType.DMA((n * per,)),
                        pltpu.SemaphoreType.DMA((n,))],
    )(*arrs)


def _pair_add(arr4, got, name):
    _, _, r, n = arr4.shape
    tr, tc = _shard_tile(r, n)

    def body(a_ref, g_ref, p_ref, own_ref):
        c = lax.axis_index("c")
        my_chip = 2 * lax.axis_index("x") + lax.axis_index("y")
        s = jnp.where(c == 0, a_ref[0, 0], a_ref[0, 1]) + g_ref[0]
        p_ref[0] = s.astype(BF16)

        @pl.when(pl.program_id(2) == my_chip)
        def _():
            own_ref[...] = s

    return _call(
        body, name=name, grid=(r // tr, n // tc, NCHIP),
        in_specs=[pl.BlockSpec((1, 2, tr, tc), lambda i, k, j: (j, 0, i, k)),
                  pl.BlockSpec((1, tr, tc), lambda i, k, j: (j, i, k))],
        out_specs=(pl.BlockSpec((1, tr, tc), lambda i, k, j: (j, i, k)), pl.BlockSpec((tr, tc), lambda i, k, j: (i, k))),
        out_shape=(jax.ShapeDtypeStruct((NCHIP, r, n), BF16), jax.ShapeDtypeStruct((r, n), F32)),
        compiler_params=_cp(("parallel", "parallel", "arbitrary")),
    )(arr4, got)


def _shard_tile(r, n):
    return (128, n) if r % 128 == 0 else (r, 256)


def _all_copies(srcs, lands, send_sems, recv_sems):
    x, y, c = lax.axis_index("x"), lax.axis_index("y"), lax.axis_index("c")
    per = NDEV - 1
    copies = []
    for a in range(len(srcs)):
        for k in range(1, NDEV):
            px = 1 - x if (k >> 2) & 1 else x
            py = 1 - y if (k >> 1) & 1 else y
            pc = 1 - c if k & 1 else c
            copies.append(pltpu.make_async_remote_copy(
                src_ref=srcs[a], dst_ref=lands[a].at[4 * x + 2 * y + c],
                send_sem=send_sems.at[a * per + k - 1], recv_sem=recv_sems.at[a * per + k - 1],
                device_id=(px, py, pc), device_id_type=pl.DeviceIdType.MESH))
    return copies


def _split_start(make_copies, peers, name, arrs, lands, after):
    n = len(arrs)
    nsem = n * peers

    def body(*refs):
        srcs, land_in = refs[:n], refs[n:2 * n]
        send_sems, recv_sems = refs[2 * n + 1:2 * n + 3]
        token = refs[-1]
        for cp in make_copies(srcs, land_in, send_sems, recv_sems):
            cp.start()
        token[...] = jnp.zeros_like(token)

    hbm = pl.BlockSpec(memory_space=pltpu.HBM)
    sem = pl.BlockSpec(memory_space=pltpu.SEMAPHORE)
    both = list(arrs) + list(lands)
    outs = _call(
        body, name=name,
        out_shape=(pltpu.SemaphoreType.DMA((nsem,)), pltpu.SemaphoreType.DMA((nsem,)),
                   *[pltpu.HBM(a.shape, a.dtype) for a in both], jax.ShapeDtypeStruct((SUBLANES, LANE), F32)),
        in_specs=[hbm] * (2 * n) + [pl.BlockSpec(memory_space=pl.ANY)],
        out_specs=(sem, sem, *[hbm] * (2 * n), pl.BlockSpec(memory_space=pltpu.VMEM)),
        input_output_aliases={i: i + 2 for i in range(2 * n)},
        compiler_params=pltpu.CompilerParams(has_side_effects=pltpu.SideEffectType.DATAFLOW_SIDE_EFFECTING),
    )(*[pltpu.with_memory_space_constraint(t, pltpu.HBM) for t in both], after)
    return outs[0], outs[1], outs[2:2 + n], outs[2 + n:2 + 2 * n], outs[-1]


def _split_wait(make_copies, name, send_sems, recv_sems, arrs, lands, after):
    n = len(arrs)

    def body(*refs):
        srcs, land_in = refs[:n], refs[n:2 * n]
        send, recv = refs[2 * n], refs[2 * n + 1]
        for cp in make_copies(srcs, land_in, send, recv):
            cp.wait_send()
            cp.wait_recv()

    hbm = pl.BlockSpec(memory_space=pltpu.HBM)
    sem = pl.BlockSpec(memory_space=pltpu.SEMAPHORE)
    both = list(arrs) + list(lands)
    outs = _call(
        body, name=name,
        out_shape=tuple(pltpu.HBM(a.shape, a.dtype) for a in both),
        in_specs=[hbm] * (2 * n) + [sem, sem, pl.BlockSpec(memory_space=pl.ANY)], out_specs=tuple([hbm] * (2 * n)),
        input_output_aliases={i: i for i in range(2 * n)},
        compiler_params=pltpu.CompilerParams(has_side_effects=pltpu.SideEffectType.DATAFLOW_SIDE_EFFECTING),
    )(*both, send_sems, recv_sems, after)
    return outs[:n], outs[n:]


def _sibling_copies(srcs, lands, send_sems, recv_sems):
    x, y, c = lax.axis_index("x"), lax.axis_index("y"), lax.axis_index("c")
    copies = []
    for a in range(len(srcs)):
        for j in range(NCHIP):
            copies.append(pltpu.make_async_remote_copy(
                src_ref=srcs[a].at[j, 1 - c], dst_ref=lands[a].at[j],
                send_sem=send_sems.at[a * NCHIP + j], recv_sem=recv_sems.at[a * NCHIP + j],
                device_id=(x, y, 1 - c), device_id_type=pl.DeviceIdType.MESH))
    return copies


def _chip_copies(srcs, lands, send_sems, recv_sems):
    x, y, c = lax.axis_index("x"), lax.axis_index("y"), lax.axis_index("c")
    per = NCHIP - 1
    copies = []
    for a in range(len(srcs)):
        for k in range(1, NCHIP):
            px = 1 - x if (k >> 1) & 1 else x
            py = 1 - y if k & 1 else y
            copies.append(pltpu.make_async_remote_copy(
                src_ref=srcs[a].at[2 * px + py], dst_ref=lands[a].at[2 * x + y],
                send_sem=send_sems.at[a * per + k - 1], recv_sem=recv_sems.at[a * per + k - 1],
                device_id=(px, py, c), device_id_type=pl.DeviceIdType.MESH))
    return copies


def _mm_tn(a, b, name):
    t, m = a.shape
    n = b.shape[1]
    tt = _pick(t, (1664, 640, 128))
    tm = _pick(m, (1024, 512, 128))
    tn = _pick(n, (1152, 1024, 512, 128))
    nt = t // tt

    def body(a_ref, b_ref, o_ref):
        s = pl.program_id(2)
        part = _dot_tn(a_ref[...], b_ref[...])

        @pl.when(s == 0)
        def _():
            o_ref[...] = part

        @pl.when(s > 0)
        def _():
            o_ref[...] += part

    return _call(
        body, name=name, grid=(m // tm, n // tn, nt),
        in_specs=[pl.BlockSpec((tt, tm), lambda i, j, s: (s, i)), pl.BlockSpec((tt, tn), lambda i, j, s: (s, j))],
        out_specs=pl.BlockSpec((tm, tn), lambda i, j, s: (i, j)),
        out_shape=jax.ShapeDtypeStruct((m, n), F32),
        compiler_params=_cp(("parallel", "parallel", "arbitrary")),
    )(a, b)


def _proj_fwd(x_ext, norm_w, w_main_t, w_ba_t):
    lp = x_ext.shape[0]
    n = w_main_t.shape[0]
    tm = _pick(lp, (832, 640, 320))
    tn = 1024

    def body(x_ref, nw_ref, w_ref, wba_ref, proj_ref, ba_ref, h_ref):
        @pl.when(pl.program_id(1) == 0)
        def _():
            x = x_ref[...]
            r = lax.rsqrt(jnp.mean(x * x, axis=-1, keepdims=True) + EPS)
            h = (x * r * nw_ref[...]).astype(BF16)
            h_ref[...] = h
            ba_ref[...] = _dot_nt(h, wba_ref[...])

        proj_ref[...] = _dot_nt(h_ref[...], w_ref[...])

    return _call(
        body, name="proj_fwd", grid=(lp // tm, n // tn),
        in_specs=[pl.BlockSpec((tm, D), lambda i, j: (i, 0)), pl.BlockSpec((1, D), lambda i, j: (0, 0)),
                  pl.BlockSpec((tn, D), lambda i, j: (j, 0)), pl.BlockSpec((LANE, D), lambda i, j: (0, 0))],
        out_specs=(pl.BlockSpec((tm, tn), lambda i, j: (i, j)), pl.BlockSpec((tm, LANE), lambda i, j: (i, 0)),
                   pl.BlockSpec((tm, D), lambda i, j: (i, 0))),
        out_shape=(jax.ShapeDtypeStruct((lp, n), F32), jax.ShapeDtypeStruct((lp, LANE), F32),
                   jax.ShapeDtypeStruct((lp, D), BF16)),
        compiler_params=_cp(("parallel", "arbitrary")),
    )(x_ext, norm_w, w_main_t, w_ba_t)


def _dh_mm(dproj, dba, w_main_t, w_ba_t, part, dh_so_far=None):
    lp, n = dproj.shape
    tm = _pick(lp, (832, 640, 320))
    tn = 1024
    tk = 2304
    nk = n // tk
    tiles = lp // tm
    first = (tiles + 1) // 2
    t0, nt = (0, first) if part == 0 else (first, tiles - first)
    if nt == 0:
        return dh_so_far

    def body(a_ref, ba_ref, b_ref, bba_ref, *rest):
        o_ref, acc = rest[-2:]
        kk = pl.program_id(2)

        @pl.when(kk == 0)
        def _():
            acc[...] = jnp.dot(ba_ref[...], bba_ref[...], preferred_element_type=F32)

        acc[...] += jnp.dot(a_ref[...], b_ref[...], preferred_element_type=F32)

        @pl.when(kk == nk - 1)
        def _():
            o_ref[...] = acc[...]

    prev = [] if dh_so_far is None else [dh_so_far]
    return _call(
        body, name=f"dh_mm_{part}", grid=(nt, D // tn, nk),
        in_specs=[pl.BlockSpec((tm, tk), lambda i, j, kk: (i + t0, kk)),
                  pl.BlockSpec((tm, LANE), lambda i, j, kk: (i + t0, 0)),
                  pl.BlockSpec((tk, tn), lambda i, j, kk: (kk, j)), pl.BlockSpec((LANE, tn), lambda i, j, kk: (0, j))]
                 + [pl.BlockSpec(memory_space=pl.ANY)] * len(prev),
        out_specs=pl.BlockSpec((tm, tn), lambda i, j, kk: (i + t0, j)),
        out_shape=jax.ShapeDtypeStruct((lp, D), F32),
        input_output_aliases={4: 0} if prev else {},
        scratch_shapes=[pltpu.VMEM((tm, tn), F32)],
        compiler_params=_cp(("parallel", "parallel", "arbitrary")),
    )(dproj, dba, w_main_t, w_ba_t, *prev)


def _beta_g(ba, ab, row0, pad):
    lane = lax.broadcasted_iota(jnp.int32, ba.shape, 1)
    rows = row0 + lax.broadcasted_iota(jnp.int32, ba.shape, 0)
    z = ba + ab[1:2, :]
    sp = jnp.maximum(z, 0.0) + jnp.log(1.0 + jnp.exp(-jnp.abs(z)))
    val = jnp.where(lane < H, _sig(ba), -jnp.exp(ab[0:1, :]) * sp)
    return jnp.where((lane < 2 * H) & (rows >= pad), val, 0.0)


def _qkv_conv_fwd(proj, ba, conv_w, ab, pad):
    lp = proj.shape[0]
    te = _pick(lp, (320,))
    hb = te // HALO_Q

    def body(main_ref, halo_ref, cw_ref, ba_ref, ab_ref, out_ref, bg_ref, pre_scr, tap_scr):
        i, s = pl.program_id(0), pl.program_id(1)
        pre_scr[:HALO_Q, :] = jnp.where(i > 0, halo_ref[...], 0.0)
        pre_scr[HALO_Q:, :] = main_ref[...]
        scale = jnp.where(s == 0, DK ** -0.5, 1.0)
        off = HALO_Q - (KQ - 1)

        def head(h, carry):
            cs = pl.ds(pl.multiple_of(h * DK, DK), DK)
            for j in range(KQ - 1):
                tap_scr[j] = pre_scr[off + j:off + j + te, cs]
            co = cw_ref[KQ - 1:KQ, cs] * pre_scr[HALO_Q:, cs]
            for j in range(KQ - 1):
                co = co + cw_ref[j:j + 1, cs] * tap_scr[j]
            a = co * _sig(co)
            r = lax.rsqrt(_rowsum(a * a) + EPS)
            out_ref[:, cs] = jnp.where(s == 2, a, a * (r * scale))
            return carry

        lax.fori_loop(0, H, head, 0, unroll=True)

        @pl.when(s == 0)
        def _():
            bg_ref[...] = _beta_g(ba_ref[...], ab_ref[...], i * te, pad)

    return _call(
        body, name="qkv_conv_fwd", grid=(lp // te, 3),
        in_specs=[pl.BlockSpec((te, D), lambda i, s: (i, s)),
                  pl.BlockSpec((HALO_Q, D), lambda i, s: (jnp.maximum(i * hb - 1, 0), s)),
                  pl.BlockSpec((KQ, D), lambda i, s: (0, s)),
                  pl.BlockSpec((te, LANE), lambda i, s: (i, 0)),
                  pl.BlockSpec((2, LANE), lambda i, s: (0, 0))],
        out_specs=(pl.BlockSpec((te, D), lambda i, s: (i, s)), pl.BlockSpec((te, LANE), lambda i, s: (i, 0))),
        out_shape=(jax.ShapeDtypeStruct((lp, 3 * D), F32), jax.ShapeDtypeStruct((lp, LANE), F32)),
        scratch_shapes=[pltpu.VMEM((te + HALO_Q, D), F32), pltpu.VMEM((KQ - 1, te, DK), F32)],
        compiler_params=_cp(("parallel", "arbitrary")),
    )(proj, proj, conv_w, ba, ab)


def _tri_masks():
    row = lax.broadcasted_iota(jnp.int32, (C, C), 0)
    col = lax.broadcasted_iota(jnp.int32, (C, C), 1)
    return row, col


def _split(a):
    hi = a.astype(BF16)
    return hi, (a - hi.astype(F32)).astype(BF16)


def _dot3(a, b, dims=(((1,), (0,)), ((), ()))):
    (ah, al), (bh, bl) = a, b
    mm = lambda x, y: lax.dot_general(x, y, dims, preferred_element_type=F32)
    return mm(ah, bh) + (mm(ah, bl) + mm(al, bh))


CHUNKS_PER_STEP = 5
CHUNKS_PER_STEP_BWD = 1
TINV_BLOCK = 16


def _tinv(ns, row, col):
    eye = (row == col).astype(F32)
    sh = TINV_BLOCK.bit_length() - 1
    same16 = (row >> sh) == (col >> sh)
    same32 = (row >> (sh + 1)) == (col >> (sh + 1))
    ys = [jnp.where(same16, -n, 0.0) for n in ns]
    ts = [eye + y for y in ys]
    sp = [_split(y) for y in ys]
    for level in range(3):
        yks = [_dot3(s, s) for s in sp]
        sp = [_split(yk) for yk in yks]
        ts = [t + _dot3(s, _split(t)) for s, t in zip(sp, ts)]
    for mask in (same32 & ~same16, ~same32):
        tsp = [_split(t) for t in ts]
        inner = [_dot3(_split(jnp.where(mask, n, 0.0)), t) for n, t in zip(ns, tsp)]
        ts = [t - _dot3(tp, _split(a)) for t, tp, a in zip(ts, tsp, inner)]
    return ts


def _chunk_common(q, k, v, bcol, gcc, gcr, incl, strict):
    dm = jnp.where(incl, jnp.exp(gcc - gcr), 0.0)
    kk = _dot_nt(k, k)
    qk = _dot_nt(q, k)
    gccw = jnp.broadcast_to(gcc, (C, DK))
    egc = jnp.exp(gccw)
    glast = gccw[C - 1:C, :]
    eend = jnp.exp(glast - gccw)
    elast = jnp.exp(glast)
    rhs = jnp.concatenate([v * bcol, k * (bcol * egc)], axis=1)
    return dm, kk, qk, egc, eend, elast, rhs


def _delta_fwd(qkv, bg):
    lp = qkv.shape[0]
    nc = lp // C
    heads = range(H)
    sls = [slice(h * DK, (h + 1) * DK) for h in heads]

    def body(q_ref, k_ref, v_ref, bg_ref, o_ref, sall_ref, tall_ref, s_scr):
        @pl.when(pl.program_id(0) == 0)
        def _():
            s_scr[...] = jnp.zeros_like(s_scr)

        row, col = _tri_masks()
        incl, strict = row >= col, row > col

        def prepare(sub):
            rs = slice(sub * C, (sub + 1) * C)
            bgt = bg_ref[rs, :]
            gc_all = _dot_hi(incl.astype(F32), bgt)
            gc_t = _dot_hi(bgt.T, (row <= col).astype(F32))
            qs, ks, vs = ([r[rs, sl] for sl in sls] for r in (q_ref, k_ref, v_ref))
            bcols = [jnp.broadcast_to(bgt[:, h:h + 1], (C, DK)) for h in heads]
            cm = [_chunk_common(qs[h], ks[h], vs[h], bcols[h], gc_all[:, H + h:H + h + 1], gc_t[H + h:H + h + 1, :],
                                incl, strict) for h in heads]
            dms, kks, qks, egcs, eends, elasts, rhss = zip(*cm)
            ts = _tinv([jnp.where(strict, bcols[h][:, :C] * kks[h] * dms[h], 0.0) for h in heads], row, col)
            sols = [_dot3(_split(ts[h]), _split(rhss[h])) for h in heads]
            qgs = [(qs[h] * egcs[h]).astype(BF16) for h in heads]
            ps = [(qks[h] * dms[h]).astype(BF16) for h in heads]
            kends = [(ks[h] * eends[h]).astype(BF16) for h in heads]
            return ts, sols, qgs, ps, kends, elasts

        prepared = [prepare(sub) for sub in range(CHUNKS_PER_STEP)]
        ss = [s_scr[h] for h in heads]
        for sub in range(CHUNKS_PER_STEP):
            rs = slice(sub * C, (sub + 1) * C)
            ts, sols, qgs, ps, kends, elasts = prepared[sub]
            sb = [s.astype(BF16) for s in ss]
            wvb = [(sols[h][:, :DK] - _dot(sols[h][:, DK:], sb[h])).astype(BF16) for h in heads]
            for h in heads:
                o_ref[rs, sls[h]] = _dot(qgs[h], sb[h]) + _dot(ps[h], wvb[h])
                sall_ref[sub, h] = ss[h]
                tall_ref[sub, h] = ts[h]
            ss = [ss[h] * elasts[h] + _dot_tn(kends[h], wvb[h]) for h in heads]
        for h in heads:
            s_scr[h] = ss[h]

    rows = CHUNKS_PER_STEP * C
    blk = lambda j: pl.BlockSpec((rows, D), lambda n: (n, j))
    return _call(
        body, name="delta_fwd", grid=(nc // CHUNKS_PER_STEP,),
        in_specs=[blk(0), blk(1), blk(2), pl.BlockSpec((rows, LANE), lambda n: (n, 0))],
        out_specs=(pl.BlockSpec((rows, D), lambda n: (n, 0)),
                   pl.BlockSpec((CHUNKS_PER_STEP, H, DK, DK), lambda n: (n, 0, 0, 0)),
                   pl.BlockSpec((CHUNKS_PER_STEP, H, C, C), lambda n: (n, 0, 0, 0))),
        out_shape=(jax.ShapeDtypeStruct((lp, D), F32), jax.ShapeDtypeStruct((nc, H, DK, DK), F32),
                   jax.ShapeDtypeStruct((nc, H, C, C), F32)),
        scratch_shapes=[pltpu.VMEM((H, DK, DK), F32)],
        compiler_params=_cp(("arbitrary",)),
    )(qkv, qkv, qkv, bg)


def _delta_bwd(qkv, bg, sall, tall, do):
    lp = qkv.shape[0]
    nc = lp // C

    heads = range(H)
    sls = [slice(h * DK, (h + 1) * DK) for h in heads]

    def body(q_ref, k_ref, v_ref, bg_ref, sall_ref, tall_ref, do_ref, dqkv_ref, dbg_ref, ds_scr):
        @pl.when(pl.program_id(0) == 0)
        def _():
            ds_scr[...] = jnp.zeros_like(ds_scr)

        dsns = [ds_scr[h] for h in heads]
        for sub in reversed(range(CHUNKS_PER_STEP_BWD)):
            dsns = chunk(sub, dsns, q_ref, k_ref, v_ref, bg_ref, sall_ref, tall_ref, do_ref, dqkv_ref, dbg_ref)
        for h in heads:
            ds_scr[h] = dsns[h]

    def chunk(sub, dsns, q_ref, k_ref, v_ref, bg_ref, sall_ref, tall_ref, do_ref, dqkv_ref, dbg_ref):
        rs = slice(sub * C, (sub + 1) * C)
        bgt = bg_ref[rs, :]
        row, col = _tri_masks()
        incl, strict = row >= col, row > col
        upper = (row <= col).astype(F32)
        gc_all = _dot_hi(incl.astype(F32), bgt)
        gc_t = _dot_hi(bgt.T, upper)
        lane = lax.broadcasted_iota(jnp.int32, (C, LANE), 1)
        lastrow = lax.broadcasted_iota(jnp.int32, (C, 1), 0) == C - 1
        qs, ks, vs, dos = ([r[rs, sl] for sl in sls] for r in (q_ref, k_ref, v_ref, do_ref))
        bcols = [jnp.broadcast_to(bgt[:, h:h + 1], (C, DK)) for h in heads]
        cm = [_chunk_common(qs[h], ks[h], vs[h], bcols[h], gc_all[:, H + h:H + h + 1], gc_t[H + h:H + h + 1, :],
                            incl, strict) for h in heads]
        dms, kks, qks, egcs, eends, elasts, rhss = zip(*cm)
        ss = [sall_ref[sub, h] for h in heads]
        ts = [tall_ref[sub, h] for h in heads]
        sb = [s.astype(BF16) for s in ss]
        dsb = [d.astype(BF16) for d in dsns]
        dob = [d.astype(BF16) for d in dos]
        sols = [_dot3(_split(ts[h]), _split(rhss[h])) for h in heads]
        ws = [sol[:, DK:] for sol in sols]
        qgs = [qs[h] * egcs[h] for h in heads]
        kends = [ks[h] * eends[h] for h in heads]
        wvs = [sols[h][:, :DK] - _dot(ws[h], sb[h]) for h in heads]
        wvb = [wv.astype(BF16) for wv in wvs]
        dwvs = [_dot_tn(qks[h] * dms[h], dob[h]) + _dot(kends[h], dsb[h]) for h in heads]
        dps = [jnp.where(incl, _dot_nt(dob[h], wvb[h]), 0.0) for h in heads]
        dqgs = [_dot_nt(dob[h], sb[h]) for h in heads]
        dkends = [_dot_nt(wvb[h], dsb[h]) for h in heads]
        ds_before = [_dot_tn(qgs[h], dob[h]) + elasts[h] * dsns[h] - _dot_tn(ws[h], dwvs[h]) for h in heads]
        dglasts = [elasts[h] * jnp.sum(ss[h] * dsns[h], keepdims=True) for h in heads]
        dws = [-_dot_nt(dwvs[h], sb[h]) for h in heads]
        tts = [_split(ts[h].T) for h in heads]
        drhss = [_dot3(tts[h], _split(jnp.concatenate([dwvs[h], dws[h]], axis=1))) for h in heads]
        nt_dims = (((1,), (1,)), ((), ()))
        dns = [jnp.where(strict, -_dot3(_split(drhss[h]), _split(sols[h]), nt_dims), 0.0) for h in heads]
        dbeta_t = jnp.zeros((C, LANE), F32)
        dgc_t = jnp.zeros((C, LANE), F32)
        for h in heads:
            q, k, v, bcol, dm, kk, qk, egc, eend = qs[h], ks[h], vs[h], bcols[h], dms[h], kks[h], qks[h], egcs[h], eends[h]
            drv, drk = drhss[h][:, :DK], drhss[h][:, DK:]
            dn, dp, dqg, dkend = dns[h], dps[h], dqgs[h], dkends[h]
            rk = _rowsum(drk * k)
            dkk = dn * (bcol[:, :C] * dm)
            dqk = dp * dm
            e = (dn * (bcol[:, :C] * kk) + dp * qk) * dm
            tk = _rowsum(dkend * kends[h])
            dgc = rk * bcol * egc + _rowsum(e) - _rowsum(e.T) + _rowsum(dqg * qgs[h]) - tk
            dgc = dgc + jnp.where(lastrow, dglasts[h] + jnp.sum(tk, keepdims=True), 0.0)
            dbeta = _rowsum(drv * v) + rk * egc + _rowsum(dn * kk * dm)
            dqkv_ref[rs, sls[h]] = _dot(dqk, k) + dqg * egc
            dqkv_ref[rs, D + h * DK:D + (h + 1) * DK] = (drk * (bcol * egc) + _dot(dkk, k) + _dot_tn(dkk, k)
                                                        + _dot_tn(dqk, q) + dkend * eend)
            dqkv_ref[rs, 2 * D + h * DK:2 * D + (h + 1) * DK] = bcol * drv
            dbeta_t = jnp.where(lane == h, dbeta, dbeta_t)
            dgc_t = jnp.where(lane == H + h, dgc, dgc_t)
        dbg_ref[rs, :] = dbeta_t + _dot_hi(upper, dgc_t)
        return ds_before

    steps = nc // CHUNKS_PER_STEP_BWD
    rows = CHUNKS_PER_STEP_BWD * C
    rev = lambda n: steps - 1 - n
    blk = lambda j: pl.BlockSpec((rows, D), lambda n: (rev(n), j))
    return _call(
        body, name="delta_bwd", grid=(steps,),
        in_specs=[blk(0), blk(1), blk(2), pl.BlockSpec((rows, LANE), lambda n: (rev(n), 0)),
                  pl.BlockSpec((CHUNKS_PER_STEP_BWD, H, DK, DK), lambda n: (rev(n), 0, 0, 0)),
                  pl.BlockSpec((CHUNKS_PER_STEP_BWD, H, C, C), lambda n: (rev(n), 0, 0, 0)),
                  pl.BlockSpec((rows, D), lambda n: (rev(n), 0))],
        out_specs=(pl.BlockSpec((rows, 3 * D), lambda n: (rev(n), 0)),
                   pl.BlockSpec((rows, LANE), lambda n: (rev(n), 0))),
        out_shape=(jax.ShapeDtypeStruct((lp, 3 * D), F32), jax.ShapeDtypeStruct((lp, LANE), F32)),
        scratch_shapes=[pltpu.VMEM((H, DK, DK), F32)],
        compiler_params=_cp(("arbitrary",)),
    )(qkv, qkv, qkv, bg, sall, tall, do)


def _o_post_fwd(o, proj, dn_w, w_dn):
    lp = o.shape[0]
    te = _pick(lp, (640, 320))

    def body(o_ref, za_ref, w_ref, wdn_ref, out_ref, ya_ref):
        za = za_ref[...]
        gate = za * _sig(za)
        for h in range(H):
            sl = slice(h * DK, (h + 1) * DK)
            oh = o_ref[:, sl]
            r = lax.rsqrt(jnp.mean(oh * oh, axis=-1, keepdims=True) + EPS)
            out_ref[:, sl] = (oh * r * w_ref[...] * gate[:, sl]).astype(BF16)
        ya_ref[...] = _dot(out_ref[...], wdn_ref[...])

    row = pl.BlockSpec((te, D), lambda i: (i, 0))
    return _call(
        body, name="o_post_fwd", grid=(lp // te,),
        in_specs=[row, pl.BlockSpec((te, D), lambda i: (i, CB_ZA)), pl.BlockSpec((1, DK), lambda i: (0, 0)),
                  pl.BlockSpec((D, D), lambda i: (0, 0))],
        out_specs=(row, row),
        out_shape=(jax.ShapeDtypeStruct((lp, D), BF16), jax.ShapeDtypeStruct((lp, D), F32)),
        compiler_params=_cp(("parallel",)),
    )(o, proj, dn_w, w_dn)


def _o_post_bwd(dy_a, w_dn, o, proj, dn_w, dproj):
    lp = o.shape[0]
    te = _pick(lp, (320,))

    def body(dya_ref, wdn_ref, o_ref, za_ref, w_ref, _, do_ref, dza_ref, dw_ref, don_ref):
        @pl.when(pl.program_id(0) == 0)
        def _():
            dw_ref[...] = jnp.zeros_like(dw_ref)

        don_ref[...] = _dot_nt(dya_ref[...], wdn_ref[...])
        za = za_ref[...]
        sz = _sig(za)
        gate, dgate = za * sz, _dsilu(za, sz)
        w = w_ref[...]
        dw = jnp.zeros((1, DK), F32)
        for h in range(H):
            sl = slice(h * DK, (h + 1) * DK)
            oh, g = o_ref[:, sl], don_ref[:, sl]
            r = lax.rsqrt(jnp.mean(oh * oh, axis=-1, keepdims=True) + EPS)
            ohat = oh * r
            dza_ref[:, sl] = (g * ohat * w * dgate[:, sl]).astype(BF16)
            don = g * gate[:, sl]
            dw = dw + _colsum(don * ohat)
            dohat = don * w
            do_ref[:, sl] = r * (dohat - ohat * jnp.mean(dohat * ohat, axis=-1, keepdims=True))
        dw_ref[...] += dw

    return _call(
        body, name="o_post_bwd", grid=(lp // te,),
        in_specs=[pl.BlockSpec((te, D), lambda i: (i, 0)), pl.BlockSpec((D, D), lambda i: (0, 0)),
                  pl.BlockSpec((te, D), lambda i: (i, 0)),
                  pl.BlockSpec((te, D), lambda i: (i, CB_ZA)), pl.BlockSpec((1, DK), lambda i: (0, 0)),
                  pl.BlockSpec(memory_space=pl.ANY)],
        out_specs=(pl.BlockSpec((te, D), lambda i: (i, 0)), pl.BlockSpec((te, D), lambda i: (i, CB_ZA)),
                   pl.BlockSpec((1, DK), lambda i: (0, 0))),
        out_shape=(jax.ShapeDtypeStruct((lp, D), F32), jax.ShapeDtypeStruct(dproj.shape, dproj.dtype),
                   jax.ShapeDtypeStruct((1, DK), F32)),
        input_output_aliases={5: 1},
        scratch_shapes=[pltpu.VMEM((te, D), F32)],
        compiler_params=_cp(("arbitrary",)),
    )(dy_a, w_dn, o, proj, dn_w, dproj)


def _qkv_conv_bwd(proj, dqkv, conv_w, dproj):
    lp = proj.shape[0]
    te = _pick(lp, (320,))
    hb = te // HALO_Q
    nt = lp // te
    last_hb = lp // HALO_Q - 1

    def body(main_ref, prev_ref, next_ref, dmain_ref, dnext_ref, cw_ref, _, dpre_ref, dcw_ref, pre_scr, dn_scr,
             tap_scr, dco_scr, dsh_scr):
        s, i = pl.program_id(0), pl.program_id(1)

        @pl.when(i == 0)
        def _():
            dcw_ref[...] = jnp.zeros_like(dcw_ref)

        ne = te + HALO_Q
        pre_scr[:HALO_Q, :] = jnp.where(i > 0, prev_ref[...], 0.0)
        pre_scr[HALO_Q:ne, :] = main_ref[...]
        pre_scr[ne:, :] = jnp.where(i < nt - 1, next_ref[...], 0.0)
        dn_scr[:te, :] = dmain_ref[...]
        dn_scr[te:, :] = jnp.where(i < nt - 1, dnext_ref[...], 0.0)
        scale = jnp.where(s == 0, DK ** -0.5, 1.0)
        off = HALO_Q - (KQ - 1)

        def head(h, carry):
            cs = pl.ds(pl.multiple_of(h * DK, DK), DK)
            for j in range(KQ - 1):
                tap_scr[j] = pre_scr[off + j:off + j + ne, cs]
            taps = [tap_scr[j] for j in range(KQ - 1)] + [pre_scr[HALO_Q:, cs]]
            co = cw_ref[0:1, cs] * taps[0]
            for j in range(1, KQ):
                co = co + cw_ref[j:j + 1, cs] * taps[j]
            sg = _sig(co)
            a = co * sg
            g = dn_scr[:, cs]
            r = lax.rsqrt(_rowsum(a * a) + EPS)
            yhat = a * r
            da = jnp.where(s == 2, g, (scale * r) * (g - yhat * _rowsum(g * yhat)))
            dco = da * _dsilu(co, sg)
            dco_scr[...] = dco
            for j in range(KQ - 1):
                dsh_scr[j] = dco_scr[KQ - 1 - j:KQ - 1 - j + te, :]
            dpre = cw_ref[KQ - 1:KQ, cs] * dco[:te, :]
            for j in range(KQ - 1):
                dpre = dpre + cw_ref[j:j + 1, cs] * dsh_scr[j]
            dpre_ref[:, cs] = dpre.astype(BF16)
            dcw_ref[:, cs] += jnp.concatenate([_colsum(dco[:te] * taps[j][:te]) for j in range(KQ)], axis=0)
            return carry

        lax.fori_loop(0, H, head, 0, unroll=True)

    return _call(
        body, name="qkv_conv_bwd", grid=(3, nt),
        in_specs=[pl.BlockSpec((te, D), lambda s, i: (i, s)),
                  pl.BlockSpec((HALO_Q, D), lambda s, i: (jnp.maximum(i * hb - 1, 0), s)),
                  pl.BlockSpec((HALO_Q, D), lambda s, i: (jnp.minimum((i + 1) * hb, last_hb), s)),
                  pl.BlockSpec((te, D), lambda s, i: (i, s)),
                  pl.BlockSpec((HALO_Q, D), lambda s, i: (jnp.minimum((i + 1) * hb, last_hb), s)),
                  pl.BlockSpec((KQ, D), lambda s, i: (0, s)),
                  pl.BlockSpec(memory_space=pl.ANY)],
        out_specs=(pl.BlockSpec((te, D), lambda s, i: (i, s)), pl.BlockSpec((KQ, D), lambda s, i: (0, s))),
        out_shape=(jax.ShapeDtypeStruct(dproj.shape, dproj.dtype), jax.ShapeDtypeStruct((KQ, 3 * D), F32)),
        input_output_aliases={6: 0},
        scratch_shapes=[pltpu.VMEM((te + 2 * HALO_Q, D), F32), pltpu.VMEM((te + HALO_Q, D), F32),
                        pltpu.VMEM((KQ - 1, te + HALO_Q, DK), F32), pltpu.VMEM((te + HALO_Q, DK), F32),
                        pltpu.VMEM((KQ - 1, te, DK), F32)],
        compiler_params=_cp(("arbitrary", "arbitrary")),
    )(proj, proj, proj, dqkv, dqkv, conv_w, dproj)


def _ba_bwd(dbg, ba, ab, pad):
    lp = ba.shape[0]
    te = _pick(lp, (640, 320))

    def body(dbg_ref, ba_ref, ab_ref, dba_ref, dab_ref):
        i = pl.program_id(0)

        @pl.when(i == 0)
        def _():
            dab_ref[...] = jnp.zeros_like(dab_ref)

        ba, ab = ba_ref[...], ab_ref[...]
        lane = lax.broadcasted_iota(jnp.int32, ba.shape, 1)
        rows = i * te + lax.broadcasted_iota(jnp.int32, ba.shape, 0)
        g = jnp.where((lane < 2 * H) & (rows >= pad), dbg_ref[...], 0.0)
        sb = _sig(ba)
        z = ba + ab[1:2, :]
        sp = jnp.maximum(z, 0.0) + jnp.log(1.0 + jnp.exp(-jnp.abs(z)))
        nea = -jnp.exp(ab[0:1, :])
        dz = g * nea * _sig(z)
        dba_ref[...] = jnp.where(lane < H, g * sb * (1.0 - sb), dz).astype(BF16)
        is_g = (lane >= H) & (lane < 2 * H)
        dab_ref[...] += jnp.concatenate([_colsum(jnp.where(is_g, g * nea * sp, 0.0)),
                                         _colsum(jnp.where(is_g, dz, 0.0))], axis=0)

    return _call(
        body, name="ba_bwd", grid=(lp // te,),
        in_specs=[pl.BlockSpec((te, LANE), lambda i: (i, 0)), pl.BlockSpec((te, LANE), lambda i: (i, 0)),
                  pl.BlockSpec((2, LANE), lambda i: (0, 0))],
        out_specs=(pl.BlockSpec((te, LANE), lambda i: (i, 0)), pl.BlockSpec((2, LANE), lambda i: (0, 0))),
        out_shape=(jax.ShapeDtypeStruct((lp, LANE), BF16), jax.ShapeDtypeStruct((2, LANE), F32)),
        compiler_params=_cp(("arbitrary",)),
    )(dbg, ba, ab)


SUBLANES = 8
CONV_RB = 64


def _fill_shifted(sh_scr, src_scr, cs):
    n = sh_scr.shape[1]
    for s in range(1, SUBLANES):
        sh_scr[s] = src_scr[s:s + n, cs]


def _shifted(sh_scr, src_scr, cs, r, r0, n):
    s, a8 = r % SUBLANES, r - r % SUBLANES
    if s == 0:
        return src_scr[r0 + a8:r0 + a8 + n, cs]
    return sh_scr[s, r0 + a8:r0 + a8 + n, :]


def _conv_b_fwd(proj, dw_w, dw_b, ln_w, ln_b, w_cf):
    lp = proj.shape[0]
    te = _pick(lp, (320,))
    hb = te // HALO_D

    def body(a_ref, b_ref, ha_ref, hb_ref, zb_ref, w_ref, wb_ref, lw_ref, lb_ref, wcf_ref, c1_ref, c3_ref, yb_ref,
             c0_scr, sh_scr):
        i = pl.program_id(0)
        c0_scr[:HALO_D, :] = jnp.where(i > 0, ha_ref[...] * _sig(hb_ref[...]), 0.0)
        c0_scr[HALO_D:, :] = a_ref[...] * _sig(b_ref[...])
        off = HALO_D - (KD - 1)
        def lane_block(cb, carry):
            cs = pl.ds(pl.multiple_of(cb * LANE, LANE), LANE)
            _fill_shifted(sh_scr, c0_scr, cs)
            for r0 in range(0, te, CONV_RB):
                acc = None
                for j in range(KD):
                    term = w_ref[j:j + 1, cs] * _shifted(sh_scr, c0_scr, cs, off + j, r0, CONV_RB)
                    acc = term if acc is None else acc + term
                c1_ref[r0:r0 + CONV_RB, cs] = acc + wb_ref[:, cs]
            return carry

        lax.fori_loop(0, D // LANE, lane_block, 0)
        c1 = c1_ref[...]
        mu = jnp.mean(c1, axis=-1, keepdims=True)
        xc = c1 - mu
        c2 = xc * lax.rsqrt(jnp.mean(xc * xc, axis=-1, keepdims=True) + EPS) * lw_ref[...] + lb_ref[...]
        zb = zb_ref[...]
        c3 = (c2 * _sig(c2) * zb * _sig(zb)).astype(BF16)
        c3_ref[...] = c3
        yb_ref[...] = _dot(c3, wcf_ref[...])

    vec = pl.BlockSpec((1, D), lambda i: (0, 0))
    row = pl.BlockSpec((te, D), lambda i: (i, 0))
    return _call(
        body, name="conv_b_fwd", grid=(lp // te,),
        in_specs=[pl.BlockSpec((te, D), lambda i: (i, CB_GA_)), pl.BlockSpec((te, D), lambda i: (i, CB_GB_)),
                  pl.BlockSpec((HALO_D, D), lambda i: (jnp.maximum(i * hb - 1, 0), CB_GA_)),
                  pl.BlockSpec((HALO_D, D), lambda i: (jnp.maximum(i * hb - 1, 0), CB_GB_)),
                  pl.BlockSpec((te, D), lambda i: (i, CB_ZB)),
                  pl.BlockSpec((KD, D), lambda i: (0, 0)), vec, vec, vec, pl.BlockSpec((D, D), lambda i: (0, 0))],
        out_specs=(row, row, row),
        out_shape=(jax.ShapeDtypeStruct((lp, D), F32), jax.ShapeDtypeStruct((lp, D), BF16),
                   jax.ShapeDtypeStruct((lp, D), F32)),
        scratch_shapes=[pltpu.VMEM((te + HALO_D, D), F32), pltpu.VMEM((SUBLANES, te + HALO_D - SUBLANES, LANE), F32)],
        compiler_params=_cp(("parallel",)),
    )(proj, proj, proj, proj, proj, dw_w, dw_b, ln_w, ln_b, w_cf)


def _conv_b_bwd1(dy_b, w_cf, c1, proj, ln_w, ln_b, dproj):
    lp = c1.shape[0]
    te = _pick(lp, (320,))

    def body(dyb_ref, wcf_ref, c1_ref, zb_ref, lw_ref, lb_ref, _, dc1_ref, dzb_ref, sums_ref):
        @pl.when(pl.program_id(0) == 0)
        def _():
            sums_ref[...] = jnp.zeros_like(sums_ref)

        c1, g = c1_ref[...], _dot_nt(dyb_ref[...], wcf_ref[...])
        mu = jnp.mean(c1, axis=-1, keepdims=True)
        xc = c1 - mu
        rstd = lax.rsqrt(jnp.mean(xc * xc, axis=-1, keepdims=True) + EPS)
        xh = xc * rstd
        lw = lw_ref[...]
        c2 = xh * lw + lb_ref[...]
        s2 = _sig(c2)
        zb = zb_ref[...]
        sz = _sig(zb)
        dc2 = g * (zb * sz) * _dsilu(c2, s2)
        dzb_ref[...] = (g * (c2 * s2) * _dsilu(zb, sz)).astype(BF16)
        dxh = dc2 * lw
        dc1 = rstd * (dxh - jnp.mean(dxh, axis=-1, keepdims=True) - xh * jnp.mean(dxh * xh, axis=-1, keepdims=True))
        dc1_ref[...] = dc1
        sums_ref[...] += jnp.concatenate([_colsum(dc2 * xh), _colsum(dc2), _colsum(dc1)], axis=0)

    vec = pl.BlockSpec((1, D), lambda i: (0, 0))
    return _call(
        body, name="conv_b_bwd1", grid=(lp // te,),
        in_specs=[pl.BlockSpec((te, D), lambda i: (i, 0)), pl.BlockSpec((D, D), lambda i: (0, 0)),
                  pl.BlockSpec((te, D), lambda i: (i, 0)),
                  pl.BlockSpec((te, D), lambda i: (i, CB_ZB)), vec, vec, pl.BlockSpec(memory_space=pl.ANY)],
        out_specs=(pl.BlockSpec((te, D), lambda i: (i, 0)), pl.BlockSpec((te, D), lambda i: (i, CB_ZB)),
                   pl.BlockSpec((3, D), lambda i: (0, 0))),
        out_shape=(jax.ShapeDtypeStruct((lp, D), F32), jax.ShapeDtypeStruct(dproj.shape, dproj.dtype),
                   jax.ShapeDtypeStruct((3, D), F32)),
        input_output_aliases={6: 1},
        compiler_params=_cp(("arbitrary",)),
    )(dy_b, w_cf, c1, proj, ln_w, ln_b, dproj)


def _conv_b_bwd2(dc1, proj, dw_w, dproj):
    lp = dc1.shape[0]
    te = _pick(lp, (320,))
    hb = te // HALO_D
    nt = lp // te
    last_hb = lp // HALO_D - 1

    def body(g_ref, gn_ref, a_ref, b_ref, ha_ref, hb_ref, w_ref, _, dab_ref, dw_ref, c0_scr, g_scr, dc0_scr,
             csh_scr, gsh_scr):
        i = pl.program_id(0)

        @pl.when(i == 0)
        def _():
            dw_ref[...] = jnp.zeros_like(dw_ref)

        a, b = a_ref[...], b_ref[...]
        sb = _sig(b)
        c0_scr[:HALO_D, :] = jnp.where(i > 0, ha_ref[...] * _sig(hb_ref[...]), 0.0)
        c0_scr[HALO_D:, :] = a * sb
        g_scr[:te, :] = g_ref[...]
        g_scr[te:, :] = jnp.where(i < nt - 1, gn_ref[...], 0.0)
        off = HALO_D - (KD - 1)
        def lane_block(cb, carry):
            cs = pl.ds(pl.multiple_of(cb * LANE, LANE), LANE)
            _fill_shifted(csh_scr, c0_scr, cs)
            _fill_shifted(gsh_scr, g_scr, cs)
            for r0 in range(0, te, CONV_RB):
                acc = None
                for j in range(KD):
                    term = w_ref[j:j + 1, cs] * _shifted(gsh_scr, g_scr, cs, KD - 1 - j, r0, CONV_RB)
                    acc = term if acc is None else acc + term
                dc0_scr[r0:r0 + CONV_RB, cs] = acc
            parts = [None] * KD
            for r0 in range(0, te, CONV_RB):
                g = g_scr[r0:r0 + CONV_RB, cs].reshape(CONV_RB // SUBLANES, SUBLANES, LANE)
                for j in range(KD):
                    x = _shifted(csh_scr, c0_scr, cs, off + j, r0, CONV_RB)
                    p = jnp.sum(g * x.reshape(CONV_RB // SUBLANES, SUBLANES, LANE), axis=0)
                    parts[j] = p if parts[j] is None else parts[j] + p
            dw_ref[:, cs] += jnp.concatenate([_colsum(p) for p in parts], axis=0)
            return carry

        lax.fori_loop(0, D // LANE, lane_block, 0)
        dc0 = dc0_scr[...]
        dab_ref[:, :D] = (dc0 * sb).astype(BF16)
        dab_ref[:, D:] = (dc0 * a * sb * (1.0 - sb)).astype(BF16)

    return _call(
        body, name="conv_b_bwd2", grid=(nt,),
        in_specs=[pl.BlockSpec((te, D), lambda i: (i, 0)),
                  pl.BlockSpec((HALO_D, D), lambda i: (jnp.minimum((i + 1) * hb, last_hb), 0)),
                  pl.BlockSpec((te, D), lambda i: (i, CB_GA_)), pl.BlockSpec((te, D), lambda i: (i, CB_GB_)),
                  pl.BlockSpec((HALO_D, D), lambda i: (jnp.maximum(i * hb - 1, 0), CB_GA_)),
                  pl.BlockSpec((HALO_D, D), lambda i: (jnp.maximum(i * hb - 1, 0), CB_GB_)),
                  pl.BlockSpec((KD, D), lambda i: (0, 0)), pl.BlockSpec(memory_space=pl.ANY)],
        out_specs=(pl.BlockSpec((te, 2 * D), lambda i: (i, CB_GA_ // 2)), pl.BlockSpec((KD, D), lambda i: (0, 0))),
        out_shape=(jax.ShapeDtypeStruct(dproj.shape, dproj.dtype), jax.ShapeDtypeStruct((KD, D), F32)),
        input_output_aliases={7: 0},
        scratch_shapes=[pltpu.VMEM((te + HALO_D, D), F32), pltpu.VMEM((te + HALO_D, D), F32), pltpu.VMEM((te, D), F32),
                        pltpu.VMEM((SUBLANES, te + HALO_D - SUBLANES, LANE), F32),
                        pltpu.VMEM((SUBLANES, te + HALO_D - SUBLANES, LANE), F32)],
        compiler_params=_cp(("arbitrary",)),
    )(dc1, dc1, proj, proj, proj, proj, dw_w, dproj)


def _merge_fwd(y_a, y_b, proj, b_cf, w_o):
    lp = y_a.shape[0]
    te = _pick(lp, (320,))

    def body(ya_ref, yb_ref, ga_ref, gb_ref, bias_ref, wo_ref, out_ref, z_ref):
        merged = (_sig(ga_ref[...]) * ya_ref[...] + _sig(gb_ref[...]) * (yb_ref[...] + bias_ref[...])).astype(BF16)
        out_ref[...] = merged
        z_ref[...] = _dot(merged, wo_ref[...])

    row = lambda j: pl.BlockSpec((te, D), lambda i: (i, j))
    return _call(
        body, name="merge_fwd", grid=(lp // te,),
        in_specs=[row(0), row(0), row(CB_MA), row(CB_MB), pl.BlockSpec((1, D), lambda i: (0, 0)),
                  pl.BlockSpec((D, D), lambda i: (0, 0))],
        out_specs=(row(0), row(0)),
        out_shape=(jax.ShapeDtypeStruct((lp, D), BF16), jax.ShapeDtypeStruct((lp, D), F32)),
        compiler_params=_cp(("parallel",)),
    )(y_a, y_b, proj, proj, b_cf, w_o)


def _merge_bwd(dx_out_b, w_o, y_a, y_b, proj, b_cf):
    lp = y_a.shape[0]
    te = _pick(lp, (320,))

    def body(dx_ref, wo_ref, ya_ref, yb_ref, ga_ref, gb_ref, bias_ref, dya_ref, dyb_ref, dg_ref, db_ref):
        @pl.when(pl.program_id(0) == 0)
        def _():
            db_ref[...] = jnp.zeros_like(db_ref)

        dm = _dot_nt(dx_ref[...], wo_ref[...])
        sa, sb = _sig(ga_ref[...]), _sig(gb_ref[...])
        dyb = sb * dm
        dya_ref[...] = (sa * dm).astype(BF16)
        dyb_ref[...] = dyb.astype(BF16)
        dg_ref[:, :D] = (dm * ya_ref[...] * sa * (1.0 - sa)).astype(BF16)
        dg_ref[:, D:] = (dm * (yb_ref[...] + bias_ref[...]) * sb * (1.0 - sb)).astype(BF16)
        db_ref[...] += _colsum(dyb)

    row = lambda j: pl.BlockSpec((te, D), lambda i: (i, j))
    act = jax.ShapeDtypeStruct((lp, D), BF16)
    return _call(
        body, name="merge_bwd", grid=(lp // te,),
        in_specs=[row(0), pl.BlockSpec((D, D), lambda i: (0, 0)), row(0), row(0), row(CB_MA), row(CB_MB),
                  pl.BlockSpec((1, D), lambda i: (0, 0))],
        out_specs=(row(0), row(0), pl.BlockSpec((te, 2 * D), lambda i: (i, CB_MA // 2)),
                   pl.BlockSpec((1, D), lambda i: (0, 0))),
        out_shape=(act, act, jax.ShapeDtypeStruct((lp, NCB * D), BF16), jax.ShapeDtypeStruct((1, D), F32)),
        compiler_params=_cp(("arbitrary",)),
    )(dx_out_b, w_o, y_a, y_b, proj, proj, b_cf)


def _final_fwd_bwd(x_ext, z, target, final_w):
    lp = x_ext.shape[0]
    te = _pick(lp, (640,))
    nsub = te // LANE

    def body(x_ref, z_ref, *rest):
        t_refs, (w_ref, dx_ref, dxb_ref, loss_ref, dw_ref) = rest[:nsub], rest[nsub:]
        i = pl.program_id(0)

        @pl.when(i == 0)
        def _():
            loss_ref[...] = jnp.zeros_like(loss_ref)
            dw_ref[...] = jnp.zeros_like(dw_ref)

        w = w_ref[...]
        for k in range(nsub):
            rs = slice(k * LANE, (k + 1) * LANE)
            xo = x_ref[rs, :] + z_ref[rs, :]
            r = lax.rsqrt(jnp.mean(xo * xo, axis=-1, keepdims=True) + EPS)
            xhat = xo * r
            err = xhat * w - t_refs[k][...]
            if k == 0:
                err = jnp.where(i > 0, err, 0.0)
            loss_ref[...] += 0.5 * jnp.sum(jnp.mean(err * err, axis=-1, keepdims=True), keepdims=True)
            dy = err * (1.0 / D)
            dw_ref[...] += _colsum(dy * xhat)
            dxn = dy * w
            dx = r * (dxn - xhat * jnp.mean(dxn * xhat, axis=-1, keepdims=True))
            dx_ref[rs, :] = dx
            dxb_ref[rs, :] = dx.astype(BF16)

    piece = lambda k: pl.BlockSpec((LANE, D), lambda i: (jnp.maximum(i * nsub + k - 1, 0), 0))
    row = pl.BlockSpec((te, D), lambda i: (i, 0))
    return _call(
        body, name="final_fwd_bwd", grid=(lp // te,),
        in_specs=[row, row] + [piece(k) for k in range(nsub)] + [pl.BlockSpec((1, D), lambda i: (0, 0))],
        out_specs=(row, row, pl.BlockSpec((1, 1), lambda i: (0, 0)), pl.BlockSpec((1, D), lambda i: (0, 0))),
        out_shape=(jax.ShapeDtypeStruct((lp, D), F32), jax.ShapeDtypeStruct((lp, D), BF16),
                   jax.ShapeDtypeStruct((1, 1), F32), jax.ShapeDtypeStruct((1, D), F32)),
        compiler_params=_cp(("arbitrary",)),
    )(x_ext, z, *([target] * nsub), final_w)


def _prenorm_bwd(dh, x_ext, dx_out, norm_w, seq):
    lp = x_ext.shape[0]
    te = _pick(lp, (640,))
    nt = lp // te
    head = lp - seq

    def body(dh_ref, x_ref, dxo_ref, w_ref, gx_ref, head_ref, dw_ref, stage, sems):
        i = pl.program_id(0)
        slot = i % 2

        def first_copy():
            return pltpu.make_async_copy(stage.at[0, pl.ds(head, te - head)], gx_ref.at[pl.ds(0, te - head)], sems.at[0])

        def tile_copy(step, s):
            return pltpu.make_async_copy(stage.at[s], gx_ref.at[pl.ds(pl.multiple_of(step * te - head, LANE), te)],
                                         sems.at[s])

        @pl.when(i == 0)
        def _():
            dw_ref[...] = jnp.zeros_like(dw_ref)

        @pl.when(i == 2)
        def _():
            first_copy().wait()

        @pl.when(i > 2)
        def _():
            tile_copy(i - 2, slot).wait()

        x, dh = x_ref[...], dh_ref[...]
        r = lax.rsqrt(jnp.mean(x * x, axis=-1, keepdims=True) + EPS)
        xhat = x * r
        dxn = dh * w_ref[...]
        stage[slot] = dxo_ref[...] + r * (dxn - xhat * jnp.mean(dxn * xhat, axis=-1, keepdims=True))
        dw_ref[...] += _colsum(dh * xhat)

        @pl.when(i == 0)
        def _():
            head_ref[...] = stage[0, :head, :]
            first_copy().start()

        @pl.when(i > 0)
        def _():
            tile_copy(i, slot).start()

        @pl.when(i == nt - 1)
        def _():
            if nt >= 2:
                (first_copy() if nt == 2 else tile_copy(nt - 2, (nt - 2) % 2)).wait()
            (first_copy() if nt == 1 else tile_copy(nt - 1, (nt - 1) % 2)).wait()

    row = pl.BlockSpec((te, D), lambda i: (i, 0))
    return _call(
        body, name="prenorm_bwd", grid=(nt,),
        in_specs=[row, row, row, pl.BlockSpec((1, D), lambda i: (0, 0))],
        out_specs=(pl.BlockSpec(memory_space=pl.ANY), pl.BlockSpec((head, D), lambda i: (0, 0)),
                   pl.BlockSpec((1, D), lambda i: (0, 0))),
        out_shape=(jax.ShapeDtypeStruct((seq, D), F32), jax.ShapeDtypeStruct((head, D), F32),
                   jax.ShapeDtypeStruct((1, D), F32)),
        scratch_shapes=[pltpu.VMEM((2, te, D), F32), pltpu.SemaphoreType.DMA((2,))],
        compiler_params=_cp(("arbitrary",)),
    )(dh, x_ext, dx_out, norm_w)


def _adam_reduce(parts, w, m, v, name):
    r, n = w.shape
    tr = _pick(r, (128,)) if r % 128 == 0 else r

    def body(p_ref, w_ref, m_ref, v_ref, g_ref, d_ref, m2_ref, v2_ref):
        g = p_ref[0]
        for s in range(1, NDEV):
            g = g + p_ref[s]
        _adam_write(g, w_ref, m_ref, v_ref, g_ref, d_ref, m2_ref, v2_ref)

    blk = pl.BlockSpec((tr, n), lambda i: (i, 0))
    out = jax.ShapeDtypeStruct((r, n), F32)
    return _call(
        body, name=name, grid=(r // tr,),
        in_specs=[pl.BlockSpec((NDEV, tr, n), lambda i: (0, i, 0)), blk, blk, blk],
        out_specs=(blk, blk, blk, blk), out_shape=(out, out, out, out),
        compiler_params=_cp(("parallel",)),
    )(parts, w, m, v)


def _adam_write(g, w_ref, m_ref, v_ref, g_ref, d_ref, m2_ref, v2_ref):
    c1 = 1.0 - ADAM_B1 ** ADAM_STEP
    c2 = 1.0 - ADAM_B2 ** ADAM_STEP
    m2 = ADAM_B1 * m_ref[...] + (1.0 - ADAM_B1) * g
    v2 = ADAM_B2 * v_ref[...] + (1.0 - ADAM_B2) * (g * g)
    g_ref[...] = g
    m2_ref[...] = m2
    v2_ref[...] = v2
    d_ref[...] = -ADAM_LR * ((m2 / c1) / (jnp.sqrt(v2 / c2) + ADAM_EPS) + ADAM_WD * w_ref[...])


def _adam_chips(own, recv, w, m, v, name):
    r, n = w.shape
    tr, tc = _shard_tile(r, n)

    def body(own_ref, p_ref, w_ref, m_ref, v_ref, g_ref, d_ref, m2_ref, v2_ref):
        my_chip = 2 * lax.axis_index("x") + lax.axis_index("y")
        g = None
        for j in range(NCHIP):
            part = jnp.where(my_chip == j, own_ref[...], p_ref[j].astype(F32))
            g = part if g is None else g + part
        _adam_write(g, w_ref, m_ref, v_ref, g_ref, d_ref, m2_ref, v2_ref)

    blk = pl.BlockSpec((tr, tc), lambda i, k: (i, k))
    out = jax.ShapeDtypeStruct((r, n), F32)
    return _call(
        body, name=name, grid=(r // tr, n // tc),
        in_specs=[blk, pl.BlockSpec((NCHIP, tr, tc), lambda i, k: (0, i, k)), blk, blk, blk],
        out_specs=(blk, blk, blk, blk), out_shape=(out, out, out, out),
        compiler_params=_cp(("parallel", "parallel")),
    )(own, recv, w, m, v)


SMALL = ("norm_w", "a_log", "dt_bias", "dn_norm_w", "dw_b", "ln_w", "ln_b", "b_cf_out", "final_norm_w")


def kernel(x, meta, norm_w, w_in, conv_qkv_w, a_log, dt_bias, dn_norm_w, w_dn_out, dw_w, dw_b, ln_w, ln_b, w_cf_out, b_cf_out, w_o, final_norm_w, loss_target, m_meta, m_norm_w, m_w_in, m_conv_qkv_w, m_a_log, m_dt_bias, m_dn_norm_w, m_w_dn_out, m_dw_w, m_dw_b, m_ln_w, m_ln_b, m_w_cf_out, m_b_cf_out, m_w_o, m_final_norm_w, v_meta, v_norm_w, v_w_in, v_conv_qkv_w, v_a_log, v_dt_bias, v_dn_norm_w, v_w_dn_out, v_dw_w, v_dw_b, v_ln_w, v_ln_b, v_w_cf_out, v_b_cf_out, v_w_o, v_final_norm_w):
    seq = x.shape[1]
    pad = (-(seq + NMETA)) % LANE
    in_w = w_in.shape[2] * NDEV
    n_qkvz = 4 * D
    n_ba = 2 * H

    me = 4 * lax.axis_index("x") + 2 * lax.axis_index("y") + lax.axis_index("c")
    late = [w_dn_out[0].astype(BF16), w_cf_out[0].astype(BF16), w_o[0].astype(BF16), conv_qkv_w[0], dw_w[0]]
    late_lands = [lax.dynamic_update_index_in_dim(jnp.zeros((NDEV,) + b.shape, b.dtype), b, me, 0) for b in late]
    w_in_g, meta_g = _gather_two_level([w_in[0].astype(BF16).T, meta], "gather_weights")
    late_send, late_recv, late_thru, late_land_thru, _ = _split_start(
        _all_copies, NDEV - 1, "gather_late_start", late, late_lands, meta_g)
    w_full_t = w_in_g.reshape(in_w, D)
    c_glu = n_qkvz + n_ba
    c_zb, c_mg = c_glu + 2 * D, c_glu + 3 * D
    w_main_t = jnp.concatenate([w_full_t[:n_qkvz], w_full_t[c_glu:c_zb], w_full_t[c_mg:], w_full_t[c_zb:c_mg]],
                               axis=0)
    w_ba_t = jnp.pad(w_full_t[n_qkvz:n_qkvz + n_ba], ((0, LANE - n_ba), (0, 0)))
    meta_full = jnp.transpose(meta_g, (1, 0, 2)).reshape(NMETA, D)
    ab = jnp.pad(jnp.concatenate([a_log, dt_bias], axis=0), ((0, 0), (H, LANE - 2 * H)))

    x_ext = jnp.concatenate([jnp.zeros((pad, D), F32), meta_full, x[0]], axis=0)

    proj, ba, h = _proj_fwd(x_ext, norm_w, w_main_t, w_ba_t)
    _, (w_dn_g, w_cf_g, w_o_g, cqw_g, dww_g) = _split_wait(
        _all_copies, "gather_late_wait", late_send, late_recv, late_thru, late_land_thru, ba)
    w_dn, w_cf, w_oo = (t.reshape(D, D) for t in (w_dn_g, w_cf_g, w_o_g))
    cqw = jnp.transpose(cqw_g, (1, 0, 2)).reshape(KQ, 3 * D)
    dww = jnp.transpose(dww_g, (1, 0, 2)).reshape(KD, D)
    qkv, bg = _qkv_conv_fwd(proj, ba, cqw, ab, pad)
    o, sall, tall = _delta_fwd(qkv, bg)
    o_n, y_a = _o_post_fwd(o, proj, dn_norm_w, w_dn)
    c1, c3, y_b = _conv_b_fwd(proj, dww, dw_b, ln_w, ln_b, w_cf)
    merged, z = _merge_fwd(y_a, y_b, proj, b_cf_out, w_oo)
    dx_out, dx_out_b, loss_part, g_final_w = _final_fwd_bwd(x_ext, z, loss_target[0], final_norm_w.reshape(1, D))

    g_w_o = _mm_tn(merged, dx_out_b, "g_w_o_mm")
    dy_a, dy_b, dproj, g_b_cf = _merge_bwd(dx_out_b, w_oo, y_a, y_b, proj, b_cf_out)
    g_w_cf = _mm_tn(c3, dy_b, "g_w_cf_mm")
    g_w_dn = _mm_tn(o_n, dy_a, "g_w_dn_mm")
    dc1, dproj, sums_b = _conv_b_bwd1(dy_b, w_cf, c1, proj, ln_w, ln_b, dproj)
    dproj, g_dw_w = _conv_b_bwd2(dc1, proj, dww, dproj)
    do, dproj, g_dn_w = _o_post_bwd(dy_a, w_dn, o, proj, dn_norm_w, dproj)
    dqkv, dbg = _delta_bwd(qkv, bg, sall, tall, do)
    dproj, g_cqw = _qkv_conv_bwd(proj, dqkv, cqw, dproj)
    dba, dab = _ba_bwd(dbg, ba, ab, pad)
    g_w_main_t = _mm_tn(dproj, h, "g_w_main_mm")
    g_w_ba_t = _mm_tn(dba, h, "g_w_ba_mm")

    g_w_full_t = jnp.concatenate([g_w_main_t[:n_qkvz], g_w_ba_t[:n_ba], g_w_main_t[CB_GA_ * D:CB_MA * D],
                                  g_w_main_t[CB_ZB * D:], g_w_main_t[CB_MA * D:CB_ZB * D]], axis=0)
    big = [t.reshape(NCHIP, 2, t.shape[0] // NDEV, D) for t in (g_w_full_t, g_w_dn, g_w_cf, g_w_o)]
    sw_send, sw_recv, big_thru, sw_land, sw_token = _split_start(
        _sibling_copies, NCHIP, "swap_sibling_start", big,
        [lax.empty((NCHIP,) + t.shape[2:], t.dtype) for t in big], g_w_ba_t)
    dh = _dh_mm(dproj, dba, w_main_t, w_ba_t + sw_token[0, 0].astype(BF16), 0)
    big_back, from_sibling = _split_wait(_sibling_copies, "swap_sibling_wait", sw_send, sw_recv, big_thru, sw_land, dh)
    pairs = [_pair_add(a, g, f"pair_add_{i}") for i, (a, g) in enumerate(zip(big_back, from_sibling))]
    send_sems, recv_sems, pair_thru, land_thru, token = _split_start(
        _chip_copies, NCHIP - 1, "scatter_chips_start",
        [p for p, _ in pairs], [jnp.zeros(p.shape, p.dtype) for p, _ in pairs], g_w_ba_t)
    dh = _dh_mm(dproj, dba, w_main_t, w_ba_t + token[0, 0].astype(BF16), 1, dh)
    grad_x, dhead, g_norm_w = _prenorm_bwd(dh, x_ext, dx_out, norm_w, seq)
    _, from_chips = _split_wait(_chip_copies, "scatter_chips_wait", send_sems, recv_sems, pair_thru, land_thru,
                                g_norm_w)

    split_cols = lambda t: jnp.transpose(t.reshape(t.shape[0], NDEV, t.shape[1] // NDEV), (1, 0, 2))
    small = {"norm_w": g_norm_w, "a_log": dab[0:1, H:2 * H], "dt_bias": dab[1:2, H:2 * H], "dn_norm_w": g_dn_w,
             "dw_b": sums_b[2:3], "ln_w": sums_b[0:1], "ln_b": sums_b[1:2], "b_cf_out": g_b_cf,
             "final_norm_w": g_final_w}
    small_vec = jnp.concatenate([small[k] for k in SMALL], axis=1)
    ns = small_vec.shape[1]
    ns_pad = (-ns) % LANE
    small_vec = jnp.pad(small_vec, ((0, 0), (0, ns_pad)))
    p_meta, p_cqw, p_dww, p_small = _exchange(
        [split_cols(dhead[pad:pad + NMETA]), split_cols(g_cqw), split_cols(g_dw_w), small_vec],
        [True] * 3 + [False], "exchange_small")

    res = {}
    res["w_in"] = tuple(t.T for t in _adam_chips(pairs[0][1], from_chips[0], w_in[0].T, m_w_in[0].T, v_w_in[0].T,
                                                   "adam_w_in"))
    res["w_dn_out"] = _adam_chips(pairs[1][1], from_chips[1], w_dn_out[0], m_w_dn_out[0], v_w_dn_out[0], "adam_w_dn")
    res["w_cf_out"] = _adam_chips(pairs[2][1], from_chips[2], w_cf_out[0], m_w_cf_out[0], v_w_cf_out[0], "adam_w_cf")
    res["w_o"] = _adam_chips(pairs[3][1], from_chips[3], w_o[0], m_w_o[0], v_w_o[0], "adam_w_o")
    res["meta"] = _adam_reduce(p_meta, meta, m_meta, v_meta, "adam_meta")
    res["conv_qkv_w"] = _adam_reduce(p_cqw, conv_qkv_w[0], m_conv_qkv_w[0], v_conv_qkv_w[0], "adam_conv_qkv_w")
    res["dw_w"] = _adam_reduce(p_dww, dw_w[0], m_dw_w[0], v_dw_w[0], "adam_dw_w")
    loc = dict(norm_w=(norm_w, m_norm_w, v_norm_w), a_log=(a_log, m_a_log, v_a_log), dt_bias=(dt_bias, m_dt_bias, v_dt_bias),
               dn_norm_w=(dn_norm_w, m_dn_norm_w, v_dn_norm_w), dw_b=(dw_b, m_dw_b, v_dw_b), ln_w=(ln_w, m_ln_w, v_ln_w),
               ln_b=(ln_b, m_ln_b, v_ln_b), b_cf_out=(b_cf_out, m_b_cf_out, v_b_cf_out),
               final_norm_w=(final_norm_w, m_final_norm_w, v_final_norm_w))
    cat = lambda j: jnp.pad(jnp.concatenate([loc[k][j].reshape(1, -1) for k in SMALL], axis=1), ((0, 0), (0, ns_pad)))
    small_res = _adam_reduce(p_small, cat(0), cat(1), cat(2), "adam_small")
    off = 0
    for k in SMALL:
        wshape = loc[k][0].shape
        nk = loc[k][0].size
        res[k] = tuple(t[:, off:off + nk].reshape(wshape) for t in small_res)
        off += nk
    shaped = dict(w_in=w_in.shape, w_dn_out=w_dn_out.shape, w_cf_out=w_cf_out.shape, w_o=w_o.shape, meta=meta.shape,
                  conv_qkv_w=conv_qkv_w.shape, dw_w=dw_w.shape)
    for k, shp in shaped.items():
        res[k] = tuple(t.reshape(shp) for t in res[k])

    loss = lax.psum(loss_part[0, 0], ("x", "y", "c"))
    order = ("meta", "norm_w", "w_in", "conv_qkv_w", "a_log", "dt_bias", "dn_norm_w", "w_dn_out", "dw_w", "dw_b", "ln_w",
             "ln_b", "w_cf_out", "b_cf_out", "w_o", "final_norm_w")
    outs = [loss, grad_x[None]]
    for j in range(4):
        outs += [res[k][j] for k in order]
    return tuple(outs)
```

```python
import functools

import jax
import jax.numpy as jnp
from jax import lax
from jax.experimental import pallas as pl
from jax.experimental.pallas import tpu as pltpu

F32 = jnp.float32
BF16 = jnp.bfloat16
HI = lax.Precision.HIGHEST

D = 1024
H = 8
DK = 128
C = 64
NMETA = 16
KQ = 4
KD = 31
HALO_Q = 8
HALO_D = 32
EPS = 1e-6
NDEV = 8
LANE = 128
MIB = 1024 * 1024

ADAM_LR, ADAM_B1, ADAM_B2, ADAM_EPS, ADAM_WD, ADAM_STEP = 0.001, 0.9, 0.999, 1e-08, 0.01, 10

CB_Q, CB_K, CB_V, CB_ZA, CB_GA_, CB_GB_, CB_MA, CB_MB, CB_ZB = range(9)
NCB = 9


def _pick(n, cands):
    for c in cands:
        if n % c == 0:
            return c
    raise ValueError(f"no tile for {n}")


def _cp(sem=None, vmem_mib=40):
    kw = dict(vmem_limit_bytes=vmem_mib * MIB)
    if sem is not None:
        kw["dimension_semantics"] = sem
    return pltpu.CompilerParams(**kw)


def _call(body, **kw):
    return pl.pallas_call(body, **kw)


def _dot(a, b):
    return jnp.dot(a.astype(BF16), b.astype(BF16), preferred_element_type=F32)


def _dot_nt(a, b):
    return lax.dot_general(a.astype(BF16), b.astype(BF16), (((1,), (1,)), ((), ())), preferred_element_type=F32)


def _dot_tn(a, b):
    return lax.dot_general(a.astype(BF16), b.astype(BF16), (((0,), (0,)), ((), ())), preferred_element_type=F32)


def _dot_hi(a, b):
    return jnp.dot(a, b, precision=HI, preferred_element_type=F32)


def _sig(x):
    return 0.5 * jnp.tanh(0.5 * x) + 0.5


def _dsilu(x, s):
    return s * (1.0 + x * (1.0 - s))


def _rowsum(x):
    return jnp.sum(x, axis=-1, keepdims=True)


def _colsum(x):
    return jnp.sum(x, axis=0, keepdims=True)


def _exchange(arrs, scatter, name):
    n = len(arrs)
    out_shape = []
    for a, sc in zip(arrs, scatter):
        shp = a.shape if sc else (NDEV,) + a.shape
        out_shape.append(jax.ShapeDtypeStruct(shp, a.dtype))

    def body(*refs):
        ins, outs = refs[:n], refs[n:2 * n]
        send_sems, recv_sems, loc_sems = refs[2 * n:]
        x, y, c = lax.axis_index("x"), lax.axis_index("y"), lax.axis_index("c")
        me = 4 * x + 2 * y + c
        copies = []
        for a in range(n):
            for k in range(1, NDEV):
                px = 1 - x if (k >> 2) & 1 else x
                py = 1 - y if (k >> 1) & 1 else y
                pc = 1 - c if k & 1 else c
                src = ins[a].at[4 * px + 2 * py + pc] if scatter[a] else ins[a]
                cp = pltpu.make_async_remote_copy(
                    src_ref=src, dst_ref=outs[a].at[me],
                    send_sem=send_sems.at[a * (NDEV - 1) + k - 1], recv_sem=recv_sems.at[a * (NDEV - 1) + k - 1],
                    device_id=(px, py, pc), device_id_type=pl.DeviceIdType.MESH)
                cp.start()
                copies.append(cp)
            loc = pltpu.make_async_copy(ins[a].at[me] if scatter[a] else ins[a], outs[a].at[me], loc_sems.at[a])
            loc.start()
            copies.append(loc)
        for cp in copies:
            cp.wait()

    any_spec = pl.BlockSpec(memory_space=pl.ANY)
    return _call(
        body, name=name, out_shape=tuple(out_shape),
        in_specs=[any_spec] * n, out_specs=tuple([any_spec] * n),
        scratch_shapes=[pltpu.SemaphoreType.DMA((n * (NDEV - 1),)), pltpu.SemaphoreType.DMA((n * (NDEV - 1),)),
                        pltpu.SemaphoreType.DMA((n,))],
    )(*arrs)


NCHIP = 4


def _gather_two_level(arrs, name):
    n = len(arrs)
    per = NDEV - 1

    def body(*refs):
        ins, outs = refs[:n], refs[n:2 * n]
        send_sems, recv_sems, loc_sems = refs[2 * n:]
        x, y, c = lax.axis_index("x"), lax.axis_index("y"), lax.axis_index("c")
        me, sibling = (x, y, c), (x, y, 1 - c)
        chips = [(1 - x, y), (x, 1 - y), (1 - x, 1 - y)]

        def slot(a, px, py, pc):
            return outs[a].at[4 * px + 2 * py + pc]

        def copy(a, k, block, to, src=None):
            return pltpu.make_async_remote_copy(
                src_ref=slot(a, *block) if src is None else src, dst_ref=slot(a, *block),
                send_sem=send_sems.at[a * per + k], recv_sem=recv_sems.at[a * per + k],
                device_id=to, device_id_type=pl.DeviceIdType.MESH)

        local, sent = [], []
        for a in range(n):
            mine = pltpu.make_async_copy(ins[a], slot(a, *me), loc_sems.at[a])
            mine.start()
            local.append(mine)
            first = [copy(a, 1 + j, me, (*chip, c), src=ins[a]) for j, chip in enumerate(chips)]
            first.append(copy(a, 0, me, sibling, src=ins[a]))
            for cp in first:
                cp.start()
            sent += first
        for j, chip in enumerate(chips):
            for a in range(n):
                copy(a, 1 + j, (*chip, c), me).wait_recv()
                cp = copy(a, 4 + j, (*chip, c), sibling)
                cp.start()
                sent.append(cp)
        for a in range(n):
            copy(a, 0, sibling, me).wait_recv()
            for j, chip in enumerate(chips):
                copy(a, 4 + j, (*chip, 1 - c), me).wait_recv()
        for cp in sent:
            cp.wait_send()
        for cp in local:
            cp.wait()

    any_spec = pl.BlockSpec(memory_space=pl.ANY)
    return _call(
        body, name=name, out_shape=tuple(jax.ShapeDtypeStruct((NDEV,) + a.shape, a.dtype) for a in arrs),
        in_specs=[any_spec] * n, out_specs=tuple([any_spec] * n),
        scratch_shapes=[pltpu.SemaphoreType.DMA((n * per,)), pltpu.SemaphoreType.DMA((n * per,)),
                        pltpu.SemaphoreType.DMA((n,))],
    )(*arrs)


def _pair_add(arr4, got, name):
    _, _, r, n = arr4.shape
    tr, tc = _shard_tile(r, n)

    def body(a_ref, g_ref, p_ref, own_ref):
        c = lax.axis_index("c")
        my_chip = 2 * lax.axis_index("x") + lax.axis_index("y")
        s = jnp.where(c == 0, a_ref[0, 0], a_ref[0, 1]) + g_ref[0]
        p_ref[0] = s.astype(BF16)

        @pl.when(pl.program_id(2) == my_chip)
        def _():
            own_ref[...] = s

    return _call(
        body, name=name, grid=(r // tr, n // tc, NCHIP),
        in_specs=[pl.BlockSpec((1, 2, tr, tc), lambda i, k, j: (j, 0, i, k)),
                  pl.BlockSpec((1, tr, tc), lambda i, k, j: (j, i, k))],
        out_specs=(pl.BlockSpec((1, tr, tc), lambda i, k, j: (j, i, k)), pl.BlockSpec((tr, tc), lambda i, k, j: (i, k))),
        out_shape=(jax.ShapeDtypeStruct((NCHIP, r, n), BF16), jax.ShapeDtypeStruct((r, n), F32)),
        compiler_params=_cp(("parallel", "parallel", "arbitrary")),
    )(arr4, got)


def _shard_tile(r, n):
    return (128, n) if r % 128 == 0 else (r, 256)


def _all_copies(srcs, lands, send_sems, recv_sems):
    x, y, c = lax.axis_index("x"), lax.axis_index("y"), lax.axis_index("c")
    per = NDEV - 1
    copies = []
    for a in range(len(srcs)):
        for k in range(1, NDEV):
            px = 1 - x if (k >> 2) & 1 else x
            py = 1 - y if (k >> 1) & 1 else y
            pc = 1 - c if k & 1 else c
            copies.append(pltpu.make_async_remote_copy(
                src_ref=srcs[a], dst_ref=lands[a].at[4 * x + 2 * y + c],
                send_sem=send_sems.at[a * per + k - 1], recv_sem=recv_sems.at[a * per + k - 1],
                device_id=(px, py, pc), device_id_type=pl.DeviceIdType.MESH))
    return copies


def _split_start(make_copies, peers, name, arrs, lands, after):
    n = len(arrs)
    nsem = n * peers

    def body(*refs):
        srcs, land_in = refs[:n], refs[n:2 * n]
        send_sems, recv_sems = refs[2 * n + 1:2 * n + 3]
        token = refs[-1]
        for cp in make_copies(srcs, land_in, send_sems, recv_sems):
            cp.start()
        token[...] = jnp.zeros_like(token)

    hbm = pl.BlockSpec(memory_space=pltpu.HBM)
    sem = pl.BlockSpec(memory_space=pltpu.SEMAPHORE)
    both = list(arrs) + list(lands)
    outs = _call(
        body, name=name,
        out_shape=(pltpu.SemaphoreType.DMA((nsem,)), pltpu.SemaphoreType.DMA((nsem,)),
                   *[pltpu.HBM(a.shape, a.dtype) for a in both], jax.ShapeDtypeStruct((SUBLANES, LANE), F32)),
        in_specs=[hbm] * (2 * n) + [pl.BlockSpec(memory_space=pl.ANY)],
        out_specs=(sem, sem, *[hbm] * (2 * n), pl.BlockSpec(memory_space=pltpu.VMEM)),
        input_output_aliases={i: i + 2 for i in range(2 * n)},
        compiler_params=pltpu.CompilerParams(has_side_effects=pltpu.SideEffectType.DATAFLOW_SIDE_EFFECTING),
    )(*[pltpu.with_memory_space_constraint(t, pltpu.HBM) for t in both], after)
    return outs[0], outs[1], outs[2:2 + n], outs[2 + n:2 + 2 * n], outs[-1]


def _split_wait(make_copies, name, send_sems, recv_sems, arrs, lands, after):
    n = len(arrs)

    def body(*refs):
        srcs, land_in = refs[:n], refs[n:2 * n]
        send, recv = refs[2 * n], refs[2 * n + 1]
        for cp in make_copies(srcs, land_in, send, recv):
            cp.wait_send()
            cp.wait_recv()

    hbm = pl.BlockSpec(memory_space=pltpu.HBM)
    sem = pl.BlockSpec(memory_space=pltpu.SEMAPHORE)
    both = list(arrs) + list(lands)
    outs = _call(
        body, name=name,
        out_shape=tuple(pltpu.HBM(a.shape, a.dtype) for a in both),
        in_specs=[hbm] * (2 * n) + [sem, sem, pl.BlockSpec(memory_space=pl.ANY)], out_specs=tuple([hbm] * (2 * n)),
        input_output_aliases={i: i for i in range(2 * n)},
        compiler_params=pltpu.CompilerParams(has_side_effects=pltpu.SideEffectType.DATAFLOW_SIDE_EFFECTING),
    )(*both, send_sems, recv_sems, after)
    return outs[:n], outs[n:]


def _sibling_copies(srcs, lands, send_sems, recv_sems):
    x, y, c = lax.axis_index("x"), lax.axis_index("y"), lax.axis_index("c")
    copies = []
    for a in range(len(srcs)):
        for j in range(NCHIP):
            copies.append(pltpu.make_async_remote_copy(
                src_ref=srcs[a].at[j, 1 - c], dst_ref=lands[a].at[j],
                send_sem=send_sems.at[a * NCHIP + j], recv_sem=recv_sems.at[a * NCHIP + j],
                device_id=(x, y, 1 - c), device_id_type=pl.DeviceIdType.MESH))
    return copies


def _chip_copies(srcs, lands, send_sems, recv_sems):
    x, y, c = lax.axis_index("x"), lax.axis_index("y"), lax.axis_index("c")
    per = NCHIP - 1
    copies = []
    for a in range(len(srcs)):
        for k in range(1, NCHIP):
            px = 1 - x if (k >> 1) & 1 else x
            py = 1 - y if k & 1 else y
            copies.append(pltpu.make_async_remote_copy(
                src_ref=srcs[a].at[2 * px + py], dst_ref=lands[a].at[2 * x + y],
                send_sem=send_sems.at[a * per + k - 1], recv_sem=recv_sems.at[a * per + k - 1],
                device_id=(px, py, c), device_id_type=pl.DeviceIdType.MESH))
    return copies


def _mm_tn(a, b, name):
    t, m = a.shape
    n = b.shape[1]
    tt = _pick(t, (1664, 640, 128))
    tm = _pick(m, (1024, 512, 128))
    tn = _pick(n, (1152, 1024, 512, 128))
    nt = t // tt

    def body(a_ref, b_ref, o_ref):
        s = pl.program_id(2)
        part = _dot_tn(a_ref[...], b_ref[...])

        @pl.when(s == 0)
        def _():
            o_ref[...] = part

        @pl.when(s > 0)
        def _():
            o_ref[...] += part

    return _call(
        body, name=name, grid=(m // tm, n // tn, nt),
        in_specs=[pl.BlockSpec((tt, tm), lambda i, j, s: (s, i)), pl.BlockSpec((tt, tn), lambda i, j, s: (s, j))],
        out_specs=pl.BlockSpec((tm, tn), lambda i, j, s: (i, j)),
        out_shape=jax.ShapeDtypeStruct((m, n), F32),
        compiler_params=_cp(("parallel", "parallel", "arbitrary")),
    )(a, b)


def _proj_fwd(x_ext, norm_w, w_main_t, w_ba_t):
    lp = x_ext.shape[0]
    n = w_main_t.shape[0]
    tm = _pick(lp, (832, 640, 320))
    tn = 1024

    def body(x_ref, nw_ref, w_ref, wba_ref, proj_ref, ba_ref, h_ref):
        @pl.when(pl.program_id(1) == 0)
        def _():
            x = x_ref[...]
            r = lax.rsqrt(jnp.mean(x * x, axis=-1, keepdims=True) + EPS)
            h = (x * r * nw_ref[...]).astype(BF16)
            h_ref[...] = h
            ba_ref[...] = _dot_nt(h, wba_ref[...])

        proj_ref[...] = _dot_nt(h_ref[...], w_ref[...])

    return _call(
        body, name="proj_fwd", grid=(lp // tm, n // tn),
        in_specs=[pl.BlockSpec((tm, D), lambda i, j: (i, 0)), pl.BlockSpec((1, D), lambda i, j: (0, 0)),
                  pl.BlockSpec((tn, D), lambda i, j: (j, 0)), pl.BlockSpec((LANE, D), lambda i, j: (0, 0))],
        out_specs=(pl.BlockSpec((tm, tn), lambda i, j: (i, j)), pl.BlockSpec((tm, LANE), lambda i, j: (i, 0)),
                   pl.BlockSpec((tm, D), lambda i, j: (i, 0))),
        out_shape=(jax.ShapeDtypeStruct((lp, n), F32), jax.ShapeDtypeStruct((lp, LANE), F32),
                   jax.ShapeDtypeStruct((lp, D), BF16)),
        compiler_params=_cp(("parallel", "arbitrary")),
    )(x_ext, norm_w, w_main_t, w_ba_t)


def _dh_mm(dproj, dba, w_main_t, w_ba_t, part, dh_so_far=None):
    lp, n = dproj.shape
    tm = _pick(lp, (832, 640, 320))
    tn = 1024
    tk = 2304
    nk = n // tk
    tiles = lp // tm
    first = (tiles + 1) // 2
    t0, nt = (0, first) if part == 0 else (first, tiles - first)
    if nt == 0:
        return dh_so_far

    def body(a_ref, ba_ref, b_ref, bba_ref, *rest):
        o_ref, acc = rest[-2:]
        kk = pl.program_id(2)

        @pl.when(kk == 0)
        def _():
            acc[...] = jnp.dot(ba_ref[...], bba_ref[...], preferred_element_type=F32)

        acc[...] += jnp.dot(a_ref[...], b_ref[...], preferred_element_type=F32)

        @pl.when(kk == nk - 1)
        def _():
            o_ref[...] = acc[...]

    prev = [] if dh_so_far is None else [dh_so_far]
    return _call(
        body, name=f"dh_mm_{part}", grid=(nt, D // tn, nk),
        in_specs=[pl.BlockSpec((tm, tk), lambda i, j, kk: (i + t0, kk)),
                  pl.BlockSpec((tm, LANE), lambda i, j, kk: (i + t0, 0)),
                  pl.BlockSpec((tk, tn), lambda i, j, kk: (kk, j)), pl.BlockSpec((LANE, tn), lambda i, j, kk: (0, j))]
                 + [pl.BlockSpec(memory_space=pl.ANY)] * len(prev),
        out_specs=pl.BlockSpec((tm, tn), lambda i, j, kk: (i + t0, j)),
        out_shape=jax.ShapeDtypeStruct((lp, D), F32),
        input_output_aliases={4: 0} if prev else {},
        scratch_shapes=[pltpu.VMEM((tm, tn), F32)],
        compiler_params=_cp(("parallel", "parallel", "arbitrary")),
    )(dproj, dba, w_main_t, w_ba_t, *prev)


def _beta_g(ba, ab, row0, pad):
    lane = lax.broadcasted_iota(jnp.int32, ba.shape, 1)
    rows = row0 + lax.broadcasted_iota(jnp.int32, ba.shape, 0)
    z = ba + ab[1:2, :]
    sp = jnp.maximum(z, 0.0) + jnp.log(1.0 + jnp.exp(-jnp.abs(z)))
    val = jnp.where(lane < H, _sig(ba), -jnp.exp(ab[0:1, :]) * sp)
    return jnp.where((lane < 2 * H) & (rows >= pad), val, 0.0)


def _qkv_conv_fwd(proj, ba, conv_w, ab, pad):
    lp = proj.shape[0]
    te = _pick(lp, (640, 320))
    hb = te // HALO_Q

    def body(main_ref, halo_ref, cw_ref, ba_ref, ab_ref, out_ref, bg_ref, pre_scr, tap_scr):
        i, s = pl.program_id(0), pl.program_id(1)
        pre_scr[:HALO_Q, :] = jnp.where(i > 0, halo_ref[...], 0.0)
        pre_scr[HALO_Q:, :] = main_ref[...]
        scale = jnp.where(s == 0, DK ** -0.5, 1.0)
        off = HALO_Q - (KQ - 1)

        def head(h, carry):
            cs = pl.ds(pl.multiple_of(h * DK, DK), DK)
            for j in range(KQ - 1):
                tap_scr[j] = pre_scr[off + j:off + j + te, cs]
            co = cw_ref[KQ - 1:KQ, cs] * pre_scr[HALO_Q:, cs]
            for j in range(KQ - 1):
                co = co + cw_ref[j:j + 1, cs] * tap_scr[j]
            a = co * _sig(co)
            r = lax.rsqrt(_rowsum(a * a) + EPS)
            out_ref[:, cs] = jnp.where(s == 2, a, a * (r * scale))
            return carry

        lax.fori_loop(0, H, head, 0, unroll=True)

        @pl.when(s == 0)
        def _():
            bg_ref[...] = _beta_g(ba_ref[...], ab_ref[...], i * te, pad)

    return _call(
        body, name="qkv_conv_fwd", grid=(lp // te, 3),
        in_specs=[pl.BlockSpec((te, D), lambda i, s: (i, s)),
                  pl.BlockSpec((HALO_Q, D), lambda i, s: (jnp.maximum(i * hb - 1, 0), s)),
                  pl.BlockSpec((KQ, D), lambda i, s: (0, s)),
                  pl.BlockSpec((te, LANE), lambda i, s: (i, 0)),
                  pl.BlockSpec((2, LANE), lambda i, s: (0, 0))],
        out_specs=(pl.BlockSpec((te, D), lambda i, s: (i, s)), pl.BlockSpec((te, LANE), lambda i, s: (i, 0))),
        out_shape=(jax.ShapeDtypeStruct((lp, 3 * D), F32), jax.ShapeDtypeStruct((lp, LANE), F32)),
        scratch_shapes=[pltpu.VMEM((te + HALO_Q, D), F32), pltpu.VMEM((KQ - 1, te, DK), F32)],
        compiler_params=_cp(("parallel", "arbitrary")),
    )(proj, proj, conv_w, ba, ab)


def _tri_masks():
    row = lax.broadcasted_iota(jnp.int32, (C, C), 0)
    col = lax.broadcasted_iota(jnp.int32, (C, C), 1)
    return row, col


def _split(a):
    hi = a.astype(BF16)
    return hi, (a - hi.astype(F32)).astype(BF16)


def _dot3(a, b, dims=(((1,), (0,)), ((), ()))):
    (ah, al), (bh, bl) = a, b
    mm = lambda x, y: lax.dot_general(x, y, dims, preferred_element_type=F32)
    return mm(ah, bh) + (mm(ah, bl) + mm(al, bh))


CHUNKS_PER_STEP = 5
CHUNKS_PER_STEP_BWD = 1
TINV_BLOCK = 16


def _tinv(ns, row, col):
    eye = (row == col).astype(F32)
    sh = TINV_BLOCK.bit_length() - 1
    same16 = (row >> sh) == (col >> sh)
    same32 = (row >> (sh + 1)) == (col >> (sh + 1))
    ys = [jnp.where(same16, -n, 0.0) for n in ns]
    ts = [eye + y for y in ys]
    sp = [_split(y) for y in ys]
    for level in range(3):
        yks = [_dot3(s, s) for s in sp]
        sp = [_split(yk) for yk in yks]
        ts = [t + _dot3(s, _split(t)) for s, t in zip(sp, ts)]
    for mask in (same32 & ~same16, ~same32):
        tsp = [_split(t) for t in ts]
        inner = [_dot3(_split(jnp.where(mask, n, 0.0)), t) for n, t in zip(ns, tsp)]
        ts = [t - _dot3(tp, _split(a)) for t, tp, a in zip(ts, tsp, inner)]
    return ts


def _chunk_common(q, k, v, bcol, gcc, gcr, incl, strict):
    dm = jnp.where(incl, jnp.exp(gcc - gcr), 0.0)
    kk = _dot_nt(k, k)
    qk = _dot_nt(q, k)
    gccw = jnp.broadcast_to(gcc, (C, DK))
    egc = jnp.exp(gccw)
    glast = gccw[C - 1:C, :]
    eend = jnp.exp(glast - gccw)
    elast = jnp.exp(glast)
    rhs = jnp.concatenate([v * bcol, k * (bcol * egc)], axis=1)
    return dm, kk, qk, egc, eend, elast, rhs


def _delta_fwd(qkv, bg):
    lp = qkv.shape[0]
    nc = lp // C
    heads = range(H)
    sls = [slice(h * DK, (h + 1) * DK) for h in heads]

    def body(q_ref, k_ref, v_ref, bg_ref, o_ref, sall_ref, tall_ref, s_scr):
        @pl.when(pl.program_id(0) == 0)
        def _():
            s_scr[...] = jnp.zeros_like(s_scr)

        row, col = _tri_masks()
        incl, strict = row >= col, row > col

        def prepare(sub):
            rs = slice(sub * C, (sub + 1) * C)
            bgt = bg_ref[rs, :]
            gc_all = _dot_hi(incl.astype(F32), bgt)
            gc_t = _dot_hi(bgt.T, (row <= col).astype(F32))
            qs, ks, vs = ([r[rs, sl] for sl in sls] for r in (q_ref, k_ref, v_ref))
            bcols = [jnp.broadcast_to(bgt[:, h:h + 1], (C, DK)) for h in heads]
            cm = [_chunk_common(qs[h], ks[h], vs[h], bcols[h], gc_all[:, H + h:H + h + 1], gc_t[H + h:H + h + 1, :],
                                incl, strict) for h in heads]
            dms, kks, qks, egcs, eends, elasts, rhss = zip(*cm)
            ts = _tinv([jnp.where(strict, bcols[h][:, :C] * kks[h] * dms[h], 0.0) for h in heads], row, col)
            sols = [_dot3(_split(ts[h]), _split(rhss[h])) for h in heads]
            qgs = [(qs[h] * egcs[h]).astype(BF16) for h in heads]
            ps = [(qks[h] * dms[h]).astype(BF16) for h in heads]
            kends = [(ks[h] * eends[h]).astype(BF16) for h in heads]
            return ts, sols, qgs, ps, kends, elasts

        prepared = [prepare(sub) for sub in range(CHUNKS_PER_STEP)]
        ss = [s_scr[h] for h in heads]
        for sub in range(CHUNKS_PER_STEP):
            rs = slice(sub * C, (sub + 1) * C)
            ts, sols, qgs, ps, kends, elasts = prepared[sub]
            sb = [s.astype(BF16) for s in ss]
            wvb = [(sols[h][:, :DK] - _dot(sols[h][:, DK:], sb[h])).astype(BF16) for h in heads]
            for h in heads:
                o_ref[rs, sls[h]] = _dot(qgs[h], sb[h]) + _dot(ps[h], wvb[h])
                sall_ref[sub, h] = ss[h]
                tall_ref[sub, h] = ts[h]
            ss = [ss[h] * elasts[h] + _dot_tn(kends[h], wvb[h]) for h in heads]
        for h in heads:
            s_scr[h] = ss[h]

    rows = CHUNKS_PER_STEP * C
    blk = lambda j: pl.BlockSpec((rows, D), lambda n: (n, j))
    return _call(
        body, name="delta_fwd", grid=(nc // CHUNKS_PER_STEP,),
        in_specs=[blk(0), blk(1), blk(2), pl.BlockSpec((rows, LANE), lambda n: (n, 0))],
        out_specs=(pl.BlockSpec((rows, D), lambda n: (n, 0)),
                   pl.BlockSpec((CHUNKS_PER_STEP, H, DK, DK), lambda n: (n, 0, 0, 0)),
                   pl.BlockSpec((CHUNKS_PER_STEP, H, C, C), lambda n: (n, 0, 0, 0))),
        out_shape=(jax.ShapeDtypeStruct((lp, D), F32), jax.ShapeDtypeStruct((nc, H, DK, DK), F32),
                   jax.ShapeDtypeStruct((nc, H, C, C), F32)),
        scratch_shapes=[pltpu.VMEM((H, DK, DK), F32)],
        compiler_params=_cp(("arbitrary",)),
    )(qkv, qkv, qkv, bg)


def _delta_bwd(qkv, bg, sall, tall, do):
    lp = qkv.shape[0]
    nc = lp // C

    heads = range(H)
    sls = [slice(h * DK, (h + 1) * DK) for h in heads]

    def body(q_ref, k_ref, v_ref, bg_ref, sall_ref, tall_ref, do_ref, dqkv_ref, dbg_ref, ds_scr):
        @pl.when(pl.program_id(0) == 0)
        def _():
            ds_scr[...] = jnp.zeros_like(ds_scr)

        dsns = [ds_scr[h] for h in heads]
        for sub in reversed(range(CHUNKS_PER_STEP_BWD)):
            dsns = chunk(sub, dsns, q_ref, k_ref, v_ref, bg_ref, sall_ref, tall_ref, do_ref, dqkv_ref, dbg_ref)
        for h in heads:
            ds_scr[h] = dsns[h]

    def chunk(sub, dsns, q_ref, k_ref, v_ref, bg_ref, sall_ref, tall_ref, do_ref, dqkv_ref, dbg_ref):
        rs = slice(sub * C, (sub + 1) * C)
        bgt = bg_ref[rs, :]
        row, col = _tri_masks()
        incl, strict = row >= col, row > col
        upper = (row <= col).astype(F32)
        gc_all = _dot_hi(incl.astype(F32), bgt)
        gc_t = _dot_hi(bgt.T, upper)
        lane = lax.broadcasted_iota(jnp.int32, (C, LANE), 1)
        lastrow = lax.broadcasted_iota(jnp.int32, (C, 1), 0) == C - 1
        qs, ks, vs, dos = ([r[rs, sl] for sl in sls] for r in (q_ref, k_ref, v_ref, do_ref))
        bcols = [jnp.broadcast_to(bgt[:, h:h + 1], (C, DK)) for h in heads]
        cm = [_chunk_common(qs[h], ks[h], vs[h], bcols[h], gc_all[:, H + h:H + h + 1], gc_t[H + h:H + h + 1, :],
                            incl, strict) for h in heads]
        dms, kks, qks, egcs, eends, elasts, rhss = zip(*cm)
        ss = [sall_ref[sub, h] for h in heads]
        ts = [tall_ref[sub, h] for h in heads]
        sb = [s.astype(BF16) for s in ss]
        dsb = [d.astype(BF16) for d in dsns]
        dob = [d.astype(BF16) for d in dos]
        sols = [_dot3(_split(ts[h]), _split(rhss[h])) for h in heads]
        ws = [sol[:, DK:] for sol in sols]
        qgs = [qs[h] * egcs[h] for h in heads]
        kends = [ks[h] * eends[h] for h in heads]
        wvs = [sols[h][:, :DK] - _dot(ws[h], sb[h]) for h in heads]
        wvb = [wv.astype(BF16) for wv in wvs]
        dwvs = [_dot_tn(qks[h] * dms[h], dob[h]) + _dot(kends[h], dsb[h]) for h in heads]
        dps = [jnp.where(incl, _dot_nt(dob[h], wvb[h]), 0.0) for h in heads]
        dqgs = [_dot_nt(dob[h], sb[h]) for h in heads]
        dkends = [_dot_nt(wvb[h], dsb[h]) for h in heads]
        ds_before = [_dot_tn(qgs[h], dob[h]) + elasts[h] * dsns[h] - _dot_tn(ws[h], dwvs[h]) for h in heads]
        dglasts = [elasts[h] * jnp.sum(ss[h] * dsns[h], keepdims=True) for h in heads]
        dws = [-_dot_nt(dwvs[h], sb[h]) for h in heads]
        tts = [_split(ts[h].T) for h in heads]
        drhss = [_dot3(tts[h], _split(jnp.concatenate([dwvs[h], dws[h]], axis=1))) for h in heads]
        nt_dims = (((1,), (1,)), ((), ()))
        dns = [jnp.where(strict, -_dot3(_split(drhss[h]), _split(sols[h]), nt_dims), 0.0) for h in heads]
        dbeta_t = jnp.zeros((C, LANE), F32)
        dgc_t = jnp.zeros((C, LANE), F32)
        for h in heads:
            q, k, v, bcol, dm, kk, qk, egc, eend = qs[h], ks[h], vs[h], bcols[h], dms[h], kks[h], qks[h], egcs[h], eends[h]
            drv, drk = drhss[h][:, :DK], drhss[h][:, DK:]
            dn, dp, dqg, dkend = dns[h], dps[h], dqgs[h], dkends[h]
            rk = _rowsum(drk * k)
            dkk = dn * (bcol[:, :C] * dm)
            dqk = dp * dm
            e = (dn * (bcol[:, :C] * kk) + dp * qk) * dm
            tk = _rowsum(dkend * kends[h])
            dgc = rk * bcol * egc + _rowsum(e) - _rowsum(e.T) + _rowsum(dqg * qgs[h]) - tk
            dgc = dgc + jnp.where(lastrow, dglasts[h] + jnp.sum(tk, keepdims=True), 0.0)
            dbeta = _rowsum(drv * v) + rk * egc + _rowsum(dn * kk * dm)
            dqkv_ref[rs, sls[h]] = _dot(dqk, k) + dqg * egc
            dqkv_ref[rs, D + h * DK:D + (h + 1) * DK] = (drk * (bcol * egc) + _dot(dkk, k) + _dot_tn(dkk, k)
                                                        + _dot_tn(dqk, q) + dkend * eend)
            dqkv_ref[rs, 2 * D + h * DK:2 * D + (h + 1) * DK] = bcol * drv
            dbeta_t = jnp.where(lane == h, dbeta, dbeta_t)
            dgc_t = jnp.where(lane == H + h, dgc, dgc_t)
        dbg_ref[rs, :] = dbeta_t + _dot_hi(upper, dgc_t)
        return ds_before

    steps = nc // CHUNKS_PER_STEP_BWD
    rows = CHUNKS_PER_STEP_BWD * C
    rev = lambda n: steps - 1 - n
    blk = lambda j: pl.BlockSpec((rows, D), lambda n: (rev(n), j))
    return _call(
        body, name="delta_bwd", grid=(steps,),
        in_specs=[blk(0), blk(1), blk(2), pl.BlockSpec((rows, LANE), lambda n: (rev(n), 0)),
                  pl.BlockSpec((CHUNKS_PER_STEP_BWD, H, DK, DK), lambda n: (rev(n), 0, 0, 0)),
                  pl.BlockSpec((CHUNKS_PER_STEP_BWD, H, C, C), lambda n: (rev(n), 0, 0, 0)),
                  pl.BlockSpec((rows, D), lambda n: (rev(n), 0))],
        out_specs=(pl.BlockSpec((rows, 3 * D), lambda n: (rev(n), 0)),
                   pl.BlockSpec((rows, LANE), lambda n: (rev(n), 0))),
        out_shape=(jax.ShapeDtypeStruct((lp, 3 * D), F32), jax.ShapeDtypeStruct((lp, LANE), F32)),
        scratch_shapes=[pltpu.VMEM((H, DK, DK), F32)],
        compiler_params=_cp(("arbitrary",)),
    )(qkv, qkv, qkv, bg, sall, tall, do)


def _o_post_fwd(o, proj, dn_w, w_dn):
    lp = o.shape[0]
    te = _pick(lp, (640, 320))

    def body(o_ref, za_ref, w_ref, wdn_ref, out_ref, ya_ref):
        za = za_ref[...]
        gate = za * _sig(za)
        for h in range(H):
            sl = slice(h * DK, (h + 1) * DK)
            oh = o_ref[:, sl]
            r = lax.rsqrt(jnp.mean(oh * oh, axis=-1, keepdims=True) + EPS)
            out_ref[:, sl] = (oh * r * w_ref[...] * gate[:, sl]).astype(BF16)
        ya_ref[...] = _dot(out_ref[...], wdn_ref[...])

    row = pl.BlockSpec((te, D), lambda i: (i, 0))
    return _call(
        body, name="o_post_fwd", grid=(lp // te,),
        in_specs=[row, pl.BlockSpec((te, D), lambda i: (i, CB_ZA)), pl.BlockSpec((1, DK), lambda i: (0, 0)),
                  pl.BlockSpec((D, D), lambda i: (0, 0))],
        out_specs=(row, row),
        out_shape=(jax.ShapeDtypeStruct((lp, D), BF16), jax.ShapeDtypeStruct((lp, D), F32)),
        compiler_params=_cp(("parallel",)),
    )(o, proj, dn_w, w_dn)


def _o_post_bwd(dy_a, w_dn, o, proj, dn_w, dproj):
    lp = o.shape[0]
    te = _pick(lp, (320,))

    def body(dya_ref, wdn_ref, o_ref, za_ref, w_ref, _, do_ref, dza_ref, dw_ref, don_ref):
        @pl.when(pl.program_id(0) == 0)
        def _():
            dw_ref[...] = jnp.zeros_like(dw_ref)

        don_ref[...] = _dot_nt(dya_ref[...], wdn_ref[...])
        za = za_ref[...]
        sz = _sig(za)
        gate, dgate = za * sz, _dsilu(za, sz)
        w = w_ref[...]
        dw = jnp.zeros((1, DK), F32)
        for h in range(H):
            sl = slice(h * DK, (h + 1) * DK)
            oh, g = o_ref[:, sl], don_ref[:, sl]
            r = lax.rsqrt(jnp.mean(oh * oh, axis=-1, keepdims=True) + EPS)
            ohat = oh * r
            dza_ref[:, sl] = (g * ohat * w * dgate[:, sl]).astype(BF16)
            don = g * gate[:, sl]
            dw = dw + _colsum(don * ohat)
            dohat = don * w
            do_ref[:, sl] = r * (dohat - ohat * jnp.mean(dohat * ohat, axis=-1, keepdims=True))
        dw_ref[...] += dw

    return _call(
        body, name="o_post_bwd", grid=(lp // te,),
        in_specs=[pl.BlockSpec((te, D), lambda i: (i, 0)), pl.BlockSpec((D, D), lambda i: (0, 0)),
                  pl.BlockSpec((te, D), lambda i: (i, 0)),
                  pl.BlockSpec((te, D), lambda i: (i, CB_ZA)), pl.BlockSpec((1, DK), lambda i: (0, 0)),
                  pl.BlockSpec(memory_space=pl.ANY)],
        out_specs=(pl.BlockSpec((te, D), lambda i: (i, 0)), pl.BlockSpec((te, D), lambda i: (i, CB_ZA)),
                   pl.BlockSpec((1, DK), lambda i: (0, 0))),
        out_shape=(jax.ShapeDtypeStruct((lp, D), F32), jax.ShapeDtypeStruct(dproj.shape, dproj.dtype),
                   jax.ShapeDtypeStruct((1, DK), F32)),
        input_output_aliases={5: 1},
        scratch_shapes=[pltpu.VMEM((te, D), F32)],
        compiler_params=_cp(("arbitrary",)),
    )(dy_a, w_dn, o, proj, dn_w, dproj)


def _qkv_conv_bwd(proj, dqkv, conv_w, dproj):
    lp = proj.shape[0]
    te = _pick(lp, (640, 320))
    hb = te // HALO_Q
    nt = lp // te
    last_hb = lp // HALO_Q - 1

    def body(main_ref, prev_ref, next_ref, dmain_ref, dnext_ref, cw_ref, _, dpre_ref, dcw_ref, pre_scr, dn_scr,
             tap_scr, dco_scr, dsh_scr):
        s, i = pl.program_id(0), pl.program_id(1)

        @pl.when(i == 0)
        def _():
            dcw_ref[...] = jnp.zeros_like(dcw_ref)

        ne = te + HALO_Q
        pre_scr[:HALO_Q, :] = jnp.where(i > 0, prev_ref[...], 0.0)
        pre_scr[HALO_Q:ne, :] = main_ref[...]
        pre_scr[ne:, :] = jnp.where(i < nt - 1, next_ref[...], 0.0)
        dn_scr[:te, :] = dmain_ref[...]
        dn_scr[te:, :] = jnp.where(i < nt - 1, dnext_ref[...], 0.0)
        scale = jnp.where(s == 0, DK ** -0.5, 1.0)
        off = HALO_Q - (KQ - 1)

        def head(h, carry):
            cs = pl.ds(pl.multiple_of(h * DK, DK), DK)
            for j in range(KQ - 1):
                tap_scr[j] = pre_scr[off + j:off + j + ne, cs]
            taps = [tap_scr[j] for j in range(KQ - 1)] + [pre_scr[HALO_Q:, cs]]
            co = cw_ref[0:1, cs] * taps[0]
            for j in range(1, KQ):
                co = co + cw_ref[j:j + 1, cs] * taps[j]
            sg = _sig(co)
            a = co * sg
            g = dn_scr[:, cs]
            r = lax.rsqrt(_rowsum(a * a) + EPS)
            yhat = a * r
            da = jnp.where(s == 2, g, (scale * r) * (g - yhat * _rowsum(g * yhat)))
            dco = da * _dsilu(co, sg)
            dco_scr[...] = dco
            for j in range(KQ - 1):
                dsh_scr[j] = dco_scr[KQ - 1 - j:KQ - 1 - j + te, :]
            dpre = cw_ref[KQ - 1:KQ, cs] * dco[:te, :]
            for j in range(KQ - 1):
                dpre = dpre + cw_ref[j:j + 1, cs] * dsh_scr[j]
            dpre_ref[:, cs] = dpre.astype(BF16)
            dcw_ref[:, cs] += jnp.concatenate([_colsum(dco[:te] * taps[j][:te]) for j in range(KQ)], axis=0)
            return carry

        lax.fori_loop(0, H, head, 0, unroll=True)

    return _call(
        body, name="qkv_conv_bwd", grid=(3, nt),
        in_specs=[pl.BlockSpec((te, D), lambda s, i: (i, s)),
                  pl.BlockSpec((HALO_Q, D), lambda s, i: (jnp.maximum(i * hb - 1, 0), s)),
                  pl.BlockSpec((HALO_Q, D), lambda s, i: (jnp.minimum((i + 1) * hb, last_hb), s)),
                  pl.BlockSpec((te, D), lambda s, i: (i, s)),
                  pl.BlockSpec((HALO_Q, D), lambda s, i: (jnp.minimum((i + 1) * hb, last_hb), s)),
                  pl.BlockSpec((KQ, D), lambda s, i: (0, s)),
                  pl.BlockSpec(memory_space=pl.ANY)],
        out_specs=(pl.BlockSpec((te, D), lambda s, i: (i, s)), pl.BlockSpec((KQ, D), lambda s, i: (0, s))),
        out_shape=(jax.ShapeDtypeStruct(dproj.shape, dproj.dtype), jax.ShapeDtypeStruct((KQ, 3 * D), F32)),
        input_output_aliases={6: 0},
        scratch_shapes=[pltpu.VMEM((te + 2 * HALO_Q, D), F32), pltpu.VMEM((te + HALO_Q, D), F32),
                        pltpu.VMEM((KQ - 1, te + HALO_Q, DK), F32), pltpu.VMEM((te + HALO_Q, DK), F32),
                        pltpu.VMEM((KQ - 1, te, DK), F32)],
        compiler_params=_cp(("arbitrary", "arbitrary")),
    )(proj, proj, proj, dqkv, dqkv, conv_w, dproj)


def _ba_bwd(dbg, ba, ab, pad):
    lp = ba.shape[0]
    te = _pick(lp, (640, 320))

    def body(dbg_ref, ba_ref, ab_ref, dba_ref, dab_ref):
        i = pl.program_id(0)

        @pl.when(i == 0)
        def _():
            dab_ref[...] = jnp.zeros_like(dab_ref)

        ba, ab = ba_ref[...], ab_ref[...]
        lane = lax.broadcasted_iota(jnp.int32, ba.shape, 1)
        rows = i * te + lax.broadcasted_iota(jnp.int32, ba.shape, 0)
        g = jnp.where((lane < 2 * H) & (rows >= pad), dbg_ref[...], 0.0)
        sb = _sig(ba)
        z = ba + ab[1:2, :]
        sp = jnp.maximum(z, 0.0) + jnp.log(1.0 + jnp.exp(-jnp.abs(z)))
        nea = -jnp.exp(ab[0:1, :])
        dz = g * nea * _sig(z)
        dba_ref[...] = jnp.where(lane < H, g * sb * (1.0 - sb), dz).astype(BF16)
        is_g = (lane >= H) & (lane < 2 * H)
        dab_ref[...] += jnp.concatenate([_colsum(jnp.where(is_g, g * nea * sp, 0.0)),
                                         _colsum(jnp.where(is_g, dz, 0.0))], axis=0)

    return _call(
        body, name="ba_bwd", grid=(lp // te,),
        in_specs=[pl.BlockSpec((te, LANE), lambda i: (i, 0)), pl.BlockSpec((te, LANE), lambda i: (i, 0)),
                  pl.BlockSpec((2, LANE), lambda i: (0, 0))],
        out_specs=(pl.BlockSpec((te, LANE), lambda i: (i, 0)), pl.BlockSpec((2, LANE), lambda i: (0, 0))),
        out_shape=(jax.ShapeDtypeStruct((lp, LANE), BF16), jax.ShapeDtypeStruct((2, LANE), F32)),
        compiler_params=_cp(("arbitrary",)),
    )(dbg, ba, ab)


SUBLANES = 8
CONV_RB = 64


def _fill_shifted(sh_scr, src_scr, cs):
    n = sh_scr.shape[1]
    for s in range(1, SUBLANES):
        sh_scr[s] = src_scr[s:s + n, cs]


def _shifted(sh_scr, src_scr, cs, r, r0, n):
    s, a8 = r % SUBLANES, r - r % SUBLANES
    if s == 0:
        return src_scr[r0 + a8:r0 + a8 + n, cs]
    return sh_scr[s, r0 + a8:r0 + a8 + n, :]


def _conv_b_fwd(proj, dw_w, dw_b, ln_w, ln_b, w_cf):
    lp = proj.shape[0]
    te = _pick(lp, (320,))
    hb = te // HALO_D

    def body(a_ref, b_ref, ha_ref, hb_ref, zb_ref, w_ref, wb_ref, lw_ref, lb_ref, wcf_ref, c1_ref, c3_ref, yb_ref,
             c0_scr, sh_scr):
        i = pl.program_id(0)
        c0_scr[:HALO_D, :] = jnp.where(i > 0, ha_ref[...] * _sig(hb_ref[...]), 0.0)
        c0_scr[HALO_D:, :] = a_ref[...] * _sig(b_ref[...])
        off = HALO_D - (KD - 1)
        def lane_block(cb, carry):
            cs = pl.ds(pl.multiple_of(cb * LANE, LANE), LANE)
            _fill_shifted(sh_scr, c0_scr, cs)
            for r0 in range(0, te, CONV_RB):
                acc = None
                for j in range(KD):
                    term = w_ref[j:j + 1, cs] * _shifted(sh_scr, c0_scr, cs, off + j, r0, CONV_RB)
                    acc = term if acc is None else acc + term
                c1_ref[r0:r0 + CONV_RB, cs] = acc + wb_ref[:, cs]
            return carry

        lax.fori_loop(0, D // LANE, lane_block, 0)
        c1 = c1_ref[...]
        mu = jnp.mean(c1, axis=-1, keepdims=True)
        xc = c1 - mu
        c2 = xc * lax.rsqrt(jnp.mean(xc * xc, axis=-1, keepdims=True) + EPS) * lw_ref[...] + lb_ref[...]
        zb = zb_ref[...]
        c3 = (c2 * _sig(c2) * zb * _sig(zb)).astype(BF16)
        c3_ref[...] = c3
        yb_ref[...] = _dot(c3, wcf_ref[...])

    vec = pl.BlockSpec((1, D), lambda i: (0, 0))
    row = pl.BlockSpec((te, D), lambda i: (i, 0))
    return _call(
        body, name="conv_b_fwd", grid=(lp // te,),
        in_specs=[pl.BlockSpec((te, D), lambda i: (i, CB_GA_)), pl.BlockSpec((te, D), lambda i: (i, CB_GB_)),
                  pl.BlockSpec((HALO_D, D), lambda i: (jnp.maximum(i * hb - 1, 0), CB_GA_)),
                  pl.BlockSpec((HALO_D, D), lambda i: (jnp.maximum(i * hb - 1, 0), CB_GB_)),
                  pl.BlockSpec((te, D), lambda i: (i, CB_ZB)),
                  pl.BlockSpec((KD, D), lambda i: (0, 0)), vec, vec, vec, pl.BlockSpec((D, D), lambda i: (0, 0))],
        out_specs=(row, row, row),
        out_shape=(jax.ShapeDtypeStruct((lp, D), F32), jax.ShapeDtypeStruct((lp, D), BF16),
                   jax.ShapeDtypeStruct((lp, D), F32)),
        scratch_shapes=[pltpu.VMEM((te + HALO_D, D), F32), pltpu.VMEM((SUBLANES, te + HALO_D - SUBLANES, LANE), F32)],
        compiler_params=_cp(("parallel",)),
    )(proj, proj, proj, proj, proj, dw_w, dw_b, ln_w, ln_b, w_cf)


def _conv_b_bwd1(dy_b, w_cf, c1, proj, ln_w, ln_b, dproj):
    lp = c1.shape[0]
    te = _pick(lp, (320,))

    def body(dyb_ref, wcf_ref, c1_ref, zb_ref, lw_ref, lb_ref, _, dc1_ref, dzb_ref, sums_ref):
        @pl.when(pl.program_id(0) == 0)
        def _():
            sums_ref[...] = jnp.zeros_like(sums_ref)

        c1, g = c1_ref[...], _dot_nt(dyb_ref[...], wcf_ref[...])
        mu = jnp.mean(c1, axis=-1, keepdims=True)
        xc = c1 - mu
        rstd = lax.rsqrt(jnp.mean(xc * xc, axis=-1, keepdims=True) + EPS)
        xh = xc * rstd
        lw = lw_ref[...]
        c2 = xh * lw + lb_ref[...]
        s2 = _sig(c2)
        zb = zb_ref[...]
        sz = _sig(zb)
        dc2 = g * (zb * sz) * _dsilu(c2, s2)
        dzb_ref[...] = (g * (c2 * s2) * _dsilu(zb, sz)).astype(BF16)
        dxh = dc2 * lw
        dc1 = rstd * (dxh - jnp.mean(dxh, axis=-1, keepdims=True) - xh * jnp.mean(dxh * xh, axis=-1, keepdims=True))
        dc1_ref[...] = dc1
        sums_ref[...] += jnp.concatenate([_colsum(dc2 * xh), _colsum(dc2), _colsum(dc1)], axis=0)

    vec = pl.BlockSpec((1, D), lambda i: (0, 0))
    return _call(
        body, name="conv_b_bwd1", grid=(lp // te,),
        in_specs=[pl.BlockSpec((te, D), lambda i: (i, 0)), pl.BlockSpec((D, D), lambda i: (0, 0)),
                  pl.BlockSpec((te, D), lambda i: (i, 0)),
                  pl.BlockSpec((te, D), lambda i: (i, CB_ZB)), vec, vec, pl.BlockSpec(memory_space=pl.ANY)],
        out_specs=(pl.BlockSpec((te, D), lambda i: (i, 0)), pl.BlockSpec((te, D), lambda i: (i, CB_ZB)),
                   pl.BlockSpec((3, D), lambda i: (0, 0))),
        out_shape=(jax.ShapeDtypeStruct((lp, D), F32), jax.ShapeDtypeStruct(dproj.shape, dproj.dtype),
                   jax.ShapeDtypeStruct((3, D), F32)),
        input_output_aliases={6: 1},
        compiler_params=_cp(("arbitrary",)),
    )(dy_b, w_cf, c1, proj, ln_w, ln_b, dproj)


def _conv_b_bwd2(dc1, proj, dw_w, dproj):
    lp = dc1.shape[0]
    te = _pick(lp, (640, 320))
    hb = te // HALO_D
    nt = lp // te
    last_hb = lp // HALO_D - 1

    def body(g_ref, gn_ref, a_ref, b_ref, ha_ref, hb_ref, w_ref, _, dab_ref, dw_ref, c0_scr, g_scr, dc0_scr,
             csh_scr, gsh_scr):
        i = pl.program_id(0)

        @pl.when(i == 0)
        def _():
            dw_ref[...] = jnp.zeros_like(dw_ref)

        a, b = a_ref[...], b_ref[...]
        sb = _sig(b)
        c0_scr[:HALO_D, :] = jnp.where(i > 0, ha_ref[...] * _sig(hb_ref[...]), 0.0)
        c0_scr[HALO_D:, :] = a * sb
        g_scr[:te, :] = g_ref[...]
        g_scr[te:, :] = jnp.where(i < nt - 1, gn_ref[...], 0.0)
        off = HALO_D - (KD - 1)
        def lane_block(cb, carry):
            cs = pl.ds(pl.multiple_of(cb * LANE, LANE), LANE)
            _fill_shifted(csh_scr, c0_scr, cs)
            _fill_shifted(gsh_scr, g_scr, cs)
            for r0 in range(0, te, CONV_RB):
                acc = None
                for j in range(KD):
                    term = w_ref[j:j + 1, cs] * _shifted(gsh_scr, g_scr, cs, KD - 1 - j, r0, CONV_RB)
                    acc = term if acc is None else acc + term
                dc0_scr[r0:r0 + CONV_RB, cs] = acc
            parts = [None] * KD
            for r0 in range(0, te, CONV_RB):
                g = g_scr[r0:r0 + CONV_RB, cs].reshape(CONV_RB // SUBLANES, SUBLANES, LANE)
                for j in range(KD):
                    x = _shifted(csh_scr, c0_scr, cs, off + j, r0, CONV_RB)
                    p = jnp.sum(g * x.reshape(CONV_RB // SUBLANES, SUBLANES, LANE), axis=0)
                    parts[j] = p if parts[j] is None else parts[j] + p
            dw_ref[:, cs] += jnp.concatenate([_colsum(p) for p in parts], axis=0)
            return carry

        lax.fori_loop(0, D // LANE, lane_block, 0)
        dc0 = dc0_scr[...]
        dab_ref[:, :D] = (dc0 * sb).astype(BF16)
        dab_ref[:, D:] = (dc0 * a * sb * (1.0 - sb)).astype(BF16)

    return _call(
        body, name="conv_b_bwd2", grid=(nt,),
        in_specs=[pl.BlockSpec((te, D), lambda i: (i, 0)),
                  pl.BlockSpec((HALO_D, D), lambda i: (jnp.minimum((i + 1) * hb, last_hb), 0)),
                  pl.BlockSpec((te, D), lambda i: (i, CB_GA_)), pl.BlockSpec((te, D), lambda i: (i, CB_GB_)),
                  pl.BlockSpec((HALO_D, D), lambda i: (jnp.maximum(i * hb - 1, 0), CB_GA_)),
                  pl.BlockSpec((HALO_D, D), lambda i: (jnp.maximum(i * hb - 1, 0), CB_GB_)),
                  pl.BlockSpec((KD, D), lambda i: (0, 0)), pl.BlockSpec(memory_space=pl.ANY)],
        out_specs=(pl.BlockSpec((te, 2 * D), lambda i: (i, CB_GA_ // 2)), pl.BlockSpec((KD, D), lambda i: (0, 0))),
        out_shape=(jax.ShapeDtypeStruct(dproj.shape, dproj.dtype), jax.ShapeDtypeStruct((KD, D), F32)),
        input_output_aliases={7: 0},
        scratch_shapes=[pltpu.VMEM((te + HALO_D, D), F32), pltpu.VMEM((te + HALO_D, D), F32), pltpu.VMEM((te, D), F32),
                        pltpu.VMEM((SUBLANES, te + HALO_D - SUBLANES, LANE), F32),
                        pltpu.VMEM((SUBLANES, te + HALO_D - SUBLANES, LANE), F32)],
        compiler_params=_cp(("arbitrary",)),
    )(dc1, dc1, proj, proj, proj, proj, dw_w, dproj)


def _merge_fwd(y_a, y_b, proj, b_cf, w_o):
    lp = y_a.shape[0]
    te = _pick(lp, (320,))

    def body(ya_ref, yb_ref, ga_ref, gb_ref, bias_ref, wo_ref, out_ref, z_ref):
        merged = (_sig(ga_ref[...]) * ya_ref[...] + _sig(gb_ref[...]) * (yb_ref[...] + bias_ref[...])).astype(BF16)
        out_ref[...] = merged
        z_ref[...] = _dot(merged, wo_ref[...])

    row = lambda j: pl.BlockSpec((te, D), lambda i: (i, j))
    return _call(
        body, name="merge_fwd", grid=(lp // te,),
        in_specs=[row(0), row(0), row(CB_MA), row(CB_MB), pl.BlockSpec((1, D), lambda i: (0, 0)),
                  pl.BlockSpec((D, D), lambda i: (0, 0))],
        out_specs=(row(0), row(0)),
        out_shape=(jax.ShapeDtypeStruct((lp, D), BF16), jax.ShapeDtypeStruct((lp, D), F32)),
        compiler_params=_cp(("parallel",)),
    )(y_a, y_b, proj, proj, b_cf, w_o)


def _merge_bwd(dx_out_b, w_o, y_a, y_b, proj, b_cf):
    lp = y_a.shape[0]
    te = _pick(lp, (320,))

    def body(dx_ref, wo_ref, ya_ref, yb_ref, ga_ref, gb_ref, bias_ref, dya_ref, dyb_ref, dg_ref, db_ref):
        @pl.when(pl.program_id(0) == 0)
        def _():
            db_ref[...] = jnp.zeros_like(db_ref)

        dm = _dot_nt(dx_ref[...], wo_ref[...])
        sa, sb = _sig(ga_ref[...]), _sig(gb_ref[...])
        dyb = sb * dm
        dya_ref[...] = (sa * dm).astype(BF16)
        dyb_ref[...] = dyb.astype(BF16)
        dg_ref[:, :D] = (dm * ya_ref[...] * sa * (1.0 - sa)).astype(BF16)
        dg_ref[:, D:] = (dm * (yb_ref[...] + bias_ref[...]) * sb * (1.0 - sb)).astype(BF16)
        db_ref[...] += _colsum(dyb)

    row = lambda j: pl.BlockSpec((te, D), lambda i: (i, j))
    act = jax.ShapeDtypeStruct((lp, D), BF16)
    return _call(
        body, name="merge_bwd", grid=(lp // te,),
        in_specs=[row(0), pl.BlockSpec((D, D), lambda i: (0, 0)), row(0), row(0), row(CB_MA), row(CB_MB),
                  pl.BlockSpec((1, D), lambda i: (0, 0))],
        out_specs=(row(0), row(0), pl.BlockSpec((te, 2 * D), lambda i: (i, CB_MA // 2)),
                   pl.BlockSpec((1, D), lambda i: (0, 0))),
        out_shape=(act, act, jax.ShapeDtypeStruct((lp, NCB * D), BF16), jax.ShapeDtypeStruct((1, D), F32)),
        compiler_params=_cp(("arbitrary",)),
    )(dx_out_b, w_o, y_a, y_b, proj, proj, b_cf)


def _final_fwd_bwd(x_ext, z, target, final_w):
    lp = x_ext.shape[0]
    te = _pick(lp, (640,))
    nsub = te // LANE

    def body(x_ref, z_ref, *rest):
        t_refs, (w_ref, dx_ref, dxb_ref, loss_ref, dw_ref) = rest[:nsub], rest[nsub:]
        i = pl.program_id(0)

        @pl.when(i == 0)
        def _():
            loss_ref[...] = jnp.zeros_like(loss_ref)
            dw_ref[...] = jnp.zeros_like(dw_ref)

        w = w_ref[...]
        for k in range(nsub):
            rs = slice(k * LANE, (k + 1) * LANE)
            xo = x_ref[rs, :] + z_ref[rs, :]
            r = lax.rsqrt(jnp.mean(xo * xo, axis=-1, keepdims=True) + EPS)
            xhat = xo * r
            err = xhat * w - t_refs[k][...]
            if k == 0:
                err = jnp.where(i > 0, err, 0.0)
            loss_ref[...] += 0.5 * jnp.sum(jnp.mean(err * err, axis=-1, keepdims=True), keepdims=True)
            dy = err * (1.0 / D)
            dw_ref[...] += _colsum(dy * xhat)
            dxn = dy * w
            dx = r * (dxn - xhat * jnp.mean(dxn * xhat, axis=-1, keepdims=True))
            dx_ref[rs, :] = dx
            dxb_ref[rs, :] = dx.astype(BF16)

    piece = lambda k: pl.BlockSpec((LANE, D), lambda i: (jnp.maximum(i * nsub + k - 1, 0), 0))
    row = pl.BlockSpec((te, D), lambda i: (i, 0))
    return _call(
        body, name="final_fwd_bwd", grid=(lp // te,),
        in_specs=[row, row] + [piece(k) for k in range(nsub)] + [pl.BlockSpec((1, D), lambda i: (0, 0))],
        out_specs=(row, row, pl.BlockSpec((1, 1), lambda i: (0, 0)), pl.BlockSpec((1, D), lambda i: (0, 0))),
        out_shape=(jax.ShapeDtypeStruct((lp, D), F32), jax.ShapeDtypeStruct((lp, D), BF16),
                   jax.ShapeDtypeStruct((1, 1), F32), jax.ShapeDtypeStruct((1, D), F32)),
        compiler_params=_cp(("arbitrary",)),
    )(x_ext, z, *([target] * nsub), final_w)


def _prenorm_bwd(dh, x_ext, dx_out, norm_w, seq):
    lp = x_ext.shape[0]
    te = _pick(lp, (640,))
    nt = lp // te
    head = lp - seq

    def body(dh_ref, x_ref, dxo_ref, w_ref, gx_ref, head_ref, dw_ref, stage, sems):
        i = pl.program_id(0)
        slot = i % 2

        def first_copy():
            return pltpu.make_async_copy(stage.at[0, pl.ds(head, te - head)], gx_ref.at[pl.ds(0, te - head)], sems.at[0])

        def tile_copy(step, s):
            return pltpu.make_async_copy(stage.at[s], gx_ref.at[pl.ds(pl.multiple_of(step * te - head, LANE), te)],
                                         sems.at[s])

        @pl.when(i == 0)
        def _():
            dw_ref[...] = jnp.zeros_like(dw_ref)

        @pl.when(i == 2)
        def _():
            first_copy().wait()

        @pl.when(i > 2)
        def _():
            tile_copy(i - 2, slot).wait()

        x, dh = x_ref[...], dh_ref[...]
        r = lax.rsqrt(jnp.mean(x * x, axis=-1, keepdims=True) + EPS)
        xhat = x * r
        dxn = dh * w_ref[...]
        stage[slot] = dxo_ref[...] + r * (dxn - xhat * jnp.mean(dxn * xhat, axis=-1, keepdims=True))
        dw_ref[...] += _colsum(dh * xhat)

        @pl.when(i == 0)
        def _():
            head_ref[...] = stage[0, :head, :]
            first_copy().start()

        @pl.when(i > 0)
        def _():
            tile_copy(i, slot).start()

        @pl.when(i == nt - 1)
        def _():
            if nt >= 2:
                (first_copy() if nt == 2 else tile_copy(nt - 2, (nt - 2) % 2)).wait()
            (first_copy() if nt == 1 else tile_copy(nt - 1, (nt - 1) % 2)).wait()

    row = pl.BlockSpec((te, D), lambda i: (i, 0))
    return _call(
        body, name="prenorm_bwd", grid=(nt,),
        in_specs=[row, row, row, pl.BlockSpec((1, D), lambda i: (0, 0))],
        out_specs=(pl.BlockSpec(memory_space=pl.ANY), pl.BlockSpec((head, D), lambda i: (0, 0)),
                   pl.BlockSpec((1, D), lambda i: (0, 0))),
        out_shape=(jax.ShapeDtypeStruct((seq, D), F32), jax.ShapeDtypeStruct((head, D), F32),
                   jax.ShapeDtypeStruct((1, D), F32)),
        scratch_shapes=[pltpu.VMEM((2, te, D), F32), pltpu.SemaphoreType.DMA((2,))],
        compiler_params=_cp(("arbitrary",)),
    )(dh, x_ext, dx_out, norm_w)


def _adam_reduce(parts, w, m, v, name):
    r, n = w.shape
    tr = _pick(r, (128,)) if r % 128 == 0 else r

    def body(p_ref, w_ref, m_ref, v_ref, g_ref, d_ref, m2_ref, v2_ref):
        g = p_ref[0]
        for s in range(1, NDEV):
            g = g + p_ref[s]
        _adam_write(g, w_ref, m_ref, v_ref, g_ref, d_ref, m2_ref, v2_ref)

    blk = pl.BlockSpec((tr, n), lambda i: (i, 0))
    out = jax.ShapeDtypeStruct((r, n), F32)
    return _call(
        body, name=name, grid=(r // tr,),
        in_specs=[pl.BlockSpec((NDEV, tr, n), lambda i: (0, i, 0)), blk, blk, blk],
        out_specs=(blk, blk, blk, blk), out_shape=(out, out, out, out),
        compiler_params=_cp(("parallel",)),
    )(parts, w, m, v)


def _adam_write(g, w_ref, m_ref, v_ref, g_ref, d_ref, m2_ref, v2_ref):
    c1 = 1.0 - ADAM_B1 ** ADAM_STEP
    c2 = 1.0 - ADAM_B2 ** ADAM_STEP
    m2 = ADAM_B1 * m_ref[...] + (1.0 - ADAM_B1) * g
    v2 = ADAM_B2 * v_ref[...] + (1.0 - ADAM_B2) * (g * g)
    g_ref[...] = g
    m2_ref[...] = m2
    v2_ref[...] = v2
    d_ref[...] = -ADAM_LR * ((m2 / c1) / (jnp.sqrt(v2 / c2) + ADAM_EPS) + ADAM_WD * w_ref[...])


def _adam_chips(own, recv, w, m, v, name):
    r, n = w.shape
    tr, tc = _shard_tile(r, n)

    def body(own_ref, p_ref, w_ref, m_ref, v_ref, g_ref, d_ref, m2_ref, v2_ref):
        my_chip = 2 * lax.axis_index("x") + lax.axis_index("y")
        g = None
        for j in range(NCHIP):
            part = jnp.where(my_chip == j, own_ref[...], p_ref[j].astype(F32))
            g = part if g is None else g + part
        _adam_write(g, w_ref, m_ref, v_ref, g_ref, d_ref, m2_ref, v2_ref)

    blk = pl.BlockSpec((tr, tc), lambda i, k: (i, k))
    out = jax.ShapeDtypeStruct((r, n), F32)
    return _call(
        body, name=name, grid=(r // tr, n // tc),
        in_specs=[blk, pl.BlockSpec((NCHIP, tr, tc), lambda i, k: (0, i, k)), blk, blk, blk],
        out_specs=(blk, blk, blk, blk), out_shape=(out, out, out, out),
        compiler_params=_cp(("parallel", "parallel")),
    )(own, recv, w, m, v)


SMALL = ("norm_w", "a_log", "dt_bias", "dn_norm_w", "dw_b", "ln_w", "ln_b", "b_cf_out", "final_norm_w")


def kernel(x, meta, norm_w, w_in, conv_qkv_w, a_log, dt_bias, dn_norm_w, w_dn_out, dw_w, dw_b, ln_w, ln_b, w_cf_out, b_cf_out, w_o, final_norm_w, loss_target, m_meta, m_norm_w, m_w_in, m_conv_qkv_w, m_a_log, m_dt_bias, m_dn_norm_w, m_w_dn_out, m_dw_w, m_dw_b, m_ln_w, m_ln_b, m_w_cf_out, m_b_cf_out, m_w_o, m_final_norm_w, v_meta, v_norm_w, v_w_in, v_conv_qkv_w, v_a_log, v_dt_bias, v_dn_norm_w, v_w_dn_out, v_dw_w, v_dw_b, v_ln_w, v_ln_b, v_w_cf_out, v_b_cf_out, v_w_o, v_final_norm_w):
    seq = x.shape[1]
    pad = (-(seq + NMETA)) % LANE
    in_w = w_in.shape[2] * NDEV
    n_qkvz = 4 * D
    n_ba = 2 * H

    me = 4 * lax.axis_index("x") + 2 * lax.axis_index("y") + lax.axis_index("c")
    late = [w_dn_out[0].astype(BF16), w_cf_out[0].astype(BF16), w_o[0].astype(BF16), conv_qkv_w[0], dw_w[0]]
    late_lands = [lax.dynamic_update_index_in_dim(jnp.zeros((NDEV,) + b.shape, b.dtype), b, me, 0) for b in late]
    w_in_g, meta_g = _gather_two_level([w_in[0].astype(BF16).T, meta], "gather_weights")
    late_send, late_recv, late_thru, late_land_thru, _ = _split_start(
        _all_copies, NDEV - 1, "gather_late_start", late, late_lands, meta_g)
    w_full_t = w_in_g.reshape(in_w, D)
    c_glu = n_qkvz + n_ba
    c_zb, c_mg = c_glu + 2 * D, c_glu + 3 * D
    w_main_t = jnp.concatenate([w_full_t[:n_qkvz], w_full_t[c_glu:c_zb], w_full_t[c_mg:], w_full_t[c_zb:c_mg]],
                               axis=0)
    w_ba_t = jnp.pad(w_full_t[n_qkvz:n_qkvz + n_ba], ((0, LANE - n_ba), (0, 0)))
    meta_full = jnp.transpose(meta_g, (1, 0, 2)).reshape(NMETA, D)
    ab = jnp.pad(jnp.concatenate([a_log, dt_bias], axis=0), ((0, 0), (H, LANE - 2 * H)))

    x_ext = jnp.concatenate([jnp.zeros((pad, D), F32), meta_full, x[0]], axis=0)

    proj, ba, h = _proj_fwd(x_ext, norm_w, w_main_t, w_ba_t)
    _, (w_dn_g, w_cf_g, w_o_g, cqw_g, dww_g) = _split_wait(
        _all_copies, "gather_late_wait", late_send, late_recv, late_thru, late_land_thru, ba)
    w_dn, w_cf, w_oo = (t.reshape(D, D) for t in (w_dn_g, w_cf_g, w_o_g))
    cqw = jnp.transpose(cqw_g, (1, 0, 2)).reshape(KQ, 3 * D)
    dww = jnp.transpose(dww_g, (1, 0, 2)).reshape(KD, D)
    qkv, bg = _qkv_conv_fwd(proj, ba, cqw, ab, pad)
    o, sall, tall = _delta_fwd(qkv, bg)
    o_n, y_a = _o_post_fwd(o, proj, dn_norm_w, w_dn)
    c1, c3, y_b = _conv_b_fwd(proj, dww, dw_b, ln_w, ln_b, w_cf)
    merged, z = _merge_fwd(y_a, y_b, proj, b_cf_out, w_oo)
    dx_out, dx_out_b, loss_part, g_final_w = _final_fwd_bwd(x_ext, z, loss_target[0], final_norm_w.reshape(1, D))

    g_w_o = _mm_tn(merged, dx_out_b, "g_w_o_mm")
    dy_a, dy_b, dproj, g_b_cf = _merge_bwd(dx_out_b, w_oo, y_a, y_b, proj, b_cf_out)
    g_w_cf = _mm_tn(c3, dy_b, "g_w_cf_mm")
    g_w_dn = _mm_tn(o_n, dy_a, "g_w_dn_mm")
    dc1, dproj, sums_b = _conv_b_bwd1(dy_b, w_cf, c1, proj, ln_w, ln_b, dproj)
    dproj, g_dw_w = _conv_b_bwd2(dc1, proj, dww, dproj)
    do, dproj, g_dn_w = _o_post_bwd(dy_a, w_dn, o, proj, dn_norm_w, dproj)
    dqkv, dbg = _delta_bwd(qkv, bg, sall, tall, do)
    dproj, g_cqw = _qkv_conv_bwd(proj, dqkv, cqw, dproj)
    dba, dab = _ba_bwd(dbg, ba, ab, pad)
    g_w_main_t = _mm_tn(dproj, h, "g_w_main_mm")
    g_w_ba_t = _mm_tn(dba, h, "g_w_ba_mm")

    g_w_full_t = jnp.concatenate([g_w_main_t[:n_qkvz], g_w_ba_t[:n_ba], g_w_main_t[CB_GA_ * D:CB_MA * D],
                                  g_w_main_t[CB_ZB * D:], g_w_main_t[CB_MA * D:CB_ZB * D]], axis=0)
    big = [t.reshape(NCHIP, 2, t.shape[0] // NDEV, D) for t in (g_w_full_t, g_w_dn, g_w_cf, g_w_o)]
    sw_send, sw_recv, big_thru, sw_land, sw_token = _split_start(
        _sibling_copies, NCHIP, "swap_sibling_start", big,
        [lax.empty((NCHIP,) + t.shape[2:], t.dtype) for t in big], g_w_ba_t)
    dh = _dh_mm(dproj, dba, w_main_t, w_ba_t + sw_token[0, 0].astype(BF16), 0)
    big_back, from_sibling = _split_wait(_sibling_copies, "swap_sibling_wait", sw_send, sw_recv, big_thru, sw_land, dh)
    pairs = [_pair_add(a, g, f"pair_add_{i}") for i, (a, g) in enumerate(zip(big_back, from_sibling))]
    send_sems, recv_sems, pair_thru, land_thru, token = _split_start(
        _chip_copies, NCHIP - 1, "scatter_chips_start",
        [p for p, _ in pairs], [jnp.zeros(p.shape, p.dtype) for p, _ in pairs], g_w_ba_t)
    dh = _dh_mm(dproj, dba, w_main_t, w_ba_t + token[0, 0].astype(BF16), 1, dh)
    grad_x, dhead, g_norm_w = _prenorm_bwd(dh, x_ext, dx_out, norm_w, seq)
    _, from_chips = _split_wait(_chip_copies, "scatter_chips_wait", send_sems, recv_sems, pair_thru, land_thru,
                                g_norm_w)

    split_cols = lambda t: jnp.transpose(t.reshape(t.shape[0], NDEV, t.shape[1] // NDEV), (1, 0, 2))
    small = {"norm_w": g_norm_w, "a_log": dab[0:1, H:2 * H], "dt_bias": dab[1:2, H:2 * H], "dn_norm_w": g_dn_w,
             "dw_b": sums_b[2:3], "ln_w": sums_b[0:1], "ln_b": sums_b[1:2], "b_cf_out": g_b_cf,
             "final_norm_w": g_final_w}
    small_vec = jnp.concatenate([small[k] for k in SMALL], axis=1)
    ns = small_vec.shape[1]
    ns_pad = (-ns) % LANE
    small_vec = jnp.pad(small_vec, ((0, 0), (0, ns_pad)))
    p_meta, p_cqw, p_dww, p_small = _exchange(
        [split_cols(dhead[pad:pad + NMETA]), split_cols(g_cqw), split_cols(g_dw_w), small_vec],
        [True] * 3 + [False], "exchange_small")

    res = {}
    res["w_in"] = tuple(t.T for t in _adam_chips(pairs[0][1], from_chips[0], w_in[0].T, m_w_in[0].T, v_w_in[0].T,
                                                   "adam_w_in"))
    res["w_dn_out"] = _adam_chips(pairs[1][1], from_chips[1], w_dn_out[0], m_w_dn_out[0], v_w_dn_out[0], "adam_w_dn")
    res["w_cf_out"] = _adam_chips(pairs[2][1], from_chips[2], w_cf_out[0], m_w_cf_out[0], v_w_cf_out[0], "adam_w_cf")
    res["w_o"] = _adam_chips(pairs[3][1], from_chips[3], w_o[0], m_w_o[0], v_w_o[0], "adam_w_o")
    res["meta"] = _adam_reduce(p_meta, meta, m_meta, v_meta, "adam_meta")
    res["conv_qkv_w"] = _adam_reduce(p_cqw, conv_qkv_w[0], m_conv_qkv_w[0], v_conv_qkv_w[0], "adam_conv_qkv_w")
    res["dw_w"] = _adam_reduce(p_dww, dw_w[0], m_dw_w[0], v_dw_w[0], "adam_dw_w")
    loc = dict(norm_w=(norm_w, m_norm_w, v_norm_w), a_log=(a_log, m_a_log, v_a_log), dt_bias=(dt_bias, m_dt_bias, v_dt_bias),
               dn_norm_w=(dn_norm_w, m_dn_norm_w, v_dn_norm_w), dw_b=(dw_b, m_dw_b, v_dw_b), ln_w=(ln_w, m_ln_w, v_ln_w),
               ln_b=(ln_b, m_ln_b, v_ln_b), b_cf_out=(b_cf_out, m_b_cf_out, v_b_cf_out),
               final_norm_w=(final_norm_w, m_final_norm_w, v_final_norm_w))
    cat = lambda j: jnp.pad(jnp.concatenate([loc[k][j].reshape(1, -1) for k in SMALL], axis=1), ((0, 0), (0, ns_pad)))
    small_res = _adam_reduce(p_small, cat(0), cat(1), cat(2), "adam_small")
    off = 0
    for k in SMALL:
        wshape = loc[k][0].shape
        nk = loc[k][0].size
        res[k] = tuple(t[:, off:off + nk].reshape(wshape) for t in small_res)
        off += nk
    shaped = dict(w_in=w_in.shape, w_dn_out=w_dn_out.shape, w_cf_out=w_cf_out.shape, w_o=w_o.shape, meta=meta.shape,
                  conv_qkv_w=conv_qkv_w.shape, dw_w=dw_w.shape)
    for k, shp in shaped.items():
        res[k] = tuple(t.reshape(shp) for t in res[k])

    loss = lax.psum(loss_part[0, 0], ("x", "y", "c"))
    order = ("meta", "norm_w", "w_in", "conv_qkv_w", "a_log", "dt_bias", "dn_norm_w", "w_dn_out", "dw_w", "dw_b", "ln_w",
             "ln_b", "w_cf_out", "b_cf_out", "w_o", "final_norm_w")
    outs = [loss, grad_x[None]]
    for j in range(4):
        outs += [res[k][j] for k in order]
    return tuple(outs)
```

```python
import functools

import jax
import jax.numpy as jnp
from jax import lax
from jax.experimental import pallas as pl
from jax.experimental.pallas import tpu as pltpu

F32 = jnp.float32
BF16 = jnp.bfloat16
HI = lax.Precision.HIGHEST

D = 1024
H = 8
DK = 128
C = 64
NMETA = 16
KQ = 4
KD = 31
HALO_Q = 8
HALO_D = 32
EPS = 1e-6
NDEV = 8
LANE = 128
MIB = 1024 * 1024

ADAM_LR, ADAM_B1, ADAM_B2, ADAM_EPS, ADAM_WD, ADAM_STEP = 0.001, 0.9, 0.999, 1e-08, 0.01, 10

CB_Q, CB_K, CB_V, CB_ZA, CB_GA_, CB_GB_, CB_MA, CB_MB, CB_ZB = range(9)
NCB = 9


def _pick(n, cands):
    for c in cands:
        if n % c == 0:
            return c
    raise ValueError(f"no tile for {n}")


def _cp(sem=None, vmem_mib=40):
    kw = dict(vmem_limit_bytes=vmem_mib * MIB)
    if sem is not None:
        kw["dimension_semantics"] = sem
    return pltpu.CompilerParams(**kw)


def _call(body, **kw):
    return pl.pallas_call(body, **kw)


def _dot(a, b):
    return jnp.dot(a.astype(BF16), b.astype(BF16), preferred_element_type=F32)


def _dot_nt(a, b):
    return lax.dot_general(a.astype(BF16), b.astype(BF16), (((1,), (1,)), ((), ())), preferred_element_type=F32)


def _dot_tn(a, b):
    return lax.dot_general(a.astype(BF16), b.astype(BF16), (((0,), (0,)), ((), ())), preferred_element_type=F32)


def _dot_hi(a, b):
    return jnp.dot(a, b, precision=HI, preferred_element_type=F32)


def _sig(x):
    return 0.5 * jnp.tanh(0.5 * x) + 0.5


def _dsilu(x, s):
    return s * (1.0 + x * (1.0 - s))


def _rowsum(x):
    return jnp.sum(x, axis=-1, keepdims=True)


def _colsum(x):
    return jnp.sum(x, axis=0, keepdims=True)


def _exchange(arrs, scatter, name):
    n = len(arrs)
    out_shape = []
    for a, sc in zip(arrs, scatter):
        shp = a.shape if sc else (NDEV,) + a.shape
        out_shape.append(jax.ShapeDtypeStruct(shp, a.dtype))

    def body(*refs):
        ins, outs = refs[:n], refs[n:2 * n]
        send_sems, recv_sems, loc_sems = refs[2 * n:]
        x, y, c = lax.axis_index("x"), lax.axis_index("y"), lax.axis_index("c")
        me = 4 * x + 2 * y + c
        copies = []
        for a in range(n):
            for k in range(1, NDEV):
                px = 1 - x if (k >> 2) & 1 else x
                py = 1 - y if (k >> 1) & 1 else y
                pc = 1 - c if k & 1 else c
                src = ins[a].at[4 * px + 2 * py + pc] if scatter[a] else ins[a]
                cp = pltpu.make_async_remote_copy(
                    src_ref=src, dst_ref=outs[a].at[me],
                    send_sem=send_sems.at[a * (NDEV - 1) + k - 1], recv_sem=recv_sems.at[a * (NDEV - 1) + k - 1],
                    device_id=(px, py, pc), device_id_type=pl.DeviceIdType.MESH)
                cp.start()
                copies.append(cp)
            loc = pltpu.make_async_copy(ins[a].at[me] if scatter[a] else ins[a], outs[a].at[me], loc_sems.at[a])
            loc.start()
            copies.append(loc)
        for cp in copies:
            cp.wait()

    any_spec = pl.BlockSpec(memory_space=pl.ANY)
    return _call(
        body, name=name, out_shape=tuple(out_shape),
        in_specs=[any_spec] * n, out_specs=tuple([any_spec] * n),
        scratch_shapes=[pltpu.SemaphoreType.DMA((n * (NDEV - 1),)), pltpu.SemaphoreType.DMA((n * (NDEV - 1),)),
                        pltpu.SemaphoreType.DMA((n,))],
    )(*arrs)


NCHIP = 4


def _gather_two_level(arrs, name):
    n = len(arrs)
    per = NDEV - 1

    def body(*refs):
        ins, outs = refs[:n], refs[n:2 * n]
        send_sems, recv_sems, loc_sems = refs[2 * n:]
        x, y, c = lax.axis_index("x"), lax.axis_index("y"), lax.axis_index("c")
        me, sibling = (x, y, c), (x, y, 1 - c)
        chips = [(1 - x, y), (x, 1 - y), (1 - x, 1 - y)]

        def slot(a, px, py, pc):
            return outs[a].at[4 * px + 2 * py + pc]

        def copy(a, k, block, to, src=None):
            return pltpu.make_async_remote_copy(
                src_ref=slot(a, *block) if src is None else src, dst_ref=slot(a, *block),
                send_sem=send_sems.at[a * per + k], recv_sem=recv_sems.at[a * per + k],
                device_id=to, device_id_type=pl.DeviceIdType.MESH)

        local, sent = [], []
        for a in range(n):
            mine = pltpu.make_async_copy(ins[a], slot(a, *me), loc_sems.at[a])
            mine.start()
            local.append(mine)
            first = [copy(a, 1 + j, me, (*chip, c), src=ins[a]) for j, chip in enumerate(chips)]
            first.append(copy(a, 0, me, sibling, src=ins[a]))
            for cp in first:
                cp.start()
            sent += first
        for j, chip in enumerate(chips):
            for a in range(n):
                copy(a, 1 + j, (*chip, c), me).wait_recv()
                cp = copy(a, 4 + j, (*chip, c), sibling)
                cp.start()
                sent.append(cp)
        for a in range(n):
            copy(a, 0, sibling, me).wait_recv()
            for j, chip in enumerate(chips):
                copy(a, 4 + j, (*chip, 1 - c), me).wait_recv()
        for cp in sent:
            cp.wait_send()
        for cp in local:
            cp.wait()

    any_spec = pl.BlockSpec(memory_space=pl.ANY)
    return _call(
        body, name=name, out_shape=tuple(jax.ShapeDtypeStruct((NDEV,) + a.shape, a.dtype) for a in arrs),
        in_specs=[any_spec] * n, out_specs=tuple([any_spec] * n),
        scratch_shapes=[pltpu.SemaphoreType.DMA((n * per,)), pltpu.SemaphoreType.DMA((n * per,)),
                        pltpu.SemaphoreType.DMA((n,))],
    )(*arrs)


def _pair_add(arr4, got, name):
    _, _, r, n = arr4.shape
    tr, tc = _shard_tile(r, n)

    def body(a_ref, g_ref, p_ref, own_ref):
        c = lax.axis_index("c")
        my_chip = 2 * lax.axis_index("x") + lax.axis_index("y")
        s = jnp.where(c == 0, a_ref[0, 0], a_ref[0, 1]) + g_ref[0]
        p_ref[0] = s.astype(BF16)

        @pl.when(pl.program_id(2) == my_chip)
        def _():
            own_ref[...] = s

    return _call(
        body, name=name, grid=(r // tr, n // tc, NCHIP),
        in_specs=[pl.BlockSpec((1, 2, tr, tc), lambda i, k, j: (j, 0, i, k)),
                  pl.BlockSpec((1, tr, tc), lambda i, k, j: (j, i, k))],
        out_specs=(pl.BlockSpec((1, tr, tc), lambda i, k, j: (j, i, k)), pl.BlockSpec((tr, tc), lambda i, k, j: (i, k))),
        out_shape=(jax.ShapeDtypeStruct((NCHIP, r, n), BF16), jax.ShapeDtypeStruct((r, n), F32)),
        compiler_params=_cp(("parallel", "parallel", "arbitrary")),
    )(arr4, got)


def _shard_tile(r, n):
    return (128, n) if r % 128 == 0 else (r, 256)


def _all_copies(srcs, lands, send_sems, recv_sems):
    x, y, c = lax.axis_index("x"), lax.axis_index("y"), lax.axis_index("c")
    per = NDEV - 1
    copies = []
    for a in range(len(srcs)):
        for k in range(1, NDEV):
            px = 1 - x if (k >> 2) & 1 else x
            py = 1 - y if (k >> 1) & 1 else y
            pc = 1 - c if k & 1 else c
            copies.append(pltpu.make_async_remote_copy(
                src_ref=srcs[a], dst_ref=lands[a].at[4 * x + 2 * y + c],
                send_sem=send_sems.at[a * per + k - 1], recv_sem=recv_sems.at[a * per + k - 1],
                device_id=(px, py, pc), device_id_type=pl.DeviceIdType.MESH))
    return copies


def _split_start(make_copies, peers, name, arrs, lands, after):
    n = len(arrs)
    nsem = n * peers

    def body(*refs):
        srcs, land_in = refs[:n], refs[n:2 * n]
        send_sems, recv_sems = refs[2 * n + 1:2 * n + 3]
        token = refs[-1]
        for cp in make_copies(srcs, land_in, send_sems, recv_sems):
            cp.start()
        token[...] = jnp.zeros_like(token)

    hbm = pl.BlockSpec(memory_space=pltpu.HBM)
    sem = pl.BlockSpec(memory_space=pltpu.SEMAPHORE)
    both = list(arrs) + list(lands)
    outs = _call(
        body, name=name,
        out_shape=(pltpu.SemaphoreType.DMA((nsem,)), pltpu.SemaphoreType.DMA((nsem,)),
                   *[pltpu.HBM(a.shape, a.dtype) for a in both], jax.ShapeDtypeStruct((SUBLANES, LANE), F32)),
        in_specs=[hbm] * (2 * n) + [pl.BlockSpec(memory_space=pl.ANY)],
        out_specs=(sem, sem, *[hbm] * (2 * n), pl.BlockSpec(memory_space=pltpu.VMEM)),
        input_output_aliases={i: i + 2 for i in range(2 * n)},
        compiler_params=pltpu.CompilerParams(has_side_effects=pltpu.SideEffectType.DATAFLOW_SIDE_EFFECTING),
    )(*[pltpu.with_memory_space_constraint(t, pltpu.HBM) for t in both], after)
    return outs[0], outs[1], outs[2:2 + n], outs[2 + n:2 + 2 * n], outs[-1]


def _split_wait(make_copies, name, send_sems, recv_sems, arrs, lands, after):
    n = len(arrs)

    def body(*refs):
        srcs, land_in = refs[:n], refs[n:2 * n]
        send, recv = refs[2 * n], refs[2 * n + 1]
        for cp in make_copies(srcs, land_in, send, recv):
            cp.wait_send()
            cp.wait_recv()

    hbm = pl.BlockSpec(memory_space=pltpu.HBM)
    sem = pl.BlockSpec(memory_space=pltpu.SEMAPHORE)
    both = list(arrs) + list(lands)
    outs = _call(
        body, name=name,
        out_shape=tuple(pltpu.HBM(a.shape, a.dtype) for a in both),
        in_specs=[hbm] * (2 * n) + [sem, sem, pl.BlockSpec(memory_space=pl.ANY)], out_specs=tuple([hbm] * (2 * n)),
        input_output_aliases={i: i for i in range(2 * n)},
        compiler_params=pltpu.CompilerParams(has_side_effects=pltpu.SideEffectType.DATAFLOW_SIDE_EFFECTING),
    )(*both, send_sems, recv_sems, after)
    return outs[:n], outs[n:]


def _sibling_copies(srcs, lands, send_sems, recv_sems):
    x, y, c = lax.axis_index("x"), lax.axis_index("y"), lax.axis_index("c")
    copies = []
    for a in range(len(srcs)):
        for j in range(NCHIP):
            copies.append(pltpu.make_async_remote_copy(
                src_ref=srcs[a].at[j, 1 - c], dst_ref=lands[a].at[j],
                send_sem=send_sems.at[a * NCHIP + j], recv_sem=recv_sems.at[a * NCHIP + j],
                device_id=(x, y, 1 - c), device_id_type=pl.DeviceIdType.MESH))
    return copies


def _chip_copies(srcs, lands, send_sems, recv_sems):
    x, y, c = lax.axis_index("x"), lax.axis_index("y"), lax.axis_index("c")
    per = NCHIP - 1
    copies = []
    for a in range(len(srcs)):
        for k in range(1, NCHIP):
            px = 1 - x if (k >> 1) & 1 else x
            py = 1 - y if k & 1 else y
            copies.append(pltpu.make_async_remote_copy(
                src_ref=srcs[a].at[2 * px + py], dst_ref=lands[a].at[2 * x + y],
                send_sem=send_sems.at[a * per + k - 1], recv_sem=recv_sems.at[a * per + k - 1],
                device_id=(px, py, c), device_id_type=pl.DeviceIdType.MESH))
    return copies


def _mm_tn(a, b, name):
    t, m = a.shape
    n = b.shape[1]
    tt = _pick(t, (1664, 640, 128))
    tm = _pick(m, (1024, 512, 128))
    tn = _pick(n, (1152, 1024, 512, 128))
    nt = t // tt

    def body(a_ref, b_ref, o_ref):
        s = pl.program_id(2)
        part = _dot_tn(a_ref[...], b_ref[...])

        @pl.when(s == 0)
        def _():
            o_ref[...] = part

        @pl.when(s > 0)
        def _():
            o_ref[...] += part

    return _call(
        body, name=name, grid=(m // tm, n // tn, nt),
        in_specs=[pl.BlockSpec((tt, tm), lambda i, j, s: (s, i)), pl.BlockSpec((tt, tn), lambda i, j, s: (s, j))],
        out_specs=pl.BlockSpec((tm, tn), lambda i, j, s: (i, j)),
        out_shape=jax.ShapeDtypeStruct((m, n), F32),
        compiler_params=_cp(("parallel", "parallel", "arbitrary")),
    )(a, b)


def _proj_fwd(x_ext, norm_w, w_main_t, w_ba_t):
    lp = x_ext.shape[0]
    n = w_main_t.shape[0]
    tm = _pick(lp, (1040, 832, 640, 320))
    tn = 1024

    def body(x_ref, nw_ref, w_ref, wba_ref, proj_ref, ba_ref, h_ref):
        @pl.when(pl.program_id(1) == 0)
        def _():
            x = x_ref[...]
            r = lax.rsqrt(jnp.mean(x * x, axis=-1, keepdims=True) + EPS)
            h = (x * r * nw_ref[...]).astype(BF16)
            h_ref[...] = h
            ba_ref[...] = _dot_nt(h, wba_ref[...])

        proj_ref[...] = _dot_nt(h_ref[...], w_ref[...])

    return _call(
        body, name="proj_fwd", grid=(lp // tm, n // tn),
        in_specs=[pl.BlockSpec((tm, D), lambda i, j: (i, 0)), pl.BlockSpec((1, D), lambda i, j: (0, 0)),
                  pl.BlockSpec((tn, D), lambda i, j: (j, 0)), pl.BlockSpec((LANE, D), lambda i, j: (0, 0))],
        out_specs=(pl.BlockSpec((tm, tn), lambda i, j: (i, j)), pl.BlockSpec((tm, LANE), lambda i, j: (i, 0)),
                   pl.BlockSpec((tm, D), lambda i, j: (i, 0))),
        out_shape=(jax.ShapeDtypeStruct((lp, n), F32), jax.ShapeDtypeStruct((lp, LANE), F32),
                   jax.ShapeDtypeStruct((lp, D), BF16)),
        compiler_params=_cp(("parallel", "arbitrary")),
    )(x_ext, norm_w, w_main_t, w_ba_t)


def _dh_mm(dproj, dba, w_main_t, w_ba_t, part, dh_so_far=None):
    lp, n = dproj.shape
    tm = _pick(lp, (1040, 832, 640, 320))
    tn = 1024
    tk = 2304
    nk = n // tk
    tiles = lp // tm
    first = (tiles + 1) // 2
    t0, nt = (0, first) if part == 0 else (first, tiles - first)
    if nt == 0:
        return dh_so_far

    def body(a_ref, ba_ref, b_ref, bba_ref, *rest):
        o_ref, acc = rest[-2:]
        kk = pl.program_id(2)

        @pl.when(kk == 0)
        def _():
            acc[...] = jnp.dot(ba_ref[...], bba_ref[...], preferred_element_type=F32)

        acc[...] += jnp.dot(a_ref[...], b_ref[...], preferred_element_type=F32)

        @pl.when(kk == nk - 1)
        def _():
            o_ref[...] = acc[...]

    prev = [] if dh_so_far is None else [dh_so_far]
    return _call(
        body, name=f"dh_mm_{part}", grid=(nt, D // tn, nk),
        in_specs=[pl.BlockSpec((tm, tk), lambda i, j, kk: (i + t0, kk)),
                  pl.BlockSpec((tm, LANE), lambda i, j, kk: (i + t0, 0)),
                  pl.BlockSpec((tk, tn), lambda i, j, kk: (kk, j)), pl.BlockSpec((LANE, tn), lambda i, j, kk: (0, j))]
                 + [pl.BlockSpec(memory_space=pl.ANY)] * len(prev),
        out_specs=pl.BlockSpec((tm, tn), lambda i, j, kk: (i + t0, j)),
        out_shape=jax.ShapeDtypeStruct((lp, D), F32),
        input_output_aliases={4: 0} if prev else {},
        scratch_shapes=[pltpu.VMEM((tm, tn), F32)],
        compiler_params=_cp(("parallel", "parallel", "arbitrary")),
    )(dproj, dba, w_main_t, w_ba_t, *prev)


def _beta_g(ba, ab, row0, pad):
    lane = lax.broadcasted_iota(jnp.int32, ba.shape, 1)
    rows = row0 + lax.broadcasted_iota(jnp.int32, ba.shape, 0)
    z = ba + ab[1:2, :]
    sp = jnp.maximum(z, 0.0) + jnp.log(1.0 + jnp.exp(-jnp.abs(z)))
    val = jnp.where(lane < H, _sig(ba), -jnp.exp(ab[0:1, :]) * sp)
    return jnp.where((lane < 2 * H) & (rows >= pad), val, 0.0)


def _qkv_conv_fwd(proj, ba, conv_w, ab, pad):
    lp = proj.shape[0]
    te = _pick(lp, (640, 320))
    hb = te // HALO_Q

    def body(main_ref, halo_ref, cw_ref, ba_ref, ab_ref, out_ref, bg_ref, pre_scr, tap_scr):
        i, s = pl.program_id(0), pl.program_id(1)
        pre_scr[:HALO_Q, :] = jnp.where(i > 0, halo_ref[...], 0.0)
        pre_scr[HALO_Q:, :] = main_ref[...]
        scale = jnp.where(s == 0, DK ** -0.5, 1.0)
        off = HALO_Q - (KQ - 1)

        def head(h, carry):
            cs = pl.ds(pl.multiple_of(h * DK, DK), DK)
            for j in range(KQ - 1):
                tap_scr[j] = pre_scr[off + j:off + j + te, cs]
            co = cw_ref[KQ - 1:KQ, cs] * pre_scr[HALO_Q:, cs]
            for j in range(KQ - 1):
                co = co + cw_ref[j:j + 1, cs] * tap_scr[j]
            a = co * _sig(co)
            r = lax.rsqrt(_rowsum(a * a) + EPS)
            out_ref[:, cs] = jnp.where(s == 2, a, a * (r * scale))
            return carry

        lax.fori_loop(0, H, head, 0, unroll=True)

        @pl.when(s == 0)
        def _():
            bg_ref[...] = _beta_g(ba_ref[...], ab_ref[...], i * te, pad)

    return _call(
        body, name="qkv_conv_fwd", grid=(lp // te, 3),
        in_specs=[pl.BlockSpec((te, D), lambda i, s: (i, s)),
                  pl.BlockSpec((HALO_Q, D), lambda i, s: (jnp.maximum(i * hb - 1, 0), s)),
                  pl.BlockSpec((KQ, D), lambda i, s: (0, s)),
                  pl.BlockSpec((te, LANE), lambda i, s: (i, 0)),
                  pl.BlockSpec((2, LANE), lambda i, s: (0, 0))],
        out_specs=(pl.BlockSpec((te, D), lambda i, s: (i, s)), pl.BlockSpec((te, LANE), lambda i, s: (i, 0))),
        out_shape=(jax.ShapeDtypeStruct((lp, 3 * D), F32), jax.ShapeDtypeStruct((lp, LANE), F32)),
        scratch_shapes=[pltpu.VMEM((te + HALO_Q, D), F32), pltpu.VMEM((KQ - 1, te, DK), F32)],
        compiler_params=_cp(("parallel", "arbitrary")),
    )(proj, proj, conv_w, ba, ab)


def _tri_masks():
    row = lax.broadcasted_iota(jnp.int32, (C, C), 0)
    col = lax.broadcasted_iota(jnp.int32, (C, C), 1)
    return row, col


def _split(a):
    hi = a.astype(BF16)
    return hi, (a - hi.astype(F32)).astype(BF16)


def _dot3(a, b, dims=(((1,), (0,)), ((), ()))):
    (ah, al), (bh, bl) = a, b
    mm = lambda x, y: lax.dot_general(x, y, dims, preferred_element_type=F32)
    return mm(ah, bh) + (mm(ah, bl) + mm(al, bh))


CHUNKS_PER_STEP = 5
CHUNKS_PER_STEP_BWD = 1
TINV_BLOCK = 16


def _tinv(ns, row, col):
    eye = (row == col).astype(F32)
    sh = TINV_BLOCK.bit_length() - 1
    same16 = (row >> sh) == (col >> sh)
    same32 = (row >> (sh + 1)) == (col >> (sh + 1))
    ys = [jnp.where(same16, -n, 0.0) for n in ns]
    ts = [eye + y for y in ys]
    sp = [_split(y) for y in ys]
    for level in range(3):
        yks = [_dot3(s, s) for s in sp]
        sp = [_split(yk) for yk in yks]
        ts = [t + _dot3(s, _split(t)) for s, t in zip(sp, ts)]
    for mask in (same32 & ~same16, ~same32):
        tsp = [_split(t) for t in ts]
        inner = [_dot3(_split(jnp.where(mask, n, 0.0)), t) for n, t in zip(ns, tsp)]
        ts = [t - _dot3(tp, _split(a)) for t, tp, a in zip(ts, tsp, inner)]
    return ts


def _chunk_common(q, k, v, bcol, gcc, gcr, incl, strict):
    dm = jnp.where(incl, jnp.exp(gcc - gcr), 0.0)
    kk = _dot_nt(k, k)
    qk = _dot_nt(q, k)
    gccw = jnp.broadcast_to(gcc, (C, DK))
    egc = jnp.exp(gccw)
    glast = gccw[C - 1:C, :]
    eend = jnp.exp(glast - gccw)
    elast = jnp.exp(glast)
    rhs = jnp.concatenate([v * bcol, k * (bcol * egc)], axis=1)
    return dm, kk, qk, egc, eend, elast, rhs


def _delta_fwd(qkv, bg):
    lp = qkv.shape[0]
    nc = lp // C
    heads = range(H)
    sls = [slice(h * DK, (h + 1) * DK) for h in heads]

    def body(q_ref, k_ref, v_ref, bg_ref, o_ref, sall_ref, tall_ref, s_scr):
        @pl.when(pl.program_id(0) == 0)
        def _():
            s_scr[...] = jnp.zeros_like(s_scr)

        row, col = _tri_masks()
        incl, strict = row >= col, row > col

        def prepare(sub):
            rs = slice(sub * C, (sub + 1) * C)
            bgt = bg_ref[rs, :]
            gc_all = _dot_hi(incl.astype(F32), bgt)
            gc_t = _dot_hi(bgt.T, (row <= col).astype(F32))
            qs, ks, vs = ([r[rs, sl] for sl in sls] for r in (q_ref, k_ref, v_ref))
            bcols = [jnp.broadcast_to(bgt[:, h:h + 1], (C, DK)) for h in heads]
            cm = [_chunk_common(qs[h], ks[h], vs[h], bcols[h], gc_all[:, H + h:H + h + 1], gc_t[H + h:H + h + 1, :],
                                incl, strict) for h in heads]
            dms, kks, qks, egcs, eends, elasts, rhss = zip(*cm)
            ts = _tinv([jnp.where(strict, bcols[h][:, :C] * kks[h] * dms[h], 0.0) for h in heads], row, col)
            sols = [_dot3(_split(ts[h]), _split(rhss[h])) for h in heads]
            qgs = [(qs[h] * egcs[h]).astype(BF16) for h in heads]
            ps = [(qks[h] * dms[h]).astype(BF16) for h in heads]
            kends = [(ks[h] * eends[h]).astype(BF16) for h in heads]
            return ts, sols, qgs, ps, kends, elasts

        prepared = [prepare(sub) for sub in range(CHUNKS_PER_STEP)]
        ss = [s_scr[h] for h in heads]
        for sub in range(CHUNKS_PER_STEP):
            rs = slice(sub * C, (sub + 1) * C)
            ts, sols, qgs, ps, kends, elasts = prepared[sub]
            sb = [s.astype(BF16) for s in ss]
            wvb = [(sols[h][:, :DK] - _dot(sols[h][:, DK:], sb[h])).astype(BF16) for h in heads]
            for h in heads:
                o_ref[rs, sls[h]] = _dot(qgs[h], sb[h]) + _dot(ps[h], wvb[h])
                sall_ref[sub, h] = ss[h]
                tall_ref[sub, h] = ts[h]
            ss = [ss[h] * elasts[h] + _dot_tn(kends[h], wvb[h]) for h in heads]
        for h in heads:
            s_scr[h] = ss[h]

    rows = CHUNKS_PER_STEP * C
    blk = lambda j: pl.BlockSpec((rows, D), lambda n: (n, j))
    return _call(
        body, name="delta_fwd", grid=(nc // CHUNKS_PER_STEP,),
        in_specs=[blk(0), blk(1), blk(2), pl.BlockSpec((rows, LANE), lambda n: (n, 0))],
        out_specs=(pl.BlockSpec((rows, D), lambda n: (n, 0)),
                   pl.BlockSpec((CHUNKS_PER_STEP, H, DK, DK), lambda n: (n, 0, 0, 0)),
                   pl.BlockSpec((CHUNKS_PER_STEP, H, C, C), lambda n: (n, 0, 0, 0))),
        out_shape=(jax.ShapeDtypeStruct((lp, D), F32), jax.ShapeDtypeStruct((nc, H, DK, DK), F32),
                   jax.ShapeDtypeStruct((nc, H, C, C), F32)),
        scratch_shapes=[pltpu.VMEM((H, DK, DK), F32)],
        compiler_params=_cp(("arbitrary",)),
    )(qkv, qkv, qkv, bg)


def _delta_bwd(qkv, bg, sall, tall, do):
    lp = qkv.shape[0]
    nc = lp // C

    heads = range(H)
    sls = [slice(h * DK, (h + 1) * DK) for h in heads]

    def body(q_ref, k_ref, v_ref, bg_ref, sall_ref, tall_ref, do_ref, dqkv_ref, dbg_ref, ds_scr):
        @pl.when(pl.program_id(0) == 0)
        def _():
            ds_scr[...] = jnp.zeros_like(ds_scr)

        dsns = [ds_scr[h] for h in heads]
        for sub in reversed(range(CHUNKS_PER_STEP_BWD)):
            dsns = chunk(sub, dsns, q_ref, k_ref, v_ref, bg_ref, sall_ref, tall_ref, do_ref, dqkv_ref, dbg_ref)
        for h in heads:
            ds_scr[h] = dsns[h]

    def chunk(sub, dsns, q_ref, k_ref, v_ref, bg_ref, sall_ref, tall_ref, do_ref, dqkv_ref, dbg_ref):
        rs = slice(sub * C, (sub + 1) * C)
        bgt = bg_ref[rs, :]
        row, col = _tri_masks()
        incl, strict = row >= col, row > col
        upper = (row <= col).astype(F32)
        gc_all = _dot_hi(incl.astype(F32), bgt)
        gc_t = _dot_hi(bgt.T, upper)
        lane = lax.broadcasted_iota(jnp.int32, (C, LANE), 1)
        lastrow = lax.broadcasted_iota(jnp.int32, (C, 1), 0) == C - 1
        qs, ks, vs, dos = ([r[rs, sl] for sl in sls] for r in (q_ref, k_ref, v_ref, do_ref))
        bcols = [jnp.broadcast_to(bgt[:, h:h + 1], (C, DK)) for h in heads]
        cm = [_chunk_common(qs[h], ks[h], vs[h], bcols[h], gc_all[:, H + h:H + h + 1], gc_t[H + h:H + h + 1, :],
                            incl, strict) for h in heads]
        dms, kks, qks, egcs, eends, elasts, rhss = zip(*cm)
        ss = [sall_ref[sub, h] for h in heads]
        ts = [tall_ref[sub, h] for h in heads]
        sb = [s.astype(BF16) for s in ss]
        dsb = [d.astype(BF16) for d in dsns]
        dob = [d.astype(BF16) for d in dos]
        sols = [_dot3(_split(ts[h]), _split(rhss[h])) for h in heads]
        ws = [sol[:, DK:] for sol in sols]
        qgs = [qs[h] * egcs[h] for h in heads]
        kends = [ks[h] * eends[h] for h in heads]
        wvs = [sols[h][:, :DK] - _dot(ws[h], sb[h]) for h in heads]
        wvb = [wv.astype(BF16) for wv in wvs]
        dwvs = [_dot_tn(qks[h] * dms[h], dob[h]) + _dot(kends[h], dsb[h]) for h in heads]
        dps = [jnp.where(incl, _dot_nt(dob[h], wvb[h]), 0.0) for h in heads]
        dqgs = [_dot_nt(dob[h], sb[h]) for h in heads]
        dkends = [_dot_nt(wvb[h], dsb[h]) for h in heads]
        ds_before = [_dot_tn(qgs[h], dob[h]) + elasts[h] * dsns[h] - _dot_tn(ws[h], dwvs[h]) for h in heads]
        dglasts = [elasts[h] * jnp.sum(ss[h] * dsns[h], keepdims=True) for h in heads]
        dws = [-_dot_nt(dwvs[h], sb[h]) for h in heads]
        tts = [_split(ts[h].T) for h in heads]
        drhss = [_dot3(tts[h], _split(jnp.concatenate([dwvs[h], dws[h]], axis=1))) for h in heads]
        nt_dims = (((1,), (1,)), ((), ()))
        dns = [jnp.where(strict, -_dot3(_split(drhss[h]), _split(sols[h]), nt_dims), 0.0) for h in heads]
        dbeta_t = jnp.zeros((C, LANE), F32)
        dgc_t = jnp.zeros((C, LANE), F32)
        for h in heads:
            q, k, v, bcol, dm, kk, qk, egc, eend = qs[h], ks[h], vs[h], bcols[h], dms[h], kks[h], qks[h], egcs[h], eends[h]
            drv, drk = drhss[h][:, :DK], drhss[h][:, DK:]
            dn, dp, dqg, dkend = dns[h], dps[h], dqgs[h], dkends[h]
            rk = _rowsum(drk * k)
            dkk = dn * (bcol[:, :C] * dm)
            dqk = dp * dm
            e = (dn * (bcol[:, :C] * kk) + dp * qk) * dm
            tk = _rowsum(dkend * kends[h])
            dgc = rk * bcol * egc + _rowsum(e) - _rowsum(e.T) + _rowsum(dqg * qgs[h]) - tk
            dgc = dgc + jnp.where(lastrow, dglasts[h] + jnp.sum(tk, keepdims=True), 0.0)
            dbeta = _rowsum(drv * v) + rk * egc + _rowsum(dn * kk * dm)
            dqkv_ref[rs, sls[h]] = _dot(dqk, k) + dqg * egc
            dqkv_ref[rs, D + h * DK:D + (h + 1) * DK] = (drk * (bcol * egc) + _dot(dkk, k) + _dot_tn(dkk, k)
                                                        + _dot_tn(dqk, q) + dkend * eend)
            dqkv_ref[rs, 2 * D + h * DK:2 * D + (h + 1) * DK] = bcol * drv
            dbeta_t = jnp.where(lane == h, dbeta, dbeta_t)
            dgc_t = jnp.where(lane == H + h, dgc, dgc_t)
        dbg_ref[rs, :] = dbeta_t + _dot_hi(upper, dgc_t)
        return ds_before

    steps = nc // CHUNKS_PER_STEP_BWD
    rows = CHUNKS_PER_STEP_BWD * C
    rev = lambda n: steps - 1 - n
    blk = lambda j: pl.BlockSpec((rows, D), lambda n: (rev(n), j))
    return _call(
        body, name="delta_bwd", grid=(steps,),
        in_specs=[blk(0), blk(1), blk(2), pl.BlockSpec((rows, LANE), lambda n: (rev(n), 0)),
                  pl.BlockSpec((CHUNKS_PER_STEP_BWD, H, DK, DK), lambda n: (rev(n), 0, 0, 0)),
                  pl.BlockSpec((CHUNKS_PER_STEP_BWD, H, C, C), lambda n: (rev(n), 0, 0, 0)),
                  pl.BlockSpec((rows, D), lambda n: (rev(n), 0))],
        out_specs=(pl.BlockSpec((rows, 3 * D), lambda n: (rev(n), 0)),
                   pl.BlockSpec((rows, LANE), lambda n: (rev(n), 0))),
        out_shape=(jax.ShapeDtypeStruct((lp, 3 * D), F32), jax.ShapeDtypeStruct((lp, LANE), F32)),
        scratch_shapes=[pltpu.VMEM((H, DK, DK), F32)],
        compiler_params=_cp(("arbitrary",)),
    )(qkv, qkv, qkv, bg, sall, tall, do)


def _o_post_fwd(o, proj, dn_w, w_dn):
    lp = o.shape[0]
    te = _pick(lp, (640, 320))

    def body(o_ref, za_ref, w_ref, wdn_ref, out_ref, ya_ref):
        za = za_ref[...]
        gate = za * _sig(za)
        for h in range(H):
            sl = slice(h * DK, (h + 1) * DK)
            oh = o_ref[:, sl]
            r = lax.rsqrt(jnp.mean(oh * oh, axis=-1, keepdims=True) + EPS)
            out_ref[:, sl] = (oh * r * w_ref[...] * gate[:, sl]).astype(BF16)
        ya_ref[...] = _dot(out_ref[...], wdn_ref[...])

    row = pl.BlockSpec((te, D), lambda i: (i, 0))
    return _call(
        body, name="o_post_fwd", grid=(lp // te,),
        in_specs=[row, pl.BlockSpec((te, D), lambda i: (i, CB_ZA)), pl.BlockSpec((1, DK), lambda i: (0, 0)),
                  pl.BlockSpec((D, D), lambda i: (0, 0))],
        out_specs=(row, row),
        out_shape=(jax.ShapeDtypeStruct((lp, D), BF16), jax.ShapeDtypeStruct((lp, D), F32)),
        compiler_params=_cp(("parallel",)),
    )(o, proj, dn_w, w_dn)


def _o_post_bwd(dy_a, w_dn, o, proj, dn_w, dproj):
    lp = o.shape[0]
    te = _pick(lp, (320,))

    def body(dya_ref, wdn_ref, o_ref, za_ref, w_ref, _, do_ref, dza_ref, dw_ref, don_ref):
        @pl.when(pl.program_id(0) == 0)
        def _():
            dw_ref[...] = jnp.zeros_like(dw_ref)

        don_ref[...] = _dot_nt(dya_ref[...], wdn_ref[...])
        za = za_ref[...]
        sz = _sig(za)
        gate, dgate = za * sz, _dsilu(za, sz)
        w = w_ref[...]
        dw = jnp.zeros((1, DK), F32)
        for h in range(H):
            sl = slice(h * DK, (h + 1) * DK)
            oh, g = o_ref[:, sl], don_ref[:, sl]
            r = lax.rsqrt(jnp.mean(oh * oh, axis=-1, keepdims=True) + EPS)
            ohat = oh * r
            dza_ref[:, sl] = (g * ohat * w * dgate[:, sl]).astype(BF16)
            don = g * gate[:, sl]
            dw = dw + _colsum(don * ohat)
            dohat = don * w
            do_ref[:, sl] = r * (dohat - ohat * jnp.mean(dohat * ohat, axis=-1, keepdims=True))
        dw_ref[...] += dw

    return _call(
        body, name="o_post_bwd", grid=(lp // te,),
        in_specs=[pl.BlockSpec((te, D), lambda i: (i, 0)), pl.BlockSpec((D, D), lambda i: (0, 0)),
                  pl.BlockSpec((te, D), lambda i: (i, 0)),
                  pl.BlockSpec((te, D), lambda i: (i, CB_ZA)), pl.BlockSpec((1, DK), lambda i: (0, 0)),
                  pl.BlockSpec(memory_space=pl.ANY)],
        out_specs=(pl.BlockSpec((te, D), lambda i: (i, 0)), pl.BlockSpec((te, D), lambda i: (i, CB_ZA)),
                   pl.BlockSpec((1, DK), lambda i: (0, 0))),
        out_shape=(jax.ShapeDtypeStruct((lp, D), F32), jax.ShapeDtypeStruct(dproj.shape, dproj.dtype),
                   jax.ShapeDtypeStruct((1, DK), F32)),
        input_output_aliases={5: 1},
        scratch_shapes=[pltpu.VMEM((te, D), F32)],
        compiler_params=_cp(("arbitrary",)),
    )(dy_a, w_dn, o, proj, dn_w, dproj)


def _qkv_conv_bwd(proj, dqkv, conv_w, dproj):
    lp = proj.shape[0]
    te = _pick(lp, (640, 320))
    hb = te // HALO_Q
    nt = lp // te
    last_hb = lp // HALO_Q - 1

    def body(main_ref, prev_ref, next_ref, dmain_ref, dnext_ref, cw_ref, _, dpre_ref, dcw_ref, pre_scr, dn_scr,
             tap_scr, dco_scr, dsh_scr):
        s, i = pl.program_id(0), pl.program_id(1)

        @pl.when(i == 0)
        def _():
            dcw_ref[...] = jnp.zeros_like(dcw_ref)

        ne = te + HALO_Q
        pre_scr[:HALO_Q, :] = jnp.where(i > 0, prev_ref[...], 0.0)
        pre_scr[HALO_Q:ne, :] = main_ref[...]
        pre_scr[ne:, :] = jnp.where(i < nt - 1, next_ref[...], 0.0)
        dn_scr[:te, :] = dmain_ref[...]
        dn_scr[te:, :] = jnp.where(i < nt - 1, dnext_ref[...], 0.0)
        scale = jnp.where(s == 0, DK ** -0.5, 1.0)
        off = HALO_Q - (KQ - 1)

        def head(h, carry):
            cs = pl.ds(pl.multiple_of(h * DK, DK), DK)
            for j in range(KQ - 1):
                tap_scr[j] = pre_scr[off + j:off + j + ne, cs]
            taps = [tap_scr[j] for j in range(KQ - 1)] + [pre_scr[HALO_Q:, cs]]
            co = cw_ref[0:1, cs] * taps[0]
            for j in range(1, KQ):
                co = co + cw_ref[j:j + 1, cs] * taps[j]
            sg = _sig(co)
            a = co * sg
            g = dn_scr[:, cs]
            r = lax.rsqrt(_rowsum(a * a) + EPS)
            yhat = a * r
            da = jnp.where(s == 2, g, (scale * r) * (g - yhat * _rowsum(g * yhat)))
            dco = da * _dsilu(co, sg)
            dco_scr[...] = dco
            for j in range(KQ - 1):
                dsh_scr[j] = dco_scr[KQ - 1 - j:KQ - 1 - j + te, :]
            dpre = cw_ref[KQ - 1:KQ, cs] * dco[:te, :]
            for j in range(KQ - 1):
                dpre = dpre + cw_ref[j:j + 1, cs] * dsh_scr[j]
            dpre_ref[:, cs] = dpre.astype(BF16)
            dcw_ref[:, cs] += jnp.concatenate([_colsum(dco[:te] * taps[j][:te]) for j in range(KQ)], axis=0)
            return carry

        lax.fori_loop(0, H, head, 0, unroll=True)

    return _call(
        body, name="qkv_conv_bwd", grid=(3, nt),
        in_specs=[pl.BlockSpec((te, D), lambda s, i: (i, s)),
                  pl.BlockSpec((HALO_Q, D), lambda s, i: (jnp.maximum(i * hb - 1, 0), s)),
                  pl.BlockSpec((HALO_Q, D), lambda s, i: (jnp.minimum((i + 1) * hb, last_hb), s)),
                  pl.BlockSpec((te, D), lambda s, i: (i, s)),
                  pl.BlockSpec((HALO_Q, D), lambda s, i: (jnp.minimum((i + 1) * hb, last_hb), s)),
                  pl.BlockSpec((KQ, D), lambda s, i: (0, s)),
                  pl.BlockSpec(memory_space=pl.ANY)],
        out_specs=(pl.BlockSpec((te, D), lambda s, i: (i, s)), pl.BlockSpec((KQ, D), lambda s, i: (0, s))),
        out_shape=(jax.ShapeDtypeStruct(dproj.shape, dproj.dtype), jax.ShapeDtypeStruct((KQ, 3 * D), F32)),
        input_output_aliases={6: 0},
        scratch_shapes=[pltpu.VMEM((te + 2 * HALO_Q, D), F32), pltpu.VMEM((te + HALO_Q, D), F32),
                        pltpu.VMEM((KQ - 1, te + HALO_Q, DK), F32), pltpu.VMEM((te + HALO_Q, DK), F32),
                        pltpu.VMEM((KQ - 1, te, DK), F32)],
        compiler_params=_cp(("arbitrary", "arbitrary")),
    )(proj, proj, proj, dqkv, dqkv, conv_w, dproj)


def _ba_bwd(dbg, ba, ab, pad):
    lp = ba.shape[0]
    te = _pick(lp, (640, 320))

    def body(dbg_ref, ba_ref, ab_ref, dba_ref, dab_ref):
        i = pl.program_id(0)

        @pl.when(i == 0)
        def _():
            dab_ref[...] = jnp.zeros_like(dab_ref)

        ba, ab = ba_ref[...], ab_ref[...]
        lane = lax.broadcasted_iota(jnp.int32, ba.shape, 1)
        rows = i * te + lax.broadcasted_iota(jnp.int32, ba.shape, 0)
        g = jnp.where((lane < 2 * H) & (rows >= pad), dbg_ref[...], 0.0)
        sb = _sig(ba)
        z = ba + ab[1:2, :]
        sp = jnp.maximum(z, 0.0) + jnp.log(1.0 + jnp.exp(-jnp.abs(z)))
        nea = -jnp.exp(ab[0:1, :])
        dz = g * nea * _sig(z)
        dba_ref[...] = jnp.where(lane < H, g * sb * (1.0 - sb), dz).astype(BF16)
        is_g = (lane >= H) & (lane < 2 * H)
        dab_ref[...] += jnp.concatenate([_colsum(jnp.where(is_g, g * nea * sp, 0.0)),
                                         _colsum(jnp.where(is_g, dz, 0.0))], axis=0)

    return _call(
        body, name="ba_bwd", grid=(lp // te,),
        in_specs=[pl.BlockSpec((te, LANE), lambda i: (i, 0)), pl.BlockSpec((te, LANE), lambda i: (i, 0)),
                  pl.BlockSpec((2, LANE), lambda i: (0, 0))],
        out_specs=(pl.BlockSpec((te, LANE), lambda i: (i, 0)), pl.BlockSpec((2, LANE), lambda i: (0, 0))),
        out_shape=(jax.ShapeDtypeStruct((lp, LANE), BF16), jax.ShapeDtypeStruct((2, LANE), F32)),
        compiler_params=_cp(("arbitrary",)),
    )(dbg, ba, ab)


SUBLANES = 8
CONV_RB = 64


def _fill_shifted(sh_scr, src_scr, cs):
    n = sh_scr.shape[1]
    for s in range(1, SUBLANES):
        sh_scr[s] = src_scr[s:s + n, cs]


def _shifted(sh_scr, src_scr, cs, r, r0, n):
    s, a8 = r % SUBLANES, r - r % SUBLANES
    if s == 0:
        return src_scr[r0 + a8:r0 + a8 + n, cs]
    return sh_scr[s, r0 + a8:r0 + a8 + n, :]


def _conv_b_fwd(proj, dw_w, dw_b, ln_w, ln_b, w_cf):
    lp = proj.shape[0]
    te = _pick(lp, (640, 320))
    hb = te // HALO_D

    def body(a_ref, b_ref, ha_ref, hb_ref, zb_ref, w_ref, wb_ref, lw_ref, lb_ref, wcf_ref, c1_ref, c3_ref, yb_ref,
             c0_scr, sh_scr):
        i = pl.program_id(0)
        c0_scr[:HALO_D, :] = jnp.where(i > 0, ha_ref[...] * _sig(hb_ref[...]), 0.0)
        c0_scr[HALO_D:, :] = a_ref[...] * _sig(b_ref[...])
        off = HALO_D - (KD - 1)
        def lane_block(cb, carry):
            cs = pl.ds(pl.multiple_of(cb * LANE, LANE), LANE)
            _fill_shifted(sh_scr, c0_scr, cs)
            for r0 in range(0, te, CONV_RB):
                acc = None
                for j in range(KD):
                    term = w_ref[j:j + 1, cs] * _shifted(sh_scr, c0_scr, cs, off + j, r0, CONV_RB)
                    acc = term if acc is None else acc + term
                c1_ref[r0:r0 + CONV_RB, cs] = acc + wb_ref[:, cs]
            return carry

        lax.fori_loop(0, D // LANE, lane_block, 0)
        c1 = c1_ref[...]
        mu = jnp.mean(c1, axis=-1, keepdims=True)
        xc = c1 - mu
        c2 = xc * lax.rsqrt(jnp.mean(xc * xc, axis=-1, keepdims=True) + EPS) * lw_ref[...] + lb_ref[...]
        zb = zb_ref[...]
        c3 = (c2 * _sig(c2) * zb * _sig(zb)).astype(BF16)
        c3_ref[...] = c3
        yb_ref[...] = _dot(c3, wcf_ref[...])

    vec = pl.BlockSpec((1, D), lambda i: (0, 0))
    row = pl.BlockSpec((te, D), lambda i: (i, 0))
    return _call(
        body, name="conv_b_fwd", grid=(lp // te,),
        in_specs=[pl.BlockSpec((te, D), lambda i: (i, CB_GA_)), pl.BlockSpec((te, D), lambda i: (i, CB_GB_)),
                  pl.BlockSpec((HALO_D, D), lambda i: (jnp.maximum(i * hb - 1, 0), CB_GA_)),
                  pl.BlockSpec((HALO_D, D), lambda i: (jnp.maximum(i * hb - 1, 0), CB_GB_)),
                  pl.BlockSpec((te, D), lambda i: (i, CB_ZB)),
                  pl.BlockSpec((KD, D), lambda i: (0, 0)), vec, vec, vec, pl.BlockSpec((D, D), lambda i: (0, 0))],
        out_specs=(row, row, row),
        out_shape=(jax.ShapeDtypeStruct((lp, D), F32), jax.ShapeDtypeStruct((lp, D), BF16),
                   jax.ShapeDtypeStruct((lp, D), F32)),
        scratch_shapes=[pltpu.VMEM((te + HALO_D, D), F32), pltpu.VMEM((SUBLANES, te + HALO_D - SUBLANES, LANE), F32)],
        compiler_params=_cp(("parallel",), vmem_mib=52),
    )(proj, proj, proj, proj, proj, dw_w, dw_b, ln_w, ln_b, w_cf)


def _conv_b_bwd1(dy_b, w_cf, c1, proj, ln_w, ln_b, dproj):
    lp = c1.shape[0]
    te = _pick(lp, (320,))

    def body(dyb_ref, wcf_ref, c1_ref, zb_ref, lw_ref, lb_ref, _, dc1_ref, dzb_ref, sums_ref):
        @pl.when(pl.program_id(0) == 0)
        def _():
            sums_ref[...] = jnp.zeros_like(sums_ref)

        c1, g = c1_ref[...], _dot_nt(dyb_ref[...], wcf_ref[...])
        mu = jnp.mean(c1, axis=-1, keepdims=True)
        xc = c1 - mu
        rstd = lax.rsqrt(jnp.mean(xc * xc, axis=-1, keepdims=True) + EPS)
        xh = xc * rstd
        lw = lw_ref[...]
        c2 = xh * lw + lb_ref[...]
        s2 = _sig(c2)
        zb = zb_ref[...]
        sz = _sig(zb)
        dc2 = g * (zb * sz) * _dsilu(c2, s2)
        dzb_ref[...] = (g * (c2 * s2) * _dsilu(zb, sz)).astype(BF16)
        dxh = dc2 * lw
        dc1 = rstd * (dxh - jnp.mean(dxh, axis=-1, keepdims=True) - xh * jnp.mean(dxh * xh, axis=-1, keepdims=True))
        dc1_ref[...] = dc1
        sums_ref[...] += jnp.concatenate([_colsum(dc2 * xh), _colsum(dc2), _colsum(dc1)], axis=0)

    vec = pl.BlockSpec((1, D), lambda i: (0, 0))
    return _call(
        body, name="conv_b_bwd1", grid=(lp // te,),
        in_specs=[pl.BlockSpec((te, D), lambda i: (i, 0)), pl.BlockSpec((D, D), lambda i: (0, 0)),
                  pl.BlockSpec((te, D), lambda i: (i, 0)),
                  pl.BlockSpec((te, D), lambda i: (i, CB_ZB)), vec, vec, pl.BlockSpec(memory_space=pl.ANY)],
        out_specs=(pl.BlockSpec((te, D), lambda i: (i, 0)), pl.BlockSpec((te, D), lambda i: (i, CB_ZB)),
                   pl.BlockSpec((3, D), lambda i: (0, 0))),
        out_shape=(jax.ShapeDtypeStruct((lp, D), F32), jax.ShapeDtypeStruct(dproj.shape, dproj.dtype),
                   jax.ShapeDtypeStruct((3, D), F32)),
        input_output_aliases={6: 1},
        compiler_params=_cp(("arbitrary",)),
    )(dy_b, w_cf, c1, proj, ln_w, ln_b, dproj)


def _conv_b_bwd2(dc1, proj, dw_w, dproj):
    lp = dc1.shape[0]
    te = _pick(lp, (640, 320))
    hb = te // HALO_D
    nt = lp // te
    last_hb = lp // HALO_D - 1

    def body(g_ref, gn_ref, a_ref, b_ref, ha_ref, hb_ref, w_ref, _, dab_ref, dw_ref, c0_scr, g_scr, dc0_scr,
             csh_scr, gsh_scr):
        i = pl.program_id(0)

        @pl.when(i == 0)
        def _():
            dw_ref[...] = jnp.zeros_like(dw_ref)

        a, b = a_ref[...], b_ref[...]
        sb = _sig(b)
        c0_scr[:HALO_D, :] = jnp.where(i > 0, ha_ref[...] * _sig(hb_ref[...]), 0.0)
        c0_scr[HALO_D:, :] = a * sb
        g_scr[:te, :] = g_ref[...]
        g_scr[te:, :] = jnp.where(i < nt - 1, gn_ref[...], 0.0)
        off = HALO_D - (KD - 1)
        def lane_block(cb, carry):
            cs = pl.ds(pl.multiple_of(cb * LANE, LANE), LANE)
            _fill_shifted(csh_scr, c0_scr, cs)
            _fill_shifted(gsh_scr, g_scr, cs)
            for r0 in range(0, te, CONV_RB):
                acc = None
                for j in range(KD):
                    term = w_ref[j:j + 1, cs] * _shifted(gsh_scr, g_scr, cs, KD - 1 - j, r0, CONV_RB)
                    acc = term if acc is None else acc + term
                dc0_scr[r0:r0 + CONV_RB, cs] = acc
            parts = [None] * KD
            for r0 in range(0, te, CONV_RB):
                g = g_scr[r0:r0 + CONV_RB, cs].reshape(CONV_RB // SUBLANES, SUBLANES, LANE)
                for j in range(KD):
                    x = _shifted(csh_scr, c0_scr, cs, off + j, r0, CONV_RB)
                    p = jnp.sum(g * x.reshape(CONV_RB // SUBLANES, SUBLANES, LANE), axis=0)
                    parts[j] = p if parts[j] is None else parts[j] + p
            dw_ref[:, cs] += jnp.concatenate([_colsum(p) for p in parts], axis=0)
            return carry

        lax.fori_loop(0, D // LANE, lane_block, 0)
        dc0 = dc0_scr[...]
        dab_ref[:, :D] = (dc0 * sb).astype(BF16)
        dab_ref[:, D:] = (dc0 * a * sb * (1.0 - sb)).astype(BF16)

    return _call(
        body, name="conv_b_bwd2", grid=(nt,),
        in_specs=[pl.BlockSpec((te, D), lambda i: (i, 0)),
                  pl.BlockSpec((HALO_D, D), lambda i: (jnp.minimum((i + 1) * hb, last_hb), 0)),
                  pl.BlockSpec((te, D), lambda i: (i, CB_GA_)), pl.BlockSpec((te, D), lambda i: (i, CB_GB_)),
                  pl.BlockSpec((HALO_D, D), lambda i: (jnp.maximum(i * hb - 1, 0), CB_GA_)),
                  pl.BlockSpec((HALO_D, D), lambda i: (jnp.maximum(i * hb - 1, 0), CB_GB_)),
                  pl.BlockSpec((KD, D), lambda i: (0, 0)), pl.BlockSpec(memory_space=pl.ANY)],
        out_specs=(pl.BlockSpec((te, 2 * D), lambda i: (i, CB_GA_ // 2)), pl.BlockSpec((KD, D), lambda i: (0, 0))),
        out_shape=(jax.ShapeDtypeStruct(dproj.shape, dproj.dtype), jax.ShapeDtypeStruct((KD, D), F32)),
        input_output_aliases={7: 0},
        scratch_shapes=[pltpu.VMEM((te + HALO_D, D), F32), pltpu.VMEM((te + HALO_D, D), F32), pltpu.VMEM((te, D), F32),
                        pltpu.VMEM((SUBLANES, te + HALO_D - SUBLANES, LANE), F32),
                        pltpu.VMEM((SUBLANES, te + HALO_D - SUBLANES, LANE), F32)],
        compiler_params=_cp(("arbitrary",)),
    )(dc1, dc1, proj, proj, proj, proj, dw_w, dproj)


def _merge_fwd(y_a, y_b, proj, b_cf, w_o):
    lp = y_a.shape[0]
    te = _pick(lp, (320,))

    def body(ya_ref, yb_ref, ga_ref, gb_ref, bias_ref, wo_ref, out_ref, z_ref):
        merged = (_sig(ga_ref[...]) * ya_ref[...] + _sig(gb_ref[...]) * (yb_ref[...] + bias_ref[...])).astype(BF16)
        out_ref[...] = merged
        z_ref[...] = _dot(merged, wo_ref[...])

    row = lambda j: pl.BlockSpec((te, D), lambda i: (i, j))
    return _call(
        body, name="merge_fwd", grid=(lp // te,),
        in_specs=[row(0), row(0), row(CB_MA), row(CB_MB), pl.BlockSpec((1, D), lambda i: (0, 0)),
                  pl.BlockSpec((D, D), lambda i: (0, 0))],
        out_specs=(row(0), row(0)),
        out_shape=(jax.ShapeDtypeStruct((lp, D), BF16), jax.ShapeDtypeStruct((lp, D), F32)),
        compiler_params=_cp(("parallel",)),
    )(y_a, y_b, proj, proj, b_cf, w_o)


def _merge_bwd(dx_out_b, w_o, y_a, y_b, proj, b_cf):
    lp = y_a.shape[0]
    te = _pick(lp, (320,))

    def body(dx_ref, wo_ref, ya_ref, yb_ref, ga_ref, gb_ref, bias_ref, dya_ref, dyb_ref, dg_ref, db_ref):
        @pl.when(pl.program_id(0) == 0)
        def _():
            db_ref[...] = jnp.zeros_like(db_ref)

        dm = _dot_nt(dx_ref[...], wo_ref[...])
        sa, sb = _sig(ga_ref[...]), _sig(gb_ref[...])
        dyb = sb * dm
        dya_ref[...] = (sa * dm).astype(BF16)
        dyb_ref[...] = dyb.astype(BF16)
        dg_ref[:, :D] = (dm * ya_ref[...] * sa * (1.0 - sa)).astype(BF16)
        dg_ref[:, D:] = (dm * (yb_ref[...] + bias_ref[...]) * sb * (1.0 - sb)).astype(BF16)
        db_ref[...] += _colsum(dyb)

    row = lambda j: pl.BlockSpec((te, D), lambda i: (i, j))
    act = jax.ShapeDtypeStruct((lp, D), BF16)
    return _call(
        body, name="merge_bwd", grid=(lp // te,),
        in_specs=[row(0), pl.BlockSpec((D, D), lambda i: (0, 0)), row(0), row(0), row(CB_MA), row(CB_MB),
                  pl.BlockSpec((1, D), lambda i: (0, 0))],
        out_specs=(row(0), row(0), pl.BlockSpec((te, 2 * D), lambda i: (i, CB_MA // 2)),
                   pl.BlockSpec((1, D), lambda i: (0, 0))),
        out_shape=(act, act, jax.ShapeDtypeStruct((lp, NCB * D), BF16), jax.ShapeDtypeStruct((1, D), F32)),
        compiler_params=_cp(("arbitrary",)),
    )(dx_out_b, w_o, y_a, y_b, proj, proj, b_cf)


def _final_fwd_bwd(x_ext, z, target, final_w):
    lp = x_ext.shape[0]
    te = _pick(lp, (640,))
    nsub = te // LANE

    def body(x_ref, z_ref, *rest):
        t_refs, (w_ref, dx_ref, dxb_ref, loss_ref, dw_ref) = rest[:nsub], rest[nsub:]
        i = pl.program_id(0)

        @pl.when(i == 0)
        def _():
            loss_ref[...] = jnp.zeros_like(loss_ref)
            dw_ref[...] = jnp.zeros_like(dw_ref)

        w = w_ref[...]
        for k in range(nsub):
            rs = slice(k * LANE, (k + 1) * LANE)
            xo = x_ref[rs, :] + z_ref[rs, :]
            r = lax.rsqrt(jnp.mean(xo * xo, axis=-1, keepdims=True) + EPS)
            xhat = xo * r
            err = xhat * w - t_refs[k][...]
            if k == 0:
                err = jnp.where(i > 0, err, 0.0)
            loss_ref[...] += 0.5 * jnp.sum(jnp.mean(err * err, axis=-1, keepdims=True), keepdims=True)
            dy = err * (1.0 / D)
            dw_ref[...] += _colsum(dy * xhat)
            dxn = dy * w
            dx = r * (dxn - xhat * jnp.mean(dxn * xhat, axis=-1, keepdims=True))
            dx_ref[rs, :] = dx
            dxb_ref[rs, :] = dx.astype(BF16)

    piece = lambda k: pl.BlockSpec((LANE, D), lambda i: (jnp.maximum(i * nsub + k - 1, 0), 0))
    row = pl.BlockSpec((te, D), lambda i: (i, 0))
    return _call(
        body, name="final_fwd_bwd", grid=(lp // te,),
        in_specs=[row, row] + [piece(k) for k in range(nsub)] + [pl.BlockSpec((1, D), lambda i: (0, 0))],
        out_specs=(row, row, pl.BlockSpec((1, 1), lambda i: (0, 0)), pl.BlockSpec((1, D), lambda i: (0, 0))),
        out_shape=(jax.ShapeDtypeStruct((lp, D), F32), jax.ShapeDtypeStruct((lp, D), BF16),
                   jax.ShapeDtypeStruct((1, 1), F32), jax.ShapeDtypeStruct((1, D), F32)),
        compiler_params=_cp(("arbitrary",)),
    )(x_ext, z, *([target] * nsub), final_w)


def _prenorm_bwd(dh, x_ext, dx_out, norm_w, seq):
    lp = x_ext.shape[0]
    te = _pick(lp, (640,))
    nt = lp // te
    head = lp - seq

    def body(dh_ref, x_ref, dxo_ref, w_ref, gx_ref, head_ref, dw_ref, stage, sems):
        i = pl.program_id(0)
        slot = i % 2

        def first_copy():
            return pltpu.make_async_copy(stage.at[0, pl.ds(head, te - head)], gx_ref.at[pl.ds(0, te - head)], sems.at[0])

        def tile_copy(step, s):
            return pltpu.make_async_copy(stage.at[s], gx_ref.at[pl.ds(pl.multiple_of(step * te - head, LANE), te)],
                                         sems.at[s])

        @pl.when(i == 0)
        def _():
            dw_ref[...] = jnp.zeros_like(dw_ref)

        @pl.when(i == 2)
        def _():
            first_copy().wait()

        @pl.when(i > 2)
        def _():
            tile_copy(i - 2, slot).wait()

        x, dh = x_ref[...], dh_ref[...]
        r = lax.rsqrt(jnp.mean(x * x, axis=-1, keepdims=True) + EPS)
        xhat = x * r
        dxn = dh * w_ref[...]
        stage[slot] = dxo_ref[...] + r * (dxn - xhat * jnp.mean(dxn * xhat, axis=-1, keepdims=True))
        dw_ref[...] += _colsum(dh * xhat)

        @pl.when(i == 0)
        def _():
            head_ref[...] = stage[0, :head, :]
            first_copy().start()

        @pl.when(i > 0)
        def _():
            tile_copy(i, slot).start()

        @pl.when(i == nt - 1)
        def _():
            if nt >= 2:
                (first_copy() if nt == 2 else tile_copy(nt - 2, (nt - 2) % 2)).wait()
            (first_copy() if nt == 1 else tile_copy(nt - 1, (nt - 1) % 2)).wait()

    row = pl.BlockSpec((te, D), lambda i: (i, 0))
    return _call(
        body, name="prenorm_bwd", grid=(nt,),
        in_specs=[row, row, row, pl.BlockSpec((1, D), lambda i: (0, 0))],
        out_specs=(pl.BlockSpec(memory_space=pl.ANY), pl.BlockSpec((head, D), lambda i: (0, 0)),
                   pl.BlockSpec((1, D), lambda i: (0, 0))),
        out_shape=(jax.ShapeDtypeStruct((seq, D), F32), jax.ShapeDtypeStruct((head, D), F32),
                   jax.ShapeDtypeStruct((1, D), F32)),
        scratch_shapes=[pltpu.VMEM((2, te, D), F32), pltpu.SemaphoreType.DMA((2,))],
        compiler_params=_cp(("arbitrary",)),
    )(dh, x_ext, dx_out, norm_w)


def _adam_reduce(parts, w, m, v, name):
    r, n = w.shape
    tr = _pick(r, (128,)) if r % 128 == 0 else r

    def body(p_ref, w_ref, m_ref, v_ref, g_ref, d_ref, m2_ref, v2_ref):
        g = p_ref[0]
        for s in range(1, NDEV):
            g = g + p_ref[s]
        _adam_write(g, w_ref, m_ref, v_ref, g_ref, d_ref, m2_ref, v2_ref)

    blk = pl.BlockSpec((tr, n), lambda i: (i, 0))
    out = jax.ShapeDtypeStruct((r, n), F32)
    return _call(
        body, name=name, grid=(r // tr,),
        in_specs=[pl.BlockSpec((NDEV, tr, n), lambda i: (0, i, 0)), blk, blk, blk],
        out_specs=(blk, blk, blk, blk), out_shape=(out, out, out, out),
        compiler_params=_cp(("parallel",)),
    )(parts, w, m, v)


def _adam_write(g, w_ref, m_ref, v_ref, g_ref, d_ref, m2_ref, v2_ref):
    c1 = 1.0 - ADAM_B1 ** ADAM_STEP
    c2 = 1.0 - ADAM_B2 ** ADAM_STEP
    m2 = ADAM_B1 * m_ref[...] + (1.0 - ADAM_B1) * g
    v2 = ADAM_B2 * v_ref[...] + (1.0 - ADAM_B2) * (g * g)
    g_ref[...] = g
    m2_ref[...] = m2
    v2_ref[...] = v2
    d_ref[...] = -ADAM_LR * ((m2 / c1) / (jnp.sqrt(v2 / c2) + ADAM_EPS) + ADAM_WD * w_ref[...])


def _adam_chips(own, recv, w, m, v, name):
    r, n = w.shape
    tr, tc = _shard_tile(r, n)

    def body(own_ref, p_ref, w_ref, m_ref, v_ref, g_ref, d_ref, m2_ref, v2_ref):
        my_chip = 2 * lax.axis_index("x") + lax.axis_index("y")
        g = None
        for j in range(NCHIP):
            part = jnp.where(my_chip == j, own_ref[...], p_ref[j].astype(F32))
            g = part if g is None else g + part
        _adam_write(g, w_ref, m_ref, v_ref, g_ref, d_ref, m2_ref, v2_ref)

    blk = pl.BlockSpec((tr, tc), lambda i, k: (i, k))
    out = jax.ShapeDtypeStruct((r, n), F32)
    return _call(
        body, name=name, grid=(r // tr, n // tc),
        in_specs=[blk, pl.BlockSpec((NCHIP, tr, tc), lambda i, k: (0, i, k)), blk, blk, blk],
        out_specs=(blk, blk, blk, blk), out_shape=(out, out, out, out),
        compiler_params=_cp(("parallel", "parallel")),
    )(own, recv, w, m, v)


SMALL = ("norm_w", "a_log", "dt_bias", "dn_norm_w", "dw_b", "ln_w", "ln_b", "b_cf_out", "final_norm_w")


def kernel(x, meta, norm_w, w_in, conv_qkv_w, a_log, dt_bias, dn_norm_w, w_dn_out, dw_w, dw_b, ln_w, ln_b, w_cf_out, b_cf_out, w_o, final_norm_w, loss_target, m_meta, m_norm_w, m_w_in, m_conv_qkv_w, m_a_log, m_dt_bias, m_dn_norm_w, m_w_dn_out, m_dw_w, m_dw_b, m_ln_w, m_ln_b, m_w_cf_out, m_b_cf_out, m_w_o, m_final_norm_w, v_meta, v_norm_w, v_w_in, v_conv_qkv_w, v_a_log, v_dt_bias, v_dn_norm_w, v_w_dn_out, v_dw_w, v_dw_b, v_ln_w, v_ln_b, v_w_cf_out, v_b_cf_out, v_w_o, v_final_norm_w):
    seq = x.shape[1]
    pad = (-(seq + NMETA)) % LANE
    in_w = w_in.shape[2] * NDEV
    n_qkvz = 4 * D
    n_ba = 2 * H

    me = 4 * lax.axis_index("x") + 2 * lax.axis_index("y") + lax.axis_index("c")
    late = [w_dn_out[0].astype(BF16), w_cf_out[0].astype(BF16), w_o[0].astype(BF16), conv_qkv_w[0], dw_w[0]]
    late_lands = [lax.dynamic_update_index_in_dim(jnp.zeros((NDEV,) + b.shape, b.dtype), b, me, 0) for b in late]
    w_in_g, meta_g = _gather_two_level([w_in[0].astype(BF16).T, meta], "gather_weights")
    late_send, late_recv, late_thru, late_land_thru, _ = _split_start(
        _all_copies, NDEV - 1, "gather_late_start", late, late_lands, meta_g)
    w_full_t = w_in_g.reshape(in_w, D)
    c_glu = n_qkvz + n_ba
    c_zb, c_mg = c_glu + 2 * D, c_glu + 3 * D
    w_main_t = jnp.concatenate([w_full_t[:n_qkvz], w_full_t[c_glu:c_zb], w_full_t[c_mg:], w_full_t[c_zb:c_mg]],
                               axis=0)
    w_ba_t = jnp.pad(w_full_t[n_qkvz:n_qkvz + n_ba], ((0, LANE - n_ba), (0, 0)))
    meta_full = jnp.transpose(meta_g, (1, 0, 2)).reshape(NMETA, D)
    ab = jnp.pad(jnp.concatenate([a_log, dt_bias], axis=0), ((0, 0), (H, LANE - 2 * H)))

    x_ext = jnp.concatenate([jnp.zeros((pad, D), F32), meta_full, x[0]], axis=0)

    proj, ba, h = _proj_fwd(x_ext, norm_w, w_main_t, w_ba_t)
    _, (w_dn_g, w_cf_g, w_o_g, cqw_g, dww_g) = _split_wait(
        _all_copies, "gather_late_wait", late_send, late_recv, late_thru, late_land_thru, ba)
    w_dn, w_cf, w_oo = (t.reshape(D, D) for t in (w_dn_g, w_cf_g, w_o_g))
    cqw = jnp.transpose(cqw_g, (1, 0, 2)).reshape(KQ, 3 * D)
    dww = jnp.transpose(dww_g, (1, 0, 2)).reshape(KD, D)
    qkv, bg = _qkv_conv_fwd(proj, ba, cqw, ab, pad)
    o, sall, tall = _delta_fwd(qkv, bg)
    o_n, y_a = _o_post_fwd(o, proj, dn_norm_w, w_dn)
    c1, c3, y_b = _conv_b_fwd(proj, dww, dw_b, ln_w, ln_b, w_cf)
    merged, z = _merge_fwd(y_a, y_b, proj, b_cf_out, w_oo)
    dx_out, dx_out_b, loss_part, g_final_w = _final_fwd_bwd(x_ext, z, loss_target[0], final_norm_w.reshape(1, D))

    g_w_o = _mm_tn(merged, dx_out_b, "g_w_o_mm")
    dy_a, dy_b, dproj, g_b_cf = _merge_bwd(dx_out_b, w_oo, y_a, y_b, proj, b_cf_out)
    g_w_cf = _mm_tn(c3, dy_b, "g_w_cf_mm")
    g_w_dn = _mm_tn(o_n, dy_a, "g_w_dn_mm")
    dc1, dproj, sums_b = _conv_b_bwd1(dy_b, w_cf, c1, proj, ln_w, ln_b, dproj)
    dproj, g_dw_w = _conv_b_bwd2(dc1, proj, dww, dproj)
    do, dproj, g_dn_w = _o_post_bwd(dy_a, w_dn, o, proj, dn_norm_w, dproj)
    dqkv, dbg = _delta_bwd(qkv, bg, sall, tall, do)
    dproj, g_cqw = _qkv_conv_bwd(proj, dqkv, cqw, dproj)
    dba, dab = _ba_bwd(dbg, ba, ab, pad)
    g_w_main_t = _mm_tn(dproj, h, "g_w_main_mm")
    g_w_ba_t = _mm_tn(dba, h, "g_w_ba_mm")

    g_w_full_t = jnp.concatenate([g_w_main_t[:n_qkvz], g_w_ba_t[:n_ba], g_w_main_t[CB_GA_ * D:CB_MA * D],
                                  g_w_main_t[CB_ZB * D:], g_w_main_t[CB_MA * D:CB_ZB * D]], axis=0)
    big = [t.reshape(NCHIP, 2, t.shape[0] // NDEV, D) for t in (g_w_full_t, g_w_dn, g_w_cf, g_w_o)]
    sw_send, sw_recv, big_thru, sw_land, sw_token = _split_start(
        _sibling_copies, NCHIP, "swap_sibling_start", big,
        [lax.empty((NCHIP,) + t.shape[2:], t.dtype) for t in big], g_w_ba_t)
    dh = _dh_mm(dproj, dba, w_main_t, w_ba_t + sw_token[0, 0].astype(BF16), 0)
    big_back, from_sibling = _split_wait(_sibling_copies, "swap_sibling_wait", sw_send, sw_recv, big_thru, sw_land, dh)
    pairs = [_pair_add(a, g, f"pair_add_{i}") for i, (a, g) in enumerate(zip(big_back, from_sibling))]
    send_sems, recv_sems, pair_thru, land_thru, token = _split_start(
        _chip_copies, NCHIP - 1, "scatter_chips_start",
        [p for p, _ in pairs], [jnp.zeros(p.shape, p.dtype) for p, _ in pairs], g_w_ba_t)
    dh = _dh_mm(dproj, dba, w_main_t, w_ba_t + token[0, 0].astype(BF16), 1, dh)
    grad_x, dhead, g_norm_w = _prenorm_bwd(dh, x_ext, dx_out, norm_w, seq)
    _, from_chips = _split_wait(_chip_copies, "scatter_chips_wait", send_sems, recv_sems, pair_thru, land_thru,
                                g_norm_w)

    split_cols = lambda t: jnp.transpose(t.reshape(t.shape[0], NDEV, t.shape[1] // NDEV), (1, 0, 2))
    small = {"norm_w": g_norm_w, "a_log": dab[0:1, H:2 * H], "dt_bias": dab[1:2, H:2 * H], "dn_norm_w": g_dn_w,
             "dw_b": sums_b[2:3], "ln_w": sums_b[0:1], "ln_b": sums_b[1:2], "b_cf_out": g_b_cf,
             "final_norm_w": g_final_w}
    small_vec = jnp.concatenate([small[k] for k in SMALL], axis=1)
    ns = small_vec.shape[1]
    ns_pad = (-ns) % LANE
    small_vec = jnp.pad(small_vec, ((0, 0), (0, ns_pad)))
    p_meta, p_cqw, p_dww, p_small = _exchange(
        [split_cols(dhead[pad:pad + NMETA]), split_cols(g_cqw), split_cols(g_dw_w), small_vec],
        [True] * 3 + [False], "exchange_small")

    res = {}
    res["w_in"] = tuple(t.T for t in _adam_chips(pairs[0][1], from_chips[0], w_in[0].T, m_w_in[0].T, v_w_in[0].T,
                                                   "adam_w_in"))
    res["w_dn_out"] = _adam_chips(pairs[1][1], from_chips[1], w_dn_out[0], m_w_dn_out[0], v_w_dn_out[0], "adam_w_dn")
    res["w_cf_out"] = _adam_chips(pairs[2][1], from_chips[2], w_cf_out[0], m_w_cf_out[0], v_w_cf_out[0], "adam_w_cf")
    res["w_o"] = _adam_chips(pairs[3][1], from_chips[3], w_o[0], m_w_o[0], v_w_o[0], "adam_w_o")
    res["meta"] = _adam_reduce(p_meta, meta, m_meta, v_meta, "adam_meta")
    res["conv_qkv_w"] = _adam_reduce(p_cqw, conv_qkv_w[0], m_conv_qkv_w[0], v_conv_qkv_w[0], "adam_conv_qkv_w")
    res["dw_w"] = _adam_reduce(p_dww, dw_w[0], m_dw_w[0], v_dw_w[0], "adam_dw_w")
    loc = dict(norm_w=(norm_w, m_norm_w, v_norm_w), a_log=(a_log, m_a_log, v_a_log), dt_bias=(dt_bias, m_dt_bias, v_dt_bias),
               dn_norm_w=(dn_norm_w, m_dn_norm_w, v_dn_norm_w), dw_b=(dw_b, m_dw_b, v_dw_b), ln_w=(ln_w, m_ln_w, v_ln_w),
               ln_b=(ln_b, m_ln_b, v_ln_b), b_cf_out=(b_cf_out, m_b_cf_out, v_b_cf_out),
               final_norm_w=(final_norm_w, m_final_norm_w, v_final_norm_w))
    cat = lambda j: jnp.pad(jnp.concatenate([loc[k][j].reshape(1, -1) for k in SMALL], axis=1), ((0, 0), (0, ns_pad)))
    small_res = _adam_reduce(p_small, cat(0), cat(1), cat(2), "adam_small")
    off = 0
    for k in SMALL:
        wshape = loc[k][0].shape
        nk = loc[k][0].size
        res[k] = tuple(t[:, off:off + nk].reshape(wshape) for t in small_res)
        off += nk
    shaped = dict(w_in=w_in.shape, w_dn_out=w_dn_out.shape, w_cf_out=w_cf_out.shape, w_o=w_o.shape, meta=meta.shape,
                  conv_qkv_w=conv_qkv_w.shape, dw_w=dw_w.shape)
    for k, shp in shaped.items():
        res[k] = tuple(t.reshape(shp) for t in res[k])

    loss = lax.psum(loss_part[0, 0], ("x", "y", "c"))
    order = ("meta", "norm_w", "w_in", "conv_qkv_w", "a_log", "dt_bias", "dn_norm_w", "w_dn_out", "dw_w", "dw_b", "ln_w",
             "ln_b", "w_cf_out", "b_cf_out", "w_o", "final_norm_w")
    outs = [loss, grad_x[None]]
    for j in range(4):
        outs += [res[k][j] for k in order]
    return tuple(outs)
```

```python
import functools

import jax
import jax.numpy as jnp
from jax import lax
from jax.experimental import pallas as pl
from jax.experimental.pallas import tpu as pltpu

F32 = jnp.float32
BF16 = jnp.bfloat16
HI = lax.Precision.HIGHEST

D = 1024
H = 8
DK = 128
C = 64
NMETA = 16
KQ = 4
KD = 31
HALO_Q = 8
HALO_D = 32
EPS = 1e-6
NDEV = 8
LANE = 128
MIB = 1024 * 1024

ADAM_LR, ADAM_B1, ADAM_B2, ADAM_EPS, ADAM_WD, ADAM_STEP = 0.001, 0.9, 0.999, 1e-08, 0.01, 10

CB_Q, CB_K, CB_V, CB_ZA, CB_GA_, CB_GB_, CB_MA, CB_MB, CB_ZB = range(9)
NCB = 9


def _pick(n, cands):
    for c in cands:
        if n % c == 0:
            return c
    raise ValueError(f"no tile for {n}")


def _cp(sem=None, vmem_mib=40):
    kw = dict(vmem_limit_bytes=vmem_mib * MIB)
    if sem is not None:
        kw["dimension_semantics"] = sem
    return pltpu.CompilerParams(**kw)


def _call(body, **kw):
    return pl.pallas_call(body, **kw)


def _dot(a, b):
    return jnp.dot(a.astype(BF16), b.astype(BF16), preferred_element_type=F32)


def _dot_nt(a, b):
    return lax.dot_general(a.astype(BF16), b.astype(BF16), (((1,), (1,)), ((), ())), preferred_element_type=F32)


def _dot_tn(a, b):
    return lax.dot_general(a.astype(BF16), b.astype(BF16), (((0,), (0,)), ((), ())), preferred_element_type=F32)


def _dot_hi(a, b):
    return jnp.dot(a, b, precision=HI, preferred_element_type=F32)


def _sig(x):
    return 0.5 * jnp.tanh(0.5 * x) + 0.5


def _dsilu(x, s):
    return s * (1.0 + x * (1.0 - s))


def _rowsum(x):
    return jnp.sum(x, axis=-1, keepdims=True)


def _colsum(x):
    return jnp.sum(x, axis=0, keepdims=True)


def _exchange(arrs, scatter, name):
    n = len(arrs)
    out_shape = []
    for a, sc in zip(arrs, scatter):
        shp = a.shape if sc else (NDEV,) + a.shape
        out_shape.append(jax.ShapeDtypeStruct(shp, a.dtype))

    def body(*refs):
        ins, outs = refs[:n], refs[n:2 * n]
        send_sems, recv_sems, loc_sems = refs[2 * n:]
        x, y, c = lax.axis_index("x"), lax.axis_index("y"), lax.axis_index("c")
        me = 4 * x + 2 * y + c
        copies = []
        for a in range(n):
            for k in range(1, NDEV):
                px = 1 - x if (k >> 2) & 1 else x
                py = 1 - y if (k >> 1) & 1 else y
                pc = 1 - c if k & 1 else c
                src = ins[a].at[4 * px + 2 * py + pc] if scatter[a] else ins[a]
                cp = pltpu.make_async_remote_copy(
                    src_ref=src, dst_ref=outs[a].at[me],
                    send_sem=send_sems.at[a * (NDEV - 1) + k - 1], recv_sem=recv_sems.at[a * (NDEV - 1) + k - 1],
                    device_id=(px, py, pc), device_id_type=pl.DeviceIdType.MESH)
                cp.start()
                copies.append(cp)
            loc = pltpu.make_async_copy(ins[a].at[me] if scatter[a] else ins[a], outs[a].at[me], loc_sems.at[a])
            loc.start()
            copies.append(loc)
        for cp in copies:
            cp.wait()

    any_spec = pl.BlockSpec(memory_space=pl.ANY)
    return _call(
        body, name=name, out_shape=tuple(out_shape),
        in_specs=[any_spec] * n, out_specs=tuple([any_spec] * n),
        scratch_shapes=[pltpu.SemaphoreType.DMA((n * (NDEV - 1),)), pltpu.SemaphoreType.DMA((n * (NDEV - 1),)),
                        pltpu.SemaphoreType.DMA((n,))],
    )(*arrs)


NCHIP = 4


def _gather_two_level(arrs, name):
    n = len(arrs)
    per = NDEV - 1

    def body(*refs):
        ins, outs = refs[:n], refs[n:2 * n]
        send_sems, recv_sems, loc_sems = refs[2 * n:]
        x, y, c = lax.axis_index("x"), lax.axis_index("y"), lax.axis_index("c")
        me, sibling = (x, y, c), (x, y, 1 - c)
        chips = [(1 - x, y), (x, 1 - y), (1 - x, 1 - y)]

        def slot(a, px, py, pc):
            return outs[a].at[4 * px + 2 * py + pc]

        def copy(a, k, block, to, src=None):
            return pltpu.make_async_remote_copy(
                src_ref=slot(a, *block) if src is None else src, dst_ref=slot(a, *block),
                send_sem=send_sems.at[a * per + k], recv_sem=recv_sems.at[a * per + k],
                device_id=to, device_id_type=pl.DeviceIdType.MESH)

        local, sent = [], []
        for a in range(n):
            mine = pltpu.make_async_copy(ins[a], slot(a, *me), loc_sems.at[a])
            mine.start()
            local.append(mine)
            first = [copy(a, 1 + j, me, (*chip, c), src=ins[a]) for j, chip in enumerate(chips)]
            first.append(copy(a, 0, me, sibling, src=ins[a]))
            for cp in first:
                cp.start()
            sent += first
        for j, chip in enumerate(chips):
            for a in range(n):
                copy(a, 1 + j, (*chip, c), me).wait_recv()
                cp = copy(a, 4 + j, (*chip, c), sibling)
                cp.start()
                sent.append(cp)
        for a in range(n):
            copy(a, 0, sibling, me).wait_recv()
            for j, chip in enumerate(chips):
                copy(a, 4 + j, (*chip, 1 - c), me).wait_recv()
        for cp in sent:
            cp.wait_send()
        for cp in local:
            cp.wait()

    any_spec = pl.BlockSpec(memory_space=pl.ANY)
    return _call(
        body, name=name, out_shape=tuple(jax.ShapeDtypeStruct((NDEV,) + a.shape, a.dtype) for a in arrs),
        in_specs=[any_spec] * n, out_specs=tuple([any_spec] * n),
        scratch_shapes=[pltpu.SemaphoreType.DMA((n * per,)), pltpu.SemaphoreType.DMA((n * per,)),
                        pltpu.SemaphoreType.DMA((n,))],
    )(*arrs)


def _pair_add(arr4, got, name):
    _, _, r, n = arr4.shape
    tr, tc = _shard_tile(r, n)

    def body(a_ref, g_ref, p_ref, own_ref):
        c = lax.axis_index("c")
        my_chip = 2 * lax.axis_index("x") + lax.axis_index("y")
        s = jnp.where(c == 0, a_ref[0, 0], a_ref[0, 1]) + g_ref[0]
        p_ref[0] = s.astype(BF16)

        @pl.when(pl.program_id(2) == my_chip)
        def _():
            own_ref[...] = s

    return _call(
        body, name=name, grid=(r // tr, n // tc, NCHIP),
        in_specs=[pl.BlockSpec((1, 2, tr, tc), lambda i, k, j: (j, 0, i, k)),
                  pl.BlockSpec((1, tr, tc), lambda i, k, j: (j, i, k))],
        out_specs=(pl.BlockSpec((1, tr, tc), lambda i, k, j: (j, i, k)), pl.BlockSpec((tr, tc), lambda i, k, j: (i, k))),
        out_shape=(jax.ShapeDtypeStruct((NCHIP, r, n), BF16), jax.ShapeDtypeStruct((r, n), F32)),
        compiler_params=_cp(("parallel", "parallel", "arbitrary")),
    )(arr4, got)


def _shard_tile(r, n):
    return (128, n) if r % 128 == 0 else (r, 256)


def _all_copies(srcs, lands, send_sems, recv_sems):
    x, y, c = lax.axis_index("x"), lax.axis_index("y"), lax.axis_index("c")
    per = NDEV - 1
    copies = []
    for a in range(len(srcs)):
        for k in range(1, NDEV):
            px = 1 - x if (k >> 2) & 1 else x
            py = 1 - y if (k >> 1) & 1 else y
            pc = 1 - c if k & 1 else c
            copies.append(pltpu.make_async_remote_copy(
                src_ref=srcs[a], dst_ref=lands[a].at[4 * x + 2 * y + c],
                send_sem=send_sems.at[a * per + k - 1], recv_sem=recv_sems.at[a * per + k - 1],
                device_id=(px, py, pc), device_id_type=pl.DeviceIdType.MESH))
    return copies


def _split_start(make_copies, peers, name, arrs, lands, after):
    n = len(arrs)
    nsem = n * peers

    def body(*refs):
        srcs, land_in = refs[:n], refs[n:2 * n]
        send_sems, recv_sems = refs[2 * n + 1:2 * n + 3]
        token = refs[-1]
        for cp in make_copies(srcs, land_in, send_sems, recv_sems):
            cp.start()
        token[...] = jnp.zeros_like(token)

    hbm = pl.BlockSpec(memory_space=pltpu.HBM)
    sem = pl.BlockSpec(memory_space=pltpu.SEMAPHORE)
    both = list(arrs) + list(lands)
    outs = _call(
        body, name=name,
        out_shape=(pltpu.SemaphoreType.DMA((nsem,)), pltpu.SemaphoreType.DMA((nsem,)),
                   *[pltpu.HBM(a.shape, a.dtype) for a in both], jax.ShapeDtypeStruct((SUBLANES, LANE), F32)),
        in_specs=[hbm] * (2 * n) + [pl.BlockSpec(memory_space=pl.ANY)],
        out_specs=(sem, sem, *[hbm] * (2 * n), pl.BlockSpec(memory_space=pltpu.VMEM)),
        input_output_aliases={i: i + 2 for i in range(2 * n)},
        compiler_params=pltpu.CompilerParams(has_side_effects=pltpu.SideEffectType.DATAFLOW_SIDE_EFFECTING),
    )(*[pltpu.with_memory_space_constraint(t, pltpu.HBM) for t in both], after)
    return outs[0], outs[1], outs[2:2 + n], outs[2 + n:2 + 2 * n], outs[-1]


def _split_wait(make_copies, name, send_sems, recv_sems, arrs, lands, after):
    n = len(arrs)

    def body(*refs):
        srcs, land_in = refs[:n], refs[n:2 * n]
        send, recv = refs[2 * n], refs[2 * n + 1]
        for cp in make_copies(srcs, land_in, send, recv):
            cp.wait_send()
            cp.wait_recv()

    hbm = pl.BlockSpec(memory_space=pltpu.HBM)
    sem = pl.BlockSpec(memory_space=pltpu.SEMAPHORE)
    both = list(arrs) + list(lands)
    outs = _call(
        body, name=name,
        out_shape=tuple(pltpu.HBM(a.shape, a.dtype) for a in both),
        in_specs=[hbm] * (2 * n) + [sem, sem, pl.BlockSpec(memory_space=pl.ANY)], out_specs=tuple([hbm] * (2 * n)),
        input_output_aliases={i: i for i in range(2 * n)},
        compiler_params=pltpu.CompilerParams(has_side_effects=pltpu.SideEffectType.DATAFLOW_SIDE_EFFECTING),
    )(*both, send_sems, recv_sems, after)
    return outs[:n], outs[n:]


def _sibling_copies(srcs, lands, send_sems, recv_sems):
    x, y, c = lax.axis_index("x"), lax.axis_index("y"), lax.axis_index("c")
    copies = []
    for a in range(len(srcs)):
        for j in range(NCHIP):
            copies.append(pltpu.make_async_remote_copy(
                src_ref=srcs[a].at[j, 1 - c], dst_ref=lands[a].at[j],
                send_sem=send_sems.at[a * NCHIP + j], recv_sem=recv_sems.at[a * NCHIP + j],
                device_id=(x, y, 1 - c), device_id_type=pl.DeviceIdType.MESH))
    return copies


def _chip_copies(srcs, lands, send_sems, recv_sems):
    x, y, c = lax.axis_index("x"), lax.axis_index("y"), lax.axis_index("c")
    per = NCHIP - 1
    copies = []
    for a in range(len(srcs)):
        for k in range(1, NCHIP):
            px = 1 - x if (k >> 1) & 1 else x
            py = 1 - y if k & 1 else y
            copies.append(pltpu.make_async_remote_copy(
                src_ref=srcs[a].at[2 * px + py], dst_ref=lands[a].at[2 * x + y],
                send_sem=send_sems.at[a * per + k - 1], recv_sem=recv_sems.at[a * per + k - 1],
                device_id=(px, py, c), device_id_type=pl.DeviceIdType.MESH))
    return copies


def _mm_tn(a, b, name):
    t, m = a.shape
    n = b.shape[1]
    tt = _pick(t, (1664, 640, 128))
    tm = _pick(m, (1536, 1024, 512, 128))
    tn = _pick(n, (1152, 1024, 512, 128))
    nt = t // tt

    def body(a_ref, b_ref, o_ref):
        s = pl.program_id(2)
        part = _dot_tn(a_ref[...], b_ref[...])

        @pl.when(s == 0)
        def _():
            o_ref[...] = part

        @pl.when(s > 0)
        def _():
            o_ref[...] += part

    return _call(
        body, name=name, grid=(m // tm, n // tn, nt),
        in_specs=[pl.BlockSpec((tt, tm), lambda i, j, s: (s, i)), pl.BlockSpec((tt, tn), lambda i, j, s: (s, j))],
        out_specs=pl.BlockSpec((tm, tn), lambda i, j, s: (i, j)),
        out_shape=jax.ShapeDtypeStruct((m, n), F32),
        compiler_params=_cp(("parallel", "parallel", "arbitrary"), vmem_mib=52),
    )(a, b)


def _proj_fwd(x_ext, norm_w, w_main_t, w_ba_t):
    lp = x_ext.shape[0]
    n = w_main_t.shape[0]
    tm = _pick(lp, (1040, 832, 640, 320))
    tn = 1024

    def body(x_ref, nw_ref, w_ref, wba_ref, proj_ref, ba_ref, h_ref):
        @pl.when(pl.program_id(1) == 0)
        def _():
            x = x_ref[...]
            r = lax.rsqrt(jnp.mean(x * x, axis=-1, keepdims=True) + EPS)
            h = (x * r * nw_ref[...]).astype(BF16)
            h_ref[...] = h
            ba_ref[...] = _dot_nt(h, wba_ref[...])

        proj_ref[...] = _dot_nt(h_ref[...], w_ref[...])

    return _call(
        body, name="proj_fwd", grid=(lp // tm, n // tn),
        in_specs=[pl.BlockSpec((tm, D), lambda i, j: (i, 0)), pl.BlockSpec((1, D), lambda i, j: (0, 0)),
                  pl.BlockSpec((tn, D), lambda i, j: (j, 0)), pl.BlockSpec((LANE, D), lambda i, j: (0, 0))],
        out_specs=(pl.BlockSpec((tm, tn), lambda i, j: (i, j)), pl.BlockSpec((tm, LANE), lambda i, j: (i, 0)),
                   pl.BlockSpec((tm, D), lambda i, j: (i, 0))),
        out_shape=(jax.ShapeDtypeStruct((lp, n), F32), jax.ShapeDtypeStruct((lp, LANE), F32),
                   jax.ShapeDtypeStruct((lp, D), BF16)),
        compiler_params=_cp(("parallel", "arbitrary")),
    )(x_ext, norm_w, w_main_t, w_ba_t)


def _dh_mm(dproj, dba, w_main_t, w_ba_t, part, dh_so_far=None):
    lp, n = dproj.shape
    tm = _pick(lp, (1040, 832, 640, 320))
    tn = 1024
    tk = 2304
    nk = n // tk
    tiles = lp // tm
    first = (tiles + 1) // 2
    t0, nt = (0, first) if part == 0 else (first, tiles - first)
    if nt == 0:
        return dh_so_far

    def body(a_ref, ba_ref, b_ref, bba_ref, *rest):
        o_ref, acc = rest[-2:]
        kk = pl.program_id(2)

        @pl.when(kk == 0)
        def _():
            acc[...] = jnp.dot(ba_ref[...], bba_ref[...], preferred_element_type=F32)

        acc[...] += jnp.dot(a_ref[...], b_ref[...], preferred_element_type=F32)

        @pl.when(kk == nk - 1)
        def _():
            o_ref[...] = acc[...]

    prev = [] if dh_so_far is None else [dh_so_far]
    return _call(
        body, name=f"dh_mm_{part}", grid=(nt, D // tn, nk),
        in_specs=[pl.BlockSpec((tm, tk), lambda i, j, kk: (i + t0, kk)),
                  pl.BlockSpec((tm, LANE), lambda i, j, kk: (i + t0, 0)),
                  pl.BlockSpec((tk, tn), lambda i, j, kk: (kk, j)), pl.BlockSpec((LANE, tn), lambda i, j, kk: (0, j))]
                 + [pl.BlockSpec(memory_space=pl.ANY)] * len(prev),
        out_specs=pl.BlockSpec((tm, tn), lambda i, j, kk: (i + t0, j)),
        out_shape=jax.ShapeDtypeStruct((lp, D), F32),
        input_output_aliases={4: 0} if prev else {},
        scratch_shapes=[pltpu.VMEM((tm, tn), F32)],
        compiler_params=_cp(("parallel", "parallel", "arbitrary")),
    )(dproj, dba, w_main_t, w_ba_t, *prev)


def _beta_g(ba, ab, row0, pad):
    lane = lax.broadcasted_iota(jnp.int32, ba.shape, 1)
    rows = row0 + lax.broadcasted_iota(jnp.int32, ba.shape, 0)
    z = ba + ab[1:2, :]
    sp = jnp.maximum(z, 0.0) + jnp.log(1.0 + jnp.exp(-jnp.abs(z)))
    val = jnp.where(lane < H, _sig(ba), -jnp.exp(ab[0:1, :]) * sp)
    return jnp.where((lane < 2 * H) & (rows >= pad), val, 0.0)


def _qkv_conv_fwd(proj, ba, conv_w, ab, pad):
    lp = proj.shape[0]
    te = _pick(lp, (640, 320))
    hb = te // HALO_Q

    def body(main_ref, halo_ref, cw_ref, ba_ref, ab_ref, out_ref, bg_ref, pre_scr, tap_scr):
        i, s = pl.program_id(0), pl.program_id(1)
        pre_scr[:HALO_Q, :] = jnp.where(i > 0, halo_ref[...], 0.0)
        pre_scr[HALO_Q:, :] = main_ref[...]
        scale = jnp.where(s == 0, DK ** -0.5, 1.0)
        off = HALO_Q - (KQ - 1)

        def head(h, carry):
            cs = pl.ds(pl.multiple_of(h * DK, DK), DK)
            for j in range(KQ - 1):
                tap_scr[j] = pre_scr[off + j:off + j + te, cs]
            co = cw_ref[KQ - 1:KQ, cs] * pre_scr[HALO_Q:, cs]
            for j in range(KQ - 1):
                co = co + cw_ref[j:j + 1, cs] * tap_scr[j]
            a = co * _sig(co)
            r = lax.rsqrt(_rowsum(a * a) + EPS)
            out_ref[:, cs] = jnp.where(s == 2, a, a * (r * scale))
            return carry

        lax.fori_loop(0, H, head, 0, unroll=True)

        @pl.when(s == 0)
        def _():
            bg_ref[...] = _beta_g(ba_ref[...], ab_ref[...], i * te, pad)

    return _call(
        body, name="qkv_conv_fwd", grid=(lp // te, 3),
        in_specs=[pl.BlockSpec((te, D), lambda i, s: (i, s)),
                  pl.BlockSpec((HALO_Q, D), lambda i, s: (jnp.maximum(i * hb - 1, 0), s)),
                  pl.BlockSpec((KQ, D), lambda i, s: (0, s)),
                  pl.BlockSpec((te, LANE), lambda i, s: (i, 0)),
                  pl.BlockSpec((2, LANE), lambda i, s: (0, 0))],
        out_specs=(pl.BlockSpec((te, D), lambda i, s: (i, s)), pl.BlockSpec((te, LANE), lambda i, s: (i, 0))),
        out_shape=(jax.ShapeDtypeStruct((lp, 3 * D), F32), jax.ShapeDtypeStruct((lp, LANE), F32)),
        scratch_shapes=[pltpu.VMEM((te + HALO_Q, D), F32), pltpu.VMEM((KQ - 1, te, DK), F32)],
        compiler_params=_cp(("parallel", "arbitrary")),
    )(proj, proj, conv_w, ba, ab)


def _tri_masks():
    row = lax.broadcasted_iota(jnp.int32, (C, C), 0)
    col = lax.broadcasted_iota(jnp.int32, (C, C), 1)
    return row, col


def _split(a):
    hi = a.astype(BF16)
    return hi, (a - hi.astype(F32)).astype(BF16)


def _dot3(a, b, dims=(((1,), (0,)), ((), ()))):
    (ah, al), (bh, bl) = a, b
    mm = lambda x, y: lax.dot_general(x, y, dims, preferred_element_type=F32)
    return mm(ah, bh) + (mm(ah, bl) + mm(al, bh))


CHUNKS_PER_STEP = 5
CHUNKS_PER_STEP_BWD = 1
TINV_BLOCK = 16


def _tinv(ns, row, col):
    eye = (row == col).astype(F32)
    sh = TINV_BLOCK.bit_length() - 1
    same16 = (row >> sh) == (col >> sh)
    same32 = (row >> (sh + 1)) == (col >> (sh + 1))
    ys = [jnp.where(same16, -n, 0.0) for n in ns]
    ts = [eye + y for y in ys]
    sp = [_split(y) for y in ys]
    for level in range(3):
        yks = [_dot3(s, s) for s in sp]
        sp = [_split(yk) for yk in yks]
        ts = [t + _dot3(s, _split(t)) for s, t in zip(sp, ts)]
    for mask in (same32 & ~same16, ~same32):
        tsp = [_split(t) for t in ts]
        inner = [_dot3(_split(jnp.where(mask, n, 0.0)), t) for n, t in zip(ns, tsp)]
        ts = [t - _dot3(tp, _split(a)) for t, tp, a in zip(ts, tsp, inner)]
    return ts


def _chunk_common(q, k, v, bcol, gcc, gcr, incl, strict):
    dm = jnp.where(incl, jnp.exp(gcc - gcr), 0.0)
    kk = _dot_nt(k, k)
    qk = _dot_nt(q, k)
    gccw = jnp.broadcast_to(gcc, (C, DK))
    egc = jnp.exp(gccw)
    glast = gccw[C - 1:C, :]
    eend = jnp.exp(glast - gccw)
    elast = jnp.exp(glast)
    rhs = jnp.concatenate([v * bcol, k * (bcol * egc)], axis=1)
    return dm, kk, qk, egc, eend, elast, rhs


def _delta_fwd(qkv, bg):
    lp = qkv.shape[0]
    nc = lp // C
    heads = range(H)
    sls = [slice(h * DK, (h + 1) * DK) for h in heads]

    def body(q_ref, k_ref, v_ref, bg_ref, o_ref, sall_ref, tall_ref, s_scr):
        @pl.when(pl.program_id(0) == 0)
        def _():
            s_scr[...] = jnp.zeros_like(s_scr)

        row, col = _tri_masks()
        incl, strict = row >= col, row > col

        def prepare(sub):
            rs = slice(sub * C, (sub + 1) * C)
            bgt = bg_ref[rs, :]
            gc_all = _dot_hi(incl.astype(F32), bgt)
            gc_t = _dot_hi(bgt.T, (row <= col).astype(F32))
            qs, ks, vs = ([r[rs, sl] for sl in sls] for r in (q_ref, k_ref, v_ref))
            bcols = [jnp.broadcast_to(bgt[:, h:h + 1], (C, DK)) for h in heads]
            cm = [_chunk_common(qs[h], ks[h], vs[h], bcols[h], gc_all[:, H + h:H + h + 1], gc_t[H + h:H + h + 1, :],
                                incl, strict) for h in heads]
            dms, kks, qks, egcs, eends, elasts, rhss = zip(*cm)
            ts = _tinv([jnp.where(strict, bcols[h][:, :C] * kks[h] * dms[h], 0.0) for h in heads], row, col)
            sols = [_dot3(_split(ts[h]), _split(rhss[h])) for h in heads]
            qgs = [(qs[h] * egcs[h]).astype(BF16) for h in heads]
            ps = [(qks[h] * dms[h]).astype(BF16) for h in heads]
            kends = [(ks[h] * eends[h]).astype(BF16) for h in heads]
            return ts, sols, qgs, ps, kends, elasts

        prepared = [prepare(sub) for sub in range(CHUNKS_PER_STEP)]
        ss = [s_scr[h] for h in heads]
        for sub in range(CHUNKS_PER_STEP):
            rs = slice(sub * C, (sub + 1) * C)
            ts, sols, qgs, ps, kends, elasts = prepared[sub]
            sb = [s.astype(BF16) for s in ss]
            wvb = [(sols[h][:, :DK] - _dot(sols[h][:, DK:], sb[h])).astype(BF16) for h in heads]
            for h in heads:
                o_ref[rs, sls[h]] = _dot(qgs[h], sb[h]) + _dot(ps[h], wvb[h])
                sall_ref[sub, h] = ss[h]
                tall_ref[sub, h] = ts[h]
            ss = [ss[h] * elasts[h] + _dot_tn(kends[h], wvb[h]) for h in heads]
        for h in heads:
            s_scr[h] = ss[h]

    rows = CHUNKS_PER_STEP * C
    blk = lambda j: pl.BlockSpec((rows, D), lambda n: (n, j))
    return _call(
        body, name="delta_fwd", grid=(nc // CHUNKS_PER_STEP,),
        in_specs=[blk(0), blk(1), blk(2), pl.BlockSpec((rows, LANE), lambda n: (n, 0))],
        out_specs=(pl.BlockSpec((rows, D), lambda n: (n, 0)),
                   pl.BlockSpec((CHUNKS_PER_STEP, H, DK, DK), lambda n: (n, 0, 0, 0)),
                   pl.BlockSpec((CHUNKS_PER_STEP, H, C, C), lambda n: (n, 0, 0, 0))),
        out_shape=(jax.ShapeDtypeStruct((lp, D), F32), jax.ShapeDtypeStruct((nc, H, DK, DK), F32),
                   jax.ShapeDtypeStruct((nc, H, C, C), F32)),
        scratch_shapes=[pltpu.VMEM((H, DK, DK), F32)],
        compiler_params=_cp(("arbitrary",)),
    )(qkv, qkv, qkv, bg)


def _delta_bwd(qkv, bg, sall, tall, do):
    lp = qkv.shape[0]
    nc = lp // C

    heads = range(H)
    sls = [slice(h * DK, (h + 1) * DK) for h in heads]

    def body(q_ref, k_ref, v_ref, bg_ref, sall_ref, tall_ref, do_ref, dqkv_ref, dbg_ref, ds_scr):
        @pl.when(pl.program_id(0) == 0)
        def _():
            ds_scr[...] = jnp.zeros_like(ds_scr)

        dsns = [ds_scr[h] for h in heads]
        for sub in reversed(range(CHUNKS_PER_STEP_BWD)):
            dsns = chunk(sub, dsns, q_ref, k_ref, v_ref, bg_ref, sall_ref, tall_ref, do_ref, dqkv_ref, dbg_ref)
        for h in heads:
            ds_scr[h] = dsns[h]

    def chunk(sub, dsns, q_ref, k_ref, v_ref, bg_ref, sall_ref, tall_ref, do_ref, dqkv_ref, dbg_ref):
        rs = slice(sub * C, (sub + 1) * C)
        bgt = bg_ref[rs, :]
        row, col = _tri_masks()
        incl, strict = row >= col, row > col
        upper = (row <= col).astype(F32)
        gc_all = _dot_hi(incl.astype(F32), bgt)
        gc_t = _dot_hi(bgt.T, upper)
        lane = lax.broadcasted_iota(jnp.int32, (C, LANE), 1)
        lastrow = lax.broadcasted_iota(jnp.int32, (C, 1), 0) == C - 1
        qs, ks, vs, dos = ([r[rs, sl] for sl in sls] for r in (q_ref, k_ref, v_ref, do_ref))
        bcols = [jnp.broadcast_to(bgt[:, h:h + 1], (C, DK)) for h in heads]
        cm = [_chunk_common(qs[h], ks[h], vs[h], bcols[h], gc_all[:, H + h:H + h + 1], gc_t[H + h:H + h + 1, :],
                            incl, strict) for h in heads]
        dms, kks, qks, egcs, eends, elasts, rhss = zip(*cm)
        ss = [sall_ref[sub, h] for h in heads]
        ts = [tall_ref[sub, h] for h in heads]
        sb = [s.astype(BF16) for s in ss]
        dsb = [d.astype(BF16) for d in dsns]
        dob = [d.astype(BF16) for d in dos]
        sols = [_dot3(_split(ts[h]), _split(rhss[h])) for h in heads]
        ws = [sol[:, DK:] for sol in sols]
        qgs = [qs[h] * egcs[h] for h in heads]
        kends = [ks[h] * eends[h] for h in heads]
        wvs = [sols[h][:, :DK] - _dot(ws[h], sb[h]) for h in heads]
        wvb = [wv.astype(BF16) for wv in wvs]
        dwvs = [_dot_tn(qks[h] * dms[h], dob[h]) + _dot(kends[h], dsb[h]) for h in heads]
        dps = [jnp.where(incl, _dot_nt(dob[h], wvb[h]), 0.0) for h in heads]
        dqgs = [_dot_nt(dob[h], sb[h]) for h in heads]
        dkends = [_dot_nt(wvb[h], dsb[h]) for h in heads]
        ds_before = [_dot_tn(qgs[h], dob[h]) + elasts[h] * dsns[h] - _dot_tn(ws[h], dwvs[h]) for h in heads]
        dglasts = [elasts[h] * jnp.sum(ss[h] * dsns[h], keepdims=True) for h in heads]
        dws = [-_dot_nt(dwvs[h], sb[h]) for h in heads]
        tts = [_split(ts[h].T) for h in heads]
        drhss = [_dot3(tts[h], _split(jnp.concatenate([dwvs[h], dws[h]], axis=1))) for h in heads]
        nt_dims = (((1,), (1,)), ((), ()))
        dns = [jnp.where(strict, -_dot3(_split(drhss[h]), _split(sols[h]), nt_dims), 0.0) for h in heads]
        dbeta_t = jnp.zeros((C, LANE), F32)
        dgc_t = jnp.zeros((C, LANE), F32)
        for h in heads:
            q, k, v, bcol, dm, kk, qk, egc, eend = qs[h], ks[h], vs[h], bcols[h], dms[h], kks[h], qks[h], egcs[h], eends[h]
            drv, drk = drhss[h][:, :DK], drhss[h][:, DK:]
            dn, dp, dqg, dkend = dns[h], dps[h], dqgs[h], dkends[h]
            rk = _rowsum(drk * k)
            dkk = dn * (bcol[:, :C] * dm)
            dqk = dp * dm
            e = (dn * (bcol[:, :C] * kk) + dp * qk) * dm
            tk = _rowsum(dkend * kends[h])
            dgc = rk * bcol * egc + _rowsum(e) - _rowsum(e.T) + _rowsum(dqg * qgs[h]) - tk
            dgc = dgc + jnp.where(lastrow, dglasts[h] + jnp.sum(tk, keepdims=True), 0.0)
            dbeta = _rowsum(drv * v) + rk * egc + _rowsum(dn * kk * dm)
            dqkv_ref[rs, sls[h]] = _dot(dqk, k) + dqg * egc
            dqkv_ref[rs, D + h * DK:D + (h + 1) * DK] = (drk * (bcol * egc) + _dot(dkk, k) + _dot_tn(dkk, k)
                                                        + _dot_tn(dqk, q) + dkend * eend)
            dqkv_ref[rs, 2 * D + h * DK:2 * D + (h + 1) * DK] = bcol * drv
            dbeta_t = jnp.where(lane == h, dbeta, dbeta_t)
            dgc_t = jnp.where(lane == H + h, dgc, dgc_t)
        dbg_ref[rs, :] = dbeta_t + _dot_hi(upper, dgc_t)
        return ds_before

    steps = nc // CHUNKS_PER_STEP_BWD
    rows = CHUNKS_PER_STEP_BWD * C
    rev = lambda n: steps - 1 - n
    blk = lambda j: pl.BlockSpec((rows, D), lambda n: (rev(n), j))
    return _call(
        body, name="delta_bwd", grid=(steps,),
        in_specs=[blk(0), blk(1), blk(2), pl.BlockSpec((rows, LANE), lambda n: (rev(n), 0)),
                  pl.BlockSpec((CHUNKS_PER_STEP_BWD, H, DK, DK), lambda n: (rev(n), 0, 0, 0)),
                  pl.BlockSpec((CHUNKS_PER_STEP_BWD, H, C, C), lambda n: (rev(n), 0, 0, 0)),
                  pl.BlockSpec((rows, D), lambda n: (rev(n), 0))],
        out_specs=(pl.BlockSpec((rows, 3 * D), lambda n: (rev(n), 0)),
                   pl.BlockSpec((rows, LANE), lambda n: (rev(n), 0))),
        out_shape=(jax.ShapeDtypeStruct((lp, 3 * D), F32), jax.ShapeDtypeStruct((lp, LANE), F32)),
        scratch_shapes=[pltpu.VMEM((H, DK, DK), F32)],
        compiler_params=_cp(("arbitrary",)),
    )(qkv, qkv, qkv, bg, sall, tall, do)


def _o_post_fwd(o, proj, dn_w, w_dn):
    lp = o.shape[0]
    te = _pick(lp, (640, 320))

    def body(o_ref, za_ref, w_ref, wdn_ref, out_ref, ya_ref):
        za = za_ref[...]
        gate = za * _sig(za)
        for h in range(H):
            sl = slice(h * DK, (h + 1) * DK)
            oh = o_ref[:, sl]
            r = lax.rsqrt(jnp.mean(oh * oh, axis=-1, keepdims=True) + EPS)
            out_ref[:, sl] = (oh * r * w_ref[...] * gate[:, sl]).astype(BF16)
        ya_ref[...] = _dot(out_ref[...], wdn_ref[...])

    row = pl.BlockSpec((te, D), lambda i: (i, 0))
    return _call(
        body, name="o_post_fwd", grid=(lp // te,),
        in_specs=[row, pl.BlockSpec((te, D), lambda i: (i, CB_ZA)), pl.BlockSpec((1, DK), lambda i: (0, 0)),
                  pl.BlockSpec((D, D), lambda i: (0, 0))],
        out_specs=(row, row),
        out_shape=(jax.ShapeDtypeStruct((lp, D), BF16), jax.ShapeDtypeStruct((lp, D), F32)),
        compiler_params=_cp(("parallel",)),
    )(o, proj, dn_w, w_dn)


def _o_post_bwd(dy_a, w_dn, o, proj, dn_w, dproj):
    lp = o.shape[0]
    te = _pick(lp, (640, 320))

    def body(dya_ref, wdn_ref, o_ref, za_ref, w_ref, _, do_ref, dza_ref, dw_ref, don_ref):
        @pl.when(pl.program_id(0) == 0)
        def _():
            dw_ref[...] = jnp.zeros_like(dw_ref)

        don_ref[...] = _dot_nt(dya_ref[...], wdn_ref[...])
        za = za_ref[...]
        sz = _sig(za)
        gate, dgate = za * sz, _dsilu(za, sz)
        w = w_ref[...]
        dw = jnp.zeros((1, DK), F32)
        for h in range(H):
            sl = slice(h * DK, (h + 1) * DK)
            oh, g = o_ref[:, sl], don_ref[:, sl]
            r = lax.rsqrt(jnp.mean(oh * oh, axis=-1, keepdims=True) + EPS)
            ohat = oh * r
            dza_ref[:, sl] = (g * ohat * w * dgate[:, sl]).astype(BF16)
            don = g * gate[:, sl]
            dw = dw + _colsum(don * ohat)
            dohat = don * w
            do_ref[:, sl] = r * (dohat - ohat * jnp.mean(dohat * ohat, axis=-1, keepdims=True))
        dw_ref[...] += dw

    return _call(
        body, name="o_post_bwd", grid=(lp // te,),
        in_specs=[pl.BlockSpec((te, D), lambda i: (i, 0)), pl.BlockSpec((D, D), lambda i: (0, 0)),
                  pl.BlockSpec((te, D), lambda i: (i, 0)),
                  pl.BlockSpec((te, D), lambda i: (i, CB_ZA)), pl.BlockSpec((1, DK), lambda i: (0, 0)),
                  pl.BlockSpec(memory_space=pl.ANY)],
        out_specs=(pl.BlockSpec((te, D), lambda i: (i, 0)), pl.BlockSpec((te, D), lambda i: (i, CB_ZA)),
                   pl.BlockSpec((1, DK), lambda i: (0, 0))),
        out_shape=(jax.ShapeDtypeStruct((lp, D), F32), jax.ShapeDtypeStruct(dproj.shape, dproj.dtype),
                   jax.ShapeDtypeStruct((1, DK), F32)),
        input_output_aliases={5: 1},
        scratch_shapes=[pltpu.VMEM((te, D), F32)],
        compiler_params=_cp(("arbitrary",), vmem_mib=52),
    )(dy_a, w_dn, o, proj, dn_w, dproj)


def _qkv_conv_bwd(proj, dqkv, conv_w, dproj):
    lp = proj.shape[0]
    te = _pick(lp, (640, 320))
    hb = te // HALO_Q
    nt = lp // te
    last_hb = lp // HALO_Q - 1

    def body(main_ref, prev_ref, next_ref, dmain_ref, dnext_ref, cw_ref, _, dpre_ref, dcw_ref, pre_scr, dn_scr,
             tap_scr, dco_scr, dsh_scr):
        s, i = pl.program_id(0), pl.program_id(1)

        @pl.when(i == 0)
        def _():
            dcw_ref[...] = jnp.zeros_like(dcw_ref)

        ne = te + HALO_Q
        pre_scr[:HALO_Q, :] = jnp.where(i > 0, prev_ref[...], 0.0)
        pre_scr[HALO_Q:ne, :] = main_ref[...]
        pre_scr[ne:, :] = jnp.where(i < nt - 1, next_ref[...], 0.0)
        dn_scr[:te, :] = dmain_ref[...]
        dn_scr[te:, :] = jnp.where(i < nt - 1, dnext_ref[...], 0.0)
        scale = jnp.where(s == 0, DK ** -0.5, 1.0)
        off = HALO_Q - (KQ - 1)

        def head(h, carry):
            cs = pl.ds(pl.multiple_of(h * DK, DK), DK)
            for j in range(KQ - 1):
                tap_scr[j] = pre_scr[off + j:off + j + ne, cs]
            taps = [tap_scr[j] for j in range(KQ - 1)] + [pre_scr[HALO_Q:, cs]]
            co = cw_ref[0:1, cs] * taps[0]
            for j in range(1, KQ):
                co = co + cw_ref[j:j + 1, cs] * taps[j]
            sg = _sig(co)
            a = co * sg
            g = dn_scr[:, cs]
            r = lax.rsqrt(_rowsum(a * a) + EPS)
            yhat = a * r
            da = jnp.where(s == 2, g, (scale * r) * (g - yhat * _rowsum(g * yhat)))
            dco = da * _dsilu(co, sg)
            dco_scr[...] = dco
            for j in range(KQ - 1):
                dsh_scr[j] = dco_scr[KQ - 1 - j:KQ - 1 - j + te, :]
            dpre = cw_ref[KQ - 1:KQ, cs] * dco[:te, :]
            for j in range(KQ - 1):
                dpre = dpre + cw_ref[j:j + 1, cs] * dsh_scr[j]
            dpre_ref[:, cs] = dpre.astype(BF16)
            dcw_ref[:, cs] += jnp.concatenate([_colsum(dco[:te] * taps[j][:te]) for j in range(KQ)], axis=0)
            return carry

        lax.fori_loop(0, H, head, 0, unroll=True)

    return _call(
        body, name="qkv_conv_bwd", grid=(3, nt),
        in_specs=[pl.BlockSpec((te, D), lambda s, i: (i, s)),
                  pl.BlockSpec((HALO_Q, D), lambda s, i: (jnp.maximum(i * hb - 1, 0), s)),
                  pl.BlockSpec((HALO_Q, D), lambda s, i: (jnp.minimum((i + 1) * hb, last_hb), s)),
                  pl.BlockSpec((te, D), lambda s, i: (i, s)),
                  pl.BlockSpec((HALO_Q, D), lambda s, i: (jnp.minimum((i + 1) * hb, last_hb), s)),
                  pl.BlockSpec((KQ, D), lambda s, i: (0, s)),
                  pl.BlockSpec(memory_space=pl.ANY)],
        out_specs=(pl.BlockSpec((te, D), lambda s, i: (i, s)), pl.BlockSpec((KQ, D), lambda s, i: (0, s))),
        out_shape=(jax.ShapeDtypeStruct(dproj.shape, dproj.dtype), jax.ShapeDtypeStruct((KQ, 3 * D), F32)),
        input_output_aliases={6: 0},
        scratch_shapes=[pltpu.VMEM((te + 2 * HALO_Q, D), F32), pltpu.VMEM((te + HALO_Q, D), F32),
                        pltpu.VMEM((KQ - 1, te + HALO_Q, DK), F32), pltpu.VMEM((te + HALO_Q, DK), F32),
                        pltpu.VMEM((KQ - 1, te, DK), F32)],
        compiler_params=_cp(("arbitrary", "arbitrary")),
    )(proj, proj, proj, dqkv, dqkv, conv_w, dproj)


def _ba_bwd(dbg, ba, ab, pad):
    lp = ba.shape[0]
    te = _pick(lp, (640, 320))

    def body(dbg_ref, ba_ref, ab_ref, dba_ref, dab_ref):
        i = pl.program_id(0)

        @pl.when(i == 0)
        def _():
            dab_ref[...] = jnp.zeros_like(dab_ref)

        ba, ab = ba_ref[...], ab_ref[...]
        lane = lax.broadcasted_iota(jnp.int32, ba.shape, 1)
        rows = i * te + lax.broadcasted_iota(jnp.int32, ba.shape, 0)
        g = jnp.where((lane < 2 * H) & (rows >= pad), dbg_ref[...], 0.0)
        sb = _sig(ba)
        z = ba + ab[1:2, :]
        sp = jnp.maximum(z, 0.0) + jnp.log(1.0 + jnp.exp(-jnp.abs(z)))
        nea = -jnp.exp(ab[0:1, :])
        dz = g * nea * _sig(z)
        dba_ref[...] = jnp.where(lane < H, g * sb * (1.0 - sb), dz).astype(BF16)
        is_g = (lane >= H) & (lane < 2 * H)
        dab_ref[...] += jnp.concatenate([_colsum(jnp.where(is_g, g * nea * sp, 0.0)),
                                         _colsum(jnp.where(is_g, dz, 0.0))], axis=0)

    return _call(
        body, name="ba_bwd", grid=(lp // te,),
        in_specs=[pl.BlockSpec((te, LANE), lambda i: (i, 0)), pl.BlockSpec((te, LANE), lambda i: (i, 0)),
                  pl.BlockSpec((2, LANE), lambda i: (0, 0))],
        out_specs=(pl.BlockSpec((te, LANE), lambda i: (i, 0)), pl.BlockSpec((2, LANE), lambda i: (0, 0))),
        out_shape=(jax.ShapeDtypeStruct((lp, LANE), BF16), jax.ShapeDtypeStruct((2, LANE), F32)),
        compiler_params=_cp(("arbitrary",)),
    )(dbg, ba, ab)


SUBLANES = 8
CONV_RB = 64


def _fill_shifted(sh_scr, src_scr, cs):
    n = sh_scr.shape[1]
    for s in range(1, SUBLANES):
        sh_scr[s] = src_scr[s:s + n, cs]


def _shifted(sh_scr, src_scr, cs, r, r0, n):
    s, a8 = r % SUBLANES, r - r % SUBLANES
    if s == 0:
        return src_scr[r0 + a8:r0 + a8 + n, cs]
    return sh_scr[s, r0 + a8:r0 + a8 + n, :]


def _conv_b_fwd(proj, dw_w, dw_b, ln_w, ln_b, w_cf):
    lp = proj.shape[0]
    te = _pick(lp, (640, 320))
    hb = te // HALO_D

    def body(a_ref, b_ref, ha_ref, hb_ref, zb_ref, w_ref, wb_ref, lw_ref, lb_ref, wcf_ref, c1_ref, c3_ref, yb_ref,
             c0_scr, sh_scr):
        i = pl.program_id(0)
        c0_scr[:HALO_D, :] = jnp.where(i > 0, ha_ref[...] * _sig(hb_ref[...]), 0.0)
        c0_scr[HALO_D:, :] = a_ref[...] * _sig(b_ref[...])
        off = HALO_D - (KD - 1)
        def lane_block(cb, carry):
            cs = pl.ds(pl.multiple_of(cb * LANE, LANE), LANE)
            _fill_shifted(sh_scr, c0_scr, cs)
            for r0 in range(0, te, CONV_RB):
                acc = None
                for j in range(KD):
                    term = w_ref[j:j + 1, cs] * _shifted(sh_scr, c0_scr, cs, off + j, r0, CONV_RB)
                    acc = term if acc is None else acc + term
                c1_ref[r0:r0 + CONV_RB, cs] = acc + wb_ref[:, cs]
            return carry

        lax.fori_loop(0, D // LANE, lane_block, 0)
        c1 = c1_ref[...]
        mu = jnp.mean(c1, axis=-1, keepdims=True)
        xc = c1 - mu
        c2 = xc * lax.rsqrt(jnp.mean(xc * xc, axis=-1, keepdims=True) + EPS) * lw_ref[...] + lb_ref[...]
        zb = zb_ref[...]
        c3 = (c2 * _sig(c2) * zb * _sig(zb)).astype(BF16)
        c3_ref[...] = c3
        yb_ref[...] = _dot(c3, wcf_ref[...])

    vec = pl.BlockSpec((1, D), lambda i: (0, 0))
    row = pl.BlockSpec((te, D), lambda i: (i, 0))
    return _call(
        body, name="conv_b_fwd", grid=(lp // te,),
        in_specs=[pl.BlockSpec((te, D), lambda i: (i, CB_GA_)), pl.BlockSpec((te, D), lambda i: (i, CB_GB_)),
                  pl.BlockSpec((HALO_D, D), lambda i: (jnp.maximum(i * hb - 1, 0), CB_GA_)),
                  pl.BlockSpec((HALO_D, D), lambda i: (jnp.maximum(i * hb - 1, 0), CB_GB_)),
                  pl.BlockSpec((te, D), lambda i: (i, CB_ZB)),
                  pl.BlockSpec((KD, D), lambda i: (0, 0)), vec, vec, vec, pl.BlockSpec((D, D), lambda i: (0, 0))],
        out_specs=(row, row, row),
        out_shape=(jax.ShapeDtypeStruct((lp, D), F32), jax.ShapeDtypeStruct((lp, D), BF16),
                   jax.ShapeDtypeStruct((lp, D), F32)),
        scratch_shapes=[pltpu.VMEM((te + HALO_D, D), F32), pltpu.VMEM((SUBLANES, te + HALO_D - SUBLANES, LANE), F32)],
        compiler_params=_cp(("parallel",), vmem_mib=52),
    )(proj, proj, proj, proj, proj, dw_w, dw_b, ln_w, ln_b, w_cf)


def _conv_b_bwd1(dy_b, w_cf, c1, proj, ln_w, ln_b, dproj):
    lp = c1.shape[0]
    te = _pick(lp, (320,))

    def body(dyb_ref, wcf_ref, c1_ref, zb_ref, lw_ref, lb_ref, _, dc1_ref, dzb_ref, sums_ref):
        @pl.when(pl.program_id(0) == 0)
        def _():
            sums_ref[...] = jnp.zeros_like(sums_ref)

        c1, g = c1_ref[...], _dot_nt(dyb_ref[...], wcf_ref[...])
        mu = jnp.mean(c1, axis=-1, keepdims=True)
        xc = c1 - mu
        rstd = lax.rsqrt(jnp.mean(xc * xc, axis=-1, keepdims=True) + EPS)
        xh = xc * rstd
        lw = lw_ref[...]
        c2 = xh * lw + lb_ref[...]
        s2 = _sig(c2)
        zb = zb_ref[...]
        sz = _sig(zb)
        dc2 = g * (zb * sz) * _dsilu(c2, s2)
        dzb_ref[...] = (g * (c2 * s2) * _dsilu(zb, sz)).astype(BF16)
        dxh = dc2 * lw
        dc1 = rstd * (dxh - jnp.mean(dxh, axis=-1, keepdims=True) - xh * jnp.mean(dxh * xh, axis=-1, keepdims=True))
        dc1_ref[...] = dc1
        sums_ref[...] += jnp.concatenate([_colsum(dc2 * xh), _colsum(dc2), _colsum(dc1)], axis=0)

    vec = pl.BlockSpec((1, D), lambda i: (0, 0))
    return _call(
        body, name="conv_b_bwd1", grid=(lp // te,),
        in_specs=[pl.BlockSpec((te, D), lambda i: (i, 0)), pl.BlockSpec((D, D), lambda i: (0, 0)),
                  pl.BlockSpec((te, D), lambda i: (i, 0)),
                  pl.BlockSpec((te, D), lambda i: (i, CB_ZB)), vec, vec, pl.BlockSpec(memory_space=pl.ANY)],
        out_specs=(pl.BlockSpec((te, D), lambda i: (i, 0)), pl.BlockSpec((te, D), lambda i: (i, CB_ZB)),
                   pl.BlockSpec((3, D), lambda i: (0, 0))),
        out_shape=(jax.ShapeDtypeStruct((lp, D), F32), jax.ShapeDtypeStruct(dproj.shape, dproj.dtype),
                   jax.ShapeDtypeStruct((3, D), F32)),
        input_output_aliases={6: 1},
        compiler_params=_cp(("arbitrary",)),
    )(dy_b, w_cf, c1, proj, ln_w, ln_b, dproj)


def _conv_b_bwd2(dc1, proj, dw_w, dproj):
    lp = dc1.shape[0]
    te = _pick(lp, (640, 320))
    hb = te // HALO_D
    nt = lp // te
    last_hb = lp // HALO_D - 1

    def body(g_ref, gn_ref, a_ref, b_ref, ha_ref, hb_ref, w_ref, _, dab_ref, dw_ref, c0_scr, g_scr, dc0_scr,
             csh_scr, gsh_scr):
        i = pl.program_id(0)

        @pl.when(i == 0)
        def _():
            dw_ref[...] = jnp.zeros_like(dw_ref)

        a, b = a_ref[...], b_ref[...]
        sb = _sig(b)
        c0_scr[:HALO_D, :] = jnp.where(i > 0, ha_ref[...] * _sig(hb_ref[...]), 0.0)
        c0_scr[HALO_D:, :] = a * sb
        g_scr[:te, :] = g_ref[...]
        g_scr[te:, :] = jnp.where(i < nt - 1, gn_ref[...], 0.0)
        off = HALO_D - (KD - 1)
        def lane_block(cb, carry):
            cs = pl.ds(pl.multiple_of(cb * LANE, LANE), LANE)
            _fill_shifted(csh_scr, c0_scr, cs)
            _fill_shifted(gsh_scr, g_scr, cs)
            for r0 in range(0, te, CONV_RB):
                acc = None
                for j in range(KD):
                    term = w_ref[j:j + 1, cs] * _shifted(gsh_scr, g_scr, cs, KD - 1 - j, r0, CONV_RB)
                    acc = term if acc is None else acc + term
                dc0_scr[r0:r0 + CONV_RB, cs] = acc
            parts = [None] * KD
            for r0 in range(0, te, CONV_RB):
                g = g_scr[r0:r0 + CONV_RB, cs].reshape(CONV_RB // SUBLANES, SUBLANES, LANE)
                for j in range(KD):
                    x = _shifted(csh_scr, c0_scr, cs, off + j, r0, CONV_RB)
                    p = jnp.sum(g * x.reshape(CONV_RB // SUBLANES, SUBLANES, LANE), axis=0)
                    parts[j] = p if parts[j] is None else parts[j] + p
            dw_ref[:, cs] += jnp.concatenate([_colsum(p) for p in parts], axis=0)
            return carry

        lax.fori_loop(0, D // LANE, lane_block, 0)
        dc0 = dc0_scr[...]
        dab_ref[:, :D] = (dc0 * sb).astype(BF16)
        dab_ref[:, D:] = (dc0 * a * sb * (1.0 - sb)).astype(BF16)

    return _call(
        body, name="conv_b_bwd2", grid=(nt,),
        in_specs=[pl.BlockSpec((te, D), lambda i: (i, 0)),
                  pl.BlockSpec((HALO_D, D), lambda i: (jnp.minimum((i + 1) * hb, last_hb), 0)),
                  pl.BlockSpec((te, D), lambda i: (i, CB_GA_)), pl.BlockSpec((te, D), lambda i: (i, CB_GB_)),
                  pl.BlockSpec((HALO_D, D), lambda i: (jnp.maximum(i * hb - 1, 0), CB_GA_)),
                  pl.BlockSpec((HALO_D, D), lambda i: (jnp.maximum(i * hb - 1, 0), CB_GB_)),
                  pl.BlockSpec((KD, D), lambda i: (0, 0)), pl.BlockSpec(memory_space=pl.ANY)],
        out_specs=(pl.BlockSpec((te, 2 * D), lambda i: (i, CB_GA_ // 2)), pl.BlockSpec((KD, D), lambda i: (0, 0))),
        out_shape=(jax.ShapeDtypeStruct(dproj.shape, dproj.dtype), jax.ShapeDtypeStruct((KD, D), F32)),
        input_output_aliases={7: 0},
        scratch_shapes=[pltpu.VMEM((te + HALO_D, D), F32), pltpu.VMEM((te + HALO_D, D), F32), pltpu.VMEM((te, D), F32),
                        pltpu.VMEM((SUBLANES, te + HALO_D - SUBLANES, LANE), F32),
                        pltpu.VMEM((SUBLANES, te + HALO_D - SUBLANES, LANE), F32)],
        compiler_params=_cp(("arbitrary",)),
    )(dc1, dc1, proj, proj, proj, proj, dw_w, dproj)


def _merge_fwd(y_a, y_b, proj, b_cf, w_o):
    lp = y_a.shape[0]
    te = _pick(lp, (640, 320))

    def body(ya_ref, yb_ref, ga_ref, gb_ref, bias_ref, wo_ref, out_ref, z_ref):
        merged = (_sig(ga_ref[...]) * ya_ref[...] + _sig(gb_ref[...]) * (yb_ref[...] + bias_ref[...])).astype(BF16)
        out_ref[...] = merged
        z_ref[...] = _dot(merged, wo_ref[...])

    row = lambda j: pl.BlockSpec((te, D), lambda i: (i, j))
    return _call(
        body, name="merge_fwd", grid=(lp // te,),
        in_specs=[row(0), row(0), row(CB_MA), row(CB_MB), pl.BlockSpec((1, D), lambda i: (0, 0)),
                  pl.BlockSpec((D, D), lambda i: (0, 0))],
        out_specs=(row(0), row(0)),
        out_shape=(jax.ShapeDtypeStruct((lp, D), BF16), jax.ShapeDtypeStruct((lp, D), F32)),
        compiler_params=_cp(("parallel",), vmem_mib=52),
    )(y_a, y_b, proj, proj, b_cf, w_o)


def _merge_bwd(dx_out_b, w_o, y_a, y_b, proj, b_cf):
    lp = y_a.shape[0]
    te = _pick(lp, (320,))

    def body(dx_ref, wo_ref, ya_ref, yb_ref, ga_ref, gb_ref, bias_ref, dya_ref, dyb_ref, dg_ref, db_ref):
        @pl.when(pl.program_id(0) == 0)
        def _():
            db_ref[...] = jnp.zeros_like(db_ref)

        dm = _dot_nt(dx_ref[...], wo_ref[...])
        sa, sb = _sig(ga_ref[...]), _sig(gb_ref[...])
        dyb = sb * dm
        dya_ref[...] = (sa * dm).astype(BF16)
        dyb_ref[...] = dyb.astype(BF16)
        dg_ref[:, :D] = (dm * ya_ref[...] * sa * (1.0 - sa)).astype(BF16)
        dg_ref[:, D:] = (dm * (yb_ref[...] + bias_ref[...]) * sb * (1.0 - sb)).astype(BF16)
        db_ref[...] += _colsum(dyb)

    row = lambda j: pl.BlockSpec((te, D), lambda i: (i, j))
    act = jax.ShapeDtypeStruct((lp, D), BF16)
    return _call(
        body, name="merge_bwd", grid=(lp // te,),
        in_specs=[row(0), pl.BlockSpec((D, D), lambda i: (0, 0)), row(0), row(0), row(CB_MA), row(CB_MB),
                  pl.BlockSpec((1, D), lambda i: (0, 0))],
        out_specs=(row(0), row(0), pl.BlockSpec((te, 2 * D), lambda i: (i, CB_MA // 2)),
                   pl.BlockSpec((1, D), lambda i: (0, 0))),
        out_shape=(act, act, jax.ShapeDtypeStruct((lp, NCB * D), BF16), jax.ShapeDtypeStruct((1, D), F32)),
        compiler_params=_cp(("arbitrary",)),
    )(dx_out_b, w_o, y_a, y_b, proj, proj, b_cf)


def _final_fwd_bwd(x_ext, z, target, final_w):
    lp = x_ext.shape[0]
    te = _pick(lp, (640,))
    nsub = te // LANE

    def body(x_ref, z_ref, *rest):
        t_refs, (w_ref, dx_ref, dxb_ref, loss_ref, dw_ref) = rest[:nsub], rest[nsub:]
        i = pl.program_id(0)

        @pl.when(i == 0)
        def _():
            loss_ref[...] = jnp.zeros_like(loss_ref)
            dw_ref[...] = jnp.zeros_like(dw_ref)

        w = w_ref[...]
        for k in range(nsub):
            rs = slice(k * LANE, (k + 1) * LANE)
            xo = x_ref[rs, :] + z_ref[rs, :]
            r = lax.rsqrt(jnp.mean(xo * xo, axis=-1, keepdims=True) + EPS)
            xhat = xo * r
            err = xhat * w - t_refs[k][...]
            if k == 0:
                err = jnp.where(i > 0, err, 0.0)
            loss_ref[...] += 0.5 * jnp.sum(jnp.mean(err * err, axis=-1, keepdims=True), keepdims=True)
            dy = err * (1.0 / D)
            dw_ref[...] += _colsum(dy * xhat)
            dxn = dy * w
            dx = r * (dxn - xhat * jnp.mean(dxn * xhat, axis=-1, keepdims=True))
            dx_ref[rs, :] = dx
            dxb_ref[rs, :] = dx.astype(BF16)

    piece = lambda k: pl.BlockSpec((LANE, D), lambda i: (jnp.maximum(i * nsub + k - 1, 0), 0))
    row = pl.BlockSpec((te, D), lambda i: (i, 0))
    return _call(
        body, name="final_fwd_bwd", grid=(lp // te,),
        in_specs=[row, row] + [piece(k) for k in range(nsub)] + [pl.BlockSpec((1, D), lambda i: (0, 0))],
        out_specs=(row, row, pl.BlockSpec((1, 1), lambda i: (0, 0)), pl.BlockSpec((1, D), lambda i: (0, 0))),
        out_shape=(jax.ShapeDtypeStruct((lp, D), F32), jax.ShapeDtypeStruct((lp, D), BF16),
                   jax.ShapeDtypeStruct((1, 1), F32), jax.ShapeDtypeStruct((1, D), F32)),
        compiler_params=_cp(("arbitrary",)),
    )(x_ext, z, *([target] * nsub), final_w)


def _prenorm_bwd(dh, x_ext, dx_out, norm_w, seq):
    lp = x_ext.shape[0]
    te = _pick(lp, (640,))
    nt = lp // te
    head = lp - seq

    def body(dh_ref, x_ref, dxo_ref, w_ref, gx_ref, head_ref, dw_ref, stage, sems):
        i = pl.program_id(0)
        slot = i % 2

        def first_copy():
            return pltpu.make_async_copy(stage.at[0, pl.ds(head, te - head)], gx_ref.at[pl.ds(0, te - head)], sems.at[0])

        def tile_copy(step, s):
            return pltpu.make_async_copy(stage.at[s], gx_ref.at[pl.ds(pl.multiple_of(step * te - head, LANE), te)],
                                         sems.at[s])

        @pl.when(i == 0)
        def _():
            dw_ref[...] = jnp.zeros_like(dw_ref)

        @pl.when(i == 2)
        def _():
            first_copy().wait()

        @pl.when(i > 2)
        def _():
            tile_copy(i - 2, slot).wait()

        x, dh = x_ref[...], dh_ref[...]
        r = lax.rsqrt(jnp.mean(x * x, axis=-1, keepdims=True) + EPS)
        xhat = x * r
        dxn = dh * w_ref[...]
        stage[slot] = dxo_ref[...] + r * (dxn - xhat * jnp.mean(dxn * xhat, axis=-1, keepdims=True))
        dw_ref[...] += _colsum(dh * xhat)

        @pl.when(i == 0)
        def _():
            head_ref[...] = stage[0, :head, :]
            first_copy().start()

        @pl.when(i > 0)
        def _():
            tile_copy(i, slot).start()

        @pl.when(i == nt - 1)
        def _():
            if nt >= 2:
                (first_copy() if nt == 2 else tile_copy(nt - 2, (nt - 2) % 2)).wait()
            (first_copy() if nt == 1 else tile_copy(nt - 1, (nt - 1) % 2)).wait()

    row = pl.BlockSpec((te, D), lambda i: (i, 0))
    return _call(
        body, name="prenorm_bwd", grid=(nt,),
        in_specs=[row, row, row, pl.BlockSpec((1, D), lambda i: (0, 0))],
        out_specs=(pl.BlockSpec(memory_space=pl.ANY), pl.BlockSpec((head, D), lambda i: (0, 0)),
                   pl.BlockSpec((1, D), lambda i: (0, 0))),
        out_shape=(jax.ShapeDtypeStruct((seq, D), F32), jax.ShapeDtypeStruct((head, D), F32),
                   jax.ShapeDtypeStruct((1, D), F32)),
        scratch_shapes=[pltpu.VMEM((2, te, D), F32), pltpu.SemaphoreType.DMA((2,))],
        compiler_params=_cp(("arbitrary",)),
    )(dh, x_ext, dx_out, norm_w)


def _adam_reduce(parts, w, m, v, name):
    r, n = w.shape
    tr = _pick(r, (128,)) if r % 128 == 0 else r

    def body(p_ref, w_ref, m_ref, v_ref, g_ref, d_ref, m2_ref, v2_ref):
        g = p_ref[0]
        for s in range(1, NDEV):
            g = g + p_ref[s]
        _adam_write(g, w_ref, m_ref, v_ref, g_ref, d_ref, m2_ref, v2_ref)

    blk = pl.BlockSpec((tr, n), lambda i: (i, 0))
    out = jax.ShapeDtypeStruct((r, n), F32)
    return _call(
        body, name=name, grid=(r // tr,),
        in_specs=[pl.BlockSpec((NDEV, tr, n), lambda i: (0, i, 0)), blk, blk, blk],
        out_specs=(blk, blk, blk, blk), out_shape=(out, out, out, out),
        compiler_params=_cp(("parallel",)),
    )(parts, w, m, v)


def _adam_write(g, w_ref, m_ref, v_ref, g_ref, d_ref, m2_ref, v2_ref):
    c1 = 1.0 - ADAM_B1 ** ADAM_STEP
    c2 = 1.0 - ADAM_B2 ** ADAM_STEP
    m2 = ADAM_B1 * m_ref[...] + (1.0 - ADAM_B1) * g
    v2 = ADAM_B2 * v_ref[...] + (1.0 - ADAM_B2) * (g * g)
    g_ref[...] = g
    m2_ref[...] = m2
    v2_ref[...] = v2
    d_ref[...] = -ADAM_LR * ((m2 / c1) / (jnp.sqrt(v2 / c2) + ADAM_EPS) + ADAM_WD * w_ref[...])


def _adam_chips(own, recv, w, m, v, name):
    r, n = w.shape
    tr, tc = _shard_tile(r, n)

    def body(own_ref, p_ref, w_ref, m_ref, v_ref, g_ref, d_ref, m2_ref, v2_ref):
        my_chip = 2 * lax.axis_index("x") + lax.axis_index("y")
        g = None
        for j in range(NCHIP):
            part = jnp.where(my_chip == j, own_ref[...], p_ref[j].astype(F32))
            g = part if g is None else g + part
        _adam_write(g, w_ref, m_ref, v_ref, g_ref, d_ref, m2_ref, v2_ref)

    blk = pl.BlockSpec((tr, tc), lambda i, k: (i, k))
    out = jax.ShapeDtypeStruct((r, n), F32)
    return _call(
        body, name=name, grid=(r // tr, n // tc),
        in_specs=[blk, pl.BlockSpec((NCHIP, tr, tc), lambda i, k: (0, i, k)), blk, blk, blk],
        out_specs=(blk, blk, blk, blk), out_shape=(out, out, out, out),
        compiler_params=_cp(("parallel", "parallel")),
    )(own, recv, w, m, v)


SMALL = ("norm_w", "a_log", "dt_bias", "dn_norm_w", "dw_b", "ln_w", "ln_b", "b_cf_out", "final_norm_w")


def kernel(x, meta, norm_w, w_in, conv_qkv_w, a_log, dt_bias, dn_norm_w, w_dn_out, dw_w, dw_b, ln_w, ln_b, w_cf_out, b_cf_out, w_o, final_norm_w, loss_target, m_meta, m_norm_w, m_w_in, m_conv_qkv_w, m_a_log, m_dt_bias, m_dn_norm_w, m_w_dn_out, m_dw_w, m_dw_b, m_ln_w, m_ln_b, m_w_cf_out, m_b_cf_out, m_w_o, m_final_norm_w, v_meta, v_norm_w, v_w_in, v_conv_qkv_w, v_a_log, v_dt_bias, v_dn_norm_w, v_w_dn_out, v_dw_w, v_dw_b, v_ln_w, v_ln_b, v_w_cf_out, v_b_cf_out, v_w_o, v_final_norm_w):
    seq = x.shape[1]
    pad = (-(seq + NMETA)) % LANE
    in_w = w_in.shape[2] * NDEV
    n_qkvz = 4 * D
    n_ba = 2 * H

    me = 4 * lax.axis_index("x") + 2 * lax.axis_index("y") + lax.axis_index("c")
    late = [w_dn_out[0].astype(BF16), w_cf_out[0].astype(BF16), w_o[0].astype(BF16), conv_qkv_w[0], dw_w[0]]
    late_lands = [lax.dynamic_update_index_in_dim(jnp.zeros((NDEV,) + b.shape, b.dtype), b, me, 0) for b in late]
    w_in_g, meta_g = _gather_two_level([w_in[0].astype(BF16).T, meta], "gather_weights")
    late_send, late_recv, late_thru, late_land_thru, _ = _split_start(
        _all_copies, NDEV - 1, "gather_late_start", late, late_lands, meta_g)
    w_full_t = w_in_g.reshape(in_w, D)
    c_glu = n_qkvz + n_ba
    c_zb, c_mg = c_glu + 2 * D, c_glu + 3 * D
    w_main_t = jnp.concatenate([w_full_t[:n_qkvz], w_full_t[c_glu:c_zb], w_full_t[c_mg:], w_full_t[c_zb:c_mg]],
                               axis=0)
    w_ba_t = jnp.pad(w_full_t[n_qkvz:n_qkvz + n_ba], ((0, LANE - n_ba), (0, 0)))
    meta_full = jnp.transpose(meta_g, (1, 0, 2)).reshape(NMETA, D)
    ab = jnp.pad(jnp.concatenate([a_log, dt_bias], axis=0), ((0, 0), (H, LANE - 2 * H)))

    x_ext = jnp.concatenate([jnp.zeros((pad, D), F32), meta_full, x[0]], axis=0)

    proj, ba, h = _proj_fwd(x_ext, norm_w, w_main_t, w_ba_t)
    _, (w_dn_g, w_cf_g, w_o_g, cqw_g, dww_g) = _split_wait(
        _all_copies, "gather_late_wait", late_send, late_recv, late_thru, late_land_thru, ba)
    w_dn, w_cf, w_oo = (t.reshape(D, D) for t in (w_dn_g, w_cf_g, w_o_g))
    cqw = jnp.transpose(cqw_g, (1, 0, 2)).reshape(KQ, 3 * D)
    dww = jnp.transpose(dww_g, (1, 0, 2)).reshape(KD, D)
    qkv, bg = _qkv_conv_fwd(proj, ba, cqw, ab, pad)
    o, sall, tall = _delta_fwd(qkv, bg)
    o_n, y_a = _o_post_fwd(o, proj, dn_norm_w, w_dn)
    c1, c3, y_b = _conv_b_fwd(proj, dww, dw_b, ln_w, ln_b, w_cf)
    merged, z = _merge_fwd(y_a, y_b, proj, b_cf_out, w_oo)
    dx_out, dx_out_b, loss_part, g_final_w = _final_fwd_bwd(x_ext, z, loss_target[0], final_norm_w.reshape(1, D))

    g_w_o = _mm_tn(merged, dx_out_b, "g_w_o_mm")
    dy_a, dy_b, dproj, g_b_cf = _merge_bwd(dx_out_b, w_oo, y_a, y_b, proj, b_cf_out)
    g_w_cf = _mm_tn(c3, dy_b, "g_w_cf_mm")
    g_w_dn = _mm_tn(o_n, dy_a, "g_w_dn_mm")
    dc1, dproj, sums_b = _conv_b_bwd1(dy_b, w_cf, c1, proj, ln_w, ln_b, dproj)
    dproj, g_dw_w = _conv_b_bwd2(dc1, proj, dww, dproj)
    do, dproj, g_dn_w = _o_post_bwd(dy_a, w_dn, o, proj, dn_norm_w, dproj)
    dqkv, dbg = _delta_bwd(qkv, bg, sall, tall, do)
    dproj, g_cqw = _qkv_conv_bwd(proj, dqkv, cqw, dproj)
    dba, dab = _ba_bwd(dbg, ba, ab, pad)
    g_w_main_t = _mm_tn(dproj, h, "g_w_main_mm")
    g_w_ba_t = _mm_tn(dba, h, "g_w_ba_mm")

    g_w_full_t = jnp.concatenate([g_w_main_t[:n_qkvz], g_w_ba_t[:n_ba], g_w_main_t[CB_GA_ * D:CB_MA * D],
                                  g_w_main_t[CB_ZB * D:], g_w_main_t[CB_MA * D:CB_ZB * D]], axis=0)
    big = [t.reshape(NCHIP, 2, t.shape[0] // NDEV, D) for t in (g_w_full_t, g_w_dn, g_w_cf, g_w_o)]
    sw_send, sw_recv, big_thru, sw_land, sw_token = _split_start(
        _sibling_copies, NCHIP, "swap_sibling_start", big,
        [lax.empty((NCHIP,) + t.shape[2:], t.dtype) for t in big], g_w_ba_t)
    dh = _dh_mm(dproj, dba, w_main_t, w_ba_t + sw_token[0, 0].astype(BF16), 0)
    big_back, from_sibling = _split_wait(_sibling_copies, "swap_sibling_wait", sw_send, sw_recv, big_thru, sw_land, dh)
    pairs = [_pair_add(a, g, f"pair_add_{i}") for i, (a, g) in enumerate(zip(big_back, from_sibling))]
    send_sems, recv_sems, pair_thru, land_thru, token = _split_start(
        _chip_copies, NCHIP - 1, "scatter_chips_start",
        [p for p, _ in pairs], [jnp.zeros(p.shape, p.dtype) for p, _ in pairs], g_w_ba_t)
    dh = _dh_mm(dproj, dba, w_main_t, w_ba_t + token[0, 0].astype(BF16), 1, dh)
    grad_x, dhead, g_norm_w = _prenorm_bwd(dh, x_ext, dx_out, norm_w, seq)
    _, from_chips = _split_wait(_chip_copies, "scatter_chips_wait", send_sems, recv_sems, pair_thru, land_thru,
                                g_norm_w)

    split_cols = lambda t: jnp.transpose(t.reshape(t.shape[0], NDEV, t.shape[1] // NDEV), (1, 0, 2))
    small = {"norm_w": g_norm_w, "a_log": dab[0:1, H:2 * H], "dt_bias": dab[1:2, H:2 * H], "dn_norm_w": g_dn_w,
             "dw_b": sums_b[2:3], "ln_w": sums_b[0:1], "ln_b": sums_b[1:2], "b_cf_out": g_b_cf,
             "final_norm_w": g_final_w}
    small_vec = jnp.concatenate([small[k] for k in SMALL], axis=1)
    ns = small_vec.shape[1]
    ns_pad = (-ns) % LANE
    small_vec = jnp.pad(small_vec, ((0, 0), (0, ns_pad)))
    p_meta, p_cqw, p_dww, p_small = _exchange(
        [split_cols(dhead[pad:pad + NMETA]), split_cols(g_cqw), split_cols(g_dw_w), small_vec],
        [True] * 3 + [False], "exchange_small")

    res = {}
    res["w_in"] = tuple(t.T for t in _adam_chips(pairs[0][1], from_chips[0], w_in[0].T, m_w_in[0].T, v_w_in[0].T,
                                                   "adam_w_in"))
    res["w_dn_out"] = _adam_chips(pairs[1][1], from_chips[1], w_dn_out[0], m_w_dn_out[0], v_w_dn_out[0], "adam_w_dn")
    res["w_cf_out"] = _adam_chips(pairs[2][1], from_chips[2], w_cf_out[0], m_w_cf_out[0], v_w_cf_out[0], "adam_w_cf")
    res["w_o"] = _adam_chips(pairs[3][1], from_chips[3], w_o[0], m_w_o[0], v_w_o[0], "adam_w_o")
    res["meta"] = _adam_reduce(p_meta, meta, m_meta, v_meta, "adam_meta")
    res["conv_qkv_w"] = _adam_reduce(p_cqw, conv_qkv_w[0], m_conv_qkv_w[0], v_conv_qkv_w[0], "adam_conv_qkv_w")
    res["dw_w"] = _adam_reduce(p_dww, dw_w[0], m_dw_w[0], v_dw_w[0], "adam_dw_w")
    loc = dict(norm_w=(norm_w, m_norm_w, v_norm_w), a_log=(a_log, m_a_log, v_a_log), dt_bias=(dt_bias, m_dt_bias, v_dt_bias),
               dn_norm_w=(dn_norm_w, m_dn_norm_w, v_dn_norm_w), dw_b=(dw_b, m_dw_b, v_dw_b), ln_w=(ln_w, m_ln_w, v_ln_w),
               ln_b=(ln_b, m_ln_b, v_ln_b), b_cf_out=(b_cf_out, m_b_cf_out, v_b_cf_out),
               final_norm_w=(final_norm_w, m_final_norm_w, v_final_norm_w))
    cat = lambda j: jnp.pad(jnp.concatenate([loc[k][j].reshape(1, -1) for k in SMALL], axis=1), ((0, 0), (0, ns_pad)))
    small_res = _adam_reduce(p_small, cat(0), cat(1), cat(2), "adam_small")
    off = 0
    for k in SMALL:
        wshape = loc[k][0].shape
        nk = loc[k][0].size
        res[k] = tuple(t[:, off:off + nk].reshape(wshape) for t in small_res)
        off += nk
    shaped = dict(w_in=w_in.shape, w_dn_out=w_dn_out.shape, w_cf_out=w_cf_out.shape, w_o=w_o.shape, meta=meta.shape,
                  conv_qkv_w=conv_qkv_w.shape, dw_w=dw_w.shape)
    for k, shp in shaped.items():
        res[k] = tuple(t.reshape(shp) for t in res[k])

    loss = lax.psum(loss_part[0, 0], ("x", "y", "c"))
    order = ("meta", "norm_w", "w_in", "conv_qkv_w", "a_log", "dt_bias", "dn_norm_w", "w_dn_out", "dw_w", "dw_b", "ln_w",
             "ln_b", "w_cf_out", "b_cf_out", "w_o", "final_norm_w")
    outs = [loss, grad_x[None]]
    for j in range(4):
        outs += [res[k][j] for k in order]
    return tuple(outs)
```

```python
import functools

import jax
import jax.numpy as jnp
from jax import lax
from jax.experimental import pallas as pl
from jax.experimental.pallas import tpu as pltpu

F32 = jnp.float32
BF16 = jnp.bfloat16
HI = lax.Precision.HIGHEST

D = 1024
H = 8
DK = 128
C = 64
NMETA = 16
KQ = 4
KD = 31
HALO_Q = 8
HALO_D = 32
EPS = 1e-6
NDEV = 8
LANE = 128
MIB = 1024 * 1024

ADAM_LR, ADAM_B1, ADAM_B2, ADAM_EPS, ADAM_WD, ADAM_STEP = 0.001, 0.9, 0.999, 1e-08, 0.01, 10

CB_Q, CB_K, CB_V, CB_ZA, CB_GA_, CB_GB_, CB_MA, CB_MB, CB_ZB = range(9)
NCB = 9


def _pick(n, cands):
    for c in cands:
        if n % c == 0:
            return c
    raise ValueError(f"no tile for {n}")


def _cp(sem=None, vmem_mib=40):
    kw = dict(vmem_limit_bytes=vmem_mib * MIB)
    if sem is not None:
        kw["dimension_semantics"] = sem
    return pltpu.CompilerParams(**kw)


def _call(body, **kw):
    return pl.pallas_call(body, **kw)


def _dot(a, b):
    return jnp.dot(a.astype(BF16), b.astype(BF16), preferred_element_type=F32)


def _dot_nt(a, b):
    return lax.dot_general(a.astype(BF16), b.astype(BF16), (((1,), (1,)), ((), ())), preferred_element_type=F32)


def _dot_tn(a, b):
    return lax.dot_general(a.astype(BF16), b.astype(BF16), (((0,), (0,)), ((), ())), preferred_element_type=F32)


def _dot_hi(a, b):
    return jnp.dot(a, b, precision=HI, preferred_element_type=F32)


def _sig(x):
    return 0.5 * jnp.tanh(0.5 * x) + 0.5


def _dsilu(x, s):
    return s * (1.0 + x * (1.0 - s))


def _rowsum(x):
    return jnp.sum(x, axis=-1, keepdims=True)


def _colsum(x):
    return jnp.sum(x, axis=0, keepdims=True)


def _exchange(arrs, scatter, name):
    n = len(arrs)
    out_shape = []
    for a, sc in zip(arrs, scatter):
        shp = a.shape if sc else (NDEV,) + a.shape
        out_shape.append(jax.ShapeDtypeStruct(shp, a.dtype))

    def body(*refs):
        ins, outs = refs[:n], refs[n:2 * n]
        send_sems, recv_sems, loc_sems = refs[2 * n:]
        x, y, c = lax.axis_index("x"), lax.axis_index("y"), lax.axis_index("c")
        me = 4 * x + 2 * y + c
        copies = []
        for a in range(n):
            for k in range(1, NDEV):
                px = 1 - x if (k >> 2) & 1 else x
                py = 1 - y if (k >> 1) & 1 else y
                pc = 1 - c if k & 1 else c
                src = ins[a].at[4 * px + 2 * py + pc] if scatter[a] else ins[a]
                cp = pltpu.make_async_remote_copy(
                    src_ref=src, dst_ref=outs[a].at[me],
                    send_sem=send_sems.at[a * (NDEV - 1) + k - 1], recv_sem=recv_sems.at[a * (NDEV - 1) + k - 1],
                    device_id=(px, py, pc), device_id_type=pl.DeviceIdType.MESH)
                cp.start()
                copies.append(cp)
            loc = pltpu.make_async_copy(ins[a].at[me] if scatter[a] else ins[a], outs[a].at[me], loc_sems.at[a])
            loc.start()
            copies.append(loc)
        for cp in copies:
            cp.wait()

    any_spec = pl.BlockSpec(memory_space=pl.ANY)
    return _call(
        body, name=name, out_shape=tuple(out_shape),
        in_specs=[any_spec] * n, out_specs=tuple([any_spec] * n),
        scratch_shapes=[pltpu.SemaphoreType.DMA((n * (NDEV - 1),)), pltpu.SemaphoreType.DMA((n * (NDEV - 1),)),
                        pltpu.SemaphoreType.DMA((n,))],
    )(*arrs)


NCHIP = 4


def _gather_two_level(arrs, name):
    n = len(arrs)
    per = NDEV - 1

    def body(*refs):
        ins, outs = refs[:n], refs[n:2 * n]
        send_sems, recv_sems, loc_sems = refs[2 * n:]
        x, y, c = lax.axis_index("x"), lax.axis_index("y"), lax.axis_index("c")
        me, sibling = (x, y, c), (x, y, 1 - c)
        chips = [(1 - x, y), (x, 1 - y), (1 - x, 1 - y)]

        def slot(a, px, py, pc):
            return outs[a].at[4 * px + 2 * py + pc]

        def copy(a, k, block, to, src=None):
            return pltpu.make_async_remote_copy(
                src_ref=slot(a, *block) if src is None else src, dst_ref=slot(a, *block),
                send_sem=send_sems.at[a * per + k], recv_sem=recv_sems.at[a * per + k],
                device_id=to, device_id_type=pl.DeviceIdType.MESH)

        local, sent = [], []
        for a in range(n):
            mine = pltpu.make_async_copy(ins[a], slot(a, *me), loc_sems.at[a])
            mine.start()
            local.append(mine)
            first = [copy(a, 1 + j, me, (*chip, c), src=ins[a]) for j, chip in enumerate(chips)]
            first.append(copy(a, 0, me, sibling, src=ins[a]))
            for cp in first:
                cp.start()
            sent += first
        for j, chip in enumerate(chips):
            for a in range(n):
                copy(a, 1 + j, (*chip, c), me).wait_recv()
                cp = copy(a, 4 + j, (*chip, c), sibling)
                cp.start()
                sent.append(cp)
        for a in range(n):
            copy(a, 0, sibling, me).wait_recv()
            for j, chip in enumerate(chips):
                copy(a, 4 + j, (*chip, 1 - c), me).wait_recv()
        for cp in sent:
            cp.wait_send()
        for cp in local:
            cp.wait()

    any_spec = pl.BlockSpec(memory_space=pl.ANY)
    return _call(
        body, name=name, out_shape=tuple(jax.ShapeDtypeStruct((NDEV,) + a.shape, a.dtype) for a in arrs),
        in_specs=[any_spec] * n, out_specs=tuple([any_spec] * n),
        scratch_shapes=[pltpu.SemaphoreType.DMA((n * per,)), pltpu.SemaphoreType.DMA((n * per,)),
                        pltpu.SemaphoreType.DMA((n,))],
    )(*arrs)


def _pair_add(arr4, got, name):
    _, _, r, n = arr4.shape
    tr, tc = _shard_tile(r, n)

    def body(a_ref, g_ref, p_ref, own_ref):
        c = lax.axis_index("c")
        my_chip = 2 * lax.axis_index("x") + lax.axis_index("y")
        s = jnp.where(c == 0, a_ref[0, 0], a_ref[0, 1]) + g_ref[0]
        p_ref[0] = s.astype(BF16)

        @pl.when(pl.program_id(2) == my_chip)
        def _():
            own_ref[...] = s

    return _call(
        body, name=name, grid=(r // tr, n // tc, NCHIP),
        in_specs=[pl.BlockSpec((1, 2, tr, tc), lambda i, k, j: (j, 0, i, k)),
                  pl.BlockSpec((1, tr, tc), lambda i, k, j: (j, i, k))],
        out_specs=(pl.BlockSpec((1, tr, tc), lambda i, k, j: (j, i, k)), pl.BlockSpec((tr, tc), lambda i, k, j: (i, k))),
        out_shape=(jax.ShapeDtypeStruct((NCHIP, r, n), BF16), jax.ShapeDtypeStruct((r, n), F32)),
        compiler_params=_cp(("parallel", "parallel", "arbitrary")),
    )(arr4, got)


def _shard_tile(r, n):
    return (128, n) if r % 128 == 0 else (r, 256)


def _all_copies(srcs, lands, send_sems, recv_sems):
    x, y, c = lax.axis_index("x"), lax.axis_index("y"), lax.axis_index("c")
    per = NDEV - 1
    copies = []
    for a in range(len(srcs)):
        for k in range(1, NDEV):
            px = 1 - x if (k >> 2) & 1 else x
            py = 1 - y if (k >> 1) & 1 else y
            pc = 1 - c if k & 1 else c
            copies.append(pltpu.make_async_remote_copy(
                src_ref=srcs[a], dst_ref=lands[a].at[4 * x + 2 * y + c],
                send_sem=send_sems.at[a * per + k - 1], recv_sem=recv_sems.at[a * per + k - 1],
                device_id=(px, py, pc), device_id_type=pl.DeviceIdType.MESH))
    return copies


def _split_start(make_copies, peers, name, arrs, lands, after):
    n = len(arrs)
    nsem = n * peers

    def body(*refs):
        srcs, land_in = refs[:n], refs[n:2 * n]
        send_sems, recv_sems = refs[2 * n + 1:2 * n + 3]
        token = refs[-1]
        for cp in make_copies(srcs, land_in, send_sems, recv_sems):
            cp.start()
        token[...] = jnp.zeros_like(token)

    hbm = pl.BlockSpec(memory_space=pltpu.HBM)
    sem = pl.BlockSpec(memory_space=pltpu.SEMAPHORE)
    both = list(arrs) + list(lands)
    outs = _call(
        body, name=name,
        out_shape=(pltpu.SemaphoreType.DMA((nsem,)), pltpu.SemaphoreType.DMA((nsem,)),
                   *[pltpu.HBM(a.shape, a.dtype) for a in both], jax.ShapeDtypeStruct((SUBLANES, LANE), F32)),
        in_specs=[hbm] * (2 * n) + [pl.BlockSpec(memory_space=pl.ANY)],
        out_specs=(sem, sem, *[hbm] * (2 * n), pl.BlockSpec(memory_space=pltpu.VMEM)),
        input_output_aliases={i: i + 2 for i in range(2 * n)},
        compiler_params=pltpu.CompilerParams(has_side_effects=pltpu.SideEffectType.DATAFLOW_SIDE_EFFECTING),
    )(*[pltpu.with_memory_space_constraint(t, pltpu.HBM) for t in both], after)
    return outs[0], outs[1], outs[2:2 + n], outs[2 + n:2 + 2 * n], outs[-1]


def _split_wait(make_copies, name, send_sems, recv_sems, arrs, lands, after):
    n = len(arrs)

    def body(*refs):
        srcs, land_in = refs[:n], refs[n:2 * n]
        send, recv = refs[2 * n], refs[2 * n + 1]
        for cp in make_copies(srcs, land_in, send, recv):
            cp.wait_send()
            cp.wait_recv()

    hbm = pl.BlockSpec(memory_space=pltpu.HBM)
    sem = pl.BlockSpec(memory_space=pltpu.SEMAPHORE)
    both = list(arrs) + list(lands)
    outs = _call(
        body, name=name,
        out_shape=tuple(pltpu.HBM(a.shape, a.dtype) for a in both),
        in_specs=[hbm] * (2 * n) + [sem, sem, pl.BlockSpec(memory_space=pl.ANY)], out_specs=tuple([hbm] * (2 * n)),
        input_output_aliases={i: i for i in range(2 * n)},
        compiler_params=pltpu.CompilerParams(has_side_effects=pltpu.SideEffectType.DATAFLOW_SIDE_EFFECTING),
    )(*both, send_sems, recv_sems, after)
    return outs[:n], outs[n:]


def _sibling_copies(srcs, lands, send_sems, recv_sems):
    x, y, c = lax.axis_index("x"), lax.axis_index("y"), lax.axis_index("c")
    copies = []
    for a in range(len(srcs)):
        for j in range(NCHIP):
            copies.append(pltpu.make_async_remote_copy(
                src_ref=srcs[a].at[j, 1 - c], dst_ref=lands[a].at[j],
                send_sem=send_sems.at[a * NCHIP + j], recv_sem=recv_sems.at[a * NCHIP + j],
                device_id=(x, y, 1 - c), device_id_type=pl.DeviceIdType.MESH))
    return copies


def _chip_copies(srcs, lands, send_sems, recv_sems):
    x, y, c = lax.axis_index("x"), lax.axis_index("y"), lax.axis_index("c")
    per = NCHIP - 1
    copies = []
    for a in range(len(srcs)):
        for k in range(1, NCHIP):
            px = 1 - x if (k >> 1) & 1 else x
            py = 1 - y if k & 1 else y
            copies.append(pltpu.make_async_remote_copy(
                src_ref=srcs[a].at[2 * px + py], dst_ref=lands[a].at[2 * x + y],
                send_sem=send_sems.at[a * per + k - 1], recv_sem=recv_sems.at[a * per + k - 1],
                device_id=(px, py, c), device_id_type=pl.DeviceIdType.MESH))
    return copies


def _mm_tn(a, b, name):
    t, m = a.shape
    n = b.shape[1]
    tt = _pick(t, (1664, 640, 128))
    tm = _pick(m, (1536, 1024, 512, 128))
    tn = _pick(n, (1152, 1024, 512, 128))
    nt = t // tt

    def body(a_ref, b_ref, o_ref):
        s = pl.program_id(2)
        part = _dot_tn(a_ref[...], b_ref[...])

        @pl.when(s == 0)
        def _():
            o_ref[...] = part

        @pl.when(s > 0)
        def _():
            o_ref[...] += part

    return _call(
        body, name=name, grid=(m // tm, n // tn, nt),
        in_specs=[pl.BlockSpec((tt, tm), lambda i, j, s: (s, i)), pl.BlockSpec((tt, tn), lambda i, j, s: (s, j))],
        out_specs=pl.BlockSpec((tm, tn), lambda i, j, s: (i, j)),
        out_shape=jax.ShapeDtypeStruct((m, n), F32),
        compiler_params=_cp(("parallel", "parallel", "arbitrary"), vmem_mib=52),
    )(a, b)


def _proj_fwd(x_ext, norm_w, w_main_t, w_ba_t):
    lp = x_ext.shape[0]
    n = w_main_t.shape[0]
    tm = _pick(lp, (1040, 832, 640, 320))
    tn = 1024

    def body(x_ref, nw_ref, w_ref, wba_ref, proj_ref, ba_ref, h_ref):
        @pl.when(pl.program_id(1) == 0)
        def _():
            x = x_ref[...]
            r = lax.rsqrt(jnp.mean(x * x, axis=-1, keepdims=True) + EPS)
            h = (x * r * nw_ref[...]).astype(BF16)
            h_ref[...] = h
            ba_ref[...] = _dot_nt(h, wba_ref[...])

        proj_ref[...] = _dot_nt(h_ref[...], w_ref[...])

    return _call(
        body, name="proj_fwd", grid=(lp // tm, n // tn),
        in_specs=[pl.BlockSpec((tm, D), lambda i, j: (i, 0)), pl.BlockSpec((1, D), lambda i, j: (0, 0)),
                  pl.BlockSpec((tn, D), lambda i, j: (j, 0)), pl.BlockSpec((LANE, D), lambda i, j: (0, 0))],
        out_specs=(pl.BlockSpec((tm, tn), lambda i, j: (i, j)), pl.BlockSpec((tm, LANE), lambda i, j: (i, 0)),
                   pl.BlockSpec((tm, D), lambda i, j: (i, 0))),
        out_shape=(jax.ShapeDtypeStruct((lp, n), F32), jax.ShapeDtypeStruct((lp, LANE), F32),
                   jax.ShapeDtypeStruct((lp, D), BF16)),
        compiler_params=_cp(("parallel", "arbitrary")),
    )(x_ext, norm_w, w_main_t, w_ba_t)


def _dh_mm(dproj, dba, w_main_t, w_ba_t, part, dh_so_far=None):
    lp, n = dproj.shape
    tm = _pick(lp, (1040, 832, 640, 320))
    tn = 1024
    tk = 2304
    nk = n // tk
    tiles = lp // tm
    first = (tiles + 1) // 2
    t0, nt = (0, first) if part == 0 else (first, tiles - first)
    if nt == 0:
        return dh_so_far

    def body(a_ref, ba_ref, b_ref, bba_ref, *rest):
        o_ref, acc = rest[-2:]
        kk = pl.program_id(2)

        @pl.when(kk == 0)
        def _():
            acc[...] = jnp.dot(ba_ref[...], bba_ref[...], preferred_element_type=F32)

        acc[...] += jnp.dot(a_ref[...], b_ref[...], preferred_element_type=F32)

        @pl.when(kk == nk - 1)
        def _():
            o_ref[...] = acc[...]

    prev = [] if dh_so_far is None else [dh_so_far]
    return _call(
        body, name=f"dh_mm_{part}", grid=(nt, D // tn, nk),
        in_specs=[pl.BlockSpec((tm, tk), lambda i, j, kk: (i + t0, kk)),
                  pl.BlockSpec((tm, LANE), lambda i, j, kk: (i + t0, 0)),
                  pl.BlockSpec((tk, tn), lambda i, j, kk: (kk, j)), pl.BlockSpec((LANE, tn), lambda i, j, kk: (0, j))]
                 + [pl.BlockSpec(memory_space=pl.ANY)] * len(prev),
        out_specs=pl.BlockSpec((tm, tn), lambda i, j, kk: (i + t0, j)),
        out_shape=jax.ShapeDtypeStruct((lp, D), F32),
        input_output_aliases={4: 0} if prev else {},
        scratch_shapes=[pltpu.VMEM((tm, tn), F32)],
        compiler_params=_cp(("parallel", "parallel", "arbitrary")),
    )(dproj, dba, w_main_t, w_ba_t, *prev)


def _beta_g(ba, ab, row0, pad):
    lane = lax.broadcasted_iota(jnp.int32, ba.shape, 1)
    rows = row0 + lax.broadcasted_iota(jnp.int32, ba.shape, 0)
    z = ba + ab[1:2, :]
    sp = jnp.maximum(z, 0.0) + jnp.log(1.0 + jnp.exp(-jnp.abs(z)))
    val = jnp.where(lane < H, _sig(ba), -jnp.exp(ab[0:1, :]) * sp)
    return jnp.where((lane < 2 * H) & (rows >= pad), val, 0.0)


def _qkv_conv_fwd(proj, ba, conv_w, ab, pad):
    lp = proj.shape[0]
    te = _pick(lp, (640, 320))
    hb = te // HALO_Q

    def body(main_ref, halo_ref, cw_ref, ba_ref, ab_ref, out_ref, bg_ref, pre_scr, tap_scr):
        i, s = pl.program_id(0), pl.program_id(1)
        pre_scr[:HALO_Q, :] = jnp.where(i > 0, halo_ref[...], 0.0)
        pre_scr[HALO_Q:, :] = main_ref[...]
        scale = jnp.where(s == 0, DK ** -0.5, 1.0)
        off = HALO_Q - (KQ - 1)

        def head(h, carry):
            cs = pl.ds(pl.multiple_of(h * DK, DK), DK)
            for j in range(KQ - 1):
                tap_scr[j] = pre_scr[off + j:off + j + te, cs]
            co = cw_ref[KQ - 1:KQ, cs] * pre_scr[HALO_Q:, cs]
            for j in range(KQ - 1):
                co = co + cw_ref[j:j + 1, cs] * tap_scr[j]
            a = co * _sig(co)
            r = lax.rsqrt(_rowsum(a * a) + EPS)
            out_ref[:, cs] = jnp.where(s == 2, a, a * (r * scale))
            return carry

        lax.fori_loop(0, H, head, 0, unroll=True)

        @pl.when(s == 0)
        def _():
            bg_ref[...] = _beta_g(ba_ref[...], ab_ref[...], i * te, pad)

    return _call(
        body, name="qkv_conv_fwd", grid=(lp // te, 3),
        in_specs=[pl.BlockSpec((te, D), lambda i, s: (i, s)),
                  pl.BlockSpec((HALO_Q, D), lambda i, s: (jnp.maximum(i * hb - 1, 0), s)),
                  pl.BlockSpec((KQ, D), lambda i, s: (0, s)),
                  pl.BlockSpec((te, LANE), lambda i, s: (i, 0)),
                  pl.BlockSpec((2, LANE), lambda i, s: (0, 0))],
        out_specs=(pl.BlockSpec((te, D), lambda i, s: (i, s)), pl.BlockSpec((te, LANE), lambda i, s: (i, 0))),
        out_shape=(jax.ShapeDtypeStruct((lp, 3 * D), F32), jax.ShapeDtypeStruct((lp, LANE), F32)),
        scratch_shapes=[pltpu.VMEM((te + HALO_Q, D), F32), pltpu.VMEM((KQ - 1, te, DK), F32)],
        compiler_params=_cp(("parallel", "arbitrary")),
    )(proj, proj, conv_w, ba, ab)


def _tri_masks():
    row = lax.broadcasted_iota(jnp.int32, (C, C), 0)
    col = lax.broadcasted_iota(jnp.int32, (C, C), 1)
    return row, col


def _split(a):
    hi = a.astype(BF16)
    return hi, (a - hi.astype(F32)).astype(BF16)


def _dot3(a, b, dims=(((1,), (0,)), ((), ()))):
    (ah, al), (bh, bl) = a, b
    mm = lambda x, y: lax.dot_general(x, y, dims, preferred_element_type=F32)
    return mm(ah, bh) + (mm(ah, bl) + mm(al, bh))


CHUNKS_PER_STEP = 5
CHUNKS_PER_STEP_BWD = 1
TINV_BLOCK = 16


def _tinv(ns, row, col):
    eye = (row == col).astype(F32)
    sh = TINV_BLOCK.bit_length() - 1
    same16 = (row >> sh) == (col >> sh)
    same32 = (row >> (sh + 1)) == (col >> (sh + 1))
    ys = [jnp.where(same16, -n, 0.0) for n in ns]
    ts = [eye + y for y in ys]
    sp = [_split(y) for y in ys]
    for level in range(3):
        yks = [_dot3(s, s) for s in sp]
        sp = [_split(yk) for yk in yks]
        ts = [t + _dot3(s, _split(t)) for s, t in zip(sp, ts)]
    for mask in (same32 & ~same16, ~same32):
        tsp = [_split(t) for t in ts]
        inner = [_dot3(_split(jnp.where(mask, n, 0.0)), t) for n, t in zip(ns, tsp)]
        ts = [t - _dot3(tp, _split(a)) for t, tp, a in zip(ts, tsp, inner)]
    return ts


def _chunk_common(q, k, v, bcol, gcc, gcr, incl, strict):
    dm = jnp.where(incl, jnp.exp(gcc - gcr), 0.0)
    kk = _dot_nt(k, k)
    qk = _dot_nt(q, k)
    gccw = jnp.broadcast_to(gcc, (C, DK))
    egc = jnp.exp(gccw)
    glast = gccw[C - 1:C, :]
    eend = jnp.exp(glast - gccw)
    elast = jnp.exp(glast)
    rhs = jnp.concatenate([v * bcol, k * (bcol * egc)], axis=1)
    return dm, kk, qk, egc, eend, elast, rhs


def _delta_fwd(qkv, bg):
    lp = qkv.shape[0]
    nc = lp // C
    heads = range(H)
    sls = [slice(h * DK, (h + 1) * DK) for h in heads]

    def body(q_ref, k_ref, v_ref, bg_ref, o_ref, sall_ref, tall_ref, s_scr):
        @pl.when(pl.program_id(0) == 0)
        def _():
            s_scr[...] = jnp.zeros_like(s_scr)

        row, col = _tri_masks()
        incl, strict = row >= col, row > col

        def prepare(sub):
            rs = slice(sub * C, (sub + 1) * C)
            bgt = bg_ref[rs, :]
            gc_all = _dot_hi(incl.astype(F32), bgt)
            gc_t = _dot_hi(bgt.T, (row <= col).astype(F32))
            qs, ks, vs = ([r[rs, sl] for sl in sls] for r in (q_ref, k_ref, v_ref))
            bcols = [jnp.broadcast_to(bgt[:, h:h + 1], (C, DK)) for h in heads]
            cm = [_chunk_common(qs[h], ks[h], vs[h], bcols[h], gc_all[:, H + h:H + h + 1], gc_t[H + h:H + h + 1, :],
                                incl, strict) for h in heads]
            dms, kks, qks, egcs, eends, elasts, rhss = zip(*cm)
            ts = _tinv([jnp.where(strict, bcols[h][:, :C] * kks[h] * dms[h], 0.0) for h in heads], row, col)
            sols = [_dot3(_split(ts[h]), _split(rhss[h])) for h in heads]
            qgs = [(qs[h] * egcs[h]).astype(BF16) for h in heads]
            ps = [(qks[h] * dms[h]).astype(BF16) for h in heads]
            kends = [(ks[h] * eends[h]).astype(BF16) for h in heads]
            return ts, sols, qgs, ps, kends, elasts

        prepared = [prepare(sub) for sub in range(CHUNKS_PER_STEP)]
        ss = [s_scr[h] for h in heads]
        for sub in range(CHUNKS_PER_STEP):
            rs = slice(sub * C, (sub + 1) * C)
            ts, sols, qgs, ps, kends, elasts = prepared[sub]
            sb = [s.astype(BF16) for s in ss]
            wvb = [(sols[h][:, :DK] - _dot(sols[h][:, DK:], sb[h])).astype(BF16) for h in heads]
            for h in heads:
                o_ref[rs, sls[h]] = _dot(qgs[h], sb[h]) + _dot(ps[h], wvb[h])
                sall_ref[sub, h] = ss[h]
                tall_ref[sub, h] = ts[h]
            ss = [ss[h] * elasts[h] + _dot_tn(kends[h], wvb[h]) for h in heads]
        for h in heads:
            s_scr[h] = ss[h]

    rows = CHUNKS_PER_STEP * C
    blk = lambda j: pl.BlockSpec((rows, D), lambda n: (n, j))
    return _call(
        body, name="delta_fwd", grid=(nc // CHUNKS_PER_STEP,),
        in_specs=[blk(0), blk(1), blk(2), pl.BlockSpec((rows, LANE), lambda n: (n, 0))],
        out_specs=(pl.BlockSpec((rows, D), lambda n: (n, 0)),
                   pl.BlockSpec((CHUNKS_PER_STEP, H, DK, DK), lambda n: (n, 0, 0, 0)),
                   pl.BlockSpec((CHUNKS_PER_STEP, H, C, C), lambda n: (n, 0, 0, 0))),
        out_shape=(jax.ShapeDtypeStruct((lp, D), F32), jax.ShapeDtypeStruct((nc, H, DK, DK), F32),
                   jax.ShapeDtypeStruct((nc, H, C, C), F32)),
        scratch_shapes=[pltpu.VMEM((H, DK, DK), F32)],
        compiler_params=_cp(("arbitrary",)),
    )(qkv, qkv, qkv, bg)


def _delta_bwd(qkv, bg, sall, tall, do):
    lp = qkv.shape[0]
    nc = lp // C

    heads = range(H)
    sls = [slice(h * DK, (h + 1) * DK) for h in heads]

    def body(q_ref, k_ref, v_ref, bg_ref, sall_ref, tall_ref, do_ref, dqkv_ref, dbg_ref, ds_scr):
        @pl.when(pl.program_id(0) == 0)
        def _():
            ds_scr[...] = jnp.zeros_like(ds_scr)

        dsns = [ds_scr[h] for h in heads]
        for sub in reversed(range(CHUNKS_PER_STEP_BWD)):
            dsns = chunk(sub, dsns, q_ref, k_ref, v_ref, bg_ref, sall_ref, tall_ref, do_ref, dqkv_ref, dbg_ref)
        for h in heads:
            ds_scr[h] = dsns[h]

    def chunk(sub, dsns, q_ref, k_ref, v_ref, bg_ref, sall_ref, tall_ref, do_ref, dqkv_ref, dbg_ref):
        rs = slice(sub * C, (sub + 1) * C)
        bgt = bg_ref[rs, :]
        row, col = _tri_masks()
        incl, strict = row >= col, row > col
        upper = (row <= col).astype(F32)
        gc_all = _dot_hi(incl.astype(F32), bgt)
        gc_t = _dot_hi(bgt.T, upper)
        lane = lax.broadcasted_iota(jnp.int32, (C, LANE), 1)
        lastrow = lax.broadcasted_iota(jnp.int32, (C, 1), 0) == C - 1
        qs, ks, vs, dos = ([r[rs, sl] for sl in sls] for r in (q_ref, k_ref, v_ref, do_ref))
        bcols = [jnp.broadcast_to(bgt[:, h:h + 1], (C, DK)) for h in heads]
        cm = [_chunk_common(qs[h], ks[h], vs[h], bcols[h], gc_all[:, H + h:H + h + 1], gc_t[H + h:H + h + 1, :],
                            incl, strict) for h in heads]
        dms, kks, qks, egcs, eends, elasts, rhss = zip(*cm)
        ss = [sall_ref[sub, h] for h in heads]
        ts = [tall_ref[sub, h] for h in heads]
        sb = [s.astype(BF16) for s in ss]
        dsb = [d.astype(BF16) for d in dsns]
        dob = [d.astype(BF16) for d in dos]
        sols = [_dot3(_split(ts[h]), _split(rhss[h])) for h in heads]
        ws = [sol[:, DK:] for sol in sols]
        qgs = [qs[h] * egcs[h] for h in heads]
        kends = [ks[h] * eends[h] for h in heads]
        wvs = [sols[h][:, :DK] - _dot(ws[h], sb[h]) for h in heads]
        wvb = [wv.astype(BF16) for wv in wvs]
        dwvs = [_dot_tn(qks[h] * dms[h], dob[h]) + _dot(kends[h], dsb[h]) for h in heads]
        dps = [jnp.where(incl, _dot_nt(dob[h], wvb[h]), 0.0) for h in heads]
        dqgs = [_dot_nt(dob[h], sb[h]) for h in heads]
        dkends = [_dot_nt(wvb[h], dsb[h]) for h in heads]
        ds_before = [_dot_tn(qgs[h], dob[h]) + elasts[h] * dsns[h] - _dot_tn(ws[h], dwvs[h]) for h in heads]
        dglasts = [elasts[h] * jnp.sum(ss[h] * dsns[h], keepdims=True) for h in heads]
        dws = [-_dot_nt(dwvs[h], sb[h]) for h in heads]
        tts = [_split(ts[h].T) for h in heads]
        drhss = [_dot3(tts[h], _split(jnp.concatenate([dwvs[h], dws[h]], axis=1))) for h in heads]
        nt_dims = (((1,), (1,)), ((), ()))
        dns = [jnp.where(strict, -_dot3(_split(drhss[h]), _split(sols[h]), nt_dims), 0.0) for h in heads]
        dbeta_t = jnp.zeros((C, LANE), F32)
        dgc_t = jnp.zeros((C, LANE), F32)
        for h in heads:
            q, k, v, bcol, dm, kk, qk, egc, eend = qs[h], ks[h], vs[h], bcols[h], dms[h], kks[h], qks[h], egcs[h], eends[h]
            drv, drk = drhss[h][:, :DK], drhss[h][:, DK:]
            dn, dp, dqg, dkend = dns[h], dps[h], dqgs[h], dkends[h]
            rk = _rowsum(drk * k)
            dkk = dn * (bcol[:, :C] * dm)
            dqk = dp * dm
            e = (dn * (bcol[:, :C] * kk) + dp * qk) * dm
            tk = _rowsum(dkend * kends[h])
            dgc = rk * bcol * egc + _rowsum(e) - _rowsum(e.T) + _rowsum(dqg * qgs[h]) - tk
            dgc = dgc + jnp.where(lastrow, dglasts[h] + jnp.sum(tk, keepdims=True), 0.0)
            dbeta = _rowsum(drv * v) + rk * egc + _rowsum(dn * kk * dm)
            dqkv_ref[rs, sls[h]] = _dot(dqk, k) + dqg * egc
            dqkv_ref[rs, D + h * DK:D + (h + 1) * DK] = (drk * (bcol * egc) + _dot(dkk, k) + _dot_tn(dkk, k)
                                                        + _dot_tn(dqk, q) + dkend * eend)
            dqkv_ref[rs, 2 * D + h * DK:2 * D + (h + 1) * DK] = bcol * drv
            dbeta_t = jnp.where(lane == h, dbeta, dbeta_t)
            dgc_t = jnp.where(lane == H + h, dgc, dgc_t)
        dbg_ref[rs, :] = dbeta_t + _dot_hi(upper, dgc_t)
        return ds_before

    steps = nc // CHUNKS_PER_STEP_BWD
    rows = CHUNKS_PER_STEP_BWD * C
    rev = lambda n: steps - 1 - n
    blk = lambda j: pl.BlockSpec((rows, D), lambda n: (rev(n), j))
    return _call(
        body, name="delta_bwd", grid=(steps,),
        in_specs=[blk(0), blk(1), blk(2), pl.BlockSpec((rows, LANE), lambda n: (rev(n), 0)),
                  pl.BlockSpec((CHUNKS_PER_STEP_BWD, H, DK, DK), lambda n: (rev(n), 0, 0, 0)),
                  pl.BlockSpec((CHUNKS_PER_STEP_BWD, H, C, C), lambda n: (rev(n), 0, 0, 0)),
                  pl.BlockSpec((rows, D), lambda n: (rev(n), 0))],
        out_specs=(pl.BlockSpec((rows, 3 * D), lambda n: (rev(n), 0)),
                   pl.BlockSpec((rows, LANE), lambda n: (rev(n), 0))),
        out_shape=(jax.ShapeDtypeStruct((lp, 3 * D), F32), jax.ShapeDtypeStruct((lp, LANE), F32)),
        scratch_shapes=[pltpu.VMEM((H, DK, DK), F32)],
        compiler_params=_cp(("arbitrary",)),
    )(qkv, qkv, qkv, bg, sall, tall, do)


def _o_post_fwd(o, proj, dn_w, w_dn):
    lp = o.shape[0]
    te = _pick(lp, (640, 320))

    def body(o_ref, za_ref, w_ref, wdn_ref, out_ref, ya_ref):
        za = za_ref[...]
        gate = za * _sig(za)
        for h in range(H):
            sl = slice(h * DK, (h + 1) * DK)
            oh = o_ref[:, sl]
            r = lax.rsqrt(jnp.mean(oh * oh, axis=-1, keepdims=True) + EPS)
            out_ref[:, sl] = (oh * r * w_ref[...] * gate[:, sl]).astype(BF16)
        ya_ref[...] = _dot(out_ref[...], wdn_ref[...])

    row = pl.BlockSpec((te, D), lambda i: (i, 0))
    return _call(
        body, name="o_post_fwd", grid=(lp // te,),
        in_specs=[row, pl.BlockSpec((te, D), lambda i: (i, CB_ZA)), pl.BlockSpec((1, DK), lambda i: (0, 0)),
                  pl.BlockSpec((D, D), lambda i: (0, 0))],
        out_specs=(row, row),
        out_shape=(jax.ShapeDtypeStruct((lp, D), BF16), jax.ShapeDtypeStruct((lp, D), F32)),
        compiler_params=_cp(("parallel",)),
    )(o, proj, dn_w, w_dn)


def _o_post_bwd(dy_a, w_dn, o, proj, dn_w, dproj):
    lp = o.shape[0]
    te = _pick(lp, (640, 320))

    def body(dya_ref, wdn_ref, o_ref, za_ref, w_ref, _, do_ref, dza_ref, dw_ref, don_ref):
        @pl.when(pl.program_id(0) == 0)
        def _():
            dw_ref[...] = jnp.zeros_like(dw_ref)

        don_ref[...] = _dot_nt(dya_ref[...], wdn_ref[...])
        za = za_ref[...]
        sz = _sig(za)
        gate, dgate = za * sz, _dsilu(za, sz)
        w = w_ref[...]
        dw = jnp.zeros((1, DK), F32)
        for h in range(H):
            sl = slice(h * DK, (h + 1) * DK)
            oh, g = o_ref[:, sl], don_ref[:, sl]
            r = lax.rsqrt(jnp.mean(oh * oh, axis=-1, keepdims=True) + EPS)
            ohat = oh * r
            dza_ref[:, sl] = (g * ohat * w * dgate[:, sl]).astype(BF16)
            don = g * gate[:, sl]
            dw = dw + _colsum(don * ohat)
            dohat = don * w
            do_ref[:, sl] = r * (dohat - ohat * jnp.mean(dohat * ohat, axis=-1, keepdims=True))
        dw_ref[...] += dw

    return _call(
        body, name="o_post_bwd", grid=(lp // te,),
        in_specs=[pl.BlockSpec((te, D), lambda i: (i, 0)), pl.BlockSpec((D, D), lambda i: (0, 0)),
                  pl.BlockSpec((te, D), lambda i: (i, 0)),
                  pl.BlockSpec((te, D), lambda i: (i, CB_ZA)), pl.BlockSpec((1, DK), lambda i: (0, 0)),
                  pl.BlockSpec(memory_space=pl.ANY)],
        out_specs=(pl.BlockSpec((te, D), lambda i: (i, 0)), pl.BlockSpec((te, D), lambda i: (i, CB_ZA)),
                   pl.BlockSpec((1, DK), lambda i: (0, 0))),
        out_shape=(jax.ShapeDtypeStruct((lp, D), F32), jax.ShapeDtypeStruct(dproj.shape, dproj.dtype),
                   jax.ShapeDtypeStruct((1, DK), F32)),
        input_output_aliases={5: 1},
        scratch_shapes=[pltpu.VMEM((te, D), F32)],
        compiler_params=_cp(("arbitrary",), vmem_mib=52),
    )(dy_a, w_dn, o, proj, dn_w, dproj)


def _qkv_conv_bwd(proj, dqkv, conv_w, dproj):
    lp = proj.shape[0]
    te = _pick(lp, (640, 320))
    hb = te // HALO_Q
    nt = lp // te
    last_hb = lp // HALO_Q - 1

    def body(main_ref, prev_ref, next_ref, dmain_ref, dnext_ref, cw_ref, _, dpre_ref, dcw_ref, pre_scr, dn_scr,
             tap_scr, dco_scr, dsh_scr):
        s, i = pl.program_id(0), pl.program_id(1)

        @pl.when(i == 0)
        def _():
            dcw_ref[...] = jnp.zeros_like(dcw_ref)

        ne = te + HALO_Q
        pre_scr[:HALO_Q, :] = jnp.where(i > 0, prev_ref[...], 0.0)
        pre_scr[HALO_Q:ne, :] = main_ref[...]
        pre_scr[ne:, :] = jnp.where(i < nt - 1, next_ref[...], 0.0)
        dn_scr[:te, :] = dmain_ref[...]
        dn_scr[te:, :] = jnp.where(i < nt - 1, dnext_ref[...], 0.0)
        scale = jnp.where(s == 0, DK ** -0.5, 1.0)
        off = HALO_Q - (KQ - 1)

        def head(h, carry):
            cs = pl.ds(pl.multiple_of(h * DK, DK), DK)
            for j in range(KQ - 1):
                tap_scr[j] = pre_scr[off + j:off + j + ne, cs]
            taps = [tap_scr[j] for j in range(KQ - 1)] + [pre_scr[HALO_Q:, cs]]
            co = cw_ref[0:1, cs] * taps[0]
            for j in range(1, KQ):
                co = co + cw_ref[j:j + 1, cs] * taps[j]
            sg = _sig(co)
            a = co * sg
            g = dn_scr[:, cs]
            r = lax.rsqrt(_rowsum(a * a) + EPS)
            yhat = a * r
            da = jnp.where(s == 2, g, (scale * r) * (g - yhat * _rowsum(g * yhat)))
            dco = da * _dsilu(co, sg)
            dco_scr[...] = dco
            for j in range(KQ - 1):
                dsh_scr[j] = dco_scr[KQ - 1 - j:KQ - 1 - j + te, :]
            dpre = cw_ref[KQ - 1:KQ, cs] * dco[:te, :]
            for j in range(KQ - 1):
                dpre = dpre + cw_ref[j:j + 1, cs] * dsh_scr[j]
            dpre_ref[:, cs] = dpre.astype(BF16)
            dcw_ref[:, cs] += jnp.concatenate([_colsum(dco[:te] * taps[j][:te]) for j in range(KQ)], axis=0)
            return carry

        lax.fori_loop(0, H, head, 0, unroll=True)

    return _call(
        body, name="qkv_conv_bwd", grid=(3, nt),
        in_specs=[pl.BlockSpec((te, D), lambda s, i: (i, s)),
                  pl.BlockSpec((HALO_Q, D), lambda s, i: (jnp.maximum(i * hb - 1, 0), s)),
                  pl.BlockSpec((HALO_Q, D), lambda s, i: (jnp.minimum((i + 1) * hb, last_hb), s)),
                  pl.BlockSpec((te, D), lambda s, i: (i, s)),
                  pl.BlockSpec((HALO_Q, D), lambda s, i: (jnp.minimum((i + 1) * hb, last_hb), s)),
                  pl.BlockSpec((KQ, D), lambda s, i: (0, s)),
                  pl.BlockSpec(memory_space=pl.ANY)],
        out_specs=(pl.BlockSpec((te, D), lambda s, i: (i, s)), pl.BlockSpec((KQ, D), lambda s, i: (0, s))),
        out_shape=(jax.ShapeDtypeStruct(dproj.shape, dproj.dtype), jax.ShapeDtypeStruct((KQ, 3 * D), F32)),
        input_output_aliases={6: 0},
        scratch_shapes=[pltpu.VMEM((te + 2 * HALO_Q, D), F32), pltpu.VMEM((te + HALO_Q, D), F32),
                        pltpu.VMEM((KQ - 1, te + HALO_Q, DK), F32), pltpu.VMEM((te + HALO_Q, DK), F32),
                        pltpu.VMEM((KQ - 1, te, DK), F32)],
        compiler_params=_cp(("arbitrary", "arbitrary")),
    )(proj, proj, proj, dqkv, dqkv, conv_w, dproj)


def _ba_bwd(dbg, ba, ab, pad):
    lp = ba.shape[0]
    te = _pick(lp, (640, 320))

    def body(dbg_ref, ba_ref, ab_ref, dba_ref, dab_ref):
        i = pl.program_id(0)

        @pl.when(i == 0)
        def _():
            dab_ref[...] = jnp.zeros_like(dab_ref)

        ba, ab = ba_ref[...], ab_ref[...]
        lane = lax.broadcasted_iota(jnp.int32, ba.shape, 1)
        rows = i * te + lax.broadcasted_iota(jnp.int32, ba.shape, 0)
        g = jnp.where((lane < 2 * H) & (rows >= pad), dbg_ref[...], 0.0)
        sb = _sig(ba)
        z = ba + ab[1:2, :]
        sp = jnp.maximum(z, 0.0) + jnp.log(1.0 + jnp.exp(-jnp.abs(z)))
        nea = -jnp.exp(ab[0:1, :])
        dz = g * nea * _sig(z)
        dba_ref[...] = jnp.where(lane < H, g * sb * (1.0 - sb), dz).astype(BF16)
        is_g = (lane >= H) & (lane < 2 * H)
        dab_ref[...] += jnp.concatenate([_colsum(jnp.where(is_g, g * nea * sp, 0.0)),
                                         _colsum(jnp.where(is_g, dz, 0.0))], axis=0)

    return _call(
        body, name="ba_bwd", grid=(lp // te,),
        in_specs=[pl.BlockSpec((te, LANE), lambda i: (i, 0)), pl.BlockSpec((te, LANE), lambda i: (i, 0)),
                  pl.BlockSpec((2, LANE), lambda i: (0, 0))],
        out_specs=(pl.BlockSpec((te, LANE), lambda i: (i, 0)), pl.BlockSpec((2, LANE), lambda i: (0, 0))),
        out_shape=(jax.ShapeDtypeStruct((lp, LANE), BF16), jax.ShapeDtypeStruct((2, LANE), F32)),
        compiler_params=_cp(("arbitrary",)),
    )(dbg, ba, ab)


SUBLANES = 8
CONV_RB = 64


def _fill_shifted(sh_scr, src_scr, cs):
    n = sh_scr.shape[1]
    for s in range(1, SUBLANES):
        sh_scr[s] = src_scr[s:s + n, cs]


def _shifted(sh_scr, src_scr, cs, r, r0, n):
    s, a8 = r % SUBLANES, r - r % SUBLANES
    if s == 0:
        return src_scr[r0 + a8:r0 + a8 + n, cs]
    return sh_scr[s, r0 + a8:r0 + a8 + n, :]


def _conv_b_fwd(proj, dw_w, dw_b, ln_w, ln_b, w_cf):
    lp = proj.shape[0]
    te = _pick(lp, (640, 320))
    hb = te // HALO_D

    def body(a_ref, b_ref, ha_ref, hb_ref, zb_ref, w_ref, wb_ref, lw_ref, lb_ref, wcf_ref, c1_ref, c3_ref, yb_ref,
             c0_scr, sh_scr):
        i = pl.program_id(0)
        c0_scr[:HALO_D, :] = jnp.where(i > 0, ha_ref[...] * _sig(hb_ref[...]), 0.0)
        c0_scr[HALO_D:, :] = a_ref[...] * _sig(b_ref[...])
        off = HALO_D - (KD - 1)
        def lane_block(cb, carry):
            cs = pl.ds(pl.multiple_of(cb * LANE, LANE), LANE)
            _fill_shifted(sh_scr, c0_scr, cs)
            for r0 in range(0, te, CONV_RB):
                acc = None
                for j in range(KD):
                    term = w_ref[j:j + 1, cs] * _shifted(sh_scr, c0_scr, cs, off + j, r0, CONV_RB)
                    acc = term if acc is None else acc + term
                c1_ref[r0:r0 + CONV_RB, cs] = acc + wb_ref[:, cs]
            return carry

        lax.fori_loop(0, D // LANE, lane_block, 0)
        c1 = c1_ref[...]
        mu = jnp.mean(c1, axis=-1, keepdims=True)
        xc = c1 - mu
        c2 = xc * lax.rsqrt(jnp.mean(xc * xc, axis=-1, keepdims=True) + EPS) * lw_ref[...] + lb_ref[...]
        zb = zb_ref[...]
        c3 = (c2 * _sig(c2) * zb * _sig(zb)).astype(BF16)
        c3_ref[...] = c3
        yb_ref[...] = _dot(c3, wcf_ref[...])

    vec = pl.BlockSpec((1, D), lambda i: (0, 0))
    row = pl.BlockSpec((te, D), lambda i: (i, 0))
    return _call(
        body, name="conv_b_fwd", grid=(lp // te,),
        in_specs=[pl.BlockSpec((te, D), lambda i: (i, CB_GA_)), pl.BlockSpec((te, D), lambda i: (i, CB_GB_)),
                  pl.BlockSpec((HALO_D, D), lambda i: (jnp.maximum(i * hb - 1, 0), CB_GA_)),
                  pl.BlockSpec((HALO_D, D), lambda i: (jnp.maximum(i * hb - 1, 0), CB_GB_)),
                  pl.BlockSpec((te, D), lambda i: (i, CB_ZB)),
                  pl.BlockSpec((KD, D), lambda i: (0, 0)), vec, vec, vec, pl.BlockSpec((D, D), lambda i: (0, 0))],
        out_specs=(row, row, row),
        out_shape=(jax.ShapeDtypeStruct((lp, D), F32), jax.ShapeDtypeStruct((lp, D), BF16),
                   jax.ShapeDtypeStruct((lp, D), F32)),
        scratch_shapes=[pltpu.VMEM((te + HALO_D, D), F32), pltpu.VMEM((SUBLANES, te + HALO_D - SUBLANES, LANE), F32)],
        compiler_params=_cp(("parallel",), vmem_mib=52),
    )(proj, proj, proj, proj, proj, dw_w, dw_b, ln_w, ln_b, w_cf)


def _conv_b_bwd1(dy_b, w_cf, c1, proj, ln_w, ln_b, dproj):
    lp = c1.shape[0]
    te = _pick(lp, (640, 320))

    def body(dyb_ref, wcf_ref, c1_ref, zb_ref, lw_ref, lb_ref, _, dc1_ref, dzb_ref, sums_ref):
        @pl.when(pl.program_id(0) == 0)
        def _():
            sums_ref[...] = jnp.zeros_like(sums_ref)

        c1, g = c1_ref[...], _dot_nt(dyb_ref[...], wcf_ref[...])
        mu = jnp.mean(c1, axis=-1, keepdims=True)
        xc = c1 - mu
        rstd = lax.rsqrt(jnp.mean(xc * xc, axis=-1, keepdims=True) + EPS)
        xh = xc * rstd
        lw = lw_ref[...]
        c2 = xh * lw + lb_ref[...]
        s2 = _sig(c2)
        zb = zb_ref[...]
        sz = _sig(zb)
        dc2 = g * (zb * sz) * _dsilu(c2, s2)
        dzb_ref[...] = (g * (c2 * s2) * _dsilu(zb, sz)).astype(BF16)
        dxh = dc2 * lw
        dc1 = rstd * (dxh - jnp.mean(dxh, axis=-1, keepdims=True) - xh * jnp.mean(dxh * xh, axis=-1, keepdims=True))
        dc1_ref[...] = dc1
        sums_ref[...] += jnp.concatenate([_colsum(dc2 * xh), _colsum(dc2), _colsum(dc1)], axis=0)

    vec = pl.BlockSpec((1, D), lambda i: (0, 0))
    return _call(
        body, name="conv_b_bwd1", grid=(lp // te,),
        in_specs=[pl.BlockSpec((te, D), lambda i: (i, 0)), pl.BlockSpec((D, D), lambda i: (0, 0)),
                  pl.BlockSpec((te, D), lambda i: (i, 0)),
                  pl.BlockSpec((te, D), lambda i: (i, CB_ZB)), vec, vec, pl.BlockSpec(memory_space=pl.ANY)],
        out_specs=(pl.BlockSpec((te, D), lambda i: (i, 0)), pl.BlockSpec((te, D), lambda i: (i, CB_ZB)),
                   pl.BlockSpec((3, D), lambda i: (0, 0))),
        out_shape=(jax.ShapeDtypeStruct((lp, D), F32), jax.ShapeDtypeStruct(dproj.shape, dproj.dtype),
                   jax.ShapeDtypeStruct((3, D), F32)),
        input_output_aliases={6: 1},
        compiler_params=_cp(("arbitrary",), vmem_mib=56),
    )(dy_b, w_cf, c1, proj, ln_w, ln_b, dproj)


def _conv_b_bwd2(dc1, proj, dw_w, dproj):
    lp = dc1.shape[0]
    te = _pick(lp, (640, 320))
    hb = te // HALO_D
    nt = lp // te
    last_hb = lp // HALO_D - 1

    def body(g_ref, gn_ref, a_ref, b_ref, ha_ref, hb_ref, w_ref, _, dab_ref, dw_ref, c0_scr, g_scr, dc0_scr,
             csh_scr, gsh_scr):
        i = pl.program_id(0)

        @pl.when(i == 0)
        def _():
            dw_ref[...] = jnp.zeros_like(dw_ref)

        a, b = a_ref[...], b_ref[...]
        sb = _sig(b)
        c0_scr[:HALO_D, :] = jnp.where(i > 0, ha_ref[...] * _sig(hb_ref[...]), 0.0)
        c0_scr[HALO_D:, :] = a * sb
        g_scr[:te, :] = g_ref[...]
        g_scr[te:, :] = jnp.where(i < nt - 1, gn_ref[...], 0.0)
        off = HALO_D - (KD - 1)
        def lane_block(cb, carry):
            cs = pl.ds(pl.multiple_of(cb * LANE, LANE), LANE)
            _fill_shifted(csh_scr, c0_scr, cs)
            _fill_shifted(gsh_scr, g_scr, cs)
            for r0 in range(0, te, CONV_RB):
                acc = None
                for j in range(KD):
                    term = w_ref[j:j + 1, cs] * _shifted(gsh_scr, g_scr, cs, KD - 1 - j, r0, CONV_RB)
                    acc = term if acc is None else acc + term
                dc0_scr[r0:r0 + CONV_RB, cs] = acc
            parts = [None] * KD
            for r0 in range(0, te, CONV_RB):
                g = g_scr[r0:r0 + CONV_RB, cs].reshape(CONV_RB // SUBLANES, SUBLANES, LANE)
                for j in range(KD):
                    x = _shifted(csh_scr, c0_scr, cs, off + j, r0, CONV_RB)
                    p = jnp.sum(g * x.reshape(CONV_RB // SUBLANES, SUBLANES, LANE), axis=0)
                    parts[j] = p if parts[j] is None else parts[j] + p
            dw_ref[:, cs] += jnp.concatenate([_colsum(p) for p in parts], axis=0)
            return carry

        lax.fori_loop(0, D // LANE, lane_block, 0)
        dc0 = dc0_scr[...]
        dab_ref[:, :D] = (dc0 * sb).astype(BF16)
        dab_ref[:, D:] = (dc0 * a * sb * (1.0 - sb)).astype(BF16)

    return _call(
        body, name="conv_b_bwd2", grid=(nt,),
        in_specs=[pl.BlockSpec((te, D), lambda i: (i, 0)),
                  pl.BlockSpec((HALO_D, D), lambda i: (jnp.minimum((i + 1) * hb, last_hb), 0)),
                  pl.BlockSpec((te, D), lambda i: (i, CB_GA_)), pl.BlockSpec((te, D), lambda i: (i, CB_GB_)),
                  pl.BlockSpec((HALO_D, D), lambda i: (jnp.maximum(i * hb - 1, 0), CB_GA_)),
                  pl.BlockSpec((HALO_D, D), lambda i: (jnp.maximum(i * hb - 1, 0), CB_GB_)),
                  pl.BlockSpec((KD, D), lambda i: (0, 0)), pl.BlockSpec(memory_space=pl.ANY)],
        out_specs=(pl.BlockSpec((te, 2 * D), lambda i: (i, CB_GA_ // 2)), pl.BlockSpec((KD, D), lambda i: (0, 0))),
        out_shape=(jax.ShapeDtypeStruct(dproj.shape, dproj.dtype), jax.ShapeDtypeStruct((KD, D), F32)),
        input_output_aliases={7: 0},
        scratch_shapes=[pltpu.VMEM((te + HALO_D, D), F32), pltpu.VMEM((te + HALO_D, D), F32), pltpu.VMEM((te, D), F32),
                        pltpu.VMEM((SUBLANES, te + HALO_D - SUBLANES, LANE), F32),
                        pltpu.VMEM((SUBLANES, te + HALO_D - SUBLANES, LANE), F32)],
        compiler_params=_cp(("arbitrary",)),
    )(dc1, dc1, proj, proj, proj, proj, dw_w, dproj)


def _merge_fwd(y_a, y_b, proj, b_cf, w_o):
    lp = y_a.shape[0]
    te = _pick(lp, (640, 320))

    def body(ya_ref, yb_ref, ga_ref, gb_ref, bias_ref, wo_ref, out_ref, z_ref):
        merged = (_sig(ga_ref[...]) * ya_ref[...] + _sig(gb_ref[...]) * (yb_ref[...] + bias_ref[...])).astype(BF16)
        out_ref[...] = merged
        z_ref[...] = _dot(merged, wo_ref[...])

    row = lambda j: pl.BlockSpec((te, D), lambda i: (i, j))
    return _call(
        body, name="merge_fwd", grid=(lp // te,),
        in_specs=[row(0), row(0), row(CB_MA), row(CB_MB), pl.BlockSpec((1, D), lambda i: (0, 0)),
                  pl.BlockSpec((D, D), lambda i: (0, 0))],
        out_specs=(row(0), row(0)),
        out_shape=(jax.ShapeDtypeStruct((lp, D), BF16), jax.ShapeDtypeStruct((lp, D), F32)),
        compiler_params=_cp(("parallel",), vmem_mib=52),
    )(y_a, y_b, proj, proj, b_cf, w_o)


def _merge_bwd(dx_out_b, w_o, y_a, y_b, proj, b_cf):
    lp = y_a.shape[0]
    te = _pick(lp, (640, 320))

    def body(dx_ref, wo_ref, ya_ref, yb_ref, ga_ref, gb_ref, bias_ref, dya_ref, dyb_ref, dg_ref, db_ref):
        @pl.when(pl.program_id(0) == 0)
        def _():
            db_ref[...] = jnp.zeros_like(db_ref)

        dm = _dot_nt(dx_ref[...], wo_ref[...])
        sa, sb = _sig(ga_ref[...]), _sig(gb_ref[...])
        dyb = sb * dm
        dya_ref[...] = (sa * dm).astype(BF16)
        dyb_ref[...] = dyb.astype(BF16)
        dg_ref[:, :D] = (dm * ya_ref[...] * sa * (1.0 - sa)).astype(BF16)
        dg_ref[:, D:] = (dm * (yb_ref[...] + bias_ref[...]) * sb * (1.0 - sb)).astype(BF16)
        db_ref[...] += _colsum(dyb)

    row = lambda j: pl.BlockSpec((te, D), lambda i: (i, j))
    act = jax.ShapeDtypeStruct((lp, D), BF16)
    return _call(
        body, name="merge_bwd", grid=(lp // te,),
        in_specs=[row(0), pl.BlockSpec((D, D), lambda i: (0, 0)), row(0), row(0), row(CB_MA), row(CB_MB),
                  pl.BlockSpec((1, D), lambda i: (0, 0))],
        out_specs=(row(0), row(0), pl.BlockSpec((te, 2 * D), lambda i: (i, CB_MA // 2)),
                   pl.BlockSpec((1, D), lambda i: (0, 0))),
        out_shape=(act, act, jax.ShapeDtypeStruct((lp, NCB * D), BF16), jax.ShapeDtypeStruct((1, D), F32)),
        compiler_params=_cp(("arbitrary",), vmem_mib=56),
    )(dx_out_b, w_o, y_a, y_b, proj, proj, b_cf)


def _final_fwd_bwd(x_ext, z, target, final_w):
    lp = x_ext.shape[0]
    te = _pick(lp, (640,))
    nsub = te // LANE

    def body(x_ref, z_ref, *rest):
        t_refs, (w_ref, dx_ref, dxb_ref, loss_ref, dw_ref) = rest[:nsub], rest[nsub:]
        i = pl.program_id(0)

        @pl.when(i == 0)
        def _():
            loss_ref[...] = jnp.zeros_like(loss_ref)
            dw_ref[...] = jnp.zeros_like(dw_ref)

        w = w_ref[...]
        for k in range(nsub):
            rs = slice(k * LANE, (k + 1) * LANE)
            xo = x_ref[rs, :] + z_ref[rs, :]
            r = lax.rsqrt(jnp.mean(xo * xo, axis=-1, keepdims=True) + EPS)
            xhat = xo * r
            err = xhat * w - t_refs[k][...]
            if k == 0:
                err = jnp.where(i > 0, err, 0.0)
            loss_ref[...] += 0.5 * jnp.sum(jnp.mean(err * err, axis=-1, keepdims=True), keepdims=True)
            dy = err * (1.0 / D)
            dw_ref[...] += _colsum(dy * xhat)
            dxn = dy * w
            dx = r * (dxn - xhat * jnp.mean(dxn * xhat, axis=-1, keepdims=True))
            dx_ref[rs, :] = dx
            dxb_ref[rs, :] = dx.astype(BF16)

    piece = lambda k: pl.BlockSpec((LANE, D), lambda i: (jnp.maximum(i * nsub + k - 1, 0), 0))
    row = pl.BlockSpec((te, D), lambda i: (i, 0))
    return _call(
        body, name="final_fwd_bwd", grid=(lp // te,),
        in_specs=[row, row] + [piece(k) for k in range(nsub)] + [pl.BlockSpec((1, D), lambda i: (0, 0))],
        out_specs=(row, row, pl.BlockSpec((1, 1), lambda i: (0, 0)), pl.BlockSpec((1, D), lambda i: (0, 0))),
        out_shape=(jax.ShapeDtypeStruct((lp, D), F32), jax.ShapeDtypeStruct((lp, D), BF16),
                   jax.ShapeDtypeStruct((1, 1), F32), jax.ShapeDtypeStruct((1, D), F32)),
        compiler_params=_cp(("arbitrary",)),
    )(x_ext, z, *([target] * nsub), final_w)


def _prenorm_bwd(dh, x_ext, dx_out, norm_w, seq):
    lp = x_ext.shape[0]
    te = _pick(lp, (640,))
    nt = lp // te
    head = lp - seq

    def body(dh_ref, x_ref, dxo_ref, w_ref, gx_ref, head_ref, dw_ref, stage, sems):
        i = pl.program_id(0)
        slot = i % 2

        def first_copy():
            return pltpu.make_async_copy(stage.at[0, pl.ds(head, te - head)], gx_ref.at[pl.ds(0, te - head)], sems.at[0])

        def tile_copy(step, s):
            return pltpu.make_async_copy(stage.at[s], gx_ref.at[pl.ds(pl.multiple_of(step * te - head, LANE), te)],
                                         sems.at[s])

        @pl.when(i == 0)
        def _():
            dw_ref[...] = jnp.zeros_like(dw_ref)

        @pl.when(i == 2)
        def _():
            first_copy().wait()

        @pl.when(i > 2)
        def _():
            tile_copy(i - 2, slot).wait()

        x, dh = x_ref[...], dh_ref[...]
        r = lax.rsqrt(jnp.mean(x * x, axis=-1, keepdims=True) + EPS)
        xhat = x * r
        dxn = dh * w_ref[...]
        stage[slot] = dxo_ref[...] + r * (dxn - xhat * jnp.mean(dxn * xhat, axis=-1, keepdims=True))
        dw_ref[...] += _colsum(dh * xhat)

        @pl.when(i == 0)
        def _():
            head_ref[...] = stage[0, :head, :]
            first_copy().start()

        @pl.when(i > 0)
        def _():
            tile_copy(i, slot).start()

        @pl.when(i == nt - 1)
        def _():
            if nt >= 2:
                (first_copy() if nt == 2 else tile_copy(nt - 2, (nt - 2) % 2)).wait()
            (first_copy() if nt == 1 else tile_copy(nt - 1, (nt - 1) % 2)).wait()

    row = pl.BlockSpec((te, D), lambda i: (i, 0))
    return _call(
        body, name="prenorm_bwd", grid=(nt,),
        in_specs=[row, row, row, pl.BlockSpec((1, D), lambda i: (0, 0))],
        out_specs=(pl.BlockSpec(memory_space=pl.ANY), pl.BlockSpec((head, D), lambda i: (0, 0)),
                   pl.BlockSpec((1, D), lambda i: (0, 0))),
        out_shape=(jax.ShapeDtypeStruct((seq, D), F32), jax.ShapeDtypeStruct((head, D), F32),
                   jax.ShapeDtypeStruct((1, D), F32)),
        scratch_shapes=[pltpu.VMEM((2, te, D), F32), pltpu.SemaphoreType.DMA((2,))],
        compiler_params=_cp(("arbitrary",)),
    )(dh, x_ext, dx_out, norm_w)


def _adam_reduce(parts, w, m, v, name):
    r, n = w.shape
    tr = _pick(r, (128,)) if r % 128 == 0 else r

    def body(p_ref, w_ref, m_ref, v_ref, g_ref, d_ref, m2_ref, v2_ref):
        g = p_ref[0]
        for s in range(1, NDEV):
            g = g + p_ref[s]
        _adam_write(g, w_ref, m_ref, v_ref, g_ref, d_ref, m2_ref, v2_ref)

    blk = pl.BlockSpec((tr, n), lambda i: (i, 0))
    out = jax.ShapeDtypeStruct((r, n), F32)
    return _call(
        body, name=name, grid=(r // tr,),
        in_specs=[pl.BlockSpec((NDEV, tr, n), lambda i: (0, i, 0)), blk, blk, blk],
        out_specs=(blk, blk, blk, blk), out_shape=(out, out, out, out),
        compiler_params=_cp(("parallel",)),
    )(parts, w, m, v)


def _adam_write(g, w_ref, m_ref, v_ref, g_ref, d_ref, m2_ref, v2_ref):
    c1 = 1.0 - ADAM_B1 ** ADAM_STEP
    c2 = 1.0 - ADAM_B2 ** ADAM_STEP
    m2 = ADAM_B1 * m_ref[...] + (1.0 - ADAM_B1) * g
    v2 = ADAM_B2 * v_ref[...] + (1.0 - ADAM_B2) * (g * g)
    g_ref[...] = g
    m2_ref[...] = m2
    v2_ref[...] = v2
    d_ref[...] = -ADAM_LR * ((m2 / c1) / (jnp.sqrt(v2 / c2) + ADAM_EPS) + ADAM_WD * w_ref[...])


def _adam_chips(own, recv, w, m, v, name):
    r, n = w.shape
    tr, tc = _shard_tile(r, n)

    def body(own_ref, p_ref, w_ref, m_ref, v_ref, g_ref, d_ref, m2_ref, v2_ref):
        my_chip = 2 * lax.axis_index("x") + lax.axis_index("y")
        g = None
        for j in range(NCHIP):
            part = jnp.where(my_chip == j, own_ref[...], p_ref[j].astype(F32))
            g = part if g is None else g + part
        _adam_write(g, w_ref, m_ref, v_ref, g_ref, d_ref, m2_ref, v2_ref)

    blk = pl.BlockSpec((tr, tc), lambda i, k: (i, k))
    out = jax.ShapeDtypeStruct((r, n), F32)
    return _call(
        body, name=name, grid=(r // tr, n // tc),
        in_specs=[blk, pl.BlockSpec((NCHIP, tr, tc), lambda i, k: (0, i, k)), blk, blk, blk],
        out_specs=(blk, blk, blk, blk), out_shape=(out, out, out, out),
        compiler_params=_cp(("parallel", "parallel")),
    )(own, recv, w, m, v)


SMALL = ("norm_w", "a_log", "dt_bias", "dn_norm_w", "dw_b", "ln_w", "ln_b", "b_cf_out", "final_norm_w")


def kernel(x, meta, norm_w, w_in, conv_qkv_w, a_log, dt_bias, dn_norm_w, w_dn_out, dw_w, dw_b, ln_w, ln_b, w_cf_out, b_cf_out, w_o, final_norm_w, loss_target, m_meta, m_norm_w, m_w_in, m_conv_qkv_w, m_a_log, m_dt_bias, m_dn_norm_w, m_w_dn_out, m_dw_w, m_dw_b, m_ln_w, m_ln_b, m_w_cf_out, m_b_cf_out, m_w_o, m_final_norm_w, v_meta, v_norm_w, v_w_in, v_conv_qkv_w, v_a_log, v_dt_bias, v_dn_norm_w, v_w_dn_out, v_dw_w, v_dw_b, v_ln_w, v_ln_b, v_w_cf_out, v_b_cf_out, v_w_o, v_final_norm_w):
    seq = x.shape[1]
    pad = (-(seq + NMETA)) % LANE
    in_w = w_in.shape[2] * NDEV
    n_qkvz = 4 * D
    n_ba = 2 * H

    me = 4 * lax.axis_index("x") + 2 * lax.axis_index("y") + lax.axis_index("c")
    late = [w_dn_out[0].astype(BF16), w_cf_out[0].astype(BF16), w_o[0].astype(BF16), conv_qkv_w[0], dw_w[0]]
    late_lands = [lax.dynamic_update_index_in_dim(jnp.zeros((NDEV,) + b.shape, b.dtype), b, me, 0) for b in late]
    w_in_g, meta_g = _gather_two_level([w_in[0].astype(BF16).T, meta], "gather_weights")
    late_send, late_recv, late_thru, late_land_thru, _ = _split_start(
        _all_copies, NDEV - 1, "gather_late_start", late, late_lands, meta_g)
    w_full_t = w_in_g.reshape(in_w, D)
    c_glu = n_qkvz + n_ba
    c_zb, c_mg = c_glu + 2 * D, c_glu + 3 * D
    w_main_t = jnp.concatenate([w_full_t[:n_qkvz], w_full_t[c_glu:c_zb], w_full_t[c_mg:], w_full_t[c_zb:c_mg]],
                               axis=0)
    w_ba_t = jnp.pad(w_full_t[n_qkvz:n_qkvz + n_ba], ((0, LANE - n_ba), (0, 0)))
    meta_full = jnp.transpose(meta_g, (1, 0, 2)).reshape(NMETA, D)
    ab = jnp.pad(jnp.concatenate([a_log, dt_bias], axis=0), ((0, 0), (H, LANE - 2 * H)))

    x_ext = jnp.concatenate([jnp.zeros((pad, D), F32), meta_full, x[0]], axis=0)

    proj, ba, h = _proj_fwd(x_ext, norm_w, w_main_t, w_ba_t)
    _, (w_dn_g, w_cf_g, w_o_g, cqw_g, dww_g) = _split_wait(
        _all_copies, "gather_late_wait", late_send, late_recv, late_thru, late_land_thru, ba)
    w_dn, w_cf, w_oo = (t.reshape(D, D) for t in (w_dn_g, w_cf_g, w_o_g))
    cqw = jnp.transpose(cqw_g, (1, 0, 2)).reshape(KQ, 3 * D)
    dww = jnp.transpose(dww_g, (1, 0, 2)).reshape(KD, D)
    qkv, bg = _qkv_conv_fwd(proj, ba, cqw, ab, pad)
    o, sall, tall = _delta_fwd(qkv, bg)
    o_n, y_a = _o_post_fwd(o, proj, dn_norm_w, w_dn)
    c1, c3, y_b = _conv_b_fwd(proj, dww, dw_b, ln_w, ln_b, w_cf)
    merged, z = _merge_fwd(y_a, y_b, proj, b_cf_out, w_oo)
    dx_out, dx_out_b, loss_part, g_final_w = _final_fwd_bwd(x_ext, z, loss_target[0], final_norm_w.reshape(1, D))

    g_w_o = _mm_tn(merged, dx_out_b, "g_w_o_mm")
    dy_a, dy_b, dproj, g_b_cf = _merge_bwd(dx_out_b, w_oo, y_a, y_b, proj, b_cf_out)
    g_w_cf = _mm_tn(c3, dy_b, "g_w_cf_mm")
    g_w_dn = _mm_tn(o_n, dy_a, "g_w_dn_mm")
    dc1, dproj, sums_b = _conv_b_bwd1(dy_b, w_cf, c1, proj, ln_w, ln_b, dproj)
    dproj, g_dw_w = _conv_b_bwd2(dc1, proj, dww, dproj)
    do, dproj, g_dn_w = _o_post_bwd(dy_a, w_dn, o, proj, dn_norm_w, dproj)
    dqkv, dbg = _delta_bwd(qkv, bg, sall, tall, do)
    dproj, g_cqw = _qkv_conv_bwd(proj, dqkv, cqw, dproj)
    dba, dab = _ba_bwd(dbg, ba, ab, pad)
    g_w_main_t = _mm_tn(dproj, h, "g_w_main_mm")
    g_w_ba_t = _mm_tn(dba, h, "g_w_ba_mm")

    g_w_full_t = jnp.concatenate([g_w_main_t[:n_qkvz], g_w_ba_t[:n_ba], g_w_main_t[CB_GA_ * D:CB_MA * D],
                                  g_w_main_t[CB_ZB * D:], g_w_main_t[CB_MA * D:CB_ZB * D]], axis=0)
    big = [t.reshape(NCHIP, 2, t.shape[0] // NDEV, D) for t in (g_w_full_t, g_w_dn, g_w_cf, g_w_o)]
    sw_send, sw_recv, big_thru, sw_land, sw_token = _split_start(
        _sibling_copies, NCHIP, "swap_sibling_start", big,
        [lax.empty((NCHIP,) + t.shape[2:], t.dtype) for t in big], g_w_ba_t)
    dh = _dh_mm(dproj, dba, w_main_t, w_ba_t + sw_token[0, 0].astype(BF16), 0)
    big_back, from_sibling = _split_wait(_sibling_copies, "swap_sibling_wait", sw_send, sw_recv, big_thru, sw_land, dh)
    pairs = [_pair_add(a, g, f"pair_add_{i}") for i, (a, g) in enumerate(zip(big_back, from_sibling))]
    send_sems, recv_sems, pair_thru, land_thru, token = _split_start(
        _chip_copies, NCHIP - 1, "scatter_chips_start",
        [p for p, _ in pairs], [jnp.zeros(p.shape, p.dtype) for p, _ in pairs], g_w_ba_t)
    dh = _dh_mm(dproj, dba, w_main_t, w_ba_t + token[0, 0].astype(BF16), 1, dh)
    grad_x, dhead, g_norm_w = _prenorm_bwd(dh, x_ext, dx_out, norm_w, seq)
    _, from_chips = _split_wait(_chip_copies, "scatter_chips_wait", send_sems, recv_sems, pair_thru, land_thru,
                                g_norm_w)

    split_cols = lambda t: jnp.transpose(t.reshape(t.shape[0], NDEV, t.shape[1] // NDEV), (1, 0, 2))
    small = {"norm_w": g_norm_w, "a_log": dab[0:1, H:2 * H], "dt_bias": dab[1:2, H:2 * H], "dn_norm_w": g_dn_w,
             "dw_b": sums_b[2:3], "ln_w": sums_b[0:1], "ln_b": sums_b[1:2], "b_cf_out": g_b_cf,
             "final_norm_w": g_final_w}
    small_vec = jnp.concatenate([small[k] for k in SMALL], axis=1)
    ns = small_vec.shape[1]
    ns_pad = (-ns) % LANE
    small_vec = jnp.pad(small_vec, ((0, 0), (0, ns_pad)))
    p_meta, p_cqw, p_dww, p_small = _exchange(
        [split_cols(dhead[pad:pad + NMETA]), split_cols(g_cqw), split_cols(g_dw_w), small_vec],
        [True] * 3 + [False], "exchange_small")

    res = {}
    res["w_in"] = tuple(t.T for t in _adam_chips(pairs[0][1], from_chips[0], w_in[0].T, m_w_in[0].T, v_w_in[0].T,
                                                   "adam_w_in"))
    res["w_dn_out"] = _adam_chips(pairs[1][1], from_chips[1], w_dn_out[0], m_w_dn_out[0], v_w_dn_out[0], "adam_w_dn")
    res["w_cf_out"] = _adam_chips(pairs[2][1], from_chips[2], w_cf_out[0], m_w_cf_out[0], v_w_cf_out[0], "adam_w_cf")
    res["w_o"] = _adam_chips(pairs[3][1], from_chips[3], w_o[0], m_w_o[0], v_w_o[0], "adam_w_o")
    res["meta"] = _adam_reduce(p_meta, meta, m_meta, v_meta, "adam_meta")
    res["conv_qkv_w"] = _adam_reduce(p_cqw, conv_qkv_w[0], m_conv_qkv_w[0], v_conv_qkv_w[0], "adam_conv_qkv_w")
    res["dw_w"] = _adam_reduce(p_dww, dw_w[0], m_dw_w[0], v_dw_w[0], "adam_dw_w")
    loc = dict(norm_w=(norm_w, m_norm_w, v_norm_w), a_log=(a_log, m_a_log, v_a_log), dt_bias=(dt_bias, m_dt_bias, v_dt_bias),
               dn_norm_w=(dn_norm_w, m_dn_norm_w, v_dn_norm_w), dw_b=(dw_b, m_dw_b, v_dw_b), ln_w=(ln_w, m_ln_w, v_ln_w),
               ln_b=(ln_b, m_ln_b, v_ln_b), b_cf_out=(b_cf_out, m_b_cf_out, v_b_cf_out),
               final_norm_w=(final_norm_w, m_final_norm_w, v_final_norm_w))
    cat = lambda j: jnp.pad(jnp.concatenate([loc[k][j].reshape(1, -1) for k in SMALL], axis=1), ((0, 0), (0, ns_pad)))
    small_res = _adam_reduce(p_small, cat(0), cat(1), cat(2), "adam_small")
    off = 0
    for k in SMALL:
        wshape = loc[k][0].shape
        nk = loc[k][0].size
        res[k] = tuple(t[:, off:off + nk].reshape(wshape) for t in small_res)
        off += nk
    shaped = dict(w_in=w_in.shape, w_dn_out=w_dn_out.shape, w_cf_out=w_cf_out.shape, w_o=w_o.shape, meta=meta.shape,
                  conv_qkv_w=conv_qkv_w.shape, dw_w=dw_w.shape)
    for k, shp in shaped.items():
        res[k] = tuple(t.reshape(shp) for t in res[k])

    loss = lax.psum(loss_part[0, 0], ("x", "y", "c"))
    order = ("meta", "norm_w", "w_in", "conv_qkv_w", "a_log", "dt_bias", "dn_norm_w", "w_dn_out", "dw_w", "dw_b", "ln_w",
             "ln_b", "w_cf_out", "b_cf_out", "w_o", "final_norm_w")
    outs = [loss, grad_x[None]]
    for j in range(4):
        outs += [res[k][j] for k in order]
    return tuple(outs)
```
